```python
import math
import jax, jax.numpy as jnp
from jax import lax
import numpy as np

D_MODEL = 1024
BATCH = 8
SEQ = 2048
DEPTH = 2

N_EVEN = (DEPTH + 1) // 2
N_ODD = DEPTH // 2
NORM_EPS = 1e-6

MIX_WIDTH = D_MODEL
A_WIDTH = D_MODEL // 2
A_HEAD_DIM = 128
A_HEADS = A_WIDTH // A_HEAD_DIM
A_CHUNK = 32
B_WIDTH = MIX_WIDTH - A_WIDTH
B_HEAD_DIM = 64
B_HEADS = B_WIDTH // B_HEAD_DIM
B_DECAY_LORA = 64
B_AAA_LORA = 64
B_GATE_LORA = 128
B_LN_EPS = 1e-5 * B_HEAD_DIM
A_SIZES = [A_WIDTH] * 5
B_SIZES = [B_WIDTH] * 3 + [B_DECAY_LORA, B_DECAY_LORA, B_AAA_LORA, B_GATE_LORA]
A_COLS = sum(A_SIZES)
B_COLS = sum(B_SIZES)
AB_COLS = A_COLS + B_COLS

C_HEADS = 8
C_HEAD_DIM = 64
C_QK_WIDTH = C_HEADS * 2 * C_HEAD_DIM
C_V_WIDTH = C_HEADS * 2 * C_HEAD_DIM
C_QKV_COLS = 2 * C_QK_WIDTH + C_V_WIDTH
Q_BLOCK = 128
REL_BUCKETS = 32
REL_MAX_DISTANCE = 128

N_EXPERTS = 32
TOP_K = 4
D_EXPERT = D_MODEL
SWIGLU_LIMIT = 7.0
SWIGLU_ALPHA = 1.702

kernel_name = 'hybrid_hgrn2_rwkv7_diffattn_moe_encoder'


def _split(t, sizes):
    return jnp.split(t, [int(v) for v in np.cumsum(sizes)[:-1]], axis=-1)


def _rmsnorm(x, g, eps=NORM_EPS):
    xf = x.astype(jnp.float32)
    y = xf * lax.rsqrt(jnp.mean(xf * xf, axis=-1, keepdims=True) + eps)
    return (y * g.astype(jnp.float32)).astype(x.dtype)


def _modulate(x, g, shift, scale):
    return _rmsnorm(x, g) * (1.0 + scale) + shift


def _centred_shift(t):
    prev = jnp.pad(t[:, :-1], ((0, 0), (1, 0), (0, 0)))
    nxt = jnp.pad(t[:, 1:], ((0, 0), (0, 1), (0, 0)))
    return 0.5 * (prev + nxt)


def _gla_chunked(q, k, v, log_f):
    B_, S_, H_, dk = q.shape
    dv = v.shape[-1]
    n = S_ // A_CHUNK
    rs = lambda t: t.reshape(B_, n, A_CHUNK, H_, t.shape[-1])
    q, k, v, log_f = rs(q), rs(k), rs(v), rs(log_f)
    b = jnp.cumsum(log_f, axis=2)
    b_last = b[:, :, -1:]
    q_in = q * jnp.exp(b)
    k_in = k * jnp.exp(-b)
    k_st = k * jnp.exp(b_last - b)
    scores = jnp.einsum('bnthk,bnshk->bnhts', q_in, k_in)
    mask = jnp.tril(jnp.ones((A_CHUNK, A_CHUNK), dtype=bool))
    o_intra = jnp.einsum('bnhts,bnshv->bnthv', jnp.where(mask, scores, 0.0), v)
    d_state = jnp.einsum('bnshk,bnshv->bnhkv', k_st, v)
    chunk_decay = jnp.exp(b_last[:, :, 0])

    def step(state, inp):
        ds, dec = inp
        return dec[..., None] * state + ds, state

    s0 = jnp.zeros((B_, H_, dk, dv), jnp.float32)
    _, s_in = lax.scan(step, s0, (jnp.moveaxis(d_state, 1, 0), jnp.moveaxis(chunk_decay, 1, 0)))
    s_in = jnp.moveaxis(s_in, 0, 1)
    o_inter = jnp.einsum('bnthk,bnhkv->bnthv', q_in, s_in)
    return (o_intra + o_inter).reshape(B_, S_, H_, dv)


def _hgrn2_mix(p_a, lb, norm_g):
    B_, S_, _ = p_a.shape
    q, f_fwd, f_bwd, i, g = _split(p_a.astype(jnp.float32), A_SIZES)
    q = jax.nn.silu(q)
    heads = lambda t: t.reshape(B_, S_, A_HEADS, A_HEAD_DIM)

    def direction(f_logit, lb_dir, rev):
        fg = lb_dir + (1.0 - lb_dir) * jax.nn.sigmoid(f_logit)
        args = [heads(t) for t in (q, 1.0 - fg, i, jnp.log(fg))]
        if rev:
            args = [jnp.flip(t, 1) for t in args]
        o = _gla_chunked(*args)
        return jnp.flip(o, 1) if rev else o

    o = direction(f_fwd, lb[0], False) + direction(f_bwd, lb[1], True)
    o = _rmsnorm(o, norm_g) * jax.nn.silu(heads(g))
    return o.reshape(B_, S_, A_WIDTH)


def _rwkv7_mix(p_b, w0, w2, a0, a2, g2, k_k, k_a, r_k, ln_g, ln_b):
    B_, S_, _ = p_b.shape
    r, k, v, wlo_f, wlo_b, alo, glo = _split(p_b.astype(jnp.float32), B_SIZES)
    wlo = jnp.stack([wlo_f, wlo_b], 0)
    w = -jax.nn.softplus(-(w0[:, None, None, :] + jnp.einsum('nbsl,nlc->nbsc', jnp.tanh(wlo), w2))) - 0.5
    decay = jnp.exp(-jnp.exp(w))
    a = jax.nn.sigmoid(a0 + alo @ a2)
    g = jax.nn.sigmoid(glo) @ g2
    heads = lambda t: t.reshape(B_, S_, B_HEADS, B_HEAD_DIM)
    kk = heads(k * k_k)
    kk = kk / jnp.maximum(jnp.sqrt(jnp.sum(kk * kk, axis=-1, keepdims=True)), 1e-12)
    k = k * (1.0 + (a - 1.0) * k_a)
    r, k, v, a = heads(r), heads(k), heads(v), heads(a)
    decay = decay.reshape(2, B_, S_, B_HEADS, B_HEAD_DIM)

    both = lambda t: jnp.stack([t, jnp.flip(t, 1)], 0)
    time_major = lambda t: jnp.moveaxis(t, 2, 0)
    w_dir = jnp.stack([decay[0], jnp.flip(decay[1], 1)], 0)
    xs = (time_major(both(r)), time_major(w_dir), time_major(both(k)),
          time_major(both(v)), time_major(both(kk)), time_major(both(kk * a)))

    def step(state, inp):
        r_t, w_t, k_t, v_t, kk_t, b_t = inp
        sa = jnp.einsum('...vk,...k->...v', state, -kk_t)
        state = (state * w_t[..., None, :] + sa[..., :, None] * b_t[..., None, :]
                 + v_t[..., :, None] * k_t[..., None, :])
        return state, jnp.einsum('...vk,...k->...v', state, r_t)

    s0 = jnp.zeros((2, B_, B_HEADS, B_HEAD_DIM, B_HEAD_DIM), jnp.float32)
    _, y = lax.scan(step, s0, xs)
    y = jnp.moveaxis(y, 0, 2)
    y = y[0] + jnp.flip(y[1], 1)
    mean = jnp.mean(y, axis=-1, keepdims=True)
    var = jnp.mean(jnp.square(y - mean), axis=-1, keepdims=True)
    yn = (y - mean) * lax.rsqrt(var + B_LN_EPS)
    yn = yn * ln_g.reshape(B_HEADS, B_HEAD_DIM) + ln_b.reshape(B_HEADS, B_HEAD_DIM)
    bonus = jnp.sum(r * k * r_k.reshape(B_HEADS, B_HEAD_DIM), axis=-1, keepdims=True) * v
    return (yn + bonus).reshape(B_, S_, B_WIDTH) * g


def _ab_mixer(h, w_in, w_out, lb, hgrn_norm_g, mu, w0, w2, a0, a2, g2, k_k, k_a, r_k, ln_g, ln_b):
    p = h @ w_in
    p_a, p_b = p[..., :A_COLS], p[..., A_COLS:]
    p_b = p_b + mu * (_centred_shift(p_b) - p_b)
    y = jnp.concatenate([_hgrn2_mix(p_a, lb, hgrn_norm_g),
                         _rwkv7_mix(p_b, w0, w2, a0, a2, g2, k_k, k_a, r_k, ln_g, ln_b)], axis=-1)
    return y @ w_out


def _t5_bucket(rel):
    nb = REL_BUCKETS // 2
    max_exact = nb // 2
    bucket = jnp.where(rel > 0, nb, 0)
    n = jnp.abs(rel)
    nf = jnp.maximum(n, 1).astype(jnp.float32)
    large = max_exact + (jnp.log(nf / max_exact) / math.log(REL_MAX_DISTANCE / max_exact)
                         * (nb - max_exact)).astype(jnp.int32)
    large = jnp.minimum(large, nb - 1)
    return bucket + jnp.where(n < max_exact, n, large)


def _diff_attention(h, w_in, w_out, lam, subln_g, rel_table, layer_idx):
    B_, S_, _ = h.shape
    qkv = h @ w_in
    q, k, v = _split(qkv, [C_QK_WIDTH, C_QK_WIDTH, C_V_WIDTH])
    q = q.reshape(B_, S_, C_HEADS, 2, C_HEAD_DIM)
    k = k.reshape(B_, S_, C_HEADS, 2, C_HEAD_DIM)
    v = v.reshape(B_, S_, C_HEADS, 2 * C_HEAD_DIM)
    lam = lam.astype(jnp.float32)
    lam_init = 0.8 - 0.6 * math.exp(-0.3 * layer_idx)
    lam_full = jnp.exp(jnp.sum(lam[0] * lam[1])) - jnp.exp(jnp.sum(lam[2] * lam[3])) + lam_init
    scale = C_HEAD_DIM ** -0.5
    nb = S_ // Q_BLOCK
    q_blocks = jnp.moveaxis(q.reshape(B_, nb, Q_BLOCK, C_HEADS, 2, C_HEAD_DIM), 1, 0)
    pos = jnp.arange(S_, dtype=jnp.int32)
    q_pos = pos.reshape(nb, Q_BLOCK)

    def block(args):
        qb, qp = args
        bucket = _t5_bucket(pos[None, :] - qp[:, None])
        bias = jnp.transpose(rel_table[bucket], (2, 0, 1)).astype(jnp.float32)
        logits = jnp.einsum('bqhmd,bkhmd->bmhqk', qb, k).astype(jnp.float32) * scale + bias
        p = jax.nn.softmax(logits, axis=-1)
        attn = p[:, 0] - lam_full * p[:, 1]
        return jnp.einsum('bhqk,bkhv->bqhv', attn.astype(v.dtype), v)

    o = lax.map(block, (q_blocks, q_pos))
    o = jnp.moveaxis(o, 0, 1).reshape(B_, S_, C_HEADS, 2 * C_HEAD_DIM)
    o = _rmsnorm(o, subln_g) * (1.0 - lam_init)
    return o.reshape(B_, S_, C_V_WIDTH) @ w_out


def _moe_ffn(h, router_w, router_b, w1, b1, w2, b2):
    B_, S_, D_ = h.shape
    t = h.reshape(-1, D_)
    logits = (t @ router_w + router_b).astype(jnp.float32)
    top_val, top_idx = lax.top_k(logits, TOP_K)
    top_w = jax.nn.softmax(top_val, axis=-1)
    gates = jnp.sum(jax.nn.one_hot(top_idx, N_EXPERTS, dtype=jnp.float32) * top_w[..., None], axis=1)

    def expert_step(acc, params):
        w1_e, b1_e, w2_e, b2_e, gate_e = params
        hh = t @ w1_e + b1_e
        x_glu = jnp.minimum(hh[:, ::2], SWIGLU_LIMIT)
        x_lin = jnp.clip(hh[:, 1::2], -SWIGLU_LIMIT, SWIGLU_LIMIT)
        act = x_glu * jax.nn.sigmoid(SWIGLU_ALPHA * x_glu) * (x_lin + 1.0)
        out = act @ w2_e + b2_e
        return acc + gate_e[:, None] * out.astype(jnp.float32), None

    acc0 = jnp.zeros((t.shape[0], D_), jnp.float32)
    acc, _ = lax.scan(expert_step, acc0, (w1, b1, w2, b2, gates.T))
    return acc.reshape(B_, S_, D_)


def setup_inputs(seed: int = 0) -> dict:
    key = jax.random.key(seed)
    keys = iter(jax.random.split(key, 48))

    def nrm(shape, scale):
        return jax.random.normal(next(keys), shape, jnp.float32) * scale

    def unif(shape):
        return jax.random.uniform(next(keys), shape, jnp.float32)

    L, NE, NO, D = DEPTH, N_EVEN, N_ODD, D_MODEL
    return {
        'x': nrm((BATCH, SEQ, D), 1.0),
        'c': nrm((BATCH, D), 1.0),
        'ada_w': nrm((L, D, 6 * D), 0.5 * D ** -0.5),
        'ada_b': nrm((L, 6 * D), 0.02),
        'norm_mix_g': 1.0 + nrm((L, D), 0.02),
        'norm_ffn_g': 1.0 + nrm((L, D), 0.02),
        'router_w': nrm((L, D, N_EXPERTS), D ** -0.5),
        'router_b': nrm((L, N_EXPERTS), 0.01),
        'moe_w1': nrm((L, N_EXPERTS, D, 2 * D_EXPERT), D ** -0.5),
        'moe_b1': nrm((L, N_EXPERTS, 2 * D_EXPERT), 0.02),
        'moe_w2': nrm((L, N_EXPERTS, D_EXPERT, D), D_EXPERT ** -0.5),
        'moe_b2': nrm((L, N_EXPERTS, D), 0.02),
        'ab_w_in': nrm((NE, D, AB_COLS), D ** -0.5),
        'ab_w_out': nrm((NE, MIX_WIDTH, D), MIX_WIDTH ** -0.5),
        'hgrn_lb': nrm((2, NE + 1, A_WIDTH), 0.5),
        'hgrn_norm_g': 1.0 + nrm((NE, A_HEAD_DIM), 0.02),
        'rwkv_mu': unif((NE, B_COLS)),
        'rwkv_w0': -2.0 + nrm((NE, 2, B_WIDTH), 0.5),
        'rwkv_w2': nrm((NE, 2, B_DECAY_LORA, B_WIDTH), 0.1),
        'rwkv_a0': nrm((NE, B_WIDTH), 0.5),
        'rwkv_a2': nrm((NE, B_AAA_LORA, B_WIDTH), B_AAA_LORA ** -0.5),
        'rwkv_g2': nrm((NE, B_GATE_LORA, B_WIDTH), B_GATE_LORA ** -0.5),
        'rwkv_k_k': 0.85 + nrm((NE, B_WIDTH), 0.05),
        'rwkv_k_a': 1.0 + nrm((NE, B_WIDTH), 0.05),
        'rwkv_r_k': nrm((NE, B_WIDTH), 0.1),
        'rwkv_ln_g': 1.0 + nrm((NE, B_WIDTH), 0.02),
        'rwkv_ln_b': nrm((NE, B_WIDTH), 0.02),
        'attn_w_in': nrm((NO, D, C_QKV_COLS), D ** -0.5),
        'attn_w_out': nrm((NO, C_V_WIDTH, D), C_V_WIDTH ** -0.5),
        'attn_lambda': nrm((NO, 4, C_HEAD_DIM), 0.1),
        'attn_subln_g': 1.0 + nrm((NO, 2 * C_HEAD_DIM), 0.02),
        'rel_bias_table': nrm((REL_BUCKETS, C_HEADS), 0.5),
        'final_norm_g': 1.0 + nrm((D,), 0.02),
    }


def reference(x, c, ada_w, ada_b, norm_mix_g, norm_ffn_g, router_w, router_b, moe_w1, moe_b1,
              moe_w2, moe_b2, ab_w_in, ab_w_out, hgrn_lb, hgrn_norm_g, rwkv_mu, rwkv_w0, rwkv_w2,
              rwkv_a0, rwkv_a2, rwkv_g2, rwkv_k_k, rwkv_k_a, rwkv_r_k, rwkv_ln_g, rwkv_ln_b,
              attn_w_in, attn_w_out, attn_lambda, attn_subln_g, rel_bias_table, final_norm_g):
    cond = jax.nn.silu(c.astype(jnp.float32))
    lb_all = jnp.cumsum(jax.nn.softmax(hgrn_lb.astype(jnp.float32), axis=1), axis=1)
    for layer in range(DEPTH):
        mod = cond @ ada_w[layer].astype(jnp.float32) + ada_b[layer]
        sh_m, sc_m, g_m, sh_f, sc_f, g_f = [m[:, None, :] for m in jnp.split(mod, 6, axis=-1)]
        h = _modulate(x, norm_mix_g[layer], sh_m, sc_m)
        j = layer // 2
        if layer % 2 == 0:
            y = _ab_mixer(h, ab_w_in[j], ab_w_out[j], lb_all[:, j], hgrn_norm_g[j], rwkv_mu[j],
                          rwkv_w0[j], rwkv_w2[j], rwkv_a0[j], rwkv_a2[j], rwkv_g2[j], rwkv_k_k[j],
                          rwkv_k_a[j], rwkv_r_k[j], rwkv_ln_g[j], rwkv_ln_b[j])
        else:
            y = _diff_attention(h, attn_w_in[j], attn_w_out[j], attn_lambda[j], attn_subln_g[j],
                                rel_bias_table, layer)
        x = x + (g_m * y).astype(x.dtype)
        h = _modulate(x, norm_ffn_g[layer], sh_f, sc_f)
        y = _moe_ffn(h, router_w[layer], router_b[layer], moe_w1[layer], moe_b1[layer],
                     moe_w2[layer], moe_b2[layer])
        x = x + (g_f * y).astype(x.dtype)
    return _rmsnorm(x, final_norm_g)
```

```python
import functools
import math

import jax
import jax.numpy as jnp
import numpy as np
from jax import lax
from jax.experimental import pallas as pl
from jax.experimental.pallas import tpu as pltpu

F32 = jnp.float32
BF16 = jnp.bfloat16
HIGHEST = lax.Precision.HIGHEST

NORM_EPS = 1e-6

A_HEAD_DIM = 128
A_CHUNK = 32
A_GROUP = 128
B_HEAD_DIM = 64
B_CHUNK = 64
B_LN_EPS = 1e-5 * B_HEAD_DIM
C_HEAD_DIM = 64
REL_BUCKETS = 32
REL_MAX_DISTANCE = 128
TOP_K = 4
SWIGLU_LIMIT = 7.0
SWIGLU_ALPHA = 1.702

V7X_VMEM_BYTES = 64 * 1024 * 1024
VMEM_LIMIT = V7X_VMEM_BYTES - 8 * 1024 * 1024
LANES = 128


def _params(*sem):
    return pltpu.CompilerParams(dimension_semantics=sem, vmem_limit_bytes=VMEM_LIMIT)


def _sigmoid(x):
    return 1.0 / (1.0 + jnp.exp(-x))


def _silu(x):
    return x * _sigmoid(x)


def _dot(a, b):
    return jnp.dot(a, b, preferred_element_type=F32)


def _dot_nt(a, b):
    return lax.dot_general(a, b, (((1,), (1,)), ((), ())), preferred_element_type=F32)


def _dot_tn(a, b):
    return lax.dot_general(a, b, (((0,), (0,)), ((), ())), preferred_element_type=F32)


def _adaln_kernel(c_ref, w_ref, b_ref, o_ref):
    cond = _silu(c_ref[...])
    o_ref[0] = jnp.dot(cond, w_ref[0], precision=HIGHEST, preferred_element_type=F32) + b_ref[0]


def _adaln(c, ada_w, ada_b):
    n_layers, d, n_out = ada_w.shape
    batch = c.shape[0]
    tn = 1536
    return pl.pallas_call(
        _adaln_kernel,
        grid=(n_layers, n_out // tn),
        in_specs=[
            pl.BlockSpec((batch, d), lambda l, j: (0, 0)),
            pl.BlockSpec((1, d, tn), lambda l, j: (l, 0, j)),
            pl.BlockSpec((1, 1, tn), lambda l, j: (l, 0, j)),
        ],
        out_specs=pl.BlockSpec((1, batch, tn), lambda l, j: (l, 0, j)),
        out_shape=jax.ShapeDtypeStruct((n_layers, batch, n_out), F32),
        compiler_params=_params("parallel", "parallel"),
        name="adaln",
    )(c, ada_w, ada_b.reshape(n_layers, 1, n_out))


def _modulated_norm(x, g, shift, scale):
    y = x * lax.rsqrt(jnp.mean(x * x, axis=-1, keepdims=True) + NORM_EPS)
    return (y * g) * (1.0 + scale) + shift


def _normmod_proj_kernel(x_ref, g_ref, sh_ref, sc_ref, *rest, n_w):
    w_refs, o_refs = rest[:n_w], rest[n_w:]
    h = _modulated_norm(x_ref[...], g_ref[...], sh_ref[0], sc_ref[0]).astype(BF16)
    for w_ref, o_ref in zip(w_refs, o_refs):
        o_ref[...] = _dot(h, w_ref[...]).astype(o_ref.dtype)


def _normmod_proj(x2, g, mod_l, shift_idx, seq, weights, out_dtypes, tm=256):
    n, d = x2.shape
    per_batch = seq // tm
    in_specs = [
        pl.BlockSpec((tm, d), lambda i: (i, 0)),
        pl.BlockSpec((1, d), lambda i: (0, 0)),
        pl.BlockSpec((1, 1, d), lambda i: (i // per_batch, 0, shift_idx)),
        pl.BlockSpec((1, 1, d), lambda i: (i // per_batch, 0, shift_idx + 1)),
    ]
    in_specs += [pl.BlockSpec(w.shape, lambda i: (0, 0)) for w in weights]
    out_specs = [pl.BlockSpec((tm, w.shape[1]), lambda i: (i, 0)) for w in weights]
    out_shape = [jax.ShapeDtypeStruct((n, w.shape[1]), dt) for w, dt in zip(weights, out_dtypes)]
    return pl.pallas_call(
        functools.partial(_normmod_proj_kernel, n_w=len(weights)),
        grid=(n // tm,),
        in_specs=in_specs,
        out_specs=out_specs,
        out_shape=out_shape,
        compiler_params=_params("parallel"),
        name="normmod_proj",
    )(x2, g.reshape(1, d), mod_l, mod_l, *weights)


def _outproj_kernel(x_ref, gate_ref, *rest, n_y):
    y_refs, w_refs, o_ref = rest[:n_y], rest[n_y:2 * n_y], rest[2 * n_y]
    acc = _dot(y_refs[0][...], w_refs[0][...])
    for y_ref, w_ref in zip(y_refs[1:], w_refs[1:]):
        acc += _dot(y_ref[...], w_ref[...])
    o_ref[...] = x_ref[...] + gate_ref[0] * acc


def _outproj_residual(x2, mod_l, gate_idx, seq, ys, ws, tm=512):
    n, d = x2.shape
    per_batch = seq // tm
    in_specs = [
        pl.BlockSpec((tm, d), lambda i: (i, 0)),
        pl.BlockSpec((1, 1, d), lambda i: (i // per_batch, 0, gate_idx)),
    ]
    in_specs += [pl.BlockSpec((tm, y.shape[1]), lambda i: (i, 0)) for y in ys]
    in_specs += [pl.BlockSpec(w.shape, lambda i: (0, 0)) for w in ws]
    return pl.pallas_call(
        functools.partial(_outproj_kernel, n_y=len(ys)),
        grid=(n // tm,),
        in_specs=in_specs,
        out_specs=pl.BlockSpec((tm, d), lambda i: (i, 0)),
        out_shape=jax.ShapeDtypeStruct((n, d), F32),
        compiler_params=_params("parallel"),
        name="outproj_residual",
    )(x2, mod_l, *ys, *ws)


def _chunk_cumsum(x, chunk, reverse):
    rows = x.shape[0]
    pos = lax.broadcasted_iota(jnp.int32, x.shape, 0) % chunk
    s = 1
    while s < chunk:
        if reverse:
            x = x + jnp.where(pos < chunk - s, pltpu.roll(x, rows - s, axis=0), 0.0)
        else:
            x = x + jnp.where(pos >= s, pltpu.roll(x, s, axis=0), 0.0)
        s *= 2
    return x


def _hgrn_group(q, f, v, lb, st, reverse):
    g_rows, dk = q.shape
    n_chunks = g_rows // A_CHUNK
    fg = lb + (1.0 - lb) * _sigmoid(f)
    k = 1.0 - fg
    b = _chunk_cumsum(jnp.log(fg), A_CHUNK, reverse)
    b3 = b.reshape(n_chunks, A_CHUNK, dk)
    edge = b3[:, 0:1, :] if reverse else b3[:, A_CHUNK - 1:A_CHUNK, :]
    q_in = (q * jnp.exp(b)).astype(BF16)
    k_in = (k * jnp.exp(-b)).astype(BF16)
    k_st = (k.reshape(n_chunks, A_CHUNK, dk) * jnp.exp(edge - b3)).astype(BF16)
    decay = jnp.exp(edge)
    vb = v.astype(BF16)
    ti = lax.broadcasted_iota(jnp.int32, (A_CHUNK, A_CHUNK), 0)
    si = lax.broadcasted_iota(jnp.int32, (A_CHUNK, A_CHUNK), 1)
    mask = (si >= ti) if reverse else (si <= ti)
    outs = [None] * n_chunks
    order = range(n_chunks - 1, -1, -1) if reverse else range(n_chunks)
    for c in order:
        rows = slice(c * A_CHUNK, (c + 1) * A_CHUNK)
        scores = jnp.where(mask, _dot_nt(q_in[rows], k_in[rows]), 0.0).astype(BF16)
        outs[c] = _dot(scores, vb[rows]) + _dot_nt(q_in[rows], st.astype(BF16))
        st = st * decay[c] + _dot_tn(vb[rows], k_st[c])
    return jnp.concatenate(outs, axis=0), st


def _hgrn_kernel(q_ref, ff_ref, fb_ref, i_ref, g_ref, lb_ref, ng_ref, o_ref, of_ref, ob_ref):
    seq, dk = q_ref.shape
    n_groups = seq // A_GROUP
    lb_f, lb_b = lb_ref[0:1, :], lb_ref[1:2, :]

    def body(j, carry):
        st_f, st_b = carry
        rf = pl.ds(pl.multiple_of(j * A_GROUP, A_GROUP), A_GROUP)
        rb = pl.ds(pl.multiple_of((n_groups - 1 - j) * A_GROUP, A_GROUP), A_GROUP)
        o_f, st_f = _hgrn_group(_silu(q_ref[rf, :]), ff_ref[rf, :], i_ref[rf, :], lb_f, st_f, False)
        o_b, st_b = _hgrn_group(_silu(q_ref[rb, :]), fb_ref[rb, :], i_ref[rb, :], lb_b, st_b, True)
        of_ref[rf, :] = o_f
        ob_ref[rb, :] = o_b
        return st_f, st_b

    zero = jnp.zeros((dk, dk), F32)
    lax.fori_loop(0, n_groups, body, (zero, zero))

    def finish(j, carry):
        r = pl.ds(pl.multiple_of(j * A_GROUP, A_GROUP), A_GROUP)
        o = of_ref[r, :] + ob_ref[r, :]
        o = o * lax.rsqrt(jnp.mean(o * o, axis=-1, keepdims=True) + NORM_EPS) * ng_ref[...]
        o_ref[r, :] = (o * _silu(g_ref[r, :])).astype(o_ref.dtype)
        return carry

    lax.fori_loop(0, n_groups, finish, 0)


def _hgrn2(p_a, lb, norm_g, batch, seq):
    n = p_a.shape[0]
    width = p_a.shape[1] // 5
    heads = width // A_HEAD_DIM
    sect = lambda s: pl.BlockSpec((seq, A_HEAD_DIM), lambda b, h, s=s: (b, s * heads + h))
    return pl.pallas_call(
        _hgrn_kernel,
        grid=(batch, heads),
        in_specs=[sect(0), sect(1), sect(2), sect(3), sect(4),
                  pl.BlockSpec((2, A_HEAD_DIM), lambda b, h: (0, h)),
                  pl.BlockSpec((1, A_HEAD_DIM), lambda b, h: (0, 0))],
        out_specs=pl.BlockSpec((seq, A_HEAD_DIM), lambda b, h: (b, h)),
        out_shape=jax.ShapeDtypeStruct((n, width), BF16),
        scratch_shapes=[pltpu.VMEM((seq, A_HEAD_DIM), F32), pltpu.VMEM((seq, A_HEAD_DIM), F32)],
        compiler_params=_params("parallel", "parallel"),
        name="hgrn2",
    )(p_a, p_a, p_a, p_a, p_a, lb, norm_g.reshape(1, A_HEAD_DIM))


def _softplus(z):
    return jnp.maximum(z, 0.0) + jnp.log(1.0 + jnp.exp(-jnp.abs(z)))


def _rwkv_prep_kernel(p_ref, prev_ref, next_ref, mu_ref, w0_ref, w2_ref, a0_ref, a2_ref, g2_ref,
                      kk_ref, ka_ref, rk_ref, ones_ref,
                      r_out, k_out, v_out, kkn_out, b_out, lwf_out, lwb_out, g_out, bonus_out):
    i = pl.program_id(1)
    last = pl.num_programs(1) - 1
    ts = p_ref.shape[0]
    width = a0_ref.shape[1]
    heads = width // B_HEAD_DIM
    p = p_ref[...]
    prev_row = jnp.where(i == 0, 0.0, prev_ref[7:8, :])
    next_row = jnp.where(i == last, 0.0, next_ref[0:1, :])
    row = lax.broadcasted_iota(jnp.int32, (ts, 1), 0)
    up = jnp.where(row == 0, prev_row, pltpu.roll(p, 1, axis=0))
    dn = jnp.where(row == ts - 1, next_row, pltpu.roll(p, ts - 1, axis=0))
    p = p + mu_ref[...] * (0.5 * (up + dn) - p)

    r = p[:, 0:width]
    k = p[:, width:2 * width]
    v = p[:, 2 * width:3 * width]
    o = 3 * width
    wlo_f = p[:, o:o + 64]
    wlo_b = p[:, o + 64:o + 128]
    alo = p[:, o + 128:o + 192]
    glo = p[:, o + 192:o + 320]

    def log_decay(wlo, d):
        z = w0_ref[d:d + 1, :] + _dot(jnp.tanh(wlo).astype(BF16), w2_ref[d])
        w = -_softplus(-z) - 0.5
        return -jnp.exp(w)

    lw_f = log_decay(wlo_f, 0)
    lw_b = log_decay(wlo_b, 1)
    a = _sigmoid(a0_ref[...] + _dot(alo.astype(BF16), a2_ref[...]))
    g = _dot(_sigmoid(glo).astype(BF16), g2_ref[...])
    kk = k * kk_ref[...]
    ones = ones_ref[...]
    head_sum = lambda t: jnp.dot(t, ones, precision=HIGHEST, preferred_element_type=F32)
    kk_n = kk / jnp.maximum(jnp.sqrt(head_sum(kk * kk)), 1e-12)
    k_mod = k * (1.0 + (a - 1.0) * ka_ref[...])
    bonus = head_sum(r * k_mod * rk_ref[...]) * v
    g_out[...] = g
    bonus_out[...] = bonus
    for h in range(heads):
        sl = slice(h * B_HEAD_DIM, (h + 1) * B_HEAD_DIM)
        r_out[0, h] = r[:, sl]
        k_out[0, h] = k_mod[:, sl]
        v_out[0, h] = v[:, sl]
        kkn_out[0, h] = kk_n[:, sl]
        b_out[0, h] = (kk_n * a)[:, sl]
        lwf_out[0, h] = lw_f[:, sl]
        lwb_out[0, h] = lw_b[:, sl]


def _rwkv_prep(p_b, mu, w0, w2, a0, a2, g2, k_k, k_a, r_k, batch, seq, ts=256):
    n, cols = p_b.shape
    width = a0.shape[0]
    heads = width // B_HEAD_DIM
    nblk = seq // ts
    rows8 = ts // 8
    head_id = np.arange(width) // B_HEAD_DIM
    ones = jnp.asarray(head_id[:, None] == head_id[None, :], F32)
    full = lambda a: pl.BlockSpec(a.shape, lambda b, i: (0,) * a.ndim)
    vec = lambda a: a.reshape(1, -1)
    args = [vec(mu), w0, w2.astype(BF16), vec(a0), a2.astype(BF16), g2.astype(BF16),
            vec(k_k), vec(k_a), vec(r_k), ones]
    hm = pl.BlockSpec((1, heads, ts, B_HEAD_DIM), lambda b, i: (b, 0, i, 0))
    tokm = pl.BlockSpec((ts, width), lambda b, i: (b * nblk + i, 0))
    hm_shape = jax.ShapeDtypeStruct((batch, heads, seq, B_HEAD_DIM), F32)
    tok_shape = jax.ShapeDtypeStruct((n, width), F32)
    return pl.pallas_call(
        _rwkv_prep_kernel,
        grid=(batch, nblk),
        in_specs=[
            pl.BlockSpec((ts, cols), lambda b, i: (b * nblk + i, 0)),
            pl.BlockSpec((8, cols), lambda b, i: (jnp.maximum((b * nblk + i) * rows8 - 1, 0), 0)),
            pl.BlockSpec((8, cols), lambda b, i: (jnp.minimum((b * nblk + i + 1) * rows8, n // 8 - 1), 0)),
        ] + [full(a) for a in args],
        out_specs=[hm] * 7 + [tokm] * 2,
        out_shape=[hm_shape] * 7 + [tok_shape] * 2,
        compiler_params=_params("parallel", "parallel"),
        name="rwkv_prep",
    )(p_b, p_b, p_b, *args)


def _nilpotent_inverse(a):
    c = a.shape[0]
    ri = lax.broadcasted_iota(jnp.int32, (c, c), 0)
    ci = lax.broadcasted_iota(jnp.int32, (c, c), 1)
    p = jnp.where(ri == ci, 1.0, 0.0) - a
    aj = _dot(a.astype(BF16), a.astype(BF16))
    span = 2
    while span < c:
        ajb = aj.astype(BF16)
        span *= 2
        if span < c:
            both = _dot(jnp.concatenate([aj, p], axis=0).astype(BF16), ajb)
            aj, p = both[:c], p + both[c:]
        else:
            p = p + _dot(p.astype(BF16), ajb)
    return p


def _rwkv_chunk(r, k, v, kk, b, lw, tt, reverse):
    c, d = r.shape
    cum = _chunk_cumsum(lw, c, reverse)
    tot = cum[0:1, :] if reverse else cum[c - 1:c, :]
    grow = jnp.exp(-cum)
    kt = kk * jnp.exp(cum - lw)
    rt = r * jnp.exp(cum)
    k_ = (k * grow).astype(BF16)
    b_ = (b * grow).astype(BF16)
    vb = v.astype(BF16)
    kb = jnp.concatenate([k_, b_], axis=0)
    big = _dot_nt(jnp.concatenate([kt, rt], axis=0).astype(BF16), kb)
    ti = lax.broadcasted_iota(jnp.int32, (c, c), 0)
    si = lax.broadcasted_iota(jnp.int32, (c, c), 1)
    strict = (si > ti) if reverse else (si < ti)
    incl = (si >= ti) if reverse else (si <= ti)
    a_ak = jnp.where(strict, big[:c, :c], 0.0)
    a_ab = jnp.where(strict, big[:c, c:], 0.0)
    p_rk = jnp.where(incl, big[c:, :c], 0.0)
    p_rb = jnp.where(incl, big[c:, c:], 0.0)
    m = _nilpotent_inverse(a_ab)
    av = _dot(a_ak.astype(BF16), vb)
    wu = _dot(m.astype(BF16), jnp.concatenate([kt, av], axis=1).astype(BF16))
    w_, u0 = wu[:, :d], wu[:, d:]
    rhs = jnp.concatenate([jnp.concatenate([v, jnp.zeros_like(v)], axis=1),
                           jnp.concatenate([u0, w_], axis=1)], axis=0).astype(BF16)
    yq = _dot(jnp.concatenate([p_rk, -p_rb], axis=1).astype(BF16), rhs)
    y1, q = yq[:, :d], rt + yq[:, d:]
    h_t = _dot_tn(jnp.concatenate([v, -u0], axis=0).astype(BF16), kb)
    bw = _dot_tn(b_, w_.astype(BF16))
    ttb = tt.astype(BF16)
    y = _dot_nt(q.astype(BF16), ttb) + y1
    tt = (tt - _dot_nt(ttb, bw.astype(BF16)) + h_t) * jnp.exp(tot)
    return y, tt


def _rwkv_scan_kernel(rf, kf, vf, kkf, bf, lwf, rb, kb, vb, kkb, bb, lwb, yf_ref, yb_ref, tf_ref, tb_ref):
    heads, ts = rf.shape[1], rf.shape[2]
    n_chunks = ts // B_CHUNK

    @pl.when(pl.program_id(1) == 0)
    def _():
        tf_ref[...] = jnp.zeros_like(tf_ref)
        tb_ref[...] = jnp.zeros_like(tb_ref)

    def body(j, carry):
        rows_f = pl.ds(pl.multiple_of(j * B_CHUNK, B_CHUNK), B_CHUNK)
        rows_b = pl.ds(pl.multiple_of((n_chunks - 1 - j) * B_CHUNK, B_CHUNK), B_CHUNK)
        for h in range(heads):
            ld = lambda ref, rows: ref[0, h, rows, :]
            y, tt = _rwkv_chunk(ld(rf, rows_f), ld(kf, rows_f), ld(vf, rows_f), ld(kkf, rows_f),
                                ld(bf, rows_f), ld(lwf, rows_f), tf_ref[h], False)
            yf_ref[0, h, rows_f, :] = y
            tf_ref[h] = tt
            y, tt = _rwkv_chunk(ld(rb, rows_b), ld(kb, rows_b), ld(vb, rows_b), ld(kkb, rows_b),
                                ld(bb, rows_b), ld(lwb, rows_b), tb_ref[h], True)
            yb_ref[0, h, rows_b, :] = y
            tb_ref[h] = tt
        return carry

    lax.fori_loop(0, n_chunks, body, 0)


def _rwkv_scan(r, k, v, kkn, b, lw_f, lw_b, ts=256):
    batch, heads, seq, d = r.shape
    nblk = seq // ts
    fwd = pl.BlockSpec((1, heads, ts, d), lambda bi, i: (bi, 0, i, 0))
    bwd = pl.BlockSpec((1, heads, ts, d), lambda bi, i: (bi, 0, nblk - 1 - i, 0))
    shape = jax.ShapeDtypeStruct(r.shape, F32)
    return pl.pallas_call(
        _rwkv_scan_kernel,
        grid=(batch, nblk),
        in_specs=[fwd] * 6 + [bwd] * 6,
        out_specs=[fwd, bwd],
        out_shape=[shape, shape],
        scratch_shapes=[pltpu.VMEM((heads, d, d), F32), pltpu.VMEM((heads, d, d), F32)],
        compiler_params=_params("parallel", "arbitrary"),
        name="rwkv_scan",
    )(r, k, v, kkn, b, lw_f, r, k, v, kkn, b, lw_b)


def _rwkv_post_kernel(yf_ref, yb_ref, g_ref, bonus_ref, lng_ref, lnb_ref, o_ref):
    heads = yf_ref.shape[1]
    outs = []
    for h in range(heads):
        y = yf_ref[0, h] + yb_ref[0, h]
        mean = jnp.mean(y, axis=-1, keepdims=True)
        var = jnp.mean(jnp.square(y - mean), axis=-1, keepdims=True)
        outs.append((y - mean) * lax.rsqrt(var + B_LN_EPS))
    yn = jnp.concatenate(outs, axis=1) * lng_ref[...] + lnb_ref[...]
    o_ref[...] = ((yn + bonus_ref[...]) * g_ref[...]).astype(o_ref.dtype)


def _rwkv_post(y_f, y_b, g, bonus, ln_g, ln_b, ts=256):
    batch, heads, seq, d = y_f.shape
    n, width = g.shape
    nblk = seq // ts
    hm = pl.BlockSpec((1, heads, ts, d), lambda b, i: (b, 0, i, 0))
    tokm = pl.BlockSpec((ts, width), lambda b, i: (b * nblk + i, 0))
    vec = pl.BlockSpec((1, width), lambda b, i: (0, 0))
    return pl.pallas_call(
        _rwkv_post_kernel,
        grid=(batch, nblk),
        in_specs=[hm, hm, tokm, tokm, vec, vec],
        out_specs=tokm,
        out_shape=jax.ShapeDtypeStruct((n, width), BF16),
        compiler_params=_params("parallel", "parallel"),
        name="rwkv_post",
    )(y_f, y_b, g, bonus, ln_g.reshape(1, width), ln_b.reshape(1, width))


def _rwkv7(p_b, mu, w0, w2, a0, a2, g2, k_k, k_a, r_k, ln_g, ln_b, batch, seq):
    r, k, v, kkn, b, lw_f, lw_b, g, bonus = _rwkv_prep(p_b, mu, w0, w2, a0, a2, g2, k_k, k_a, r_k, batch, seq)
    y_f, y_b = _rwkv_scan(r, k, v, kkn, b, lw_f, lw_b)
    return _rwkv_post(y_f, y_b, g, bonus, ln_g, ln_b)


_BUCKET_EDGES = (0, 1, 2, 3, 4, 5, 6, 7, 8, 12, 16, 23, 32, 46, 64, 91)
ATTN_TK = 128


def _bias_band_kernel(table_ref, o_ref):
    h = pl.program_id(0)
    n_d, tq, tk = o_ref.shape[1:]
    half = REL_BUCKETS // 2
    r = lax.broadcasted_iota(jnp.int32, (tq, tk), 0)
    c = lax.broadcasted_iota(jnp.int32, (tq, tk), 1)
    for d in range(n_d):
        rel = tk * (d - 1) + c - r
        n = jnp.abs(rel)
        vals = []
        for sign in range(2):
            val = jnp.full((tq, tk), table_ref[sign * half + half - 1, h], F32)
            for bkt in range(half - 2, -1, -1):
                val = jnp.where(n < _BUCKET_EDGES[bkt + 1], table_ref[sign * half + bkt, h], val)
            vals.append(val)
        o_ref[0, d] = jnp.where(rel > 0, vals[1], vals[0])


def _bias_band(rel_table, tq):
    heads = rel_table.shape[1]
    n_d = tq // ATTN_TK + 2
    return pl.pallas_call(
        _bias_band_kernel,
        grid=(heads,),
        in_specs=[pl.BlockSpec(memory_space=pltpu.SMEM)],
        out_specs=pl.BlockSpec((1, n_d, tq, ATTN_TK), lambda h: (h, 0, 0, 0)),
        out_shape=jax.ShapeDtypeStruct((heads, n_d, tq, ATTN_TK), F32),
        compiler_params=_params("parallel"),
        name="bias_band",
    )(rel_table)


def _diff_attn_kernel(table_ref, lam_ref, q_ref, k_ref, v_ref, band_ref, g_ref, o_ref, *, lam_init):
    h = pl.program_id(1)
    i = pl.program_id(2)
    tq, dh2 = q_ref.shape
    seq = k_ref.shape[0]
    n_d = band_ref.shape[1]
    ratio = tq // ATTN_TK
    half = REL_BUCKETS // 2
    q = q_ref[...]
    k = k_ref[...]
    lane = lax.broadcasted_iota(jnp.int32, (1, dh2), 1)
    zero = jnp.zeros_like(q)
    far_neg = table_ref[half - 1, h]
    far_pos = table_ref[REL_BUCKETS - 1, h]
    tiles = []
    for j in range(seq // ATTN_TK):
        off = j - ratio * i
        near = band_ref[0, jnp.clip(off + 1, 0, n_d - 1)]
        tiles.append(jnp.where(off < -1, far_neg, jnp.where(off > ratio, far_pos, near)))
    bias = jnp.concatenate(tiles, axis=1)
    scale = C_HEAD_DIM ** -0.5
    probs = []
    for m in range(2):
        qm = jnp.where((lane >= m * C_HEAD_DIM) & (lane < (m + 1) * C_HEAD_DIM), q, zero)
        logits = _dot_nt(qm, k) * scale + bias
        e = jnp.exp(logits - jnp.max(logits, axis=-1, keepdims=True))
        probs.append(e * (1.0 / jnp.sum(e, axis=-1, keepdims=True)))
    attn = (probs[0] - lam_ref[0] * probs[1]).astype(BF16)
    o = _dot(attn, v_ref[...])
    o = o * lax.rsqrt(jnp.mean(o * o, axis=-1, keepdims=True) + NORM_EPS) * g_ref[...]
    o_ref[...] = (o * (1.0 - lam_init)).astype(o_ref.dtype)


def _diff_attention(qkv, rel_table, lam_full, subln_g, lam_init, batch, seq, tq=256):
    n = qkv.shape[0]
    dh2 = 2 * C_HEAD_DIM
    heads = qkv.shape[1] // (3 * dh2)
    nq = seq // tq
    band = _bias_band(rel_table, tq)
    return pl.pallas_call(
        functools.partial(_diff_attn_kernel, lam_init=lam_init),
        grid=(batch, heads, nq),
        in_specs=[
            pl.BlockSpec(memory_space=pltpu.SMEM),
            pl.BlockSpec(memory_space=pltpu.SMEM),
            pl.BlockSpec((tq, dh2), lambda b, h, i: (b * nq + i, h)),
            pl.BlockSpec((seq, dh2), lambda b, h, i: (b, heads + h)),
            pl.BlockSpec((seq, dh2), lambda b, h, i: (b, 2 * heads + h)),
            pl.BlockSpec((1,) + band.shape[1:], lambda b, h, i: (h, 0, 0, 0)),
            pl.BlockSpec((1, dh2), lambda b, h, i: (0, 0)),
        ],
        out_specs=pl.BlockSpec((tq, dh2), lambda b, h, i: (b * nq + i, h)),
        out_shape=jax.ShapeDtypeStruct((n, heads * dh2), BF16),
        compiler_params=_params("parallel", "parallel", "parallel"),
        name="diff_attention",
    )(rel_table, lam_full.reshape(1), qkv, qkv, qkv, band, subln_g.reshape(1, dh2))


MOE_TILE = 256
_HI_MASK = np.uint32(0xFFFF0000)


def _pack_bf16_pair(lo, hi):
    as_bits = lambda t: lax.bitcast_convert_type(t.astype(BF16).astype(F32), jnp.uint32)
    return (as_bits(hi) & _HI_MASK) | (as_bits(lo) >> 16)


def _unpack_bf16_pair(word):
    lo = lax.bitcast_convert_type(word << 16, F32).astype(BF16)
    hi = lax.bitcast_convert_type(word & _HI_MASK, F32).astype(BF16)
    return lo, hi


def _router_kernel(x_ref, g_ref, sh_ref, sc_ref, rw_ref, rb_ref, hp_ref, idx_ref, w_ref, rank_ref, cnt_ref,
                   run_ref, *, n_experts):
    @pl.when(pl.program_id(0) == 0)
    def _():
        run_ref[...] = jnp.zeros_like(run_ref)

    tm, d = x_ref.shape
    h = _modulated_norm(x_ref[...], g_ref[...], sh_ref[0], sc_ref[0])
    hp_ref[...] = _pack_bf16_pair(h[:, :d // 2], h[:, d // 2:])
    logits = jnp.dot(h, rw_ref[...], precision=HIGHEST, preferred_element_type=F32) + rb_ref[...]
    lane = lax.broadcasted_iota(jnp.int32, logits.shape, 1).astype(F32)
    neg = jnp.float32(-jnp.inf)
    logits = jnp.where(lane < n_experts, logits, neg)
    picks, vals = [], []
    for _ in range(TOP_K):
        m = jnp.max(logits, axis=-1, keepdims=True)
        first = jnp.min(jnp.where(logits == m, lane, float(LANES)), axis=-1, keepdims=True)
        hit = lane == first
        picks.append((first, hit))
        vals.append(m)
        logits = jnp.where(hit, neg, logits)
    es = [jnp.exp(v - vals[0]) for v in vals]
    inv = 1.0 / (es[0] + es[1] + es[2] + es[3])
    assigned = picks[0][1] | picks[1][1] | picks[2][1] | picks[3][1]
    onehot = jnp.where(assigned, 1.0, 0.0)
    ri = lax.broadcasted_iota(jnp.int32, (tm, tm), 0)
    ci = lax.broadcasted_iota(jnp.int32, (tm, tm), 1)
    earlier = jnp.where(ci < ri, 1.0, 0.0).astype(BF16)
    before = _dot(earlier, onehot.astype(BF16)) + run_ref[...]
    idx_o = jnp.zeros(logits.shape, F32)
    w_o = jnp.zeros(logits.shape, F32)
    rank_o = jnp.zeros(logits.shape, F32)
    for j, (first, hit) in enumerate(picks):
        rank_j = jnp.sum(jnp.where(hit, before, 0.0), axis=-1, keepdims=True)
        idx_o = jnp.where(lane == j, first, idx_o)
        w_o = jnp.where(lane == j, es[j] * inv, w_o)
        rank_o = jnp.where(lane == j, rank_j, rank_o)
    idx_ref[...] = idx_o
    w_ref[...] = w_o
    rank_ref[...] = rank_o
    run_ref[...] += jnp.sum(onehot, axis=0, keepdims=True)
    cnt_ref[...] = run_ref[...]


def _router(x2, g, mod_l, shift_idx, seq, router_w, router_b):
    n, d = x2.shape
    tm = MOE_TILE
    n_experts = router_w.shape[1]
    per_batch = seq // tm
    rw = jnp.zeros((d, LANES), F32).at[:, :n_experts].set(router_w)
    rb = jnp.zeros((1, LANES), F32).at[0, :n_experts].set(router_b)
    lane_out = pl.BlockSpec((tm, LANES), lambda i: (i, 0))
    lane_shape = jax.ShapeDtypeStruct((n, LANES), F32)
    return pl.pallas_call(
        functools.partial(_router_kernel, n_experts=n_experts),
        grid=(n // tm,),
        in_specs=[
            pl.BlockSpec((tm, d), lambda i: (i, 0)),
            pl.BlockSpec((1, d), lambda i: (0, 0)),
            pl.BlockSpec((1, 1, d), lambda i: (i // per_batch, 0, shift_idx)),
            pl.BlockSpec((1, 1, d), lambda i: (i // per_batch, 0, shift_idx + 1)),
            pl.BlockSpec((d, LANES), lambda i: (0, 0)),
            pl.BlockSpec((1, LANES), lambda i: (0, 0)),
        ],
        out_specs=[pl.BlockSpec((tm, d // 2), lambda i: (i, 0)), lane_out, lane_out, lane_out,
                   pl.BlockSpec((1, LANES), lambda i: (0, 0))],
        out_shape=[jax.ShapeDtypeStruct((n, d // 2), jnp.uint32), lane_shape, lane_shape, lane_shape,
                   jax.ShapeDtypeStruct((1, LANES), F32)],
        scratch_shapes=[pltpu.VMEM((1, LANES), F32)],
        compiler_params=_params("arbitrary"),
        name="moe_router",
    )(x2, g.reshape(1, d), mod_l, mod_l, rw, rb)


def _dispatch_kernel(pos_ref, h_ref, zeros_ref, xs_ref, sem):
    del zeros_ref
    tm = h_ref.shape[0]

    def row_copy(r, p):
        return pltpu.make_async_copy(h_ref.at[pl.ds(r, 1)], xs_ref.at[pl.ds(p, 1)], sem)

    def issue(r, carry):
        for j in range(TOP_K):
            row_copy(r, pos_ref[r * TOP_K + j]).start()
        return carry

    lax.fori_loop(0, tm, issue, 0)

    def drain(r, carry):
        for j in range(TOP_K):
            row_copy(r, pos_ref[r * TOP_K + j]).wait()
        return carry

    lax.fori_loop(0, tm, drain, 0)


def _dispatch(hp, pos_flat, n_rows):
    n, dw = hp.shape
    tm = MOE_TILE
    zeros = jnp.zeros((n_rows, dw), jnp.uint32)
    return pl.pallas_call(
        _dispatch_kernel,
        grid=(n // tm,),
        in_specs=[
            pl.BlockSpec((tm * TOP_K,), lambda i: (i,), memory_space=pltpu.SMEM),
            pl.BlockSpec((tm, dw), lambda i: (i, 0)),
            pl.BlockSpec(memory_space=pl.ANY),
        ],
        out_specs=pl.BlockSpec(memory_space=pl.ANY),
        out_shape=jax.ShapeDtypeStruct((n_rows, dw), jnp.uint32),
        scratch_shapes=[pltpu.SemaphoreType.DMA(())],
        input_output_aliases={2: 0},
        compiler_params=_params("arbitrary"),
        name="moe_dispatch",
    )(pos_flat, hp, zeros)


def _experts_kernel(te_ref, nu_ref, xs_ref, w1g_ref, w1l_ref, b1g_ref, b1l_ref, w2_ref, b2_ref, ys_ref):
    del te_ref

    @pl.when(pl.program_id(0) < nu_ref[0])
    def _():
        lo, hi = _unpack_bf16_pair(xs_ref[...])
        x = jnp.concatenate([lo, hi], axis=1)
        x_glu = jnp.minimum(_dot(x, w1g_ref[0]) + b1g_ref[0], SWIGLU_LIMIT)
        x_lin = jnp.clip(_dot(x, w1l_ref[0]) + b1l_ref[0], -SWIGLU_LIMIT, SWIGLU_LIMIT)
        act = x_glu * _sigmoid(SWIGLU_ALPHA * x_glu) * (x_lin + 1.0)
        ys_ref[...] = _dot(act.astype(BF16), w2_ref[0]) + b2_ref[0]

    @pl.when(pl.program_id(0) >= nu_ref[0])
    def _():
        ys_ref[...] = jnp.zeros_like(ys_ref)


def _experts(xs, tile_expert, n_used, w1g, w1l, b1g, b1l, w2, b2):
    n_rows, dw = xs.shape
    tm = MOE_TILE
    n_exp, d, de = w1g.shape
    row = lambda t, te, nu: (jnp.minimum(t, nu[0] - 1), 0)
    exp3 = lambda t, te, nu: (te[t], 0, 0)
    grid_spec = pltpu.PrefetchScalarGridSpec(
        num_scalar_prefetch=2,
        grid=(n_rows // tm,),
        in_specs=[
            pl.BlockSpec((tm, dw), row),
            pl.BlockSpec((1, d, de), exp3),
            pl.BlockSpec((1, d, de), exp3),
            pl.BlockSpec((1, 1, de), exp3),
            pl.BlockSpec((1, 1, de), exp3),
            pl.BlockSpec((1, de, d), exp3),
            pl.BlockSpec((1, 1, d), exp3),
        ],
        out_specs=pl.BlockSpec((tm, d), lambda t, te, nu: (t, 0)),
    )
    return pl.pallas_call(
        _experts_kernel,
        grid_spec=grid_spec,
        out_shape=jax.ShapeDtypeStruct((n_rows, d), F32),
        compiler_params=_params("arbitrary"),
        name="moe_experts",
    )(tile_expert, n_used, xs, w1g, w1l, b1g, b1l, w2, b2)


def _combine_kernel(pos_ref, x_ref, gate_ref, w_ref, fg_ref, ys_ref, o_ref, buf, sem, *, final_norm):
    tm = x_ref.shape[0]

    def row_copy(r, j):
        return pltpu.make_async_copy(ys_ref.at[pl.ds(pos_ref[r * TOP_K + j], 1)], buf.at[j, pl.ds(r, 1)], sem)

    def issue(r, carry):
        for j in range(TOP_K):
            row_copy(r, j).start()
        return carry

    lax.fori_loop(0, tm, issue, 0)

    def drain(r, carry):
        for j in range(TOP_K):
            row_copy(r, j).wait()
        return carry

    lax.fori_loop(0, tm, drain, 0)
    w = w_ref[...]
    y = w[:, 0:1] * buf[0]
    for j in range(1, TOP_K):
        y += w[:, j:j + 1] * buf[j]
    out = x_ref[...] + gate_ref[0] * y
    if final_norm:
        out = out * lax.rsqrt(jnp.mean(out * out, axis=-1, keepdims=True) + NORM_EPS) * fg_ref[...]
    o_ref[...] = out


def _combine(x2, mod_l, gate_idx, seq, top_w, pos_flat, ys, final_g, final_norm):
    n, d = x2.shape
    tm = MOE_TILE
    per_batch = seq // tm
    return pl.pallas_call(
        functools.partial(_combine_kernel, final_norm=final_norm),
        grid=(n // tm,),
        in_specs=[
            pl.BlockSpec((tm * TOP_K,), lambda i: (i,), memory_space=pltpu.SMEM),
            pl.BlockSpec((tm, d), lambda i: (i, 0)),
            pl.BlockSpec((1, 1, d), lambda i: (i // per_batch, 0, gate_idx)),
            pl.BlockSpec((tm, LANES), lambda i: (i, 0)),
            pl.BlockSpec((1, d), lambda i: (0, 0)),
            pl.BlockSpec(memory_space=pl.ANY),
        ],
        out_specs=pl.BlockSpec((tm, d), lambda i: (i, 0)),
        out_shape=jax.ShapeDtypeStruct((n, d), F32),
        scratch_shapes=[pltpu.VMEM((TOP_K, tm, d), F32), pltpu.SemaphoreType.DMA(())],
        compiler_params=_params("arbitrary"),
        name="moe_combine",
    )(pos_flat, x2, mod_l, top_w, final_g.reshape(1, d), ys)


def _moe_ffn(x2, norm_g, mod_l, seq, router_w, router_b, w1, b1, w2, b2, final_g, final_norm):
    n, d = x2.shape
    n_exp = router_w.shape[1]
    tile = MOE_TILE
    hp, idx_f, top_w, rank_f, counts = _router(x2, norm_g, mod_l, 3, seq, router_w, router_b)
    counts = counts[0, :n_exp].astype(jnp.int32)
    padded = (counts + tile - 1) // tile * tile
    ends = jnp.cumsum(padded)
    starts = ends - padded
    idx = idx_f[:, :TOP_K].astype(jnp.int32)
    pos_flat = (starts[idx] + rank_f[:, :TOP_K].astype(jnp.int32)).reshape(-1)
    n_rows = n * TOP_K + n_exp * tile
    n_tiles = n_rows // tile
    tile_expert = jnp.searchsorted(ends, jnp.arange(n_tiles, dtype=jnp.int32) * tile, side="right")
    tile_expert = jnp.minimum(tile_expert, n_exp - 1).astype(jnp.int32)
    n_used = (ends[-1:] // tile).astype(jnp.int32)
    xs = _dispatch(hp, pos_flat, n_rows)
    de = w1.shape[2] // 2
    w1g = w1[:, :, 0::2].astype(BF16)
    w1l = w1[:, :, 1::2].astype(BF16)
    b1g = b1[:, 0::2].reshape(n_exp, 1, de)
    b1l = b1[:, 1::2].reshape(n_exp, 1, de)
    ys = _experts(xs, tile_expert, n_used, w1g, w1l, b1g, b1l, w2.astype(BF16), b2.reshape(n_exp, 1, d))
    return _combine(x2, mod_l, 5, seq, top_w, pos_flat, ys, final_g, final_norm)


def kernel(x, c, ada_w, ada_b, norm_mix_g, norm_ffn_g, router_w, router_b, moe_w1, moe_b1, moe_w2, moe_b2, ab_w_in, ab_w_out, hgrn_lb, hgrn_norm_g, rwkv_mu, rwkv_w0, rwkv_w2, rwkv_a0, rwkv_a2, rwkv_g2, rwkv_k_k, rwkv_k_a, rwkv_r_k, rwkv_ln_g, rwkv_ln_b, attn_w_in, attn_w_out, attn_lambda, attn_subln_g, rel_bias_table, final_norm_g):
    batch, seq, d = x.shape
    n = batch * seq
    depth = ada_w.shape[0]
    x2 = x.reshape(n, d)
    mod = _adaln(c, ada_w, ada_b)
    lb_all = jnp.cumsum(jax.nn.softmax(hgrn_lb.astype(F32), axis=1), axis=1)
    for layer in range(depth):
        mod_l = mod[layer].reshape(batch, 1, 6 * d)
        j = layer // 2
        if layer % 2 == 0:
            a_cols = 5 * (d // 2)
            w_in = ab_w_in[j].astype(BF16)
            p_a, p_b = _normmod_proj(x2, norm_mix_g[layer], mod_l, 0, seq,
                                     [w_in[:, :a_cols], w_in[:, a_cols:]], [F32, F32])
            y_a = _hgrn2(p_a, lb_all[:, j], hgrn_norm_g[j], batch, seq)
            y_b = _rwkv7(p_b, rwkv_mu[j], rwkv_w0[j], rwkv_w2[j], rwkv_a0[j], rwkv_a2[j], rwkv_g2[j],
                         rwkv_k_k[j], rwkv_k_a[j], rwkv_r_k[j], rwkv_ln_g[j], rwkv_ln_b[j], batch, seq)
            w_out = ab_w_out[j].astype(BF16)
            x2 = _outproj_residual(x2, mod_l, 2, seq, [y_a, y_b], [w_out[:d // 2], w_out[d // 2:]])
        else:
            (qkv,) = _normmod_proj(x2, norm_mix_g[layer], mod_l, 0, seq, [attn_w_in[j].astype(BF16)], [BF16])
            lam = attn_lambda[j].astype(F32)
            lam_init = 0.8 - 0.6 * math.exp(-0.3 * layer)
            lam_full = jnp.exp(jnp.sum(lam[0] * lam[1])) - jnp.exp(jnp.sum(lam[2] * lam[3])) + lam_init
            o = _diff_attention(qkv, rel_bias_table, lam_full, attn_subln_g[j], lam_init, batch, seq)
            x2 = _outproj_residual(x2, mod_l, 2, seq, [o], [attn_w_out[j].astype(BF16)])
        x2 = _moe_ffn(x2, norm_ffn_g[layer], mod_l, seq, router_w[layer], router_b[layer], moe_w1[layer],
                      moe_b1[layer], moe_w2[layer], moe_b2[layer], final_norm_g, layer == depth - 1)
    return x2.reshape(batch, seq, d)
```

```python
import functools
import math

import jax
import jax.numpy as jnp
import numpy as np
from jax import lax
from jax.experimental import pallas as pl
from jax.experimental.pallas import tpu as pltpu

F32 = jnp.float32
BF16 = jnp.bfloat16
HIGHEST = lax.Precision.HIGHEST

NORM_EPS = 1e-6

A_HEAD_DIM = 128
A_CHUNK = 32
A_GROUP = 128
B_HEAD_DIM = 64
B_CHUNK = 64
B_LN_EPS = 1e-5 * B_HEAD_DIM
C_HEAD_DIM = 64
REL_BUCKETS = 32
REL_MAX_DISTANCE = 128
TOP_K = 4
SWIGLU_LIMIT = 7.0
SWIGLU_ALPHA = 1.702

V7X_VMEM_BYTES = 64 * 1024 * 1024
VMEM_LIMIT = V7X_VMEM_BYTES - 8 * 1024 * 1024
LANES = 128


def _params(*sem):
    return pltpu.CompilerParams(dimension_semantics=sem, vmem_limit_bytes=VMEM_LIMIT)


def _sigmoid(x):
    return 1.0 / (1.0 + jnp.exp(-x))


def _silu(x):
    return x * _sigmoid(x)


def _dot(a, b):
    return jnp.dot(a, b, preferred_element_type=F32)


def _dot_nt(a, b):
    return lax.dot_general(a, b, (((1,), (1,)), ((), ())), preferred_element_type=F32)


def _dot_tn(a, b):
    return lax.dot_general(a, b, (((0,), (0,)), ((), ())), preferred_element_type=F32)


def _adaln_kernel(c_ref, w_ref, b_ref, o_ref):
    cond = _silu(c_ref[...])
    o_ref[0] = jnp.dot(cond, w_ref[0], precision=HIGHEST, preferred_element_type=F32) + b_ref[0]


def _adaln(c, ada_w, ada_b):
    n_layers, d, n_out = ada_w.shape
    batch = c.shape[0]
    tn = 1536
    return pl.pallas_call(
        _adaln_kernel,
        grid=(n_layers, n_out // tn),
        in_specs=[
            pl.BlockSpec((batch, d), lambda l, j: (0, 0)),
            pl.BlockSpec((1, d, tn), lambda l, j: (l, 0, j)),
            pl.BlockSpec((1, 1, tn), lambda l, j: (l, 0, j)),
        ],
        out_specs=pl.BlockSpec((1, batch, tn), lambda l, j: (l, 0, j)),
        out_shape=jax.ShapeDtypeStruct((n_layers, batch, n_out), F32),
        compiler_params=_params("parallel", "parallel"),
        name="adaln",
    )(c, ada_w, ada_b.reshape(n_layers, 1, n_out))


def _modulated_norm(x, g, shift, scale):
    y = x * lax.rsqrt(jnp.mean(x * x, axis=-1, keepdims=True) + NORM_EPS)
    return (y * g) * (1.0 + scale) + shift


def _normmod_proj_kernel(x_ref, g_ref, sh_ref, sc_ref, *rest, n_w):
    w_refs, o_refs = rest[:n_w], rest[n_w:]
    h = _modulated_norm(x_ref[...], g_ref[...], sh_ref[0], sc_ref[0]).astype(BF16)
    for w_ref, o_ref in zip(w_refs, o_refs):
        o_ref[...] = _dot(h, w_ref[...]).astype(o_ref.dtype)


def _normmod_proj(x2, g, mod_l, shift_idx, seq, weights, out_dtypes, tm=256):
    n, d = x2.shape
    per_batch = seq // tm
    in_specs = [
        pl.BlockSpec((tm, d), lambda i: (i, 0)),
        pl.BlockSpec((1, d), lambda i: (0, 0)),
        pl.BlockSpec((1, 1, d), lambda i: (i // per_batch, 0, shift_idx)),
        pl.BlockSpec((1, 1, d), lambda i: (i // per_batch, 0, shift_idx + 1)),
    ]
    in_specs += [pl.BlockSpec(w.shape, lambda i: (0, 0)) for w in weights]
    out_specs = [pl.BlockSpec((tm, w.shape[1]), lambda i: (i, 0)) for w in weights]
    out_shape = [jax.ShapeDtypeStruct((n, w.shape[1]), dt) for w, dt in zip(weights, out_dtypes)]
    return pl.pallas_call(
        functools.partial(_normmod_proj_kernel, n_w=len(weights)),
        grid=(n // tm,),
        in_specs=in_specs,
        out_specs=out_specs,
        out_shape=out_shape,
        compiler_params=_params("parallel"),
        name="normmod_proj",
    )(x2, g.reshape(1, d), mod_l, mod_l, *weights)


def _outproj_kernel(x_ref, gate_ref, *rest, n_y):
    y_refs, w_refs, o_ref = rest[:n_y], rest[n_y:2 * n_y], rest[2 * n_y]
    acc = _dot(y_refs[0][...], w_refs[0][...])
    for y_ref, w_ref in zip(y_refs[1:], w_refs[1:]):
        acc += _dot(y_ref[...], w_ref[...])
    o_ref[...] = x_ref[...] + gate_ref[0] * acc


def _outproj_residual(x2, mod_l, gate_idx, seq, ys, ws, tm=512):
    n, d = x2.shape
    per_batch = seq // tm
    in_specs = [
        pl.BlockSpec((tm, d), lambda i: (i, 0)),
        pl.BlockSpec((1, 1, d), lambda i: (i // per_batch, 0, gate_idx)),
    ]
    in_specs += [pl.BlockSpec((tm, y.shape[1]), lambda i: (i, 0)) for y in ys]
    in_specs += [pl.BlockSpec(w.shape, lambda i: (0, 0)) for w in ws]
    return pl.pallas_call(
        functools.partial(_outproj_kernel, n_y=len(ys)),
        grid=(n // tm,),
        in_specs=in_specs,
        out_specs=pl.BlockSpec((tm, d), lambda i: (i, 0)),
        out_shape=jax.ShapeDtypeStruct((n, d), F32),
        compiler_params=_params("parallel"),
        name="outproj_residual",
    )(x2, mod_l, *ys, *ws)


def _chunk_cumsum(x, chunk, reverse):
    rows = x.shape[0]
    pos = lax.broadcasted_iota(jnp.int32, x.shape, 0) % chunk
    s = 1
    while s < chunk:
        if reverse:
            x = x + jnp.where(pos < chunk - s, pltpu.roll(x, rows - s, axis=0), 0.0)
        else:
            x = x + jnp.where(pos >= s, pltpu.roll(x, s, axis=0), 0.0)
        s *= 2
    return x


def _hgrn_groups(slabs):
    g_rows, dk = slabs[0][0].shape
    n_chunks = g_rows // A_CHUNK
    ti = lax.broadcasted_iota(jnp.int32, (A_CHUNK, A_CHUNK), 0)
    si = lax.broadcasted_iota(jnp.int32, (A_CHUNK, A_CHUNK), 1)
    chunk_rows = [slice(c * A_CHUNK, (c + 1) * A_CHUNK) for c in range(n_chunks)]
    prep = []
    for q, f, v, lb, st, reverse in slabs:
        fg = lb + (1.0 - lb) * _sigmoid(f)
        k = 1.0 - fg
        b = _chunk_cumsum(jnp.log(fg), A_CHUNK, reverse)
        b3 = b.reshape(n_chunks, A_CHUNK, dk)
        edge = b3[:, 0:1, :] if reverse else b3[:, A_CHUNK - 1:A_CHUNK, :]
        prep.append(dict(
            q_in=(q * jnp.exp(b)).astype(BF16), k_in=(k * jnp.exp(-b)).astype(BF16),
            k_st=(k.reshape(n_chunks, A_CHUNK, dk) * jnp.exp(edge - b3)).astype(BF16),
            decay=jnp.exp(edge), vb=v.astype(BF16), st=st, reverse=reverse,
            mask=(si >= ti) if reverse else (si <= ti)))
    for p in prep:
        p["scores"] = [_dot_nt(p["q_in"][r], p["k_in"][r]) for r in chunk_rows]
        p["dstate"] = [_dot_tn(p["vb"][r], p["k_st"][c]) for c, r in enumerate(chunk_rows)]
    for p in prep:
        p["intra"] = [_dot(jnp.where(p["mask"], s, 0.0).astype(BF16), p["vb"][r])
                      for s, r in zip(p["scores"], chunk_rows)]
    results = []
    for p in prep:
        st = p["st"]
        outs = [None] * n_chunks
        for c in (range(n_chunks - 1, -1, -1) if p["reverse"] else range(n_chunks)):
            outs[c] = p["intra"][c] + _dot_nt(p["q_in"][chunk_rows[c]], st.astype(BF16))
            st = st * p["decay"][c] + p["dstate"][c]
        results.append((jnp.concatenate(outs, axis=0), st))
    return results


def _hgrn_kernel(q_ref, ff_ref, fb_ref, i_ref, g_ref, lb_ref, ng_ref, o_ref, of_ref, ob_ref):
    seq, dk = q_ref.shape
    n_groups = seq // A_GROUP
    lb_f, lb_b = lb_ref[0:1, :], lb_ref[1:2, :]

    def body(j, carry):
        st_f, st_b = carry
        rf = pl.ds(pl.multiple_of(j * A_GROUP, A_GROUP), A_GROUP)
        rb = pl.ds(pl.multiple_of((n_groups - 1 - j) * A_GROUP, A_GROUP), A_GROUP)
        (o_f, st_f), (o_b, st_b) = _hgrn_groups([
            (_silu(q_ref[rf, :]), ff_ref[rf, :], i_ref[rf, :], lb_f, st_f, False),
            (_silu(q_ref[rb, :]), fb_ref[rb, :], i_ref[rb, :], lb_b, st_b, True)])
        of_ref[rf, :] = o_f
        ob_ref[rb, :] = o_b
        return st_f, st_b

    zero = jnp.zeros((dk, dk), F32)
    lax.fori_loop(0, n_groups, body, (zero, zero))

    def finish(j, carry):
        r = pl.ds(pl.multiple_of(j * A_GROUP, A_GROUP), A_GROUP)
        o = of_ref[r, :] + ob_ref[r, :]
        o = o * lax.rsqrt(jnp.mean(o * o, axis=-1, keepdims=True) + NORM_EPS) * ng_ref[...]
        o_ref[r, :] = (o * _silu(g_ref[r, :])).astype(o_ref.dtype)
        return carry

    lax.fori_loop(0, n_groups, finish, 0)


def _hgrn2(p_a, lb, norm_g, batch, seq):
    n = p_a.shape[0]
    width = p_a.shape[1] // 5
    heads = width // A_HEAD_DIM
    sect = lambda s: pl.BlockSpec((seq, A_HEAD_DIM), lambda b, h, s=s: (b, s * heads + h))
    return pl.pallas_call(
        _hgrn_kernel,
        grid=(batch, heads),
        in_specs=[sect(0), sect(1), sect(2), sect(3), sect(4),
                  pl.BlockSpec((2, A_HEAD_DIM), lambda b, h: (0, h)),
                  pl.BlockSpec((1, A_HEAD_DIM), lambda b, h: (0, 0))],
        out_specs=pl.BlockSpec((seq, A_HEAD_DIM), lambda b, h: (b, h)),
        out_shape=jax.ShapeDtypeStruct((n, width), BF16),
        scratch_shapes=[pltpu.VMEM((seq, A_HEAD_DIM), F32), pltpu.VMEM((seq, A_HEAD_DIM), F32)],
        compiler_params=_params("parallel", "parallel"),
        name="hgrn2",
    )(p_a, p_a, p_a, p_a, p_a, lb, norm_g.reshape(1, A_HEAD_DIM))


def _softplus(z):
    return jnp.maximum(z, 0.0) + jnp.log(1.0 + jnp.exp(-jnp.abs(z)))


def _rwkv_prep_kernel(p_ref, prev_ref, next_ref, mu_ref, w0_ref, w2_ref, a0_ref, a2_ref, g2_ref,
                      kk_ref, ka_ref, rk_ref, ones_ref,
                      r_out, k_out, v_out, kkn_out, b_out, lwf_out, lwb_out, g_out, bonus_out):
    i = pl.program_id(1)
    last = pl.num_programs(1) - 1
    ts = p_ref.shape[0]
    width = a0_ref.shape[1]
    heads = width // B_HEAD_DIM
    p = p_ref[...]
    prev_row = jnp.where(i == 0, 0.0, prev_ref[7:8, :])
    next_row = jnp.where(i == last, 0.0, next_ref[0:1, :])
    row = lax.broadcasted_iota(jnp.int32, (ts, 1), 0)
    up = jnp.where(row == 0, prev_row, pltpu.roll(p, 1, axis=0))
    dn = jnp.where(row == ts - 1, next_row, pltpu.roll(p, ts - 1, axis=0))
    p = p + mu_ref[...] * (0.5 * (up + dn) - p)

    r = p[:, 0:width]
    k = p[:, width:2 * width]
    v = p[:, 2 * width:3 * width]
    o = 3 * width
    wlo_f = p[:, o:o + 64]
    wlo_b = p[:, o + 64:o + 128]
    alo = p[:, o + 128:o + 192]
    glo = p[:, o + 192:o + 320]

    def log_decay(wlo, d):
        z = w0_ref[d:d + 1, :] + _dot(jnp.tanh(wlo).astype(BF16), w2_ref[d])
        w = -_softplus(-z) - 0.5
        return -jnp.exp(w)

    lw_f = log_decay(wlo_f, 0)
    lw_b = log_decay(wlo_b, 1)
    a = _sigmoid(a0_ref[...] + _dot(alo.astype(BF16), a2_ref[...]))
    g = _dot(_sigmoid(glo).astype(BF16), g2_ref[...])
    kk = k * kk_ref[...]
    ones = ones_ref[...]
    head_sum = lambda t: jnp.dot(t, ones, precision=HIGHEST, preferred_element_type=F32)
    kk_n = kk / jnp.maximum(jnp.sqrt(head_sum(kk * kk)), 1e-12)
    k_mod = k * (1.0 + (a - 1.0) * ka_ref[...])
    bonus = head_sum(r * k_mod * rk_ref[...]) * v
    g_out[...] = g
    bonus_out[...] = bonus
    for h in range(heads):
        sl = slice(h * B_HEAD_DIM, (h + 1) * B_HEAD_DIM)
        r_out[0, h] = r[:, sl]
        k_out[0, h] = k_mod[:, sl]
        v_out[0, h] = v[:, sl]
        kkn_out[0, h] = kk_n[:, sl]
        b_out[0, h] = (kk_n * a)[:, sl]
        lwf_out[0, h] = lw_f[:, sl]
        lwb_out[0, h] = lw_b[:, sl]


def _rwkv_prep(p_b, mu, w0, w2, a0, a2, g2, k_k, k_a, r_k, batch, seq, ts=256):
    n, cols = p_b.shape
    width = a0.shape[0]
    heads = width // B_HEAD_DIM
    nblk = seq // ts
    rows8 = ts // 8
    head_id = np.arange(width) // B_HEAD_DIM
    ones = jnp.asarray(head_id[:, None] == head_id[None, :], F32)
    full = lambda a: pl.BlockSpec(a.shape, lambda b, i: (0,) * a.ndim)
    vec = lambda a: a.reshape(1, -1)
    args = [vec(mu), w0, w2.astype(BF16), vec(a0), a2.astype(BF16), g2.astype(BF16),
            vec(k_k), vec(k_a), vec(r_k), ones]
    hm = pl.BlockSpec((1, heads, ts, B_HEAD_DIM), lambda b, i: (b, 0, i, 0))
    tokm = pl.BlockSpec((ts, width), lambda b, i: (b * nblk + i, 0))
    hm_shape = jax.ShapeDtypeStruct((batch, heads, seq, B_HEAD_DIM), F32)
    tok_shape = jax.ShapeDtypeStruct((n, width), F32)
    return pl.pallas_call(
        _rwkv_prep_kernel,
        grid=(batch, nblk),
        in_specs=[
            pl.BlockSpec((ts, cols), lambda b, i: (b * nblk + i, 0)),
            pl.BlockSpec((8, cols), lambda b, i: (jnp.maximum((b * nblk + i) * rows8 - 1, 0), 0)),
            pl.BlockSpec((8, cols), lambda b, i: (jnp.minimum((b * nblk + i + 1) * rows8, n // 8 - 1), 0)),
        ] + [full(a) for a in args],
        out_specs=[hm] * 7 + [tokm] * 2,
        out_shape=[hm_shape] * 7 + [tok_shape] * 2,
        compiler_params=_params("parallel", "parallel"),
        name="rwkv_prep",
    )(p_b, p_b, p_b, *args)


def _rwkv_chunks(chains):
    c, d = chains[0][0].shape
    ti = lax.broadcasted_iota(jnp.int32, (c, c), 0)
    si = lax.broadcasted_iota(jnp.int32, (c, c), 1)
    eye = jnp.where(ti == si, 1.0, 0.0)
    masks = {rev: (((si > ti), (si >= ti)) if rev else ((si < ti), (si <= ti))) for rev in (False, True)}

    st = []
    for r, k, v, kk, b, lw, tt, rev in chains:
        cum = _chunk_cumsum(lw, c, rev)
        grow = jnp.exp(-cum)
        kt = kk * jnp.exp(cum - lw)
        rt = r * jnp.exp(cum)
        b_ = (b * grow).astype(BF16)
        kb = jnp.concatenate([(k * grow).astype(BF16), b_], axis=0)
        st.append(dict(v=v, vb=v.astype(BF16), kt=kt, rt=rt, b_=b_, kb=kb, tt=tt, ttb=tt.astype(BF16),
                       decay=jnp.exp(cum[0:1, :] if rev else cum[c - 1:c, :]), rev=rev,
                       lhs=jnp.concatenate([kt, rt], axis=0).astype(BF16)))
    for s in st:
        s["big"] = _dot_nt(s["lhs"], s["kb"])
    for s in st:
        strict, incl = masks[s["rev"]]
        big = s["big"]
        a_ab = jnp.where(strict, big[:c, c:], 0.0)
        s["a_ak"] = jnp.where(strict, big[:c, :c], 0.0).astype(BF16)
        s["p"] = jnp.concatenate([jnp.where(incl, big[c:, :c], 0.0), jnp.where(incl, -big[c:, c:], 0.0)],
                                 axis=1).astype(BF16)
        s["m"] = eye - a_ab
        s["ab"] = a_ab.astype(BF16)
    for s in st:
        s["aj"] = _dot(s["ab"], s["ab"])
        s["av"] = _dot(s["a_ak"], s["vb"])
    span = 2
    while span < c:
        span *= 2
        for s in st:
            ajb = s["aj"].astype(BF16)
            if span < c:
                both = _dot(jnp.concatenate([s["aj"], s["m"]], axis=0).astype(BF16), ajb)
                s["aj"], s["m"] = both[:c], s["m"] + both[c:]
            else:
                s["m"] = s["m"] + _dot(s["m"].astype(BF16), ajb)
    for s in st:
        s["wu"] = _dot(s["m"].astype(BF16), jnp.concatenate([s["kt"], s["av"]], axis=1).astype(BF16))
    for s in st:
        w_, u0, v = s["wu"][:, :d], s["wu"][:, d:], s["v"]
        rhs = jnp.concatenate([jnp.concatenate([v, jnp.zeros_like(v)], axis=1),
                               jnp.concatenate([u0, w_], axis=1)], axis=0).astype(BF16)
        s["yq"] = _dot(s["p"], rhs)
        s["h_t"] = _dot_tn(jnp.concatenate([v, -u0], axis=0).astype(BF16), s["kb"])
        s["bw"] = _dot_tn(s["b_"], w_.astype(BF16))
    out = []
    for s in st:
        q = s["rt"] + s["yq"][:, d:]
        y = _dot_nt(q.astype(BF16), s["ttb"]) + s["yq"][:, :d]
        tt = (s["tt"] - _dot_nt(s["ttb"], s["bw"].astype(BF16)) + s["h_t"]) * s["decay"]
        out.append((y, tt))
    return out


def _rwkv_scan_kernel(rf, kf, vf, kkf, bf, lwf, rb, kb, vb, kkb, bb, lwb, yf_ref, yb_ref, tf_ref, tb_ref):
    heads, ts = rf.shape[1], rf.shape[2]
    n_chunks = ts // B_CHUNK

    @pl.when(pl.program_id(1) == 0)
    def _():
        tf_ref[...] = jnp.zeros_like(tf_ref)
        tb_ref[...] = jnp.zeros_like(tb_ref)

    def body(j, carry):
        rows_f = pl.ds(pl.multiple_of(j * B_CHUNK, B_CHUNK), B_CHUNK)
        rows_b = pl.ds(pl.multiple_of((n_chunks - 1 - j) * B_CHUNK, B_CHUNK), B_CHUNK)
        chains = []
        for h in range(heads):
            chains.append([ref[0, h, rows_f, :] for ref in (rf, kf, vf, kkf, bf, lwf)] + [tf_ref[h], False])
            chains.append([ref[0, h, rows_b, :] for ref in (rb, kb, vb, kkb, bb, lwb)] + [tb_ref[h], True])
        res = _rwkv_chunks(chains)
        for h in range(heads):
            yf_ref[0, h, rows_f, :], tf_ref[h] = res[2 * h]
            yb_ref[0, h, rows_b, :], tb_ref[h] = res[2 * h + 1]
        return carry

    lax.fori_loop(0, n_chunks, body, 0)


def _rwkv_scan(r, k, v, kkn, b, lw_f, lw_b, ts=256):
    batch, heads, seq, d = r.shape
    nblk = seq // ts
    fwd = pl.BlockSpec((1, heads, ts, d), lambda bi, i: (bi, 0, i, 0))
    bwd = pl.BlockSpec((1, heads, ts, d), lambda bi, i: (bi, 0, nblk - 1 - i, 0))
    shape = jax.ShapeDtypeStruct(r.shape, F32)
    return pl.pallas_call(
        _rwkv_scan_kernel,
        grid=(batch, nblk),
        in_specs=[fwd] * 6 + [bwd] * 6,
        out_specs=[fwd, bwd],
        out_shape=[shape, shape],
        scratch_shapes=[pltpu.VMEM((heads, d, d), F32), pltpu.VMEM((heads, d, d), F32)],
        compiler_params=_params("parallel", "arbitrary"),
        name="rwkv_scan",
    )(r, k, v, kkn, b, lw_f, r, k, v, kkn, b, lw_b)


def _rwkv_post_kernel(yf_ref, yb_ref, g_ref, bonus_ref, lng_ref, lnb_ref, o_ref):
    heads = yf_ref.shape[1]
    outs = []
    for h in range(heads):
        y = yf_ref[0, h] + yb_ref[0, h]
        mean = jnp.mean(y, axis=-1, keepdims=True)
        var = jnp.mean(jnp.square(y - mean), axis=-1, keepdims=True)
        outs.append((y - mean) * lax.rsqrt(var + B_LN_EPS))
    yn = jnp.concatenate(outs, axis=1) * lng_ref[...] + lnb_ref[...]
    o_ref[...] = ((yn + bonus_ref[...]) * g_ref[...]).astype(o_ref.dtype)


def _rwkv_post(y_f, y_b, g, bonus, ln_g, ln_b, ts=256):
    batch, heads, seq, d = y_f.shape
    n, width = g.shape
    nblk = seq // ts
    hm = pl.BlockSpec((1, heads, ts, d), lambda b, i: (b, 0, i, 0))
    tokm = pl.BlockSpec((ts, width), lambda b, i: (b * nblk + i, 0))
    vec = pl.BlockSpec((1, width), lambda b, i: (0, 0))
    return pl.pallas_call(
        _rwkv_post_kernel,
        grid=(batch, nblk),
        in_specs=[hm, hm, tokm, tokm, vec, vec],
        out_specs=tokm,
        out_shape=jax.ShapeDtypeStruct((n, width), BF16),
        compiler_params=_params("parallel", "parallel"),
        name="rwkv_post",
    )(y_f, y_b, g, bonus, ln_g.reshape(1, width), ln_b.reshape(1, width))


def _rwkv7(p_b, mu, w0, w2, a0, a2, g2, k_k, k_a, r_k, ln_g, ln_b, batch, seq):
    r, k, v, kkn, b, lw_f, lw_b, g, bonus = _rwkv_prep(p_b, mu, w0, w2, a0, a2, g2, k_k, k_a, r_k, batch, seq)
    y_f, y_b = _rwkv_scan(r, k, v, kkn, b, lw_f, lw_b)
    return _rwkv_post(y_f, y_b, g, bonus, ln_g, ln_b)


_BUCKET_EDGES = (0, 1, 2, 3, 4, 5, 6, 7, 8, 12, 16, 23, 32, 46, 64, 91)
ATTN_TK = 128


def _bias_band_kernel(table_ref, o_ref):
    h = pl.program_id(0)
    n_d, tq, tk = o_ref.shape[1:]
    half = REL_BUCKETS // 2
    r = lax.broadcasted_iota(jnp.int32, (tq, tk), 0)
    c = lax.broadcasted_iota(jnp.int32, (tq, tk), 1)
    for d in range(n_d):
        rel = tk * (d - 1) + c - r
        n = jnp.abs(rel)
        vals = []
        for sign in range(2):
            val = jnp.full((tq, tk), table_ref[sign * half + half - 1, h], F32)
            for bkt in range(half - 2, -1, -1):
                val = jnp.where(n < _BUCKET_EDGES[bkt + 1], table_ref[sign * half + bkt, h], val)
            vals.append(val)
        o_ref[0, d] = jnp.where(rel > 0, vals[1], vals[0])


def _bias_band(rel_table, tq):
    heads = rel_table.shape[1]
    n_d = tq // ATTN_TK + 2
    return pl.pallas_call(
        _bias_band_kernel,
        grid=(heads,),
        in_specs=[pl.BlockSpec(memory_space=pltpu.SMEM)],
        out_specs=pl.BlockSpec((1, n_d, tq, ATTN_TK), lambda h: (h, 0, 0, 0)),
        out_shape=jax.ShapeDtypeStruct((heads, n_d, tq, ATTN_TK), F32),
        compiler_params=_params("parallel"),
        name="bias_band",
    )(rel_table)


def _diff_attn_kernel(table_ref, lam_ref, q_ref, k_ref, v_ref, band_ref, g_ref, o_ref, *, lam_init):
    h = pl.program_id(1)
    i = pl.program_id(2)
    tq, dh2 = q_ref.shape
    seq = k_ref.shape[0]
    n_d = band_ref.shape[1]
    ratio = tq // ATTN_TK
    half = REL_BUCKETS // 2
    q = q_ref[...]
    k = k_ref[...]
    lane = lax.broadcasted_iota(jnp.int32, (1, dh2), 1)
    zero = jnp.zeros_like(q)
    far_neg = table_ref[half - 1, h]
    far_pos = table_ref[REL_BUCKETS - 1, h]
    tiles = []
    for j in range(seq // ATTN_TK):
        off = j - ratio * i
        near = band_ref[0, jnp.clip(off + 1, 0, n_d - 1)]
        tiles.append(jnp.where(off < -1, far_neg, jnp.where(off > ratio, far_pos, near)))
    bias = jnp.concatenate(tiles, axis=1)
    q = q * jnp.asarray(C_HEAD_DIM ** -0.5, q.dtype)
    v = v_ref[...]
    parts = []
    for m in range(2):
        qm = jnp.where((lane >= m * C_HEAD_DIM) & (lane < (m + 1) * C_HEAD_DIM), q, zero)
        logits = _dot_nt(qm, k) + bias
        e = jnp.exp(logits - jnp.max(logits, axis=-1, keepdims=True))
        parts.append(_dot(e.astype(BF16), v) * (1.0 / jnp.sum(e, axis=-1, keepdims=True)))
    o = parts[0] - lam_ref[0] * parts[1]
    o = o * lax.rsqrt(jnp.mean(o * o, axis=-1, keepdims=True) + NORM_EPS) * g_ref[...]
    o_ref[...] = (o * (1.0 - lam_init)).astype(o_ref.dtype)


def _diff_attention(qkv, rel_table, lam_full, subln_g, lam_init, batch, seq, tq=256):
    n = qkv.shape[0]
    dh2 = 2 * C_HEAD_DIM
    heads = qkv.shape[1] // (3 * dh2)
    nq = seq // tq
    band = _bias_band(rel_table, tq)
    return pl.pallas_call(
        functools.partial(_diff_attn_kernel, lam_init=lam_init),
        grid=(batch, heads, nq),
        in_specs=[
            pl.BlockSpec(memory_space=pltpu.SMEM),
            pl.BlockSpec(memory_space=pltpu.SMEM),
            pl.BlockSpec((tq, dh2), lambda b, h, i: (b * nq + i, h)),
            pl.BlockSpec((seq, dh2), lambda b, h, i: (b, heads + h)),
            pl.BlockSpec((seq, dh2), lambda b, h, i: (b, 2 * heads + h)),
            pl.BlockSpec((1,) + band.shape[1:], lambda b, h, i: (h, 0, 0, 0)),
            pl.BlockSpec((1, dh2), lambda b, h, i: (0, 0)),
        ],
        out_specs=pl.BlockSpec((tq, dh2), lambda b, h, i: (b * nq + i, h)),
        out_shape=jax.ShapeDtypeStruct((n, heads * dh2), BF16),
        compiler_params=_params("parallel", "parallel", "parallel"),
        name="diff_attention",
    )(rel_table, lam_full.reshape(1), qkv, qkv, qkv, band, subln_g.reshape(1, dh2))


MOE_TILE = 256
_HI_MASK = np.uint32(0xFFFF0000)


def _pack_bf16_pair(lo, hi):
    as_bits = lambda t: lax.bitcast_convert_type(t.astype(BF16).astype(F32), jnp.uint32)
    return (as_bits(hi) & _HI_MASK) | (as_bits(lo) >> 16)


def _unpack_bf16_pair(word):
    lo = lax.bitcast_convert_type(word << 16, F32).astype(BF16)
    hi = lax.bitcast_convert_type(word & _HI_MASK, F32).astype(BF16)
    return lo, hi


def _router_kernel(x_ref, g_ref, sh_ref, sc_ref, rw_ref, rb_ref, hp_ref, idx_ref, w_ref, rank_ref, cnt_ref,
                   run_ref, *, n_experts):
    @pl.when(pl.program_id(0) == 0)
    def _():
        run_ref[...] = jnp.zeros_like(run_ref)

    tm, d = x_ref.shape
    h = _modulated_norm(x_ref[...], g_ref[...], sh_ref[0], sc_ref[0])
    hp_ref[...] = _pack_bf16_pair(h[:, :d // 2], h[:, d // 2:])
    logits = jnp.dot(h, rw_ref[...], precision=HIGHEST, preferred_element_type=F32) + rb_ref[...]
    lane = lax.broadcasted_iota(jnp.int32, logits.shape, 1).astype(F32)
    neg = jnp.float32(-jnp.inf)
    logits = jnp.where(lane < n_experts, logits, neg)
    picks, vals = [], []
    for _ in range(TOP_K):
        m = jnp.max(logits, axis=-1, keepdims=True)
        first = jnp.min(jnp.where(logits == m, lane, float(LANES)), axis=-1, keepdims=True)
        hit = lane == first
        picks.append((first, hit))
        vals.append(m)
        logits = jnp.where(hit, neg, logits)
    es = [jnp.exp(v - vals[0]) for v in vals]
    inv = 1.0 / (es[0] + es[1] + es[2] + es[3])
    assigned = picks[0][1] | picks[1][1] | picks[2][1] | picks[3][1]
    onehot = jnp.where(assigned, 1.0, 0.0)
    ri = lax.broadcasted_iota(jnp.int32, (tm, tm), 0)
    ci = lax.broadcasted_iota(jnp.int32, (tm, tm), 1)
    earlier = jnp.where(ci < ri, 1.0, 0.0).astype(BF16)
    before = _dot(earlier, onehot.astype(BF16)) + run_ref[...]
    idx_o = jnp.zeros(logits.shape, F32)
    w_o = jnp.zeros(logits.shape, F32)
    rank_o = jnp.zeros(logits.shape, F32)
    for j, (first, hit) in enumerate(picks):
        rank_j = jnp.sum(jnp.where(hit, before, 0.0), axis=-1, keepdims=True)
        idx_o = jnp.where(lane == j, first, idx_o)
        w_o = jnp.where(lane == j, es[j] * inv, w_o)
        rank_o = jnp.where(lane == j, rank_j, rank_o)
    idx_ref[...] = idx_o
    w_ref[...] = w_o
    rank_ref[...] = rank_o
    run_ref[...] += jnp.sum(onehot, axis=0, keepdims=True)
    cnt_ref[...] = run_ref[...]


def _router(x2, g, mod_l, shift_idx, seq, router_w, router_b):
    n, d = x2.shape
    tm = MOE_TILE
    n_experts = router_w.shape[1]
    per_batch = seq // tm
    rw = jnp.zeros((d, LANES), F32).at[:, :n_experts].set(router_w)
    rb = jnp.zeros((1, LANES), F32).at[0, :n_experts].set(router_b)
    lane_out = pl.BlockSpec((tm, LANES), lambda i: (i, 0))
    lane_shape = jax.ShapeDtypeStruct((n, LANES), F32)
    return pl.pallas_call(
        functools.partial(_router_kernel, n_experts=n_experts),
        grid=(n // tm,),
        in_specs=[
            pl.BlockSpec((tm, d), lambda i: (i, 0)),
            pl.BlockSpec((1, d), lambda i: (0, 0)),
            pl.BlockSpec((1, 1, d), lambda i: (i // per_batch, 0, shift_idx)),
            pl.BlockSpec((1, 1, d), lambda i: (i // per_batch, 0, shift_idx + 1)),
            pl.BlockSpec((d, LANES), lambda i: (0, 0)),
            pl.BlockSpec((1, LANES), lambda i: (0, 0)),
        ],
        out_specs=[pl.BlockSpec((tm, d // 2), lambda i: (i, 0)), lane_out, lane_out, lane_out,
                   pl.BlockSpec((1, LANES), lambda i: (0, 0))],
        out_shape=[jax.ShapeDtypeStruct((n, d // 2), jnp.uint32), lane_shape, lane_shape, lane_shape,
                   jax.ShapeDtypeStruct((1, LANES), F32)],
        scratch_shapes=[pltpu.VMEM((1, LANES), F32)],
        compiler_params=_params("arbitrary"),
        name="moe_router",
    )(x2, g.reshape(1, d), mod_l, mod_l, rw, rb)


def _dispatch_kernel(pos_ref, h_ref, zeros_ref, xs_ref, sem):
    del zeros_ref
    tm = h_ref.shape[0]

    def row_copy(r, p):
        return pltpu.make_async_copy(h_ref.at[pl.ds(r, 1)], xs_ref.at[pl.ds(p, 1)], sem)

    def issue(r, carry):
        for j in range(TOP_K):
            row_copy(r, pos_ref[r * TOP_K + j]).start()
        return carry

    lax.fori_loop(0, tm, issue, 0)

    def drain(r, carry):
        for j in range(TOP_K):
            row_copy(r, pos_ref[r * TOP_K + j]).wait()
        return carry

    lax.fori_loop(0, tm, drain, 0)


def _dispatch(hp, pos_flat, n_rows):
    n, dw = hp.shape
    tm = MOE_TILE
    zeros = jnp.zeros((n_rows, dw), jnp.uint32)
    return pl.pallas_call(
        _dispatch_kernel,
        grid=(n // tm,),
        in_specs=[
            pl.BlockSpec((tm * TOP_K,), lambda i: (i,), memory_space=pltpu.SMEM),
            pl.BlockSpec((tm, dw), lambda i: (i, 0)),
            pl.BlockSpec(memory_space=pl.ANY),
        ],
        out_specs=pl.BlockSpec(memory_space=pl.ANY),
        out_shape=jax.ShapeDtypeStruct((n_rows, dw), jnp.uint32),
        scratch_shapes=[pltpu.SemaphoreType.DMA(())],
        input_output_aliases={2: 0},
        compiler_params=_params("arbitrary"),
        name="moe_dispatch",
    )(pos_flat, hp, zeros)


GLU_TILE = 2 * LANES


def _experts_kernel(te_ref, nu_ref, xs_ref, w1_ref, b1_ref, w2_ref, b2_ref, perm_ref, ys_ref, w1p_ref, w2b_ref):
    t = pl.program_id(0)
    used = t < nu_ref[0]
    new_expert = (t == 0) | (te_ref[t] != te_ref[jnp.maximum(t - 1, 0)])
    n_col_tiles = w1_ref.shape[2] // GLU_TILE

    @pl.when(used & new_expert)
    def _():
        for c in range(n_col_tiles):
            cols = slice(c * GLU_TILE, (c + 1) * GLU_TILE)
            w1p_ref[:, cols] = _dot(w1_ref[0, :, cols].astype(BF16), perm_ref[...]).astype(BF16)
        w2b_ref[...] = w2_ref[0].astype(BF16)

    @pl.when(used)
    def _():
        lo, hi = _unpack_bf16_pair(xs_ref[...])
        x = jnp.concatenate([lo, hi], axis=1)
        hh = _dot(x, w1p_ref[...]) + b1_ref[0]
        acts = []
        for c in range(n_col_tiles):
            x_glu = jnp.minimum(hh[:, c * GLU_TILE:c * GLU_TILE + LANES], SWIGLU_LIMIT)
            x_lin = jnp.clip(hh[:, c * GLU_TILE + LANES:(c + 1) * GLU_TILE], -SWIGLU_LIMIT, SWIGLU_LIMIT)
            acts.append((x_glu * _sigmoid(SWIGLU_ALPHA * x_glu) * (x_lin + 1.0)).astype(BF16))
        out = _dot(jnp.concatenate(acts, axis=1), w2b_ref[...]) + b2_ref[0]
        half = out.shape[1] // 2
        ys_ref[...] = _pack_bf16_pair(out[:, :half], out[:, half:])

    @pl.when(t >= nu_ref[0])
    def _():
        ys_ref[...] = jnp.zeros_like(ys_ref)


def _experts(xs, tile_expert, n_used, w1, b1, w2, b2):
    n_rows, dw = xs.shape
    tm = MOE_TILE
    n_exp, d, de2 = w1.shape
    src = np.arange(GLU_TILE)
    dst = np.where(src % 2 == 0, src // 2, LANES + src // 2)
    perm = jnp.asarray(dst[:, None] == np.arange(GLU_TILE)[None, :], BF16)
    b1p = b1.reshape(n_exp, de2 // GLU_TILE, LANES, 2).transpose(0, 1, 3, 2).reshape(n_exp, 1, de2)
    row = lambda t, te, nu: (jnp.minimum(t, nu[0] - 1), 0)
    exp3 = lambda t, te, nu: (te[t], 0, 0)
    grid_spec = pltpu.PrefetchScalarGridSpec(
        num_scalar_prefetch=2,
        grid=(n_rows // tm,),
        in_specs=[
            pl.BlockSpec((tm, dw), row),
            pl.BlockSpec((1, d, de2), exp3),
            pl.BlockSpec((1, 1, de2), exp3),
            pl.BlockSpec((1, de2 // 2, d), exp3),
            pl.BlockSpec((1, 1, d), exp3),
            pl.BlockSpec((GLU_TILE, GLU_TILE), lambda t, te, nu: (0, 0)),
        ],
        out_specs=pl.BlockSpec((tm, d // 2), lambda t, te, nu: (t, 0)),
        scratch_shapes=[pltpu.VMEM((d, de2), BF16), pltpu.VMEM((de2 // 2, d), BF16)],
    )
    return pl.pallas_call(
        _experts_kernel,
        grid_spec=grid_spec,
        out_shape=jax.ShapeDtypeStruct((n_rows, d // 2), jnp.uint32),
        compiler_params=_params("arbitrary"),
        name="moe_experts",
    )(tile_expert, n_used, xs, w1, b1p, w2, b2.reshape(n_exp, 1, d), perm)


def _combine_kernel(pos_ref, x_ref, gate_ref, w_ref, fg_ref, ys_ref, o_ref, buf, sem, *, final_norm):
    tm = x_ref.shape[0]

    def row_copy(r, j):
        return pltpu.make_async_copy(ys_ref.at[pl.ds(pos_ref[r * TOP_K + j], 1)], buf.at[j, pl.ds(r, 1)], sem)

    def issue(r, carry):
        for j in range(TOP_K):
            row_copy(r, j).start()
        return carry

    lax.fori_loop(0, tm, issue, 0)

    def drain(r, carry):
        for j in range(TOP_K):
            row_copy(r, j).wait()
        return carry

    lax.fori_loop(0, tm, drain, 0)
    w = w_ref[...]
    y_lo = y_hi = None
    for j in range(TOP_K):
        lo, hi = _unpack_bf16_pair(buf[j])
        wj = w[:, j:j + 1]
        y_lo = wj * lo if j == 0 else y_lo + wj * lo
        y_hi = wj * hi if j == 0 else y_hi + wj * hi
    y = jnp.concatenate([y_lo, y_hi], axis=1)
    out = x_ref[...] + gate_ref[0] * y
    if final_norm:
        out = out * lax.rsqrt(jnp.mean(out * out, axis=-1, keepdims=True) + NORM_EPS) * fg_ref[...]
    o_ref[...] = out


def _combine(x2, mod_l, gate_idx, seq, top_w, pos_flat, ys, final_g, final_norm):
    n, d = x2.shape
    tm = MOE_TILE
    per_batch = seq // tm
    return pl.pallas_call(
        functools.partial(_combine_kernel, final_norm=final_norm),
        grid=(n // tm,),
        in_specs=[
            pl.BlockSpec((tm * TOP_K,), lambda i: (i,), memory_space=pltpu.SMEM),
            pl.BlockSpec((tm, d), lambda i: (i, 0)),
            pl.BlockSpec((1, 1, d), lambda i: (i // per_batch, 0, gate_idx)),
            pl.BlockSpec((tm, LANES), lambda i: (i, 0)),
            pl.BlockSpec((1, d), lambda i: (0, 0)),
            pl.BlockSpec(memory_space=pl.ANY),
        ],
        out_specs=pl.BlockSpec((tm, d), lambda i: (i, 0)),
        out_shape=jax.ShapeDtypeStruct((n, d), F32),
        scratch_shapes=[pltpu.VMEM((TOP_K, tm, d // 2), jnp.uint32), pltpu.SemaphoreType.DMA(())],
        compiler_params=_params("arbitrary"),
        name="moe_combine",
    )(pos_flat, x2, mod_l, top_w, final_g.reshape(1, d), ys)


def _moe_ffn(x2, norm_g, mod_l, seq, router_w, router_b, w1, b1, w2, b2, final_g, final_norm):
    n, d = x2.shape
    n_exp = router_w.shape[1]
    tile = MOE_TILE
    hp, idx_f, top_w, rank_f, counts = _router(x2, norm_g, mod_l, 3, seq, router_w, router_b)
    counts = counts[0, :n_exp].astype(jnp.int32)
    padded = (counts + tile - 1) // tile * tile
    ends = jnp.cumsum(padded)
    starts = ends - padded
    idx = idx_f[:, :TOP_K].astype(jnp.int32)
    expert_ids = jnp.arange(n_exp, dtype=jnp.int32)
    start_of = jnp.sum(jnp.where(idx[..., None] == expert_ids, starts, 0), axis=-1)
    pos_flat = (start_of + rank_f[:, :TOP_K].astype(jnp.int32)).reshape(-1)
    n_rows = n * TOP_K + n_exp * tile
    n_tiles = n_rows // tile
    tile_start = jnp.arange(n_tiles, dtype=jnp.int32) * tile
    tile_expert = jnp.sum((ends[None, :] <= tile_start[:, None]).astype(jnp.int32), axis=1)
    tile_expert = jnp.minimum(tile_expert, n_exp - 1)
    n_used = (ends[-1:] // tile).astype(jnp.int32)
    xs = _dispatch(hp, pos_flat, n_rows)
    ys = _experts(xs, tile_expert, n_used, w1, b1, w2, b2)
    return _combine(x2, mod_l, 5, seq, top_w, pos_flat, ys, final_g, final_norm)


def kernel(x, c, ada_w, ada_b, norm_mix_g, norm_ffn_g, router_w, router_b, moe_w1, moe_b1, moe_w2, moe_b2, ab_w_in, ab_w_out, hgrn_lb, hgrn_norm_g, rwkv_mu, rwkv_w0, rwkv_w2, rwkv_a0, rwkv_a2, rwkv_g2, rwkv_k_k, rwkv_k_a, rwkv_r_k, rwkv_ln_g, rwkv_ln_b, attn_w_in, attn_w_out, attn_lambda, attn_subln_g, rel_bias_table, final_norm_g):
    batch, seq, d = x.shape
    n = batch * seq
    depth = ada_w.shape[0]
    x2 = x.reshape(n, d)
    mod = _adaln(c, ada_w, ada_b)
    lb_all = jnp.cumsum(jax.nn.softmax(hgrn_lb.astype(F32), axis=1), axis=1)
    for layer in range(depth):
        mod_l = mod[layer].reshape(batch, 1, 6 * d)
        j = layer // 2
        if layer % 2 == 0:
            a_cols = 5 * (d // 2)
            w_in = ab_w_in[j].astype(BF16)
            p_a, p_b = _normmod_proj(x2, norm_mix_g[layer], mod_l, 0, seq,
                                     [w_in[:, :a_cols], w_in[:, a_cols:]], [F32, F32])
            y_a = _hgrn2(p_a, lb_all[:, j], hgrn_norm_g[j], batch, seq)
            y_b = _rwkv7(p_b, rwkv_mu[j], rwkv_w0[j], rwkv_w2[j], rwkv_a0[j], rwkv_a2[j], rwkv_g2[j],
                         rwkv_k_k[j], rwkv_k_a[j], rwkv_r_k[j], rwkv_ln_g[j], rwkv_ln_b[j], batch, seq)
            w_out = ab_w_out[j].astype(BF16)
            x2 = _outproj_residual(x2, mod_l, 2, seq, [y_a, y_b], [w_out[:d // 2], w_out[d // 2:]])
        else:
            (qkv,) = _normmod_proj(x2, norm_mix_g[layer], mod_l, 0, seq, [attn_w_in[j].astype(BF16)], [BF16])
            lam = attn_lambda[j].astype(F32)
            lam_init = 0.8 - 0.6 * math.exp(-0.3 * layer)
            lam_full = jnp.exp(jnp.sum(lam[0] * lam[1])) - jnp.exp(jnp.sum(lam[2] * lam[3])) + lam_init
            o = _diff_attention(qkv, rel_bias_table, lam_full, attn_subln_g[j], lam_init, batch, seq)
            x2 = _outproj_residual(x2, mod_l, 2, seq, [o], [attn_w_out[j].astype(BF16)])
        x2 = _moe_ffn(x2, norm_ffn_g[layer], mod_l, seq, router_w[layer], router_b[layer], moe_w1[layer],
                      moe_b1[layer], moe_w2[layer], moe_b2[layer], final_norm_g, layer == depth - 1)
    return x2.reshape(batch, seq, d)
```

```python
import functools
import math

import jax
import jax.numpy as jnp
import numpy as np
from jax import lax
from jax.experimental import pallas as pl
from jax.experimental.pallas import tpu as pltpu

F32 = jnp.float32
BF16 = jnp.bfloat16
HIGHEST = lax.Precision.HIGHEST

NORM_EPS = 1e-6

A_HEAD_DIM = 128
A_CHUNK = 32
A_GROUP = 128
B_HEAD_DIM = 64
B_CHUNK = 64
B_LN_EPS = 1e-5 * B_HEAD_DIM
C_HEAD_DIM = 64
REL_BUCKETS = 32
REL_MAX_DISTANCE = 128
TOP_K = 4
SWIGLU_LIMIT = 7.0
SWIGLU_ALPHA = 1.702

V7X_VMEM_BYTES = 64 * 1024 * 1024
VMEM_LIMIT = V7X_VMEM_BYTES - 8 * 1024 * 1024
LANES = 128


def _params(*sem):
    return pltpu.CompilerParams(dimension_semantics=sem, vmem_limit_bytes=VMEM_LIMIT)


def _sigmoid(x):
    return 1.0 / (1.0 + jnp.exp(-x))


def _silu(x):
    return x * _sigmoid(x)


def _dot(a, b):
    return jnp.dot(a, b, preferred_element_type=F32)


def _dot_nt(a, b):
    return lax.dot_general(a, b, (((1,), (1,)), ((), ())), preferred_element_type=F32)


def _dot_tn(a, b):
    return lax.dot_general(a, b, (((0,), (0,)), ((), ())), preferred_element_type=F32)


def _adaln_kernel(c_ref, w_ref, b_ref, o_ref):
    cond = _silu(c_ref[...])
    o_ref[0] = jnp.dot(cond, w_ref[0], precision=HIGHEST, preferred_element_type=F32) + b_ref[0]


def _adaln(c, ada_w, ada_b):
    n_layers, d, n_out = ada_w.shape
    batch = c.shape[0]
    tn = 1536
    return pl.pallas_call(
        _adaln_kernel,
        grid=(n_layers, n_out // tn),
        in_specs=[
            pl.BlockSpec((batch, d), lambda l, j: (0, 0)),
            pl.BlockSpec((1, d, tn), lambda l, j: (l, 0, j)),
            pl.BlockSpec((1, 1, tn), lambda l, j: (l, 0, j)),
        ],
        out_specs=pl.BlockSpec((1, batch, tn), lambda l, j: (l, 0, j)),
        out_shape=jax.ShapeDtypeStruct((n_layers, batch, n_out), F32),
        compiler_params=_params("parallel", "parallel"),
        name="adaln",
    )(c, ada_w, ada_b.reshape(n_layers, 1, n_out))


def _modulated_norm(x, g, shift, scale):
    y = x * lax.rsqrt(jnp.mean(x * x, axis=-1, keepdims=True) + NORM_EPS)
    return (y * g) * (1.0 + scale) + shift


def _normmod_proj_kernel(x_ref, g_ref, sh_ref, sc_ref, *rest, n_w):
    w_refs, o_refs = rest[:n_w], rest[n_w:]
    h = _modulated_norm(x_ref[...], g_ref[...], sh_ref[0], sc_ref[0]).astype(BF16)
    for w_ref, o_ref in zip(w_refs, o_refs):
        o_ref[...] = _dot(h, w_ref[...]).astype(o_ref.dtype)


def _normmod_proj(x2, g, mod_l, shift_idx, seq, weights, out_dtypes, tm=256):
    n, d = x2.shape
    per_batch = seq // tm
    in_specs = [
        pl.BlockSpec((tm, d), lambda i: (i, 0)),
        pl.BlockSpec((1, d), lambda i: (0, 0)),
        pl.BlockSpec((1, 1, d), lambda i: (i // per_batch, 0, shift_idx)),
        pl.BlockSpec((1, 1, d), lambda i: (i // per_batch, 0, shift_idx + 1)),
    ]
    in_specs += [pl.BlockSpec(w.shape, lambda i: (0, 0)) for w in weights]
    out_specs = [pl.BlockSpec((tm, w.shape[1]), lambda i: (i, 0)) for w in weights]
    out_shape = [jax.ShapeDtypeStruct((n, w.shape[1]), dt) for w, dt in zip(weights, out_dtypes)]
    return pl.pallas_call(
        functools.partial(_normmod_proj_kernel, n_w=len(weights)),
        grid=(n // tm,),
        in_specs=in_specs,
        out_specs=out_specs,
        out_shape=out_shape,
        compiler_params=_params("parallel"),
        name="normmod_proj",
    )(x2, g.reshape(1, d), mod_l, mod_l, *weights)


def _outproj_kernel(x_ref, gate_ref, *rest, n_y):
    y_refs, w_refs, o_ref = rest[:n_y], rest[n_y:2 * n_y], rest[2 * n_y]
    acc = _dot(y_refs[0][...], w_refs[0][...])
    for y_ref, w_ref in zip(y_refs[1:], w_refs[1:]):
        acc += _dot(y_ref[...], w_ref[...])
    o_ref[...] = x_ref[...] + gate_ref[0] * acc


def _outproj_residual(x2, mod_l, gate_idx, seq, ys, ws, tm=512):
    n, d = x2.shape
    per_batch = seq // tm
    in_specs = [
        pl.BlockSpec((tm, d), lambda i: (i, 0)),
        pl.BlockSpec((1, 1, d), lambda i: (i // per_batch, 0, gate_idx)),
    ]
    in_specs += [pl.BlockSpec((tm, y.shape[1]), lambda i: (i, 0)) for y in ys]
    in_specs += [pl.BlockSpec(w.shape, lambda i: (0, 0)) for w in ws]
    return pl.pallas_call(
        functools.partial(_outproj_kernel, n_y=len(ys)),
        grid=(n // tm,),
        in_specs=in_specs,
        out_specs=pl.BlockSpec((tm, d), lambda i: (i, 0)),
        out_shape=jax.ShapeDtypeStruct((n, d), F32),
        compiler_params=_params("parallel"),
        name="outproj_residual",
    )(x2, mod_l, *ys, *ws)


def _chunk_cumsum(x, chunk, reverse):
    rows = x.shape[0]
    pos = lax.broadcasted_iota(jnp.int32, x.shape, 0) % chunk
    s = 1
    while s < chunk:
        if reverse:
            x = x + jnp.where(pos < chunk - s, pltpu.roll(x, rows - s, axis=0), 0.0)
        else:
            x = x + jnp.where(pos >= s, pltpu.roll(x, s, axis=0), 0.0)
        s *= 2
    return x


def _hgrn_groups(slabs):
    g_rows, dk = slabs[0][0].shape
    n_chunks = g_rows // A_CHUNK
    ti = lax.broadcasted_iota(jnp.int32, (A_CHUNK, A_CHUNK), 0)
    si = lax.broadcasted_iota(jnp.int32, (A_CHUNK, A_CHUNK), 1)
    chunk_rows = [slice(c * A_CHUNK, (c + 1) * A_CHUNK) for c in range(n_chunks)]
    prep = []
    for q, f, v, lb, st, reverse in slabs:
        fg = lb + (1.0 - lb) * _sigmoid(f)
        k = 1.0 - fg
        b = _chunk_cumsum(jnp.log(fg), A_CHUNK, reverse)
        b3 = b.reshape(n_chunks, A_CHUNK, dk)
        edge = b3[:, 0:1, :] if reverse else b3[:, A_CHUNK - 1:A_CHUNK, :]
        prep.append(dict(
            q_in=(q * jnp.exp(b)).astype(BF16), k_in=(k * jnp.exp(-b)).astype(BF16),
            k_st=(k.reshape(n_chunks, A_CHUNK, dk) * jnp.exp(edge - b3)).astype(BF16),
            decay=jnp.exp(edge), vb=v.astype(BF16), st=st, reverse=reverse,
            mask=(si >= ti) if reverse else (si <= ti)))
    for p in prep:
        p["scores"] = [_dot_nt(p["q_in"][r], p["k_in"][r]) for r in chunk_rows]
        p["dstate"] = [_dot_tn(p["vb"][r], p["k_st"][c]) for c, r in enumerate(chunk_rows)]
    for p in prep:
        p["intra"] = [_dot(jnp.where(p["mask"], s, 0.0).astype(BF16), p["vb"][r])
                      for s, r in zip(p["scores"], chunk_rows)]
    results = []
    for p in prep:
        st = p["st"]
        outs = [None] * n_chunks
        for c in (range(n_chunks - 1, -1, -1) if p["reverse"] else range(n_chunks)):
            outs[c] = p["intra"][c] + _dot_nt(p["q_in"][chunk_rows[c]], st.astype(BF16))
            st = st * p["decay"][c] + p["dstate"][c]
        results.append((jnp.concatenate(outs, axis=0), st))
    return results


def _hgrn_kernel(q_ref, ff_ref, fb_ref, i_ref, g_ref, lb_ref, ng_ref, o_ref, of_ref, ob_ref):
    seq, dk = q_ref.shape
    n_groups = seq // A_GROUP
    lb_f, lb_b = lb_ref[0:1, :], lb_ref[1:2, :]

    def body(j, carry):
        st_f, st_b = carry
        rf = pl.ds(pl.multiple_of(j * A_GROUP, A_GROUP), A_GROUP)
        rb = pl.ds(pl.multiple_of((n_groups - 1 - j) * A_GROUP, A_GROUP), A_GROUP)
        (o_f, st_f), (o_b, st_b) = _hgrn_groups([
            (_silu(q_ref[rf, :]), ff_ref[rf, :], i_ref[rf, :], lb_f, st_f, False),
            (_silu(q_ref[rb, :]), fb_ref[rb, :], i_ref[rb, :], lb_b, st_b, True)])
        of_ref[rf, :] = o_f
        ob_ref[rb, :] = o_b
        return st_f, st_b

    zero = jnp.zeros((dk, dk), F32)
    lax.fori_loop(0, n_groups, body, (zero, zero))

    def finish(j, carry):
        r = pl.ds(pl.multiple_of(j * A_GROUP, A_GROUP), A_GROUP)
        o = of_ref[r, :] + ob_ref[r, :]
        o = o * lax.rsqrt(jnp.mean(o * o, axis=-1, keepdims=True) + NORM_EPS) * ng_ref[...]
        o_ref[r, :] = (o * _silu(g_ref[r, :])).astype(o_ref.dtype)
        return carry

    lax.fori_loop(0, n_groups, finish, 0)


def _hgrn2(p_a, lb, norm_g, batch, seq):
    n = p_a.shape[0]
    width = p_a.shape[1] // 5
    heads = width // A_HEAD_DIM
    sect = lambda s: pl.BlockSpec((seq, A_HEAD_DIM), lambda b, h, s=s: (b, s * heads + h))
    return pl.pallas_call(
        _hgrn_kernel,
        grid=(batch, heads),
        in_specs=[sect(0), sect(1), sect(2), sect(3), sect(4),
                  pl.BlockSpec((2, A_HEAD_DIM), lambda b, h: (0, h)),
                  pl.BlockSpec((1, A_HEAD_DIM), lambda b, h: (0, 0))],
        out_specs=pl.BlockSpec((seq, A_HEAD_DIM), lambda b, h: (b, h)),
        out_shape=jax.ShapeDtypeStruct((n, width), BF16),
        scratch_shapes=[pltpu.VMEM((seq, A_HEAD_DIM), F32), pltpu.VMEM((seq, A_HEAD_DIM), F32)],
        compiler_params=_params("parallel", "parallel"),
        name="hgrn2",
    )(p_a, p_a, p_a, p_a, p_a, lb, norm_g.reshape(1, A_HEAD_DIM))


def _softplus(z):
    return jnp.maximum(z, 0.0) + jnp.log(1.0 + jnp.exp(-jnp.abs(z)))


def _rwkv_prep_kernel(p_ref, prev_ref, next_ref, mu_ref, w0_ref, w2_ref, a0_ref, a2_ref, g2_ref,
                      kk_ref, ka_ref, rk_ref, ones_ref,
                      r_out, k_out, v_out, kkn_out, b_out, lwf_out, lwb_out, g_out, bonus_out):
    i = pl.program_id(1)
    last = pl.num_programs(1) - 1
    ts = p_ref.shape[0]
    width = a0_ref.shape[1]
    heads = width // B_HEAD_DIM
    p = p_ref[...]
    prev_row = jnp.where(i == 0, 0.0, prev_ref[7:8, :])
    next_row = jnp.where(i == last, 0.0, next_ref[0:1, :])
    row = lax.broadcasted_iota(jnp.int32, (ts, 1), 0)
    up = jnp.where(row == 0, prev_row, pltpu.roll(p, 1, axis=0))
    dn = jnp.where(row == ts - 1, next_row, pltpu.roll(p, ts - 1, axis=0))
    p = p + mu_ref[...] * (0.5 * (up + dn) - p)

    r = p[:, 0:width]
    k = p[:, width:2 * width]
    v = p[:, 2 * width:3 * width]
    o = 3 * width
    wlo_f = p[:, o:o + 64]
    wlo_b = p[:, o + 64:o + 128]
    alo = p[:, o + 128:o + 192]
    glo = p[:, o + 192:o + 320]

    def log_decay(wlo, d):
        z = w0_ref[d:d + 1, :] + _dot(jnp.tanh(wlo).astype(BF16), w2_ref[d])
        w = -_softplus(-z) - 0.5
        return -jnp.exp(w)

    lw_f = log_decay(wlo_f, 0)
    lw_b = log_decay(wlo_b, 1)
    a = _sigmoid(a0_ref[...] + _dot(alo.astype(BF16), a2_ref[...]))
    g = _dot(_sigmoid(glo).astype(BF16), g2_ref[...])
    kk = k * kk_ref[...]
    ones = ones_ref[...]
    head_sum = lambda t: jnp.dot(t, ones, precision=HIGHEST, preferred_element_type=F32)
    kk_n = kk / jnp.maximum(jnp.sqrt(head_sum(kk * kk)), 1e-12)
    k_mod = k * (1.0 + (a - 1.0) * ka_ref[...])
    bonus = head_sum(r * k_mod * rk_ref[...]) * v
    g_out[...] = g
    bonus_out[...] = bonus
    for h in range(heads):
        sl = slice(h * B_HEAD_DIM, (h + 1) * B_HEAD_DIM)
        r_out[0, h] = r[:, sl]
        k_out[0, h] = k_mod[:, sl]
        v_out[0, h] = v[:, sl]
        kkn_out[0, h] = kk_n[:, sl]
        b_out[0, h] = (kk_n * a)[:, sl]
        lwf_out[0, h] = lw_f[:, sl]
        lwb_out[0, h] = lw_b[:, sl]


def _rwkv_prep(p_b, mu, w0, w2, a0, a2, g2, k_k, k_a, r_k, batch, seq, ts=256):
    n, cols = p_b.shape
    width = a0.shape[0]
    heads = width // B_HEAD_DIM
    nblk = seq // ts
    rows8 = ts // 8
    head_id = np.arange(width) // B_HEAD_DIM
    ones = jnp.asarray(head_id[:, None] == head_id[None, :], F32)
    full = lambda a: pl.BlockSpec(a.shape, lambda b, i: (0,) * a.ndim)
    vec = lambda a: a.reshape(1, -1)
    args = [vec(mu), w0, w2.astype(BF16), vec(a0), a2.astype(BF16), g2.astype(BF16),
            vec(k_k), vec(k_a), vec(r_k), ones]
    hm = pl.BlockSpec((1, heads, ts, B_HEAD_DIM), lambda b, i: (b, 0, i, 0))
    tokm = pl.BlockSpec((ts, width), lambda b, i: (b * nblk + i, 0))
    hm_shape = jax.ShapeDtypeStruct((batch, heads, seq, B_HEAD_DIM), F32)
    tok_shape = jax.ShapeDtypeStruct((n, width), F32)
    return pl.pallas_call(
        _rwkv_prep_kernel,
        grid=(batch, nblk),
        in_specs=[
            pl.BlockSpec((ts, cols), lambda b, i: (b * nblk + i, 0)),
            pl.BlockSpec((8, cols), lambda b, i: (jnp.maximum((b * nblk + i) * rows8 - 1, 0), 0)),
            pl.BlockSpec((8, cols), lambda b, i: (jnp.minimum((b * nblk + i + 1) * rows8, n // 8 - 1), 0)),
        ] + [full(a) for a in args],
        out_specs=[hm] * 7 + [tokm] * 2,
        out_shape=[hm_shape] * 7 + [tok_shape] * 2,
        compiler_params=_params("parallel", "parallel"),
        name="rwkv_prep",
    )(p_b, p_b, p_b, *args)


def _rwkv_chunks(chains):
    c, d = chains[0][0].shape
    ti = lax.broadcasted_iota(jnp.int32, (c, c), 0)
    si = lax.broadcasted_iota(jnp.int32, (c, c), 1)
    eye = jnp.where(ti == si, 1.0, 0.0)
    masks = {rev: (((si > ti), (si >= ti)) if rev else ((si < ti), (si <= ti))) for rev in (False, True)}

    st = []
    for r, k, v, kk, b, lw, tt, rev in chains:
        cum = _chunk_cumsum(lw, c, rev)
        grow = jnp.exp(-cum)
        kt = kk * jnp.exp(cum - lw)
        rt = r * jnp.exp(cum)
        b_ = (b * grow).astype(BF16)
        kb = jnp.concatenate([(k * grow).astype(BF16), b_], axis=0)
        st.append(dict(v=v, vb=v.astype(BF16), kt=kt, rt=rt, b_=b_, kb=kb, tt=tt, ttb=tt.astype(BF16),
                       decay=jnp.exp(cum[0:1, :] if rev else cum[c - 1:c, :]), rev=rev,
                       lhs=jnp.concatenate([kt, rt], axis=0).astype(BF16)))
    for s in st:
        s["big"] = _dot_nt(s["lhs"], s["kb"])
    for s in st:
        strict, incl = masks[s["rev"]]
        big = s["big"]
        a_ab = jnp.where(strict, big[:c, c:], 0.0)
        s["a_ak"] = jnp.where(strict, big[:c, :c], 0.0).astype(BF16)
        s["p"] = jnp.concatenate([jnp.where(incl, big[c:, :c], 0.0), jnp.where(incl, -big[c:, c:], 0.0)],
                                 axis=1).astype(BF16)
        s["m"] = eye - a_ab
        s["ab"] = a_ab.astype(BF16)
    for s in st:
        s["aj"] = _dot(s["ab"], s["ab"])
        s["av"] = _dot(s["a_ak"], s["vb"])
    span = 2
    while span < c:
        span *= 2
        for s in st:
            ajb = s["aj"].astype(BF16)
            if span < c:
                both = _dot(jnp.concatenate([s["aj"], s["m"]], axis=0).astype(BF16), ajb)
                s["aj"], s["m"] = both[:c], s["m"] + both[c:]
            else:
                s["m"] = s["m"] + _dot(s["m"].astype(BF16), ajb)
    for s in st:
        s["wu"] = _dot(s["m"].astype(BF16), jnp.concatenate([s["kt"], s["av"]], axis=1).astype(BF16))
    for s in st:
        w_, u0, v = s["wu"][:, :d], s["wu"][:, d:], s["v"]
        rhs = jnp.concatenate([jnp.concatenate([v, jnp.zeros_like(v)], axis=1),
                               jnp.concatenate([u0, w_], axis=1)], axis=0).astype(BF16)
        s["yq"] = _dot(s["p"], rhs)
        s["h_t"] = _dot_tn(jnp.concatenate([v, -u0], axis=0).astype(BF16), s["kb"])
        s["bw"] = _dot_tn(s["b_"], w_.astype(BF16))
    out = []
    for s in st:
        q = s["rt"] + s["yq"][:, d:]
        y = _dot_nt(q.astype(BF16), s["ttb"]) + s["yq"][:, :d]
        tt = (s["tt"] - _dot_nt(s["ttb"], s["bw"].astype(BF16)) + s["h_t"]) * s["decay"]
        out.append((y, tt))
    return out


def _rwkv_scan_kernel(rf, kf, vf, kkf, bf, lwf, rb, kb, vb, kkb, bb, lwb, yf_ref, yb_ref, tf_ref, tb_ref):
    heads, ts = rf.shape[1], rf.shape[2]
    n_chunks = ts // B_CHUNK

    @pl.when(pl.program_id(1) == 0)
    def _():
        tf_ref[...] = jnp.zeros_like(tf_ref)
        tb_ref[...] = jnp.zeros_like(tb_ref)

    def body(j, carry):
        rows_f = pl.ds(pl.multiple_of(j * B_CHUNK, B_CHUNK), B_CHUNK)
        rows_b = pl.ds(pl.multiple_of((n_chunks - 1 - j) * B_CHUNK, B_CHUNK), B_CHUNK)
        chains = []
        for h in range(heads):
            chains.append([ref[0, h, rows_f, :] for ref in (rf, kf, vf, kkf, bf, lwf)] + [tf_ref[h], False])
            chains.append([ref[0, h, rows_b, :] for ref in (rb, kb, vb, kkb, bb, lwb)] + [tb_ref[h], True])
        res = _rwkv_chunks(chains)
        for h in range(heads):
            yf_ref[0, h, rows_f, :], tf_ref[h] = res[2 * h]
            yb_ref[0, h, rows_b, :], tb_ref[h] = res[2 * h + 1]
        return carry

    lax.fori_loop(0, n_chunks, body, 0)


def _rwkv_scan(r, k, v, kkn, b, lw_f, lw_b, ts=256):
    batch, heads, seq, d = r.shape
    nblk = seq // ts
    fwd = pl.BlockSpec((1, heads, ts, d), lambda bi, i: (bi, 0, i, 0))
    bwd = pl.BlockSpec((1, heads, ts, d), lambda bi, i: (bi, 0, nblk - 1 - i, 0))
    shape = jax.ShapeDtypeStruct(r.shape, F32)
    return pl.pallas_call(
        _rwkv_scan_kernel,
        grid=(batch, nblk),
        in_specs=[fwd] * 6 + [bwd] * 6,
        out_specs=[fwd, bwd],
        out_shape=[shape, shape],
        scratch_shapes=[pltpu.VMEM((heads, d, d), F32), pltpu.VMEM((heads, d, d), F32)],
        compiler_params=_params("parallel", "arbitrary"),
        name="rwkv_scan",
    )(r, k, v, kkn, b, lw_f, r, k, v, kkn, b, lw_b)


def _rwkv_post_kernel(yf_ref, yb_ref, g_ref, bonus_ref, lng_ref, lnb_ref, o_ref):
    heads = yf_ref.shape[1]
    outs = []
    for h in range(heads):
        y = yf_ref[0, h] + yb_ref[0, h]
        mean = jnp.mean(y, axis=-1, keepdims=True)
        var = jnp.mean(jnp.square(y - mean), axis=-1, keepdims=True)
        outs.append((y - mean) * lax.rsqrt(var + B_LN_EPS))
    yn = jnp.concatenate(outs, axis=1) * lng_ref[...] + lnb_ref[...]
    o_ref[...] = ((yn + bonus_ref[...]) * g_ref[...]).astype(o_ref.dtype)


def _rwkv_post(y_f, y_b, g, bonus, ln_g, ln_b, ts=256):
    batch, heads, seq, d = y_f.shape
    n, width = g.shape
    nblk = seq // ts
    hm = pl.BlockSpec((1, heads, ts, d), lambda b, i: (b, 0, i, 0))
    tokm = pl.BlockSpec((ts, width), lambda b, i: (b * nblk + i, 0))
    vec = pl.BlockSpec((1, width), lambda b, i: (0, 0))
    return pl.pallas_call(
        _rwkv_post_kernel,
        grid=(batch, nblk),
        in_specs=[hm, hm, tokm, tokm, vec, vec],
        out_specs=tokm,
        out_shape=jax.ShapeDtypeStruct((n, width), BF16),
        compiler_params=_params("parallel", "parallel"),
        name="rwkv_post",
    )(y_f, y_b, g, bonus, ln_g.reshape(1, width), ln_b.reshape(1, width))


def _rwkv7(p_b, mu, w0, w2, a0, a2, g2, k_k, k_a, r_k, ln_g, ln_b, batch, seq):
    r, k, v, kkn, b, lw_f, lw_b, g, bonus = _rwkv_prep(p_b, mu, w0, w2, a0, a2, g2, k_k, k_a, r_k, batch, seq)
    y_f, y_b = _rwkv_scan(r, k, v, kkn, b, lw_f, lw_b)
    return _rwkv_post(y_f, y_b, g, bonus, ln_g, ln_b)


_BUCKET_EDGES = (0, 1, 2, 3, 4, 5, 6, 7, 8, 12, 16, 23, 32, 46, 64, 91)
ATTN_TK = 128


def _bias_band_kernel(table_ref, o_ref):
    h = pl.program_id(0)
    n_d, tq, tk = o_ref.shape[1:]
    half = REL_BUCKETS // 2
    r = lax.broadcasted_iota(jnp.int32, (tq, tk), 0)
    c = lax.broadcasted_iota(jnp.int32, (tq, tk), 1)
    for d in range(n_d):
        rel = tk * (d - 1) + c - r
        n = jnp.abs(rel)
        vals = []
        for sign in range(2):
            val = jnp.full((tq, tk), table_ref[sign * half + half - 1, h], F32)
            for bkt in range(half - 2, -1, -1):
                val = jnp.where(n < _BUCKET_EDGES[bkt + 1], table_ref[sign * half + bkt, h], val)
            vals.append(val)
        o_ref[0, d] = jnp.where(rel > 0, vals[1], vals[0])


def _bias_band(rel_table, tq):
    heads = rel_table.shape[1]
    n_d = tq // ATTN_TK + 2
    return pl.pallas_call(
        _bias_band_kernel,
        grid=(heads,),
        in_specs=[pl.BlockSpec(memory_space=pltpu.SMEM)],
        out_specs=pl.BlockSpec((1, n_d, tq, ATTN_TK), lambda h: (h, 0, 0, 0)),
        out_shape=jax.ShapeDtypeStruct((heads, n_d, tq, ATTN_TK), F32),
        compiler_params=_params("parallel"),
        name="bias_band",
    )(rel_table)


def _diff_attn_kernel(table_ref, lam_ref, q_ref, k_ref, v_ref, band_ref, g_ref, o_ref, lg_ref, *, lam_init):
    h = pl.program_id(1)
    i = pl.program_id(2)
    tq, dh2 = q_ref.shape
    seq = k_ref.shape[0]
    n_d = band_ref.shape[1]
    n_kt = seq // ATTN_TK
    ratio = tq // ATTN_TK
    half = REL_BUCKETS // 2
    k = k_ref[...]
    v = v_ref[...]
    lane = lax.broadcasted_iota(jnp.int32, (1, dh2), 1)
    q = q_ref[...] * jnp.asarray(C_HEAD_DIM ** -0.5, q_ref.dtype)
    zero = jnp.zeros_like(q)
    first_band = ratio * i - 1
    col = lax.broadcasted_iota(jnp.int32, (1, seq), 1)
    far_row = jnp.where(col < first_band * ATTN_TK, table_ref[half - 1, h],
                        jnp.where(col >= (first_band + n_d) * ATTN_TK, table_ref[REL_BUCKETS - 1, h],
                                  -jnp.inf))
    row_max = []
    for m in range(2):
        qm = jnp.where((lane >= m * C_HEAD_DIM) & (lane < (m + 1) * C_HEAD_DIM), q, zero)
        far = _dot_nt(qm, k) + far_row
        mx = jnp.max(far, axis=-1, keepdims=True)
        lg_ref[m] = far
        for d in (0, n_d - 1) + tuple(range(1, n_d - 1)):
            kt = first_band + d
            valid = (kt >= 0) & (kt < n_kt)
            c0 = pl.multiple_of(jnp.clip(kt, 0, n_kt - 1) * ATTN_TK, ATTN_TK)
            near = _dot_nt(qm, k_ref[pl.ds(c0, ATTN_TK), :]) + band_ref[0, d]
            mx = jnp.maximum(mx, jnp.where(valid, jnp.max(near, axis=-1, keepdims=True), -jnp.inf))
            lg_ref[m, :, pl.ds(c0, ATTN_TK)] = near
        row_max.append(mx)
    es = [jnp.exp(lg_ref[m] - row_max[m]) for m in range(2)]
    parts = [_dot(e.astype(BF16), v) * (1.0 / jnp.sum(e, axis=-1, keepdims=True)) for e in es]
    o = parts[0] - lam_ref[0] * parts[1]
    o = o * lax.rsqrt(jnp.mean(o * o, axis=-1, keepdims=True) + NORM_EPS) * g_ref[...]
    o_ref[...] = (o * (1.0 - lam_init)).astype(o_ref.dtype)


def _diff_attention(qkv, rel_table, lam_full, subln_g, lam_init, batch, seq, tq=256):
    n = qkv.shape[0]
    dh2 = 2 * C_HEAD_DIM
    heads = qkv.shape[1] // (3 * dh2)
    nq = seq // tq
    band = _bias_band(rel_table, tq)
    return pl.pallas_call(
        functools.partial(_diff_attn_kernel, lam_init=lam_init),
        grid=(batch, heads, nq),
        in_specs=[
            pl.BlockSpec(memory_space=pltpu.SMEM),
            pl.BlockSpec(memory_space=pltpu.SMEM),
            pl.BlockSpec((tq, dh2), lambda b, h, i: (b * nq + i, h)),
            pl.BlockSpec((seq, dh2), lambda b, h, i: (b, heads + h)),
            pl.BlockSpec((seq, dh2), lambda b, h, i: (b, 2 * heads + h)),
            pl.BlockSpec((1,) + band.shape[1:], lambda b, h, i: (h, 0, 0, 0)),
            pl.BlockSpec((1, dh2), lambda b, h, i: (0, 0)),
        ],
        out_specs=pl.BlockSpec((tq, dh2), lambda b, h, i: (b * nq + i, h)),
        out_shape=jax.ShapeDtypeStruct((n, heads * dh2), BF16),
        scratch_shapes=[pltpu.VMEM((2, tq, seq), F32)],
        compiler_params=_params("parallel", "parallel", "parallel"),
        name="diff_attention",
    )(rel_table, lam_full.reshape(1), qkv, qkv, qkv, band, subln_g.reshape(1, dh2))


MOE_TILE = 256
_HI_MASK = np.uint32(0xFFFF0000)


def _pack_bf16_pair(lo, hi):
    as_bits = lambda t: lax.bitcast_convert_type(t.astype(BF16).astype(F32), jnp.uint32)
    return (as_bits(hi) & _HI_MASK) | (as_bits(lo) >> 16)


def _unpack_bf16_pair(word):
    lo = lax.bitcast_convert_type(word << 16, F32).astype(BF16)
    hi = lax.bitcast_convert_type(word & _HI_MASK, F32).astype(BF16)
    return lo, hi


def _router_kernel(x_ref, g_ref, sh_ref, sc_ref, rw_ref, rb_ref, hp_ref, idx_ref, w_ref, rank_ref, cnt_ref,
                   run_ref, *, n_experts):
    @pl.when(pl.program_id(0) == 0)
    def _():
        run_ref[...] = jnp.zeros_like(run_ref)

    tm, d = x_ref.shape
    h = _modulated_norm(x_ref[...], g_ref[...], sh_ref[0], sc_ref[0])
    hp_ref[...] = _pack_bf16_pair(h[:, :d // 2], h[:, d // 2:])
    logits = jnp.dot(h, rw_ref[...], precision=HIGHEST, preferred_element_type=F32) + rb_ref[...]
    lane = lax.broadcasted_iota(jnp.int32, logits.shape, 1).astype(F32)
    neg = jnp.float32(-jnp.inf)
    logits = jnp.where(lane < n_experts, logits, neg)
    picks, vals = [], []
    for _ in range(TOP_K):
        m = jnp.max(logits, axis=-1, keepdims=True)
        first = jnp.min(jnp.where(logits == m, lane, float(LANES)), axis=-1, keepdims=True)
        hit = lane == first
        picks.append((first, hit))
        vals.append(m)
        logits = jnp.where(hit, neg, logits)
    es = [jnp.exp(v - vals[0]) for v in vals]
    inv = 1.0 / (es[0] + es[1] + es[2] + es[3])
    assigned = picks[0][1] | picks[1][1] | picks[2][1] | picks[3][1]
    onehot = jnp.where(assigned, 1.0, 0.0)
    ri = lax.broadcasted_iota(jnp.int32, (tm, tm), 0)
    ci = lax.broadcasted_iota(jnp.int32, (tm, tm), 1)
    earlier = jnp.where(ci < ri, 1.0, 0.0).astype(BF16)
    before = _dot(earlier, onehot.astype(BF16)) + run_ref[...]
    idx_o = jnp.zeros(logits.shape, F32)
    w_o = jnp.zeros(logits.shape, F32)
    rank_o = jnp.zeros(logits.shape, F32)
    for j, (first, hit) in enumerate(picks):
        rank_j = jnp.sum(jnp.where(hit, before, 0.0), axis=-1, keepdims=True)
        idx_o = jnp.where(lane == j, first, idx_o)
        w_o = jnp.where(lane == j, es[j] * inv, w_o)
        rank_o = jnp.where(lane == j, rank_j, rank_o)
    idx_ref[...] = idx_o
    w_ref[...] = w_o
    rank_ref[...] = rank_o
    run_ref[...] += jnp.sum(onehot, axis=0, keepdims=True)
    cnt_ref[...] = run_ref[...]


def _router(x2, g, mod_l, shift_idx, seq, router_w, router_b):
    n, d = x2.shape
    tm = MOE_TILE
    n_experts = router_w.shape[1]
    per_batch = seq // tm
    rw = jnp.zeros((d, LANES), F32).at[:, :n_experts].set(router_w)
    rb = jnp.zeros((1, LANES), F32).at[0, :n_experts].set(router_b)
    lane_out = pl.BlockSpec((tm, LANES), lambda i: (i, 0))
    lane_shape = jax.ShapeDtypeStruct((n, LANES), F32)
    return pl.pallas_call(
        functools.partial(_router_kernel, n_experts=n_experts),
        grid=(n // tm,),
        in_specs=[
            pl.BlockSpec((tm, d), lambda i: (i, 0)),
            pl.BlockSpec((1, d), lambda i: (0, 0)),
            pl.BlockSpec((1, 1, d), lambda i: (i // per_batch, 0, shift_idx)),
            pl.BlockSpec((1, 1, d), lambda i: (i // per_batch, 0, shift_idx + 1)),
            pl.BlockSpec((d, LANES), lambda i: (0, 0)),
            pl.BlockSpec((1, LANES), lambda i: (0, 0)),
        ],
        out_specs=[pl.BlockSpec((tm, d // 2), lambda i: (i, 0)), lane_out, lane_out, lane_out,
                   pl.BlockSpec((1, LANES), lambda i: (0, 0))],
        out_shape=[jax.ShapeDtypeStruct((n, d // 2), jnp.uint32), lane_shape, lane_shape, lane_shape,
                   jax.ShapeDtypeStruct((1, LANES), F32)],
        scratch_shapes=[pltpu.VMEM((1, LANES), F32)],
        compiler_params=_params("arbitrary"),
        name="moe_router",
    )(x2, g.reshape(1, d), mod_l, mod_l, rw, rb)


def _dispatch_kernel(pos_ref, h_ref, zeros_ref, xs_ref, sem):
    del zeros_ref
    tm = h_ref.shape[0]

    def row_copy(r, p):
        return pltpu.make_async_copy(h_ref.at[pl.ds(r, 1)], xs_ref.at[pl.ds(p, 1)], sem)

    def issue(r, carry):
        for j in range(TOP_K):
            row_copy(r, pos_ref[r * TOP_K + j]).start(priority=j % 2)
        return carry

    lax.fori_loop(0, tm, issue, 0)

    def drain(r, carry):
        for j in range(TOP_K):
            row_copy(r, pos_ref[r * TOP_K + j]).wait()
        return carry

    lax.fori_loop(0, tm, drain, 0)


def _dispatch(hp, pos_flat, n_rows):
    n, dw = hp.shape
    tm = MOE_TILE
    zeros = jnp.zeros((n_rows, dw), jnp.uint32)
    return pl.pallas_call(
        _dispatch_kernel,
        grid=(n // tm,),
        in_specs=[
            pl.BlockSpec((tm * TOP_K,), lambda i: (i,), memory_space=pltpu.SMEM),
            pl.BlockSpec((tm, dw), lambda i: (i, 0)),
            pl.BlockSpec(memory_space=pl.ANY),
        ],
        out_specs=pl.BlockSpec(memory_space=pl.ANY),
        out_shape=jax.ShapeDtypeStruct((n_rows, dw), jnp.uint32),
        scratch_shapes=[pltpu.SemaphoreType.DMA(())],
        input_output_aliases={2: 0},
        compiler_params=_params("arbitrary"),
        name="moe_dispatch",
    )(pos_flat, hp, zeros)


GLU_TILE = 2 * LANES


def _experts_kernel(te_ref, nu_ref, slot_ref, nxt_ref, xs_ref, b1_ref, b2_ref, perm_ref, w1_hbm, w2_hbm,
                    ys_ref, w1buf, w2buf, w1p_ref, w2b_ref, sem, *, layer):
    t = pl.program_id(0)
    used = t < nu_ref[0]
    new_expert = (t == 0) | (te_ref[t] != te_ref[jnp.maximum(t - 1, 0)])
    n_col_tiles = w1p_ref.shape[1] // GLU_TILE

    def weight_copies(e, slot):
        return (pltpu.make_async_copy(w1_hbm.at[layer, e], w1buf.at[slot], sem.at[0, slot]),
                pltpu.make_async_copy(w2_hbm.at[layer, e], w2buf.at[slot], sem.at[1, slot]))

    @pl.when(used & new_expert)
    def _():
        slot = slot_ref[t]

        @pl.when(t == 0)
        def _():
            for cp in weight_copies(te_ref[t], slot):
                cp.start()

        for cp in weight_copies(te_ref[t], slot):
            cp.wait()

        @pl.when(nxt_ref[t] >= 0)
        def _():
            for cp in weight_copies(nxt_ref[t], 1 - slot):
                cp.start()

        for c in range(n_col_tiles):
            cols = slice(c * GLU_TILE, (c + 1) * GLU_TILE)
            w1p_ref[:, cols] = _dot(w1buf[slot, :, cols].astype(BF16), perm_ref[...]).astype(BF16)
        w2b_ref[...] = w2buf[slot].astype(BF16)

    @pl.when(used)
    def _():
        lo, hi = _unpack_bf16_pair(xs_ref[...])
        x = jnp.concatenate([lo, hi], axis=1)
        hh = _dot(x, w1p_ref[...]) + b1_ref[0]
        acts = []
        for c in range(n_col_tiles):
            x_glu = jnp.minimum(hh[:, c * GLU_TILE:c * GLU_TILE + LANES], SWIGLU_LIMIT)
            x_lin = jnp.clip(hh[:, c * GLU_TILE + LANES:(c + 1) * GLU_TILE], -SWIGLU_LIMIT, SWIGLU_LIMIT)
            acts.append((x_glu * _sigmoid(SWIGLU_ALPHA * x_glu) * (x_lin + 1.0)).astype(BF16))
        out = _dot(jnp.concatenate(acts, axis=1), w2b_ref[...]) + b2_ref[0]
        half = out.shape[1] // 2
        ys_ref[...] = _pack_bf16_pair(out[:, :half], out[:, half:])

    @pl.when(t >= nu_ref[0])
    def _():
        ys_ref[...] = jnp.zeros_like(ys_ref)


def _experts(xs, tile_expert, n_used, slot, nxt, w1_all, b1, w2_all, b2, layer):
    n_rows, dw = xs.shape
    tm = MOE_TILE
    _, n_exp, d, de2 = w1_all.shape
    src = np.arange(GLU_TILE)
    dst = np.where(src % 2 == 0, src // 2, LANES + src // 2)
    perm = jnp.asarray(dst[:, None] == np.arange(GLU_TILE)[None, :], BF16)
    b1p = b1.reshape(n_exp, de2 // GLU_TILE, LANES, 2).transpose(0, 1, 3, 2).reshape(n_exp, 1, de2)
    row = lambda t, te, nu, sl, nx: (jnp.maximum(jnp.minimum(t, nu[0] - 1), 0), 0)
    exp3 = lambda t, te, nu, sl, nx: (te[t], 0, 0)
    grid_spec = pltpu.PrefetchScalarGridSpec(
        num_scalar_prefetch=4,
        grid=(n_rows // tm,),
        in_specs=[
            pl.BlockSpec((tm, dw), row),
            pl.BlockSpec((1, 1, de2), exp3),
            pl.BlockSpec((1, 1, d), exp3),
            pl.BlockSpec((GLU_TILE, GLU_TILE), lambda t, te, nu, sl, nx: (0, 0)),
            pl.BlockSpec(memory_space=pl.ANY),
            pl.BlockSpec(memory_space=pl.ANY),
        ],
        out_specs=pl.BlockSpec((tm, d // 2), lambda t, te, nu, sl, nx: (t, 0)),
        scratch_shapes=[pltpu.VMEM((2, d, de2), F32), pltpu.VMEM((2, de2 // 2, d), F32),
                        pltpu.VMEM((d, de2), BF16), pltpu.VMEM((de2 // 2, d), BF16),
                        pltpu.SemaphoreType.DMA((2, 2))],
    )
    return pl.pallas_call(
        functools.partial(_experts_kernel, layer=layer),
        grid_spec=grid_spec,
        out_shape=jax.ShapeDtypeStruct((n_rows, d // 2), jnp.uint32),
        compiler_params=_params("arbitrary"),
        name="moe_experts",
    )(tile_expert, n_used, slot, nxt, xs, b1p, b2.reshape(n_exp, 1, d), perm, w1_all, w2_all)


def _combine_kernel(pos_ref, x_ref, gate_ref, w_ref, fg_ref, ys_ref, o_ref, buf, sem, *, final_norm):
    tm = x_ref.shape[0]

    def row_copy(r, j):
        return pltpu.make_async_copy(ys_ref.at[pl.ds(pos_ref[r * TOP_K + j], 1)], buf.at[j, pl.ds(r, 1)], sem)

    def issue(r, carry):
        for j in range(TOP_K):
            row_copy(r, j).start(priority=j % 2)
        return carry

    lax.fori_loop(0, tm, issue, 0)

    def drain(r, carry):
        for j in range(TOP_K):
            row_copy(r, j).wait()
        return carry

    lax.fori_loop(0, tm, drain, 0)
    w = w_ref[...]
    y_lo = y_hi = None
    for j in range(TOP_K):
        lo, hi = _unpack_bf16_pair(buf[j])
        wj = w[:, j:j + 1]
        y_lo = wj * lo if j == 0 else y_lo + wj * lo
        y_hi = wj * hi if j == 0 else y_hi + wj * hi
    y = jnp.concatenate([y_lo, y_hi], axis=1)
    out = x_ref[...] + gate_ref[0] * y
    if final_norm:
        out = out * lax.rsqrt(jnp.mean(out * out, axis=-1, keepdims=True) + NORM_EPS) * fg_ref[...]
    o_ref[...] = out


def _combine(x2, mod_l, gate_idx, seq, top_w, pos_flat, ys, final_g, final_norm):
    n, d = x2.shape
    tm = MOE_TILE
    per_batch = seq // tm
    return pl.pallas_call(
        functools.partial(_combine_kernel, final_norm=final_norm),
        grid=(n // tm,),
        in_specs=[
            pl.BlockSpec((tm * TOP_K,), lambda i: (i,), memory_space=pltpu.SMEM),
            pl.BlockSpec((tm, d), lambda i: (i, 0)),
            pl.BlockSpec((1, 1, d), lambda i: (i // per_batch, 0, gate_idx)),
            pl.BlockSpec((tm, LANES), lambda i: (i, 0)),
            pl.BlockSpec((1, d), lambda i: (0, 0)),
            pl.BlockSpec(memory_space=pl.ANY),
        ],
        out_specs=pl.BlockSpec((tm, d), lambda i: (i, 0)),
        out_shape=jax.ShapeDtypeStruct((n, d), F32),
        scratch_shapes=[pltpu.VMEM((TOP_K, tm, d // 2), jnp.uint32), pltpu.SemaphoreType.DMA(())],
        compiler_params=_params("arbitrary"),
        name="moe_combine",
    )(pos_flat, x2, mod_l, top_w, final_g.reshape(1, d), ys)


def _moe_ffn(x2, norm_g, mod_l, seq, router_w, router_b, w1_all, b1, w2_all, b2, layer, final_g, final_norm):
    n, d = x2.shape
    n_exp = router_w.shape[1]
    tile = MOE_TILE
    hp, idx_f, top_w, rank_f, counts = _router(x2, norm_g, mod_l, 3, seq, router_w, router_b)
    counts = counts[0, :n_exp].astype(jnp.int32)
    padded = (counts + tile - 1) // tile * tile
    ends = jnp.cumsum(padded)
    starts = ends - padded
    idx = idx_f[:, :TOP_K].astype(jnp.int32)
    expert_ids = jnp.arange(n_exp, dtype=jnp.int32)
    start_of = jnp.sum(jnp.where(idx[..., None] == expert_ids, starts, 0), axis=-1)
    pos_flat = (start_of + rank_f[:, :TOP_K].astype(jnp.int32)).reshape(-1)
    n_rows = n * TOP_K + n_exp * tile
    n_tiles = n_rows // tile
    tile_start = jnp.arange(n_tiles, dtype=jnp.int32) * tile
    tile_expert = jnp.sum((ends[None, :] <= tile_start[:, None]).astype(jnp.int32), axis=1)
    tile_expert = jnp.minimum(tile_expert, n_exp - 1)
    n_used = (ends[-1:] // tile).astype(jnp.int32)
    nonempty = counts > 0
    ordinal = jnp.cumsum(nonempty.astype(jnp.int32)) - 1
    later = nonempty[None, :] & (expert_ids[None, :] > expert_ids[:, None])
    nxt_e = jnp.min(jnp.where(later, expert_ids[None, :], n_exp), axis=1)
    nxt_e = jnp.where(nxt_e == n_exp, -1, nxt_e)
    slot = (ordinal[tile_expert] % 2).astype(jnp.int32)
    nxt = nxt_e[tile_expert].astype(jnp.int32)
    xs = _dispatch(hp, pos_flat, n_rows)
    ys = _experts(xs, tile_expert, n_used, slot, nxt, w1_all, b1, w2_all, b2, layer)
    return _combine(x2, mod_l, 5, seq, top_w, pos_flat, ys, final_g, final_norm)


def kernel(x, c, ada_w, ada_b, norm_mix_g, norm_ffn_g, router_w, router_b, moe_w1, moe_b1, moe_w2, moe_b2, ab_w_in, ab_w_out, hgrn_lb, hgrn_norm_g, rwkv_mu, rwkv_w0, rwkv_w2, rwkv_a0, rwkv_a2, rwkv_g2, rwkv_k_k, rwkv_k_a, rwkv_r_k, rwkv_ln_g, rwkv_ln_b, attn_w_in, attn_w_out, attn_lambda, attn_subln_g, rel_bias_table, final_norm_g):
    batch, seq, d = x.shape
    n = batch * seq
    depth = ada_w.shape[0]
    x2 = x.reshape(n, d)
    mod = _adaln(c, ada_w, ada_b)
    lb_all = jnp.cumsum(jax.nn.softmax(hgrn_lb.astype(F32), axis=1), axis=1)
    for layer in range(depth):
        mod_l = mod[layer].reshape(batch, 1, 6 * d)
        j = layer // 2
        if layer % 2 == 0:
            a_cols = 5 * (d // 2)
            w_in = ab_w_in[j].astype(BF16)
            p_a, p_b = _normmod_proj(x2, norm_mix_g[layer], mod_l, 0, seq,
                                     [w_in[:, :a_cols], w_in[:, a_cols:]], [F32, F32])
            y_a = _hgrn2(p_a, lb_all[:, j], hgrn_norm_g[j], batch, seq)
            y_b = _rwkv7(p_b, rwkv_mu[j], rwkv_w0[j], rwkv_w2[j], rwkv_a0[j], rwkv_a2[j], rwkv_g2[j],
                         rwkv_k_k[j], rwkv_k_a[j], rwkv_r_k[j], rwkv_ln_g[j], rwkv_ln_b[j], batch, seq)
            w_out = ab_w_out[j].astype(BF16)
            x2 = _outproj_residual(x2, mod_l, 2, seq, [y_a, y_b], [w_out[:d // 2], w_out[d // 2:]])
        else:
            (qkv,) = _normmod_proj(x2, norm_mix_g[layer], mod_l, 0, seq, [attn_w_in[j].astype(BF16)], [BF16])
            lam = attn_lambda[j].astype(F32)
            lam_init = 0.8 - 0.6 * math.exp(-0.3 * layer)
            lam_full = jnp.exp(jnp.sum(lam[0] * lam[1])) - jnp.exp(jnp.sum(lam[2] * lam[3])) + lam_init
            o = _diff_attention(qkv, rel_bias_table, lam_full, attn_subln_g[j], lam_init, batch, seq)
            x2 = _outproj_residual(x2, mod_l, 2, seq, [o], [attn_w_out[j].astype(BF16)])
        x2 = _moe_ffn(x2, norm_ffn_g[layer], mod_l, seq, router_w[layer], router_b[layer], moe_w1,
                      moe_b1[layer], moe_w2, moe_b2[layer], layer, final_norm_g, layer == depth - 1)
    return x2.reshape(batch, seq, d)
```

```python
import functools
import math

import jax
import jax.numpy as jnp
import numpy as np
from jax import lax
from jax.experimental import pallas as pl
from jax.experimental.pallas import tpu as pltpu
from jax.experimental.pallas import tpu_sc as plsc

F32 = jnp.float32
BF16 = jnp.bfloat16
HIGHEST = lax.Precision.HIGHEST

NORM_EPS = 1e-6

A_HEAD_DIM = 128
A_CHUNK = 32
A_GROUP = 128
B_HEAD_DIM = 64
B_CHUNK = 64
B_LN_EPS = 1e-5 * B_HEAD_DIM
C_HEAD_DIM = 64
REL_BUCKETS = 32
REL_MAX_DISTANCE = 128
TOP_K = 4
SWIGLU_LIMIT = 7.0
SWIGLU_ALPHA = 1.702

V7X_VMEM_BYTES = 64 * 1024 * 1024
VMEM_LIMIT = V7X_VMEM_BYTES - 8 * 1024 * 1024
LANES = 128


def _params(*sem):
    return pltpu.CompilerParams(dimension_semantics=sem, vmem_limit_bytes=VMEM_LIMIT)


def _sigmoid(x):
    return 1.0 / (1.0 + jnp.exp(-x))


def _silu(x):
    return x * _sigmoid(x)


def _dot(a, b):
    return jnp.dot(a, b, preferred_element_type=F32)


def _dot_nt(a, b):
    return lax.dot_general(a, b, (((1,), (1,)), ((), ())), preferred_element_type=F32)


def _dot_tn(a, b):
    return lax.dot_general(a, b, (((0,), (0,)), ((), ())), preferred_element_type=F32)


def _adaln_kernel(c_ref, w_ref, b_ref, o_ref):
    cond = _silu(c_ref[...])
    o_ref[0] = jnp.dot(cond, w_ref[0], precision=HIGHEST, preferred_element_type=F32) + b_ref[0]


def _adaln(c, ada_w, ada_b):
    n_layers, d, n_out = ada_w.shape
    batch = c.shape[0]
    tn = 1536
    return pl.pallas_call(
        _adaln_kernel,
        grid=(n_layers, n_out // tn),
        in_specs=[
            pl.BlockSpec((batch, d), lambda l, j: (0, 0)),
            pl.BlockSpec((1, d, tn), lambda l, j: (l, 0, j)),
            pl.BlockSpec((1, 1, tn), lambda l, j: (l, 0, j)),
        ],
        out_specs=pl.BlockSpec((1, batch, tn), lambda l, j: (l, 0, j)),
        out_shape=jax.ShapeDtypeStruct((n_layers, batch, n_out), F32),
        compiler_params=_params("parallel", "parallel"),
        name="adaln",
    )(c, ada_w, ada_b.reshape(n_layers, 1, n_out))


def _modulated_norm(x, g, shift, scale):
    y = x * lax.rsqrt(jnp.mean(x * x, axis=-1, keepdims=True) + NORM_EPS)
    return (y * g) * (1.0 + scale) + shift


def _normmod_proj_kernel(x_ref, g_ref, sh_ref, sc_ref, *rest, n_w):
    w_refs, o_refs = rest[:n_w], rest[n_w:]
    h = _modulated_norm(x_ref[...], g_ref[...], sh_ref[0], sc_ref[0]).astype(BF16)
    for w_ref, o_ref in zip(w_refs, o_refs):
        o_ref[...] = _dot(h, w_ref[...]).astype(o_ref.dtype)


def _normmod_proj(x2, g, mod_l, shift_idx, seq, weights, out_dtypes, tm=256):
    n, d = x2.shape
    per_batch = seq // tm
    in_specs = [
        pl.BlockSpec((tm, d), lambda i: (i, 0)),
        pl.BlockSpec((1, d), lambda i: (0, 0)),
        pl.BlockSpec((1, 1, d), lambda i: (i // per_batch, 0, shift_idx)),
        pl.BlockSpec((1, 1, d), lambda i: (i // per_batch, 0, shift_idx + 1)),
    ]
    in_specs += [pl.BlockSpec(w.shape, lambda i: (0, 0)) for w in weights]
    out_specs = [pl.BlockSpec((tm, w.shape[1]), lambda i: (i, 0)) for w in weights]
    out_shape = [jax.ShapeDtypeStruct((n, w.shape[1]), dt) for w, dt in zip(weights, out_dtypes)]
    return pl.pallas_call(
        functools.partial(_normmod_proj_kernel, n_w=len(weights)),
        grid=(n // tm,),
        in_specs=in_specs,
        out_specs=out_specs,
        out_shape=out_shape,
        compiler_params=_params("parallel"),
        name="normmod_proj",
    )(x2, g.reshape(1, d), mod_l, mod_l, *weights)


def _outproj_kernel(x_ref, gate_ref, *rest, n_y):
    y_refs, w_refs, o_ref = rest[:n_y], rest[n_y:2 * n_y], rest[2 * n_y]
    acc = _dot(y_refs[0][...], w_refs[0][...])
    for y_ref, w_ref in zip(y_refs[1:], w_refs[1:]):
        acc += _dot(y_ref[...], w_ref[...])
    o_ref[...] = x_ref[...] + gate_ref[0] * acc


def _outproj_residual(x2, mod_l, gate_idx, seq, ys, ws, tm=512):
    n, d = x2.shape
    per_batch = seq // tm
    in_specs = [
        pl.BlockSpec((tm, d), lambda i: (i, 0)),
        pl.BlockSpec((1, 1, d), lambda i: (i // per_batch, 0, gate_idx)),
    ]
    in_specs += [pl.BlockSpec((tm, y.shape[1]), lambda i: (i, 0)) for y in ys]
    in_specs += [pl.BlockSpec(w.shape, lambda i: (0, 0)) for w in ws]
    return pl.pallas_call(
        functools.partial(_outproj_kernel, n_y=len(ys)),
        grid=(n // tm,),
        in_specs=in_specs,
        out_specs=pl.BlockSpec((tm, d), lambda i: (i, 0)),
        out_shape=jax.ShapeDtypeStruct((n, d), F32),
        compiler_params=_params("parallel"),
        name="outproj_residual",
    )(x2, mod_l, *ys, *ws)


def _chunk_cumsum(x, chunk, reverse):
    rows = x.shape[0]
    pos = lax.broadcasted_iota(jnp.int32, x.shape, 0) % chunk
    s = 1
    while s < chunk:
        if reverse:
            x = x + jnp.where(pos < chunk - s, pltpu.roll(x, rows - s, axis=0), 0.0)
        else:
            x = x + jnp.where(pos >= s, pltpu.roll(x, s, axis=0), 0.0)
        s *= 2
    return x


def _hgrn_groups(slabs):
    g_rows, dk = slabs[0][0].shape
    n_chunks = g_rows // A_CHUNK
    ti = lax.broadcasted_iota(jnp.int32, (A_CHUNK, A_CHUNK), 0)
    si = lax.broadcasted_iota(jnp.int32, (A_CHUNK, A_CHUNK), 1)
    chunk_rows = [slice(c * A_CHUNK, (c + 1) * A_CHUNK) for c in range(n_chunks)]
    prep = []
    for q, f, v, lb, st, reverse in slabs:
        fg = lb + (1.0 - lb) * _sigmoid(f)
        k = 1.0 - fg
        b = _chunk_cumsum(jnp.log(fg), A_CHUNK, reverse)
        b3 = b.reshape(n_chunks, A_CHUNK, dk)
        edge = b3[:, 0:1, :] if reverse else b3[:, A_CHUNK - 1:A_CHUNK, :]
        prep.append(dict(
            q_in=(q * jnp.exp(b)).astype(BF16), k_in=(k * jnp.exp(-b)).astype(BF16),
            k_st=(k.reshape(n_chunks, A_CHUNK, dk) * jnp.exp(edge - b3)).astype(BF16),
            decay=jnp.exp(edge), vb=v.astype(BF16), st=st, reverse=reverse,
            mask=(si >= ti) if reverse else (si <= ti)))
    for p in prep:
        p["scores"] = [_dot_nt(p["q_in"][r], p["k_in"][r]) for r in chunk_rows]
        p["dstate"] = [_dot_tn(p["vb"][r], p["k_st"][c]) for c, r in enumerate(chunk_rows)]
    for p in prep:
        p["intra"] = [_dot(jnp.where(p["mask"], s, 0.0).astype(BF16), p["vb"][r])
                      for s, r in zip(p["scores"], chunk_rows)]
    results = []
    for p in prep:
        st = p["st"]
        outs = [None] * n_chunks
        for c in (range(n_chunks - 1, -1, -1) if p["reverse"] else range(n_chunks)):
            outs[c] = p["intra"][c] + _dot_nt(p["q_in"][chunk_rows[c]], st.astype(BF16))
            st = st * p["decay"][c] + p["dstate"][c]
        results.append((jnp.concatenate(outs, axis=0), st))
    return results


def _hgrn_kernel(q_ref, ff_ref, fb_ref, i_ref, g_ref, lb_ref, ng_ref, o_ref, of_ref, ob_ref):
    seq, dk = q_ref.shape
    n_groups = seq // A_GROUP
    lb_f, lb_b = lb_ref[0:1, :], lb_ref[1:2, :]

    def body(j, carry):
        st_f, st_b = carry
        rf = pl.ds(pl.multiple_of(j * A_GROUP, A_GROUP), A_GROUP)
        rb = pl.ds(pl.multiple_of((n_groups - 1 - j) * A_GROUP, A_GROUP), A_GROUP)
        (o_f, st_f), (o_b, st_b) = _hgrn_groups([
            (_silu(q_ref[rf, :]), ff_ref[rf, :], i_ref[rf, :], lb_f, st_f, False),
            (_silu(q_ref[rb, :]), fb_ref[rb, :], i_ref[rb, :], lb_b, st_b, True)])
        of_ref[rf, :] = o_f
        ob_ref[rb, :] = o_b
        return st_f, st_b

    zero = jnp.zeros((dk, dk), F32)
    lax.fori_loop(0, n_groups, body, (zero, zero))

    def finish(j, carry):
        r = pl.ds(pl.multiple_of(j * A_GROUP, A_GROUP), A_GROUP)
        o = of_ref[r, :] + ob_ref[r, :]
        o = o * lax.rsqrt(jnp.mean(o * o, axis=-1, keepdims=True) + NORM_EPS) * ng_ref[...]
        o_ref[r, :] = (o * _silu(g_ref[r, :])).astype(o_ref.dtype)
        return carry

    lax.fori_loop(0, n_groups, finish, 0)


def _hgrn2(p_a, lb, norm_g, batch, seq):
    n = p_a.shape[0]
    width = p_a.shape[1] // 5
    heads = width // A_HEAD_DIM
    sect = lambda s: pl.BlockSpec((seq, A_HEAD_DIM), lambda b, h, s=s: (b, s * heads + h))
    return pl.pallas_call(
        _hgrn_kernel,
        grid=(batch, heads),
        in_specs=[sect(0), sect(1), sect(2), sect(3), sect(4),
                  pl.BlockSpec((2, A_HEAD_DIM), lambda b, h: (0, h)),
                  pl.BlockSpec((1, A_HEAD_DIM), lambda b, h: (0, 0))],
        out_specs=pl.BlockSpec((seq, A_HEAD_DIM), lambda b, h: (b, h)),
        out_shape=jax.ShapeDtypeStruct((n, width), BF16),
        scratch_shapes=[pltpu.VMEM((seq, A_HEAD_DIM), F32), pltpu.VMEM((seq, A_HEAD_DIM), F32)],
        compiler_params=_params("parallel", "parallel"),
        name="hgrn2",
    )(p_a, p_a, p_a, p_a, p_a, lb, norm_g.reshape(1, A_HEAD_DIM))


def _softplus(z):
    return jnp.maximum(z, 0.0) + jnp.log(1.0 + jnp.exp(-jnp.abs(z)))


def _rwkv_prep_kernel(p_ref, prev_ref, next_ref, mu_ref, w0_ref, w2_ref, a0_ref, a2_ref, g2_ref,
                      kk_ref, ka_ref, rk_ref, ones_ref,
                      r_out, k_out, v_out, kkn_out, b_out, lwf_out, lwb_out, g_out, bonus_out):
    i = pl.program_id(1)
    last = pl.num_programs(1) - 1
    ts = p_ref.shape[0]
    width = a0_ref.shape[1]
    heads = width // B_HEAD_DIM
    p = p_ref[...]
    prev_row = jnp.where(i == 0, 0.0, prev_ref[7:8, :])
    next_row = jnp.where(i == last, 0.0, next_ref[0:1, :])
    row = lax.broadcasted_iota(jnp.int32, (ts, 1), 0)
    up = jnp.where(row == 0, prev_row, pltpu.roll(p, 1, axis=0))
    dn = jnp.where(row == ts - 1, next_row, pltpu.roll(p, ts - 1, axis=0))
    p = p + mu_ref[...] * (0.5 * (up + dn) - p)

    r = p[:, 0:width]
    k = p[:, width:2 * width]
    v = p[:, 2 * width:3 * width]
    o = 3 * width
    wlo_f = p[:, o:o + 64]
    wlo_b = p[:, o + 64:o + 128]
    alo = p[:, o + 128:o + 192]
    glo = p[:, o + 192:o + 320]

    def log_decay(wlo, d):
        z = w0_ref[d:d + 1, :] + _dot(jnp.tanh(wlo).astype(BF16), w2_ref[d])
        w = -_softplus(-z) - 0.5
        return -jnp.exp(w)

    lw_f = log_decay(wlo_f, 0)
    lw_b = log_decay(wlo_b, 1)
    a = _sigmoid(a0_ref[...] + _dot(alo.astype(BF16), a2_ref[...]))
    g = _dot(_sigmoid(glo).astype(BF16), g2_ref[...])
    kk = k * kk_ref[...]
    ones = ones_ref[...]
    head_sum = lambda t: jnp.dot(t, ones, precision=HIGHEST, preferred_element_type=F32)
    kk_n = kk / jnp.maximum(jnp.sqrt(head_sum(kk * kk)), 1e-12)
    k_mod = k * (1.0 + (a - 1.0) * ka_ref[...])
    bonus = head_sum(r * k_mod * rk_ref[...]) * v
    g_out[...] = g
    bonus_out[...] = bonus
    for h in range(heads):
        sl = slice(h * B_HEAD_DIM, (h + 1) * B_HEAD_DIM)
        r_out[0, h] = r[:, sl]
        k_out[0, h] = k_mod[:, sl]
        v_out[0, h] = v[:, sl]
        kkn_out[0, h] = kk_n[:, sl]
        b_out[0, h] = (kk_n * a)[:, sl]
        lwf_out[0, h] = lw_f[:, sl]
        lwb_out[0, h] = lw_b[:, sl]


def _rwkv_prep(p_b, mu, w0, w2, a0, a2, g2, k_k, k_a, r_k, batch, seq, ts=256):
    n, cols = p_b.shape
    width = a0.shape[0]
    heads = width // B_HEAD_DIM
    nblk = seq // ts
    rows8 = ts // 8
    head_id = np.arange(width) // B_HEAD_DIM
    ones = jnp.asarray(head_id[:, None] == head_id[None, :], F32)
    full = lambda a: pl.BlockSpec(a.shape, lambda b, i: (0,) * a.ndim)
    vec = lambda a: a.reshape(1, -1)
    args = [vec(mu), w0, w2.astype(BF16), vec(a0), a2.astype(BF16), g2.astype(BF16),
            vec(k_k), vec(k_a), vec(r_k), ones]
    hm = pl.BlockSpec((1, heads, ts, B_HEAD_DIM), lambda b, i: (b, 0, i, 0))
    tokm = pl.BlockSpec((ts, width), lambda b, i: (b * nblk + i, 0))
    hm_shape = jax.ShapeDtypeStruct((batch, heads, seq, B_HEAD_DIM), F32)
    tok_shape = jax.ShapeDtypeStruct((n, width), F32)
    return pl.pallas_call(
        _rwkv_prep_kernel,
        grid=(batch, nblk),
        in_specs=[
            pl.BlockSpec((ts, cols), lambda b, i: (b * nblk + i, 0)),
            pl.BlockSpec((8, cols), lambda b, i: (jnp.maximum((b * nblk + i) * rows8 - 1, 0), 0)),
            pl.BlockSpec((8, cols), lambda b, i: (jnp.minimum((b * nblk + i + 1) * rows8, n // 8 - 1), 0)),
        ] + [full(a) for a in args],
        out_specs=[hm] * 7 + [tokm] * 2,
        out_shape=[hm_shape] * 7 + [tok_shape] * 2,
        compiler_params=_params("parallel", "parallel"),
        name="rwkv_prep",
    )(p_b, p_b, p_b, *args)


def _rwkv_chunks(chains):
    c, d = chains[0][0].shape
    ti = lax.broadcasted_iota(jnp.int32, (c, c), 0)
    si = lax.broadcasted_iota(jnp.int32, (c, c), 1)
    eye = jnp.where(ti == si, 1.0, 0.0)
    masks = {rev: (((si > ti), (si >= ti)) if rev else ((si < ti), (si <= ti))) for rev in (False, True)}

    st = []
    for r, k, v, kk, b, lw, tt, rev in chains:
        cum = _chunk_cumsum(lw, c, rev)
        grow = jnp.exp(-cum)
        kt = kk * jnp.exp(cum - lw)
        rt = r * jnp.exp(cum)
        b_ = (b * grow).astype(BF16)
        kb = jnp.concatenate([(k * grow).astype(BF16), b_], axis=0)
        st.append(dict(v=v, vb=v.astype(BF16), kt=kt, rt=rt, b_=b_, kb=kb, tt=tt, ttb=tt.astype(BF16),
                       decay=jnp.exp(cum[0:1, :] if rev else cum[c - 1:c, :]), rev=rev,
                       lhs=jnp.concatenate([kt, rt], axis=0).astype(BF16)))
    for s in st:
        s["big"] = _dot_nt(s["lhs"], s["kb"])
    for s in st:
        strict, incl = masks[s["rev"]]
        big = s["big"]
        a_ab = jnp.where(strict, big[:c, c:], 0.0)
        s["a_ak"] = jnp.where(strict, big[:c, :c], 0.0).astype(BF16)
        s["p"] = jnp.concatenate([jnp.where(incl, big[c:, :c], 0.0), jnp.where(incl, -big[c:, c:], 0.0)],
                                 axis=1).astype(BF16)
        s["m"] = eye - a_ab
        s["ab"] = a_ab.astype(BF16)
    for s in st:
        s["aj"] = _dot(s["ab"], s["ab"])
        s["av"] = _dot(s["a_ak"], s["vb"])
    span = 2
    while span < c:
        span *= 2
        for s in st:
            ajb = s["aj"].astype(BF16)
            if span < c:
                both = _dot(jnp.concatenate([s["aj"], s["m"]], axis=0).astype(BF16), ajb)
                s["aj"], s["m"] = both[:c], s["m"] + both[c:]
            else:
                s["m"] = s["m"] + _dot(s["m"].astype(BF16), ajb)
    for s in st:
        s["wu"] = _dot(s["m"].astype(BF16), jnp.concatenate([s["kt"], s["av"]], axis=1).astype(BF16))
    for s in st:
        w_, u0, v = s["wu"][:, :d], s["wu"][:, d:], s["v"]
        rhs = jnp.concatenate([jnp.concatenate([v, jnp.zeros_like(v)], axis=1),
                               jnp.concatenate([u0, w_], axis=1)], axis=0).astype(BF16)
        s["yq"] = _dot(s["p"], rhs)
        s["h_t"] = _dot_tn(jnp.concatenate([v, -u0], axis=0).astype(BF16), s["kb"])
        s["bw"] = _dot_tn(s["b_"], w_.astype(BF16))
    out = []
    for s in st:
        q = s["rt"] + s["yq"][:, d:]
        y = _dot_nt(q.astype(BF16), s["ttb"]) + s["yq"][:, :d]
        tt = (s["tt"] - _dot_nt(s["ttb"], s["bw"].astype(BF16)) + s["h_t"]) * s["decay"]
        out.append((y, tt))
    return out


def _rwkv_scan_kernel(rf, kf, vf, kkf, bf, lwf, rb, kb, vb, kkb, bb, lwb, yf_ref, yb_ref, tf_ref, tb_ref):
    heads, ts = rf.shape[1], rf.shape[2]
    n_chunks = ts // B_CHUNK

    @pl.when(pl.program_id(1) == 0)
    def _():
        tf_ref[...] = jnp.zeros_like(tf_ref)
        tb_ref[...] = jnp.zeros_like(tb_ref)

    def body(j, carry):
        rows_f = pl.ds(pl.multiple_of(j * B_CHUNK, B_CHUNK), B_CHUNK)
        rows_b = pl.ds(pl.multiple_of((n_chunks - 1 - j) * B_CHUNK, B_CHUNK), B_CHUNK)
        chains = []
        for h in range(heads):
            chains.append([ref[0, h, rows_f, :] for ref in (rf, kf, vf, kkf, bf, lwf)] + [tf_ref[h], False])
            chains.append([ref[0, h, rows_b, :] for ref in (rb, kb, vb, kkb, bb, lwb)] + [tb_ref[h], True])
        res = _rwkv_chunks(chains)
        for h in range(heads):
            yf_ref[0, h, rows_f, :], tf_ref[h] = res[2 * h]
            yb_ref[0, h, rows_b, :], tb_ref[h] = res[2 * h + 1]
        return carry

    lax.fori_loop(0, n_chunks, body, 0)


def _rwkv_scan(r, k, v, kkn, b, lw_f, lw_b, ts=256):
    batch, heads, seq, d = r.shape
    nblk = seq // ts
    fwd = pl.BlockSpec((1, heads, ts, d), lambda bi, i: (bi, 0, i, 0))
    bwd = pl.BlockSpec((1, heads, ts, d), lambda bi, i: (bi, 0, nblk - 1 - i, 0))
    shape = jax.ShapeDtypeStruct(r.shape, F32)
    return pl.pallas_call(
        _rwkv_scan_kernel,
        grid=(batch, nblk),
        in_specs=[fwd] * 6 + [bwd] * 6,
        out_specs=[fwd, bwd],
        out_shape=[shape, shape],
        scratch_shapes=[pltpu.VMEM((heads, d, d), F32), pltpu.VMEM((heads, d, d), F32)],
        compiler_params=_params("parallel", "arbitrary"),
        name="rwkv_scan",
    )(r, k, v, kkn, b, lw_f, r, k, v, kkn, b, lw_b)


def _rwkv_post_kernel(yf_ref, yb_ref, g_ref, bonus_ref, lng_ref, lnb_ref, o_ref):
    heads = yf_ref.shape[1]
    outs = []
    for h in range(heads):
        y = yf_ref[0, h] + yb_ref[0, h]
        mean = jnp.mean(y, axis=-1, keepdims=True)
        var = jnp.mean(jnp.square(y - mean), axis=-1, keepdims=True)
        outs.append((y - mean) * lax.rsqrt(var + B_LN_EPS))
    yn = jnp.concatenate(outs, axis=1) * lng_ref[...] + lnb_ref[...]
    o_ref[...] = ((yn + bonus_ref[...]) * g_ref[...]).astype(o_ref.dtype)


def _rwkv_post(y_f, y_b, g, bonus, ln_g, ln_b, ts=256):
    batch, heads, seq, d = y_f.shape
    n, width = g.shape
    nblk = seq // ts
    hm = pl.BlockSpec((1, heads, ts, d), lambda b, i: (b, 0, i, 0))
    tokm = pl.BlockSpec((ts, width), lambda b, i: (b * nblk + i, 0))
    vec = pl.BlockSpec((1, width), lambda b, i: (0, 0))
    return pl.pallas_call(
        _rwkv_post_kernel,
        grid=(batch, nblk),
        in_specs=[hm, hm, tokm, tokm, vec, vec],
        out_specs=tokm,
        out_shape=jax.ShapeDtypeStruct((n, width), BF16),
        compiler_params=_params("parallel", "parallel"),
        name="rwkv_post",
    )(y_f, y_b, g, bonus, ln_g.reshape(1, width), ln_b.reshape(1, width))


def _rwkv7(p_b, mu, w0, w2, a0, a2, g2, k_k, k_a, r_k, ln_g, ln_b, batch, seq):
    r, k, v, kkn, b, lw_f, lw_b, g, bonus = _rwkv_prep(p_b, mu, w0, w2, a0, a2, g2, k_k, k_a, r_k, batch, seq)
    y_f, y_b = _rwkv_scan(r, k, v, kkn, b, lw_f, lw_b)
    return _rwkv_post(y_f, y_b, g, bonus, ln_g, ln_b)


_BUCKET_EDGES = (0, 1, 2, 3, 4, 5, 6, 7, 8, 12, 16, 23, 32, 46, 64, 91)
ATTN_TK = 128


def _bias_band_kernel(table_ref, o_ref):
    h = pl.program_id(0)
    n_d, tq, tk = o_ref.shape[1:]
    half = REL_BUCKETS // 2
    r = lax.broadcasted_iota(jnp.int32, (tq, tk), 0)
    c = lax.broadcasted_iota(jnp.int32, (tq, tk), 1)
    for d in range(n_d):
        rel = tk * (d - 1) + c - r
        n = jnp.abs(rel)
        vals = []
        for sign in range(2):
            val = jnp.full((tq, tk), table_ref[sign * half + half - 1, h], F32)
            for bkt in range(half - 2, -1, -1):
                val = jnp.where(n < _BUCKET_EDGES[bkt + 1], table_ref[sign * half + bkt, h], val)
            vals.append(val)
        o_ref[0, d] = jnp.where(rel > 0, vals[1], vals[0])


def _bias_band(rel_table, tq):
    heads = rel_table.shape[1]
    n_d = tq // ATTN_TK + 2
    return pl.pallas_call(
        _bias_band_kernel,
        grid=(heads,),
        in_specs=[pl.BlockSpec(memory_space=pltpu.SMEM)],
        out_specs=pl.BlockSpec((1, n_d, tq, ATTN_TK), lambda h: (h, 0, 0, 0)),
        out_shape=jax.ShapeDtypeStruct((heads, n_d, tq, ATTN_TK), F32),
        compiler_params=_params("parallel"),
        name="bias_band",
    )(rel_table)


def _diff_attn_kernel(table_ref, lam_ref, q_ref, k_ref, v_ref, band_ref, g_ref, o_ref, lg_ref, *, lam_init):
    h = pl.program_id(1)
    i = pl.program_id(2)
    tq, dh2 = q_ref.shape
    seq = k_ref.shape[0]
    n_d = band_ref.shape[1]
    n_kt = seq // ATTN_TK
    ratio = tq // ATTN_TK
    half = REL_BUCKETS // 2
    k = k_ref[...]
    v = v_ref[...]
    lane = lax.broadcasted_iota(jnp.int32, (1, dh2), 1)
    q = q_ref[...] * jnp.asarray(C_HEAD_DIM ** -0.5, q_ref.dtype)
    zero = jnp.zeros_like(q)
    first_band = ratio * i - 1
    col = lax.broadcasted_iota(jnp.int32, (1, seq), 1)
    far_row = jnp.where(col < first_band * ATTN_TK, table_ref[half - 1, h],
                        jnp.where(col >= (first_band + n_d) * ATTN_TK, table_ref[REL_BUCKETS - 1, h],
                                  -jnp.inf))
    row_max = []
    for m in range(2):
        qm = jnp.where((lane >= m * C_HEAD_DIM) & (lane < (m + 1) * C_HEAD_DIM), q, zero)
        far = _dot_nt(qm, k) + far_row
        mx = jnp.max(far, axis=-1, keepdims=True)
        lg_ref[m] = far
        for d in (0, n_d - 1) + tuple(range(1, n_d - 1)):
            kt = first_band + d
            valid = (kt >= 0) & (kt < n_kt)
            c0 = pl.multiple_of(jnp.clip(kt, 0, n_kt - 1) * ATTN_TK, ATTN_TK)
            near = _dot_nt(qm, k_ref[pl.ds(c0, ATTN_TK), :]) + band_ref[0, d]
            mx = jnp.maximum(mx, jnp.where(valid, jnp.max(near, axis=-1, keepdims=True), -jnp.inf))
            lg_ref[m, :, pl.ds(c0, ATTN_TK)] = near
        row_max.append(mx)
    es = [jnp.exp(lg_ref[m] - row_max[m]) for m in range(2)]
    parts = [_dot(e.astype(BF16), v) * (1.0 / jnp.sum(e, axis=-1, keepdims=True)) for e in es]
    o = parts[0] - lam_ref[0] * parts[1]
    o = o * lax.rsqrt(jnp.mean(o * o, axis=-1, keepdims=True) + NORM_EPS) * g_ref[...]
    o_ref[...] = (o * (1.0 - lam_init)).astype(o_ref.dtype)


def _diff_attention(qkv, rel_table, lam_full, subln_g, lam_init, batch, seq, tq=256):
    n = qkv.shape[0]
    dh2 = 2 * C_HEAD_DIM
    heads = qkv.shape[1] // (3 * dh2)
    nq = seq // tq
    band = _bias_band(rel_table, tq)
    return pl.pallas_call(
        functools.partial(_diff_attn_kernel, lam_init=lam_init),
        grid=(batch, heads, nq),
        in_specs=[
            pl.BlockSpec(memory_space=pltpu.SMEM),
            pl.BlockSpec(memory_space=pltpu.SMEM),
            pl.BlockSpec((tq, dh2), lambda b, h, i: (b * nq + i, h)),
            pl.BlockSpec((seq, dh2), lambda b, h, i: (b, heads + h)),
            pl.BlockSpec((seq, dh2), lambda b, h, i: (b, 2 * heads + h)),
            pl.BlockSpec((1,) + band.shape[1:], lambda b, h, i: (h, 0, 0, 0)),
            pl.BlockSpec((1, dh2), lambda b, h, i: (0, 0)),
        ],
        out_specs=pl.BlockSpec((tq, dh2), lambda b, h, i: (b * nq + i, h)),
        out_shape=jax.ShapeDtypeStruct((n, heads * dh2), BF16),
        scratch_shapes=[pltpu.VMEM((2, tq, seq), F32)],
        compiler_params=_params("parallel", "parallel", "parallel"),
        name="diff_attention",
    )(rel_table, lam_full.reshape(1), qkv, qkv, qkv, band, subln_g.reshape(1, dh2))


MOE_TILE = 256
_HI_MASK = np.uint32(0xFFFF0000)


def _pack_bf16_pair(lo, hi):
    as_bits = lambda t: lax.bitcast_convert_type(t.astype(BF16).astype(F32), jnp.uint32)
    return (as_bits(hi) & _HI_MASK) | (as_bits(lo) >> 16)


def _unpack_bf16_pair(word):
    lo = lax.bitcast_convert_type(word << 16, F32).astype(BF16)
    hi = lax.bitcast_convert_type(word & _HI_MASK, F32).astype(BF16)
    return lo, hi


def _router_kernel(x_ref, g_ref, sh_ref, sc_ref, rw_ref, rb_ref, hp_ref, idx_ref, w_ref, rank_ref, cnt_ref,
                   run_ref, *, n_experts):
    @pl.when(pl.program_id(0) == 0)
    def _():
        run_ref[...] = jnp.zeros_like(run_ref)

    tm, d = x_ref.shape
    h = _modulated_norm(x_ref[...], g_ref[...], sh_ref[0], sc_ref[0])
    hp_ref[...] = _pack_bf16_pair(h[:, :d // 2], h[:, d // 2:])
    logits = jnp.dot(h, rw_ref[...], precision=HIGHEST, preferred_element_type=F32) + rb_ref[...]
    lane = lax.broadcasted_iota(jnp.int32, logits.shape, 1).astype(F32)
    neg = jnp.float32(-jnp.inf)
    logits = jnp.where(lane < n_experts, logits, neg)
    picks, vals = [], []
    for _ in range(TOP_K):
        m = jnp.max(logits, axis=-1, keepdims=True)
        first = jnp.min(jnp.where(logits == m, lane, float(LANES)), axis=-1, keepdims=True)
        hit = lane == first
        picks.append((first, hit))
        vals.append(m)
        logits = jnp.where(hit, neg, logits)
    es = [jnp.exp(v - vals[0]) for v in vals]
    inv = 1.0 / (es[0] + es[1] + es[2] + es[3])
    assigned = picks[0][1] | picks[1][1] | picks[2][1] | picks[3][1]
    onehot = jnp.where(assigned, 1.0, 0.0)
    ri = lax.broadcasted_iota(jnp.int32, (tm, tm), 0)
    ci = lax.broadcasted_iota(jnp.int32, (tm, tm), 1)
    earlier = jnp.where(ci < ri, 1.0, 0.0).astype(BF16)
    before = _dot(earlier, onehot.astype(BF16)) + run_ref[...]
    idx_o = jnp.zeros(logits.shape, F32)
    w_o = jnp.zeros(logits.shape, F32)
    rank_o = jnp.zeros(logits.shape, F32)
    for j, (first, hit) in enumerate(picks):
        rank_j = jnp.sum(jnp.where(hit, before, 0.0), axis=-1, keepdims=True)
        idx_o = jnp.where(lane == j, first, idx_o)
        w_o = jnp.where(lane == j, es[j] * inv, w_o)
        rank_o = jnp.where(lane == j, rank_j, rank_o)
    idx_ref[...] = idx_o
    w_ref[...] = w_o
    rank_ref[...] = rank_o
    run_ref[...] += jnp.sum(onehot, axis=0, keepdims=True)
    cnt_ref[...] = run_ref[...]


def _router(x2, g, mod_l, shift_idx, seq, router_w, router_b):
    n, d = x2.shape
    tm = MOE_TILE
    n_experts = router_w.shape[1]
    per_batch = seq // tm
    rw = jnp.zeros((d, LANES), F32).at[:, :n_experts].set(router_w)
    rb = jnp.zeros((1, LANES), F32).at[0, :n_experts].set(router_b)
    lane_out = pl.BlockSpec((tm, LANES), lambda i: (i, 0))
    lane_shape = jax.ShapeDtypeStruct((n, LANES), F32)
    return pl.pallas_call(
        functools.partial(_router_kernel, n_experts=n_experts),
        grid=(n // tm,),
        in_specs=[
            pl.BlockSpec((tm, d), lambda i: (i, 0)),
            pl.BlockSpec((1, d), lambda i: (0, 0)),
            pl.BlockSpec((1, 1, d), lambda i: (i // per_batch, 0, shift_idx)),
            pl.BlockSpec((1, 1, d), lambda i: (i // per_batch, 0, shift_idx + 1)),
            pl.BlockSpec((d, LANES), lambda i: (0, 0)),
            pl.BlockSpec((1, LANES), lambda i: (0, 0)),
        ],
        out_specs=[pl.BlockSpec((tm, d // 2), lambda i: (i, 0)), lane_out, lane_out, lane_out,
                   pl.BlockSpec((1, LANES), lambda i: (0, 0))],
        out_shape=[jax.ShapeDtypeStruct((n, d // 2), jnp.uint32), lane_shape, lane_shape, lane_shape,
                   jax.ShapeDtypeStruct((1, LANES), F32)],
        scratch_shapes=[pltpu.VMEM((1, LANES), F32)],
        compiler_params=_params("arbitrary"),
        name="moe_router",
    )(x2, g.reshape(1, d), mod_l, mod_l, rw, rb)


def _dispatch_kernel(pos_ref, h_ref, zeros_ref, xs_ref, sem):
    del zeros_ref
    tm = h_ref.shape[0]

    def row_copy(r, p):
        return pltpu.make_async_copy(h_ref.at[pl.ds(r, 1)], xs_ref.at[pl.ds(p, 1)], sem)

    def issue(r, carry):
        for j in range(TOP_K):
            row_copy(r, pos_ref[r * TOP_K + j]).start(priority=j % 2)
        return carry

    lax.fori_loop(0, tm, issue, 0)

    def drain(r, carry):
        for j in range(TOP_K):
            row_copy(r, pos_ref[r * TOP_K + j]).wait()
        return carry

    lax.fori_loop(0, tm, drain, 0)


def _dispatch(hp, pos_flat, n_rows):
    n, dw = hp.shape
    tm = MOE_TILE
    zeros = jnp.zeros((n_rows, dw), jnp.uint32)
    return pl.pallas_call(
        _dispatch_kernel,
        grid=(n // tm,),
        in_specs=[
            pl.BlockSpec((tm * TOP_K,), lambda i: (i,), memory_space=pltpu.SMEM),
            pl.BlockSpec((tm, dw), lambda i: (i, 0)),
            pl.BlockSpec(memory_space=pl.ANY),
        ],
        out_specs=pl.BlockSpec(memory_space=pl.ANY),
        out_shape=jax.ShapeDtypeStruct((n_rows, dw), jnp.uint32),
        scratch_shapes=[pltpu.SemaphoreType.DMA(())],
        input_output_aliases={2: 0},
        compiler_params=_params("arbitrary"),
        name="moe_dispatch",
    )(pos_flat, hp, zeros)


GLU_TILE = 2 * LANES


def _experts_kernel(te_ref, nu_ref, slot_ref, nxt_ref, xs_ref, b1_ref, b2_ref, perm_ref, w1_hbm, w2_hbm,
                    ys_ref, w1buf, w2buf, w1p_ref, w2b_ref, sem, *, layer):
    t = pl.program_id(0)
    used = t < nu_ref[0]
    new_expert = (t == 0) | (te_ref[t] != te_ref[jnp.maximum(t - 1, 0)])
    n_col_tiles = w1p_ref.shape[1] // GLU_TILE

    def weight_copies(e, slot):
        return (pltpu.make_async_copy(w1_hbm.at[layer, e], w1buf.at[slot], sem.at[0, slot]),
                pltpu.make_async_copy(w2_hbm.at[layer, e], w2buf.at[slot], sem.at[1, slot]))

    @pl.when(used & new_expert)
    def _():
        slot = slot_ref[t]

        @pl.when(t == 0)
        def _():
            for cp in weight_copies(te_ref[t], slot):
                cp.start()

        for cp in weight_copies(te_ref[t], slot):
            cp.wait()

        @pl.when(nxt_ref[t] >= 0)
        def _():
            for cp in weight_copies(nxt_ref[t], 1 - slot):
                cp.start()

        for c in range(n_col_tiles):
            cols = slice(c * GLU_TILE, (c + 1) * GLU_TILE)
            w1p_ref[:, cols] = _dot(w1buf[slot, :, cols].astype(BF16), perm_ref[...]).astype(BF16)
        w2b_ref[...] = w2buf[slot].astype(BF16)

    @pl.when(used)
    def _():
        lo, hi = _unpack_bf16_pair(xs_ref[...])
        x = jnp.concatenate([lo, hi], axis=1)
        hh = _dot(x, w1p_ref[...]) + b1_ref[0]
        acts = []
        for c in range(n_col_tiles):
            x_glu = jnp.minimum(hh[:, c * GLU_TILE:c * GLU_TILE + LANES], SWIGLU_LIMIT)
            x_lin = jnp.clip(hh[:, c * GLU_TILE + LANES:(c + 1) * GLU_TILE], -SWIGLU_LIMIT, SWIGLU_LIMIT)
            acts.append((x_glu * _sigmoid(SWIGLU_ALPHA * x_glu) * (x_lin + 1.0)).astype(BF16))
        out = _dot(jnp.concatenate(acts, axis=1), w2b_ref[...]) + b2_ref[0]
        half = out.shape[1] // 2
        ys_ref[...] = _pack_bf16_pair(out[:, :half], out[:, half:])

    @pl.when(t >= nu_ref[0])
    def _():
        ys_ref[...] = jnp.zeros_like(ys_ref)


def _experts(xs, tile_expert, n_used, slot, nxt, w1_all, b1, w2_all, b2, layer):
    n_rows, dw = xs.shape
    tm = MOE_TILE
    _, n_exp, d, de2 = w1_all.shape
    src = np.arange(GLU_TILE)
    dst = np.where(src % 2 == 0, src // 2, LANES + src // 2)
    perm = jnp.asarray(dst[:, None] == np.arange(GLU_TILE)[None, :], BF16)
    b1p = b1.reshape(n_exp, de2 // GLU_TILE, LANES, 2).transpose(0, 1, 3, 2).reshape(n_exp, 1, de2)
    row = lambda t, te, nu, sl, nx: (jnp.maximum(jnp.minimum(t, nu[0] - 1), 0), 0)
    exp3 = lambda t, te, nu, sl, nx: (te[t], 0, 0)
    grid_spec = pltpu.PrefetchScalarGridSpec(
        num_scalar_prefetch=4,
        grid=(n_rows // tm,),
        in_specs=[
            pl.BlockSpec((tm, dw), row),
            pl.BlockSpec((1, 1, de2), exp3),
            pl.BlockSpec((1, 1, d), exp3),
            pl.BlockSpec((GLU_TILE, GLU_TILE), lambda t, te, nu, sl, nx: (0, 0)),
            pl.BlockSpec(memory_space=pl.ANY),
            pl.BlockSpec(memory_space=pl.ANY),
        ],
        out_specs=pl.BlockSpec((tm, d // 2), lambda t, te, nu, sl, nx: (t, 0)),
        scratch_shapes=[pltpu.VMEM((2, d, de2), F32), pltpu.VMEM((2, de2 // 2, d), F32),
                        pltpu.VMEM((d, de2), BF16), pltpu.VMEM((de2 // 2, d), BF16),
                        pltpu.SemaphoreType.DMA((2, 2))],
    )
    return pl.pallas_call(
        functools.partial(_experts_kernel, layer=layer),
        grid_spec=grid_spec,
        out_shape=jax.ShapeDtypeStruct((n_rows, d // 2), jnp.uint32),
        compiler_params=_params("arbitrary"),
        name="moe_experts",
    )(tile_expert, n_used, slot, nxt, xs, b1p, b2.reshape(n_exp, 1, d), perm, w1_all, w2_all)


def _combine_kernel(pos_ref, x_ref, gate_ref, w_ref, fg_ref, ys_ref, o_ref, buf, sem, *, final_norm):
    tm = x_ref.shape[0]

    def row_copy(r, j):
        return pltpu.make_async_copy(ys_ref.at[pl.ds(pos_ref[r * TOP_K + j], 1)], buf.at[j, pl.ds(r, 1)], sem)

    def issue(r, carry):
        for j in range(TOP_K):
            row_copy(r, j).start(priority=j % 2)
        return carry

    lax.fori_loop(0, tm, issue, 0)

    def drain(r, carry):
        for j in range(TOP_K):
            row_copy(r, j).wait()
        return carry

    lax.fori_loop(0, tm, drain, 0)
    w = w_ref[...]
    y_lo = y_hi = None
    for j in range(TOP_K):
        lo, hi = _unpack_bf16_pair(buf[j])
        wj = w[:, j:j + 1]
        y_lo = wj * lo if j == 0 else y_lo + wj * lo
        y_hi = wj * hi if j == 0 else y_hi + wj * hi
    y = jnp.concatenate([y_lo, y_hi], axis=1)
    out = x_ref[...] + gate_ref[0] * y
    if final_norm:
        out = out * lax.rsqrt(jnp.mean(out * out, axis=-1, keepdims=True) + NORM_EPS) * fg_ref[...]
    o_ref[...] = out


def _combine(x2, mod_l, gate_idx, seq, top_w, pos_flat, ys, final_g, final_norm):
    n, d = x2.shape
    tm = MOE_TILE
    per_batch = seq // tm
    return pl.pallas_call(
        functools.partial(_combine_kernel, final_norm=final_norm),
        grid=(n // tm,),
        in_specs=[
            pl.BlockSpec((tm * TOP_K,), lambda i: (i,), memory_space=pltpu.SMEM),
            pl.BlockSpec((tm, d), lambda i: (i, 0)),
            pl.BlockSpec((1, 1, d), lambda i: (i // per_batch, 0, gate_idx)),
            pl.BlockSpec((tm, LANES), lambda i: (i, 0)),
            pl.BlockSpec((1, d), lambda i: (0, 0)),
            pl.BlockSpec(memory_space=pl.ANY),
        ],
        out_specs=pl.BlockSpec((tm, d), lambda i: (i, 0)),
        out_shape=jax.ShapeDtypeStruct((n, d), F32),
        scratch_shapes=[pltpu.VMEM((TOP_K, tm, d // 2), jnp.uint32), pltpu.SemaphoreType.DMA(())],
        compiler_params=_params("arbitrary"),
        name="moe_combine",
    )(pos_flat, x2, mod_l, top_w, final_g.reshape(1, d), ys)


SC_GATHER_WINDOW = 128
SC_GATHER_SPLIT = 2


def _sc_gather_rows(table, idx):
    n_out, full_width = idx.shape[0], table.shape[1]
    table = table.reshape(table.shape[0] * SC_GATHER_SPLIT, full_width // SC_GATHER_SPLIT)
    idx = (idx[:, None] * SC_GATHER_SPLIT + jnp.arange(SC_GATHER_SPLIT, dtype=jnp.int32)).reshape(-1)
    n_idx = idx.shape[0]
    width = table.shape[1]
    sc = plsc.get_sparse_core_info()
    steps = n_idx // SC_GATHER_WINDOW
    per_core = steps // sc.num_cores
    mesh = plsc.VectorSubcoreMesh(core_axis_name="core", subcore_axis_name="subcore")

    @functools.partial(pl.kernel, out_type=jax.ShapeDtypeStruct((n_idx, width), table.dtype), mesh=mesh,
                       name="sc_gather_rows")
    def gather(table_hbm, idx_hbm, out_hbm):
        def body(idx_vmem, out_vmem):
            pltpu.sync_copy(table_hbm.at[idx_vmem.at[0]], out_vmem)

        pltpu.emit_pipeline(
            body,
            grid=(sc.num_cores, per_core),
            in_specs=[pl.BlockSpec((1, SC_GATHER_WINDOW), index_map=lambda c, i: (0, c * per_core + i))],
            out_specs=[pl.BlockSpec((SC_GATHER_WINDOW, width), index_map=lambda c, i: (c * per_core + i, 0))],
            core_axis_name=("core", "subcore"),
            dimension_semantics=(pltpu.PARALLEL, pltpu.PARALLEL),
        )(idx_hbm, out_hbm)

    return gather(table, idx.reshape(1, n_idx)).reshape(n_out, full_width)


def _combine_dense_kernel(x_ref, gate_ref, w_ref, fg_ref, y4_ref, o_ref, *, final_norm):
    half = y4_ref.shape[1] // TOP_K
    w = w_ref[...]
    y_lo = y_hi = None
    for j in range(TOP_K):
        lo, hi = _unpack_bf16_pair(y4_ref[:, j * half:(j + 1) * half])
        wj = w[:, j:j + 1]
        y_lo = wj * lo if j == 0 else y_lo + wj * lo
        y_hi = wj * hi if j == 0 else y_hi + wj * hi
    out = x_ref[...] + gate_ref[0] * jnp.concatenate([y_lo, y_hi], axis=1)
    if final_norm:
        out = out * lax.rsqrt(jnp.mean(out * out, axis=-1, keepdims=True) + NORM_EPS) * fg_ref[...]
    o_ref[...] = out


def _combine_dense(x2, mod_l, gate_idx, seq, top_w, y4, final_g, final_norm, tm=512):
    n, d = x2.shape
    per_batch = seq // tm
    return pl.pallas_call(
        functools.partial(_combine_dense_kernel, final_norm=final_norm),
        grid=(n // tm,),
        in_specs=[
            pl.BlockSpec((tm, d), lambda i: (i, 0)),
            pl.BlockSpec((1, 1, d), lambda i: (i // per_batch, 0, gate_idx)),
            pl.BlockSpec((tm, LANES), lambda i: (i, 0)),
            pl.BlockSpec((1, d), lambda i: (0, 0)),
            pl.BlockSpec((tm, y4.shape[1]), lambda i: (i, 0)),
        ],
        out_specs=pl.BlockSpec((tm, d), lambda i: (i, 0)),
        out_shape=jax.ShapeDtypeStruct((n, d), F32),
        compiler_params=_params("parallel"),
        name="moe_combine",
    )(x2, mod_l, top_w, final_g.reshape(1, d), y4)


def _moe_ffn(x2, norm_g, mod_l, seq, router_w, router_b, w1_all, b1, w2_all, b2, layer, final_g, final_norm):
    n, d = x2.shape
    n_exp = router_w.shape[1]
    tile = MOE_TILE
    hp, idx_f, top_w, rank_f, counts = _router(x2, norm_g, mod_l, 3, seq, router_w, router_b)
    counts = counts[0, :n_exp].astype(jnp.int32)
    padded = (counts + tile - 1) // tile * tile
    ends = jnp.cumsum(padded)
    starts = ends - padded
    idx = idx_f[:, :TOP_K].astype(jnp.int32)
    expert_ids = jnp.arange(n_exp, dtype=jnp.int32)
    start_of = jnp.sum(jnp.where(idx[..., None] == expert_ids, starts, 0), axis=-1)
    pos_flat = (start_of + rank_f[:, :TOP_K].astype(jnp.int32)).reshape(-1)
    n_rows = n * TOP_K + n_exp * tile
    n_tiles = n_rows // tile
    tile_start = jnp.arange(n_tiles, dtype=jnp.int32) * tile
    tile_expert = jnp.sum((ends[None, :] <= tile_start[:, None]).astype(jnp.int32), axis=1)
    tile_expert = jnp.minimum(tile_expert, n_exp - 1)
    n_used = (ends[-1:] // tile).astype(jnp.int32)
    nonempty = counts > 0
    ordinal = jnp.cumsum(nonempty.astype(jnp.int32)) - 1
    later = nonempty[None, :] & (expert_ids[None, :] > expert_ids[:, None])
    nxt_e = jnp.min(jnp.where(later, expert_ids[None, :], n_exp), axis=1)
    nxt_e = jnp.where(nxt_e == n_exp, -1, nxt_e)
    slot = (ordinal[tile_expert] % 2).astype(jnp.int32)
    nxt = nxt_e[tile_expert].astype(jnp.int32)
    token_of = jnp.arange(n * TOP_K, dtype=jnp.int32) // TOP_K
    src = jnp.zeros((n_rows,), jnp.int32).at[pos_flat].set(token_of, unique_indices=True)
    xs = _sc_gather_rows(hp, src)
    ys = _experts(xs, tile_expert, n_used, slot, nxt, w1_all, b1, w2_all, b2, layer)
    y4 = _sc_gather_rows(ys, pos_flat).reshape(n, TOP_K * (d // 2))
    return _combine_dense(x2, mod_l, 5, seq, top_w, y4, final_g, final_norm)


def kernel(x, c, ada_w, ada_b, norm_mix_g, norm_ffn_g, router_w, router_b, moe_w1, moe_b1, moe_w2, moe_b2, ab_w_in, ab_w_out, hgrn_lb, hgrn_norm_g, rwkv_mu, rwkv_w0, rwkv_w2, rwkv_a0, rwkv_a2, rwkv_g2, rwkv_k_k, rwkv_k_a, rwkv_r_k, rwkv_ln_g, rwkv_ln_b, attn_w_in, attn_w_out, attn_lambda, attn_subln_g, rel_bias_table, final_norm_g):
    batch, seq, d = x.shape
    n = batch * seq
    depth = ada_w.shape[0]
    x2 = x.reshape(n, d)
    mod = _adaln(c, ada_w, ada_b)
    lb_all = jnp.cumsum(jax.nn.softmax(hgrn_lb.astype(F32), axis=1), axis=1)
    for layer in range(depth):
        mod_l = mod[layer].reshape(batch, 1, 6 * d)
        j = layer // 2
        if layer % 2 == 0:
            a_cols = 5 * (d // 2)
            w_in = ab_w_in[j].astype(BF16)
            p_a, p_b = _normmod_proj(x2, norm_mix_g[layer], mod_l, 0, seq,
                                     [w_in[:, :a_cols], w_in[:, a_cols:]], [F32, F32])
            y_a = _hgrn2(p_a, lb_all[:, j], hgrn_norm_g[j], batch, seq)
            y_b = _rwkv7(p_b, rwkv_mu[j], rwkv_w0[j], rwkv_w2[j], rwkv_a0[j], rwkv_a2[j], rwkv_g2[j],
                         rwkv_k_k[j], rwkv_k_a[j], rwkv_r_k[j], rwkv_ln_g[j], rwkv_ln_b[j], batch, seq)
            w_out = ab_w_out[j].astype(BF16)
            x2 = _outproj_residual(x2, mod_l, 2, seq, [y_a, y_b], [w_out[:d // 2], w_out[d // 2:]])
        else:
            (qkv,) = _normmod_proj(x2, norm_mix_g[layer], mod_l, 0, seq, [attn_w_in[j].astype(BF16)], [BF16])
            lam = attn_lambda[j].astype(F32)
            lam_init = 0.8 - 0.6 * math.exp(-0.3 * layer)
            lam_full = jnp.exp(jnp.sum(lam[0] * lam[1])) - jnp.exp(jnp.sum(lam[2] * lam[3])) + lam_init
            o = _diff_attention(qkv, rel_bias_table, lam_full, attn_subln_g[j], lam_init, batch, seq)
            x2 = _outproj_residual(x2, mod_l, 2, seq, [o], [attn_w_out[j].astype(BF16)])
        x2 = _moe_ffn(x2, norm_ffn_g[layer], mod_l, seq, router_w[layer], router_b[layer], moe_w1,
                      moe_b1[layer], moe_w2, moe_b2[layer], layer, final_norm_g, layer == depth - 1)
    return x2.reshape(batch, seq, d)
```

```python
import functools
import math

import jax
import jax.numpy as jnp
import numpy as np
from jax import lax
from jax.experimental import pallas as pl
from jax.experimental.pallas import tpu as pltpu
from jax.experimental.pallas import tpu_sc as plsc

F32 = jnp.float32
BF16 = jnp.bfloat16
HIGHEST = lax.Precision.HIGHEST

NORM_EPS = 1e-6

A_HEAD_DIM = 128
A_CHUNK = 32
A_GROUP = 128
B_HEAD_DIM = 64
B_CHUNK = 64
B_LN_EPS = 1e-5 * B_HEAD_DIM
C_HEAD_DIM = 64
REL_BUCKETS = 32
REL_MAX_DISTANCE = 128
TOP_K = 4
SWIGLU_LIMIT = 7.0
SWIGLU_ALPHA = 1.702

V7X_VMEM_BYTES = 64 * 1024 * 1024
VMEM_LIMIT = V7X_VMEM_BYTES - 8 * 1024 * 1024
LANES = 128


def _params(*sem):
    return pltpu.CompilerParams(dimension_semantics=sem, vmem_limit_bytes=VMEM_LIMIT)


def _sigmoid(x):
    return 1.0 / (1.0 + jnp.exp(-x))


def _silu(x):
    return x * _sigmoid(x)


def _dot(a, b):
    return jnp.dot(a, b, preferred_element_type=F32)


def _dot_nt(a, b):
    return lax.dot_general(a, b, (((1,), (1,)), ((), ())), preferred_element_type=F32)


def _dot_tn(a, b):
    return lax.dot_general(a, b, (((0,), (0,)), ((), ())), preferred_element_type=F32)


def _adaln_kernel(c_ref, w_ref, b_ref, o_ref):
    cond = _silu(c_ref[...])
    o_ref[0] = jnp.dot(cond, w_ref[0], precision=HIGHEST, preferred_element_type=F32) + b_ref[0]


def _adaln(c, ada_w, ada_b):
    n_layers, d, n_out = ada_w.shape
    batch = c.shape[0]
    tn = 1536
    return pl.pallas_call(
        _adaln_kernel,
        grid=(n_layers, n_out // tn),
        in_specs=[
            pl.BlockSpec((batch, d), lambda l, j: (0, 0)),
            pl.BlockSpec((1, d, tn), lambda l, j: (l, 0, j)),
            pl.BlockSpec((1, 1, tn), lambda l, j: (l, 0, j)),
        ],
        out_specs=pl.BlockSpec((1, batch, tn), lambda l, j: (l, 0, j)),
        out_shape=jax.ShapeDtypeStruct((n_layers, batch, n_out), F32),
        compiler_params=_params("parallel", "parallel"),
        name="adaln",
    )(c, ada_w, ada_b.reshape(n_layers, 1, n_out))


def _modulated_norm(x, g, shift, scale):
    y = x * lax.rsqrt(jnp.mean(x * x, axis=-1, keepdims=True) + NORM_EPS)
    return (y * g) * (1.0 + scale) + shift


def _normmod_proj_kernel(x_ref, g_ref, sh_ref, sc_ref, *rest, n_w):
    w_refs, o_refs = rest[:n_w], rest[n_w:]
    h = _modulated_norm(x_ref[...], g_ref[...], sh_ref[0], sc_ref[0]).astype(BF16)
    for w_ref, o_ref in zip(w_refs, o_refs):
        o_ref[...] = _dot(h, w_ref[...]).astype(o_ref.dtype)


def _normmod_proj(x2, g, mod_l, shift_idx, seq, weights, out_dtypes, tm=256):
    n, d = x2.shape
    per_batch = seq // tm
    in_specs = [
        pl.BlockSpec((tm, d), lambda i: (i, 0)),
        pl.BlockSpec((1, d), lambda i: (0, 0)),
        pl.BlockSpec((1, 1, d), lambda i: (i // per_batch, 0, shift_idx)),
        pl.BlockSpec((1, 1, d), lambda i: (i // per_batch, 0, shift_idx + 1)),
    ]
    in_specs += [pl.BlockSpec(w.shape, lambda i: (0, 0)) for w in weights]
    out_specs = [pl.BlockSpec((tm, w.shape[1]), lambda i: (i, 0)) for w in weights]
    out_shape = [jax.ShapeDtypeStruct((n, w.shape[1]), dt) for w, dt in zip(weights, out_dtypes)]
    return pl.pallas_call(
        functools.partial(_normmod_proj_kernel, n_w=len(weights)),
        grid=(n // tm,),
        in_specs=in_specs,
        out_specs=out_specs,
        out_shape=out_shape,
        compiler_params=_params("parallel"),
        name="normmod_proj",
    )(x2, g.reshape(1, d), mod_l, mod_l, *weights)


def _outproj_kernel(x_ref, gate_ref, *rest, n_y):
    y_refs, w_refs, o_ref = rest[:n_y], rest[n_y:2 * n_y], rest[2 * n_y]
    acc = _dot(y_refs[0][...], w_refs[0][...])
    for y_ref, w_ref in zip(y_refs[1:], w_refs[1:]):
        acc += _dot(y_ref[...], w_ref[...])
    o_ref[...] = x_ref[...] + gate_ref[0] * acc


def _outproj_residual(x2, mod_l, gate_idx, seq, ys, ws, tm=512):
    n, d = x2.shape
    per_batch = seq // tm
    in_specs = [
        pl.BlockSpec((tm, d), lambda i: (i, 0)),
        pl.BlockSpec((1, 1, d), lambda i: (i // per_batch, 0, gate_idx)),
    ]
    in_specs += [pl.BlockSpec((tm, y.shape[1]), lambda i: (i, 0)) for y in ys]
    in_specs += [pl.BlockSpec(w.shape, lambda i: (0, 0)) for w in ws]
    return pl.pallas_call(
        functools.partial(_outproj_kernel, n_y=len(ys)),
        grid=(n // tm,),
        in_specs=in_specs,
        out_specs=pl.BlockSpec((tm, d), lambda i: (i, 0)),
        out_shape=jax.ShapeDtypeStruct((n, d), F32),
        compiler_params=_params("parallel"),
        name="outproj_residual",
    )(x2, mod_l, *ys, *ws)


def _chunk_cumsum(x, chunk, reverse):
    rows = x.shape[0]
    pos = lax.broadcasted_iota(jnp.int32, x.shape, 0) % chunk
    s = 1
    while s < chunk:
        if reverse:
            x = x + jnp.where(pos < chunk - s, pltpu.roll(x, rows - s, axis=0), 0.0)
        else:
            x = x + jnp.where(pos >= s, pltpu.roll(x, s, axis=0), 0.0)
        s *= 2
    return x


def _hgrn_groups(slabs):
    g_rows, dk = slabs[0][0].shape
    n_chunks = g_rows // A_CHUNK
    ti = lax.broadcasted_iota(jnp.int32, (A_CHUNK, A_CHUNK), 0)
    si = lax.broadcasted_iota(jnp.int32, (A_CHUNK, A_CHUNK), 1)
    chunk_rows = [slice(c * A_CHUNK, (c + 1) * A_CHUNK) for c in range(n_chunks)]
    prep = []
    for q, f, v, lb, st, reverse in slabs:
        fg = lb + (1.0 - lb) * _sigmoid(f)
        k = 1.0 - fg
        b = _chunk_cumsum(jnp.log(fg), A_CHUNK, reverse)
        b3 = b.reshape(n_chunks, A_CHUNK, dk)
        edge = b3[:, 0:1, :] if reverse else b3[:, A_CHUNK - 1:A_CHUNK, :]
        prep.append(dict(
            q_in=(q * jnp.exp(b)).astype(BF16), k_in=(k * jnp.exp(-b)).astype(BF16),
            k_st=(k.reshape(n_chunks, A_CHUNK, dk) * jnp.exp(edge - b3)).astype(BF16),
            decay=jnp.exp(edge), vb=v.astype(BF16), st=st, reverse=reverse,
            mask=(si >= ti) if reverse else (si <= ti)))
    for p in prep:
        p["scores"] = [_dot_nt(p["q_in"][r], p["k_in"][r]) for r in chunk_rows]
        p["dstate"] = [_dot_tn(p["vb"][r], p["k_st"][c]) for c, r in enumerate(chunk_rows)]
    for p in prep:
        p["intra"] = [_dot(jnp.where(p["mask"], s, 0.0).astype(BF16), p["vb"][r])
                      for s, r in zip(p["scores"], chunk_rows)]
    results = []
    for p in prep:
        st = p["st"]
        outs = [None] * n_chunks
        for c in (range(n_chunks - 1, -1, -1) if p["reverse"] else range(n_chunks)):
            outs[c] = p["intra"][c] + _dot_nt(p["q_in"][chunk_rows[c]], st.astype(BF16))
            st = st * p["decay"][c] + p["dstate"][c]
        results.append((jnp.concatenate(outs, axis=0), st))
    return results


def _hgrn_kernel(q_ref, ff_ref, fb_ref, i_ref, g_ref, lb_ref, ng_ref, o_ref, of_ref, ob_ref):
    seq, dk = q_ref.shape
    n_groups = seq // A_GROUP
    lb_f, lb_b = lb_ref[0:1, :], lb_ref[1:2, :]

    def body(j, carry):
        st_f, st_b = carry
        rf = pl.ds(pl.multiple_of(j * A_GROUP, A_GROUP), A_GROUP)
        rb = pl.ds(pl.multiple_of((n_groups - 1 - j) * A_GROUP, A_GROUP), A_GROUP)
        (o_f, st_f), (o_b, st_b) = _hgrn_groups([
            (_silu(q_ref[rf, :]), ff_ref[rf, :], i_ref[rf, :], lb_f, st_f, False),
            (_silu(q_ref[rb, :]), fb_ref[rb, :], i_ref[rb, :], lb_b, st_b, True)])
        of_ref[rf, :] = o_f
        ob_ref[rb, :] = o_b
        return st_f, st_b

    zero = jnp.zeros((dk, dk), F32)
    lax.fori_loop(0, n_groups, body, (zero, zero))

    def finish(j, carry):
        r = pl.ds(pl.multiple_of(j * A_GROUP, A_GROUP), A_GROUP)
        o = of_ref[r, :] + ob_ref[r, :]
        o = o * lax.rsqrt(jnp.mean(o * o, axis=-1, keepdims=True) + NORM_EPS) * ng_ref[...]
        o_ref[r, :] = (o * _silu(g_ref[r, :])).astype(o_ref.dtype)
        return carry

    lax.fori_loop(0, n_groups, finish, 0)


def _hgrn2(p_a, lb, norm_g, batch, seq):
    n = p_a.shape[0]
    width = p_a.shape[1] // 5
    heads = width // A_HEAD_DIM
    sect = lambda s: pl.BlockSpec((seq, A_HEAD_DIM), lambda b, h, s=s: (b, s * heads + h))
    return pl.pallas_call(
        _hgrn_kernel,
        grid=(batch, heads),
        in_specs=[sect(0), sect(1), sect(2), sect(3), sect(4),
                  pl.BlockSpec((2, A_HEAD_DIM), lambda b, h: (0, h)),
                  pl.BlockSpec((1, A_HEAD_DIM), lambda b, h: (0, 0))],
        out_specs=pl.BlockSpec((seq, A_HEAD_DIM), lambda b, h: (b, h)),
        out_shape=jax.ShapeDtypeStruct((n, width), BF16),
        scratch_shapes=[pltpu.VMEM((seq, A_HEAD_DIM), F32), pltpu.VMEM((seq, A_HEAD_DIM), F32)],
        compiler_params=_params("parallel", "parallel"),
        name="hgrn2",
    )(p_a, p_a, p_a, p_a, p_a, lb, norm_g.reshape(1, A_HEAD_DIM))


def _softplus(z):
    return jnp.maximum(z, 0.0) + jnp.log(1.0 + jnp.exp(-jnp.abs(z)))


def _rwkv_prep_kernel(p_ref, prev_ref, next_ref, mu_ref, w0_ref, w2_ref, a0_ref, a2_ref, g2_ref,
                      kk_ref, ka_ref, rk_ref, ones_ref,
                      r_out, k_out, v_out, kkn_out, b_out, lwf_out, lwb_out, g_out, bonus_out):
    i = pl.program_id(1)
    last = pl.num_programs(1) - 1
    ts = p_ref.shape[0]
    width = a0_ref.shape[1]
    heads = width // B_HEAD_DIM
    p = p_ref[...]
    prev_row = jnp.where(i == 0, 0.0, prev_ref[7:8, :])
    next_row = jnp.where(i == last, 0.0, next_ref[0:1, :])
    row = lax.broadcasted_iota(jnp.int32, (ts, 1), 0)
    up = jnp.where(row == 0, prev_row, pltpu.roll(p, 1, axis=0))
    dn = jnp.where(row == ts - 1, next_row, pltpu.roll(p, ts - 1, axis=0))
    p = p + mu_ref[...] * (0.5 * (up + dn) - p)

    r = p[:, 0:width]
    k = p[:, width:2 * width]
    v = p[:, 2 * width:3 * width]
    o = 3 * width
    wlo_f = p[:, o:o + 64]
    wlo_b = p[:, o + 64:o + 128]
    alo = p[:, o + 128:o + 192]
    glo = p[:, o + 192:o + 320]

    def log_decay(wlo, d):
        z = w0_ref[d:d + 1, :] + _dot(jnp.tanh(wlo).astype(BF16), w2_ref[d])
        w = -_softplus(-z) - 0.5
        return -jnp.exp(w)

    lw_f = log_decay(wlo_f, 0)
    lw_b = log_decay(wlo_b, 1)
    a = _sigmoid(a0_ref[...] + _dot(alo.astype(BF16), a2_ref[...]))
    g = _dot(_sigmoid(glo).astype(BF16), g2_ref[...])
    kk = k * kk_ref[...]
    ones = ones_ref[...]
    head_sum = lambda t: jnp.dot(t, ones, precision=HIGHEST, preferred_element_type=F32)
    kk_n = kk / jnp.maximum(jnp.sqrt(head_sum(kk * kk)), 1e-12)
    k_mod = k * (1.0 + (a - 1.0) * ka_ref[...])
    bonus = head_sum(r * k_mod * rk_ref[...]) * v
    g_out[...] = g
    bonus_out[...] = bonus
    for h in range(heads):
        sl = slice(h * B_HEAD_DIM, (h + 1) * B_HEAD_DIM)
        r_out[0, h] = r[:, sl]
        k_out[0, h] = k_mod[:, sl]
        v_out[0, h] = v[:, sl]
        kkn_out[0, h] = kk_n[:, sl]
        b_out[0, h] = (kk_n * a)[:, sl]
        lwf_out[0, h] = lw_f[:, sl]
        lwb_out[0, h] = lw_b[:, sl]


def _rwkv_prep(p_b, mu, w0, w2, a0, a2, g2, k_k, k_a, r_k, batch, seq, ts=256):
    n, cols = p_b.shape
    width = a0.shape[0]
    heads = width // B_HEAD_DIM
    nblk = seq // ts
    rows8 = ts // 8
    head_id = np.arange(width) // B_HEAD_DIM
    ones = jnp.asarray(head_id[:, None] == head_id[None, :], F32)
    full = lambda a: pl.BlockSpec(a.shape, lambda b, i: (0,) * a.ndim)
    vec = lambda a: a.reshape(1, -1)
    args = [vec(mu), w0, w2.astype(BF16), vec(a0), a2.astype(BF16), g2.astype(BF16),
            vec(k_k), vec(k_a), vec(r_k), ones]
    hm = pl.BlockSpec((1, heads, ts, B_HEAD_DIM), lambda b, i: (b, 0, i, 0))
    tokm = pl.BlockSpec((ts, width), lambda b, i: (b * nblk + i, 0))
    hm_shape = jax.ShapeDtypeStruct((batch, heads, seq, B_HEAD_DIM), F32)
    tok_shape = jax.ShapeDtypeStruct((n, width), F32)
    return pl.pallas_call(
        _rwkv_prep_kernel,
        grid=(batch, nblk),
        in_specs=[
            pl.BlockSpec((ts, cols), lambda b, i: (b * nblk + i, 0)),
            pl.BlockSpec((8, cols), lambda b, i: (jnp.maximum((b * nblk + i) * rows8 - 1, 0), 0)),
            pl.BlockSpec((8, cols), lambda b, i: (jnp.minimum((b * nblk + i + 1) * rows8, n // 8 - 1), 0)),
        ] + [full(a) for a in args],
        out_specs=[hm] * 7 + [tokm] * 2,
        out_shape=[hm_shape] * 7 + [tok_shape] * 2,
        compiler_params=_params("parallel", "parallel"),
        name="rwkv_prep",
    )(p_b, p_b, p_b, *args)


def _rwkv_chunks(chains):
    c, d = chains[0][0].shape
    ti = lax.broadcasted_iota(jnp.int32, (c, c), 0)
    si = lax.broadcasted_iota(jnp.int32, (c, c), 1)
    eye = jnp.where(ti == si, 1.0, 0.0)
    masks = {rev: (((si > ti), (si >= ti)) if rev else ((si < ti), (si <= ti))) for rev in (False, True)}

    st = []
    for r, k, v, kk, b, lw, tt, rev in chains:
        cum = _chunk_cumsum(lw, c, rev)
        grow = jnp.exp(-cum)
        kt = kk * jnp.exp(cum - lw)
        rt = r * jnp.exp(cum)
        b_ = (b * grow).astype(BF16)
        kb = jnp.concatenate([(k * grow).astype(BF16), b_], axis=0)
        st.append(dict(v=v, vb=v.astype(BF16), kt=kt, rt=rt, b_=b_, kb=kb, tt=tt, ttb=tt.astype(BF16),
                       decay=jnp.exp(cum[0:1, :] if rev else cum[c - 1:c, :]), rev=rev,
                       lhs=jnp.concatenate([kt, rt], axis=0).astype(BF16)))
    for s in st:
        s["big"] = _dot_nt(s["lhs"], s["kb"])
    for s in st:
        strict, incl = masks[s["rev"]]
        big = s["big"]
        a_ab = jnp.where(strict, big[:c, c:], 0.0)
        s["a_ak"] = jnp.where(strict, big[:c, :c], 0.0).astype(BF16)
        s["p"] = jnp.concatenate([jnp.where(incl, big[c:, :c], 0.0), jnp.where(incl, -big[c:, c:], 0.0)],
                                 axis=1).astype(BF16)
        s["m"] = eye - a_ab
        s["ab"] = a_ab.astype(BF16)
    for s in st:
        s["aj"] = _dot(s["ab"], s["ab"])
        s["av"] = _dot(s["a_ak"], s["vb"])
    span = 2
    while span < c:
        span *= 2
        for s in st:
            ajb = s["aj"].astype(BF16)
            if span < c:
                both = _dot(jnp.concatenate([s["aj"], s["m"]], axis=0).astype(BF16), ajb)
                s["aj"], s["m"] = both[:c], s["m"] + both[c:]
            else:
                s["m"] = s["m"] + _dot(s["m"].astype(BF16), ajb)
    for s in st:
        s["wu"] = _dot(s["m"].astype(BF16), jnp.concatenate([s["kt"], s["av"]], axis=1).astype(BF16))
    for s in st:
        w_, u0, v = s["wu"][:, :d], s["wu"][:, d:], s["v"]
        rhs = jnp.concatenate([jnp.concatenate([v, jnp.zeros_like(v)], axis=1),
                               jnp.concatenate([u0, w_], axis=1)], axis=0).astype(BF16)
        s["yq"] = _dot(s["p"], rhs)
        s["h_t"] = _dot_tn(jnp.concatenate([v, -u0], axis=0).astype(BF16), s["kb"])
        s["bw"] = _dot_tn(s["b_"], w_.astype(BF16))
    out = []
    for s in st:
        q = s["rt"] + s["yq"][:, d:]
        y = _dot_nt(q.astype(BF16), s["ttb"]) + s["yq"][:, :d]
        tt = (s["tt"] - _dot_nt(s["ttb"], s["bw"].astype(BF16)) + s["h_t"]) * s["decay"]
        out.append((y, tt))
    return out


def _rwkv_scan_kernel(rf, kf, vf, kkf, bf, lwf, rb, kb, vb, kkb, bb, lwb, yf_ref, yb_ref, tf_ref, tb_ref):
    heads, ts = rf.shape[1], rf.shape[2]
    n_chunks = ts // B_CHUNK

    @pl.when(pl.program_id(1) == 0)
    def _():
        tf_ref[...] = jnp.zeros_like(tf_ref)
        tb_ref[...] = jnp.zeros_like(tb_ref)

    def body(j, carry):
        rows_f = pl.ds(pl.multiple_of(j * B_CHUNK, B_CHUNK), B_CHUNK)
        rows_b = pl.ds(pl.multiple_of((n_chunks - 1 - j) * B_CHUNK, B_CHUNK), B_CHUNK)
        chains = []
        for h in range(heads):
            chains.append([ref[0, h, rows_f, :] for ref in (rf, kf, vf, kkf, bf, lwf)] + [tf_ref[h], False])
            chains.append([ref[0, h, rows_b, :] for ref in (rb, kb, vb, kkb, bb, lwb)] + [tb_ref[h], True])
        res = _rwkv_chunks(chains)
        for h in range(heads):
            yf_ref[0, h, rows_f, :], tf_ref[h] = res[2 * h]
            yb_ref[0, h, rows_b, :], tb_ref[h] = res[2 * h + 1]
        return carry

    lax.fori_loop(0, n_chunks, body, 0)


def _rwkv_scan(r, k, v, kkn, b, lw_f, lw_b, ts=256):
    batch, heads, seq, d = r.shape
    nblk = seq // ts
    fwd = pl.BlockSpec((1, heads, ts, d), lambda bi, i: (bi, 0, i, 0))
    bwd = pl.BlockSpec((1, heads, ts, d), lambda bi, i: (bi, 0, nblk - 1 - i, 0))
    shape = jax.ShapeDtypeStruct(r.shape, F32)
    return pl.pallas_call(
        _rwkv_scan_kernel,
        grid=(batch, nblk),
        in_specs=[fwd] * 6 + [bwd] * 6,
        out_specs=[fwd, bwd],
        out_shape=[shape, shape],
        scratch_shapes=[pltpu.VMEM((heads, d, d), F32), pltpu.VMEM((heads, d, d), F32)],
        compiler_params=_params("parallel", "arbitrary"),
        name="rwkv_scan",
    )(r, k, v, kkn, b, lw_f, r, k, v, kkn, b, lw_b)


def _rwkv_post_kernel(yf_ref, yb_ref, g_ref, bonus_ref, lng_ref, lnb_ref, o_ref):
    heads = yf_ref.shape[1]
    outs = []
    for h in range(heads):
        y = yf_ref[0, h] + yb_ref[0, h]
        mean = jnp.mean(y, axis=-1, keepdims=True)
        var = jnp.mean(jnp.square(y - mean), axis=-1, keepdims=True)
        outs.append((y - mean) * lax.rsqrt(var + B_LN_EPS))
    yn = jnp.concatenate(outs, axis=1) * lng_ref[...] + lnb_ref[...]
    o_ref[...] = ((yn + bonus_ref[...]) * g_ref[...]).astype(o_ref.dtype)


def _rwkv_post(y_f, y_b, g, bonus, ln_g, ln_b, ts=256):
    batch, heads, seq, d = y_f.shape
    n, width = g.shape
    nblk = seq // ts
    hm = pl.BlockSpec((1, heads, ts, d), lambda b, i: (b, 0, i, 0))
    tokm = pl.BlockSpec((ts, width), lambda b, i: (b * nblk + i, 0))
    vec = pl.BlockSpec((1, width), lambda b, i: (0, 0))
    return pl.pallas_call(
        _rwkv_post_kernel,
        grid=(batch, nblk),
        in_specs=[hm, hm, tokm, tokm, vec, vec],
        out_specs=tokm,
        out_shape=jax.ShapeDtypeStruct((n, width), BF16),
        compiler_params=_params("parallel", "parallel"),
        name="rwkv_post",
    )(y_f, y_b, g, bonus, ln_g.reshape(1, width), ln_b.reshape(1, width))


def _rwkv7(p_b, mu, w0, w2, a0, a2, g2, k_k, k_a, r_k, ln_g, ln_b, batch, seq):
    r, k, v, kkn, b, lw_f, lw_b, g, bonus = _rwkv_prep(p_b, mu, w0, w2, a0, a2, g2, k_k, k_a, r_k, batch, seq)
    y_f, y_b = _rwkv_scan(r, k, v, kkn, b, lw_f, lw_b)
    return _rwkv_post(y_f, y_b, g, bonus, ln_g, ln_b)


_BUCKET_EDGES = (0, 1, 2, 3, 4, 5, 6, 7, 8, 12, 16, 23, 32, 46, 64, 91)
ATTN_TK = 128


def _bias_band_kernel(table_ref, o_ref):
    h = pl.program_id(0)
    n_d, tq, tk = o_ref.shape[1:]
    half = REL_BUCKETS // 2
    r = lax.broadcasted_iota(jnp.int32, (tq, tk), 0)
    c = lax.broadcasted_iota(jnp.int32, (tq, tk), 1)
    for d in range(n_d):
        rel = tk * (d - 1) + c - r
        n = jnp.abs(rel)
        vals = []
        for sign in range(2):
            val = jnp.full((tq, tk), table_ref[sign * half + half - 1, h], F32)
            for bkt in range(half - 2, -1, -1):
                val = jnp.where(n < _BUCKET_EDGES[bkt + 1], table_ref[sign * half + bkt, h], val)
            vals.append(val)
        o_ref[0, d] = jnp.where(rel > 0, vals[1], vals[0])


def _bias_band(rel_table, tq):
    heads = rel_table.shape[1]
    n_d = tq // ATTN_TK + 2
    return pl.pallas_call(
        _bias_band_kernel,
        grid=(heads,),
        in_specs=[pl.BlockSpec(memory_space=pltpu.SMEM)],
        out_specs=pl.BlockSpec((1, n_d, tq, ATTN_TK), lambda h: (h, 0, 0, 0)),
        out_shape=jax.ShapeDtypeStruct((heads, n_d, tq, ATTN_TK), F32),
        compiler_params=_params("parallel"),
        name="bias_band",
    )(rel_table)


def _diff_attn_kernel(table_ref, lam_ref, q_ref, k_ref, v_ref, band_ref, g_ref, o_ref, lg_ref, *, lam_init):
    h = pl.program_id(1)
    i = pl.program_id(2)
    tq, dh2 = q_ref.shape
    seq = k_ref.shape[0]
    n_d = band_ref.shape[1]
    n_kt = seq // ATTN_TK
    ratio = tq // ATTN_TK
    half = REL_BUCKETS // 2
    k = k_ref[...]
    v = v_ref[...]
    lane = lax.broadcasted_iota(jnp.int32, (1, dh2), 1)
    q = q_ref[...] * jnp.asarray(C_HEAD_DIM ** -0.5, q_ref.dtype)
    zero = jnp.zeros_like(q)
    first_band = ratio * i - 1
    col = lax.broadcasted_iota(jnp.int32, (1, seq), 1)
    far_row = jnp.where(col < first_band * ATTN_TK, table_ref[half - 1, h],
                        jnp.where(col >= (first_band + n_d) * ATTN_TK, table_ref[REL_BUCKETS - 1, h],
                                  -jnp.inf))
    row_max = []
    for m in range(2):
        qm = jnp.where((lane >= m * C_HEAD_DIM) & (lane < (m + 1) * C_HEAD_DIM), q, zero)
        far = _dot_nt(qm, k) + far_row
        mx = jnp.max(far, axis=-1, keepdims=True)
        lg_ref[m] = far
        for d in (0, n_d - 1) + tuple(range(1, n_d - 1)):
            kt = first_band + d
            valid = (kt >= 0) & (kt < n_kt)
            c0 = pl.multiple_of(jnp.clip(kt, 0, n_kt - 1) * ATTN_TK, ATTN_TK)
            near = _dot_nt(qm, k_ref[pl.ds(c0, ATTN_TK), :]) + band_ref[0, d]
            mx = jnp.maximum(mx, jnp.where(valid, jnp.max(near, axis=-1, keepdims=True), -jnp.inf))
            lg_ref[m, :, pl.ds(c0, ATTN_TK)] = near
        row_max.append(mx)
    es = [jnp.exp(lg_ref[m] - row_max[m]) for m in range(2)]
    parts = [_dot(e.astype(BF16), v) * (1.0 / jnp.sum(e, axis=-1, keepdims=True)) for e in es]
    o = parts[0] - lam_ref[0] * parts[1]
    o = o * lax.rsqrt(jnp.mean(o * o, axis=-1, keepdims=True) + NORM_EPS) * g_ref[...]
    o_ref[...] = (o * (1.0 - lam_init)).astype(o_ref.dtype)


def _diff_attention(qkv, rel_table, lam_full, subln_g, lam_init, batch, seq, tq=256):
    n = qkv.shape[0]
    dh2 = 2 * C_HEAD_DIM
    heads = qkv.shape[1] // (3 * dh2)
    nq = seq // tq
    band = _bias_band(rel_table, tq)
    return pl.pallas_call(
        functools.partial(_diff_attn_kernel, lam_init=lam_init),
        grid=(batch, heads, nq),
        in_specs=[
            pl.BlockSpec(memory_space=pltpu.SMEM),
            pl.BlockSpec(memory_space=pltpu.SMEM),
            pl.BlockSpec((tq, dh2), lambda b, h, i: (b * nq + i, h)),
            pl.BlockSpec((seq, dh2), lambda b, h, i: (b, heads + h)),
            pl.BlockSpec((seq, dh2), lambda b, h, i: (b, 2 * heads + h)),
            pl.BlockSpec((1,) + band.shape[1:], lambda b, h, i: (h, 0, 0, 0)),
            pl.BlockSpec((1, dh2), lambda b, h, i: (0, 0)),
        ],
        out_specs=pl.BlockSpec((tq, dh2), lambda b, h, i: (b * nq + i, h)),
        out_shape=jax.ShapeDtypeStruct((n, heads * dh2), BF16),
        scratch_shapes=[pltpu.VMEM((2, tq, seq), F32)],
        compiler_params=_params("parallel", "parallel", "parallel"),
        name="diff_attention",
    )(rel_table, lam_full.reshape(1), qkv, qkv, qkv, band, subln_g.reshape(1, dh2))


MOE_TILE = 256
_HI_MASK = np.uint32(0xFFFF0000)


def _pack_bf16_pair(lo, hi):
    as_bits = lambda t: lax.bitcast_convert_type(t.astype(BF16).astype(F32), jnp.uint32)
    return (as_bits(hi) & _HI_MASK) | (as_bits(lo) >> 16)


def _unpack_bf16_pair(word):
    lo = lax.bitcast_convert_type(word << 16, F32).astype(BF16)
    hi = lax.bitcast_convert_type(word & _HI_MASK, F32).astype(BF16)
    return lo, hi


def _router_kernel(x_ref, g_ref, sh_ref, sc_ref, rw_ref, rb_ref, hp_ref, idx_ref, w_ref, rank_ref, cnt_ref,
                   run_ref, *, n_experts):
    @pl.when(pl.program_id(0) == 0)
    def _():
        run_ref[...] = jnp.zeros_like(run_ref)

    tm, d = x_ref.shape
    h = _modulated_norm(x_ref[...], g_ref[...], sh_ref[0], sc_ref[0])
    hp_ref[...] = _pack_bf16_pair(h[:, :d // 2], h[:, d // 2:])
    logits = jnp.dot(h, rw_ref[...], precision=HIGHEST, preferred_element_type=F32) + rb_ref[...]
    lane = lax.broadcasted_iota(jnp.int32, logits.shape, 1).astype(F32)
    neg = jnp.float32(-jnp.inf)
    logits = jnp.where(lane < n_experts, logits, neg)
    picks, vals = [], []
    for _ in range(TOP_K):
        m = jnp.max(logits, axis=-1, keepdims=True)
        first = jnp.min(jnp.where(logits == m, lane, float(LANES)), axis=-1, keepdims=True)
        hit = lane == first
        picks.append((first, hit))
        vals.append(m)
        logits = jnp.where(hit, neg, logits)
    es = [jnp.exp(v - vals[0]) for v in vals]
    inv = 1.0 / (es[0] + es[1] + es[2] + es[3])
    assigned = picks[0][1] | picks[1][1] | picks[2][1] | picks[3][1]
    onehot = jnp.where(assigned, 1.0, 0.0)
    ri = lax.broadcasted_iota(jnp.int32, (tm, tm), 0)
    ci = lax.broadcasted_iota(jnp.int32, (tm, tm), 1)
    earlier = jnp.where(ci < ri, 1.0, 0.0).astype(BF16)
    before = _dot(earlier, onehot.astype(BF16)) + run_ref[...]
    idx_o = jnp.zeros(logits.shape, F32)
    w_o = jnp.zeros(logits.shape, F32)
    rank_o = jnp.zeros(logits.shape, F32)
    for j, (first, hit) in enumerate(picks):
        rank_j = jnp.sum(jnp.where(hit, before, 0.0), axis=-1, keepdims=True)
        idx_o = jnp.where(lane == j, first, idx_o)
        w_o = jnp.where(lane == j, es[j] * inv, w_o)
        rank_o = jnp.where(lane == j, rank_j, rank_o)
    idx_ref[...] = idx_o
    w_ref[...] = w_o
    rank_ref[...] = rank_o
    run_ref[...] += jnp.sum(onehot, axis=0, keepdims=True)
    cnt_ref[...] = run_ref[...]


def _router(x2, g, mod_l, shift_idx, seq, router_w, router_b):
    n, d = x2.shape
    tm = MOE_TILE
    n_experts = router_w.shape[1]
    per_batch = seq // tm
    rw = jnp.zeros((d, LANES), F32).at[:, :n_experts].set(router_w)
    rb = jnp.zeros((1, LANES), F32).at[0, :n_experts].set(router_b)
    lane_out = pl.BlockSpec((tm, LANES), lambda i: (i, 0))
    lane_shape = jax.ShapeDtypeStruct((n, LANES), F32)
    return pl.pallas_call(
        functools.partial(_router_kernel, n_experts=n_experts),
        grid=(n // tm,),
        in_specs=[
            pl.BlockSpec((tm, d), lambda i: (i, 0)),
            pl.BlockSpec((1, d), lambda i: (0, 0)),
            pl.BlockSpec((1, 1, d), lambda i: (i // per_batch, 0, shift_idx)),
            pl.BlockSpec((1, 1, d), lambda i: (i // per_batch, 0, shift_idx + 1)),
            pl.BlockSpec((d, LANES), lambda i: (0, 0)),
            pl.BlockSpec((1, LANES), lambda i: (0, 0)),
        ],
        out_specs=[pl.BlockSpec((tm, d // 2), lambda i: (i, 0)), lane_out, lane_out, lane_out,
                   pl.BlockSpec((1, LANES), lambda i: (0, 0))],
        out_shape=[jax.ShapeDtypeStruct((n, d // 2), jnp.uint32), lane_shape, lane_shape, lane_shape,
                   jax.ShapeDtypeStruct((1, LANES), F32)],
        scratch_shapes=[pltpu.VMEM((1, LANES), F32)],
        compiler_params=_params("arbitrary"),
        name="moe_router",
    )(x2, g.reshape(1, d), mod_l, mod_l, rw, rb)


def _dispatch_kernel(pos_ref, h_ref, zeros_ref, xs_ref, sem):
    del zeros_ref
    tm = h_ref.shape[0]

    def row_copy(r, p):
        return pltpu.make_async_copy(h_ref.at[pl.ds(r, 1)], xs_ref.at[pl.ds(p, 1)], sem)

    def issue(r, carry):
        for j in range(TOP_K):
            row_copy(r, pos_ref[r * TOP_K + j]).start(priority=j % 2)
        return carry

    lax.fori_loop(0, tm, issue, 0)

    def drain(r, carry):
        for j in range(TOP_K):
            row_copy(r, pos_ref[r * TOP_K + j]).wait()
        return carry

    lax.fori_loop(0, tm, drain, 0)


def _dispatch(hp, pos_flat, n_rows):
    n, dw = hp.shape
    tm = MOE_TILE
    zeros = jnp.zeros((n_rows, dw), jnp.uint32)
    return pl.pallas_call(
        _dispatch_kernel,
        grid=(n // tm,),
        in_specs=[
            pl.BlockSpec((tm * TOP_K,), lambda i: (i,), memory_space=pltpu.SMEM),
            pl.BlockSpec((tm, dw), lambda i: (i, 0)),
            pl.BlockSpec(memory_space=pl.ANY),
        ],
        out_specs=pl.BlockSpec(memory_space=pl.ANY),
        out_shape=jax.ShapeDtypeStruct((n_rows, dw), jnp.uint32),
        scratch_shapes=[pltpu.SemaphoreType.DMA(())],
        input_output_aliases={2: 0},
        compiler_params=_params("arbitrary"),
        name="moe_dispatch",
    )(pos_flat, hp, zeros)


GLU_TILE = 2 * LANES


def _experts_kernel(te_ref, nu_ref, slot_ref, nxt_ref, xs_ref, b1_ref, b2_ref, perm_ref, w1_hbm, w2_hbm,
                    ys_ref, w1buf, w2buf, w1p_ref, w2b_ref, sem, *, layer):
    t = pl.program_id(0)
    used = t < nu_ref[0]
    new_expert = (t == 0) | (te_ref[t] != te_ref[jnp.maximum(t - 1, 0)])
    n_col_tiles = w1p_ref.shape[1] // GLU_TILE

    def weight_copies(e, slot):
        return (pltpu.make_async_copy(w1_hbm.at[layer, e], w1buf.at[slot], sem.at[0, slot]),
                pltpu.make_async_copy(w2_hbm.at[layer, e], w2buf.at[slot], sem.at[1, slot]))

    @pl.when(used & new_expert)
    def _():
        slot = slot_ref[t]

        @pl.when(t == 0)
        def _():
            for cp in weight_copies(te_ref[t], slot):
                cp.start()

        for cp in weight_copies(te_ref[t], slot):
            cp.wait()

        @pl.when(nxt_ref[t] >= 0)
        def _():
            for cp in weight_copies(nxt_ref[t], 1 - slot):
                cp.start()

        for c in range(n_col_tiles):
            cols = slice(c * GLU_TILE, (c + 1) * GLU_TILE)
            w1p_ref[:, cols] = _dot(w1buf[slot, :, cols].astype(BF16), perm_ref[...]).astype(BF16)
        w2b_ref[...] = w2buf[slot].astype(BF16)

    @pl.when(used)
    def _():
        lo, hi = _unpack_bf16_pair(xs_ref[...])
        x = jnp.concatenate([lo, hi], axis=1)
        hh = _dot(x, w1p_ref[...]) + b1_ref[0]
        acts = []
        for c in range(n_col_tiles):
            x_glu = jnp.minimum(hh[:, c * GLU_TILE:c * GLU_TILE + LANES], SWIGLU_LIMIT)
            x_lin = jnp.clip(hh[:, c * GLU_TILE + LANES:(c + 1) * GLU_TILE], -SWIGLU_LIMIT, SWIGLU_LIMIT)
            acts.append((x_glu * _sigmoid(SWIGLU_ALPHA * x_glu) * (x_lin + 1.0)).astype(BF16))
        out = _dot(jnp.concatenate(acts, axis=1), w2b_ref[...]) + b2_ref[0]
        half = out.shape[1] // 2
        ys_ref[...] = _pack_bf16_pair(out[:, :half], out[:, half:])

    @pl.when(t >= nu_ref[0])
    def _():
        ys_ref[...] = jnp.zeros_like(ys_ref)


def _experts(xs, tile_expert, n_used, slot, nxt, w1_all, b1, w2_all, b2, layer):
    n_rows, dw = xs.shape
    tm = MOE_TILE
    _, n_exp, d, de2 = w1_all.shape
    src = np.arange(GLU_TILE)
    dst = np.where(src % 2 == 0, src // 2, LANES + src // 2)
    perm = jnp.asarray(dst[:, None] == np.arange(GLU_TILE)[None, :], BF16)
    b1p = b1.reshape(n_exp, de2 // GLU_TILE, LANES, 2).transpose(0, 1, 3, 2).reshape(n_exp, 1, de2)
    row = lambda t, te, nu, sl, nx: (jnp.maximum(jnp.minimum(t, nu[0] - 1), 0), 0)
    exp3 = lambda t, te, nu, sl, nx: (te[t], 0, 0)
    grid_spec = pltpu.PrefetchScalarGridSpec(
        num_scalar_prefetch=4,
        grid=(n_rows // tm,),
        in_specs=[
            pl.BlockSpec((tm, dw), row),
            pl.BlockSpec((1, 1, de2), exp3),
            pl.BlockSpec((1, 1, d), exp3),
            pl.BlockSpec((GLU_TILE, GLU_TILE), lambda t, te, nu, sl, nx: (0, 0)),
            pl.BlockSpec(memory_space=pl.ANY),
            pl.BlockSpec(memory_space=pl.ANY),
        ],
        out_specs=pl.BlockSpec((tm, d // 2), lambda t, te, nu, sl, nx: (t, 0)),
        scratch_shapes=[pltpu.VMEM((2, d, de2), F32), pltpu.VMEM((2, de2 // 2, d), F32),
                        pltpu.VMEM((d, de2), BF16), pltpu.VMEM((de2 // 2, d), BF16),
                        pltpu.SemaphoreType.DMA((2, 2))],
    )
    return pl.pallas_call(
        functools.partial(_experts_kernel, layer=layer),
        grid_spec=grid_spec,
        out_shape=jax.ShapeDtypeStruct((n_rows, d // 2), jnp.uint32),
        compiler_params=_params("arbitrary"),
        name="moe_experts",
    )(tile_expert, n_used, slot, nxt, xs, b1p, b2.reshape(n_exp, 1, d), perm, w1_all, w2_all)


def _combine_kernel(pos_ref, x_ref, gate_ref, w_ref, fg_ref, ys_ref, o_ref, buf, sem, *, final_norm):
    tm = x_ref.shape[0]

    def row_copy(r, j):
        return pltpu.make_async_copy(ys_ref.at[pl.ds(pos_ref[r * TOP_K + j], 1)], buf.at[j, pl.ds(r, 1)], sem)

    def issue(r, carry):
        for j in range(TOP_K):
            row_copy(r, j).start(priority=j % 2)
        return carry

    lax.fori_loop(0, tm, issue, 0)

    def drain(r, carry):
        for j in range(TOP_K):
            row_copy(r, j).wait()
        return carry

    lax.fori_loop(0, tm, drain, 0)
    w = w_ref[...]
    y_lo = y_hi = None
    for j in range(TOP_K):
        lo, hi = _unpack_bf16_pair(buf[j])
        wj = w[:, j:j + 1]
        y_lo = wj * lo if j == 0 else y_lo + wj * lo
        y_hi = wj * hi if j == 0 else y_hi + wj * hi
    y = jnp.concatenate([y_lo, y_hi], axis=1)
    out = x_ref[...] + gate_ref[0] * y
    if final_norm:
        out = out * lax.rsqrt(jnp.mean(out * out, axis=-1, keepdims=True) + NORM_EPS) * fg_ref[...]
    o_ref[...] = out


def _combine(x2, mod_l, gate_idx, seq, top_w, pos_flat, ys, final_g, final_norm):
    n, d = x2.shape
    tm = MOE_TILE
    per_batch = seq // tm
    return pl.pallas_call(
        functools.partial(_combine_kernel, final_norm=final_norm),
        grid=(n // tm,),
        in_specs=[
            pl.BlockSpec((tm * TOP_K,), lambda i: (i,), memory_space=pltpu.SMEM),
            pl.BlockSpec((tm, d), lambda i: (i, 0)),
            pl.BlockSpec((1, 1, d), lambda i: (i // per_batch, 0, gate_idx)),
            pl.BlockSpec((tm, LANES), lambda i: (i, 0)),
            pl.BlockSpec((1, d), lambda i: (0, 0)),
            pl.BlockSpec(memory_space=pl.ANY),
        ],
        out_specs=pl.BlockSpec((tm, d), lambda i: (i, 0)),
        out_shape=jax.ShapeDtypeStruct((n, d), F32),
        scratch_shapes=[pltpu.VMEM((TOP_K, tm, d // 2), jnp.uint32), pltpu.SemaphoreType.DMA(())],
        compiler_params=_params("arbitrary"),
        name="moe_combine",
    )(pos_flat, x2, mod_l, top_w, final_g.reshape(1, d), ys)


SC_WINDOW = 128


def _sc_worker_rows(n_rows):
    sc = plsc.get_sparse_core_info()
    per_worker = n_rows // (sc.num_cores * sc.num_subcores)
    worker = lax.axis_index("subcore") * sc.num_cores + lax.axis_index("core")
    return worker * per_worker, per_worker


def _sc_mesh():
    return plsc.VectorSubcoreMesh(core_axis_name="core", subcore_axis_name="subcore")


def _sc_gather_rows(table, idx):
    n_idx, width = idx.shape[0], table.shape[1]

    @functools.partial(
        pl.kernel, out_type=jax.ShapeDtypeStruct((n_idx, width), table.dtype), mesh=_sc_mesh(),
        scratch_types=[pltpu.VMEM((SC_WINDOW,), jnp.int32), pltpu.VMEM((SC_WINDOW, width), table.dtype)],
        name="sc_gather_rows")
    def gather(table_hbm, idx_hbm, out_hbm, idx_v, rows_v):
        first, count = _sc_worker_rows(n_idx)

        @pl.loop(0, count // SC_WINDOW)
        def _(c):
            rows = pl.ds(first + c * SC_WINDOW, SC_WINDOW)
            pltpu.sync_copy(idx_hbm.at[rows], idx_v)
            pltpu.sync_copy(table_hbm.at[idx_v], rows_v)
            pltpu.sync_copy(rows_v, out_hbm.at[rows])

    return gather(table, idx)


def _sc_scatter_rows(rows, pos, n_out):
    n, width = rows.shape
    k = pos.shape[0]

    @functools.partial(
        pl.kernel, out_type=jax.ShapeDtypeStruct((n_out, width), rows.dtype), mesh=_sc_mesh(),
        scratch_types=[pltpu.VMEM((k, SC_WINDOW), jnp.int32), pltpu.VMEM((SC_WINDOW, width), rows.dtype)],
        name="sc_scatter_rows")
    def scatter(rows_hbm, pos_hbm, out_hbm, pos_v, rows_v):
        first, count = _sc_worker_rows(n)

        @pl.loop(0, count // SC_WINDOW)
        def _(c):
            src = pl.ds(first + c * SC_WINDOW, SC_WINDOW)
            pltpu.sync_copy(pos_hbm.at[:, src], pos_v)
            pltpu.sync_copy(rows_hbm.at[src], rows_v)
            for j in range(k):
                pltpu.sync_copy(rows_v, out_hbm.at[pos_v.at[j]])

    return scatter(rows, pos)


def _combine_dense_kernel(x_ref, gate_ref, w_ref, fg_ref, y4_ref, o_ref, *, final_norm):
    half = y4_ref.shape[1] // TOP_K
    w = w_ref[...]
    y_lo = y_hi = None
    for j in range(TOP_K):
        lo, hi = _unpack_bf16_pair(y4_ref[:, j * half:(j + 1) * half])
        wj = w[:, j:j + 1]
        y_lo = wj * lo if j == 0 else y_lo + wj * lo
        y_hi = wj * hi if j == 0 else y_hi + wj * hi
    out = x_ref[...] + gate_ref[0] * jnp.concatenate([y_lo, y_hi], axis=1)
    if final_norm:
        out = out * lax.rsqrt(jnp.mean(out * out, axis=-1, keepdims=True) + NORM_EPS) * fg_ref[...]
    o_ref[...] = out


def _combine_dense(x2, mod_l, gate_idx, seq, top_w, y4, final_g, final_norm, tm=512):
    n, d = x2.shape
    per_batch = seq // tm
    return pl.pallas_call(
        functools.partial(_combine_dense_kernel, final_norm=final_norm),
        grid=(n // tm,),
        in_specs=[
            pl.BlockSpec((tm, d), lambda i: (i, 0)),
            pl.BlockSpec((1, 1, d), lambda i: (i // per_batch, 0, gate_idx)),
            pl.BlockSpec((tm, LANES), lambda i: (i, 0)),
            pl.BlockSpec((1, d), lambda i: (0, 0)),
            pl.BlockSpec((tm, y4.shape[1]), lambda i: (i, 0)),
        ],
        out_specs=pl.BlockSpec((tm, d), lambda i: (i, 0)),
        out_shape=jax.ShapeDtypeStruct((n, d), F32),
        compiler_params=_params("parallel"),
        name="moe_combine",
    )(x2, mod_l, top_w, final_g.reshape(1, d), y4)


def _moe_ffn(x2, norm_g, mod_l, seq, router_w, router_b, w1_all, b1, w2_all, b2, layer, final_g, final_norm):
    n, d = x2.shape
    n_exp = router_w.shape[1]
    tile = MOE_TILE
    hp, idx_f, top_w, rank_f, counts = _router(x2, norm_g, mod_l, 3, seq, router_w, router_b)
    counts = counts[0, :n_exp].astype(jnp.int32)
    padded = (counts + tile - 1) // tile * tile
    ends = jnp.cumsum(padded)
    starts = ends - padded
    idx = idx_f[:, :TOP_K].astype(jnp.int32)
    expert_ids = jnp.arange(n_exp, dtype=jnp.int32)
    start_of = jnp.sum(jnp.where(idx[..., None] == expert_ids, starts, 0), axis=-1)
    pos_flat = (start_of + rank_f[:, :TOP_K].astype(jnp.int32)).reshape(-1)
    n_rows = n * TOP_K + n_exp * tile
    n_tiles = n_rows // tile
    tile_start = jnp.arange(n_tiles, dtype=jnp.int32) * tile
    tile_expert = jnp.sum((ends[None, :] <= tile_start[:, None]).astype(jnp.int32), axis=1)
    tile_expert = jnp.minimum(tile_expert, n_exp - 1)
    n_used = (ends[-1:] // tile).astype(jnp.int32)
    nonempty = counts > 0
    ordinal = jnp.cumsum(nonempty.astype(jnp.int32)) - 1
    later = nonempty[None, :] & (expert_ids[None, :] > expert_ids[:, None])
    nxt_e = jnp.min(jnp.where(later, expert_ids[None, :], n_exp), axis=1)
    nxt_e = jnp.where(nxt_e == n_exp, -1, nxt_e)
    slot = (ordinal[tile_expert] % 2).astype(jnp.int32)
    nxt = nxt_e[tile_expert].astype(jnp.int32)
    xs = _sc_scatter_rows(hp, pos_flat.reshape(n, TOP_K).T, n_rows)
    ys = _experts(xs, tile_expert, n_used, slot, nxt, w1_all, b1, w2_all, b2, layer)
    y4 = _sc_gather_rows(ys, pos_flat).reshape(n, TOP_K * (d // 2))
    return _combine_dense(x2, mod_l, 5, seq, top_w, y4, final_g, final_norm)


def kernel(x, c, ada_w, ada_b, norm_mix_g, norm_ffn_g, router_w, router_b, moe_w1, moe_b1, moe_w2, moe_b2, ab_w_in, ab_w_out, hgrn_lb, hgrn_norm_g, rwkv_mu, rwkv_w0, rwkv_w2, rwkv_a0, rwkv_a2, rwkv_g2, rwkv_k_k, rwkv_k_a, rwkv_r_k, rwkv_ln_g, rwkv_ln_b, attn_w_in, attn_w_out, attn_lambda, attn_subln_g, rel_bias_table, final_norm_g):
    batch, seq, d = x.shape
    n = batch * seq
    depth = ada_w.shape[0]
    x2 = x.reshape(n, d)
    mod = _adaln(c, ada_w, ada_b)
    lb_all = jnp.cumsum(jax.nn.softmax(hgrn_lb.astype(F32), axis=1), axis=1)
    for layer in range(depth):
        mod_l = mod[layer].reshape(batch, 1, 6 * d)
        j = layer // 2
        if layer % 2 == 0:
            a_cols = 5 * (d // 2)
            w_in = ab_w_in[j].astype(BF16)
            p_a, p_b = _normmod_proj(x2, norm_mix_g[layer], mod_l, 0, seq,
                                     [w_in[:, :a_cols], w_in[:, a_cols:]], [F32, F32])
            y_a = _hgrn2(p_a, lb_all[:, j], hgrn_norm_g[j], batch, seq)
            y_b = _rwkv7(p_b, rwkv_mu[j], rwkv_w0[j], rwkv_w2[j], rwkv_a0[j], rwkv_a2[j], rwkv_g2[j],
                         rwkv_k_k[j], rwkv_k_a[j], rwkv_r_k[j], rwkv_ln_g[j], rwkv_ln_b[j], batch, seq)
            w_out = ab_w_out[j].astype(BF16)
            x2 = _outproj_residual(x2, mod_l, 2, seq, [y_a, y_b], [w_out[:d // 2], w_out[d // 2:]])
        else:
            (qkv,) = _normmod_proj(x2, norm_mix_g[layer], mod_l, 0, seq, [attn_w_in[j].astype(BF16)], [BF16])
            lam = attn_lambda[j].astype(F32)
            lam_init = 0.8 - 0.6 * math.exp(-0.3 * layer)
            lam_full = jnp.exp(jnp.sum(lam[0] * lam[1])) - jnp.exp(jnp.sum(lam[2] * lam[3])) + lam_init
            o = _diff_attention(qkv, rel_bias_table, lam_full, attn_subln_g[j], lam_init, batch, seq)
            x2 = _outproj_residual(x2, mod_l, 2, seq, [o], [attn_w_out[j].astype(BF16)])
        x2 = _moe_ffn(x2, norm_ffn_g[layer], mod_l, seq, router_w[layer], router_b[layer], moe_w1,
                      moe_b1[layer], moe_w2, moe_b2[layer], layer, final_norm_g, layer == depth - 1)
    return x2.reshape(batch, seq, d)
```

```python
import functools
import math

import jax
import jax.numpy as jnp
import numpy as np
from jax import lax
from jax.experimental import pallas as pl
from jax.experimental.pallas import tpu as pltpu
from jax.experimental.pallas import tpu_sc as plsc

F32 = jnp.float32
BF16 = jnp.bfloat16
HIGHEST = lax.Precision.HIGHEST

NORM_EPS = 1e-6

A_HEAD_DIM = 128
A_CHUNK = 32
A_GROUP = 128
B_HEAD_DIM = 64
B_CHUNK = 64
B_LN_EPS = 1e-5 * B_HEAD_DIM
C_HEAD_DIM = 64
REL_BUCKETS = 32
REL_MAX_DISTANCE = 128
TOP_K = 4
SWIGLU_LIMIT = 7.0
SWIGLU_ALPHA = 1.702

V7X_VMEM_BYTES = 64 * 1024 * 1024
VMEM_LIMIT = V7X_VMEM_BYTES - 8 * 1024 * 1024
LANES = 128


def _params(*sem):
    return pltpu.CompilerParams(dimension_semantics=sem, vmem_limit_bytes=VMEM_LIMIT)


def _sigmoid(x):
    return 1.0 / (1.0 + jnp.exp(-x))


def _silu(x):
    return x * _sigmoid(x)


def _dot(a, b):
    return jnp.dot(a, b, preferred_element_type=F32)


def _dot_nt(a, b):
    return lax.dot_general(a, b, (((1,), (1,)), ((), ())), preferred_element_type=F32)


def _dot_tn(a, b):
    return lax.dot_general(a, b, (((0,), (0,)), ((), ())), preferred_element_type=F32)


def _adaln_kernel(c_ref, w_ref, b_ref, o_ref):
    cond = _silu(c_ref[...])
    o_ref[0] = jnp.dot(cond, w_ref[0], precision=HIGHEST, preferred_element_type=F32) + b_ref[0]


def _adaln(c, ada_w, ada_b):
    n_layers, d, n_out = ada_w.shape
    batch = c.shape[0]
    tn = 1536
    return pl.pallas_call(
        _adaln_kernel,
        grid=(n_layers, n_out // tn),
        in_specs=[
            pl.BlockSpec((batch, d), lambda l, j: (0, 0)),
            pl.BlockSpec((1, d, tn), lambda l, j: (l, 0, j)),
            pl.BlockSpec((1, 1, tn), lambda l, j: (l, 0, j)),
        ],
        out_specs=pl.BlockSpec((1, batch, tn), lambda l, j: (l, 0, j)),
        out_shape=jax.ShapeDtypeStruct((n_layers, batch, n_out), F32),
        compiler_params=_params("parallel", "parallel"),
        name="adaln",
    )(c, ada_w, ada_b.reshape(n_layers, 1, n_out))


def _modulated_norm(x, g, shift, scale):
    y = x * lax.rsqrt(jnp.mean(x * x, axis=-1, keepdims=True) + NORM_EPS)
    return (y * g) * (1.0 + scale) + shift


def _normmod_proj_kernel(x_ref, g_ref, sh_ref, sc_ref, *rest, n_w):
    w_refs, o_refs = rest[:n_w], rest[n_w:]
    h = _modulated_norm(x_ref[...], g_ref[...], sh_ref[0], sc_ref[0]).astype(BF16)
    for w_ref, o_ref in zip(w_refs, o_refs):
        o_ref[...] = _dot(h, w_ref[...]).astype(o_ref.dtype)


def _normmod_proj(x2, g, mod_l, shift_idx, seq, weights, out_dtypes, tm=256):
    n, d = x2.shape
    per_batch = seq // tm
    in_specs = [
        pl.BlockSpec((tm, d), lambda i: (i, 0)),
        pl.BlockSpec((1, d), lambda i: (0, 0)),
        pl.BlockSpec((1, 1, d), lambda i: (i // per_batch, 0, shift_idx)),
        pl.BlockSpec((1, 1, d), lambda i: (i // per_batch, 0, shift_idx + 1)),
    ]
    in_specs += [pl.BlockSpec(w.shape, lambda i: (0, 0)) for w in weights]
    out_specs = [pl.BlockSpec((tm, w.shape[1]), lambda i: (i, 0)) for w in weights]
    out_shape = [jax.ShapeDtypeStruct((n, w.shape[1]), dt) for w, dt in zip(weights, out_dtypes)]
    return pl.pallas_call(
        functools.partial(_normmod_proj_kernel, n_w=len(weights)),
        grid=(n // tm,),
        in_specs=in_specs,
        out_specs=out_specs,
        out_shape=out_shape,
        compiler_params=_params("parallel"),
        name="normmod_proj",
    )(x2, g.reshape(1, d), mod_l, mod_l, *weights)


def _outproj_kernel(x_ref, gate_ref, *rest, n_y):
    y_refs, w_refs, o_ref = rest[:n_y], rest[n_y:2 * n_y], rest[2 * n_y]
    acc = _dot(y_refs[0][...], w_refs[0][...])
    for y_ref, w_ref in zip(y_refs[1:], w_refs[1:]):
        acc += _dot(y_ref[...], w_ref[...])
    o_ref[...] = x_ref[...] + gate_ref[0] * acc


def _outproj_residual(x2, mod_l, gate_idx, seq, ys, ws, tm=512):
    n, d = x2.shape
    per_batch = seq // tm
    in_specs = [
        pl.BlockSpec((tm, d), lambda i: (i, 0)),
        pl.BlockSpec((1, 1, d), lambda i: (i // per_batch, 0, gate_idx)),
    ]
    in_specs += [pl.BlockSpec((tm, y.shape[1]), lambda i: (i, 0)) for y in ys]
    in_specs += [pl.BlockSpec(w.shape, lambda i: (0, 0)) for w in ws]
    return pl.pallas_call(
        functools.partial(_outproj_kernel, n_y=len(ys)),
        grid=(n // tm,),
        in_specs=in_specs,
        out_specs=pl.BlockSpec((tm, d), lambda i: (i, 0)),
        out_shape=jax.ShapeDtypeStruct((n, d), F32),
        compiler_params=_params("parallel"),
        name="outproj_residual",
    )(x2, mod_l, *ys, *ws)


def _chunk_cumsum(x, chunk, reverse):
    rows = x.shape[0]
    pos = lax.broadcasted_iota(jnp.int32, x.shape, 0) % chunk
    s = 1
    while s < chunk:
        if reverse:
            x = x + jnp.where(pos < chunk - s, pltpu.roll(x, rows - s, axis=0), 0.0)
        else:
            x = x + jnp.where(pos >= s, pltpu.roll(x, s, axis=0), 0.0)
        s *= 2
    return x


def _hgrn_groups(slabs):
    g_rows, dk = slabs[0][0].shape
    n_chunks = g_rows // A_CHUNK
    ti = lax.broadcasted_iota(jnp.int32, (A_CHUNK, A_CHUNK), 0)
    si = lax.broadcasted_iota(jnp.int32, (A_CHUNK, A_CHUNK), 1)
    chunk_rows = [slice(c * A_CHUNK, (c + 1) * A_CHUNK) for c in range(n_chunks)]
    prep = []
    for q, f, v, lb, st, reverse in slabs:
        fg = lb + (1.0 - lb) * _sigmoid(f)
        k = 1.0 - fg
        b = _chunk_cumsum(jnp.log(fg), A_CHUNK, reverse)
        b3 = b.reshape(n_chunks, A_CHUNK, dk)
        edge = b3[:, 0:1, :] if reverse else b3[:, A_CHUNK - 1:A_CHUNK, :]
        prep.append(dict(
            q_in=(q * jnp.exp(b)).astype(BF16), k_in=(k * jnp.exp(-b)).astype(BF16),
            k_st=(k.reshape(n_chunks, A_CHUNK, dk) * jnp.exp(edge - b3)).astype(BF16),
            decay=jnp.exp(edge), vb=v.astype(BF16), st=st, reverse=reverse,
            mask=(si >= ti) if reverse else (si <= ti)))
    for p in prep:
        p["scores"] = [_dot_nt(p["q_in"][r], p["k_in"][r]) for r in chunk_rows]
        p["dstate"] = [_dot_tn(p["vb"][r], p["k_st"][c]) for c, r in enumerate(chunk_rows)]
    for p in prep:
        p["intra"] = [_dot(jnp.where(p["mask"], s, 0.0).astype(BF16), p["vb"][r])
                      for s, r in zip(p["scores"], chunk_rows)]
    results = []
    for p in prep:
        st = p["st"]
        outs = [None] * n_chunks
        for c in (range(n_chunks - 1, -1, -1) if p["reverse"] else range(n_chunks)):
            outs[c] = p["intra"][c] + _dot_nt(p["q_in"][chunk_rows[c]], st.astype(BF16))
            st = st * p["decay"][c] + p["dstate"][c]
        results.append((jnp.concatenate(outs, axis=0), st))
    return results


def _hgrn_kernel(q_ref, ff_ref, fb_ref, i_ref, g_ref, lb_ref, ng_ref, o_ref, of_ref, ob_ref):
    seq, dk = q_ref.shape
    n_groups = seq // A_GROUP
    lb_f, lb_b = lb_ref[0:1, :], lb_ref[1:2, :]

    def body(j, carry):
        st_f, st_b = carry
        rf = pl.ds(pl.multiple_of(j * A_GROUP, A_GROUP), A_GROUP)
        rb = pl.ds(pl.multiple_of((n_groups - 1 - j) * A_GROUP, A_GROUP), A_GROUP)
        (o_f, st_f), (o_b, st_b) = _hgrn_groups([
            (_silu(q_ref[rf, :]), ff_ref[rf, :], i_ref[rf, :], lb_f, st_f, False),
            (_silu(q_ref[rb, :]), fb_ref[rb, :], i_ref[rb, :], lb_b, st_b, True)])
        of_ref[rf, :] = o_f
        ob_ref[rb, :] = o_b
        return st_f, st_b

    zero = jnp.zeros((dk, dk), F32)
    lax.fori_loop(0, n_groups, body, (zero, zero))

    def finish(j, carry):
        r = pl.ds(pl.multiple_of(j * A_GROUP, A_GROUP), A_GROUP)
        o = of_ref[r, :] + ob_ref[r, :]
        o = o * lax.rsqrt(jnp.mean(o * o, axis=-1, keepdims=True) + NORM_EPS) * ng_ref[...]
        o_ref[r, :] = (o * _silu(g_ref[r, :])).astype(o_ref.dtype)
        return carry

    lax.fori_loop(0, n_groups, finish, 0)


def _hgrn2(p_a, lb, norm_g, batch, seq):
    n = p_a.shape[0]
    width = p_a.shape[1] // 5
    heads = width // A_HEAD_DIM
    sect = lambda s: pl.BlockSpec((seq, A_HEAD_DIM), lambda b, h, s=s: (b, s * heads + h))
    return pl.pallas_call(
        _hgrn_kernel,
        grid=(batch, heads),
        in_specs=[sect(0), sect(1), sect(2), sect(3), sect(4),
                  pl.BlockSpec((2, A_HEAD_DIM), lambda b, h: (0, h)),
                  pl.BlockSpec((1, A_HEAD_DIM), lambda b, h: (0, 0))],
        out_specs=pl.BlockSpec((seq, A_HEAD_DIM), lambda b, h: (b, h)),
        out_shape=jax.ShapeDtypeStruct((n, width), BF16),
        scratch_shapes=[pltpu.VMEM((seq, A_HEAD_DIM), F32), pltpu.VMEM((seq, A_HEAD_DIM), F32)],
        compiler_params=_params("parallel", "parallel"),
        name="hgrn2",
    )(p_a, p_a, p_a, p_a, p_a, lb, norm_g.reshape(1, A_HEAD_DIM))


def _softplus(z):
    return jnp.maximum(z, 0.0) + jnp.log(1.0 + jnp.exp(-jnp.abs(z)))


def _rwkv_prep_kernel(p_ref, prev_ref, next_ref, mu_ref, w0_ref, w2_ref, a0_ref, a2_ref, g2_ref,
                      kk_ref, ka_ref, rk_ref, ones_ref,
                      r_out, k_out, v_out, kkn_out, b_out, lwf_out, lwb_out, g_out, bonus_out):
    i = pl.program_id(1)
    last = pl.num_programs(1) - 1
    ts = p_ref.shape[0]
    width = a0_ref.shape[1]
    heads = width // B_HEAD_DIM
    p = p_ref[...]
    prev_row = jnp.where(i == 0, 0.0, prev_ref[7:8, :])
    next_row = jnp.where(i == last, 0.0, next_ref[0:1, :])
    row = lax.broadcasted_iota(jnp.int32, (ts, 1), 0)
    up = jnp.where(row == 0, prev_row, pltpu.roll(p, 1, axis=0))
    dn = jnp.where(row == ts - 1, next_row, pltpu.roll(p, ts - 1, axis=0))
    p = p + mu_ref[...] * (0.5 * (up + dn) - p)

    r = p[:, 0:width]
    k = p[:, width:2 * width]
    v = p[:, 2 * width:3 * width]
    o = 3 * width
    wlo_f = p[:, o:o + 64]
    wlo_b = p[:, o + 64:o + 128]
    alo = p[:, o + 128:o + 192]
    glo = p[:, o + 192:o + 320]

    def log_decay(wlo, d):
        z = w0_ref[d:d + 1, :] + _dot(jnp.tanh(wlo).astype(BF16), w2_ref[d])
        w = -_softplus(-z) - 0.5
        return -jnp.exp(w)

    lw_f = log_decay(wlo_f, 0)
    lw_b = log_decay(wlo_b, 1)
    a = _sigmoid(a0_ref[...] + _dot(alo.astype(BF16), a2_ref[...]))
    g = _dot(_sigmoid(glo).astype(BF16), g2_ref[...])
    kk = k * kk_ref[...]
    ones = ones_ref[...]

    def head_sum(t):
        hi = t.astype(BF16)
        lo = (t - hi.astype(F32)).astype(BF16)
        return _dot(hi, ones) + _dot(lo, ones)

    kk_n = kk / jnp.maximum(jnp.sqrt(head_sum(kk * kk)), 1e-12)
    k_mod = k * (1.0 + (a - 1.0) * ka_ref[...])
    bonus = head_sum(r * k_mod * rk_ref[...]) * v
    g_out[...] = g
    bonus_out[...] = bonus
    for h in range(heads):
        sl = slice(h * B_HEAD_DIM, (h + 1) * B_HEAD_DIM)
        r_out[0, h] = r[:, sl]
        k_out[0, h] = k_mod[:, sl]
        v_out[0, h] = v[:, sl]
        kkn_out[0, h] = kk_n[:, sl]
        b_out[0, h] = (kk_n * a)[:, sl]
        lwf_out[0, h] = lw_f[:, sl]
        lwb_out[0, h] = lw_b[:, sl]


def _rwkv_prep(p_b, mu, w0, w2, a0, a2, g2, k_k, k_a, r_k, batch, seq, ts=256):
    n, cols = p_b.shape
    width = a0.shape[0]
    heads = width // B_HEAD_DIM
    nblk = seq // ts
    rows8 = ts // 8
    head_id = np.arange(width) // B_HEAD_DIM
    ones = jnp.asarray(head_id[:, None] == head_id[None, :], BF16)
    full = lambda a: pl.BlockSpec(a.shape, lambda b, i: (0,) * a.ndim)
    vec = lambda a: a.reshape(1, -1)
    args = [vec(mu), w0, w2.astype(BF16), vec(a0), a2.astype(BF16), g2.astype(BF16),
            vec(k_k), vec(k_a), vec(r_k), ones]
    hm = pl.BlockSpec((1, heads, ts, B_HEAD_DIM), lambda b, i: (b, 0, i, 0))
    tokm = pl.BlockSpec((ts, width), lambda b, i: (b * nblk + i, 0))
    hm_shape = jax.ShapeDtypeStruct((batch, heads, seq, B_HEAD_DIM), F32)
    tok_shape = jax.ShapeDtypeStruct((n, width), F32)
    return pl.pallas_call(
        _rwkv_prep_kernel,
        grid=(batch, nblk),
        in_specs=[
            pl.BlockSpec((ts, cols), lambda b, i: (b * nblk + i, 0)),
            pl.BlockSpec((8, cols), lambda b, i: (jnp.maximum((b * nblk + i) * rows8 - 1, 0), 0)),
            pl.BlockSpec((8, cols), lambda b, i: (jnp.minimum((b * nblk + i + 1) * rows8, n // 8 - 1), 0)),
        ] + [full(a) for a in args],
        out_specs=[hm] * 7 + [tokm] * 2,
        out_shape=[hm_shape] * 7 + [tok_shape] * 2,
        compiler_params=_params("parallel", "parallel"),
        name="rwkv_prep",
    )(p_b, p_b, p_b, *args)


def _rwkv_chunks(chains):
    c, d = chains[0][0].shape
    ti = lax.broadcasted_iota(jnp.int32, (c, c), 0)
    si = lax.broadcasted_iota(jnp.int32, (c, c), 1)
    eye = jnp.where(ti == si, 1.0, 0.0)
    masks = {rev: (((si > ti), (si >= ti)) if rev else ((si < ti), (si <= ti))) for rev in (False, True)}

    st = []
    for r, k, v, kk, b, lw, tt, rev in chains:
        cum = _chunk_cumsum(lw, c, rev)
        grow = jnp.exp(-cum)
        kt = kk * jnp.exp(cum - lw)
        rt = r * jnp.exp(cum)
        b_ = (b * grow).astype(BF16)
        kb = jnp.concatenate([(k * grow).astype(BF16), b_], axis=0)
        st.append(dict(v=v, vb=v.astype(BF16), kt=kt, rt=rt, b_=b_, kb=kb, tt=tt, ttb=tt.astype(BF16),
                       decay=jnp.exp(cum[0:1, :] if rev else cum[c - 1:c, :]), rev=rev,
                       lhs=jnp.concatenate([kt, rt], axis=0).astype(BF16)))
    for s in st:
        s["big"] = _dot_nt(s["lhs"], s["kb"])
    for s in st:
        strict, incl = masks[s["rev"]]
        big = s["big"]
        a_ab = jnp.where(strict, big[:c, c:], 0.0)
        s["a_ak"] = jnp.where(strict, big[:c, :c], 0.0).astype(BF16)
        s["p"] = jnp.concatenate([jnp.where(incl, big[c:, :c], 0.0), jnp.where(incl, -big[c:, c:], 0.0)],
                                 axis=1).astype(BF16)
        s["m"] = eye - a_ab
        s["ab"] = a_ab.astype(BF16)
    for s in st:
        s["aj"] = _dot(s["ab"], s["ab"])
        s["av"] = _dot(s["a_ak"], s["vb"])
    span = 2
    while span < c:
        span *= 2
        for s in st:
            ajb = s["aj"].astype(BF16)
            if span < c:
                both = _dot(jnp.concatenate([s["aj"], s["m"]], axis=0).astype(BF16), ajb)
                s["aj"], s["m"] = both[:c], s["m"] + both[c:]
            else:
                s["m"] = s["m"] + _dot(s["m"].astype(BF16), ajb)
    for s in st:
        s["wu"] = _dot(s["m"].astype(BF16), jnp.concatenate([s["kt"], s["av"]], axis=1).astype(BF16))
    for s in st:
        w_, u0, v = s["wu"][:, :d], s["wu"][:, d:], s["v"]
        rhs = jnp.concatenate([jnp.concatenate([v, jnp.zeros_like(v)], axis=1),
                               jnp.concatenate([u0, w_], axis=1)], axis=0).astype(BF16)
        s["yq"] = _dot(s["p"], rhs)
        s["h_t"] = _dot_tn(jnp.concatenate([v, -u0], axis=0).astype(BF16), s["kb"])
        s["bw"] = _dot_tn(s["b_"], w_.astype(BF16))
    out = []
    for s in st:
        q = s["rt"] + s["yq"][:, d:]
        y = _dot_nt(q.astype(BF16), s["ttb"]) + s["yq"][:, :d]
        tt = (s["tt"] - _dot_nt(s["ttb"], s["bw"].astype(BF16)) + s["h_t"]) * s["decay"]
        out.append((y, tt))
    return out


def _rwkv_scan_kernel(rf, kf, vf, kkf, bf, lwf, rb, kb, vb, kkb, bb, lwb, yf_ref, yb_ref, tf_ref, tb_ref):
    heads, ts = rf.shape[1], rf.shape[2]
    n_chunks = ts // B_CHUNK

    @pl.when(pl.program_id(1) == 0)
    def _():
        tf_ref[...] = jnp.zeros_like(tf_ref)
        tb_ref[...] = jnp.zeros_like(tb_ref)

    def body(j, carry):
        rows_f = pl.ds(pl.multiple_of(j * B_CHUNK, B_CHUNK), B_CHUNK)
        rows_b = pl.ds(pl.multiple_of((n_chunks - 1 - j) * B_CHUNK, B_CHUNK), B_CHUNK)
        chains = []
        for h in range(heads):
            chains.append([ref[0, h, rows_f, :] for ref in (rf, kf, vf, kkf, bf, lwf)] + [tf_ref[h], False])
            chains.append([ref[0, h, rows_b, :] for ref in (rb, kb, vb, kkb, bb, lwb)] + [tb_ref[h], True])
        res = _rwkv_chunks(chains)
        for h in range(heads):
            yf_ref[0, h, rows_f, :], tf_ref[h] = res[2 * h]
            yb_ref[0, h, rows_b, :], tb_ref[h] = res[2 * h + 1]
        return carry

    lax.fori_loop(0, n_chunks, body, 0)


def _rwkv_scan(r, k, v, kkn, b, lw_f, lw_b, ts=256):
    batch, heads, seq, d = r.shape
    nblk = seq // ts
    fwd = pl.BlockSpec((1, heads, ts, d), lambda bi, i: (bi, 0, i, 0))
    bwd = pl.BlockSpec((1, heads, ts, d), lambda bi, i: (bi, 0, nblk - 1 - i, 0))
    shape = jax.ShapeDtypeStruct(r.shape, F32)
    return pl.pallas_call(
        _rwkv_scan_kernel,
        grid=(batch, nblk),
        in_specs=[fwd] * 6 + [bwd] * 6,
        out_specs=[fwd, bwd],
        out_shape=[shape, shape],
        scratch_shapes=[pltpu.VMEM((heads, d, d), F32), pltpu.VMEM((heads, d, d), F32)],
        compiler_params=_params("parallel", "arbitrary"),
        name="rwkv_scan",
    )(r, k, v, kkn, b, lw_f, r, k, v, kkn, b, lw_b)


def _rwkv_post_kernel(yf_ref, yb_ref, g_ref, bonus_ref, lng_ref, lnb_ref, o_ref):
    heads = yf_ref.shape[1]
    outs = []
    for h in range(heads):
        y = yf_ref[0, h] + yb_ref[0, h]
        mean = jnp.mean(y, axis=-1, keepdims=True)
        var = jnp.mean(jnp.square(y - mean), axis=-1, keepdims=True)
        outs.append((y - mean) * lax.rsqrt(var + B_LN_EPS))
    yn = jnp.concatenate(outs, axis=1) * lng_ref[...] + lnb_ref[...]
    o_ref[...] = ((yn + bonus_ref[...]) * g_ref[...]).astype(o_ref.dtype)


def _rwkv_post(y_f, y_b, g, bonus, ln_g, ln_b, ts=256):
    batch, heads, seq, d = y_f.shape
    n, width = g.shape
    nblk = seq // ts
    hm = pl.BlockSpec((1, heads, ts, d), lambda b, i: (b, 0, i, 0))
    tokm = pl.BlockSpec((ts, width), lambda b, i: (b * nblk + i, 0))
    vec = pl.BlockSpec((1, width), lambda b, i: (0, 0))
    return pl.pallas_call(
        _rwkv_post_kernel,
        grid=(batch, nblk),
        in_specs=[hm, hm, tokm, tokm, vec, vec],
        out_specs=tokm,
        out_shape=jax.ShapeDtypeStruct((n, width), BF16),
        compiler_params=_params("parallel", "parallel"),
        name="rwkv_post",
    )(y_f, y_b, g, bonus, ln_g.reshape(1, width), ln_b.reshape(1, width))


def _rwkv7(p_b, mu, w0, w2, a0, a2, g2, k_k, k_a, r_k, ln_g, ln_b, batch, seq):
    r, k, v, kkn, b, lw_f, lw_b, g, bonus = _rwkv_prep(p_b, mu, w0, w2, a0, a2, g2, k_k, k_a, r_k, batch, seq)
    y_f, y_b = _rwkv_scan(r, k, v, kkn, b, lw_f, lw_b)
    return _rwkv_post(y_f, y_b, g, bonus, ln_g, ln_b)


_BUCKET_EDGES = (0, 1, 2, 3, 4, 5, 6, 7, 8, 12, 16, 23, 32, 46, 64, 91)
ATTN_TK = 128


def _bias_band_kernel(table_ref, o_ref):
    h = pl.program_id(0)
    n_d, tq, tk = o_ref.shape[1:]
    half = REL_BUCKETS // 2
    r = lax.broadcasted_iota(jnp.int32, (tq, tk), 0)
    c = lax.broadcasted_iota(jnp.int32, (tq, tk), 1)
    for d in range(n_d):
        rel = tk * (d - 1) + c - r
        n = jnp.abs(rel)
        vals = []
        for sign in range(2):
            val = jnp.full((tq, tk), table_ref[sign * half + half - 1, h], F32)
            for bkt in range(half - 2, -1, -1):
                val = jnp.where(n < _BUCKET_EDGES[bkt + 1], table_ref[sign * half + bkt, h], val)
            vals.append(val)
        o_ref[0, d] = jnp.where(rel > 0, vals[1], vals[0])


def _bias_band(rel_table, tq):
    heads = rel_table.shape[1]
    n_d = tq // ATTN_TK + 2
    return pl.pallas_call(
        _bias_band_kernel,
        grid=(heads,),
        in_specs=[pl.BlockSpec(memory_space=pltpu.SMEM)],
        out_specs=pl.BlockSpec((1, n_d, tq, ATTN_TK), lambda h: (h, 0, 0, 0)),
        out_shape=jax.ShapeDtypeStruct((heads, n_d, tq, ATTN_TK), F32),
        compiler_params=_params("parallel"),
        name="bias_band",
    )(rel_table)


def _diff_attn_kernel(table_ref, lam_ref, q_ref, k_ref, v_ref, band_ref, g_ref, o_ref, lg_ref, *, lam_init):
    h = pl.program_id(1)
    i = pl.program_id(2)
    tq, dh2 = q_ref.shape
    seq = k_ref.shape[0]
    n_d = band_ref.shape[1]
    n_kt = seq // ATTN_TK
    ratio = tq // ATTN_TK
    half = REL_BUCKETS // 2
    k = k_ref[...]
    v = v_ref[...]
    lane = lax.broadcasted_iota(jnp.int32, (1, dh2), 1)
    q = q_ref[...] * jnp.asarray(C_HEAD_DIM ** -0.5, q_ref.dtype)
    zero = jnp.zeros_like(q)
    first_band = ratio * i - 1
    col = lax.broadcasted_iota(jnp.int32, (1, seq), 1)
    far_row = jnp.where(col < first_band * ATTN_TK, table_ref[half - 1, h],
                        jnp.where(col >= (first_band + n_d) * ATTN_TK, table_ref[REL_BUCKETS - 1, h],
                                  -jnp.inf))
    row_max = []
    for m in range(2):
        qm = jnp.where((lane >= m * C_HEAD_DIM) & (lane < (m + 1) * C_HEAD_DIM), q, zero)
        far = _dot_nt(qm, k) + far_row
        mx = jnp.max(far, axis=-1, keepdims=True)
        lg_ref[m] = far
        for d in (0, n_d - 1) + tuple(range(1, n_d - 1)):
            kt = first_band + d
            valid = (kt >= 0) & (kt < n_kt)
            c0 = pl.multiple_of(jnp.clip(kt, 0, n_kt - 1) * ATTN_TK, ATTN_TK)
            near = _dot_nt(qm, k_ref[pl.ds(c0, ATTN_TK), :]) + band_ref[0, d]
            mx = jnp.maximum(mx, jnp.where(valid, jnp.max(near, axis=-1, keepdims=True), -jnp.inf))
            lg_ref[m, :, pl.ds(c0, ATTN_TK)] = near
        row_max.append(mx)
    es = [jnp.exp(lg_ref[m] - row_max[m]) for m in range(2)]
    parts = [_dot(e.astype(BF16), v) * (1.0 / jnp.sum(e, axis=-1, keepdims=True)) for e in es]
    o = parts[0] - lam_ref[0] * parts[1]
    o = o * lax.rsqrt(jnp.mean(o * o, axis=-1, keepdims=True) + NORM_EPS) * g_ref[...]
    o_ref[...] = (o * (1.0 - lam_init)).astype(o_ref.dtype)


def _diff_attention(qkv, rel_table, lam_full, subln_g, lam_init, batch, seq, tq=256):
    n = qkv.shape[0]
    dh2 = 2 * C_HEAD_DIM
    heads = qkv.shape[1] // (3 * dh2)
    nq = seq // tq
    band = _bias_band(rel_table, tq)
    return pl.pallas_call(
        functools.partial(_diff_attn_kernel, lam_init=lam_init),
        grid=(batch, heads, nq),
        in_specs=[
            pl.BlockSpec(memory_space=pltpu.SMEM),
            pl.BlockSpec(memory_space=pltpu.SMEM),
            pl.BlockSpec((tq, dh2), lambda b, h, i: (b * nq + i, h)),
            pl.BlockSpec((seq, dh2), lambda b, h, i: (b, heads + h)),
            pl.BlockSpec((seq, dh2), lambda b, h, i: (b, 2 * heads + h)),
            pl.BlockSpec((1,) + band.shape[1:], lambda b, h, i: (h, 0, 0, 0)),
            pl.BlockSpec((1, dh2), lambda b, h, i: (0, 0)),
        ],
        out_specs=pl.BlockSpec((tq, dh2), lambda b, h, i: (b * nq + i, h)),
        out_shape=jax.ShapeDtypeStruct((n, heads * dh2), BF16),
        scratch_shapes=[pltpu.VMEM((2, tq, seq), F32)],
        compiler_params=_params("parallel", "parallel", "parallel"),
        name="diff_attention",
    )(rel_table, lam_full.reshape(1), qkv, qkv, qkv, band, subln_g.reshape(1, dh2))


MOE_TILE = 256
EXPERT_TILE = 256
_HI_MASK = np.uint32(0xFFFF0000)


def _pack_bf16_pair(lo, hi):
    as_bits = lambda t: lax.bitcast_convert_type(t.astype(BF16).astype(F32), jnp.uint32)
    return (as_bits(hi) & _HI_MASK) | (as_bits(lo) >> 16)


def _unpack_bf16_pair(word):
    lo = lax.bitcast_convert_type(word << 16, F32).astype(BF16)
    hi = lax.bitcast_convert_type(word & _HI_MASK, F32).astype(BF16)
    return lo, hi


def _router_kernel(x_ref, g_ref, sh_ref, sc_ref, rw_ref, rb_ref, hp_ref, idx_ref, w_ref, rank_ref, cnt_ref,
                   run_ref, *, n_experts):
    @pl.when(pl.program_id(0) == 0)
    def _():
        run_ref[...] = jnp.zeros_like(run_ref)

    tm, d = x_ref.shape
    h = _modulated_norm(x_ref[...], g_ref[...], sh_ref[0], sc_ref[0])
    hp_ref[...] = _pack_bf16_pair(h[:, :d // 2], h[:, d // 2:])
    logits = _dot(h.astype(BF16), rw_ref[...]) + rb_ref[...]
    lane = lax.broadcasted_iota(jnp.int32, logits.shape, 1).astype(F32)
    neg = jnp.float32(-jnp.inf)
    logits = jnp.where(lane < n_experts, logits, neg)
    picks, vals = [], []
    for _ in range(TOP_K):
        m = jnp.max(logits, axis=-1, keepdims=True)
        first = jnp.min(jnp.where(logits == m, lane, float(LANES)), axis=-1, keepdims=True)
        hit = lane == first
        picks.append((first, hit))
        vals.append(m)
        logits = jnp.where(hit, neg, logits)
    es = [jnp.exp(v - vals[0]) for v in vals]
    inv = 1.0 / (es[0] + es[1] + es[2] + es[3])
    assigned = picks[0][1] | picks[1][1] | picks[2][1] | picks[3][1]
    onehot = jnp.where(assigned, 1.0, 0.0)
    ri = lax.broadcasted_iota(jnp.int32, (tm, tm), 0)
    ci = lax.broadcasted_iota(jnp.int32, (tm, tm), 1)
    earlier = jnp.where(ci < ri, 1.0, 0.0).astype(BF16)
    before = _dot(earlier, onehot.astype(BF16)) + run_ref[...]
    idx_o = jnp.zeros(logits.shape, F32)
    w_o = jnp.zeros(logits.shape, F32)
    rank_o = jnp.zeros(logits.shape, F32)
    for j, (first, hit) in enumerate(picks):
        rank_j = jnp.sum(jnp.where(hit, before, 0.0), axis=-1, keepdims=True)
        idx_o = jnp.where(lane == j, first, idx_o)
        w_o = jnp.where(lane == j, es[j] * inv, w_o)
        rank_o = jnp.where(lane == j, rank_j, rank_o)
    idx_ref[...] = idx_o
    w_ref[...] = w_o
    rank_ref[...] = rank_o
    run_ref[...] += jnp.sum(onehot, axis=0, keepdims=True)
    cnt_ref[...] = run_ref[...]


def _router(x2, g, mod_l, shift_idx, seq, router_w, router_b):
    n, d = x2.shape
    tm = MOE_TILE
    n_experts = router_w.shape[1]
    per_batch = seq // tm
    rw = jnp.zeros((d, LANES), BF16).at[:, :n_experts].set(router_w.astype(BF16))
    rb = jnp.zeros((1, LANES), F32).at[0, :n_experts].set(router_b)
    lane_out = pl.BlockSpec((tm, LANES), lambda i: (i, 0))
    lane_shape = jax.ShapeDtypeStruct((n, LANES), F32)
    return pl.pallas_call(
        functools.partial(_router_kernel, n_experts=n_experts),
        grid=(n // tm,),
        in_specs=[
            pl.BlockSpec((tm, d), lambda i: (i, 0)),
            pl.BlockSpec((1, d), lambda i: (0, 0)),
            pl.BlockSpec((1, 1, d), lambda i: (i // per_batch, 0, shift_idx)),
            pl.BlockSpec((1, 1, d), lambda i: (i // per_batch, 0, shift_idx + 1)),
            pl.BlockSpec((d, LANES), lambda i: (0, 0)),
            pl.BlockSpec((1, LANES), lambda i: (0, 0)),
        ],
        out_specs=[pl.BlockSpec((tm, d // 2), lambda i: (i, 0)), lane_out, lane_out, lane_out,
                   pl.BlockSpec((1, LANES), lambda i: (0, 0))],
        out_shape=[jax.ShapeDtypeStruct((n, d // 2), jnp.uint32), lane_shape, lane_shape, lane_shape,
                   jax.ShapeDtypeStruct((1, LANES), F32)],
        scratch_shapes=[pltpu.VMEM((1, LANES), F32)],
        compiler_params=_params("arbitrary"),
        name="moe_router",
    )(x2, g.reshape(1, d), mod_l, mod_l, rw, rb)


GLU_TILE = 2 * LANES


def _experts_kernel(te_ref, nu_ref, slot_ref, nxt_ref, xs_ref, b1_ref, b2_ref, perm_ref, w1_hbm, w2_hbm,
                    ys_ref, w1buf, w2buf, w1p_ref, w2b_ref, sem, *, layer):
    t = pl.program_id(0)
    used = t < nu_ref[0]
    new_expert = (t == 0) | (te_ref[t] != te_ref[jnp.maximum(t - 1, 0)])
    n_col_tiles = w1p_ref.shape[1] // GLU_TILE

    def weight_copies(e, slot):
        return (pltpu.make_async_copy(w1_hbm.at[layer, e], w1buf.at[slot], sem.at[0, slot]),
                pltpu.make_async_copy(w2_hbm.at[layer, e], w2buf.at[slot], sem.at[1, slot]))

    @pl.when(used & new_expert)
    def _():
        slot = slot_ref[t]

        @pl.when(t == 0)
        def _():
            for cp in weight_copies(te_ref[t], slot):
                cp.start()

        for cp in weight_copies(te_ref[t], slot):
            cp.wait()

        @pl.when(nxt_ref[t] >= 0)
        def _():
            for cp in weight_copies(nxt_ref[t], 1 - slot):
                cp.start()

        for c in range(n_col_tiles):
            cols = slice(c * GLU_TILE, (c + 1) * GLU_TILE)
            w1p_ref[:, cols] = _dot(w1buf[slot, :, cols].astype(BF16), perm_ref[...]).astype(BF16)
        w2b_ref[...] = w2buf[slot].astype(BF16)

    @pl.when(used)
    def _():
        lo, hi = _unpack_bf16_pair(xs_ref[...])
        x = jnp.concatenate([lo, hi], axis=1)
        hh = _dot(x, w1p_ref[...]) + b1_ref[0]
        acts = []
        for c in range(n_col_tiles):
            x_glu = jnp.minimum(hh[:, c * GLU_TILE:c * GLU_TILE + LANES], SWIGLU_LIMIT)
            x_lin = jnp.clip(hh[:, c * GLU_TILE + LANES:(c + 1) * GLU_TILE], -SWIGLU_LIMIT, SWIGLU_LIMIT)
            acts.append((x_glu * _sigmoid(SWIGLU_ALPHA * x_glu) * (x_lin + 1.0)).astype(BF16))
        out = _dot(jnp.concatenate(acts, axis=1), w2b_ref[...]) + b2_ref[0]
        half = out.shape[1] // 2
        ys_ref[...] = _pack_bf16_pair(out[:, :half], out[:, half:])

    @pl.when(t >= nu_ref[0])
    def _():
        ys_ref[...] = jnp.zeros_like(ys_ref)


def _experts(xs, tile_expert, n_used, slot, nxt, w1_all, b1, w2_all, b2, layer):
    n_rows, dw = xs.shape
    tm = EXPERT_TILE
    _, n_exp, d, de2 = w1_all.shape
    src = np.arange(GLU_TILE)
    dst = np.where(src % 2 == 0, src // 2, LANES + src // 2)
    perm = jnp.asarray(dst[:, None] == np.arange(GLU_TILE)[None, :], BF16)
    b1p = b1.reshape(n_exp, de2 // GLU_TILE, LANES, 2).transpose(0, 1, 3, 2).reshape(n_exp, 1, de2)
    row = lambda t, te, nu, sl, nx: (jnp.maximum(jnp.minimum(t, nu[0] - 1), 0), 0)
    exp3 = lambda t, te, nu, sl, nx: (te[t], 0, 0)
    grid_spec = pltpu.PrefetchScalarGridSpec(
        num_scalar_prefetch=4,
        grid=(n_rows // tm,),
        in_specs=[
            pl.BlockSpec((tm, dw), row),
            pl.BlockSpec((1, 1, de2), exp3),
            pl.BlockSpec((1, 1, d), exp3),
            pl.BlockSpec((GLU_TILE, GLU_TILE), lambda t, te, nu, sl, nx: (0, 0)),
            pl.BlockSpec(memory_space=pl.ANY),
            pl.BlockSpec(memory_space=pl.ANY),
        ],
        out_specs=pl.BlockSpec((tm, d // 2), lambda t, te, nu, sl, nx: (t, 0)),
        scratch_shapes=[pltpu.VMEM((2, d, de2), F32), pltpu.VMEM((2, de2 // 2, d), F32),
                        pltpu.VMEM((d, de2), BF16), pltpu.VMEM((de2 // 2, d), BF16),
                        pltpu.SemaphoreType.DMA((2, 2))],
    )
    return pl.pallas_call(
        functools.partial(_experts_kernel, layer=layer),
        grid_spec=grid_spec,
        out_shape=jax.ShapeDtypeStruct((n_rows, d // 2), jnp.uint32),
        compiler_params=_params("arbitrary"),
        name="moe_experts",
    )(tile_expert, n_used, slot, nxt, xs, b1p, b2.reshape(n_exp, 1, d), perm, w1_all, w2_all)


SC_WINDOW = 128


def _sc_worker_rows(n_rows):
    sc = plsc.get_sparse_core_info()
    per_worker = n_rows // (sc.num_cores * sc.num_subcores)
    worker = lax.axis_index("subcore") * sc.num_cores + lax.axis_index("core")
    return worker * per_worker, per_worker


def _sc_mesh():
    return plsc.VectorSubcoreMesh(core_axis_name="core", subcore_axis_name="subcore")


def _sc_gather_rows(table, idx):
    n_idx, width = idx.shape[0], table.shape[1]

    @functools.partial(
        pl.kernel, out_type=jax.ShapeDtypeStruct((n_idx, width), table.dtype), mesh=_sc_mesh(),
        scratch_types=[pltpu.VMEM((SC_WINDOW,), jnp.int32), pltpu.VMEM((SC_WINDOW, width), table.dtype)],
        name="sc_gather_rows")
    def gather(table_hbm, idx_hbm, out_hbm, idx_v, rows_v):
        first, count = _sc_worker_rows(n_idx)

        @pl.loop(0, count // SC_WINDOW)
        def _(c):
            rows = pl.ds(first + c * SC_WINDOW, SC_WINDOW)
            pltpu.sync_copy(idx_hbm.at[rows], idx_v)
            pltpu.sync_copy(table_hbm.at[idx_v], rows_v)
            pltpu.sync_copy(rows_v, out_hbm.at[rows])

    return gather(table, idx)


def _sc_scatter_rows(rows, pos, n_out):
    n, width = rows.shape
    k = pos.shape[0]

    @functools.partial(
        pl.kernel, out_type=jax.ShapeDtypeStruct((n_out, width), rows.dtype), mesh=_sc_mesh(),
        scratch_types=[pltpu.VMEM((k, SC_WINDOW), jnp.int32), pltpu.VMEM((SC_WINDOW, width), rows.dtype)],
        name="sc_scatter_rows")
    def scatter(rows_hbm, pos_hbm, out_hbm, pos_v, rows_v):
        first, count = _sc_worker_rows(n)

        @pl.loop(0, count // SC_WINDOW)
        def _(c):
            src = pl.ds(first + c * SC_WINDOW, SC_WINDOW)
            pltpu.sync_copy(pos_hbm.at[:, src], pos_v)
            pltpu.sync_copy(rows_hbm.at[src], rows_v)
            for j in range(k):
                pltpu.sync_copy(rows_v, out_hbm.at[pos_v.at[j]])

    return scatter(rows, pos)


def _combine_dense_kernel(x_ref, gate_ref, w_ref, fg_ref, *rest, final_norm):
    y_refs, o_ref = rest[:TOP_K], rest[TOP_K]
    w = w_ref[...]
    y_lo = y_hi = None
    for j in range(TOP_K):
        lo, hi = _unpack_bf16_pair(y_refs[j][...])
        wj = w[:, j:j + 1]
        y_lo = wj * lo if j == 0 else y_lo + wj * lo
        y_hi = wj * hi if j == 0 else y_hi + wj * hi
    out = x_ref[...] + gate_ref[0] * jnp.concatenate([y_lo, y_hi], axis=1)
    if final_norm:
        out = out * lax.rsqrt(jnp.mean(out * out, axis=-1, keepdims=True) + NORM_EPS) * fg_ref[...]
    o_ref[...] = out


def _combine_dense(x2, mod_l, gate_idx, seq, top_w, y4, final_g, final_norm, tm=512):
    n, d = x2.shape
    per_batch = seq // tm
    blocks = n // tm
    slot_spec = lambda j: pl.BlockSpec((tm, d // 2), lambda i: (j * blocks + i, 0))
    return pl.pallas_call(
        functools.partial(_combine_dense_kernel, final_norm=final_norm),
        grid=(blocks,),
        in_specs=[
            pl.BlockSpec((tm, d), lambda i: (i, 0)),
            pl.BlockSpec((1, 1, d), lambda i: (i // per_batch, 0, gate_idx)),
            pl.BlockSpec((tm, LANES), lambda i: (i, 0)),
            pl.BlockSpec((1, d), lambda i: (0, 0)),
        ] + [slot_spec(j) for j in range(TOP_K)],
        out_specs=pl.BlockSpec((tm, d), lambda i: (i, 0)),
        out_shape=jax.ShapeDtypeStruct((n, d), F32),
        compiler_params=_params("parallel"),
        name="moe_combine",
    )(x2, mod_l, top_w, final_g.reshape(1, d), *([y4] * TOP_K))


def _moe_ffn(x2, norm_g, mod_l, seq, router_w, router_b, w1_all, b1, w2_all, b2, layer, final_g, final_norm):
    n, d = x2.shape
    n_exp = router_w.shape[1]
    tile = EXPERT_TILE
    hp, idx_f, top_w, rank_f, counts = _router(x2, norm_g, mod_l, 3, seq, router_w, router_b)
    counts = counts[0, :n_exp].astype(jnp.int32)
    padded = (counts + tile - 1) // tile * tile
    ends = jnp.cumsum(padded)
    starts = ends - padded
    idx = idx_f[:, :TOP_K].astype(jnp.int32)
    expert_ids = jnp.arange(n_exp, dtype=jnp.int32)
    start_of = jnp.sum(jnp.where(idx[..., None] == expert_ids, starts, 0), axis=-1)
    pos_flat = (start_of + rank_f[:, :TOP_K].astype(jnp.int32)).reshape(-1)
    n_rows = n * TOP_K + n_exp * tile
    n_tiles = n_rows // tile
    tile_start = jnp.arange(n_tiles, dtype=jnp.int32) * tile
    tile_expert = jnp.sum((ends[None, :] <= tile_start[:, None]).astype(jnp.int32), axis=1)
    tile_expert = jnp.minimum(tile_expert, n_exp - 1)
    n_used = (ends[-1:] // tile).astype(jnp.int32)
    nonempty = counts > 0
    ordinal = jnp.cumsum(nonempty.astype(jnp.int32)) - 1
    later = nonempty[None, :] & (expert_ids[None, :] > expert_ids[:, None])
    nxt_e = jnp.min(jnp.where(later, expert_ids[None, :], n_exp), axis=1)
    nxt_e = jnp.where(nxt_e == n_exp, -1, nxt_e)
    slot = (ordinal[tile_expert] % 2).astype(jnp.int32)
    nxt = nxt_e[tile_expert].astype(jnp.int32)
    pos_slot_major = pos_flat.reshape(n, TOP_K).T
    xs = _sc_scatter_rows(hp, pos_slot_major, n_rows)
    ys = _experts(xs, tile_expert, n_used, slot, nxt, w1_all, b1, w2_all, b2, layer)
    y4 = _sc_gather_rows(ys, pos_slot_major.reshape(-1))
    return _combine_dense(x2, mod_l, 5, seq, top_w, y4, final_g, final_norm)


def kernel(x, c, ada_w, ada_b, norm_mix_g, norm_ffn_g, router_w, router_b, moe_w1, moe_b1, moe_w2, moe_b2, ab_w_in, ab_w_out, hgrn_lb, hgrn_norm_g, rwkv_mu, rwkv_w0, rwkv_w2, rwkv_a0, rwkv_a2, rwkv_g2, rwkv_k_k, rwkv_k_a, rwkv_r_k, rwkv_ln_g, rwkv_ln_b, attn_w_in, attn_w_out, attn_lambda, attn_subln_g, rel_bias_table, final_norm_g):
    batch, seq, d = x.shape
    n = batch * seq
    depth = ada_w.shape[0]
    x2 = x.reshape(n, d)
    mod = _adaln(c, ada_w, ada_b)
    lb_all = jnp.cumsum(jax.nn.softmax(hgrn_lb.astype(F32), axis=1), axis=1)
    for layer in range(depth):
        mod_l = mod[layer].reshape(batch, 1, 6 * d)
        j = layer // 2
        if layer % 2 == 0:
            a_cols = 5 * (d // 2)
            w_in = ab_w_in[j].astype(BF16)
            p_a, p_b = _normmod_proj(x2, norm_mix_g[layer], mod_l, 0, seq,
                                     [w_in[:, :a_cols], w_in[:, a_cols:]], [F32, F32])
            y_a = _hgrn2(p_a, lb_all[:, j], hgrn_norm_g[j], batch, seq)
            y_b = _rwkv7(p_b, rwkv_mu[j], rwkv_w0[j], rwkv_w2[j], rwkv_a0[j], rwkv_a2[j], rwkv_g2[j],
                         rwkv_k_k[j], rwkv_k_a[j], rwkv_r_k[j], rwkv_ln_g[j], rwkv_ln_b[j], batch, seq)
            w_out = ab_w_out[j].astype(BF16)
            x2 = _outproj_residual(x2, mod_l, 2, seq, [y_a, y_b], [w_out[:d // 2], w_out[d // 2:]])
        else:
            (qkv,) = _normmod_proj(x2, norm_mix_g[layer], mod_l, 0, seq, [attn_w_in[j].astype(BF16)], [BF16])
            lam = attn_lambda[j].astype(F32)
            lam_init = 0.8 - 0.6 * math.exp(-0.3 * layer)
            lam_full = jnp.exp(jnp.sum(lam[0] * lam[1])) - jnp.exp(jnp.sum(lam[2] * lam[3])) + lam_init
            o = _diff_attention(qkv, rel_bias_table, lam_full, attn_subln_g[j], lam_init, batch, seq)
            x2 = _outproj_residual(x2, mod_l, 2, seq, [o], [attn_w_out[j].astype(BF16)])
        x2 = _moe_ffn(x2, norm_ffn_g[layer], mod_l, seq, router_w[layer], router_b[layer], moe_w1,
                      moe_b1[layer], moe_w2, moe_b2[layer], layer, final_norm_g, layer == depth - 1)
    return x2.reshape(batch, seq, d)
```

```python
import functools
import math

import jax
import jax.numpy as jnp
import numpy as np
from jax import lax
from jax.experimental import pallas as pl
from jax.experimental.pallas import tpu as pltpu
from jax.experimental.pallas import tpu_sc as plsc

F32 = jnp.float32
BF16 = jnp.bfloat16
HIGHEST = lax.Precision.HIGHEST

NORM_EPS = 1e-6

A_HEAD_DIM = 128
A_CHUNK = 32
A_GROUP = 128
A_HEADS_PER_STEP = 2
B_HEAD_DIM = 64
B_CHUNK = 64
B_GROUP = 8
B_LN_EPS = 1e-5 * B_HEAD_DIM
C_HEAD_DIM = 64
REL_BUCKETS = 32
REL_MAX_DISTANCE = 128
TOP_K = 4
SWIGLU_LIMIT = 7.0
SWIGLU_ALPHA = 1.702

V7X_VMEM_BYTES = 64 * 1024 * 1024
VMEM_LIMIT = V7X_VMEM_BYTES - 8 * 1024 * 1024
LANES = 128


def _params(*sem):
    return pltpu.CompilerParams(dimension_semantics=sem, vmem_limit_bytes=VMEM_LIMIT)


def _sigmoid(x):
    return 1.0 / (1.0 + jnp.exp(-x))


def _silu(x):
    return x * _sigmoid(x)


def _dot(a, b):
    return jnp.dot(a, b, preferred_element_type=F32)


def _dot_nt(a, b):
    return lax.dot_general(a, b, (((1,), (1,)), ((), ())), preferred_element_type=F32)


def _dot_tn(a, b):
    return lax.dot_general(a, b, (((0,), (0,)), ((), ())), preferred_element_type=F32)


def _adaln_kernel(c_ref, w_ref, b_ref, o_ref):
    cond = _silu(c_ref[...])
    o_ref[0] = jnp.dot(cond, w_ref[0], precision=HIGHEST, preferred_element_type=F32) + b_ref[0]


def _adaln(c, ada_w, ada_b):
    n_layers, d, n_out = ada_w.shape
    batch = c.shape[0]
    tn = 1536
    return pl.pallas_call(
        _adaln_kernel,
        grid=(n_layers, n_out // tn),
        in_specs=[
            pl.BlockSpec((batch, d), lambda l, j: (0, 0)),
            pl.BlockSpec((1, d, tn), lambda l, j: (l, 0, j)),
            pl.BlockSpec((1, 1, tn), lambda l, j: (l, 0, j)),
        ],
        out_specs=pl.BlockSpec((1, batch, tn), lambda l, j: (l, 0, j)),
        out_shape=jax.ShapeDtypeStruct((n_layers, batch, n_out), F32),
        compiler_params=_params("parallel", "parallel"),
        name="adaln",
    )(c, ada_w, ada_b.reshape(n_layers, 1, n_out))


def _modulated_norm(x, g, shift, scale):
    y = x * lax.rsqrt(jnp.mean(x * x, axis=-1, keepdims=True) + NORM_EPS)
    return (y * g) * (1.0 + scale) + shift


def _normmod_proj_kernel(x_ref, g_ref, sh_ref, sc_ref, *rest, n_w):
    w_refs, o_refs = rest[:n_w], rest[n_w:]
    h = _modulated_norm(x_ref[...], g_ref[...], sh_ref[0], sc_ref[0]).astype(BF16)
    for w_ref, o_ref in zip(w_refs, o_refs):
        o_ref[...] = _dot(h, w_ref[...]).astype(o_ref.dtype)


def _normmod_proj(x2, g, mod_l, shift_idx, seq, weights, out_dtypes, tm=256):
    n, d = x2.shape
    per_batch = seq // tm
    in_specs = [
        pl.BlockSpec((tm, d), lambda i: (i, 0)),
        pl.BlockSpec((1, d), lambda i: (0, 0)),
        pl.BlockSpec((1, 1, d), lambda i: (i // per_batch, 0, shift_idx)),
        pl.BlockSpec((1, 1, d), lambda i: (i // per_batch, 0, shift_idx + 1)),
    ]
    in_specs += [pl.BlockSpec(w.shape, lambda i: (0, 0)) for w in weights]
    out_specs = [pl.BlockSpec((tm, w.shape[1]), lambda i: (i, 0)) for w in weights]
    out_shape = [jax.ShapeDtypeStruct((n, w.shape[1]), dt) for w, dt in zip(weights, out_dtypes)]
    return pl.pallas_call(
        functools.partial(_normmod_proj_kernel, n_w=len(weights)),
        grid=(n // tm,),
        in_specs=in_specs,
        out_specs=out_specs,
        out_shape=out_shape,
        compiler_params=_params("parallel"),
        name="normmod_proj",
    )(x2, g.reshape(1, d), mod_l, mod_l, *weights)


def _outproj_kernel(x_ref, gate_ref, *rest, n_y):
    y_refs, w_refs, o_ref = rest[:n_y], rest[n_y:2 * n_y], rest[2 * n_y]
    acc = _dot(y_refs[0][...], w_refs[0][...])
    for y_ref, w_ref in zip(y_refs[1:], w_refs[1:]):
        acc += _dot(y_ref[...], w_ref[...])
    o_ref[...] = x_ref[...] + gate_ref[0] * acc


def _outproj_residual(x2, mod_l, gate_idx, seq, ys, ws, tm=512):
    n, d = x2.shape
    per_batch = seq // tm
    in_specs = [
        pl.BlockSpec((tm, d), lambda i: (i, 0)),
        pl.BlockSpec((1, 1, d), lambda i: (i // per_batch, 0, gate_idx)),
    ]
    in_specs += [pl.BlockSpec((tm, y.shape[1]), lambda i: (i, 0)) for y in ys]
    in_specs += [pl.BlockSpec(w.shape, lambda i: (0, 0)) for w in ws]
    return pl.pallas_call(
        functools.partial(_outproj_kernel, n_y=len(ys)),
        grid=(n // tm,),
        in_specs=in_specs,
        out_specs=pl.BlockSpec((tm, d), lambda i: (i, 0)),
        out_shape=jax.ShapeDtypeStruct((n, d), F32),
        compiler_params=_params("parallel"),
        name="outproj_residual",
    )(x2, mod_l, *ys, *ws)


def _chunk_cumsum(x, chunk, reverse):
    rows = x.shape[0]
    pos = lax.broadcasted_iota(jnp.int32, x.shape, 0) % chunk
    s = 1
    while s < chunk:
        if reverse:
            x = x + jnp.where(pos < chunk - s, pltpu.roll(x, rows - s, axis=0), 0.0)
        else:
            x = x + jnp.where(pos >= s, pltpu.roll(x, s, axis=0), 0.0)
        s *= 2
    return x


def _hgrn_groups(slabs):
    g_rows, dk = slabs[0][0].shape
    n_chunks = g_rows // A_CHUNK
    ti = lax.broadcasted_iota(jnp.int32, (A_CHUNK, A_CHUNK), 0)
    si = lax.broadcasted_iota(jnp.int32, (A_CHUNK, A_CHUNK), 1)
    chunk_rows = [slice(c * A_CHUNK, (c + 1) * A_CHUNK) for c in range(n_chunks)]
    prep = []
    for q, f, v, lb, st, reverse in slabs:
        fg = lb + (1.0 - lb) * _sigmoid(f)
        k = 1.0 - fg
        b = _chunk_cumsum(jnp.log(fg), A_CHUNK, reverse)
        b3 = b.reshape(n_chunks, A_CHUNK, dk)
        edge = b3[:, 0:1, :] if reverse else b3[:, A_CHUNK - 1:A_CHUNK, :]
        prep.append(dict(
            q_in=(q * jnp.exp(b)).astype(BF16), k_in=(k * jnp.exp(-b)).astype(BF16),
            k_st=(k.reshape(n_chunks, A_CHUNK, dk) * jnp.exp(edge - b3)).astype(BF16),
            decay=jnp.exp(edge), vb=v.astype(BF16), st=st, reverse=reverse,
            mask=(si >= ti) if reverse else (si <= ti)))
    for p in prep:
        p["scores"] = [_dot_nt(p["q_in"][r], p["k_in"][r]) for r in chunk_rows]
        p["dstate"] = [_dot_tn(p["vb"][r], p["k_st"][c]) for c, r in enumerate(chunk_rows)]
    for p in prep:
        p["intra"] = [_dot(jnp.where(p["mask"], s, 0.0).astype(BF16), p["vb"][r])
                      for s, r in zip(p["scores"], chunk_rows)]
    results = []
    for p in prep:
        st = p["st"]
        outs = [None] * n_chunks
        for c in (range(n_chunks - 1, -1, -1) if p["reverse"] else range(n_chunks)):
            outs[c] = p["intra"][c] + _dot_nt(p["q_in"][chunk_rows[c]], st.astype(BF16))
            st = st * p["decay"][c] + p["dstate"][c]
        results.append((jnp.concatenate(outs, axis=0), st))
    return results


def _hgrn_kernel(q_ref, ff_ref, fb_ref, i_ref, g_ref, lb_ref, ng_ref, o_ref, of_ref, ob_ref):
    seq = q_ref.shape[0]
    dk = A_HEAD_DIM
    n_heads = q_ref.shape[1] // dk
    n_groups = seq // A_GROUP
    head_cols = [slice(h * dk, (h + 1) * dk) for h in range(n_heads)]

    def body(j, states):
        rf = pl.ds(pl.multiple_of(j * A_GROUP, A_GROUP), A_GROUP)
        rb = pl.ds(pl.multiple_of((n_groups - 1 - j) * A_GROUP, A_GROUP), A_GROUP)
        slabs = []
        for h, cols in enumerate(head_cols):
            slabs.append((_silu(q_ref[rf, cols]), ff_ref[rf, cols], i_ref[rf, cols], lb_ref[0:1, cols],
                          states[2 * h], False))
            slabs.append((_silu(q_ref[rb, cols]), fb_ref[rb, cols], i_ref[rb, cols], lb_ref[1:2, cols],
                          states[2 * h + 1], True))
        results = _hgrn_groups(slabs)
        for h, cols in enumerate(head_cols):
            of_ref[rf, cols] = results[2 * h][0]
            ob_ref[rb, cols] = results[2 * h + 1][0]
        return tuple(r[1] for r in results)

    zero = jnp.zeros((dk, dk), F32)
    lax.fori_loop(0, n_groups, body, (zero,) * (2 * n_heads))

    def finish(j, carry):
        r = pl.ds(pl.multiple_of(j * A_GROUP, A_GROUP), A_GROUP)
        for cols in head_cols:
            o = of_ref[r, cols] + ob_ref[r, cols]
            o = o * lax.rsqrt(jnp.mean(o * o, axis=-1, keepdims=True) + NORM_EPS) * ng_ref[...]
            o_ref[r, cols] = (o * _silu(g_ref[r, cols])).astype(o_ref.dtype)
        return carry

    lax.fori_loop(0, n_groups, finish, 0)


def _hgrn2(p_a, lb, norm_g, batch, seq):
    n = p_a.shape[0]
    width = p_a.shape[1] // 5
    cols = A_HEADS_PER_STEP * A_HEAD_DIM
    steps = width // cols
    sect = lambda s: pl.BlockSpec((seq, cols), lambda b, h, s=s: (b, s * steps + h))
    return pl.pallas_call(
        _hgrn_kernel,
        grid=(batch, steps),
        in_specs=[sect(0), sect(1), sect(2), sect(3), sect(4),
                  pl.BlockSpec((2, cols), lambda b, h: (0, h)),
                  pl.BlockSpec((1, A_HEAD_DIM), lambda b, h: (0, 0))],
        out_specs=pl.BlockSpec((seq, cols), lambda b, h: (b, h)),
        out_shape=jax.ShapeDtypeStruct((n, width), BF16),
        scratch_shapes=[pltpu.VMEM((seq, cols), F32), pltpu.VMEM((seq, cols), F32)],
        compiler_params=_params("parallel", "parallel"),
        name="hgrn2",
    )(p_a, p_a, p_a, p_a, p_a, lb, norm_g.reshape(1, A_HEAD_DIM))


def _softplus(z):
    return jnp.maximum(z, 0.0) + jnp.log(1.0 + jnp.exp(-jnp.abs(z)))


def _rwkv_prep_kernel(p_ref, prev_ref, next_ref, mu_ref, w0_ref, w2_ref, a0_ref, a2_ref, g2_ref,
                      kk_ref, ka_ref, rk_ref, ones_ref, tri_ref,
                      r_out, k_out, v_out, kkn_out, b_out, cumf_out, cumb_out, g_out, bonus_out):
    i = pl.program_id(1)
    last = pl.num_programs(1) - 1
    ts = p_ref.shape[0]
    width = a0_ref.shape[1]
    heads = width // B_HEAD_DIM
    p = p_ref[...]
    prev_row = jnp.where(i == 0, 0.0, prev_ref[7:8, :])
    next_row = jnp.where(i == last, 0.0, next_ref[0:1, :])
    row = lax.broadcasted_iota(jnp.int32, (ts, 1), 0)
    up = jnp.where(row == 0, prev_row, pltpu.roll(p, 1, axis=0))
    dn = jnp.where(row == ts - 1, next_row, pltpu.roll(p, ts - 1, axis=0))
    p = p + mu_ref[...] * (0.5 * (up + dn) - p)

    r = p[:, 0:width]
    k = p[:, width:2 * width]
    v = p[:, 2 * width:3 * width]
    o = 3 * width
    wlo_f = p[:, o:o + 64]
    wlo_b = p[:, o + 64:o + 128]
    alo = p[:, o + 128:o + 192]
    glo = p[:, o + 192:o + 320]

    def log_decay(wlo, d):
        z = w0_ref[d:d + 1, :] + _dot(jnp.tanh(wlo).astype(BF16), w2_ref[d])
        w = -_softplus(-z) - 0.5
        return -jnp.exp(w)

    def split_dot(sel, t, right):
        hi = t.astype(BF16)
        lo = (t - hi.astype(F32)).astype(BF16)
        return (_dot(hi, sel) + _dot(lo, sel)) if right else (_dot(sel, hi) + _dot(sel, lo))

    cum_f = split_dot(tri_ref[0], log_decay(wlo_f, 0), False)
    cum_b = split_dot(tri_ref[1], log_decay(wlo_b, 1), False)
    a = _sigmoid(a0_ref[...] + _dot(alo.astype(BF16), a2_ref[...]))
    g = _dot(_sigmoid(glo).astype(BF16), g2_ref[...])
    kk = k * kk_ref[...]
    head_sum = lambda t: split_dot(ones_ref[...], t, True)

    kk_n = kk / jnp.maximum(jnp.sqrt(head_sum(kk * kk)), 1e-12)
    k_mod = k * (1.0 + (a - 1.0) * ka_ref[...])
    bonus = head_sum(r * k_mod * rk_ref[...]) * v
    g_out[...] = g
    bonus_out[...] = bonus
    for h in range(heads):
        sl = slice(h * B_HEAD_DIM, (h + 1) * B_HEAD_DIM)
        r_out[0, h] = r[:, sl]
        k_out[0, h] = k_mod[:, sl]
        v_out[0, h] = v[:, sl]
        kkn_out[0, h] = kk_n[:, sl]
        b_out[0, h] = (kk_n * a)[:, sl]
        cumf_out[0, h] = cum_f[:, sl]
        cumb_out[0, h] = cum_b[:, sl]


def _rwkv_prep(p_b, mu, w0, w2, a0, a2, g2, k_k, k_a, r_k, batch, seq, ts=256):
    n, cols = p_b.shape
    width = a0.shape[0]
    heads = width // B_HEAD_DIM
    nblk = seq // ts
    rows8 = ts // 8
    head_id = np.arange(width) // B_HEAD_DIM
    ones = jnp.asarray(head_id[:, None] == head_id[None, :], BF16)
    full = lambda a: pl.BlockSpec(a.shape, lambda b, i: (0,) * a.ndim)
    vec = lambda a: a.reshape(1, -1)
    row = np.arange(ts)
    same_chunk = (row[:, None] // B_CHUNK) == (row[None, :] // B_CHUNK)
    tri = jnp.asarray(np.stack([same_chunk & (row[None, :] <= row[:, None]),
                                same_chunk & (row[None, :] >= row[:, None])]), BF16)
    args = [vec(mu), w0, w2.astype(BF16), vec(a0), a2.astype(BF16), g2.astype(BF16),
            vec(k_k), vec(k_a), vec(r_k), ones, tri]
    hm = pl.BlockSpec((1, heads, ts, B_HEAD_DIM), lambda b, i: (b, 0, i, 0))
    tokm = pl.BlockSpec((ts, width), lambda b, i: (b * nblk + i, 0))
    hm_shape = jax.ShapeDtypeStruct((batch, heads, seq, B_HEAD_DIM), F32)
    tok_shape = jax.ShapeDtypeStruct((n, width), F32)
    return pl.pallas_call(
        _rwkv_prep_kernel,
        grid=(batch, nblk),
        in_specs=[
            pl.BlockSpec((ts, cols), lambda b, i: (b * nblk + i, 0)),
            pl.BlockSpec((8, cols), lambda b, i: (jnp.maximum((b * nblk + i) * rows8 - 1, 0), 0)),
            pl.BlockSpec((8, cols), lambda b, i: (jnp.minimum((b * nblk + i + 1) * rows8, n // 8 - 1), 0)),
        ] + [full(a) for a in args],
        out_specs=[hm] * 7 + [tokm] * 2,
        out_shape=[hm_shape] * 7 + [tok_shape] * 2,
        compiler_params=_params("parallel", "parallel"),
        name="rwkv_prep",
    )(p_b, p_b, p_b, *args)


def _rwkv_chunks(chains):
    c, d = chains[0][0].shape
    ti = lax.broadcasted_iota(jnp.int32, (c, c), 0)
    si = lax.broadcasted_iota(jnp.int32, (c, c), 1)
    eye = jnp.where(ti == si, 1.0, 0.0)
    masks = {rev: (((si > ti), (si >= ti)) if rev else ((si < ti), (si <= ti))) for rev in (False, True)}

    row = lax.broadcasted_iota(jnp.int32, (c, d), 0)
    st = []
    for r, k, v, kk, b, cum, tt, rev in chains:
        before = (jnp.where(row == c - 1, 0.0, pltpu.roll(cum, c - 1, axis=0)) if rev
                  else jnp.where(row == 0, 0.0, pltpu.roll(cum, 1, axis=0)))
        grow = jnp.exp(-cum)
        kt = kk * jnp.exp(before)
        rt = r * jnp.exp(cum)
        b_ = (b * grow).astype(BF16)
        kb = jnp.concatenate([(k * grow).astype(BF16), b_], axis=0)
        st.append(dict(v=v, vb=v.astype(BF16), kt=kt, rt=rt, b_=b_, kb=kb, tt=tt, ttb=tt.astype(BF16),
                       decay=jnp.exp(cum[0:1, :] if rev else cum[c - 1:c, :]), rev=rev,
                       lhs=jnp.concatenate([kt, rt], axis=0).astype(BF16)))
    for s in st:
        s["big"] = _dot_nt(s["lhs"], s["kb"])
    for s in st:
        strict, incl = masks[s["rev"]]
        big = s["big"]
        a_ab = jnp.where(strict, big[:c, c:], 0.0)
        s["a_ak"] = jnp.where(strict, big[:c, :c], 0.0).astype(BF16)
        s["p"] = jnp.concatenate([jnp.where(incl, big[c:, :c], 0.0), jnp.where(incl, -big[c:, c:], 0.0)],
                                 axis=1).astype(BF16)
        s["m"] = eye - a_ab
        s["ab"] = a_ab.astype(BF16)
    for s in st:
        s["aj"] = _dot(s["ab"], s["ab"])
        s["av"] = _dot(s["a_ak"], s["vb"])
    span = 2
    while span < c:
        span *= 2
        for s in st:
            ajb = s["aj"].astype(BF16)
            if span < c:
                both = _dot(jnp.concatenate([s["aj"], s["m"]], axis=0).astype(BF16), ajb)
                s["aj"], s["m"] = both[:c], s["m"] + both[c:]
            else:
                s["m"] = s["m"] + _dot(s["m"].astype(BF16), ajb)
    for s in st:
        s["wu"] = _dot(s["m"].astype(BF16), jnp.concatenate([s["kt"], s["av"]], axis=1).astype(BF16))
    for s in st:
        w_, u0, v = s["wu"][:, :d], s["wu"][:, d:], s["v"]
        rhs = jnp.concatenate([jnp.concatenate([v, jnp.zeros_like(v)], axis=1),
                               jnp.concatenate([u0, w_], axis=1)], axis=0).astype(BF16)
        s["yq"] = _dot(s["p"], rhs)
        s["h_t"] = _dot_tn(jnp.concatenate([v, -u0], axis=0).astype(BF16), s["kb"])
        s["bw"] = _dot_tn(s["b_"], w_.astype(BF16))
    out = []
    for s in st:
        q = s["rt"] + s["yq"][:, d:]
        y = _dot_nt(q.astype(BF16), s["ttb"]) + s["yq"][:, :d]
        tt = (s["tt"] - _dot_nt(s["ttb"], s["bw"].astype(BF16)) + s["h_t"]) * s["decay"]
        out.append((y, tt))
    return out


def _rwkv_scan_kernel(rf, kf, vf, kkf, bf, lwf, rb, kb, vb, kkb, bb, lwb, yf_ref, yb_ref, tf_ref, tb_ref):
    heads, ts = rf.shape[1], rf.shape[2]
    n_chunks = ts // B_CHUNK

    @pl.when(pl.program_id(1) == 0)
    def _():
        tf_ref[...] = jnp.zeros_like(tf_ref)
        tb_ref[...] = jnp.zeros_like(tb_ref)

    def body(j, carry):
        rows_f = pl.ds(pl.multiple_of(j * B_CHUNK, B_CHUNK), B_CHUNK)
        rows_b = pl.ds(pl.multiple_of((n_chunks - 1 - j) * B_CHUNK, B_CHUNK), B_CHUNK)
        for h0 in range(0, heads, B_GROUP):
            group = range(h0, h0 + B_GROUP)
            chains = []
            for h in group:
                chains.append([ref[0, h, rows_f, :] for ref in (rf, kf, vf, kkf, bf, lwf)] + [tf_ref[h], False])
                chains.append([ref[0, h, rows_b, :] for ref in (rb, kb, vb, kkb, bb, lwb)] + [tb_ref[h], True])
            res = _rwkv_chunks(chains)
            for n, h in enumerate(group):
                yf_ref[0, h, rows_f, :], tf_ref[h] = res[2 * n]
                yb_ref[0, h, rows_b, :], tb_ref[h] = res[2 * n + 1]
        return carry

    lax.fori_loop(0, n_chunks, body, 0)


def _rwkv_scan(r, k, v, kkn, b, lw_f, lw_b, ts=256):
    batch, heads, seq, d = r.shape
    nblk = seq // ts
    fwd = pl.BlockSpec((1, heads, ts, d), lambda bi, i: (bi, 0, i, 0))
    bwd = pl.BlockSpec((1, heads, ts, d), lambda bi, i: (bi, 0, nblk - 1 - i, 0))
    shape = jax.ShapeDtypeStruct(r.shape, F32)
    return pl.pallas_call(
        _rwkv_scan_kernel,
        grid=(batch, nblk),
        in_specs=[fwd] * 6 + [bwd] * 6,
        out_specs=[fwd, bwd],
        out_shape=[shape, shape],
        scratch_shapes=[pltpu.VMEM((heads, d, d), F32), pltpu.VMEM((heads, d, d), F32)],
        compiler_params=_params("parallel", "arbitrary"),
        name="rwkv_scan",
    )(r, k, v, kkn, b, lw_f, r, k, v, kkn, b, lw_b)


def _rwkv_post_kernel(yf_ref, yb_ref, g_ref, bonus_ref, lng_ref, lnb_ref, o_ref):
    heads = yf_ref.shape[1]
    outs = []
    for h in range(heads):
        y = yf_ref[0, h] + yb_ref[0, h]
        mean = jnp.mean(y, axis=-1, keepdims=True)
        var = jnp.mean(jnp.square(y - mean), axis=-1, keepdims=True)
        outs.append((y - mean) * lax.rsqrt(var + B_LN_EPS))
    yn = jnp.concatenate(outs, axis=1) * lng_ref[...] + lnb_ref[...]
    o_ref[...] = ((yn + bonus_ref[...]) * g_ref[...]).astype(o_ref.dtype)


def _rwkv_post(y_f, y_b, g, bonus, ln_g, ln_b, ts=256):
    batch, heads, seq, d = y_f.shape
    n, width = g.shape
    nblk = seq // ts
    hm = pl.BlockSpec((1, heads, ts, d), lambda b, i: (b, 0, i, 0))
    tokm = pl.BlockSpec((ts, width), lambda b, i: (b * nblk + i, 0))
    vec = pl.BlockSpec((1, width), lambda b, i: (0, 0))
    return pl.pallas_call(
        _rwkv_post_kernel,
        grid=(batch, nblk),
        in_specs=[hm, hm, tokm, tokm, vec, vec],
        out_specs=tokm,
        out_shape=jax.ShapeDtypeStruct((n, width), BF16),
        compiler_params=_params("parallel", "parallel"),
        name="rwkv_post",
    )(y_f, y_b, g, bonus, ln_g.reshape(1, width), ln_b.reshape(1, width))


def _rwkv7(p_b, mu, w0, w2, a0, a2, g2, k_k, k_a, r_k, ln_g, ln_b, batch, seq):
    r, k, v, kkn, b, lw_f, lw_b, g, bonus = _rwkv_prep(p_b, mu, w0, w2, a0, a2, g2, k_k, k_a, r_k, batch, seq)
    y_f, y_b = _rwkv_scan(r, k, v, kkn, b, lw_f, lw_b)
    return _rwkv_post(y_f, y_b, g, bonus, ln_g, ln_b)


_BUCKET_EDGES = (0, 1, 2, 3, 4, 5, 6, 7, 8, 12, 16, 23, 32, 46, 64, 91)
ATTN_TK = 128


def _bias_band_kernel(table_ref, o_ref):
    h = pl.program_id(0)
    n_d, tq, tk = o_ref.shape[1:]
    half = REL_BUCKETS // 2
    r = lax.broadcasted_iota(jnp.int32, (tq, tk), 0)
    c = lax.broadcasted_iota(jnp.int32, (tq, tk), 1)
    for d in range(n_d):
        rel = tk * (d - 1) + c - r
        n = jnp.abs(rel)
        vals = []
        for sign in range(2):
            val = jnp.full((tq, tk), table_ref[sign * half + half - 1, h], F32)
            for bkt in range(half - 2, -1, -1):
                val = jnp.where(n < _BUCKET_EDGES[bkt + 1], table_ref[sign * half + bkt, h], val)
            vals.append(val)
        o_ref[0, d] = jnp.where(rel > 0, vals[1], vals[0])


def _bias_band(rel_table, tq):
    heads = rel_table.shape[1]
    n_d = tq // ATTN_TK + 2
    return pl.pallas_call(
        _bias_band_kernel,
        grid=(heads,),
        in_specs=[pl.BlockSpec(memory_space=pltpu.SMEM)],
        out_specs=pl.BlockSpec((1, n_d, tq, ATTN_TK), lambda h: (h, 0, 0, 0)),
        out_shape=jax.ShapeDtypeStruct((heads, n_d, tq, ATTN_TK), F32),
        compiler_params=_params("parallel"),
        name="bias_band",
    )(rel_table)


def _diff_attn_kernel(table_ref, lam_ref, q_ref, k_ref, v_ref, band_ref, g_ref, o_ref, lg_ref, vext_ref, *,
                      lam_init):
    h = pl.program_id(1)
    i = pl.program_id(2)
    tq, dh2 = q_ref.shape
    seq = k_ref.shape[0]
    n_d = band_ref.shape[1]
    n_kt = seq // ATTN_TK
    ratio = tq // ATTN_TK
    half = REL_BUCKETS // 2
    k = k_ref[...]

    @pl.when(i == 0)
    def _():
        v = v_ref[...]
        vext_ref[...] = jnp.concatenate([v, jnp.ones_like(v)], axis=1)

    lane = lax.broadcasted_iota(jnp.int32, (1, dh2), 1)
    q = q_ref[...] * jnp.asarray(C_HEAD_DIM ** -0.5, q_ref.dtype)
    zero = jnp.zeros_like(q)
    first_band = ratio * i - 1
    col = lax.broadcasted_iota(jnp.int32, (1, seq), 1)
    far_row = jnp.where(col < first_band * ATTN_TK, table_ref[half - 1, h],
                        jnp.where(col >= (first_band + n_d) * ATTN_TK, table_ref[REL_BUCKETS - 1, h],
                                  -jnp.inf))
    row_max = []
    for m in range(2):
        qm = jnp.where((lane >= m * C_HEAD_DIM) & (lane < (m + 1) * C_HEAD_DIM), q, zero)
        far = _dot_nt(qm, k) + far_row
        mx = jnp.max(far, axis=-1, keepdims=True)
        lg_ref[m] = far
        for d in (0, n_d - 1) + tuple(range(1, n_d - 1)):
            kt = first_band + d
            valid = (kt >= 0) & (kt < n_kt)
            c0 = pl.multiple_of(jnp.clip(kt, 0, n_kt - 1) * ATTN_TK, ATTN_TK)
            near = _dot_nt(qm, k_ref[pl.ds(c0, ATTN_TK), :]) + band_ref[0, d]
            mx = jnp.maximum(mx, jnp.where(valid, jnp.max(near, axis=-1, keepdims=True), -jnp.inf))
            lg_ref[m, :, pl.ds(c0, ATTN_TK)] = near
        row_max.append(mx)
    parts = []
    for m in range(2):
        e = jnp.exp(lg_ref[m] - row_max[m]).astype(BF16)
        pv = _dot(e, vext_ref[...])
        parts.append(pv[:, :dh2] * (1.0 / pv[:, dh2:dh2 + 1]))
    o = parts[0] - lam_ref[0] * parts[1]
    o = o * lax.rsqrt(jnp.mean(o * o, axis=-1, keepdims=True) + NORM_EPS) * g_ref[...]
    o_ref[...] = (o * (1.0 - lam_init)).astype(o_ref.dtype)


def _diff_attention(qkv, rel_table, lam_full, subln_g, lam_init, batch, seq, tq=256):
    n = qkv.shape[0]
    dh2 = 2 * C_HEAD_DIM
    heads = qkv.shape[1] // (3 * dh2)
    nq = seq // tq
    band = _bias_band(rel_table, tq)
    return pl.pallas_call(
        functools.partial(_diff_attn_kernel, lam_init=lam_init),
        grid=(batch, heads, nq),
        in_specs=[
            pl.BlockSpec(memory_space=pltpu.SMEM),
            pl.BlockSpec(memory_space=pltpu.SMEM),
            pl.BlockSpec((tq, dh2), lambda b, h, i: (b * nq + i, h)),
            pl.BlockSpec((seq, dh2), lambda b, h, i: (b, heads + h)),
            pl.BlockSpec((seq, dh2), lambda b, h, i: (b, 2 * heads + h)),
            pl.BlockSpec((1,) + band.shape[1:], lambda b, h, i: (h, 0, 0, 0)),
            pl.BlockSpec((1, dh2), lambda b, h, i: (0, 0)),
        ],
        out_specs=pl.BlockSpec((tq, dh2), lambda b, h, i: (b * nq + i, h)),
        out_shape=jax.ShapeDtypeStruct((n, heads * dh2), BF16),
        scratch_shapes=[pltpu.VMEM((2, tq, seq), F32), pltpu.VMEM((seq, 2 * dh2), BF16)],
        compiler_params=_params("parallel", "parallel", "arbitrary"),
        name="diff_attention",
    )(rel_table, lam_full.reshape(1), qkv, qkv, qkv, band, subln_g.reshape(1, dh2))


MOE_TILE = 256
EXPERT_TILE = 256
_HI_MASK = np.uint32(0xFFFF0000)


def _pack_bf16_pair(lo, hi):
    as_bits = lambda t: lax.bitcast_convert_type(t.astype(BF16).astype(F32), jnp.uint32)
    return (as_bits(hi) & _HI_MASK) | (as_bits(lo) >> 16)


def _unpack_bf16_pair(word):
    lo = lax.bitcast_convert_type(word << 16, F32).astype(BF16)
    hi = lax.bitcast_convert_type(word & _HI_MASK, F32).astype(BF16)
    return lo, hi


def _router_kernel(x_ref, g_ref, sh_ref, sc_ref, rw_ref, rb_ref, hp_ref, idx_ref, w_ref, rank_ref, cnt_ref,
                   run_ref, *, n_experts):
    @pl.when(pl.program_id(0) == 0)
    def _():
        run_ref[...] = jnp.zeros_like(run_ref)

    tm, d = x_ref.shape
    h = _modulated_norm(x_ref[...], g_ref[...], sh_ref[0], sc_ref[0])
    hp_ref[...] = _pack_bf16_pair(h[:, :d // 2], h[:, d // 2:])
    logits = _dot(h.astype(BF16), rw_ref[...]) + rb_ref[...]
    lane = lax.broadcasted_iota(jnp.int32, logits.shape, 1).astype(F32)
    neg = jnp.float32(-jnp.inf)
    logits = jnp.where(lane < n_experts, logits, neg)
    picks, vals = [], []
    for _ in range(TOP_K):
        m = jnp.max(logits, axis=-1, keepdims=True)
        first = jnp.min(jnp.where(logits == m, lane, float(LANES)), axis=-1, keepdims=True)
        hit = lane == first
        picks.append((first, hit))
        vals.append(m)
        logits = jnp.where(hit, neg, logits)
    es = [jnp.exp(v - vals[0]) for v in vals]
    inv = 1.0 / (es[0] + es[1] + es[2] + es[3])
    assigned = picks[0][1] | picks[1][1] | picks[2][1] | picks[3][1]
    onehot = jnp.where(assigned, 1.0, 0.0)
    ri = lax.broadcasted_iota(jnp.int32, (tm, tm), 0)
    ci = lax.broadcasted_iota(jnp.int32, (tm, tm), 1)
    earlier = jnp.where(ci < ri, 1.0, 0.0).astype(BF16)
    before = _dot(earlier, onehot.astype(BF16)) + run_ref[...]
    idx_o = jnp.zeros(logits.shape, F32)
    w_o = jnp.zeros(logits.shape, F32)
    rank_o = jnp.zeros(logits.shape, F32)
    for j, (first, hit) in enumerate(picks):
        rank_j = jnp.sum(jnp.where(hit, before, 0.0), axis=-1, keepdims=True)
        idx_o = jnp.where(lane == j, first, idx_o)
        w_o = jnp.where(lane == j, es[j] * inv, w_o)
        rank_o = jnp.where(lane == j, rank_j, rank_o)
    idx_ref[...] = idx_o
    w_ref[...] = w_o
    rank_ref[...] = rank_o
    run_ref[...] += jnp.sum(onehot, axis=0, keepdims=True)
    cnt_ref[...] = run_ref[...]


def _router(x2, g, mod_l, shift_idx, seq, router_w, router_b):
    n, d = x2.shape
    tm = MOE_TILE
    n_experts = router_w.shape[1]
    per_batch = seq // tm
    rw = jnp.zeros((d, LANES), BF16).at[:, :n_experts].set(router_w.astype(BF16))
    rb = jnp.zeros((1, LANES), F32).at[0, :n_experts].set(router_b)
    lane_out = pl.BlockSpec((tm, LANES), lambda i: (i, 0))
    lane_shape = jax.ShapeDtypeStruct((n, LANES), F32)
    return pl.pallas_call(
        functools.partial(_router_kernel, n_experts=n_experts),
        grid=(n // tm,),
        in_specs=[
            pl.BlockSpec((tm, d), lambda i: (i, 0)),
            pl.BlockSpec((1, d), lambda i: (0, 0)),
            pl.BlockSpec((1, 1, d), lambda i: (i // per_batch, 0, shift_idx)),
            pl.BlockSpec((1, 1, d), lambda i: (i // per_batch, 0, shift_idx + 1)),
            pl.BlockSpec((d, LANES), lambda i: (0, 0)),
            pl.BlockSpec((1, LANES), lambda i: (0, 0)),
        ],
        out_specs=[pl.BlockSpec((tm, d // 2), lambda i: (i, 0)), lane_out, lane_out, lane_out,
                   pl.BlockSpec((1, LANES), lambda i: (0, 0))],
        out_shape=[jax.ShapeDtypeStruct((n, d // 2), jnp.uint32), lane_shape, lane_shape, lane_shape,
                   jax.ShapeDtypeStruct((1, LANES), F32)],
        scratch_shapes=[pltpu.VMEM((1, LANES), F32)],
        compiler_params=_params("arbitrary"),
        name="moe_router",
    )(x2, g.reshape(1, d), mod_l, mod_l, rw, rb)


GLU_TILE = 2 * LANES


def _experts_kernel(te_ref, nu_ref, slot_ref, nxt_ref, xs_ref, b1_ref, b2_ref, perm_ref, w1_hbm, w2_hbm,
                    ys_ref, w1buf, w2buf, w1p_ref, w2b_ref, sem, *, layer):
    t = pl.program_id(0)
    used = t < nu_ref[0]
    new_expert = (t == 0) | (te_ref[t] != te_ref[jnp.maximum(t - 1, 0)])
    n_col_tiles = w1p_ref.shape[1] // GLU_TILE

    def weight_copies(e, slot):
        return (pltpu.make_async_copy(w1_hbm.at[layer, e], w1buf.at[slot], sem.at[0, slot]),
                pltpu.make_async_copy(w2_hbm.at[layer, e], w2buf.at[slot], sem.at[1, slot]))

    @pl.when(used & new_expert)
    def _():
        slot = slot_ref[t]

        @pl.when(t == 0)
        def _():
            for cp in weight_copies(te_ref[t], slot):
                cp.start()

        for cp in weight_copies(te_ref[t], slot):
            cp.wait()

        @pl.when(nxt_ref[t] >= 0)
        def _():
            for cp in weight_copies(nxt_ref[t], 1 - slot):
                cp.start()

        for c in range(n_col_tiles):
            cols = slice(c * GLU_TILE, (c + 1) * GLU_TILE)
            w1p_ref[:, cols] = _dot(w1buf[slot, :, cols].astype(BF16), perm_ref[...]).astype(BF16)
        w2b_ref[...] = w2buf[slot].astype(BF16)

    @pl.when(used)
    def _():
        lo, hi = _unpack_bf16_pair(xs_ref[...])
        x = jnp.concatenate([lo, hi], axis=1)
        hh = _dot(x, w1p_ref[...]) + b1_ref[0]
        acts = []
        for c in range(n_col_tiles):
            x_glu = jnp.minimum(hh[:, c * GLU_TILE:c * GLU_TILE + LANES], SWIGLU_LIMIT)
            x_lin = jnp.clip(hh[:, c * GLU_TILE + LANES:(c + 1) * GLU_TILE], -SWIGLU_LIMIT, SWIGLU_LIMIT)
            acts.append((x_glu * _sigmoid(SWIGLU_ALPHA * x_glu) * (x_lin + 1.0)).astype(BF16))
        out = _dot(jnp.concatenate(acts, axis=1), w2b_ref[...]) + b2_ref[0]
        half = out.shape[1] // 2
        ys_ref[...] = _pack_bf16_pair(out[:, :half], out[:, half:])

    @pl.when(t >= nu_ref[0])
    def _():
        ys_ref[...] = jnp.zeros_like(ys_ref)


def _experts(xs, tile_expert, n_used, slot, nxt, w1_all, b1, w2_all, b2, layer):
    n_rows, dw = xs.shape
    tm = EXPERT_TILE
    _, n_exp, d, de2 = w1_all.shape
    src = np.arange(GLU_TILE)
    dst = np.where(src % 2 == 0, src // 2, LANES + src // 2)
    perm = jnp.asarray(dst[:, None] == np.arange(GLU_TILE)[None, :], BF16)
    b1p = b1.reshape(n_exp, de2 // GLU_TILE, LANES, 2).transpose(0, 1, 3, 2).reshape(n_exp, 1, de2)
    row = lambda t, te, nu, sl, nx: (jnp.maximum(jnp.minimum(t, nu[0] - 1), 0), 0)
    exp3 = lambda t, te, nu, sl, nx: (te[t], 0, 0)
    grid_spec = pltpu.PrefetchScalarGridSpec(
        num_scalar_prefetch=4,
        grid=(n_rows // tm,),
        in_specs=[
            pl.BlockSpec((tm, dw), row),
            pl.BlockSpec((1, 1, de2), exp3),
            pl.BlockSpec((1, 1, d), exp3),
            pl.BlockSpec((GLU_TILE, GLU_TILE), lambda t, te, nu, sl, nx: (0, 0)),
            pl.BlockSpec(memory_space=pl.ANY),
            pl.BlockSpec(memory_space=pl.ANY),
        ],
        out_specs=pl.BlockSpec((tm, d // 2), lambda t, te, nu, sl, nx: (t, 0)),
        scratch_shapes=[pltpu.VMEM((2, d, de2), F32), pltpu.VMEM((2, de2 // 2, d), F32),
                        pltpu.VMEM((d, de2), BF16), pltpu.VMEM((de2 // 2, d), BF16),
                        pltpu.SemaphoreType.DMA((2, 2))],
    )
    return pl.pallas_call(
        functools.partial(_experts_kernel, layer=layer),
        grid_spec=grid_spec,
        out_shape=jax.ShapeDtypeStruct((n_rows, d // 2), jnp.uint32),
        compiler_params=_params("arbitrary"),
        name="moe_experts",
    )(tile_expert, n_used, slot, nxt, xs, b1p, b2.reshape(n_exp, 1, d), perm, w1_all, w2_all)


SC_WINDOW = 128


def _sc_worker_rows(n_rows):
    sc = plsc.get_sparse_core_info()
    per_worker = n_rows // (sc.num_cores * sc.num_subcores)
    worker = lax.axis_index("subcore") * sc.num_cores + lax.axis_index("core")
    return worker * per_worker, per_worker


def _sc_mesh():
    return plsc.VectorSubcoreMesh(core_axis_name="core", subcore_axis_name="subcore")


def _sc_gather_rows(table, idx):
    n_idx, width = idx.shape[0], table.shape[1]

    @functools.partial(
        pl.kernel, out_type=jax.ShapeDtypeStruct((n_idx, width), table.dtype), mesh=_sc_mesh(),
        scratch_types=[pltpu.VMEM((SC_WINDOW,), jnp.int32), pltpu.VMEM((SC_WINDOW, width), table.dtype)],
        name="sc_gather_rows")
    def gather(table_hbm, idx_hbm, out_hbm, idx_v, rows_v):
        first, count = _sc_worker_rows(n_idx)

        @pl.loop(0, count // SC_WINDOW)
        def _(c):
            rows = pl.ds(first + c * SC_WINDOW, SC_WINDOW)
            pltpu.sync_copy(idx_hbm.at[rows], idx_v)
            pltpu.sync_copy(table_hbm.at[idx_v], rows_v)
            pltpu.sync_copy(rows_v, out_hbm.at[rows])

    return gather(table, idx)


def _sc_scatter_rows(rows, pos, n_out):
    n, width = rows.shape
    k = pos.shape[0]

    @functools.partial(
        pl.kernel, out_type=jax.ShapeDtypeStruct((n_out, width), rows.dtype), mesh=_sc_mesh(),
        scratch_types=[pltpu.VMEM((k, SC_WINDOW), jnp.int32), pltpu.VMEM((SC_WINDOW, width), rows.dtype)],
        name="sc_scatter_rows")
    def scatter(rows_hbm, pos_hbm, out_hbm, pos_v, rows_v):
        first, count = _sc_worker_rows(n)

        @pl.loop(0, count // SC_WINDOW)
        def _(c):
            src = pl.ds(first + c * SC_WINDOW, SC_WINDOW)
            pltpu.sync_copy(pos_hbm.at[:, src], pos_v)
            pltpu.sync_copy(rows_hbm.at[src], rows_v)
            for j in range(k):
                pltpu.sync_copy(rows_v, out_hbm.at[pos_v.at[j]])

    return scatter(rows, pos)


def _combine_dense_kernel(x_ref, gate_ref, w_ref, fg_ref, *rest, final_norm):
    y_refs, o_ref = rest[:TOP_K], rest[TOP_K]
    w = w_ref[...]
    y_lo = y_hi = None
    for j in range(TOP_K):
        lo, hi = _unpack_bf16_pair(y_refs[j][...])
        wj = w[:, j:j + 1]
        y_lo = wj * lo if j == 0 else y_lo + wj * lo
        y_hi = wj * hi if j == 0 else y_hi + wj * hi
    out = x_ref[...] + gate_ref[0] * jnp.concatenate([y_lo, y_hi], axis=1)
    if final_norm:
        out = out * lax.rsqrt(jnp.mean(out * out, axis=-1, keepdims=True) + NORM_EPS) * fg_ref[...]
    o_ref[...] = out


def _combine_dense(x2, mod_l, gate_idx, seq, top_w, y4, final_g, final_norm, tm=512):
    n, d = x2.shape
    per_batch = seq // tm
    blocks = n // tm
    slot_spec = lambda j: pl.BlockSpec((tm, d // 2), lambda i: (j * blocks + i, 0))
    return pl.pallas_call(
        functools.partial(_combine_dense_kernel, final_norm=final_norm),
        grid=(blocks,),
        in_specs=[
            pl.BlockSpec((tm, d), lambda i: (i, 0)),
            pl.BlockSpec((1, 1, d), lambda i: (i // per_batch, 0, gate_idx)),
            pl.BlockSpec((tm, LANES), lambda i: (i, 0)),
            pl.BlockSpec((1, d), lambda i: (0, 0)),
        ] + [slot_spec(j) for j in range(TOP_K)],
        out_specs=pl.BlockSpec((tm, d), lambda i: (i, 0)),
        out_shape=jax.ShapeDtypeStruct((n, d), F32),
        compiler_params=_params("parallel"),
        name="moe_combine",
    )(x2, mod_l, top_w, final_g.reshape(1, d), *([y4] * TOP_K))


def _moe_ffn(x2, norm_g, mod_l, seq, router_w, router_b, w1_all, b1, w2_all, b2, layer, final_g, final_norm):
    n, d = x2.shape
    n_exp = router_w.shape[1]
    tile = EXPERT_TILE
    hp, idx_f, top_w, rank_f, counts = _router(x2, norm_g, mod_l, 3, seq, router_w, router_b)
    counts = counts[0, :n_exp].astype(jnp.int32)
    padded = (counts + tile - 1) // tile * tile
    ends = jnp.cumsum(padded)
    starts = ends - padded
    idx = idx_f[:, :TOP_K].astype(jnp.int32)
    expert_ids = jnp.arange(n_exp, dtype=jnp.int32)
    start_of = jnp.sum(jnp.where(idx[..., None] == expert_ids, starts, 0), axis=-1)
    pos_flat = (start_of + rank_f[:, :TOP_K].astype(jnp.int32)).reshape(-1)
    n_rows = n * TOP_K + n_exp * tile
    n_tiles = n_rows // tile
    tile_start = jnp.arange(n_tiles, dtype=jnp.int32) * tile
    tile_expert = jnp.sum((ends[None, :] <= tile_start[:, None]).astype(jnp.int32), axis=1)
    tile_expert = jnp.minimum(tile_expert, n_exp - 1)
    n_used = (ends[-1:] // tile).astype(jnp.int32)
    nonempty = counts > 0
    ordinal = jnp.cumsum(nonempty.astype(jnp.int32)) - 1
    later = nonempty[None, :] & (expert_ids[None, :] > expert_ids[:, None])
    nxt_e = jnp.min(jnp.where(later, expert_ids[None, :], n_exp), axis=1)
    nxt_e = jnp.where(nxt_e == n_exp, -1, nxt_e)
    slot = (ordinal[tile_expert] % 2).astype(jnp.int32)
    nxt = nxt_e[tile_expert].astype(jnp.int32)
    pos_slot_major = pos_flat.reshape(n, TOP_K).T
    xs = _sc_scatter_rows(hp, pos_slot_major, n_rows)
    ys = _experts(xs, tile_expert, n_used, slot, nxt, w1_all, b1, w2_all, b2, layer)
    y4 = _sc_gather_rows(ys, pos_slot_major.reshape(-1))
    return _combine_dense(x2, mod_l, 5, seq, top_w, y4, final_g, final_norm)


def kernel(x, c, ada_w, ada_b, norm_mix_g, norm_ffn_g, router_w, router_b, moe_w1, moe_b1, moe_w2, moe_b2, ab_w_in, ab_w_out, hgrn_lb, hgrn_norm_g, rwkv_mu, rwkv_w0, rwkv_w2, rwkv_a0, rwkv_a2, rwkv_g2, rwkv_k_k, rwkv_k_a, rwkv_r_k, rwkv_ln_g, rwkv_ln_b, attn_w_in, attn_w_out, attn_lambda, attn_subln_g, rel_bias_table, final_norm_g):
    batch, seq, d = x.shape
    n = batch * seq
    depth = ada_w.shape[0]
    x2 = x.reshape(n, d)
    mod = _adaln(c, ada_w, ada_b)
    lb_all = jnp.cumsum(jax.nn.softmax(hgrn_lb.astype(F32), axis=1), axis=1)
    for layer in range(depth):
        mod_l = mod[layer].reshape(batch, 1, 6 * d)
        j = layer // 2
        if layer % 2 == 0:
            a_cols = 5 * (d // 2)
            w_in = ab_w_in[j].astype(BF16)
            p_a, p_b = _normmod_proj(x2, norm_mix_g[layer], mod_l, 0, seq,
                                     [w_in[:, :a_cols], w_in[:, a_cols:]], [F32, F32])
            y_a = _hgrn2(p_a, lb_all[:, j], hgrn_norm_g[j], batch, seq)
            y_b = _rwkv7(p_b, rwkv_mu[j], rwkv_w0[j], rwkv_w2[j], rwkv_a0[j], rwkv_a2[j], rwkv_g2[j],
                         rwkv_k_k[j], rwkv_k_a[j], rwkv_r_k[j], rwkv_ln_g[j], rwkv_ln_b[j], batch, seq)
            w_out = ab_w_out[j].astype(BF16)
            x2 = _outproj_residual(x2, mod_l, 2, seq, [y_a, y_b], [w_out[:d // 2], w_out[d // 2:]])
        else:
            (qkv,) = _normmod_proj(x2, norm_mix_g[layer], mod_l, 0, seq, [attn_w_in[j].astype(BF16)], [BF16])
            lam = attn_lambda[j].astype(F32)
            lam_init = 0.8 - 0.6 * math.exp(-0.3 * layer)
            lam_full = jnp.exp(jnp.sum(lam[0] * lam[1])) - jnp.exp(jnp.sum(lam[2] * lam[3])) + lam_init
            o = _diff_attention(qkv, rel_bias_table, lam_full, attn_subln_g[j], lam_init, batch, seq)
            x2 = _outproj_residual(x2, mod_l, 2, seq, [o], [attn_w_out[j].astype(BF16)])
        x2 = _moe_ffn(x2, norm_ffn_g[layer], mod_l, seq, router_w[layer], router_b[layer], moe_w1,
                      moe_b1[layer], moe_w2, moe_b2[layer], layer, final_norm_g, layer == depth - 1)
    return x2.reshape(batch, seq, d)
```

```python
import functools
import math

import jax
import jax.numpy as jnp
import numpy as np
from jax import lax
from jax.experimental import pallas as pl
from jax.experimental.pallas import tpu as pltpu
from jax.experimental.pallas import tpu_sc as plsc

F32 = jnp.float32
BF16 = jnp.bfloat16
HIGHEST = lax.Precision.HIGHEST

NORM_EPS = 1e-6

A_HEAD_DIM = 128
A_CHUNK = 32
A_GROUP = 128
A_HEADS_PER_STEP = 2
B_HEAD_DIM = 64
B_CHUNK = 64
B_GROUP = 8
B_LN_EPS = 1e-5 * B_HEAD_DIM
C_HEAD_DIM = 64
REL_BUCKETS = 32
REL_MAX_DISTANCE = 128
TOP_K = 4
SWIGLU_LIMIT = 7.0
SWIGLU_ALPHA = 1.702

V7X_VMEM_BYTES = 64 * 1024 * 1024
VMEM_LIMIT = V7X_VMEM_BYTES - 8 * 1024 * 1024
LANES = 128


def _params(*sem):
    return pltpu.CompilerParams(dimension_semantics=sem, vmem_limit_bytes=VMEM_LIMIT)


def _sigmoid(x):
    return 1.0 / (1.0 + jnp.exp(-x))


def _silu(x):
    return x * _sigmoid(x)


def _dot(a, b):
    return jnp.dot(a, b, preferred_element_type=F32)


def _dot_nt(a, b):
    return lax.dot_general(a, b, (((1,), (1,)), ((), ())), preferred_element_type=F32)


def _dot_tn(a, b):
    return lax.dot_general(a, b, (((0,), (0,)), ((), ())), preferred_element_type=F32)


def _adaln_kernel(c_ref, w_ref, b_ref, o_ref):
    cond = _silu(c_ref[...])
    o_ref[0] = jnp.dot(cond, w_ref[0], precision=HIGHEST, preferred_element_type=F32) + b_ref[0]


def _adaln(c, ada_w, ada_b):
    n_layers, d, n_out = ada_w.shape
    batch = c.shape[0]
    tn = 1536
    return pl.pallas_call(
        _adaln_kernel,
        grid=(n_layers, n_out // tn),
        in_specs=[
            pl.BlockSpec((batch, d), lambda l, j: (0, 0)),
            pl.BlockSpec((1, d, tn), lambda l, j: (l, 0, j)),
            pl.BlockSpec((1, 1, tn), lambda l, j: (l, 0, j)),
        ],
        out_specs=pl.BlockSpec((1, batch, tn), lambda l, j: (l, 0, j)),
        out_shape=jax.ShapeDtypeStruct((n_layers, batch, n_out), F32),
        compiler_params=_params("parallel", "parallel"),
        name="adaln",
    )(c, ada_w, ada_b.reshape(n_layers, 1, n_out))


def _modulated_norm(x, g, shift, scale):
    y = x * lax.rsqrt(jnp.mean(x * x, axis=-1, keepdims=True) + NORM_EPS)
    return (y * g) * (1.0 + scale) + shift


def _normmod_proj_kernel(x_ref, g_ref, sh_ref, sc_ref, *rest, n_w):
    w_refs, o_refs = rest[:n_w], rest[n_w:]
    h = _modulated_norm(x_ref[...], g_ref[...], sh_ref[0], sc_ref[0]).astype(BF16)
    for w_ref, o_ref in zip(w_refs, o_refs):
        o_ref[...] = _dot(h, w_ref[...]).astype(o_ref.dtype)


def _normmod_proj(x2, g, mod_l, shift_idx, seq, weights, out_dtypes, tm=512):
    n, d = x2.shape
    per_batch = seq // tm
    in_specs = [
        pl.BlockSpec((tm, d), lambda i: (i, 0)),
        pl.BlockSpec((1, d), lambda i: (0, 0)),
        pl.BlockSpec((1, 1, d), lambda i: (i // per_batch, 0, shift_idx)),
        pl.BlockSpec((1, 1, d), lambda i: (i // per_batch, 0, shift_idx + 1)),
    ]
    in_specs += [pl.BlockSpec(w.shape, lambda i: (0, 0)) for w in weights]
    out_specs = [pl.BlockSpec((tm, w.shape[1]), lambda i: (i, 0)) for w in weights]
    out_shape = [jax.ShapeDtypeStruct((n, w.shape[1]), dt) for w, dt in zip(weights, out_dtypes)]
    return pl.pallas_call(
        functools.partial(_normmod_proj_kernel, n_w=len(weights)),
        grid=(n // tm,),
        in_specs=in_specs,
        out_specs=out_specs,
        out_shape=out_shape,
        compiler_params=_params("parallel"),
        name="normmod_proj",
    )(x2, g.reshape(1, d), mod_l, mod_l, *weights)


def _outproj_kernel(x_ref, gate_ref, *rest, n_y):
    y_refs, w_refs, o_ref = rest[:n_y], rest[n_y:2 * n_y], rest[2 * n_y]
    acc = _dot(y_refs[0][...], w_refs[0][...])
    for y_ref, w_ref in zip(y_refs[1:], w_refs[1:]):
        acc += _dot(y_ref[...], w_ref[...])
    o_ref[...] = x_ref[...] + gate_ref[0] * acc


def _outproj_residual(x2, mod_l, gate_idx, seq, ys, ws, tm=512):
    n, d = x2.shape
    per_batch = seq // tm
    in_specs = [
        pl.BlockSpec((tm, d), lambda i: (i, 0)),
        pl.BlockSpec((1, 1, d), lambda i: (i // per_batch, 0, gate_idx)),
    ]
    in_specs += [pl.BlockSpec((tm, y.shape[1]), lambda i: (i, 0)) for y in ys]
    in_specs += [pl.BlockSpec(w.shape, lambda i: (0, 0)) for w in ws]
    return pl.pallas_call(
        functools.partial(_outproj_kernel, n_y=len(ys)),
        grid=(n // tm,),
        in_specs=in_specs,
        out_specs=pl.BlockSpec((tm, d), lambda i: (i, 0)),
        out_shape=jax.ShapeDtypeStruct((n, d), F32),
        compiler_params=_params("parallel"),
        name="outproj_residual",
    )(x2, mod_l, *ys, *ws)


def _chunk_cumsum(x, chunk, reverse):
    rows = x.shape[0]
    pos = lax.broadcasted_iota(jnp.int32, x.shape, 0) % chunk
    s = 1
    while s < chunk:
        if reverse:
            x = x + jnp.where(pos < chunk - s, pltpu.roll(x, rows - s, axis=0), 0.0)
        else:
            x = x + jnp.where(pos >= s, pltpu.roll(x, s, axis=0), 0.0)
        s *= 2
    return x


def _hgrn_groups(slabs):
    g_rows, dk = slabs[0][0].shape
    n_chunks = g_rows // A_CHUNK
    ti = lax.broadcasted_iota(jnp.int32, (A_CHUNK, A_CHUNK), 0)
    si = lax.broadcasted_iota(jnp.int32, (A_CHUNK, A_CHUNK), 1)
    chunk_rows = [slice(c * A_CHUNK, (c + 1) * A_CHUNK) for c in range(n_chunks)]
    prep = []
    for q, f, v, lb, st, reverse in slabs:
        fg = lb + (1.0 - lb) * _sigmoid(f)
        k = 1.0 - fg
        b = _chunk_cumsum(jnp.log(fg), A_CHUNK, reverse)
        b3 = b.reshape(n_chunks, A_CHUNK, dk)
        edge = b3[:, 0:1, :] if reverse else b3[:, A_CHUNK - 1:A_CHUNK, :]
        prep.append(dict(
            q_in=(q * jnp.exp(b)).astype(BF16), k_in=(k * jnp.exp(-b)).astype(BF16),
            k_st=(k.reshape(n_chunks, A_CHUNK, dk) * jnp.exp(edge - b3)).astype(BF16),
            decay=jnp.exp(edge), vb=v.astype(BF16), st=st, reverse=reverse,
            mask=(si >= ti) if reverse else (si <= ti)))
    for p in prep:
        p["scores"] = [_dot_nt(p["q_in"][r], p["k_in"][r]) for r in chunk_rows]
        p["dstate"] = [_dot_tn(p["vb"][r], p["k_st"][c]) for c, r in enumerate(chunk_rows)]
    for p in prep:
        p["intra"] = [_dot(jnp.where(p["mask"], s, 0.0).astype(BF16), p["vb"][r])
                      for s, r in zip(p["scores"], chunk_rows)]
    results = []
    for p in prep:
        st = p["st"]
        outs = [None] * n_chunks
        for c in (range(n_chunks - 1, -1, -1) if p["reverse"] else range(n_chunks)):
            outs[c] = p["intra"][c] + _dot_nt(p["q_in"][chunk_rows[c]], st.astype(BF16))
            st = st * p["decay"][c] + p["dstate"][c]
        results.append((jnp.concatenate(outs, axis=0), st))
    return results


def _hgrn_kernel(q_ref, ff_ref, fb_ref, i_ref, g_ref, lb_ref, ng_ref, o_ref, of_ref, ob_ref):
    seq = q_ref.shape[0]
    dk = A_HEAD_DIM
    n_heads = q_ref.shape[1] // dk
    n_groups = seq // A_GROUP
    head_cols = [slice(h * dk, (h + 1) * dk) for h in range(n_heads)]

    def body(j, states):
        rf = pl.ds(pl.multiple_of(j * A_GROUP, A_GROUP), A_GROUP)
        rb = pl.ds(pl.multiple_of((n_groups - 1 - j) * A_GROUP, A_GROUP), A_GROUP)
        slabs = []
        for h, cols in enumerate(head_cols):
            slabs.append((_silu(q_ref[rf, cols]), ff_ref[rf, cols], i_ref[rf, cols], lb_ref[0:1, cols],
                          states[2 * h], False))
            slabs.append((_silu(q_ref[rb, cols]), fb_ref[rb, cols], i_ref[rb, cols], lb_ref[1:2, cols],
                          states[2 * h + 1], True))
        results = _hgrn_groups(slabs)
        for h, cols in enumerate(head_cols):
            of_ref[rf, cols] = results[2 * h][0]
            ob_ref[rb, cols] = results[2 * h + 1][0]
        return tuple(r[1] for r in results)

    zero = jnp.zeros((dk, dk), F32)
    lax.fori_loop(0, n_groups, body, (zero,) * (2 * n_heads))

    def finish(j, carry):
        r = pl.ds(pl.multiple_of(j * A_GROUP, A_GROUP), A_GROUP)
        for cols in head_cols:
            o = of_ref[r, cols] + ob_ref[r, cols]
            o = o * lax.rsqrt(jnp.mean(o * o, axis=-1, keepdims=True) + NORM_EPS) * ng_ref[...]
            o_ref[r, cols] = (o * _silu(g_ref[r, cols])).astype(o_ref.dtype)
        return carry

    lax.fori_loop(0, n_groups, finish, 0)


def _hgrn2(p_a, lb, norm_g, batch, seq):
    n = p_a.shape[0]
    width = p_a.shape[1] // 5
    cols = A_HEADS_PER_STEP * A_HEAD_DIM
    steps = width // cols
    sect = lambda s: pl.BlockSpec((seq, cols), lambda b, h, s=s: (b, s * steps + h))
    return pl.pallas_call(
        _hgrn_kernel,
        grid=(batch, steps),
        in_specs=[sect(0), sect(1), sect(2), sect(3), sect(4),
                  pl.BlockSpec((2, cols), lambda b, h: (0, h)),
                  pl.BlockSpec((1, A_HEAD_DIM), lambda b, h: (0, 0))],
        out_specs=pl.BlockSpec((seq, cols), lambda b, h: (b, h)),
        out_shape=jax.ShapeDtypeStruct((n, width), BF16),
        scratch_shapes=[pltpu.VMEM((seq, cols), F32), pltpu.VMEM((seq, cols), F32)],
        compiler_params=_params("parallel", "parallel"),
        name="hgrn2",
    )(p_a, p_a, p_a, p_a, p_a, lb, norm_g.reshape(1, A_HEAD_DIM))


def _softplus(z):
    return jnp.maximum(z, 0.0) + jnp.log(1.0 + jnp.exp(-jnp.abs(z)))


def _rwkv_prep_kernel(p_ref, prev_ref, next_ref, mu_ref, w0_ref, w2_ref, a0_ref, a2_ref, g2_ref,
                      kk_ref, ka_ref, rk_ref, ones_ref, tri_ref,
                      r_out, k_out, v_out, kkn_out, b_out, cumf_out, cumb_out, g_out, bonus_out):
    i = pl.program_id(1)
    last = pl.num_programs(1) - 1
    ts = p_ref.shape[0]
    width = a0_ref.shape[1]
    heads = width // B_HEAD_DIM
    p = p_ref[...]
    prev_row = jnp.where(i == 0, 0.0, prev_ref[7:8, :])
    next_row = jnp.where(i == last, 0.0, next_ref[0:1, :])
    row = lax.broadcasted_iota(jnp.int32, (ts, 1), 0)
    up = jnp.where(row == 0, prev_row, pltpu.roll(p, 1, axis=0))
    dn = jnp.where(row == ts - 1, next_row, pltpu.roll(p, ts - 1, axis=0))
    p = p + mu_ref[...] * (0.5 * (up + dn) - p)

    r = p[:, 0:width]
    k = p[:, width:2 * width]
    v = p[:, 2 * width:3 * width]
    o = 3 * width
    wlo_f = p[:, o:o + 64]
    wlo_b = p[:, o + 64:o + 128]
    alo = p[:, o + 128:o + 192]
    glo = p[:, o + 192:o + 320]

    def log_decay(wlo, d):
        z = w0_ref[d:d + 1, :] + _dot(jnp.tanh(wlo).astype(BF16), w2_ref[d])
        w = -_softplus(-z) - 0.5
        return -jnp.exp(w)

    def split_dot(sel, t, right):
        hi = t.astype(BF16)
        lo = (t - hi.astype(F32)).astype(BF16)
        return (_dot(hi, sel) + _dot(lo, sel)) if right else (_dot(sel, hi) + _dot(sel, lo))

    cum_f = split_dot(tri_ref[0], log_decay(wlo_f, 0), False)
    cum_b = split_dot(tri_ref[1], log_decay(wlo_b, 1), False)
    a = _sigmoid(a0_ref[...] + _dot(alo.astype(BF16), a2_ref[...]))
    g = _dot(_sigmoid(glo).astype(BF16), g2_ref[...])
    kk = k * kk_ref[...]
    head_sum = lambda t: split_dot(ones_ref[...], t, True)

    kk_n = kk / jnp.maximum(jnp.sqrt(head_sum(kk * kk)), 1e-12)
    k_mod = k * (1.0 + (a - 1.0) * ka_ref[...])
    bonus = head_sum(r * k_mod * rk_ref[...]) * v
    g_out[...] = g
    bonus_out[...] = bonus
    for h in range(heads):
        sl = slice(h * B_HEAD_DIM, (h + 1) * B_HEAD_DIM)
        r_out[0, h] = r[:, sl]
        k_out[0, h] = k_mod[:, sl]
        v_out[0, h] = v[:, sl]
        kkn_out[0, h] = kk_n[:, sl]
        b_out[0, h] = (kk_n * a)[:, sl]
        cumf_out[0, h] = cum_f[:, sl]
        cumb_out[0, h] = cum_b[:, sl]


def _rwkv_prep(p_b, mu, w0, w2, a0, a2, g2, k_k, k_a, r_k, batch, seq, ts=256):
    n, cols = p_b.shape
    width = a0.shape[0]
    heads = width // B_HEAD_DIM
    nblk = seq // ts
    rows8 = ts // 8
    head_id = np.arange(width) // B_HEAD_DIM
    ones = jnp.asarray(head_id[:, None] == head_id[None, :], BF16)
    full = lambda a: pl.BlockSpec(a.shape, lambda b, i: (0,) * a.ndim)
    vec = lambda a: a.reshape(1, -1)
    row = np.arange(ts)
    same_chunk = (row[:, None] // B_CHUNK) == (row[None, :] // B_CHUNK)
    tri = jnp.asarray(np.stack([same_chunk & (row[None, :] <= row[:, None]),
                                same_chunk & (row[None, :] >= row[:, None])]), BF16)
    args = [vec(mu), w0, w2.astype(BF16), vec(a0), a2.astype(BF16), g2.astype(BF16),
            vec(k_k), vec(k_a), vec(r_k), ones, tri]
    hm = pl.BlockSpec((1, heads, ts, B_HEAD_DIM), lambda b, i: (b, 0, i, 0))
    tokm = pl.BlockSpec((ts, width), lambda b, i: (b * nblk + i, 0))
    hm_shape = jax.ShapeDtypeStruct((batch, heads, seq, B_HEAD_DIM), F32)
    tok_shape = jax.ShapeDtypeStruct((n, width), F32)
    return pl.pallas_call(
        _rwkv_prep_kernel,
        grid=(batch, nblk),
        in_specs=[
            pl.BlockSpec((ts, cols), lambda b, i: (b * nblk + i, 0)),
            pl.BlockSpec((8, cols), lambda b, i: (jnp.maximum((b * nblk + i) * rows8 - 1, 0), 0)),
            pl.BlockSpec((8, cols), lambda b, i: (jnp.minimum((b * nblk + i + 1) * rows8, n // 8 - 1), 0)),
        ] + [full(a) for a in args],
        out_specs=[hm] * 7 + [tokm] * 2,
        out_shape=[hm_shape] * 7 + [tok_shape] * 2,
        compiler_params=_params("parallel", "parallel"),
        name="rwkv_prep",
    )(p_b, p_b, p_b, *args)


def _rwkv_chunks(chains):
    c, d = chains[0][0].shape
    ti = lax.broadcasted_iota(jnp.int32, (c, c), 0)
    si = lax.broadcasted_iota(jnp.int32, (c, c), 1)
    eye = jnp.where(ti == si, 1.0, 0.0)
    masks = {rev: (((si > ti), (si >= ti)) if rev else ((si < ti), (si <= ti))) for rev in (False, True)}

    row = lax.broadcasted_iota(jnp.int32, (c, d), 0)
    st = []
    for r, k, v, kk, b, cum, tt, rev in chains:
        before = (jnp.where(row == c - 1, 0.0, pltpu.roll(cum, c - 1, axis=0)) if rev
                  else jnp.where(row == 0, 0.0, pltpu.roll(cum, 1, axis=0)))
        grow = jnp.exp(-cum)
        kt = kk * jnp.exp(before)
        rt = r * jnp.exp(cum)
        b_ = (b * grow).astype(BF16)
        kb = jnp.concatenate([(k * grow).astype(BF16), b_], axis=0)
        st.append(dict(v=v, vb=v.astype(BF16), kt=kt, rt=rt, b_=b_, kb=kb, tt=tt, ttb=tt.astype(BF16),
                       decay=jnp.exp(cum[0:1, :] if rev else cum[c - 1:c, :]), rev=rev,
                       lhs=jnp.concatenate([kt, rt], axis=0).astype(BF16)))
    for s in st:
        s["big"] = _dot_nt(s["lhs"], s["kb"])
    for s in st:
        strict, incl = masks[s["rev"]]
        big = s["big"]
        a_ab = jnp.where(strict, big[:c, c:], 0.0)
        s["a_ak"] = jnp.where(strict, big[:c, :c], 0.0).astype(BF16)
        s["p"] = jnp.concatenate([jnp.where(incl, big[c:, :c], 0.0), jnp.where(incl, -big[c:, c:], 0.0)],
                                 axis=1).astype(BF16)
        s["m"] = eye - a_ab
        s["ab"] = a_ab.astype(BF16)
    for s in st:
        s["aj"] = _dot(s["ab"], s["ab"])
        s["av"] = _dot(s["a_ak"], s["vb"])
    span = 2
    while span < c:
        span *= 2
        for s in st:
            ajb = s["aj"].astype(BF16)
            if span < c:
                both = _dot(jnp.concatenate([s["aj"], s["m"]], axis=0).astype(BF16), ajb)
                s["aj"], s["m"] = both[:c], s["m"] + both[c:]
            else:
                s["m"] = s["m"] + _dot(s["m"].astype(BF16), ajb)
    for s in st:
        s["wu"] = _dot(s["m"].astype(BF16), jnp.concatenate([s["kt"], s["av"]], axis=1).astype(BF16))
    for s in st:
        w_, u0, v = s["wu"][:, :d], s["wu"][:, d:], s["v"]
        rhs = jnp.concatenate([jnp.concatenate([v, jnp.zeros_like(v)], axis=1),
                               jnp.concatenate([u0, w_], axis=1)], axis=0).astype(BF16)
        s["yq"] = _dot(s["p"], rhs)
        s["h_t"] = _dot_tn(jnp.concatenate([v, -u0], axis=0).astype(BF16), s["kb"])
        s["bw"] = _dot_tn(s["b_"], w_.astype(BF16))
    out = []
    for s in st:
        q = s["rt"] + s["yq"][:, d:]
        y = _dot_nt(q.astype(BF16), s["ttb"]) + s["yq"][:, :d]
        tt = (s["tt"] - _dot_nt(s["ttb"], s["bw"].astype(BF16)) + s["h_t"]) * s["decay"]
        out.append((y, tt))
    return out


def _rwkv_scan_kernel(rf, kf, vf, kkf, bf, lwf, rb, kb, vb, kkb, bb, lwb, yf_ref, yb_ref, tf_ref, tb_ref):
    heads, ts = rf.shape[1], rf.shape[2]
    n_chunks = ts // B_CHUNK

    @pl.when(pl.program_id(1) == 0)
    def _():
        tf_ref[...] = jnp.zeros_like(tf_ref)
        tb_ref[...] = jnp.zeros_like(tb_ref)

    def body(j, carry):
        rows_f = pl.ds(pl.multiple_of(j * B_CHUNK, B_CHUNK), B_CHUNK)
        rows_b = pl.ds(pl.multiple_of((n_chunks - 1 - j) * B_CHUNK, B_CHUNK), B_CHUNK)
        for h0 in range(0, heads, B_GROUP):
            group = range(h0, h0 + B_GROUP)
            chains = []
            for h in group:
                chains.append([ref[0, h, rows_f, :] for ref in (rf, kf, vf, kkf, bf, lwf)] + [tf_ref[h], False])
                chains.append([ref[0, h, rows_b, :] for ref in (rb, kb, vb, kkb, bb, lwb)] + [tb_ref[h], True])
            res = _rwkv_chunks(chains)
            for n, h in enumerate(group):
                yf_ref[0, h, rows_f, :], tf_ref[h] = res[2 * n]
                yb_ref[0, h, rows_b, :], tb_ref[h] = res[2 * n + 1]
        return carry

    lax.fori_loop(0, n_chunks, body, 0)


def _rwkv_scan(r, k, v, kkn, b, lw_f, lw_b, ts=256):
    batch, heads, seq, d = r.shape
    nblk = seq // ts
    fwd = pl.BlockSpec((1, heads, ts, d), lambda bi, i: (bi, 0, i, 0))
    bwd = pl.BlockSpec((1, heads, ts, d), lambda bi, i: (bi, 0, nblk - 1 - i, 0))
    shape = jax.ShapeDtypeStruct(r.shape, F32)
    return pl.pallas_call(
        _rwkv_scan_kernel,
        grid=(batch, nblk),
        in_specs=[fwd] * 6 + [bwd] * 6,
        out_specs=[fwd, bwd],
        out_shape=[shape, shape],
        scratch_shapes=[pltpu.VMEM((heads, d, d), F32), pltpu.VMEM((heads, d, d), F32)],
        compiler_params=_params("parallel", "arbitrary"),
        name="rwkv_scan",
    )(r, k, v, kkn, b, lw_f, r, k, v, kkn, b, lw_b)


def _rwkv_post_kernel(yf_ref, yb_ref, g_ref, bonus_ref, lng_ref, lnb_ref, o_ref):
    heads = yf_ref.shape[1]
    outs = []
    for h in range(heads):
        y = yf_ref[0, h] + yb_ref[0, h]
        mean = jnp.mean(y, axis=-1, keepdims=True)
        var = jnp.mean(jnp.square(y - mean), axis=-1, keepdims=True)
        outs.append((y - mean) * lax.rsqrt(var + B_LN_EPS))
    yn = jnp.concatenate(outs, axis=1) * lng_ref[...] + lnb_ref[...]
    o_ref[...] = ((yn + bonus_ref[...]) * g_ref[...]).astype(o_ref.dtype)


def _rwkv_post(y_f, y_b, g, bonus, ln_g, ln_b, ts=256):
    batch, heads, seq, d = y_f.shape
    n, width = g.shape
    nblk = seq // ts
    hm = pl.BlockSpec((1, heads, ts, d), lambda b, i: (b, 0, i, 0))
    tokm = pl.BlockSpec((ts, width), lambda b, i: (b * nblk + i, 0))
    vec = pl.BlockSpec((1, width), lambda b, i: (0, 0))
    return pl.pallas_call(
        _rwkv_post_kernel,
        grid=(batch, nblk),
        in_specs=[hm, hm, tokm, tokm, vec, vec],
        out_specs=tokm,
        out_shape=jax.ShapeDtypeStruct((n, width), BF16),
        compiler_params=_params("parallel", "parallel"),
        name="rwkv_post",
    )(y_f, y_b, g, bonus, ln_g.reshape(1, width), ln_b.reshape(1, width))


def _rwkv7(p_b, mu, w0, w2, a0, a2, g2, k_k, k_a, r_k, ln_g, ln_b, batch, seq):
    r, k, v, kkn, b, lw_f, lw_b, g, bonus = _rwkv_prep(p_b, mu, w0, w2, a0, a2, g2, k_k, k_a, r_k, batch, seq)
    y_f, y_b = _rwkv_scan(r, k, v, kkn, b, lw_f, lw_b)
    return _rwkv_post(y_f, y_b, g, bonus, ln_g, ln_b)


_BUCKET_EDGES = (0, 1, 2, 3, 4, 5, 6, 7, 8, 12, 16, 23, 32, 46, 64, 91)
ATTN_TK = 128


def _bias_band_kernel(table_ref, o_ref):
    h = pl.program_id(0)
    n_d, tq, tk = o_ref.shape[1:]
    half = REL_BUCKETS // 2
    r = lax.broadcasted_iota(jnp.int32, (tq, tk), 0)
    c = lax.broadcasted_iota(jnp.int32, (tq, tk), 1)
    for d in range(n_d):
        rel = tk * (d - 1) + c - r
        n = jnp.abs(rel)
        vals = []
        for sign in range(2):
            val = jnp.full((tq, tk), table_ref[sign * half + half - 1, h], F32)
            for bkt in range(half - 2, -1, -1):
                val = jnp.where(n < _BUCKET_EDGES[bkt + 1], table_ref[sign * half + bkt, h], val)
            vals.append(val)
        o_ref[0, d] = jnp.where(rel > 0, vals[1], vals[0])


def _bias_band(rel_table, tq):
    heads = rel_table.shape[1]
    n_d = tq // ATTN_TK + 2
    return pl.pallas_call(
        _bias_band_kernel,
        grid=(heads,),
        in_specs=[pl.BlockSpec(memory_space=pltpu.SMEM)],
        out_specs=pl.BlockSpec((1, n_d, tq, ATTN_TK), lambda h: (h, 0, 0, 0)),
        out_shape=jax.ShapeDtypeStruct((heads, n_d, tq, ATTN_TK), F32),
        compiler_params=_params("parallel"),
        name="bias_band",
    )(rel_table)


def _diff_attn_kernel(table_ref, lam_ref, q_ref, k_ref, v_ref, band_ref, g_ref, o_ref, lg_ref, vext_ref, *,
                      lam_init):
    h = pl.program_id(1)
    i = pl.program_id(2)
    tq, dh2 = q_ref.shape
    seq = k_ref.shape[0]
    n_d = band_ref.shape[1]
    n_kt = seq // ATTN_TK
    ratio = tq // ATTN_TK
    half = REL_BUCKETS // 2
    k = k_ref[...]

    @pl.when(i == 0)
    def _():
        v = v_ref[...]
        vext_ref[...] = jnp.concatenate([v, jnp.ones_like(v)], axis=1)

    lane = lax.broadcasted_iota(jnp.int32, (1, dh2), 1)
    q = q_ref[...] * jnp.asarray(C_HEAD_DIM ** -0.5, q_ref.dtype)
    zero = jnp.zeros_like(q)
    first_band = ratio * i - 1
    col = lax.broadcasted_iota(jnp.int32, (1, seq), 1)
    far_row = jnp.where(col < first_band * ATTN_TK, table_ref[half - 1, h],
                        jnp.where(col >= (first_band + n_d) * ATTN_TK, table_ref[REL_BUCKETS - 1, h],
                                  -jnp.inf))
    row_max = []
    for m in range(2):
        qm = jnp.where((lane >= m * C_HEAD_DIM) & (lane < (m + 1) * C_HEAD_DIM), q, zero)
        far = _dot_nt(qm, k) + far_row
        mx = jnp.max(far, axis=-1, keepdims=True)
        lg_ref[m] = far
        for d in (0, n_d - 1) + tuple(range(1, n_d - 1)):
            kt = first_band + d
            valid = (kt >= 0) & (kt < n_kt)
            c0 = pl.multiple_of(jnp.clip(kt, 0, n_kt - 1) * ATTN_TK, ATTN_TK)
            near = _dot_nt(qm, k_ref[pl.ds(c0, ATTN_TK), :]) + band_ref[0, d]
            mx = jnp.maximum(mx, jnp.where(valid, jnp.max(near, axis=-1, keepdims=True), -jnp.inf))
            lg_ref[m, :, pl.ds(c0, ATTN_TK)] = near
        row_max.append(mx)
    parts = []
    for m in range(2):
        e = jnp.exp(lg_ref[m] - row_max[m]).astype(BF16)
        pv = _dot(e, vext_ref[...])
        parts.append(pv[:, :dh2] * (1.0 / pv[:, dh2:dh2 + 1]))
    o = parts[0] - lam_ref[0] * parts[1]
    o = o * lax.rsqrt(jnp.mean(o * o, axis=-1, keepdims=True) + NORM_EPS) * g_ref[...]
    o_ref[...] = (o * (1.0 - lam_init)).astype(o_ref.dtype)


def _diff_attention(qkv, rel_table, lam_full, subln_g, lam_init, batch, seq, tq=512):
    n = qkv.shape[0]
    dh2 = 2 * C_HEAD_DIM
    heads = qkv.shape[1] // (3 * dh2)
    nq = seq // tq
    band = _bias_band(rel_table, tq)
    return pl.pallas_call(
        functools.partial(_diff_attn_kernel, lam_init=lam_init),
        grid=(batch, heads, nq),
        in_specs=[
            pl.BlockSpec(memory_space=pltpu.SMEM),
            pl.BlockSpec(memory_space=pltpu.SMEM),
            pl.BlockSpec((tq, dh2), lambda b, h, i: (b * nq + i, h)),
            pl.BlockSpec((seq, dh2), lambda b, h, i: (b, heads + h)),
            pl.BlockSpec((seq, dh2), lambda b, h, i: (b, 2 * heads + h)),
            pl.BlockSpec((1,) + band.shape[1:], lambda b, h, i: (h, 0, 0, 0)),
            pl.BlockSpec((1, dh2), lambda b, h, i: (0, 0)),
        ],
        out_specs=pl.BlockSpec((tq, dh2), lambda b, h, i: (b * nq + i, h)),
        out_shape=jax.ShapeDtypeStruct((n, heads * dh2), BF16),
        scratch_shapes=[pltpu.VMEM((2, tq, seq), F32), pltpu.VMEM((seq, 2 * dh2), BF16)],
        compiler_params=_params("parallel", "parallel", "arbitrary"),
        name="diff_attention",
    )(rel_table, lam_full.reshape(1), qkv, qkv, qkv, band, subln_g.reshape(1, dh2))


MOE_TILE = 256
EXPERT_TILE = 256
EXPERT_CHUNK = 512
_HI_MASK = np.uint32(0xFFFF0000)


def _pack_bf16_pair(lo, hi):
    as_bits = lambda t: lax.bitcast_convert_type(t.astype(BF16).astype(F32), jnp.uint32)
    return (as_bits(hi) & _HI_MASK) | (as_bits(lo) >> 16)


def _unpack_bf16_pair(word):
    lo = lax.bitcast_convert_type(word << 16, F32).astype(BF16)
    hi = lax.bitcast_convert_type(word & _HI_MASK, F32).astype(BF16)
    return lo, hi


def _router_kernel(x_ref, g_ref, sh_ref, sc_ref, rw_ref, rb_ref, hp_ref, idx_ref, w_ref, rank_ref, cnt_ref,
                   run_ref, *, n_experts):
    @pl.when(pl.program_id(0) == 0)
    def _():
        run_ref[...] = jnp.zeros_like(run_ref)

    tm, d = x_ref.shape
    h = _modulated_norm(x_ref[...], g_ref[...], sh_ref[0], sc_ref[0])
    hp_ref[...] = _pack_bf16_pair(h[:, :d // 2], h[:, d // 2:])
    logits = _dot(h.astype(BF16), rw_ref[...]) + rb_ref[...]
    lane = lax.broadcasted_iota(jnp.int32, logits.shape, 1).astype(F32)
    neg = jnp.float32(-jnp.inf)
    logits = jnp.where(lane < n_experts, logits, neg)
    picks, vals = [], []
    for _ in range(TOP_K):
        m = jnp.max(logits, axis=-1, keepdims=True)
        first = jnp.min(jnp.where(logits == m, lane, float(LANES)), axis=-1, keepdims=True)
        hit = lane == first
        picks.append((first, hit))
        vals.append(m)
        logits = jnp.where(hit, neg, logits)
    es = [jnp.exp(v - vals[0]) for v in vals]
    inv = 1.0 / (es[0] + es[1] + es[2] + es[3])
    assigned = picks[0][1] | picks[1][1] | picks[2][1] | picks[3][1]
    onehot = jnp.where(assigned, 1.0, 0.0)
    ri = lax.broadcasted_iota(jnp.int32, (tm, tm), 0)
    ci = lax.broadcasted_iota(jnp.int32, (tm, tm), 1)
    earlier = jnp.where(ci < ri, 1.0, 0.0).astype(BF16)
    before = _dot(earlier, onehot.astype(BF16)) + run_ref[...]
    idx_o = jnp.zeros(logits.shape, F32)
    w_o = jnp.zeros(logits.shape, F32)
    rank_o = jnp.zeros(logits.shape, F32)
    for j, (first, hit) in enumerate(picks):
        rank_j = jnp.sum(jnp.where(hit, before, 0.0), axis=-1, keepdims=True)
        idx_o = jnp.where(lane == j, first, idx_o)
        w_o = jnp.where(lane == j, es[j] * inv, w_o)
        rank_o = jnp.where(lane == j, rank_j, rank_o)
    idx_ref[...] = idx_o
    w_ref[...] = w_o
    rank_ref[...] = rank_o
    run_ref[...] += jnp.sum(onehot, axis=0, keepdims=True)
    cnt_ref[...] = run_ref[...]


def _router(x2, g, mod_l, shift_idx, seq, router_w, router_b):
    n, d = x2.shape
    tm = MOE_TILE
    n_experts = router_w.shape[1]
    per_batch = seq // tm
    rw = jnp.zeros((d, LANES), BF16).at[:, :n_experts].set(router_w.astype(BF16))
    rb = jnp.zeros((1, LANES), F32).at[0, :n_experts].set(router_b)
    lane_out = pl.BlockSpec((tm, LANES), lambda i: (i, 0))
    lane_shape = jax.ShapeDtypeStruct((n, LANES), F32)
    return pl.pallas_call(
        functools.partial(_router_kernel, n_experts=n_experts),
        grid=(n // tm,),
        in_specs=[
            pl.BlockSpec((tm, d), lambda i: (i, 0)),
            pl.BlockSpec((1, d), lambda i: (0, 0)),
            pl.BlockSpec((1, 1, d), lambda i: (i // per_batch, 0, shift_idx)),
            pl.BlockSpec((1, 1, d), lambda i: (i // per_batch, 0, shift_idx + 1)),
            pl.BlockSpec((d, LANES), lambda i: (0, 0)),
            pl.BlockSpec((1, LANES), lambda i: (0, 0)),
        ],
        out_specs=[pl.BlockSpec((tm, d // 2), lambda i: (i, 0)), lane_out, lane_out, lane_out,
                   pl.BlockSpec((1, LANES), lambda i: (0, 0))],
        out_shape=[jax.ShapeDtypeStruct((n, d // 2), jnp.uint32), lane_shape, lane_shape, lane_shape,
                   jax.ShapeDtypeStruct((1, LANES), F32)],
        scratch_shapes=[pltpu.VMEM((1, LANES), F32)],
        compiler_params=_params("arbitrary"),
        name="moe_router",
    )(x2, g.reshape(1, d), mod_l, mod_l, rw, rb)


GLU_TILE = 2 * LANES


def _experts_kernel(te_ref, nu_ref, slot_ref, nxt_ref, xs_ref, b1_ref, b2_ref, perm_ref, w1_hbm, w2_hbm,
                    ys_ref, w1buf, w2buf, w1p_ref, w2b_ref, sem, *, layer):
    t = pl.program_id(0)
    used = t < nu_ref[0]
    new_expert = (t == 0) | (te_ref[t] != te_ref[jnp.maximum(t - 1, 0)])
    n_col_tiles = w1p_ref.shape[1] // GLU_TILE

    def weight_copies(e, slot):
        return (pltpu.make_async_copy(w1_hbm.at[layer, e], w1buf.at[slot], sem.at[0, slot]),
                pltpu.make_async_copy(w2_hbm.at[layer, e], w2buf.at[slot], sem.at[1, slot]))

    @pl.when(used & new_expert)
    def _():
        slot = slot_ref[t]

        @pl.when(t == 0)
        def _():
            for cp in weight_copies(te_ref[t], slot):
                cp.start()

        for cp in weight_copies(te_ref[t], slot):
            cp.wait()

        @pl.when(nxt_ref[t] >= 0)
        def _():
            for cp in weight_copies(nxt_ref[t], 1 - slot):
                cp.start()

        for c in range(n_col_tiles):
            cols = slice(c * GLU_TILE, (c + 1) * GLU_TILE)
            w1p_ref[:, cols] = _dot(w1buf[slot, :, cols].astype(BF16), perm_ref[...]).astype(BF16)
        w2b_ref[...] = w2buf[slot].astype(BF16)

    @pl.when(used)
    def _():
        lo, hi = _unpack_bf16_pair(xs_ref[...])
        x = jnp.concatenate([lo, hi], axis=1)
        out = b2_ref[0]
        for j in range(n_col_tiles * LANES // EXPERT_CHUNK):
            hcols = slice(2 * j * EXPERT_CHUNK, 2 * (j + 1) * EXPERT_CHUNK)
            hh = _dot(x, w1p_ref[:, hcols]) + b1_ref[0, :, hcols]
            acts = []
            for c in range(2 * EXPERT_CHUNK // GLU_TILE):
                x_glu = jnp.minimum(hh[:, c * GLU_TILE:c * GLU_TILE + LANES], SWIGLU_LIMIT)
                x_lin = jnp.clip(hh[:, c * GLU_TILE + LANES:(c + 1) * GLU_TILE], -SWIGLU_LIMIT, SWIGLU_LIMIT)
                acts.append((x_glu * _sigmoid(SWIGLU_ALPHA * x_glu) * (x_lin + 1.0)).astype(BF16))
            out = out + _dot(jnp.concatenate(acts, axis=1), w2b_ref[j * EXPERT_CHUNK:(j + 1) * EXPERT_CHUNK, :])
        half = out.shape[1] // 2
        ys_ref[...] = _pack_bf16_pair(out[:, :half], out[:, half:])

    @pl.when(t >= nu_ref[0])
    def _():
        ys_ref[...] = jnp.zeros_like(ys_ref)


def _experts(xs, tile_expert, n_used, slot, nxt, w1_all, b1, w2_all, b2, layer):
    n_rows, dw = xs.shape
    tm = EXPERT_TILE
    _, n_exp, d, de2 = w1_all.shape
    src = np.arange(GLU_TILE)
    dst = np.where(src % 2 == 0, src // 2, LANES + src // 2)
    perm = jnp.asarray(dst[:, None] == np.arange(GLU_TILE)[None, :], BF16)
    b1p = b1.reshape(n_exp, de2 // GLU_TILE, LANES, 2).transpose(0, 1, 3, 2).reshape(n_exp, 1, de2)
    row = lambda t, te, nu, sl, nx: (jnp.maximum(jnp.minimum(t, nu[0] - 1), 0), 0)
    exp3 = lambda t, te, nu, sl, nx: (te[t], 0, 0)
    grid_spec = pltpu.PrefetchScalarGridSpec(
        num_scalar_prefetch=4,
        grid=(n_rows // tm,),
        in_specs=[
            pl.BlockSpec((tm, dw), row),
            pl.BlockSpec((1, 1, de2), exp3),
            pl.BlockSpec((1, 1, d), exp3),
            pl.BlockSpec((GLU_TILE, GLU_TILE), lambda t, te, nu, sl, nx: (0, 0)),
            pl.BlockSpec(memory_space=pl.ANY),
            pl.BlockSpec(memory_space=pl.ANY),
        ],
        out_specs=pl.BlockSpec((tm, d // 2), lambda t, te, nu, sl, nx: (t, 0)),
        scratch_shapes=[pltpu.VMEM((2, d, de2), F32), pltpu.VMEM((2, de2 // 2, d), F32),
                        pltpu.VMEM((d, de2), BF16), pltpu.VMEM((de2 // 2, d), BF16),
                        pltpu.SemaphoreType.DMA((2, 2))],
    )
    return pl.pallas_call(
        functools.partial(_experts_kernel, layer=layer),
        grid_spec=grid_spec,
        out_shape=jax.ShapeDtypeStruct((n_rows, d // 2), jnp.uint32),
        compiler_params=_params("arbitrary"),
        name="moe_experts",
    )(tile_expert, n_used, slot, nxt, xs, b1p, b2.reshape(n_exp, 1, d), perm, w1_all, w2_all)


SC_WINDOW = 128


def _sc_worker_rows(n_rows):
    sc = plsc.get_sparse_core_info()
    per_worker = n_rows // (sc.num_cores * sc.num_subcores)
    worker = lax.axis_index("subcore") * sc.num_cores + lax.axis_index("core")
    return worker * per_worker, per_worker


def _sc_mesh():
    return plsc.VectorSubcoreMesh(core_axis_name="core", subcore_axis_name="subcore")


def _sc_gather_rows(table, idx):
    n_idx, width = idx.shape[0], table.shape[1]

    @functools.partial(
        pl.kernel, out_type=jax.ShapeDtypeStruct((n_idx, width), table.dtype), mesh=_sc_mesh(),
        scratch_types=[pltpu.VMEM((SC_WINDOW,), jnp.int32), pltpu.VMEM((SC_WINDOW, width), table.dtype)],
        name="sc_gather_rows")
    def gather(table_hbm, idx_hbm, out_hbm, idx_v, rows_v):
        first, count = _sc_worker_rows(n_idx)

        @pl.loop(0, count // SC_WINDOW)
        def _(c):
            rows = pl.ds(first + c * SC_WINDOW, SC_WINDOW)
            pltpu.sync_copy(idx_hbm.at[rows], idx_v)
            pltpu.sync_copy(table_hbm.at[idx_v], rows_v)
            pltpu.sync_copy(rows_v, out_hbm.at[rows])

    return gather(table, idx)


def _sc_scatter_rows(rows, pos, n_out):
    n, width = rows.shape
    k = pos.shape[0]

    @functools.partial(
        pl.kernel, out_type=jax.ShapeDtypeStruct((n_out, width), rows.dtype), mesh=_sc_mesh(),
        scratch_types=[pltpu.VMEM((k, SC_WINDOW), jnp.int32), pltpu.VMEM((SC_WINDOW, width), rows.dtype)],
        name="sc_scatter_rows")
    def scatter(rows_hbm, pos_hbm, out_hbm, pos_v, rows_v):
        first, count = _sc_worker_rows(n)

        @pl.loop(0, count // SC_WINDOW)
        def _(c):
            src = pl.ds(first + c * SC_WINDOW, SC_WINDOW)
            pltpu.sync_copy(pos_hbm.at[:, src], pos_v)
            pltpu.sync_copy(rows_hbm.at[src], rows_v)
            for j in range(k):
                pltpu.sync_copy(rows_v, out_hbm.at[pos_v.at[j]])

    return scatter(rows, pos)


def _combine_dense_kernel(x_ref, gate_ref, w_ref, fg_ref, *rest, final_norm):
    y_refs, o_ref = rest[:TOP_K], rest[TOP_K]
    w = w_ref[...]
    y_lo = y_hi = None
    for j in range(TOP_K):
        lo, hi = _unpack_bf16_pair(y_refs[j][...])
        wj = w[:, j:j + 1]
        y_lo = wj * lo if j == 0 else y_lo + wj * lo
        y_hi = wj * hi if j == 0 else y_hi + wj * hi
    out = x_ref[...] + gate_ref[0] * jnp.concatenate([y_lo, y_hi], axis=1)
    if final_norm:
        out = out * lax.rsqrt(jnp.mean(out * out, axis=-1, keepdims=True) + NORM_EPS) * fg_ref[...]
    o_ref[...] = out


def _combine_dense(x2, mod_l, gate_idx, seq, top_w, y4, final_g, final_norm, tm=512):
    n, d = x2.shape
    per_batch = seq // tm
    blocks = n // tm
    slot_spec = lambda j: pl.BlockSpec((tm, d // 2), lambda i: (j * blocks + i, 0))
    return pl.pallas_call(
        functools.partial(_combine_dense_kernel, final_norm=final_norm),
        grid=(blocks,),
        in_specs=[
            pl.BlockSpec((tm, d), lambda i: (i, 0)),
            pl.BlockSpec((1, 1, d), lambda i: (i // per_batch, 0, gate_idx)),
            pl.BlockSpec((tm, LANES), lambda i: (i, 0)),
            pl.BlockSpec((1, d), lambda i: (0, 0)),
        ] + [slot_spec(j) for j in range(TOP_K)],
        out_specs=pl.BlockSpec((tm, d), lambda i: (i, 0)),
        out_shape=jax.ShapeDtypeStruct((n, d), F32),
        compiler_params=_params("parallel"),
        name="moe_combine",
    )(x2, mod_l, top_w, final_g.reshape(1, d), *([y4] * TOP_K))


def _moe_ffn(x2, norm_g, mod_l, seq, router_w, router_b, w1_all, b1, w2_all, b2, layer, final_g, final_norm):
    n, d = x2.shape
    n_exp = router_w.shape[1]
    tile = EXPERT_TILE
    hp, idx_f, top_w, rank_f, counts = _router(x2, norm_g, mod_l, 3, seq, router_w, router_b)
    counts = counts[0, :n_exp].astype(jnp.int32)
    padded = (counts + tile - 1) // tile * tile
    ends = jnp.cumsum(padded)
    starts = ends - padded
    idx = idx_f[:, :TOP_K].astype(jnp.int32)
    expert_ids = jnp.arange(n_exp, dtype=jnp.int32)
    start_of = jnp.sum(jnp.where(idx[..., None] == expert_ids, starts, 0), axis=-1)
    pos_flat = (start_of + rank_f[:, :TOP_K].astype(jnp.int32)).reshape(-1)
    n_rows = n * TOP_K + n_exp * tile
    n_tiles = n_rows // tile
    tile_start = jnp.arange(n_tiles, dtype=jnp.int32) * tile
    tile_expert = jnp.sum((ends[None, :] <= tile_start[:, None]).astype(jnp.int32), axis=1)
    tile_expert = jnp.minimum(tile_expert, n_exp - 1)
    n_used = (ends[-1:] // tile).astype(jnp.int32)
    nonempty = counts > 0
    ordinal = jnp.cumsum(nonempty.astype(jnp.int32)) - 1
    later = nonempty[None, :] & (expert_ids[None, :] > expert_ids[:, None])
    nxt_e = jnp.min(jnp.where(later, expert_ids[None, :], n_exp), axis=1)
    nxt_e = jnp.where(nxt_e == n_exp, -1, nxt_e)
    slot = (ordinal[tile_expert] % 2).astype(jnp.int32)
    nxt = nxt_e[tile_expert].astype(jnp.int32)
    pos_slot_major = pos_flat.reshape(n, TOP_K).T
    xs = _sc_scatter_rows(hp, pos_slot_major, n_rows)
    ys = _experts(xs, tile_expert, n_used, slot, nxt, w1_all, b1, w2_all, b2, layer)
    y4 = _sc_gather_rows(ys, pos_slot_major.reshape(-1))
    return _combine_dense(x2, mod_l, 5, seq, top_w, y4, final_g, final_norm)


def kernel(x, c, ada_w, ada_b, norm_mix_g, norm_ffn_g, router_w, router_b, moe_w1, moe_b1, moe_w2, moe_b2, ab_w_in, ab_w_out, hgrn_lb, hgrn_norm_g, rwkv_mu, rwkv_w0, rwkv_w2, rwkv_a0, rwkv_a2, rwkv_g2, rwkv_k_k, rwkv_k_a, rwkv_r_k, rwkv_ln_g, rwkv_ln_b, attn_w_in, attn_w_out, attn_lambda, attn_subln_g, rel_bias_table, final_norm_g):
    batch, seq, d = x.shape
    n = batch * seq
    depth = ada_w.shape[0]
    x2 = x.reshape(n, d)
    mod = _adaln(c, ada_w, ada_b)
    lb_all = jnp.cumsum(jax.nn.softmax(hgrn_lb.astype(F32), axis=1), axis=1)
    for layer in range(depth):
        mod_l = mod[layer].reshape(batch, 1, 6 * d)
        j = layer // 2
        if layer % 2 == 0:
            a_cols = 5 * (d // 2)
            w_in = ab_w_in[j].astype(BF16)
            p_a, p_b = _normmod_proj(x2, norm_mix_g[layer], mod_l, 0, seq,
                                     [w_in[:, :a_cols], w_in[:, a_cols:]], [F32, F32])
            y_a = _hgrn2(p_a, lb_all[:, j], hgrn_norm_g[j], batch, seq)
            y_b = _rwkv7(p_b, rwkv_mu[j], rwkv_w0[j], rwkv_w2[j], rwkv_a0[j], rwkv_a2[j], rwkv_g2[j],
                         rwkv_k_k[j], rwkv_k_a[j], rwkv_r_k[j], rwkv_ln_g[j], rwkv_ln_b[j], batch, seq)
            w_out = ab_w_out[j].astype(BF16)
            x2 = _outproj_residual(x2, mod_l, 2, seq, [y_a, y_b], [w_out[:d // 2], w_out[d // 2:]])
        else:
            (qkv,) = _normmod_proj(x2, norm_mix_g[layer], mod_l, 0, seq, [attn_w_in[j].astype(BF16)], [BF16])
            lam = attn_lambda[j].astype(F32)
            lam_init = 0.8 - 0.6 * math.exp(-0.3 * layer)
            lam_full = jnp.exp(jnp.sum(lam[0] * lam[1])) - jnp.exp(jnp.sum(lam[2] * lam[3])) + lam_init
            o = _diff_attention(qkv, rel_bias_table, lam_full, attn_subln_g[j], lam_init, batch, seq)
            x2 = _outproj_residual(x2, mod_l, 2, seq, [o], [attn_w_out[j].astype(BF16)])
        x2 = _moe_ffn(x2, norm_ffn_g[layer], mod_l, seq, router_w[layer], router_b[layer], moe_w1,
                      moe_b1[layer], moe_w2, moe_b2[layer], layer, final_norm_g, layer == depth - 1)
    return x2.reshape(batch, seq, d)
```

```python
import functools
import math

import jax
import jax.numpy as jnp
import numpy as np
from jax import lax
from jax.experimental import pallas as pl
from jax.experimental.pallas import tpu as pltpu
from jax.experimental.pallas import tpu_sc as plsc

F32 = jnp.float32
BF16 = jnp.bfloat16
HIGHEST = lax.Precision.HIGHEST

NORM_EPS = 1e-6

A_HEAD_DIM = 128
A_CHUNK = 32
A_GROUP = 128
A_HEADS_PER_STEP = 2
B_HEAD_DIM = 64
B_CHUNK = 64
B_GROUP = 8
B_LN_EPS = 1e-5 * B_HEAD_DIM
C_HEAD_DIM = 64
REL_BUCKETS = 32
REL_MAX_DISTANCE = 128
TOP_K = 4
SWIGLU_LIMIT = 7.0
SWIGLU_ALPHA = 1.702

V7X_VMEM_BYTES = 64 * 1024 * 1024
VMEM_LIMIT = V7X_VMEM_BYTES - 8 * 1024 * 1024
LANES = 128


def _params(*sem):
    return pltpu.CompilerParams(dimension_semantics=sem, vmem_limit_bytes=VMEM_LIMIT)


def _sigmoid(x):
    return 1.0 / (1.0 + jnp.exp(-x))


def _silu(x):
    return x * _sigmoid(x)


def _dot(a, b):
    return jnp.dot(a, b, preferred_element_type=F32)


def _dot_nt(a, b):
    return lax.dot_general(a, b, (((1,), (1,)), ((), ())), preferred_element_type=F32)


def _dot_tn(a, b):
    return lax.dot_general(a, b, (((0,), (0,)), ((), ())), preferred_element_type=F32)


def _adaln_kernel(c_ref, w_ref, b_ref, o_ref):
    cond = _silu(c_ref[...])
    o_ref[0] = jnp.dot(cond, w_ref[0], precision=HIGHEST, preferred_element_type=F32) + b_ref[0]


def _adaln(c, ada_w, ada_b):
    n_layers, d, n_out = ada_w.shape
    batch = c.shape[0]
    tn = 1536
    return pl.pallas_call(
        _adaln_kernel,
        grid=(n_layers, n_out // tn),
        in_specs=[
            pl.BlockSpec((batch, d), lambda l, j: (0, 0)),
            pl.BlockSpec((1, d, tn), lambda l, j: (l, 0, j)),
            pl.BlockSpec((1, 1, tn), lambda l, j: (l, 0, j)),
        ],
        out_specs=pl.BlockSpec((1, batch, tn), lambda l, j: (l, 0, j)),
        out_shape=jax.ShapeDtypeStruct((n_layers, batch, n_out), F32),
        compiler_params=_params("parallel", "parallel"),
        name="adaln",
    )(c, ada_w, ada_b.reshape(n_layers, 1, n_out))


def _modulated_norm(x, g, shift, scale):
    y = x * lax.rsqrt(jnp.mean(x * x, axis=-1, keepdims=True) + NORM_EPS)
    return (y * g) * (1.0 + scale) + shift


def _normmod_proj_kernel(x_ref, g_ref, sh_ref, sc_ref, *rest, n_w):
    w_refs, o_refs = rest[:n_w], rest[n_w:]
    h = _modulated_norm(x_ref[...], g_ref[...], sh_ref[0], sc_ref[0]).astype(BF16)
    for w_ref, o_ref in zip(w_refs, o_refs):
        o_ref[...] = _dot(h, w_ref[...]).astype(o_ref.dtype)


def _normmod_proj(x2, g, mod_l, shift_idx, seq, weights, out_dtypes, tm=512):
    n, d = x2.shape
    per_batch = seq // tm
    in_specs = [
        pl.BlockSpec((tm, d), lambda i: (i, 0)),
        pl.BlockSpec((1, d), lambda i: (0, 0)),
        pl.BlockSpec((1, 1, d), lambda i: (i // per_batch, 0, shift_idx)),
        pl.BlockSpec((1, 1, d), lambda i: (i // per_batch, 0, shift_idx + 1)),
    ]
    in_specs += [pl.BlockSpec(w.shape, lambda i: (0, 0)) for w in weights]
    out_specs = [pl.BlockSpec((tm, w.shape[1]), lambda i: (i, 0)) for w in weights]
    out_shape = [jax.ShapeDtypeStruct((n, w.shape[1]), dt) for w, dt in zip(weights, out_dtypes)]
    return pl.pallas_call(
        functools.partial(_normmod_proj_kernel, n_w=len(weights)),
        grid=(n // tm,),
        in_specs=in_specs,
        out_specs=out_specs,
        out_shape=out_shape,
        compiler_params=_params("parallel"),
        name="normmod_proj",
    )(x2, g.reshape(1, d), mod_l, mod_l, *weights)


def _outproj_kernel(x_ref, gate_ref, *rest, n_y):
    y_refs, w_refs, o_ref = rest[:n_y], rest[n_y:2 * n_y], rest[2 * n_y]
    acc = _dot(y_refs[0][...], w_refs[0][...])
    for y_ref, w_ref in zip(y_refs[1:], w_refs[1:]):
        acc += _dot(y_ref[...], w_ref[...])
    o_ref[...] = x_ref[...] + gate_ref[0] * acc


def _outproj_residual(x2, mod_l, gate_idx, seq, ys, ws, tm=512):
    n, d = x2.shape
    per_batch = seq // tm
    in_specs = [
        pl.BlockSpec((tm, d), lambda i: (i, 0)),
        pl.BlockSpec((1, 1, d), lambda i: (i // per_batch, 0, gate_idx)),
    ]
    in_specs += [pl.BlockSpec((tm, y.shape[1]), lambda i: (i, 0)) for y in ys]
    in_specs += [pl.BlockSpec(w.shape, lambda i: (0, 0)) for w in ws]
    return pl.pallas_call(
        functools.partial(_outproj_kernel, n_y=len(ys)),
        grid=(n // tm,),
        in_specs=in_specs,
        out_specs=pl.BlockSpec((tm, d), lambda i: (i, 0)),
        out_shape=jax.ShapeDtypeStruct((n, d), F32),
        compiler_params=_params("parallel"),
        name="outproj_residual",
    )(x2, mod_l, *ys, *ws)


def _chunk_cumsum(x, chunk, reverse):
    rows = x.shape[0]
    pos = lax.broadcasted_iota(jnp.int32, x.shape, 0) % chunk
    s = 1
    while s < chunk:
        if reverse:
            x = x + jnp.where(pos < chunk - s, pltpu.roll(x, rows - s, axis=0), 0.0)
        else:
            x = x + jnp.where(pos >= s, pltpu.roll(x, s, axis=0), 0.0)
        s *= 2
    return x


def _hgrn_groups(slabs):
    g_rows, dk = slabs[0][0].shape
    n_chunks = g_rows // A_CHUNK
    ti = lax.broadcasted_iota(jnp.int32, (A_CHUNK, A_CHUNK), 0)
    si = lax.broadcasted_iota(jnp.int32, (A_CHUNK, A_CHUNK), 1)
    chunk_rows = [slice(c * A_CHUNK, (c + 1) * A_CHUNK) for c in range(n_chunks)]
    prep = []
    for q, f, v, lb, st, reverse in slabs:
        fg = lb + (1.0 - lb) * _sigmoid(f)
        k = 1.0 - fg
        b = _chunk_cumsum(jnp.log(fg), A_CHUNK, reverse)
        b3 = b.reshape(n_chunks, A_CHUNK, dk)
        edge = b3[:, 0:1, :] if reverse else b3[:, A_CHUNK - 1:A_CHUNK, :]
        prep.append(dict(
            q_in=(q * jnp.exp(b)).astype(BF16), k_in=(k * jnp.exp(-b)).astype(BF16),
            k_st=(k.reshape(n_chunks, A_CHUNK, dk) * jnp.exp(edge - b3)).astype(BF16),
            decay=jnp.exp(edge), vb=v.astype(BF16), st=st, reverse=reverse,
            mask=(si >= ti) if reverse else (si <= ti)))
    for p in prep:
        p["scores"] = [_dot_nt(p["q_in"][r], p["k_in"][r]) for r in chunk_rows]
        p["dstate"] = [_dot_tn(p["vb"][r], p["k_st"][c]) for c, r in enumerate(chunk_rows)]
    for p in prep:
        p["intra"] = [_dot(jnp.where(p["mask"], s, 0.0).astype(BF16), p["vb"][r])
                      for s, r in zip(p["scores"], chunk_rows)]
    results = []
    for p in prep:
        st = p["st"]
        outs = [None] * n_chunks
        for c in (range(n_chunks - 1, -1, -1) if p["reverse"] else range(n_chunks)):
            outs[c] = p["intra"][c] + _dot_nt(p["q_in"][chunk_rows[c]], st.astype(BF16))
            st = st * p["decay"][c] + p["dstate"][c]
        results.append((jnp.concatenate(outs, axis=0), st))
    return results


def _hgrn_kernel(q_ref, ff_ref, fb_ref, i_ref, g_ref, lb_ref, ng_ref, o_ref, of_ref, ob_ref):
    seq = q_ref.shape[0]
    dk = A_HEAD_DIM
    n_heads = q_ref.shape[1] // dk
    n_groups = seq // A_GROUP
    head_cols = [slice(h * dk, (h + 1) * dk) for h in range(n_heads)]

    def body(j, states):
        rf = pl.ds(pl.multiple_of(j * A_GROUP, A_GROUP), A_GROUP)
        rb = pl.ds(pl.multiple_of((n_groups - 1 - j) * A_GROUP, A_GROUP), A_GROUP)
        slabs = []
        for h, cols in enumerate(head_cols):
            slabs.append((_silu(q_ref[rf, cols]), ff_ref[rf, cols], i_ref[rf, cols], lb_ref[0:1, cols],
                          states[2 * h], False))
            slabs.append((_silu(q_ref[rb, cols]), fb_ref[rb, cols], i_ref[rb, cols], lb_ref[1:2, cols],
                          states[2 * h + 1], True))
        results = _hgrn_groups(slabs)
        for h, cols in enumerate(head_cols):
            of_ref[rf, cols] = results[2 * h][0]
            ob_ref[rb, cols] = results[2 * h + 1][0]
        return tuple(r[1] for r in results)

    zero = jnp.zeros((dk, dk), F32)
    lax.fori_loop(0, n_groups, body, (zero,) * (2 * n_heads))

    def finish(j, carry):
        r = pl.ds(pl.multiple_of(j * A_GROUP, A_GROUP), A_GROUP)
        for cols in head_cols:
            o = of_ref[r, cols] + ob_ref[r, cols]
            o = o * lax.rsqrt(jnp.mean(o * o, axis=-1, keepdims=True) + NORM_EPS) * ng_ref[...]
            o_ref[r, cols] = (o * _silu(g_ref[r, cols])).astype(o_ref.dtype)
        return carry

    lax.fori_loop(0, n_groups, finish, 0)


def _hgrn2(p_a, lb, norm_g, batch, seq):
    n = p_a.shape[0]
    width = p_a.shape[1] // 5
    cols = A_HEADS_PER_STEP * A_HEAD_DIM
    steps = width // cols
    sect = lambda s: pl.BlockSpec((seq, cols), lambda b, h, s=s: (b, s * steps + h))
    return pl.pallas_call(
        _hgrn_kernel,
        grid=(batch, steps),
        in_specs=[sect(0), sect(1), sect(2), sect(3), sect(4),
                  pl.BlockSpec((2, cols), lambda b, h: (0, h)),
                  pl.BlockSpec((1, A_HEAD_DIM), lambda b, h: (0, 0))],
        out_specs=pl.BlockSpec((seq, cols), lambda b, h: (b, h)),
        out_shape=jax.ShapeDtypeStruct((n, width), BF16),
        scratch_shapes=[pltpu.VMEM((seq, cols), F32), pltpu.VMEM((seq, cols), F32)],
        compiler_params=_params("parallel", "parallel"),
        name="hgrn2",
    )(p_a, p_a, p_a, p_a, p_a, lb, norm_g.reshape(1, A_HEAD_DIM))


def _softplus(z):
    return jnp.maximum(z, 0.0) + jnp.log(1.0 + jnp.exp(-jnp.abs(z)))


def _rwkv_prep_kernel(p_ref, prev_ref, next_ref, mu_ref, w0_ref, w2_ref, a0_ref, a2_ref, g2_ref,
                      kk_ref, ka_ref, rk_ref, ones_ref, tri_ref,
                      r_out, k_out, v_out, kkn_out, b_out, cumf_out, cumb_out, g_out, bonus_out):
    i = pl.program_id(1)
    last = pl.num_programs(1) - 1
    ts = p_ref.shape[0]
    width = a0_ref.shape[1]
    heads = width // B_HEAD_DIM
    p = p_ref[...]
    prev_row = jnp.where(i == 0, 0.0, prev_ref[7:8, :])
    next_row = jnp.where(i == last, 0.0, next_ref[0:1, :])
    row = lax.broadcasted_iota(jnp.int32, (ts, 1), 0)
    up = jnp.where(row == 0, prev_row, pltpu.roll(p, 1, axis=0))
    dn = jnp.where(row == ts - 1, next_row, pltpu.roll(p, ts - 1, axis=0))
    p = p + mu_ref[...] * (0.5 * (up + dn) - p)

    r = p[:, 0:width]
    k = p[:, width:2 * width]
    v = p[:, 2 * width:3 * width]
    o = 3 * width
    wlo_f = p[:, o:o + 64]
    wlo_b = p[:, o + 64:o + 128]
    alo = p[:, o + 128:o + 192]
    glo = p[:, o + 192:o + 320]

    def log_decay(wlo, d):
        z = w0_ref[d:d + 1, :] + _dot(jnp.tanh(wlo).astype(BF16), w2_ref[d])
        w = -_softplus(-z) - 0.5
        return -jnp.exp(w)

    def split_dot(sel, t, right):
        hi = t.astype(BF16)
        lo = (t - hi.astype(F32)).astype(BF16)
        return (_dot(hi, sel) + _dot(lo, sel)) if right else (_dot(sel, hi) + _dot(sel, lo))

    cum_f = split_dot(tri_ref[0], log_decay(wlo_f, 0), False)
    cum_b = split_dot(tri_ref[1], log_decay(wlo_b, 1), False)
    a = _sigmoid(a0_ref[...] + _dot(alo.astype(BF16), a2_ref[...]))
    g = _dot(_sigmoid(glo).astype(BF16), g2_ref[...])
    kk = k * kk_ref[...]
    head_sum = lambda t: split_dot(ones_ref[...], t, True)

    kk_n = kk / jnp.maximum(jnp.sqrt(head_sum(kk * kk)), 1e-12)
    k_mod = k * (1.0 + (a - 1.0) * ka_ref[...])
    bonus = head_sum(r * k_mod * rk_ref[...]) * v
    g_out[...] = g
    bonus_out[...] = bonus
    for h in range(heads):
        sl = slice(h * B_HEAD_DIM, (h + 1) * B_HEAD_DIM)
        r_out[0, h] = r[:, sl]
        k_out[0, h] = k_mod[:, sl]
        v_out[0, h] = v[:, sl]
        kkn_out[0, h] = kk_n[:, sl]
        b_out[0, h] = (kk_n * a)[:, sl]
        cumf_out[0, h] = cum_f[:, sl]
        cumb_out[0, h] = cum_b[:, sl]


def _rwkv_prep(p_b, mu, w0, w2, a0, a2, g2, k_k, k_a, r_k, batch, seq, ts=256):
    n, cols = p_b.shape
    width = a0.shape[0]
    heads = width // B_HEAD_DIM
    nblk = seq // ts
    rows8 = ts // 8
    head_id = np.arange(width) // B_HEAD_DIM
    ones = jnp.asarray(head_id[:, None] == head_id[None, :], BF16)
    full = lambda a: pl.BlockSpec(a.shape, lambda b, i: (0,) * a.ndim)
    vec = lambda a: a.reshape(1, -1)
    row = np.arange(ts)
    same_chunk = (row[:, None] // B_CHUNK) == (row[None, :] // B_CHUNK)
    tri = jnp.asarray(np.stack([same_chunk & (row[None, :] <= row[:, None]),
                                same_chunk & (row[None, :] >= row[:, None])]), BF16)
    args = [vec(mu), w0, w2.astype(BF16), vec(a0), a2.astype(BF16), g2.astype(BF16),
            vec(k_k), vec(k_a), vec(r_k), ones, tri]
    hm = pl.BlockSpec((1, heads, ts, B_HEAD_DIM), lambda b, i: (b, 0, i, 0))
    tokm = pl.BlockSpec((ts, width), lambda b, i: (b * nblk + i, 0))
    hm_shape = jax.ShapeDtypeStruct((batch, heads, seq, B_HEAD_DIM), F32)
    tok_shape = jax.ShapeDtypeStruct((n, width), F32)
    return pl.pallas_call(
        _rwkv_prep_kernel,
        grid=(batch, nblk),
        in_specs=[
            pl.BlockSpec((ts, cols), lambda b, i: (b * nblk + i, 0)),
            pl.BlockSpec((8, cols), lambda b, i: (jnp.maximum((b * nblk + i) * rows8 - 1, 0), 0)),
            pl.BlockSpec((8, cols), lambda b, i: (jnp.minimum((b * nblk + i + 1) * rows8, n // 8 - 1), 0)),
        ] + [full(a) for a in args],
        out_specs=[hm] * 7 + [tokm] * 2,
        out_shape=[hm_shape] * 7 + [tok_shape] * 2,
        compiler_params=_params("parallel", "parallel"),
        name="rwkv_prep",
    )(p_b, p_b, p_b, *args)


def _rwkv_chunks(chains):
    c, d = chains[0][0].shape
    ti = lax.broadcasted_iota(jnp.int32, (c, c), 0)
    si = lax.broadcasted_iota(jnp.int32, (c, c), 1)
    eye = jnp.where(ti == si, 1.0, 0.0)
    masks = {rev: (((si > ti), (si >= ti)) if rev else ((si < ti), (si <= ti))) for rev in (False, True)}

    row = lax.broadcasted_iota(jnp.int32, (c, d), 0)
    st = []
    for r, k, v, kk, b, cum, tt, rev in chains:
        before = (jnp.where(row == c - 1, 0.0, pltpu.roll(cum, c - 1, axis=0)) if rev
                  else jnp.where(row == 0, 0.0, pltpu.roll(cum, 1, axis=0)))
        grow = jnp.exp(-cum)
        kt = kk * jnp.exp(before)
        rt = r * jnp.exp(cum)
        b_ = (b * grow).astype(BF16)
        kb = jnp.concatenate([(k * grow).astype(BF16), b_], axis=0)
        st.append(dict(v=v, vb=v.astype(BF16), kt=kt, rt=rt, b_=b_, kb=kb, tt=tt, ttb=tt.astype(BF16),
                       decay=jnp.exp(cum[0:1, :] if rev else cum[c - 1:c, :]), rev=rev,
                       lhs=jnp.concatenate([kt, rt], axis=0).astype(BF16)))
    for s in st:
        s["big"] = _dot_nt(s["lhs"], s["kb"])
    for s in st:
        strict, incl = masks[s["rev"]]
        big = s["big"]
        a_ab = jnp.where(strict, big[:c, c:], 0.0)
        s["a_ak"] = jnp.where(strict, big[:c, :c], 0.0).astype(BF16)
        s["p"] = jnp.concatenate([jnp.where(incl, big[c:, :c], 0.0), jnp.where(incl, -big[c:, c:], 0.0)],
                                 axis=1).astype(BF16)
        s["m"] = eye - a_ab
        s["ab"] = a_ab.astype(BF16)
    for s in st:
        s["aj"] = _dot(s["ab"], s["ab"])
        s["av"] = _dot(s["a_ak"], s["vb"])
    span = 2
    while span < c:
        span *= 2
        for s in st:
            ajb = s["aj"].astype(BF16)
            if span < c:
                both = _dot(jnp.concatenate([s["aj"], s["m"]], axis=0).astype(BF16), ajb)
                s["aj"], s["m"] = both[:c], s["m"] + both[c:]
            else:
                s["m"] = s["m"] + _dot(s["m"].astype(BF16), ajb)
    for s in st:
        s["wu"] = _dot(s["m"].astype(BF16), jnp.concatenate([s["kt"], s["av"]], axis=1).astype(BF16))
    for s in st:
        w_, u0, v = s["wu"][:, :d], s["wu"][:, d:], s["v"]
        rhs = jnp.concatenate([jnp.concatenate([v, jnp.zeros_like(v)], axis=1),
                               jnp.concatenate([u0, w_], axis=1)], axis=0).astype(BF16)
        s["yq"] = _dot(s["p"], rhs)
        s["h_t"] = _dot_tn(jnp.concatenate([v, -u0], axis=0).astype(BF16), s["kb"])
        s["bw"] = _dot_tn(s["b_"], w_.astype(BF16))
    out = []
    for s in st:
        q = s["rt"] + s["yq"][:, d:]
        y = _dot_nt(q.astype(BF16), s["ttb"]) + s["yq"][:, :d]
        tt = (s["tt"] - _dot_nt(s["ttb"], s["bw"].astype(BF16)) + s["h_t"]) * s["decay"]
        out.append((y, tt))
    return out


def _rwkv_scan_kernel(rf, kf, vf, kkf, bf, lwf, rb, kb, vb, kkb, bb, lwb, yf_ref, yb_ref, tf_ref, tb_ref):
    heads, ts = rf.shape[1], rf.shape[2]
    n_chunks = ts // B_CHUNK

    @pl.when(pl.program_id(1) == 0)
    def _():
        tf_ref[...] = jnp.zeros_like(tf_ref)
        tb_ref[...] = jnp.zeros_like(tb_ref)

    def body(j, carry):
        rows_f = pl.ds(pl.multiple_of(j * B_CHUNK, B_CHUNK), B_CHUNK)
        rows_b = pl.ds(pl.multiple_of((n_chunks - 1 - j) * B_CHUNK, B_CHUNK), B_CHUNK)
        for h0 in range(0, heads, B_GROUP):
            group = range(h0, h0 + B_GROUP)
            chains = []
            for h in group:
                chains.append([ref[0, h, rows_f, :] for ref in (rf, kf, vf, kkf, bf, lwf)] + [tf_ref[h], False])
                chains.append([ref[0, h, rows_b, :] for ref in (rb, kb, vb, kkb, bb, lwb)] + [tb_ref[h], True])
            res = _rwkv_chunks(chains)
            for n, h in enumerate(group):
                yf_ref[0, h, rows_f, :], tf_ref[h] = res[2 * n]
                yb_ref[0, h, rows_b, :], tb_ref[h] = res[2 * n + 1]
        return carry

    lax.fori_loop(0, n_chunks, body, 0)


def _rwkv_scan(r, k, v, kkn, b, lw_f, lw_b, ts=256):
    batch, heads, seq, d = r.shape
    nblk = seq // ts
    fwd = pl.BlockSpec((1, heads, ts, d), lambda bi, i: (bi, 0, i, 0))
    bwd = pl.BlockSpec((1, heads, ts, d), lambda bi, i: (bi, 0, nblk - 1 - i, 0))
    shape = jax.ShapeDtypeStruct(r.shape, F32)
    return pl.pallas_call(
        _rwkv_scan_kernel,
        grid=(batch, nblk),
        in_specs=[fwd] * 6 + [bwd] * 6,
        out_specs=[fwd, bwd],
        out_shape=[shape, shape],
        scratch_shapes=[pltpu.VMEM((heads, d, d), F32), pltpu.VMEM((heads, d, d), F32)],
        compiler_params=_params("parallel", "arbitrary"),
        name="rwkv_scan",
    )(r, k, v, kkn, b, lw_f, r, k, v, kkn, b, lw_b)


def _rwkv_post_kernel(yf_ref, yb_ref, g_ref, bonus_ref, lng_ref, lnb_ref, o_ref):
    heads = yf_ref.shape[1]
    outs = []
    for h in range(heads):
        y = yf_ref[0, h] + yb_ref[0, h]
        mean = jnp.mean(y, axis=-1, keepdims=True)
        var = jnp.mean(jnp.square(y - mean), axis=-1, keepdims=True)
        outs.append((y - mean) * lax.rsqrt(var + B_LN_EPS))
    yn = jnp.concatenate(outs, axis=1) * lng_ref[...] + lnb_ref[...]
    o_ref[...] = ((yn + bonus_ref[...]) * g_ref[...]).astype(o_ref.dtype)


def _rwkv_post(y_f, y_b, g, bonus, ln_g, ln_b, ts=256):
    batch, heads, seq, d = y_f.shape
    n, width = g.shape
    nblk = seq // ts
    hm = pl.BlockSpec((1, heads, ts, d), lambda b, i: (b, 0, i, 0))
    tokm = pl.BlockSpec((ts, width), lambda b, i: (b * nblk + i, 0))
    vec = pl.BlockSpec((1, width), lambda b, i: (0, 0))
    return pl.pallas_call(
        _rwkv_post_kernel,
        grid=(batch, nblk),
        in_specs=[hm, hm, tokm, tokm, vec, vec],
        out_specs=tokm,
        out_shape=jax.ShapeDtypeStruct((n, width), BF16),
        compiler_params=_params("parallel", "parallel"),
        name="rwkv_post",
    )(y_f, y_b, g, bonus, ln_g.reshape(1, width), ln_b.reshape(1, width))


def _rwkv7(p_b, mu, w0, w2, a0, a2, g2, k_k, k_a, r_k, ln_g, ln_b, batch, seq):
    r, k, v, kkn, b, lw_f, lw_b, g, bonus = _rwkv_prep(p_b, mu, w0, w2, a0, a2, g2, k_k, k_a, r_k, batch, seq)
    y_f, y_b = _rwkv_scan(r, k, v, kkn, b, lw_f, lw_b)
    return _rwkv_post(y_f, y_b, g, bonus, ln_g, ln_b)


_BUCKET_EDGES = (0, 1, 2, 3, 4, 5, 6, 7, 8, 12, 16, 23, 32, 46, 64, 91)
ATTN_TK = 128


def _bias_band_kernel(table_ref, o_ref):
    h = pl.program_id(0)
    n_d, tq, tk = o_ref.shape[1:]
    half = REL_BUCKETS // 2
    r = lax.broadcasted_iota(jnp.int32, (tq, tk), 0)
    c = lax.broadcasted_iota(jnp.int32, (tq, tk), 1)
    for d in range(n_d):
        rel = tk * (d - 1) + c - r
        n = jnp.abs(rel)
        vals = []
        for sign in range(2):
            val = jnp.full((tq, tk), table_ref[sign * half + half - 1, h], F32)
            for bkt in range(half - 2, -1, -1):
                val = jnp.where(n < _BUCKET_EDGES[bkt + 1], table_ref[sign * half + bkt, h], val)
            vals.append(val)
        o_ref[0, d] = jnp.where(rel > 0, vals[1], vals[0])


def _bias_band(rel_table, tq):
    heads = rel_table.shape[1]
    n_d = tq // ATTN_TK + 2
    return pl.pallas_call(
        _bias_band_kernel,
        grid=(heads,),
        in_specs=[pl.BlockSpec(memory_space=pltpu.SMEM)],
        out_specs=pl.BlockSpec((1, n_d, tq, ATTN_TK), lambda h: (h, 0, 0, 0)),
        out_shape=jax.ShapeDtypeStruct((heads, n_d, tq, ATTN_TK), F32),
        compiler_params=_params("parallel"),
        name="bias_band",
    )(rel_table)


def _diff_attn_kernel(table_ref, lam_ref, q_ref, k_ref, v_ref, band_ref, g_ref, o_ref, lg_ref, vext_ref, *,
                      lam_init):
    h = pl.program_id(1)
    i = pl.program_id(2)
    tq, dh2 = q_ref.shape
    seq = k_ref.shape[0]
    n_d = band_ref.shape[1]
    n_kt = seq // ATTN_TK
    ratio = tq // ATTN_TK
    half = REL_BUCKETS // 2
    k = k_ref[...]

    @pl.when(i == 0)
    def _():
        v = v_ref[...]
        vext_ref[...] = jnp.concatenate([v, jnp.ones_like(v)], axis=1)

    lane = lax.broadcasted_iota(jnp.int32, (1, dh2), 1)
    q = q_ref[...] * jnp.asarray(C_HEAD_DIM ** -0.5, q_ref.dtype)
    zero = jnp.zeros_like(q)
    first_band = ratio * i - 1
    col = lax.broadcasted_iota(jnp.int32, (1, seq), 1)
    far_row = jnp.where(col < first_band * ATTN_TK, table_ref[half - 1, h],
                        jnp.where(col >= (first_band + n_d) * ATTN_TK, table_ref[REL_BUCKETS - 1, h],
                                  -jnp.inf))
    row_max = []
    for m in range(2):
        qm = jnp.where((lane >= m * C_HEAD_DIM) & (lane < (m + 1) * C_HEAD_DIM), q, zero)
        far = _dot_nt(qm, k) + far_row
        mx = jnp.max(far, axis=-1, keepdims=True)
        lg_ref[m] = far
        for d in (0, n_d - 1) + tuple(range(1, n_d - 1)):
            kt = first_band + d
            valid = (kt >= 0) & (kt < n_kt)
            c0 = pl.multiple_of(jnp.clip(kt, 0, n_kt - 1) * ATTN_TK, ATTN_TK)
            near = _dot_nt(qm, k_ref[pl.ds(c0, ATTN_TK), :]) + band_ref[0, d]
            mx = jnp.maximum(mx, jnp.where(valid, jnp.max(near, axis=-1, keepdims=True), -jnp.inf))
            lg_ref[m, :, pl.ds(c0, ATTN_TK)] = near
        row_max.append(mx)
    parts = []
    for m in range(2):
        e = jnp.exp(lg_ref[m] - row_max[m]).astype(BF16)
        pv = _dot(e, vext_ref[...])
        parts.append(pv[:, :dh2] * (1.0 / pv[:, dh2:dh2 + 1]))
    o = parts[0] - lam_ref[0] * parts[1]
    o = o * lax.rsqrt(jnp.mean(o * o, axis=-1, keepdims=True) + NORM_EPS) * g_ref[...]
    o_ref[...] = (o * (1.0 - lam_init)).astype(o_ref.dtype)


def _diff_attention(qkv, rel_table, lam_full, subln_g, lam_init, batch, seq, tq=256):
    n = qkv.shape[0]
    dh2 = 2 * C_HEAD_DIM
    heads = qkv.shape[1] // (3 * dh2)
    nq = seq // tq
    band = _bias_band(rel_table, tq)
    return pl.pallas_call(
        functools.partial(_diff_attn_kernel, lam_init=lam_init),
        grid=(batch, heads, nq),
        in_specs=[
            pl.BlockSpec(memory_space=pltpu.SMEM),
            pl.BlockSpec(memory_space=pltpu.SMEM),
            pl.BlockSpec((tq, dh2), lambda b, h, i: (b * nq + i, h)),
            pl.BlockSpec((seq, dh2), lambda b, h, i: (b, heads + h)),
            pl.BlockSpec((seq, dh2), lambda b, h, i: (b, 2 * heads + h)),
            pl.BlockSpec((1,) + band.shape[1:], lambda b, h, i: (h, 0, 0, 0)),
            pl.BlockSpec((1, dh2), lambda b, h, i: (0, 0)),
        ],
        out_specs=pl.BlockSpec((tq, dh2), lambda b, h, i: (b * nq + i, h)),
        out_shape=jax.ShapeDtypeStruct((n, heads * dh2), BF16),
        scratch_shapes=[pltpu.VMEM((2, tq, seq), F32), pltpu.VMEM((seq, 2 * dh2), BF16)],
        compiler_params=_params("parallel", "parallel", "arbitrary"),
        name="diff_attention",
    )(rel_table, lam_full.reshape(1), qkv, qkv, qkv, band, subln_g.reshape(1, dh2))


MOE_TILE = 256
EXPERT_TILE = 512
EXPERT_CHUNK = 512
_HI_MASK = np.uint32(0xFFFF0000)


def _pack_bf16_pair(lo, hi):
    as_bits = lambda t: lax.bitcast_convert_type(t.astype(BF16).astype(F32), jnp.uint32)
    return (as_bits(hi) & _HI_MASK) | (as_bits(lo) >> 16)


def _unpack_bf16_pair(word):
    lo = lax.bitcast_convert_type(word << 16, F32).astype(BF16)
    hi = lax.bitcast_convert_type(word & _HI_MASK, F32).astype(BF16)
    return lo, hi


def _router_kernel(x_ref, g_ref, sh_ref, sc_ref, rw_ref, rb_ref, hp_ref, idx_ref, w_ref, rank_ref, cnt_ref,
                   run_ref, *, n_experts):
    @pl.when(pl.program_id(0) == 0)
    def _():
        run_ref[...] = jnp.zeros_like(run_ref)

    tm, d = x_ref.shape
    h = _modulated_norm(x_ref[...], g_ref[...], sh_ref[0], sc_ref[0])
    hp_ref[...] = _pack_bf16_pair(h[:, :d // 2], h[:, d // 2:])
    logits = _dot(h.astype(BF16), rw_ref[...]) + rb_ref[...]
    lane = lax.broadcasted_iota(jnp.int32, logits.shape, 1).astype(F32)
    neg = jnp.float32(-jnp.inf)
    logits = jnp.where(lane < n_experts, logits, neg)
    picks, vals = [], []
    for _ in range(TOP_K):
        m = jnp.max(logits, axis=-1, keepdims=True)
        first = jnp.min(jnp.where(logits == m, lane, float(LANES)), axis=-1, keepdims=True)
        hit = lane == first
        picks.append((first, hit))
        vals.append(m)
        logits = jnp.where(hit, neg, logits)
    es = [jnp.exp(v - vals[0]) for v in vals]
    inv = 1.0 / (es[0] + es[1] + es[2] + es[3])
    assigned = picks[0][1] | picks[1][1] | picks[2][1] | picks[3][1]
    onehot = jnp.where(assigned, 1.0, 0.0)
    ri = lax.broadcasted_iota(jnp.int32, (tm, tm), 0)
    ci = lax.broadcasted_iota(jnp.int32, (tm, tm), 1)
    earlier = jnp.where(ci < ri, 1.0, 0.0).astype(BF16)
    before = _dot(earlier, onehot.astype(BF16)) + run_ref[...]
    idx_o = jnp.zeros(logits.shape, F32)
    w_o = jnp.zeros(logits.shape, F32)
    rank_o = jnp.zeros(logits.shape, F32)
    for j, (first, hit) in enumerate(picks):
        rank_j = jnp.sum(jnp.where(hit, before, 0.0), axis=-1, keepdims=True)
        idx_o = jnp.where(lane == j, first, idx_o)
        w_o = jnp.where(lane == j, es[j] * inv, w_o)
        rank_o = jnp.where(lane == j, rank_j, rank_o)
    idx_ref[...] = idx_o
    w_ref[...] = w_o
    rank_ref[...] = rank_o
    run_ref[...] += jnp.sum(onehot, axis=0, keepdims=True)
    cnt_ref[...] = run_ref[...]


def _router(x2, g, mod_l, shift_idx, seq, router_w, router_b):
    n, d = x2.shape
    tm = MOE_TILE
    n_experts = router_w.shape[1]
    per_batch = seq // tm
    rw = jnp.zeros((d, LANES), BF16).at[:, :n_experts].set(router_w.astype(BF16))
    rb = jnp.zeros((1, LANES), F32).at[0, :n_experts].set(router_b)
    lane_out = pl.BlockSpec((tm, LANES), lambda i: (i, 0))
    lane_shape = jax.ShapeDtypeStruct((n, LANES), F32)
    return pl.pallas_call(
        functools.partial(_router_kernel, n_experts=n_experts),
        grid=(n // tm,),
        in_specs=[
            pl.BlockSpec((tm, d), lambda i: (i, 0)),
            pl.BlockSpec((1, d), lambda i: (0, 0)),
            pl.BlockSpec((1, 1, d), lambda i: (i // per_batch, 0, shift_idx)),
            pl.BlockSpec((1, 1, d), lambda i: (i // per_batch, 0, shift_idx + 1)),
            pl.BlockSpec((d, LANES), lambda i: (0, 0)),
            pl.BlockSpec((1, LANES), lambda i: (0, 0)),
        ],
        out_specs=[pl.BlockSpec((tm, d // 2), lambda i: (i, 0)), lane_out, lane_out, lane_out,
                   pl.BlockSpec((1, LANES), lambda i: (0, 0))],
        out_shape=[jax.ShapeDtypeStruct((n, d // 2), jnp.uint32), lane_shape, lane_shape, lane_shape,
                   jax.ShapeDtypeStruct((1, LANES), F32)],
        scratch_shapes=[pltpu.VMEM((1, LANES), F32)],
        compiler_params=_params("arbitrary"),
        name="moe_router",
    )(x2, g.reshape(1, d), mod_l, mod_l, rw, rb)


GLU_TILE = 2 * LANES


def _experts_kernel(te_ref, nu_ref, slot_ref, nxt_ref, xs_ref, b1_ref, b2_ref, perm_ref, w1_hbm, w2_hbm,
                    ys_ref, w1buf, w2buf, w1p_ref, w2b_ref, sem, *, layer):
    t = pl.program_id(0)
    used = t < nu_ref[0]
    new_expert = (t == 0) | (te_ref[t] != te_ref[jnp.maximum(t - 1, 0)])
    n_col_tiles = w1p_ref.shape[1] // GLU_TILE

    def weight_copies(e, slot):
        return (pltpu.make_async_copy(w1_hbm.at[layer, e], w1buf.at[slot], sem.at[0, slot]),
                pltpu.make_async_copy(w2_hbm.at[layer, e], w2buf.at[slot], sem.at[1, slot]))

    @pl.when(used & new_expert)
    def _():
        slot = slot_ref[t]

        @pl.when(t == 0)
        def _():
            for cp in weight_copies(te_ref[t], slot):
                cp.start()

        for cp in weight_copies(te_ref[t], slot):
            cp.wait()

        @pl.when(nxt_ref[t] >= 0)
        def _():
            for cp in weight_copies(nxt_ref[t], 1 - slot):
                cp.start()

        for c in range(n_col_tiles):
            cols = slice(c * GLU_TILE, (c + 1) * GLU_TILE)
            w1p_ref[:, cols] = _dot(w1buf[slot, :, cols].astype(BF16), perm_ref[...]).astype(BF16)
        w2b_ref[...] = w2buf[slot].astype(BF16)

    @pl.when(used)
    def _():
        lo, hi = _unpack_bf16_pair(xs_ref[...])
        x = jnp.concatenate([lo, hi], axis=1)
        out = b2_ref[0]
        for j in range(n_col_tiles * LANES // EXPERT_CHUNK):
            hcols = slice(2 * j * EXPERT_CHUNK, 2 * (j + 1) * EXPERT_CHUNK)
            hh = _dot(x, w1p_ref[:, hcols]) + b1_ref[0, :, hcols]
            acts = []
            for c in range(2 * EXPERT_CHUNK // GLU_TILE):
                x_glu = jnp.minimum(hh[:, c * GLU_TILE:c * GLU_TILE + LANES], SWIGLU_LIMIT)
                x_lin = jnp.clip(hh[:, c * GLU_TILE + LANES:(c + 1) * GLU_TILE], -SWIGLU_LIMIT, SWIGLU_LIMIT)
                acts.append((x_glu * _sigmoid(SWIGLU_ALPHA * x_glu) * (x_lin + 1.0)).astype(BF16))
            out = out + _dot(jnp.concatenate(acts, axis=1), w2b_ref[j * EXPERT_CHUNK:(j + 1) * EXPERT_CHUNK, :])
        half = out.shape[1] // 2
        ys_ref[...] = _pack_bf16_pair(out[:, :half], out[:, half:])

    @pl.when(t >= nu_ref[0])
    def _():
        ys_ref[...] = jnp.zeros_like(ys_ref)


def _experts(xs, tile_expert, n_used, slot, nxt, w1_all, b1, w2_all, b2, layer):
    n_rows, dw = xs.shape
    tm = EXPERT_TILE
    _, n_exp, d, de2 = w1_all.shape
    src = np.arange(GLU_TILE)
    dst = np.where(src % 2 == 0, src // 2, LANES + src // 2)
    perm = jnp.asarray(dst[:, None] == np.arange(GLU_TILE)[None, :], BF16)
    b1p = b1.reshape(n_exp, de2 // GLU_TILE, LANES, 2).transpose(0, 1, 3, 2).reshape(n_exp, 1, de2)
    row = lambda t, te, nu, sl, nx: (jnp.maximum(jnp.minimum(t, nu[0] - 1), 0), 0)
    exp3 = lambda t, te, nu, sl, nx: (te[t], 0, 0)
    grid_spec = pltpu.PrefetchScalarGridSpec(
        num_scalar_prefetch=4,
        grid=(n_rows // tm,),
        in_specs=[
            pl.BlockSpec((tm, dw), row),
            pl.BlockSpec((1, 1, de2), exp3),
            pl.BlockSpec((1, 1, d), exp3),
            pl.BlockSpec((GLU_TILE, GLU_TILE), lambda t, te, nu, sl, nx: (0, 0)),
            pl.BlockSpec(memory_space=pl.ANY),
            pl.BlockSpec(memory_space=pl.ANY),
        ],
        out_specs=pl.BlockSpec((tm, d // 2), lambda t, te, nu, sl, nx: (t, 0)),
        scratch_shapes=[pltpu.VMEM((2, d, de2), F32), pltpu.VMEM((2, de2 // 2, d), F32),
                        pltpu.VMEM((d, de2), BF16), pltpu.VMEM((de2 // 2, d), BF16),
                        pltpu.SemaphoreType.DMA((2, 2))],
    )
    return pl.pallas_call(
        functools.partial(_experts_kernel, layer=layer),
        grid_spec=grid_spec,
        out_shape=jax.ShapeDtypeStruct((n_rows, d // 2), jnp.uint32),
        compiler_params=_params("arbitrary"),
        name="moe_experts",
    )(tile_expert, n_used, slot, nxt, xs, b1p, b2.reshape(n_exp, 1, d), perm, w1_all, w2_all)


SC_WINDOW = 128


def _sc_worker_rows(n_rows):
    sc = plsc.get_sparse_core_info()
    per_worker = n_rows // (sc.num_cores * sc.num_subcores)
    worker = lax.axis_index("subcore") * sc.num_cores + lax.axis_index("core")
    return worker * per_worker, per_worker


def _sc_mesh():
    return plsc.VectorSubcoreMesh(core_axis_name="core", subcore_axis_name="subcore")


def _sc_gather_rows(table, idx):
    n_idx, width = idx.shape[0], table.shape[1]

    @functools.partial(
        pl.kernel, out_type=jax.ShapeDtypeStruct((n_idx, width), table.dtype), mesh=_sc_mesh(),
        scratch_types=[pltpu.VMEM((SC_WINDOW,), jnp.int32), pltpu.VMEM((SC_WINDOW, width), table.dtype)],
        name="sc_gather_rows")
    def gather(table_hbm, idx_hbm, out_hbm, idx_v, rows_v):
        first, count = _sc_worker_rows(n_idx)

        @pl.loop(0, count // SC_WINDOW)
        def _(c):
            rows = pl.ds(first + c * SC_WINDOW, SC_WINDOW)
            pltpu.sync_copy(idx_hbm.at[rows], idx_v)
            pltpu.sync_copy(table_hbm.at[idx_v], rows_v)
            pltpu.sync_copy(rows_v, out_hbm.at[rows])

    return gather(table, idx)


def _sc_scatter_rows(rows, pos, n_out):
    n, width = rows.shape
    k = pos.shape[0]

    @functools.partial(
        pl.kernel, out_type=jax.ShapeDtypeStruct((n_out, width), rows.dtype), mesh=_sc_mesh(),
        scratch_types=[pltpu.VMEM((k, SC_WINDOW), jnp.int32), pltpu.VMEM((SC_WINDOW, width), rows.dtype)],
        name="sc_scatter_rows")
    def scatter(rows_hbm, pos_hbm, out_hbm, pos_v, rows_v):
        first, count = _sc_worker_rows(n)

        @pl.loop(0, count // SC_WINDOW)
        def _(c):
            src = pl.ds(first + c * SC_WINDOW, SC_WINDOW)
            pltpu.sync_copy(pos_hbm.at[:, src], pos_v)
            pltpu.sync_copy(rows_hbm.at[src], rows_v)
            for j in range(k):
                pltpu.sync_copy(rows_v, out_hbm.at[pos_v.at[j]])

    return scatter(rows, pos)


def _combine_dense_kernel(x_ref, gate_ref, w_ref, fg_ref, *rest, final_norm):
    y_refs, o_ref = rest[:TOP_K], rest[TOP_K]
    w = w_ref[...]
    y_lo = y_hi = None
    for j in range(TOP_K):
        lo, hi = _unpack_bf16_pair(y_refs[j][...])
        wj = w[:, j:j + 1]
        y_lo = wj * lo if j == 0 else y_lo + wj * lo
        y_hi = wj * hi if j == 0 else y_hi + wj * hi
    out = x_ref[...] + gate_ref[0] * jnp.concatenate([y_lo, y_hi], axis=1)
    if final_norm:
        out = out * lax.rsqrt(jnp.mean(out * out, axis=-1, keepdims=True) + NORM_EPS) * fg_ref[...]
    o_ref[...] = out


def _combine_dense(x2, mod_l, gate_idx, seq, top_w, y4, final_g, final_norm, tm=512):
    n, d = x2.shape
    per_batch = seq // tm
    blocks = n // tm
    slot_spec = lambda j: pl.BlockSpec((tm, d // 2), lambda i: (j * blocks + i, 0))
    return pl.pallas_call(
        functools.partial(_combine_dense_kernel, final_norm=final_norm),
        grid=(blocks,),
        in_specs=[
            pl.BlockSpec((tm, d), lambda i: (i, 0)),
            pl.BlockSpec((1, 1, d), lambda i: (i // per_batch, 0, gate_idx)),
            pl.BlockSpec((tm, LANES), lambda i: (i, 0)),
            pl.BlockSpec((1, d), lambda i: (0, 0)),
        ] + [slot_spec(j) for j in range(TOP_K)],
        out_specs=pl.BlockSpec((tm, d), lambda i: (i, 0)),
        out_shape=jax.ShapeDtypeStruct((n, d), F32),
        compiler_params=_params("parallel"),
        name="moe_combine",
    )(x2, mod_l, top_w, final_g.reshape(1, d), *([y4] * TOP_K))


def _moe_ffn(x2, norm_g, mod_l, seq, router_w, router_b, w1_all, b1, w2_all, b2, layer, final_g, final_norm):
    n, d = x2.shape
    n_exp = router_w.shape[1]
    tile = EXPERT_TILE
    hp, idx_f, top_w, rank_f, counts = _router(x2, norm_g, mod_l, 3, seq, router_w, router_b)
    counts = counts[0, :n_exp].astype(jnp.int32)
    padded = (counts + tile - 1) // tile * tile
    ends = jnp.cumsum(padded)
    starts = ends - padded
    idx = idx_f[:, :TOP_K].astype(jnp.int32)
    expert_ids = jnp.arange(n_exp, dtype=jnp.int32)
    start_of = jnp.sum(jnp.where(idx[..., None] == expert_ids, starts, 0), axis=-1)
    pos_flat = (start_of + rank_f[:, :TOP_K].astype(jnp.int32)).reshape(-1)
    n_rows = n * TOP_K + n_exp * tile
    n_tiles = n_rows // tile
    tile_start = jnp.arange(n_tiles, dtype=jnp.int32) * tile
    tile_expert = jnp.sum((ends[None, :] <= tile_start[:, None]).astype(jnp.int32), axis=1)
    tile_expert = jnp.minimum(tile_expert, n_exp - 1)
    n_used = (ends[-1:] // tile).astype(jnp.int32)
    nonempty = counts > 0
    ordinal = jnp.cumsum(nonempty.astype(jnp.int32)) - 1
    later = nonempty[None, :] & (expert_ids[None, :] > expert_ids[:, None])
    nxt_e = jnp.min(jnp.where(later, expert_ids[None, :], n_exp), axis=1)
    nxt_e = jnp.where(nxt_e == n_exp, -1, nxt_e)
    slot = (ordinal[tile_expert] % 2).astype(jnp.int32)
    nxt = nxt_e[tile_expert].astype(jnp.int32)
    pos_slot_major = pos_flat.reshape(n, TOP_K).T
    xs = _sc_scatter_rows(hp, pos_slot_major, n_rows)
    ys = _experts(xs, tile_expert, n_used, slot, nxt, w1_all, b1, w2_all, b2, layer)
    y4 = _sc_gather_rows(ys, pos_slot_major.reshape(-1))
    return _combine_dense(x2, mod_l, 5, seq, top_w, y4, final_g, final_norm)


def kernel(x, c, ada_w, ada_b, norm_mix_g, norm_ffn_g, router_w, router_b, moe_w1, moe_b1, moe_w2, moe_b2, ab_w_in, ab_w_out, hgrn_lb, hgrn_norm_g, rwkv_mu, rwkv_w0, rwkv_w2, rwkv_a0, rwkv_a2, rwkv_g2, rwkv_k_k, rwkv_k_a, rwkv_r_k, rwkv_ln_g, rwkv_ln_b, attn_w_in, attn_w_out, attn_lambda, attn_subln_g, rel_bias_table, final_norm_g):
    batch, seq, d = x.shape
    n = batch * seq
    depth = ada_w.shape[0]
    x2 = x.reshape(n, d)
    mod = _adaln(c, ada_w, ada_b)
    lb_all = jnp.cumsum(jax.nn.softmax(hgrn_lb.astype(F32), axis=1), axis=1)
    for layer in range(depth):
        mod_l = mod[layer].reshape(batch, 1, 6 * d)
        j = layer // 2
        if layer % 2 == 0:
            a_cols = 5 * (d // 2)
            w_in = ab_w_in[j].astype(BF16)
            p_a, p_b = _normmod_proj(x2, norm_mix_g[layer], mod_l, 0, seq,
                                     [w_in[:, :a_cols], w_in[:, a_cols:]], [F32, F32])
            y_a = _hgrn2(p_a, lb_all[:, j], hgrn_norm_g[j], batch, seq)
            y_b = _rwkv7(p_b, rwkv_mu[j], rwkv_w0[j], rwkv_w2[j], rwkv_a0[j], rwkv_a2[j], rwkv_g2[j],
                         rwkv_k_k[j], rwkv_k_a[j], rwkv_r_k[j], rwkv_ln_g[j], rwkv_ln_b[j], batch, seq)
            w_out = ab_w_out[j].astype(BF16)
            x2 = _outproj_residual(x2, mod_l, 2, seq, [y_a, y_b], [w_out[:d // 2], w_out[d // 2:]])
        else:
            (qkv,) = _normmod_proj(x2, norm_mix_g[layer], mod_l, 0, seq, [attn_w_in[j].astype(BF16)], [BF16])
            lam = attn_lambda[j].astype(F32)
            lam_init = 0.8 - 0.6 * math.exp(-0.3 * layer)
            lam_full = jnp.exp(jnp.sum(lam[0] * lam[1])) - jnp.exp(jnp.sum(lam[2] * lam[3])) + lam_init
            o = _diff_attention(qkv, rel_bias_table, lam_full, attn_subln_g[j], lam_init, batch, seq)
            x2 = _outproj_residual(x2, mod_l, 2, seq, [o], [attn_w_out[j].astype(BF16)])
        x2 = _moe_ffn(x2, norm_ffn_g[layer], mod_l, seq, router_w[layer], router_b[layer], moe_w1,
                      moe_b1[layer], moe_w2, moe_b2[layer], layer, final_norm_g, layer == depth - 1)
    return x2.reshape(batch, seq, d)
```

```python
import functools
import math

import jax
import jax.numpy as jnp
import numpy as np
from jax import lax
from jax.experimental import pallas as pl
from jax.experimental.pallas import tpu as pltpu
from jax.experimental.pallas import tpu_sc as plsc

F32 = jnp.float32
BF16 = jnp.bfloat16
HIGHEST = lax.Precision.HIGHEST

NORM_EPS = 1e-6

A_HEAD_DIM = 128
A_CHUNK = 32
A_GROUP = 128
A_HEADS_PER_STEP = 2
B_HEAD_DIM = 64
B_CHUNK = 64
B_GROUP = 8
B_LN_EPS = 1e-5 * B_HEAD_DIM
C_HEAD_DIM = 64
REL_BUCKETS = 32
REL_MAX_DISTANCE = 128
TOP_K = 4
SWIGLU_LIMIT = 7.0
SWIGLU_ALPHA = 1.702

V7X_VMEM_BYTES = 64 * 1024 * 1024
VMEM_LIMIT = V7X_VMEM_BYTES - 8 * 1024 * 1024
LANES = 128


def _params(*sem):
    return pltpu.CompilerParams(dimension_semantics=sem, vmem_limit_bytes=VMEM_LIMIT)


def _sigmoid(x):
    return 1.0 / (1.0 + jnp.exp(-x))


def _silu(x):
    return x * _sigmoid(x)


def _dot(a, b):
    return jnp.dot(a, b, preferred_element_type=F32)


def _dot_nt(a, b):
    return lax.dot_general(a, b, (((1,), (1,)), ((), ())), preferred_element_type=F32)


def _dot_tn(a, b):
    return lax.dot_general(a, b, (((0,), (0,)), ((), ())), preferred_element_type=F32)


def _adaln_kernel(c_ref, w_ref, b_ref, o_ref):
    cond = _silu(c_ref[...])
    o_ref[0] = jnp.dot(cond, w_ref[0], precision=HIGHEST, preferred_element_type=F32) + b_ref[0]


def _adaln(c, ada_w, ada_b):
    n_layers, d, n_out = ada_w.shape
    batch = c.shape[0]
    tn = 1536
    return pl.pallas_call(
        _adaln_kernel,
        grid=(n_layers, n_out // tn),
        in_specs=[
            pl.BlockSpec((batch, d), lambda l, j: (0, 0)),
            pl.BlockSpec((1, d, tn), lambda l, j: (l, 0, j)),
            pl.BlockSpec((1, 1, tn), lambda l, j: (l, 0, j)),
        ],
        out_specs=pl.BlockSpec((1, batch, tn), lambda l, j: (l, 0, j)),
        out_shape=jax.ShapeDtypeStruct((n_layers, batch, n_out), F32),
        compiler_params=_params("parallel", "parallel"),
        name="adaln",
    )(c, ada_w, ada_b.reshape(n_layers, 1, n_out))


def _modulated_norm(x, g, shift, scale):
    y = x * lax.rsqrt(jnp.mean(x * x, axis=-1, keepdims=True) + NORM_EPS)
    return (y * g) * (1.0 + scale) + shift


def _normmod_proj_kernel(x_ref, g_ref, sh_ref, sc_ref, *rest, n_w):
    w_refs, o_refs = rest[:n_w], rest[n_w:]
    h = _modulated_norm(x_ref[...], g_ref[...], sh_ref[0], sc_ref[0]).astype(BF16)
    for w_ref, o_ref in zip(w_refs, o_refs):
        o_ref[...] = _dot(h, w_ref[...]).astype(o_ref.dtype)


def _normmod_proj(x2, g, mod_l, shift_idx, seq, weights, out_dtypes, tm=512):
    n, d = x2.shape
    per_batch = seq // tm
    in_specs = [
        pl.BlockSpec((tm, d), lambda i: (i, 0)),
        pl.BlockSpec((1, d), lambda i: (0, 0)),
        pl.BlockSpec((1, 1, d), lambda i: (i // per_batch, 0, shift_idx)),
        pl.BlockSpec((1, 1, d), lambda i: (i // per_batch, 0, shift_idx + 1)),
    ]
    in_specs += [pl.BlockSpec(w.shape, lambda i: (0, 0)) for w in weights]
    out_specs = [pl.BlockSpec((tm, w.shape[1]), lambda i: (i, 0)) for w in weights]
    out_shape = [jax.ShapeDtypeStruct((n, w.shape[1]), dt) for w, dt in zip(weights, out_dtypes)]
    return pl.pallas_call(
        functools.partial(_normmod_proj_kernel, n_w=len(weights)),
        grid=(n // tm,),
        in_specs=in_specs,
        out_specs=out_specs,
        out_shape=out_shape,
        compiler_params=_params("parallel"),
        name="normmod_proj",
    )(x2, g.reshape(1, d), mod_l, mod_l, *weights)


def _outproj_kernel(x_ref, gate_ref, *rest, n_y):
    y_refs, w_refs, o_ref = rest[:n_y], rest[n_y:2 * n_y], rest[2 * n_y]
    acc = _dot(y_refs[0][...], w_refs[0][...])
    for y_ref, w_ref in zip(y_refs[1:], w_refs[1:]):
        acc += _dot(y_ref[...], w_ref[...])
    o_ref[...] = x_ref[...] + gate_ref[0] * acc


def _outproj_residual(x2, mod_l, gate_idx, seq, ys, ws, tm=512):
    n, d = x2.shape
    per_batch = seq // tm
    in_specs = [
        pl.BlockSpec((tm, d), lambda i: (i, 0)),
        pl.BlockSpec((1, 1, d), lambda i: (i // per_batch, 0, gate_idx)),
    ]
    in_specs += [pl.BlockSpec((tm, y.shape[1]), lambda i: (i, 0)) for y in ys]
    in_specs += [pl.BlockSpec(w.shape, lambda i: (0, 0)) for w in ws]
    return pl.pallas_call(
        functools.partial(_outproj_kernel, n_y=len(ys)),
        grid=(n // tm,),
        in_specs=in_specs,
        out_specs=pl.BlockSpec((tm, d), lambda i: (i, 0)),
        out_shape=jax.ShapeDtypeStruct((n, d), F32),
        compiler_params=_params("parallel"),
        name="outproj_residual",
    )(x2, mod_l, *ys, *ws)


def _chunk_cumsum(x, chunk, reverse):
    rows = x.shape[0]
    pos = lax.broadcasted_iota(jnp.int32, x.shape, 0) % chunk
    s = 1
    while s < chunk:
        if reverse:
            x = x + jnp.where(pos < chunk - s, pltpu.roll(x, rows - s, axis=0), 0.0)
        else:
            x = x + jnp.where(pos >= s, pltpu.roll(x, s, axis=0), 0.0)
        s *= 2
    return x


def _hgrn_groups(slabs):
    g_rows, dk = slabs[0][0].shape
    n_chunks = g_rows // A_CHUNK
    ti = lax.broadcasted_iota(jnp.int32, (A_CHUNK, A_CHUNK), 0)
    si = lax.broadcasted_iota(jnp.int32, (A_CHUNK, A_CHUNK), 1)
    chunk_rows = [slice(c * A_CHUNK, (c + 1) * A_CHUNK) for c in range(n_chunks)]
    prep = []
    for q, f, v, lb, st, reverse in slabs:
        fg = lb + (1.0 - lb) * _sigmoid(f)
        k = 1.0 - fg
        b = _chunk_cumsum(jnp.log(fg), A_CHUNK, reverse)
        b3 = b.reshape(n_chunks, A_CHUNK, dk)
        edge = b3[:, 0:1, :] if reverse else b3[:, A_CHUNK - 1:A_CHUNK, :]
        prep.append(dict(
            q_in=(q * jnp.exp(b)).astype(BF16), k_in=(k * jnp.exp(-b)).astype(BF16),
            k_st=(k.reshape(n_chunks, A_CHUNK, dk) * jnp.exp(edge - b3)).astype(BF16),
            decay=jnp.exp(edge), vb=v.astype(BF16), st=st, reverse=reverse,
            mask=(si >= ti) if reverse else (si <= ti)))
    for p in prep:
        p["scores"] = [_dot_nt(p["q_in"][r], p["k_in"][r]) for r in chunk_rows]
        p["dstate"] = [_dot_tn(p["vb"][r], p["k_st"][c]) for c, r in enumerate(chunk_rows)]
    for p in prep:
        p["intra"] = [_dot(jnp.where(p["mask"], s, 0.0).astype(BF16), p["vb"][r])
                      for s, r in zip(p["scores"], chunk_rows)]
    results = []
    for p in prep:
        st = p["st"]
        outs = [None] * n_chunks
        for c in (range(n_chunks - 1, -1, -1) if p["reverse"] else range(n_chunks)):
            outs[c] = p["intra"][c] + _dot_nt(p["q_in"][chunk_rows[c]], st.astype(BF16))
            st = st * p["decay"][c] + p["dstate"][c]
        results.append((jnp.concatenate(outs, axis=0), st))
    return results


def _hgrn_kernel(q_ref, ff_ref, fb_ref, i_ref, g_ref, lb_ref, ng_ref, o_ref, of_ref, ob_ref):
    seq = q_ref.shape[0]
    dk = A_HEAD_DIM
    n_heads = q_ref.shape[1] // dk
    n_groups = seq // A_GROUP
    head_cols = [slice(h * dk, (h + 1) * dk) for h in range(n_heads)]

    def body(j, states):
        rf = pl.ds(pl.multiple_of(j * A_GROUP, A_GROUP), A_GROUP)
        rb = pl.ds(pl.multiple_of((n_groups - 1 - j) * A_GROUP, A_GROUP), A_GROUP)
        slabs = []
        for h, cols in enumerate(head_cols):
            slabs.append((_silu(q_ref[rf, cols]), ff_ref[rf, cols], i_ref[rf, cols], lb_ref[0:1, cols],
                          states[2 * h], False))
            slabs.append((_silu(q_ref[rb, cols]), fb_ref[rb, cols], i_ref[rb, cols], lb_ref[1:2, cols],
                          states[2 * h + 1], True))
        results = _hgrn_groups(slabs)
        for h, cols in enumerate(head_cols):
            of_ref[rf, cols] = results[2 * h][0]
            ob_ref[rb, cols] = results[2 * h + 1][0]
        return tuple(r[1] for r in results)

    zero = jnp.zeros((dk, dk), F32)
    lax.fori_loop(0, n_groups, body, (zero,) * (2 * n_heads))

    def finish(j, carry):
        r = pl.ds(pl.multiple_of(j * A_GROUP, A_GROUP), A_GROUP)
        for cols in head_cols:
            o = of_ref[r, cols] + ob_ref[r, cols]
            o = o * lax.rsqrt(jnp.mean(o * o, axis=-1, keepdims=True) + NORM_EPS) * ng_ref[...]
            o_ref[r, cols] = (o * _silu(g_ref[r, cols])).astype(o_ref.dtype)
        return carry

    lax.fori_loop(0, n_groups, finish, 0)


def _hgrn2(p_a, lb, norm_g, batch, seq):
    n = p_a.shape[0]
    width = p_a.shape[1] // 5
    cols = A_HEADS_PER_STEP * A_HEAD_DIM
    steps = width // cols
    sect = lambda s: pl.BlockSpec((seq, cols), lambda b, h, s=s: (b, s * steps + h))
    return pl.pallas_call(
        _hgrn_kernel,
        grid=(batch, steps),
        in_specs=[sect(0), sect(1), sect(2), sect(3), sect(4),
                  pl.BlockSpec((2, cols), lambda b, h: (0, h)),
                  pl.BlockSpec((1, A_HEAD_DIM), lambda b, h: (0, 0))],
        out_specs=pl.BlockSpec((seq, cols), lambda b, h: (b, h)),
        out_shape=jax.ShapeDtypeStruct((n, width), BF16),
        scratch_shapes=[pltpu.VMEM((seq, cols), F32), pltpu.VMEM((seq, cols), F32)],
        compiler_params=_params("parallel", "parallel"),
        name="hgrn2",
    )(p_a, p_a, p_a, p_a, p_a, lb, norm_g.reshape(1, A_HEAD_DIM))


def _softplus(z):
    return jnp.maximum(z, 0.0) + jnp.log(1.0 + jnp.exp(-jnp.abs(z)))


def _rwkv_prep_kernel(p_ref, prev_ref, next_ref, mu_ref, w0_ref, w2_ref, a0_ref, a2_ref, g2_ref,
                      kk_ref, ka_ref, rk_ref, ones_ref, tri_ref,
                      r_out, k_out, v_out, kkn_out, b_out, cumf_out, cumb_out, g_out, bonus_out):
    i = pl.program_id(1)
    last = pl.num_programs(1) - 1
    ts = p_ref.shape[0]
    width = a0_ref.shape[1]
    heads = width // B_HEAD_DIM
    p = p_ref[...]
    prev_row = jnp.where(i == 0, 0.0, prev_ref[7:8, :])
    next_row = jnp.where(i == last, 0.0, next_ref[0:1, :])
    row = lax.broadcasted_iota(jnp.int32, (ts, 1), 0)
    up = jnp.where(row == 0, prev_row, pltpu.roll(p, 1, axis=0))
    dn = jnp.where(row == ts - 1, next_row, pltpu.roll(p, ts - 1, axis=0))
    p = p + mu_ref[...] * (0.5 * (up + dn) - p)

    r = p[:, 0:width]
    k = p[:, width:2 * width]
    v = p[:, 2 * width:3 * width]
    o = 3 * width
    wlo_f = p[:, o:o + 64]
    wlo_b = p[:, o + 64:o + 128]
    alo = p[:, o + 128:o + 192]
    glo = p[:, o + 192:o + 320]

    def log_decay(wlo, d):
        z = w0_ref[d:d + 1, :] + _dot(jnp.tanh(wlo).astype(BF16), w2_ref[d])
        w = -_softplus(-z) - 0.5
        return -jnp.exp(w)

    def split_dot(sel, t, right):
        hi = t.astype(BF16)
        lo = (t - hi.astype(F32)).astype(BF16)
        return (_dot(hi, sel) + _dot(lo, sel)) if right else (_dot(sel, hi) + _dot(sel, lo))

    cum_f = split_dot(tri_ref[0], log_decay(wlo_f, 0), False)
    cum_b = split_dot(tri_ref[1], log_decay(wlo_b, 1), False)
    a = _sigmoid(a0_ref[...] + _dot(alo.astype(BF16), a2_ref[...]))
    g = _dot(_sigmoid(glo).astype(BF16), g2_ref[...])
    kk = k * kk_ref[...]
    head_sum = lambda t: split_dot(ones_ref[...], t, True)

    kk_n = kk / jnp.maximum(jnp.sqrt(head_sum(kk * kk)), 1e-12)
    k_mod = k * (1.0 + (a - 1.0) * ka_ref[...])
    bonus = head_sum(r * k_mod * rk_ref[...]) * v
    g_out[...] = g
    bonus_out[...] = bonus
    for h in range(heads):
        sl = slice(h * B_HEAD_DIM, (h + 1) * B_HEAD_DIM)
        r_out[0, h] = r[:, sl]
        k_out[0, h] = k_mod[:, sl]
        v_out[0, h] = v[:, sl]
        kkn_out[0, h] = kk_n[:, sl]
        b_out[0, h] = (kk_n * a)[:, sl]
        cumf_out[0, h] = cum_f[:, sl]
        cumb_out[0, h] = cum_b[:, sl]


def _rwkv_prep(p_b, mu, w0, w2, a0, a2, g2, k_k, k_a, r_k, batch, seq, ts=256):
    n, cols = p_b.shape
    width = a0.shape[0]
    heads = width // B_HEAD_DIM
    nblk = seq // ts
    rows8 = ts // 8
    head_id = np.arange(width) // B_HEAD_DIM
    ones = jnp.asarray(head_id[:, None] == head_id[None, :], BF16)
    full = lambda a: pl.BlockSpec(a.shape, lambda b, i: (0,) * a.ndim)
    vec = lambda a: a.reshape(1, -1)
    row = np.arange(ts)
    same_chunk = (row[:, None] // B_CHUNK) == (row[None, :] // B_CHUNK)
    tri = jnp.asarray(np.stack([same_chunk & (row[None, :] <= row[:, None]),
                                same_chunk & (row[None, :] >= row[:, None])]), BF16)
    args = [vec(mu), w0, w2.astype(BF16), vec(a0), a2.astype(BF16), g2.astype(BF16),
            vec(k_k), vec(k_a), vec(r_k), ones, tri]
    hm = pl.BlockSpec((1, heads, ts, B_HEAD_DIM), lambda b, i: (b, 0, i, 0))
    tokm = pl.BlockSpec((ts, width), lambda b, i: (b * nblk + i, 0))
    hm_shape = jax.ShapeDtypeStruct((batch, heads, seq, B_HEAD_DIM), F32)
    tok_shape = jax.ShapeDtypeStruct((n, width), F32)
    return pl.pallas_call(
        _rwkv_prep_kernel,
        grid=(batch, nblk),
        in_specs=[
            pl.BlockSpec((ts, cols), lambda b, i: (b * nblk + i, 0)),
            pl.BlockSpec((8, cols), lambda b, i: (jnp.maximum((b * nblk + i) * rows8 - 1, 0), 0)),
            pl.BlockSpec((8, cols), lambda b, i: (jnp.minimum((b * nblk + i + 1) * rows8, n // 8 - 1), 0)),
        ] + [full(a) for a in args],
        out_specs=[hm] * 7 + [tokm] * 2,
        out_shape=[hm_shape] * 7 + [tok_shape] * 2,
        compiler_params=_params("parallel", "parallel"),
        name="rwkv_prep",
    )(p_b, p_b, p_b, *args)


def _rwkv_chunks(chains):
    c, d = chains[0][0].shape
    ti = lax.broadcasted_iota(jnp.int32, (c, c), 0)
    si = lax.broadcasted_iota(jnp.int32, (c, c), 1)
    eye = jnp.where(ti == si, 1.0, 0.0)
    masks = {rev: (((si > ti), (si >= ti)) if rev else ((si < ti), (si <= ti))) for rev in (False, True)}

    row = lax.broadcasted_iota(jnp.int32, (c, d), 0)
    st = []
    for r, k, v, kk, b, cum, tt, rev in chains:
        before = (jnp.where(row == c - 1, 0.0, pltpu.roll(cum, c - 1, axis=0)) if rev
                  else jnp.where(row == 0, 0.0, pltpu.roll(cum, 1, axis=0)))
        grow = jnp.exp(-cum)
        kt = kk * jnp.exp(before)
        rt = r * jnp.exp(cum)
        b_ = (b * grow).astype(BF16)
        kb = jnp.concatenate([(k * grow).astype(BF16), b_], axis=0)
        st.append(dict(v=v, vb=v.astype(BF16), kt=kt, rt=rt, b_=b_, kb=kb, tt=tt, ttb=tt.astype(BF16),
                       decay=jnp.exp(cum[0:1, :] if rev else cum[c - 1:c, :]), rev=rev,
                       lhs=jnp.concatenate([kt, rt], axis=0).astype(BF16)))
    for s in st:
        s["big"] = _dot_nt(s["lhs"], s["kb"])
    for s in st:
        strict, incl = masks[s["rev"]]
        big = s["big"]
        a_ab = jnp.where(strict, big[:c, c:], 0.0)
        s["a_ak"] = jnp.where(strict, big[:c, :c], 0.0).astype(BF16)
        s["p"] = jnp.concatenate([jnp.where(incl, big[c:, :c], 0.0), jnp.where(incl, -big[c:, c:], 0.0)],
                                 axis=1).astype(BF16)
        s["m"] = eye - a_ab
        s["ab"] = a_ab.astype(BF16)
    for s in st:
        s["aj"] = _dot(s["ab"], s["ab"])
        s["av"] = _dot(s["a_ak"], s["vb"])
    span = 2
    while span < c:
        span *= 2
        for s in st:
            ajb = s["aj"].astype(BF16)
            if span < c:
                both = _dot(jnp.concatenate([s["aj"], s["m"]], axis=0).astype(BF16), ajb)
                s["aj"], s["m"] = both[:c], s["m"] + both[c:]
            else:
                s["m"] = s["m"] + _dot(s["m"].astype(BF16), ajb)
    for s in st:
        s["wu"] = _dot(s["m"].astype(BF16), jnp.concatenate([s["kt"], s["av"]], axis=1).astype(BF16))
    for s in st:
        w_, u0, v = s["wu"][:, :d], s["wu"][:, d:], s["v"]
        rhs = jnp.concatenate([jnp.concatenate([v, jnp.zeros_like(v)], axis=1),
                               jnp.concatenate([u0, w_], axis=1)], axis=0).astype(BF16)
        s["yq"] = _dot(s["p"], rhs)
        s["h_t"] = _dot_tn(jnp.concatenate([v, -u0], axis=0).astype(BF16), s["kb"])
        s["bw"] = _dot_tn(s["b_"], w_.astype(BF16))
    out = []
    for s in st:
        q = s["rt"] + s["yq"][:, d:]
        y = _dot_nt(q.astype(BF16), s["ttb"]) + s["yq"][:, :d]
        tt = (s["tt"] - _dot_nt(s["ttb"], s["bw"].astype(BF16)) + s["h_t"]) * s["decay"]
        out.append((y, tt))
    return out


def _rwkv_scan_kernel(rf, kf, vf, kkf, bf, lwf, rb, kb, vb, kkb, bb, lwb, yf_ref, yb_ref, tf_ref, tb_ref):
    heads, ts = rf.shape[1], rf.shape[2]
    n_chunks = ts // B_CHUNK

    @pl.when(pl.program_id(1) == 0)
    def _():
        tf_ref[...] = jnp.zeros_like(tf_ref)
        tb_ref[...] = jnp.zeros_like(tb_ref)

    def body(j, carry):
        rows_f = pl.ds(pl.multiple_of(j * B_CHUNK, B_CHUNK), B_CHUNK)
        rows_b = pl.ds(pl.multiple_of((n_chunks - 1 - j) * B_CHUNK, B_CHUNK), B_CHUNK)
        for h0 in range(0, heads, B_GROUP):
            group = range(h0, h0 + B_GROUP)
            chains = []
            for h in group:
                chains.append([ref[0, h, rows_f, :] for ref in (rf, kf, vf, kkf, bf, lwf)] + [tf_ref[h], False])
                chains.append([ref[0, h, rows_b, :] for ref in (rb, kb, vb, kkb, bb, lwb)] + [tb_ref[h], True])
            res = _rwkv_chunks(chains)
            for n, h in enumerate(group):
                yf_ref[0, h, rows_f, :], tf_ref[h] = res[2 * n]
                yb_ref[0, h, rows_b, :], tb_ref[h] = res[2 * n + 1]
        return carry

    lax.fori_loop(0, n_chunks, body, 0)


def _rwkv_scan(r, k, v, kkn, b, lw_f, lw_b, ts=256):
    batch, heads, seq, d = r.shape
    nblk = seq // ts
    fwd = pl.BlockSpec((1, heads, ts, d), lambda bi, i: (bi, 0, i, 0))
    bwd = pl.BlockSpec((1, heads, ts, d), lambda bi, i: (bi, 0, nblk - 1 - i, 0))
    shape = jax.ShapeDtypeStruct(r.shape, F32)
    return pl.pallas_call(
        _rwkv_scan_kernel,
        grid=(batch, nblk),
        in_specs=[fwd] * 6 + [bwd] * 6,
        out_specs=[fwd, bwd],
        out_shape=[shape, shape],
        scratch_shapes=[pltpu.VMEM((heads, d, d), F32), pltpu.VMEM((heads, d, d), F32)],
        compiler_params=_params("parallel", "arbitrary"),
        name="rwkv_scan",
    )(r, k, v, kkn, b, lw_f, r, k, v, kkn, b, lw_b)


def _rwkv_post_kernel(yf_ref, yb_ref, g_ref, bonus_ref, lng_ref, lnb_ref, o_ref):
    heads = yf_ref.shape[1]
    outs = []
    for h in range(heads):
        y = yf_ref[0, h] + yb_ref[0, h]
        mean = jnp.mean(y, axis=-1, keepdims=True)
        var = jnp.mean(jnp.square(y - mean), axis=-1, keepdims=True)
        outs.append((y - mean) * lax.rsqrt(var + B_LN_EPS))
    yn = jnp.concatenate(outs, axis=1) * lng_ref[...] + lnb_ref[...]
    o_ref[...] = ((yn + bonus_ref[...]) * g_ref[...]).astype(o_ref.dtype)


def _rwkv_post(y_f, y_b, g, bonus, ln_g, ln_b, ts=256):
    batch, heads, seq, d = y_f.shape
    n, width = g.shape
    nblk = seq // ts
    hm = pl.BlockSpec((1, heads, ts, d), lambda b, i: (b, 0, i, 0))
    tokm = pl.BlockSpec((ts, width), lambda b, i: (b * nblk + i, 0))
    vec = pl.BlockSpec((1, width), lambda b, i: (0, 0))
    return pl.pallas_call(
        _rwkv_post_kernel,
        grid=(batch, nblk),
        in_specs=[hm, hm, tokm, tokm, vec, vec],
        out_specs=tokm,
        out_shape=jax.ShapeDtypeStruct((n, width), BF16),
        compiler_params=_params("parallel", "parallel"),
        name="rwkv_post",
    )(y_f, y_b, g, bonus, ln_g.reshape(1, width), ln_b.reshape(1, width))


def _rwkv7(p_b, mu, w0, w2, a0, a2, g2, k_k, k_a, r_k, ln_g, ln_b, batch, seq):
    r, k, v, kkn, b, lw_f, lw_b, g, bonus = _rwkv_prep(p_b, mu, w0, w2, a0, a2, g2, k_k, k_a, r_k, batch, seq)
    y_f, y_b = _rwkv_scan(r, k, v, kkn, b, lw_f, lw_b)
    return _rwkv_post(y_f, y_b, g, bonus, ln_g, ln_b)


_BUCKET_EDGES = (0, 1, 2, 3, 4, 5, 6, 7, 8, 12, 16, 23, 32, 46, 64, 91)
ATTN_TK = 128
C_HEADS_PER_STEP = 2


def _bias_band_kernel(table_ref, o_ref):
    h = pl.program_id(0)
    n_d, tq, tk = o_ref.shape[1:]
    half = REL_BUCKETS // 2
    r = lax.broadcasted_iota(jnp.int32, (tq, tk), 0)
    c = lax.broadcasted_iota(jnp.int32, (tq, tk), 1)
    for d in range(n_d):
        rel = tk * (d - 1) + c - r
        n = jnp.abs(rel)
        vals = []
        for sign in range(2):
            val = jnp.full((tq, tk), table_ref[sign * half + half - 1, h], F32)
            for bkt in range(half - 2, -1, -1):
                val = jnp.where(n < _BUCKET_EDGES[bkt + 1], table_ref[sign * half + bkt, h], val)
            vals.append(val)
        o_ref[0, d] = jnp.where(rel > 0, vals[1], vals[0])


def _bias_band(rel_table, tq):
    heads = rel_table.shape[1]
    n_d = tq // ATTN_TK + 2
    return pl.pallas_call(
        _bias_band_kernel,
        grid=(heads,),
        in_specs=[pl.BlockSpec(memory_space=pltpu.SMEM)],
        out_specs=pl.BlockSpec((1, n_d, tq, ATTN_TK), lambda h: (h, 0, 0, 0)),
        out_shape=jax.ShapeDtypeStruct((heads, n_d, tq, ATTN_TK), F32),
        compiler_params=_params("parallel"),
        name="bias_band",
    )(rel_table)


def _diff_attn_kernel(table_ref, lam_ref, q_ref, k_ref, v_ref, band_ref, g_ref, o_ref, lg_ref, vext_ref, kt_ref,
                      *, lam_init):
    i = pl.program_id(2)
    tq = q_ref.shape[0]
    dh2 = 2 * C_HEAD_DIM
    n_heads = q_ref.shape[1] // dh2
    seq = k_ref.shape[0]
    n_d = band_ref.shape[1]
    n_kt = seq // ATTN_TK
    ratio = tq // ATTN_TK
    half = REL_BUCKETS // 2
    head_cols = [slice(hh * dh2, (hh + 1) * dh2) for hh in range(n_heads)]

    @pl.when(i == 0)
    def _():
        for hh, cols in enumerate(head_cols):
            kt_ref[hh] = k_ref[:, cols].astype(F32).T.astype(BF16)
            v = v_ref[:, cols]
            vext_ref[hh] = jnp.concatenate([v, jnp.ones_like(v)], axis=1)

    lane = lax.broadcasted_iota(jnp.int32, (1, dh2), 1)
    first_band = ratio * i - 1
    col = lax.broadcasted_iota(jnp.int32, (1, seq), 1)
    for hh, cols in enumerate(head_cols):
        h = pl.program_id(1) * n_heads + hh
        q = q_ref[:, cols] * jnp.asarray(C_HEAD_DIM ** -0.5, q_ref.dtype)
        zero = jnp.zeros_like(q)
        far_row = jnp.where(col < first_band * ATTN_TK, table_ref[half - 1, h],
                            jnp.where(col >= (first_band + n_d) * ATTN_TK, table_ref[REL_BUCKETS - 1, h],
                                      -jnp.inf))
        row_max = []
        for m in range(2):
            qm = jnp.where((lane >= m * C_HEAD_DIM) & (lane < (m + 1) * C_HEAD_DIM), q, zero)
            far = _dot(qm, kt_ref[hh]) + far_row
            mx = jnp.max(far, axis=-1, keepdims=True)
            lg_ref[hh, m] = far
            for d in (0, n_d - 1) + tuple(range(1, n_d - 1)):
                kt = first_band + d
                valid = (kt >= 0) & (kt < n_kt)
                c0 = pl.multiple_of(jnp.clip(kt, 0, n_kt - 1) * ATTN_TK, ATTN_TK)
                near = _dot(qm, kt_ref[hh, :, pl.ds(c0, ATTN_TK)]) + band_ref[hh, d]
                mx = jnp.maximum(mx, jnp.where(valid, jnp.max(near, axis=-1, keepdims=True), -jnp.inf))
                lg_ref[hh, m, :, pl.ds(c0, ATTN_TK)] = near
            row_max.append(mx)
        parts = []
        for m in range(2):
            e = jnp.exp(lg_ref[hh, m] - row_max[m]).astype(BF16)
            pv = _dot(e, vext_ref[hh])
            parts.append(pv[:, :dh2] * (1.0 / pv[:, dh2:dh2 + 1]))
        o = parts[0] - lam_ref[0] * parts[1]
        o = o * lax.rsqrt(jnp.mean(o * o, axis=-1, keepdims=True) + NORM_EPS) * g_ref[...]
        o_ref[:, cols] = (o * (1.0 - lam_init)).astype(o_ref.dtype)


def _diff_attention(qkv, rel_table, lam_full, subln_g, lam_init, batch, seq, tq=256):
    n = qkv.shape[0]
    dh2 = 2 * C_HEAD_DIM
    heads = qkv.shape[1] // (3 * dh2)
    nq = seq // tq
    per = C_HEADS_PER_STEP
    groups = heads // per
    cols = per * dh2
    band = _bias_band(rel_table, tq)
    return pl.pallas_call(
        functools.partial(_diff_attn_kernel, lam_init=lam_init),
        grid=(batch, groups, nq),
        in_specs=[
            pl.BlockSpec(memory_space=pltpu.SMEM),
            pl.BlockSpec(memory_space=pltpu.SMEM),
            pl.BlockSpec((tq, cols), lambda b, h, i: (b * nq + i, h)),
            pl.BlockSpec((seq, cols), lambda b, h, i: (b, groups + h)),
            pl.BlockSpec((seq, cols), lambda b, h, i: (b, 2 * groups + h)),
            pl.BlockSpec((per,) + band.shape[1:], lambda b, h, i: (h, 0, 0, 0)),
            pl.BlockSpec((1, dh2), lambda b, h, i: (0, 0)),
        ],
        out_specs=pl.BlockSpec((tq, cols), lambda b, h, i: (b * nq + i, h)),
        out_shape=jax.ShapeDtypeStruct((n, heads * dh2), BF16),
        scratch_shapes=[pltpu.VMEM((per, 2, tq, seq), F32), pltpu.VMEM((per, seq, 2 * dh2), BF16),
                        pltpu.VMEM((per, dh2, seq), BF16)],
        compiler_params=_params("parallel", "parallel", "arbitrary"),
        name="diff_attention",
    )(rel_table, lam_full.reshape(1), qkv, qkv, qkv, band, subln_g.reshape(1, dh2))


MOE_TILE = 256
EXPERT_TILE = 512
EXPERT_CHUNK = 512
_HI_MASK = np.uint32(0xFFFF0000)


def _pack_bf16_pair(lo, hi):
    as_bits = lambda t: lax.bitcast_convert_type(t.astype(BF16).astype(F32), jnp.uint32)
    return (as_bits(hi) & _HI_MASK) | (as_bits(lo) >> 16)


def _unpack_bf16_pair(word):
    lo = lax.bitcast_convert_type(word << 16, F32).astype(BF16)
    hi = lax.bitcast_convert_type(word & _HI_MASK, F32).astype(BF16)
    return lo, hi


def _router_kernel(x_ref, g_ref, sh_ref, sc_ref, rw_ref, rb_ref, hp_ref, idx_ref, w_ref, rank_ref, cnt_ref,
                   run_ref, *, n_experts):
    @pl.when(pl.program_id(0) == 0)
    def _():
        run_ref[...] = jnp.zeros_like(run_ref)

    tm, d = x_ref.shape
    h = _modulated_norm(x_ref[...], g_ref[...], sh_ref[0], sc_ref[0])
    hp_ref[...] = _pack_bf16_pair(h[:, :d // 2], h[:, d // 2:])
    logits = _dot(h.astype(BF16), rw_ref[...]) + rb_ref[...]
    lane = lax.broadcasted_iota(jnp.int32, logits.shape, 1).astype(F32)
    neg = jnp.float32(-jnp.inf)
    logits = jnp.where(lane < n_experts, logits, neg)
    picks, vals = [], []
    for _ in range(TOP_K):
        m = jnp.max(logits, axis=-1, keepdims=True)
        first = jnp.min(jnp.where(logits == m, lane, float(LANES)), axis=-1, keepdims=True)
        hit = lane == first
        picks.append((first, hit))
        vals.append(m)
        logits = jnp.where(hit, neg, logits)
    es = [jnp.exp(v - vals[0]) for v in vals]
    inv = 1.0 / (es[0] + es[1] + es[2] + es[3])
    assigned = picks[0][1] | picks[1][1] | picks[2][1] | picks[3][1]
    onehot = jnp.where(assigned, 1.0, 0.0)
    ri = lax.broadcasted_iota(jnp.int32, (tm, tm), 0)
    ci = lax.broadcasted_iota(jnp.int32, (tm, tm), 1)
    earlier = jnp.where(ci < ri, 1.0, 0.0).astype(BF16)
    before = _dot(earlier, onehot.astype(BF16)) + run_ref[...]
    idx_o = jnp.zeros(logits.shape, F32)
    w_o = jnp.zeros(logits.shape, F32)
    rank_o = jnp.zeros(logits.shape, F32)
    for j, (first, hit) in enumerate(picks):
        rank_j = jnp.sum(jnp.where(hit, before, 0.0), axis=-1, keepdims=True)
        idx_o = jnp.where(lane == j, first, idx_o)
        w_o = jnp.where(lane == j, es[j] * inv, w_o)
        rank_o = jnp.where(lane == j, rank_j, rank_o)
    idx_ref[...] = idx_o
    w_ref[...] = w_o
    rank_ref[...] = rank_o
    run_ref[...] += jnp.sum(onehot, axis=0, keepdims=True)
    cnt_ref[...] = run_ref[...]


def _router(x2, g, mod_l, shift_idx, seq, router_w, router_b):
    n, d = x2.shape
    tm = MOE_TILE
    n_experts = router_w.shape[1]
    per_batch = seq // tm
    rw = jnp.zeros((d, LANES), BF16).at[:, :n_experts].set(router_w.astype(BF16))
    rb = jnp.zeros((1, LANES), F32).at[0, :n_experts].set(router_b)
    lane_out = pl.BlockSpec((tm, LANES), lambda i: (i, 0))
    lane_shape = jax.ShapeDtypeStruct((n, LANES), F32)
    return pl.pallas_call(
        functools.partial(_router_kernel, n_experts=n_experts),
        grid=(n // tm,),
        in_specs=[
            pl.BlockSpec((tm, d), lambda i: (i, 0)),
            pl.BlockSpec((1, d), lambda i: (0, 0)),
            pl.BlockSpec((1, 1, d), lambda i: (i // per_batch, 0, shift_idx)),
            pl.BlockSpec((1, 1, d), lambda i: (i // per_batch, 0, shift_idx + 1)),
            pl.BlockSpec((d, LANES), lambda i: (0, 0)),
            pl.BlockSpec((1, LANES), lambda i: (0, 0)),
        ],
        out_specs=[pl.BlockSpec((tm, d // 2), lambda i: (i, 0)), lane_out, lane_out, lane_out,
                   pl.BlockSpec((1, LANES), lambda i: (0, 0))],
        out_shape=[jax.ShapeDtypeStruct((n, d // 2), jnp.uint32), lane_shape, lane_shape, lane_shape,
                   jax.ShapeDtypeStruct((1, LANES), F32)],
        scratch_shapes=[pltpu.VMEM((1, LANES), F32)],
        compiler_params=_params("arbitrary"),
        name="moe_router",
    )(x2, g.reshape(1, d), mod_l, mod_l, rw, rb)


GLU_TILE = 2 * LANES


def _experts_kernel(te_ref, nu_ref, slot_ref, nxt_ref, xs_ref, b1_ref, b2_ref, perm_ref, w1_hbm, w2_hbm,
                    ys_ref, w1buf, w2buf, w1p_ref, w2b_ref, sem, *, layer):
    t = pl.program_id(0)
    used = t < nu_ref[0]
    new_expert = (t == 0) | (te_ref[t] != te_ref[jnp.maximum(t - 1, 0)])
    n_col_tiles = w1p_ref.shape[1] // GLU_TILE

    def weight_copies(e, slot):
        return (pltpu.make_async_copy(w1_hbm.at[layer, e], w1buf.at[slot], sem.at[0, slot]),
                pltpu.make_async_copy(w2_hbm.at[layer, e], w2buf.at[slot], sem.at[1, slot]))

    @pl.when(used & new_expert)
    def _():
        slot = slot_ref[t]

        @pl.when(t == 0)
        def _():
            for cp in weight_copies(te_ref[t], slot):
                cp.start()

        for cp in weight_copies(te_ref[t], slot):
            cp.wait()

        @pl.when(nxt_ref[t] >= 0)
        def _():
            for cp in weight_copies(nxt_ref[t], 1 - slot):
                cp.start()

        for c in range(n_col_tiles):
            cols = slice(c * GLU_TILE, (c + 1) * GLU_TILE)
            w1p_ref[:, cols] = _dot(w1buf[slot, :, cols].astype(BF16), perm_ref[...]).astype(BF16)
        w2b_ref[...] = w2buf[slot].astype(BF16)

    @pl.when(used)
    def _():
        lo, hi = _unpack_bf16_pair(xs_ref[...])
        x = jnp.concatenate([lo, hi], axis=1)
        out = b2_ref[0]
        for j in range(n_col_tiles * LANES // EXPERT_CHUNK):
            hcols = slice(2 * j * EXPERT_CHUNK, 2 * (j + 1) * EXPERT_CHUNK)
            hh = _dot(x, w1p_ref[:, hcols]) + b1_ref[0, :, hcols]
            acts = []
            for c in range(2 * EXPERT_CHUNK // GLU_TILE):
                x_glu = jnp.minimum(hh[:, c * GLU_TILE:c * GLU_TILE + LANES], SWIGLU_LIMIT)
                x_lin = jnp.clip(hh[:, c * GLU_TILE + LANES:(c + 1) * GLU_TILE], -SWIGLU_LIMIT, SWIGLU_LIMIT)
                acts.append((x_glu * _sigmoid(SWIGLU_ALPHA * x_glu) * (x_lin + 1.0)).astype(BF16))
            out = out + _dot(jnp.concatenate(acts, axis=1), w2b_ref[j * EXPERT_CHUNK:(j + 1) * EXPERT_CHUNK, :])
        half = out.shape[1] // 2
        ys_ref[...] = _pack_bf16_pair(out[:, :half], out[:, half:])

    @pl.when(t >= nu_ref[0])
    def _():
        ys_ref[...] = jnp.zeros_like(ys_ref)


def _experts(xs, tile_expert, n_used, slot, nxt, w1_all, b1, w2_all, b2, layer):
    n_rows, dw = xs.shape
    tm = EXPERT_TILE
    _, n_exp, d, de2 = w1_all.shape
    src = np.arange(GLU_TILE)
    dst = np.where(src % 2 == 0, src // 2, LANES + src // 2)
    perm = jnp.asarray(dst[:, None] == np.arange(GLU_TILE)[None, :], BF16)
    b1p = b1.reshape(n_exp, de2 // GLU_TILE, LANES, 2).transpose(0, 1, 3, 2).reshape(n_exp, 1, de2)
    row = lambda t, te, nu, sl, nx: (jnp.maximum(jnp.minimum(t, nu[0] - 1), 0), 0)
    exp3 = lambda t, te, nu, sl, nx: (te[t], 0, 0)
    grid_spec = pltpu.PrefetchScalarGridSpec(
        num_scalar_prefetch=4,
        grid=(n_rows // tm,),
        in_specs=[
            pl.BlockSpec((tm, dw), row),
            pl.BlockSpec((1, 1, de2), exp3),
            pl.BlockSpec((1, 1, d), exp3),
            pl.BlockSpec((GLU_TILE, GLU_TILE), lambda t, te, nu, sl, nx: (0, 0)),
            pl.BlockSpec(memory_space=pl.ANY),
            pl.BlockSpec(memory_space=pl.ANY),
        ],
        out_specs=pl.BlockSpec((tm, d // 2), lambda t, te, nu, sl, nx: (t, 0)),
        scratch_shapes=[pltpu.VMEM((2, d, de2), F32), pltpu.VMEM((2, de2 // 2, d), F32),
                        pltpu.VMEM((d, de2), BF16), pltpu.VMEM((de2 // 2, d), BF16),
                        pltpu.SemaphoreType.DMA((2, 2))],
    )
    return pl.pallas_call(
        functools.partial(_experts_kernel, layer=layer),
        grid_spec=grid_spec,
        out_shape=jax.ShapeDtypeStruct((n_rows, d // 2), jnp.uint32),
        compiler_params=_params("arbitrary"),
        name="moe_experts",
    )(tile_expert, n_used, slot, nxt, xs, b1p, b2.reshape(n_exp, 1, d), perm, w1_all, w2_all)


SC_WINDOW = 128


def _sc_worker_rows(n_rows):
    sc = plsc.get_sparse_core_info()
    per_worker = n_rows // (sc.num_cores * sc.num_subcores)
    worker = lax.axis_index("subcore") * sc.num_cores + lax.axis_index("core")
    return worker * per_worker, per_worker


def _sc_mesh():
    return plsc.VectorSubcoreMesh(core_axis_name="core", subcore_axis_name="subcore")


def _sc_gather_rows(table, idx):
    n_idx, width = idx.shape[0], table.shape[1]

    @functools.partial(
        pl.kernel, out_type=jax.ShapeDtypeStruct((n_idx, width), table.dtype), mesh=_sc_mesh(),
        scratch_types=[pltpu.VMEM((SC_WINDOW,), jnp.int32), pltpu.VMEM((SC_WINDOW, width), table.dtype)],
        name="sc_gather_rows")
    def gather(table_hbm, idx_hbm, out_hbm, idx_v, rows_v):
        first, count = _sc_worker_rows(n_idx)

        @pl.loop(0, count // SC_WINDOW)
        def _(c):
            rows = pl.ds(first + c * SC_WINDOW, SC_WINDOW)
            pltpu.sync_copy(idx_hbm.at[rows], idx_v)
            pltpu.sync_copy(table_hbm.at[idx_v], rows_v)
            pltpu.sync_copy(rows_v, out_hbm.at[rows])

    return gather(table, idx)


def _sc_scatter_rows(rows, pos, n_out):
    n, width = rows.shape
    k = pos.shape[0]

    @functools.partial(
        pl.kernel, out_type=jax.ShapeDtypeStruct((n_out, width), rows.dtype), mesh=_sc_mesh(),
        scratch_types=[pltpu.VMEM((k, SC_WINDOW), jnp.int32), pltpu.VMEM((SC_WINDOW, width), rows.dtype)],
        name="sc_scatter_rows")
    def scatter(rows_hbm, pos_hbm, out_hbm, pos_v, rows_v):
        first, count = _sc_worker_rows(n)

        @pl.loop(0, count // SC_WINDOW)
        def _(c):
            src = pl.ds(first + c * SC_WINDOW, SC_WINDOW)
            pltpu.sync_copy(pos_hbm.at[:, src], pos_v)
            pltpu.sync_copy(rows_hbm.at[src], rows_v)
            for j in range(k):
                pltpu.sync_copy(rows_v, out_hbm.at[pos_v.at[j]])

    return scatter(rows, pos)


def _combine_dense_kernel(x_ref, gate_ref, w_ref, fg_ref, *rest, final_norm):
    y_refs, o_ref = rest[:TOP_K], rest[TOP_K]
    w = w_ref[...]
    y_lo = y_hi = None
    for j in range(TOP_K):
        lo, hi = _unpack_bf16_pair(y_refs[j][...])
        wj = w[:, j:j + 1]
        y_lo = wj * lo if j == 0 else y_lo + wj * lo
        y_hi = wj * hi if j == 0 else y_hi + wj * hi
    out = x_ref[...] + gate_ref[0] * jnp.concatenate([y_lo, y_hi], axis=1)
    if final_norm:
        out = out * lax.rsqrt(jnp.mean(out * out, axis=-1, keepdims=True) + NORM_EPS) * fg_ref[...]
    o_ref[...] = out


def _combine_dense(x2, mod_l, gate_idx, seq, top_w, y4, final_g, final_norm, tm=512):
    n, d = x2.shape
    per_batch = seq // tm
    blocks = n // tm
    slot_spec = lambda j: pl.BlockSpec((tm, d // 2), lambda i: (j * blocks + i, 0))
    return pl.pallas_call(
        functools.partial(_combine_dense_kernel, final_norm=final_norm),
        grid=(blocks,),
        in_specs=[
            pl.BlockSpec((tm, d), lambda i: (i, 0)),
            pl.BlockSpec((1, 1, d), lambda i: (i // per_batch, 0, gate_idx)),
            pl.BlockSpec((tm, LANES), lambda i: (i, 0)),
            pl.BlockSpec((1, d), lambda i: (0, 0)),
        ] + [slot_spec(j) for j in range(TOP_K)],
        out_specs=pl.BlockSpec((tm, d), lambda i: (i, 0)),
        out_shape=jax.ShapeDtypeStruct((n, d), F32),
        compiler_params=_params("parallel"),
        name="moe_combine",
    )(x2, mod_l, top_w, final_g.reshape(1, d), *([y4] * TOP_K))


def _moe_ffn(x2, norm_g, mod_l, seq, router_w, router_b, w1_all, b1, w2_all, b2, layer, final_g, final_norm):
    n, d = x2.shape
    n_exp = router_w.shape[1]
    tile = EXPERT_TILE
    hp, idx_f, top_w, rank_f, counts = _router(x2, norm_g, mod_l, 3, seq, router_w, router_b)
    counts = counts[0, :n_exp].astype(jnp.int32)
    padded = (counts + tile - 1) // tile * tile
    ends = jnp.cumsum(padded)
    starts = ends - padded
    idx = idx_f[:, :TOP_K].astype(jnp.int32)
    expert_ids = jnp.arange(n_exp, dtype=jnp.int32)
    start_of = jnp.sum(jnp.where(idx[..., None] == expert_ids, starts, 0), axis=-1)
    pos_flat = (start_of + rank_f[:, :TOP_K].astype(jnp.int32)).reshape(-1)
    n_rows = n * TOP_K + n_exp * tile
    n_tiles = n_rows // tile
    tile_start = jnp.arange(n_tiles, dtype=jnp.int32) * tile
    tile_expert = jnp.sum((ends[None, :] <= tile_start[:, None]).astype(jnp.int32), axis=1)
    tile_expert = jnp.minimum(tile_expert, n_exp - 1)
    n_used = (ends[-1:] // tile).astype(jnp.int32)
    nonempty = counts > 0
    ordinal = jnp.cumsum(nonempty.astype(jnp.int32)) - 1
    later = nonempty[None, :] & (expert_ids[None, :] > expert_ids[:, None])
    nxt_e = jnp.min(jnp.where(later, expert_ids[None, :], n_exp), axis=1)
    nxt_e = jnp.where(nxt_e == n_exp, -1, nxt_e)
    slot = (ordinal[tile_expert] % 2).astype(jnp.int32)
    nxt = nxt_e[tile_expert].astype(jnp.int32)
    pos_slot_major = pos_flat.reshape(n, TOP_K).T
    xs = _sc_scatter_rows(hp, pos_slot_major, n_rows)
    ys = _experts(xs, tile_expert, n_used, slot, nxt, w1_all, b1, w2_all, b2, layer)
    y4 = _sc_gather_rows(ys, pos_slot_major.reshape(-1))
    return _combine_dense(x2, mod_l, 5, seq, top_w, y4, final_g, final_norm)


def kernel(x, c, ada_w, ada_b, norm_mix_g, norm_ffn_g, router_w, router_b, moe_w1, moe_b1, moe_w2, moe_b2, ab_w_in, ab_w_out, hgrn_lb, hgrn_norm_g, rwkv_mu, rwkv_w0, rwkv_w2, rwkv_a0, rwkv_a2, rwkv_g2, rwkv_k_k, rwkv_k_a, rwkv_r_k, rwkv_ln_g, rwkv_ln_b, attn_w_in, attn_w_out, attn_lambda, attn_subln_g, rel_bias_table, final_norm_g):
    batch, seq, d = x.shape
    n = batch * seq
    depth = ada_w.shape[0]
    x2 = x.reshape(n, d)
    mod = _adaln(c, ada_w, ada_b)
    lb_all = jnp.cumsum(jax.nn.softmax(hgrn_lb.astype(F32), axis=1), axis=1)
    for layer in range(depth):
        mod_l = mod[layer].reshape(batch, 1, 6 * d)
        j = layer // 2
        if layer % 2 == 0:
            a_cols = 5 * (d // 2)
            w_in = ab_w_in[j].astype(BF16)
            p_a, p_b = _normmod_proj(x2, norm_mix_g[layer], mod_l, 0, seq,
                                     [w_in[:, :a_cols], w_in[:, a_cols:]], [F32, F32])
            y_a = _hgrn2(p_a, lb_all[:, j], hgrn_norm_g[j], batch, seq)
            y_b = _rwkv7(p_b, rwkv_mu[j], rwkv_w0[j], rwkv_w2[j], rwkv_a0[j], rwkv_a2[j], rwkv_g2[j],
                         rwkv_k_k[j], rwkv_k_a[j], rwkv_r_k[j], rwkv_ln_g[j], rwkv_ln_b[j], batch, seq)
            w_out = ab_w_out[j].astype(BF16)
            x2 = _outproj_residual(x2, mod_l, 2, seq, [y_a, y_b], [w_out[:d // 2], w_out[d // 2:]])
        else:
            (qkv,) = _normmod_proj(x2, norm_mix_g[layer], mod_l, 0, seq, [attn_w_in[j].astype(BF16)], [BF16])
            lam = attn_lambda[j].astype(F32)
            lam_init = 0.8 - 0.6 * math.exp(-0.3 * layer)
            lam_full = jnp.exp(jnp.sum(lam[0] * lam[1])) - jnp.exp(jnp.sum(lam[2] * lam[3])) + lam_init
            o = _diff_attention(qkv, rel_bias_table, lam_full, attn_subln_g[j], lam_init, batch, seq)
            x2 = _outproj_residual(x2, mod_l, 2, seq, [o], [attn_w_out[j].astype(BF16)])
        x2 = _moe_ffn(x2, norm_ffn_g[layer], mod_l, seq, router_w[layer], router_b[layer], moe_w1,
                      moe_b1[layer], moe_w2, moe_b2[layer], layer, final_norm_g, layer == depth - 1)
    return x2.reshape(batch, seq, d)
```

```python
import functools
import math

import jax
import jax.numpy as jnp
import numpy as np
from jax import lax
from jax.experimental import pallas as pl
from jax.experimental.pallas import tpu as pltpu
from jax.experimental.pallas import tpu_sc as plsc

F32 = jnp.float32
BF16 = jnp.bfloat16
HIGHEST = lax.Precision.HIGHEST

NORM_EPS = 1e-6

A_HEAD_DIM = 128
A_CHUNK = 32
A_GROUP = 128
A_HEADS_PER_STEP = 2
B_HEAD_DIM = 64
B_CHUNK = 64
B_GROUP = 8
B_LN_EPS = 1e-5 * B_HEAD_DIM
C_HEAD_DIM = 64
REL_BUCKETS = 32
REL_MAX_DISTANCE = 128
TOP_K = 4
SWIGLU_LIMIT = 7.0
SWIGLU_ALPHA = 1.702

V7X_VMEM_BYTES = 64 * 1024 * 1024
VMEM_LIMIT = V7X_VMEM_BYTES - 8 * 1024 * 1024
LANES = 128


def _params(*sem):
    return pltpu.CompilerParams(dimension_semantics=sem, vmem_limit_bytes=VMEM_LIMIT)


def _sigmoid(x):
    return 1.0 / (1.0 + jnp.exp(-x))


def _silu(x):
    return x * _sigmoid(x)


def _dot(a, b):
    return jnp.dot(a, b, preferred_element_type=F32)


def _dot_nt(a, b):
    return lax.dot_general(a, b, (((1,), (1,)), ((), ())), preferred_element_type=F32)


def _dot_tn(a, b):
    return lax.dot_general(a, b, (((0,), (0,)), ((), ())), preferred_element_type=F32)


def _adaln_kernel(c_ref, w_ref, b_ref, o_ref):
    cond = _silu(c_ref[...])
    o_ref[0] = jnp.dot(cond, w_ref[0], precision=HIGHEST, preferred_element_type=F32) + b_ref[0]


def _adaln(c, ada_w, ada_b):
    n_layers, d, n_out = ada_w.shape
    batch = c.shape[0]
    tn = 1536
    return pl.pallas_call(
        _adaln_kernel,
        grid=(n_layers, n_out // tn),
        in_specs=[
            pl.BlockSpec((batch, d), lambda l, j: (0, 0)),
            pl.BlockSpec((1, d, tn), lambda l, j: (l, 0, j)),
            pl.BlockSpec((1, 1, tn), lambda l, j: (l, 0, j)),
        ],
        out_specs=pl.BlockSpec((1, batch, tn), lambda l, j: (l, 0, j)),
        out_shape=jax.ShapeDtypeStruct((n_layers, batch, n_out), F32),
        compiler_params=_params("parallel", "parallel"),
        name="adaln",
    )(c, ada_w, ada_b.reshape(n_layers, 1, n_out))


def _modulated_norm(x, g, shift, scale):
    y = x * lax.rsqrt(jnp.mean(x * x, axis=-1, keepdims=True) + NORM_EPS)
    return (y * g) * (1.0 + scale) + shift


def _normmod_proj_kernel(x_ref, g_ref, sh_ref, sc_ref, *rest, n_w):
    w_refs, o_refs = rest[:n_w], rest[n_w:]
    h = _modulated_norm(x_ref[...], g_ref[...], sh_ref[0], sc_ref[0]).astype(BF16)
    for w_ref, o_ref in zip(w_refs, o_refs):
        o_ref[...] = _dot(h, w_ref[...]).astype(o_ref.dtype)


def _normmod_proj(x2, g, mod_l, shift_idx, seq, weights, out_dtypes, tm=512):
    n, d = x2.shape
    per_batch = seq // tm
    in_specs = [
        pl.BlockSpec((tm, d), lambda i: (i, 0)),
        pl.BlockSpec((1, d), lambda i: (0, 0)),
        pl.BlockSpec((1, 1, d), lambda i: (i // per_batch, 0, shift_idx)),
        pl.BlockSpec((1, 1, d), lambda i: (i // per_batch, 0, shift_idx + 1)),
    ]
    in_specs += [pl.BlockSpec(w.shape, lambda i: (0, 0)) for w in weights]
    out_specs = [pl.BlockSpec((tm, w.shape[1]), lambda i: (i, 0)) for w in weights]
    out_shape = [jax.ShapeDtypeStruct((n, w.shape[1]), dt) for w, dt in zip(weights, out_dtypes)]
    return pl.pallas_call(
        functools.partial(_normmod_proj_kernel, n_w=len(weights)),
        grid=(n // tm,),
        in_specs=in_specs,
        out_specs=out_specs,
        out_shape=out_shape,
        compiler_params=_params("parallel"),
        name="normmod_proj",
    )(x2, g.reshape(1, d), mod_l, mod_l, *weights)


def _outproj_kernel(x_ref, gate_ref, *rest, n_y):
    y_refs, w_refs, o_ref = rest[:n_y], rest[n_y:2 * n_y], rest[2 * n_y]
    acc = _dot(y_refs[0][...], w_refs[0][...])
    for y_ref, w_ref in zip(y_refs[1:], w_refs[1:]):
        acc += _dot(y_ref[...], w_ref[...])
    o_ref[...] = x_ref[...] + gate_ref[0] * acc


def _outproj_residual(x2, mod_l, gate_idx, seq, ys, ws, tm=512):
    n, d = x2.shape
    per_batch = seq // tm
    in_specs = [
        pl.BlockSpec((tm, d), lambda i: (i, 0)),
        pl.BlockSpec((1, 1, d), lambda i: (i // per_batch, 0, gate_idx)),
    ]
    in_specs += [pl.BlockSpec((tm, y.shape[1]), lambda i: (i, 0)) for y in ys]
    in_specs += [pl.BlockSpec(w.shape, lambda i: (0, 0)) for w in ws]
    return pl.pallas_call(
        functools.partial(_outproj_kernel, n_y=len(ys)),
        grid=(n // tm,),
        in_specs=in_specs,
        out_specs=pl.BlockSpec((tm, d), lambda i: (i, 0)),
        out_shape=jax.ShapeDtypeStruct((n, d), F32),
        compiler_params=_params("parallel"),
        name="outproj_residual",
    )(x2, mod_l, *ys, *ws)


def _chunk_cumsum(x, chunk, reverse):
    rows = x.shape[0]
    pos = lax.broadcasted_iota(jnp.int32, x.shape, 0) % chunk
    s = 1
    while s < chunk:
        if reverse:
            x = x + jnp.where(pos < chunk - s, pltpu.roll(x, rows - s, axis=0), 0.0)
        else:
            x = x + jnp.where(pos >= s, pltpu.roll(x, s, axis=0), 0.0)
        s *= 2
    return x


def _hgrn_groups(slabs):
    g_rows, dk = slabs[0][0].shape
    n_chunks = g_rows // A_CHUNK
    ti = lax.broadcasted_iota(jnp.int32, (A_CHUNK, A_CHUNK), 0)
    si = lax.broadcasted_iota(jnp.int32, (A_CHUNK, A_CHUNK), 1)
    chunk_rows = [slice(c * A_CHUNK, (c + 1) * A_CHUNK) for c in range(n_chunks)]
    prep = []
    for q, f, v, lb, st, reverse in slabs:
        fg = lb + (1.0 - lb) * _sigmoid(f)
        k = 1.0 - fg
        b = _chunk_cumsum(jnp.log(fg), A_CHUNK, reverse)
        b3 = b.reshape(n_chunks, A_CHUNK, dk)
        edge = b3[:, 0:1, :] if reverse else b3[:, A_CHUNK - 1:A_CHUNK, :]
        prep.append(dict(
            q_in=(q * jnp.exp(b)).astype(BF16), k_in=(k * jnp.exp(-b)).astype(BF16),
            k_st=(k.reshape(n_chunks, A_CHUNK, dk) * jnp.exp(edge - b3)).astype(BF16),
            decay=jnp.exp(edge), vb=v.astype(BF16), st=st, reverse=reverse,
            mask=(si >= ti) if reverse else (si <= ti)))
    for p in prep:
        p["scores"] = [_dot_nt(p["q_in"][r], p["k_in"][r]) for r in chunk_rows]
        p["dstate"] = [_dot_tn(p["vb"][r], p["k_st"][c]) for c, r in enumerate(chunk_rows)]
    for p in prep:
        p["intra"] = [_dot(jnp.where(p["mask"], s, 0.0).astype(BF16), p["vb"][r])
                      for s, r in zip(p["scores"], chunk_rows)]
    results = []
    for p in prep:
        st = p["st"]
        outs = [None] * n_chunks
        for c in (range(n_chunks - 1, -1, -1) if p["reverse"] else range(n_chunks)):
            outs[c] = p["intra"][c] + _dot_nt(p["q_in"][chunk_rows[c]], st.astype(BF16))
            st = st * p["decay"][c] + p["dstate"][c]
        results.append((jnp.concatenate(outs, axis=0), st))
    return results


def _hgrn_kernel(q_ref, ff_ref, fb_ref, i_ref, g_ref, lb_ref, ng_ref, o_ref, of_ref, ob_ref):
    seq = q_ref.shape[0]
    dk = A_HEAD_DIM
    n_heads = q_ref.shape[1] // dk
    n_groups = seq // A_GROUP
    head_cols = [slice(h * dk, (h + 1) * dk) for h in range(n_heads)]

    def body(j, states):
        rf = pl.ds(pl.multiple_of(j * A_GROUP, A_GROUP), A_GROUP)
        rb = pl.ds(pl.multiple_of((n_groups - 1 - j) * A_GROUP, A_GROUP), A_GROUP)
        slabs = []
        for h, cols in enumerate(head_cols):
            slabs.append((_silu(q_ref[rf, cols]), ff_ref[rf, cols], i_ref[rf, cols], lb_ref[0:1, cols],
                          states[2 * h], False))
            slabs.append((_silu(q_ref[rb, cols]), fb_ref[rb, cols], i_ref[rb, cols], lb_ref[1:2, cols],
                          states[2 * h + 1], True))
        results = _hgrn_groups(slabs)
        for h, cols in enumerate(head_cols):
            of_ref[rf, cols] = results[2 * h][0]
            ob_ref[rb, cols] = results[2 * h + 1][0]
        return tuple(r[1] for r in results)

    zero = jnp.zeros((dk, dk), F32)
    lax.fori_loop(0, n_groups, body, (zero,) * (2 * n_heads))

    def finish(j, carry):
        r = pl.ds(pl.multiple_of(j * A_GROUP, A_GROUP), A_GROUP)
        for cols in head_cols:
            o = of_ref[r, cols] + ob_ref[r, cols]
            o = o * lax.rsqrt(jnp.mean(o * o, axis=-1, keepdims=True) + NORM_EPS) * ng_ref[...]
            o_ref[r, cols] = (o * _silu(g_ref[r, cols])).astype(o_ref.dtype)
        return carry

    lax.fori_loop(0, n_groups, finish, 0)


def _hgrn2(p_a, lb, norm_g, batch, seq):
    n = p_a.shape[0]
    width = p_a.shape[1] // 5
    cols = A_HEADS_PER_STEP * A_HEAD_DIM
    steps = width // cols
    sect = lambda s: pl.BlockSpec((seq, cols), lambda b, h, s=s: (b, s * steps + h))
    return pl.pallas_call(
        _hgrn_kernel,
        grid=(batch, steps),
        in_specs=[sect(0), sect(1), sect(2), sect(3), sect(4),
                  pl.BlockSpec((2, cols), lambda b, h: (0, h)),
                  pl.BlockSpec((1, A_HEAD_DIM), lambda b, h: (0, 0))],
        out_specs=pl.BlockSpec((seq, cols), lambda b, h: (b, h)),
        out_shape=jax.ShapeDtypeStruct((n, width), BF16),
        scratch_shapes=[pltpu.VMEM((seq, cols), F32), pltpu.VMEM((seq, cols), F32)],
        compiler_params=_params("parallel", "parallel"),
        name="hgrn2",
    )(p_a, p_a, p_a, p_a, p_a, lb, norm_g.reshape(1, A_HEAD_DIM))


def _softplus(z):
    return jnp.maximum(z, 0.0) + jnp.log(1.0 + jnp.exp(-jnp.abs(z)))


def _rwkv_prep_kernel(p_ref, prev_ref, next_ref, mu_ref, w0_ref, w2_ref, a0_ref, a2_ref, g2_ref,
                      kk_ref, ka_ref, rk_ref, ones_ref, tri_ref,
                      r_out, k_out, v_out, kkn_out, b_out, cumf_out, cumb_out, g_out, bonus_out):
    i = pl.program_id(1)
    last = pl.num_programs(1) - 1
    ts = p_ref.shape[0]
    width = a0_ref.shape[1]
    heads = width // B_HEAD_DIM
    p = p_ref[...]
    prev_row = jnp.where(i == 0, 0.0, prev_ref[7:8, :])
    next_row = jnp.where(i == last, 0.0, next_ref[0:1, :])
    row = lax.broadcasted_iota(jnp.int32, (ts, 1), 0)
    up = jnp.where(row == 0, prev_row, pltpu.roll(p, 1, axis=0))
    dn = jnp.where(row == ts - 1, next_row, pltpu.roll(p, ts - 1, axis=0))
    p = p + mu_ref[...] * (0.5 * (up + dn) - p)

    r = p[:, 0:width]
    k = p[:, width:2 * width]
    v = p[:, 2 * width:3 * width]
    o = 3 * width
    wlo_f = p[:, o:o + 64]
    wlo_b = p[:, o + 64:o + 128]
    alo = p[:, o + 128:o + 192]
    glo = p[:, o + 192:o + 320]

    def log_decay(wlo, d):
        z = w0_ref[d:d + 1, :] + _dot(jnp.tanh(wlo).astype(BF16), w2_ref[d])
        w = -_softplus(-z) - 0.5
        return -jnp.exp(w)

    def split_dot(sel, t, right):
        hi = t.astype(BF16)
        lo = (t - hi.astype(F32)).astype(BF16)
        return (_dot(hi, sel) + _dot(lo, sel)) if right else (_dot(sel, hi) + _dot(sel, lo))

    cum_f = split_dot(tri_ref[0], log_decay(wlo_f, 0), False)
    cum_b = split_dot(tri_ref[1], log_decay(wlo_b, 1), False)
    a = _sigmoid(a0_ref[...] + _dot(alo.astype(BF16), a2_ref[...]))
    g = _dot(_sigmoid(glo).astype(BF16), g2_ref[...])
    kk = k * kk_ref[...]
    head_sum = lambda t: split_dot(ones_ref[...], t, True)

    kk_n = kk / jnp.maximum(jnp.sqrt(head_sum(kk * kk)), 1e-12)
    k_mod = k * (1.0 + (a - 1.0) * ka_ref[...])
    bonus = head_sum(r * k_mod * rk_ref[...]) * v
    g_out[...] = g
    bonus_out[...] = bonus
    for h in range(heads):
        sl = slice(h * B_HEAD_DIM, (h + 1) * B_HEAD_DIM)
        r_out[0, h] = r[:, sl]
        k_out[0, h] = k_mod[:, sl]
        v_out[0, h] = v[:, sl]
        kkn_out[0, h] = kk_n[:, sl]
        b_out[0, h] = (kk_n * a)[:, sl]
        cumf_out[0, h] = cum_f[:, sl]
        cumb_out[0, h] = cum_b[:, sl]


def _rwkv_prep(p_b, mu, w0, w2, a0, a2, g2, k_k, k_a, r_k, batch, seq, ts=256):
    n, cols = p_b.shape
    width = a0.shape[0]
    heads = width // B_HEAD_DIM
    nblk = seq // ts
    rows8 = ts // 8
    head_id = np.arange(width) // B_HEAD_DIM
    ones = jnp.asarray(head_id[:, None] == head_id[None, :], BF16)
    full = lambda a: pl.BlockSpec(a.shape, lambda b, i: (0,) * a.ndim)
    vec = lambda a: a.reshape(1, -1)
    row = np.arange(ts)
    same_chunk = (row[:, None] // B_CHUNK) == (row[None, :] // B_CHUNK)
    tri = jnp.asarray(np.stack([same_chunk & (row[None, :] <= row[:, None]),
                                same_chunk & (row[None, :] >= row[:, None])]), BF16)
    args = [vec(mu), w0, w2.astype(BF16), vec(a0), a2.astype(BF16), g2.astype(BF16),
            vec(k_k), vec(k_a), vec(r_k), ones, tri]
    hm = pl.BlockSpec((1, heads, ts, B_HEAD_DIM), lambda b, i: (b, 0, i, 0))
    tokm = pl.BlockSpec((ts, width), lambda b, i: (b * nblk + i, 0))
    hm_shape = jax.ShapeDtypeStruct((batch, heads, seq, B_HEAD_DIM), F32)
    tok_shape = jax.ShapeDtypeStruct((n, width), F32)
    return pl.pallas_call(
        _rwkv_prep_kernel,
        grid=(batch, nblk),
        in_specs=[
            pl.BlockSpec((ts, cols), lambda b, i: (b * nblk + i, 0)),
            pl.BlockSpec((8, cols), lambda b, i: (jnp.maximum((b * nblk + i) * rows8 - 1, 0), 0)),
            pl.BlockSpec((8, cols), lambda b, i: (jnp.minimum((b * nblk + i + 1) * rows8, n // 8 - 1), 0)),
        ] + [full(a) for a in args],
        out_specs=[hm] * 7 + [tokm] * 2,
        out_shape=[hm_shape] * 7 + [tok_shape] * 2,
        compiler_params=_params("parallel", "parallel"),
        name="rwkv_prep",
    )(p_b, p_b, p_b, *args)


def _rwkv_chunks(chains):
    c, d = chains[0][0].shape
    ti = lax.broadcasted_iota(jnp.int32, (c, c), 0)
    si = lax.broadcasted_iota(jnp.int32, (c, c), 1)
    eye = jnp.where(ti == si, 1.0, 0.0)
    masks = {rev: (((si > ti), (si >= ti)) if rev else ((si < ti), (si <= ti))) for rev in (False, True)}
    t2 = lax.broadcasted_iota(jnp.int32, (c, 2 * c), 0)
    s2 = lax.broadcasted_iota(jnp.int32, (c, 2 * c), 1)
    s2 = jnp.where(s2 >= c, s2 - c, s2)
    incl2 = {False: s2 <= t2, True: s2 >= t2}

    row = lax.broadcasted_iota(jnp.int32, (c, d), 0)
    st = []
    for r, k, v, kk, b, cum, tt, rev in chains:
        before = (jnp.where(row == c - 1, 0.0, pltpu.roll(cum, c - 1, axis=0)) if rev
                  else jnp.where(row == 0, 0.0, pltpu.roll(cum, 1, axis=0)))
        grow = jnp.exp(-cum)
        kt = (kk * jnp.exp(before)).astype(BF16)
        rt = r * jnp.exp(cum)
        b_ = (b * grow).astype(BF16)
        kb = jnp.concatenate([(k * grow).astype(BF16), b_], axis=0)
        st.append(dict(vb=v.astype(BF16), kt=kt, rt=rt, b_=b_, kb=kb, tt=tt, ttb=tt.astype(BF16),
                       decay=jnp.exp(cum[0:1, :] if rev else cum[c - 1:c, :]), rev=rev,
                       lhs=jnp.concatenate([kt, rt.astype(BF16)], axis=0)))
    for s in st:
        s["big"] = _dot_nt(s["lhs"], s["kb"])
    for s in st:
        strict, incl = masks[s["rev"]]
        big = s["big"]
        a_ab = jnp.where(strict, big[:c, c:], 0.0)
        s["a_ak"] = jnp.where(strict, big[:c, :c], 0.0).astype(BF16)
        s["p"] = jnp.where(incl2[s["rev"]], big[c:], 0.0).astype(BF16)
        s["m"] = eye - a_ab
        s["ab"] = a_ab.astype(BF16)
    for s in st:
        s["aj"] = _dot(s["ab"], s["ab"])
        s["av"] = _dot(s["a_ak"], s["vb"])
    span = 2
    while span < c:
        span *= 2
        for s in st:
            ajb = s["aj"].astype(BF16)
            if span < c:
                both = _dot(jnp.concatenate([ajb, s["m"].astype(BF16)], axis=0), ajb)
                s["aj"], s["m"] = both[:c], s["m"] + both[c:]
            else:
                s["m"] = s["m"] + _dot(s["m"].astype(BF16), ajb)
    for s in st:
        wu = _dot(s["m"].astype(BF16), jnp.concatenate([s["kt"], s["av"].astype(BF16)], axis=1))
        s["nwu"] = -wu.astype(BF16)
    for s in st:
        nw, nu0, vb = s["nwu"][:, :d], s["nwu"][:, d:], s["vb"]
        rhs = jnp.concatenate([jnp.concatenate([vb, jnp.zeros_like(vb)], axis=1),
                               jnp.concatenate([nu0, nw], axis=1)], axis=0)
        s["yq"] = _dot(s["p"], rhs)
        s["h_t"] = _dot_tn(jnp.concatenate([vb, nu0], axis=0), s["kb"])
        s["nbw"] = _dot_tn(s["b_"], nw)
    out = []
    for s in st:
        q = s["rt"] + s["yq"][:, d:]
        y = _dot_nt(q.astype(BF16), s["ttb"]) + s["yq"][:, :d]
        tt = (s["tt"] + _dot_nt(s["ttb"], s["nbw"].astype(BF16)) + s["h_t"]) * s["decay"]
        out.append((y, tt))
    return out


def _rwkv_scan_kernel(rf, kf, vf, kkf, bf, lwf, rb, kb, vb, kkb, bb, lwb, yf_ref, yb_ref, tf_ref, tb_ref):
    heads, ts = rf.shape[1], rf.shape[2]
    n_chunks = ts // B_CHUNK

    @pl.when(pl.program_id(1) == 0)
    def _():
        tf_ref[...] = jnp.zeros_like(tf_ref)
        tb_ref[...] = jnp.zeros_like(tb_ref)

    def body(j, carry):
        rows_f = pl.ds(pl.multiple_of(j * B_CHUNK, B_CHUNK), B_CHUNK)
        rows_b = pl.ds(pl.multiple_of((n_chunks - 1 - j) * B_CHUNK, B_CHUNK), B_CHUNK)
        for h0 in range(0, heads, B_GROUP):
            group = range(h0, h0 + B_GROUP)
            chains = []
            for h in group:
                chains.append([ref[0, h, rows_f, :] for ref in (rf, kf, vf, kkf, bf, lwf)] + [tf_ref[h], False])
                chains.append([ref[0, h, rows_b, :] for ref in (rb, kb, vb, kkb, bb, lwb)] + [tb_ref[h], True])
            res = _rwkv_chunks(chains)
            for n, h in enumerate(group):
                yf_ref[0, h, rows_f, :], tf_ref[h] = res[2 * n]
                yb_ref[0, h, rows_b, :], tb_ref[h] = res[2 * n + 1]
        return carry

    lax.fori_loop(0, n_chunks, body, 0)


def _rwkv_scan(r, k, v, kkn, b, lw_f, lw_b, ts=256):
    batch, heads, seq, d = r.shape
    nblk = seq // ts
    fwd = pl.BlockSpec((1, heads, ts, d), lambda bi, i: (bi, 0, i, 0))
    bwd = pl.BlockSpec((1, heads, ts, d), lambda bi, i: (bi, 0, nblk - 1 - i, 0))
    shape = jax.ShapeDtypeStruct(r.shape, F32)
    return pl.pallas_call(
        _rwkv_scan_kernel,
        grid=(batch, nblk),
        in_specs=[fwd] * 6 + [bwd] * 6,
        out_specs=[fwd, bwd],
        out_shape=[shape, shape],
        scratch_shapes=[pltpu.VMEM((heads, d, d), F32), pltpu.VMEM((heads, d, d), F32)],
        compiler_params=_params("parallel", "arbitrary"),
        name="rwkv_scan",
    )(r, k, v, kkn, b, lw_f, r, k, v, kkn, b, lw_b)


def _rwkv_post_kernel(yf_ref, yb_ref, g_ref, bonus_ref, lng_ref, lnb_ref, o_ref):
    heads = yf_ref.shape[1]
    outs = []
    for h in range(heads):
        y = yf_ref[0, h] + yb_ref[0, h]
        mean = jnp.mean(y, axis=-1, keepdims=True)
        var = jnp.mean(jnp.square(y - mean), axis=-1, keepdims=True)
        outs.append((y - mean) * lax.rsqrt(var + B_LN_EPS))
    yn = jnp.concatenate(outs, axis=1) * lng_ref[...] + lnb_ref[...]
    o_ref[...] = ((yn + bonus_ref[...]) * g_ref[...]).astype(o_ref.dtype)


def _rwkv_post(y_f, y_b, g, bonus, ln_g, ln_b, ts=256):
    batch, heads, seq, d = y_f.shape
    n, width = g.shape
    nblk = seq // ts
    hm = pl.BlockSpec((1, heads, ts, d), lambda b, i: (b, 0, i, 0))
    tokm = pl.BlockSpec((ts, width), lambda b, i: (b * nblk + i, 0))
    vec = pl.BlockSpec((1, width), lambda b, i: (0, 0))
    return pl.pallas_call(
        _rwkv_post_kernel,
        grid=(batch, nblk),
        in_specs=[hm, hm, tokm, tokm, vec, vec],
        out_specs=tokm,
        out_shape=jax.ShapeDtypeStruct((n, width), BF16),
        compiler_params=_params("parallel", "parallel"),
        name="rwkv_post",
    )(y_f, y_b, g, bonus, ln_g.reshape(1, width), ln_b.reshape(1, width))


def _rwkv7(p_b, mu, w0, w2, a0, a2, g2, k_k, k_a, r_k, ln_g, ln_b, batch, seq):
    r, k, v, kkn, b, lw_f, lw_b, g, bonus = _rwkv_prep(p_b, mu, w0, w2, a0, a2, g2, k_k, k_a, r_k, batch, seq)
    y_f, y_b = _rwkv_scan(r, k, v, kkn, b, lw_f, lw_b)
    return _rwkv_post(y_f, y_b, g, bonus, ln_g, ln_b)


_BUCKET_EDGES = (0, 1, 2, 3, 4, 5, 6, 7, 8, 12, 16, 23, 32, 46, 64, 91)
ATTN_TK = 128
C_HEADS_PER_STEP = 4


def _bias_band_kernel(table_ref, o_ref):
    h = pl.program_id(0)
    n_d, tq, tk = o_ref.shape[1:]
    half = REL_BUCKETS // 2
    r = lax.broadcasted_iota(jnp.int32, (tq, tk), 0)
    c = lax.broadcasted_iota(jnp.int32, (tq, tk), 1)
    for d in range(n_d):
        rel = tk * (d - 1) + c - r
        n = jnp.abs(rel)
        vals = []
        for sign in range(2):
            val = jnp.full((tq, tk), table_ref[sign * half + half - 1, h], F32)
            for bkt in range(half - 2, -1, -1):
                val = jnp.where(n < _BUCKET_EDGES[bkt + 1], table_ref[sign * half + bkt, h], val)
            vals.append(val)
        o_ref[0, d] = jnp.where(rel > 0, vals[1], vals[0])


def _bias_band(rel_table, tq):
    heads = rel_table.shape[1]
    n_d = tq // ATTN_TK + 2
    return pl.pallas_call(
        _bias_band_kernel,
        grid=(heads,),
        in_specs=[pl.BlockSpec(memory_space=pltpu.SMEM)],
        out_specs=pl.BlockSpec((1, n_d, tq, ATTN_TK), lambda h: (h, 0, 0, 0)),
        out_shape=jax.ShapeDtypeStruct((heads, n_d, tq, ATTN_TK), F32),
        compiler_params=_params("parallel"),
        name="bias_band",
    )(rel_table)


def _diff_attn_kernel(table_ref, lam_ref, q_ref, k_ref, v_ref, band_ref, g_ref, o_ref, lg_ref, vext_ref, kt_ref,
                      *, lam_init):
    i = pl.program_id(2)
    tq = q_ref.shape[0]
    dh2 = 2 * C_HEAD_DIM
    n_heads = q_ref.shape[1] // dh2
    seq = k_ref.shape[0]
    n_d = band_ref.shape[1]
    n_kt = seq // ATTN_TK
    ratio = tq // ATTN_TK
    half = REL_BUCKETS // 2
    head_cols = [slice(hh * dh2, (hh + 1) * dh2) for hh in range(n_heads)]

    @pl.when(i == 0)
    def _():
        for hh, cols in enumerate(head_cols):
            kt_ref[hh] = k_ref[:, cols].astype(F32).T.astype(BF16)
            v = v_ref[:, cols]
            vext_ref[hh] = jnp.concatenate([v, jnp.ones_like(v)], axis=1)

    lane = lax.broadcasted_iota(jnp.int32, (1, dh2), 1)
    first_band = ratio * i - 1
    col = lax.broadcasted_iota(jnp.int32, (1, seq), 1)
    for hh, cols in enumerate(head_cols):
        h = pl.program_id(1) * n_heads + hh
        q = q_ref[:, cols] * jnp.asarray(C_HEAD_DIM ** -0.5, q_ref.dtype)
        zero = jnp.zeros_like(q)
        far_row = jnp.where(col < first_band * ATTN_TK, table_ref[half - 1, h],
                            jnp.where(col >= (first_band + n_d) * ATTN_TK, table_ref[REL_BUCKETS - 1, h],
                                      -jnp.inf))
        row_max = []
        for m in range(2):
            qm = jnp.where((lane >= m * C_HEAD_DIM) & (lane < (m + 1) * C_HEAD_DIM), q, zero)
            far = _dot(qm, kt_ref[hh]) + far_row
            mx = jnp.max(far, axis=-1, keepdims=True)
            lg_ref[hh, m] = far
            for d in (0, n_d - 1) + tuple(range(1, n_d - 1)):
                kt = first_band + d
                valid = (kt >= 0) & (kt < n_kt)
                c0 = pl.multiple_of(jnp.clip(kt, 0, n_kt - 1) * ATTN_TK, ATTN_TK)
                near = _dot(qm, kt_ref[hh, :, pl.ds(c0, ATTN_TK)]) + band_ref[hh, d]
                mx = jnp.maximum(mx, jnp.where(valid, jnp.max(near, axis=-1, keepdims=True), -jnp.inf))
                lg_ref[hh, m, :, pl.ds(c0, ATTN_TK)] = near
            row_max.append(mx)
        parts = []
        for m in range(2):
            e = jnp.exp(lg_ref[hh, m] - row_max[m]).astype(BF16)
            pv = _dot(e, vext_ref[hh])
            parts.append(pv[:, :dh2] * (1.0 / pv[:, dh2:dh2 + 1]))
        o = parts[0] - lam_ref[0] * parts[1]
        o = o * lax.rsqrt(jnp.mean(o * o, axis=-1, keepdims=True) + NORM_EPS) * g_ref[...]
        o_ref[:, cols] = (o * (1.0 - lam_init)).astype(o_ref.dtype)


def _diff_attention(qkv, rel_table, lam_full, subln_g, lam_init, batch, seq, tq=256):
    n = qkv.shape[0]
    dh2 = 2 * C_HEAD_DIM
    heads = qkv.shape[1] // (3 * dh2)
    nq = seq // tq
    per = C_HEADS_PER_STEP
    groups = heads // per
    cols = per * dh2
    band = _bias_band(rel_table, tq)
    return pl.pallas_call(
        functools.partial(_diff_attn_kernel, lam_init=lam_init),
        grid=(batch, groups, nq),
        in_specs=[
            pl.BlockSpec(memory_space=pltpu.SMEM),
            pl.BlockSpec(memory_space=pltpu.SMEM),
            pl.BlockSpec((tq, cols), lambda b, h, i: (b * nq + i, h)),
            pl.BlockSpec((seq, cols), lambda b, h, i: (b, groups + h)),
            pl.BlockSpec((seq, cols), lambda b, h, i: (b, 2 * groups + h)),
            pl.BlockSpec((per,) + band.shape[1:], lambda b, h, i: (h, 0, 0, 0)),
            pl.BlockSpec((1, dh2), lambda b, h, i: (0, 0)),
        ],
        out_specs=pl.BlockSpec((tq, cols), lambda b, h, i: (b * nq + i, h)),
        out_shape=jax.ShapeDtypeStruct((n, heads * dh2), BF16),
        scratch_shapes=[pltpu.VMEM((per, 2, tq, seq), F32), pltpu.VMEM((per, seq, 2 * dh2), BF16),
                        pltpu.VMEM((per, dh2, seq), BF16)],
        compiler_params=_params("parallel", "parallel", "arbitrary"),
        name="diff_attention",
    )(rel_table, lam_full.reshape(1), qkv, qkv, qkv, band, subln_g.reshape(1, dh2))


MOE_TILE = 256
EXPERT_TILE = 512
EXPERT_CHUNK = 512
_HI_MASK = np.uint32(0xFFFF0000)


def _pack_bf16_pair(lo, hi):
    as_bits = lambda t: lax.bitcast_convert_type(t.astype(BF16).astype(F32), jnp.uint32)
    return (as_bits(hi) & _HI_MASK) | (as_bits(lo) >> 16)


def _unpack_bf16_pair(word):
    lo = lax.bitcast_convert_type(word << 16, F32).astype(BF16)
    hi = lax.bitcast_convert_type(word & _HI_MASK, F32).astype(BF16)
    return lo, hi


def _router_kernel(x_ref, g_ref, sh_ref, sc_ref, rw_ref, rb_ref, hp_ref, idx_ref, w_ref, rank_ref, cnt_ref,
                   run_ref, *, n_experts):
    @pl.when(pl.program_id(0) == 0)
    def _():
        run_ref[...] = jnp.zeros_like(run_ref)

    tm, d = x_ref.shape
    h = _modulated_norm(x_ref[...], g_ref[...], sh_ref[0], sc_ref[0])
    hp_ref[...] = _pack_bf16_pair(h[:, :d // 2], h[:, d // 2:])
    logits = _dot(h.astype(BF16), rw_ref[...]) + rb_ref[...]
    lane = lax.broadcasted_iota(jnp.int32, logits.shape, 1).astype(F32)
    neg = jnp.float32(-jnp.inf)
    logits = jnp.where(lane < n_experts, logits, neg)
    picks, vals = [], []
    for _ in range(TOP_K):
        m = jnp.max(logits, axis=-1, keepdims=True)
        first = jnp.min(jnp.where(logits == m, lane, float(LANES)), axis=-1, keepdims=True)
        hit = lane == first
        picks.append((first, hit))
        vals.append(m)
        logits = jnp.where(hit, neg, logits)
    es = [jnp.exp(v - vals[0]) for v in vals]
    inv = 1.0 / (es[0] + es[1] + es[2] + es[3])
    assigned = picks[0][1] | picks[1][1] | picks[2][1] | picks[3][1]
    onehot = jnp.where(assigned, 1.0, 0.0)
    ri = lax.broadcasted_iota(jnp.int32, (tm, tm), 0)
    ci = lax.broadcasted_iota(jnp.int32, (tm, tm), 1)
    earlier = jnp.where(ci < ri, 1.0, 0.0).astype(BF16)
    before = _dot(earlier, onehot.astype(BF16)) + run_ref[...]
    idx_o = jnp.zeros(logits.shape, F32)
    w_o = jnp.zeros(logits.shape, F32)
    rank_o = jnp.zeros(logits.shape, F32)
    for j, (first, hit) in enumerate(picks):
        rank_j = jnp.sum(jnp.where(hit, before, 0.0), axis=-1, keepdims=True)
        idx_o = jnp.where(lane == j, first, idx_o)
        w_o = jnp.where(lane == j, es[j] * inv, w_o)
        rank_o = jnp.where(lane == j, rank_j, rank_o)
    idx_ref[...] = idx_o
    w_ref[...] = w_o
    rank_ref[...] = rank_o
    run_ref[...] += jnp.sum(onehot, axis=0, keepdims=True)
    cnt_ref[...] = run_ref[...]


def _router(x2, g, mod_l, shift_idx, seq, router_w, router_b):
    n, d = x2.shape
    tm = MOE_TILE
    n_experts = router_w.shape[1]
    per_batch = seq // tm
    rw = jnp.zeros((d, LANES), BF16).at[:, :n_experts].set(router_w.astype(BF16))
    rb = jnp.zeros((1, LANES), F32).at[0, :n_experts].set(router_b)
    lane_out = pl.BlockSpec((tm, LANES), lambda i: (i, 0))
    lane_shape = jax.ShapeDtypeStruct((n, LANES), F32)
    return pl.pallas_call(
        functools.partial(_router_kernel, n_experts=n_experts),
        grid=(n // tm,),
        in_specs=[
            pl.BlockSpec((tm, d), lambda i: (i, 0)),
            pl.BlockSpec((1, d), lambda i: (0, 0)),
            pl.BlockSpec((1, 1, d), lambda i: (i // per_batch, 0, shift_idx)),
            pl.BlockSpec((1, 1, d), lambda i: (i // per_batch, 0, shift_idx + 1)),
            pl.BlockSpec((d, LANES), lambda i: (0, 0)),
            pl.BlockSpec((1, LANES), lambda i: (0, 0)),
        ],
        out_specs=[pl.BlockSpec((tm, d // 2), lambda i: (i, 0)), lane_out, lane_out, lane_out,
                   pl.BlockSpec((1, LANES), lambda i: (0, 0))],
        out_shape=[jax.ShapeDtypeStruct((n, d // 2), jnp.uint32), lane_shape, lane_shape, lane_shape,
                   jax.ShapeDtypeStruct((1, LANES), F32)],
        scratch_shapes=[pltpu.VMEM((1, LANES), F32)],
        compiler_params=_params("arbitrary"),
        name="moe_router",
    )(x2, g.reshape(1, d), mod_l, mod_l, rw, rb)


GLU_TILE = 2 * LANES


def _experts_kernel(te_ref, nu_ref, slot_ref, nxt_ref, xs_ref, b1_ref, b2_ref, perm_ref, w1_hbm, w2_hbm,
                    ys_ref, w1buf, w2buf, w1p_ref, w2b_ref, sem, *, layer):
    t = pl.program_id(0)
    used = t < nu_ref[0]
    new_expert = (t == 0) | (te_ref[t] != te_ref[jnp.maximum(t - 1, 0)])
    n_col_tiles = w1p_ref.shape[1] // GLU_TILE

    def weight_copies(e, slot):
        return (pltpu.make_async_copy(w1_hbm.at[layer, e], w1buf.at[slot], sem.at[0, slot]),
                pltpu.make_async_copy(w2_hbm.at[layer, e], w2buf.at[slot], sem.at[1, slot]))

    @pl.when(used & new_expert)
    def _():
        slot = slot_ref[t]

        @pl.when(t == 0)
        def _():
            for cp in weight_copies(te_ref[t], slot):
                cp.start()

        for cp in weight_copies(te_ref[t], slot):
            cp.wait()

        @pl.when(nxt_ref[t] >= 0)
        def _():
            for cp in weight_copies(nxt_ref[t], 1 - slot):
                cp.start()

        for c in range(n_col_tiles):
            cols = slice(c * GLU_TILE, (c + 1) * GLU_TILE)
            w1p_ref[:, cols] = _dot(w1buf[slot, :, cols].astype(BF16), perm_ref[...]).astype(BF16)
        w2b_ref[...] = w2buf[slot].astype(BF16)

    @pl.when(used)
    def _():
        lo, hi = _unpack_bf16_pair(xs_ref[...])
        x = jnp.concatenate([lo, hi], axis=1)
        out = b2_ref[0]
        for j in range(n_col_tiles * LANES // EXPERT_CHUNK):
            hcols = slice(2 * j * EXPERT_CHUNK, 2 * (j + 1) * EXPERT_CHUNK)
            hh = _dot(x, w1p_ref[:, hcols]) + b1_ref[0, :, hcols]
            acts = []
            for c in range(2 * EXPERT_CHUNK // GLU_TILE):
                x_glu = jnp.minimum(hh[:, c * GLU_TILE:c * GLU_TILE + LANES], SWIGLU_LIMIT)
                x_lin = jnp.clip(hh[:, c * GLU_TILE + LANES:(c + 1) * GLU_TILE], -SWIGLU_LIMIT, SWIGLU_LIMIT)
                acts.append((x_glu * _sigmoid(SWIGLU_ALPHA * x_glu) * (x_lin + 1.0)).astype(BF16))
            out = out + _dot(jnp.concatenate(acts, axis=1), w2b_ref[j * EXPERT_CHUNK:(j + 1) * EXPERT_CHUNK, :])
        half = out.shape[1] // 2
        ys_ref[...] = _pack_bf16_pair(out[:, :half], out[:, half:])

    @pl.when(t >= nu_ref[0])
    def _():
        ys_ref[...] = jnp.zeros_like(ys_ref)


def _experts(xs, tile_expert, n_used, slot, nxt, w1_all, b1, w2_all, b2, layer):
    n_rows, dw = xs.shape
    tm = EXPERT_TILE
    _, n_exp, d, de2 = w1_all.shape
    src = np.arange(GLU_TILE)
    dst = np.where(src % 2 == 0, src // 2, LANES + src // 2)
    perm = jnp.asarray(dst[:, None] == np.arange(GLU_TILE)[None, :], BF16)
    b1p = b1.reshape(n_exp, de2 // GLU_TILE, LANES, 2).transpose(0, 1, 3, 2).reshape(n_exp, 1, de2)
    row = lambda t, te, nu, sl, nx: (jnp.maximum(jnp.minimum(t, nu[0] - 1), 0), 0)
    exp3 = lambda t, te, nu, sl, nx: (te[t], 0, 0)
    grid_spec = pltpu.PrefetchScalarGridSpec(
        num_scalar_prefetch=4,
        grid=(n_rows // tm,),
        in_specs=[
            pl.BlockSpec((tm, dw), row),
            pl.BlockSpec((1, 1, de2), exp3),
            pl.BlockSpec((1, 1, d), exp3),
            pl.BlockSpec((GLU_TILE, GLU_TILE), lambda t, te, nu, sl, nx: (0, 0)),
            pl.BlockSpec(memory_space=pl.ANY),
            pl.BlockSpec(memory_space=pl.ANY),
        ],
        out_specs=pl.BlockSpec((tm, d // 2), lambda t, te, nu, sl, nx: (t, 0)),
        scratch_shapes=[pltpu.VMEM((2, d, de2), F32), pltpu.VMEM((2, de2 // 2, d), F32),
                        pltpu.VMEM((d, de2), BF16), pltpu.VMEM((de2 // 2, d), BF16),
                        pltpu.SemaphoreType.DMA((2, 2))],
    )
    return pl.pallas_call(
        functools.partial(_experts_kernel, layer=layer),
        grid_spec=grid_spec,
        out_shape=jax.ShapeDtypeStruct((n_rows, d // 2), jnp.uint32),
        compiler_params=_params("arbitrary"),
        name="moe_experts",
    )(tile_expert, n_used, slot, nxt, xs, b1p, b2.reshape(n_exp, 1, d), perm, w1_all, w2_all)


SC_WINDOW = 128


def _sc_worker_rows(n_rows):
    sc = plsc.get_sparse_core_info()
    per_worker = n_rows // (sc.num_cores * sc.num_subcores)
    worker = lax.axis_index("subcore") * sc.num_cores + lax.axis_index("core")
    return worker * per_worker, per_worker


def _sc_mesh():
    return plsc.VectorSubcoreMesh(core_axis_name="core", subcore_axis_name="subcore")


def _sc_gather_rows(table, idx):
    n_idx, width = idx.shape[0], table.shape[1]

    @functools.partial(
        pl.kernel, out_type=jax.ShapeDtypeStruct((n_idx, width), table.dtype), mesh=_sc_mesh(),
        scratch_types=[pltpu.VMEM((SC_WINDOW,), jnp.int32), pltpu.VMEM((SC_WINDOW, width), table.dtype)],
        name="sc_gather_rows")
    def gather(table_hbm, idx_hbm, out_hbm, idx_v, rows_v):
        first, count = _sc_worker_rows(n_idx)

        @pl.loop(0, count // SC_WINDOW)
        def _(c):
            rows = pl.ds(first + c * SC_WINDOW, SC_WINDOW)
            pltpu.sync_copy(idx_hbm.at[rows], idx_v)
            pltpu.sync_copy(table_hbm.at[idx_v], rows_v)
            pltpu.sync_copy(rows_v, out_hbm.at[rows])

    return gather(table, idx)


def _sc_scatter_rows(rows, pos, n_out):
    n, width = rows.shape
    k = pos.shape[0]

    @functools.partial(
        pl.kernel, out_type=jax.ShapeDtypeStruct((n_out, width), rows.dtype), mesh=_sc_mesh(),
        scratch_types=[pltpu.VMEM((k, SC_WINDOW), jnp.int32), pltpu.VMEM((SC_WINDOW, width), rows.dtype)],
        name="sc_scatter_rows")
    def scatter(rows_hbm, pos_hbm, out_hbm, pos_v, rows_v):
        first, count = _sc_worker_rows(n)

        @pl.loop(0, count // SC_WINDOW)
        def _(c):
            src = pl.ds(first + c * SC_WINDOW, SC_WINDOW)
            pltpu.sync_copy(pos_hbm.at[:, src], pos_v)
            pltpu.sync_copy(rows_hbm.at[src], rows_v)
            for j in range(k):
                pltpu.sync_copy(rows_v, out_hbm.at[pos_v.at[j]])

    return scatter(rows, pos)


def _combine_dense_kernel(x_ref, gate_ref, w_ref, fg_ref, *rest, final_norm):
    y_refs, o_ref = rest[:TOP_K], rest[TOP_K]
    w = w_ref[...]
    y_lo = y_hi = None
    for j in range(TOP_K):
        lo, hi = _unpack_bf16_pair(y_refs[j][...])
        wj = w[:, j:j + 1]
        y_lo = wj * lo if j == 0 else y_lo + wj * lo
        y_hi = wj * hi if j == 0 else y_hi + wj * hi
    out = x_ref[...] + gate_ref[0] * jnp.concatenate([y_lo, y_hi], axis=1)
    if final_norm:
        out = out * lax.rsqrt(jnp.mean(out * out, axis=-1, keepdims=True) + NORM_EPS) * fg_ref[...]
    o_ref[...] = out


def _combine_dense(x2, mod_l, gate_idx, seq, top_w, y4, final_g, final_norm, tm=512):
    n, d = x2.shape
    per_batch = seq // tm
    blocks = n // tm
    slot_spec = lambda j: pl.BlockSpec((tm, d // 2), lambda i: (j * blocks + i, 0))
    return pl.pallas_call(
        functools.partial(_combine_dense_kernel, final_norm=final_norm),
        grid=(blocks,),
        in_specs=[
            pl.BlockSpec((tm, d), lambda i: (i, 0)),
            pl.BlockSpec((1, 1, d), lambda i: (i // per_batch, 0, gate_idx)),
            pl.BlockSpec((tm, LANES), lambda i: (i, 0)),
            pl.BlockSpec((1, d), lambda i: (0, 0)),
        ] + [slot_spec(j) for j in range(TOP_K)],
        out_specs=pl.BlockSpec((tm, d), lambda i: (i, 0)),
        out_shape=jax.ShapeDtypeStruct((n, d), F32),
        compiler_params=_params("parallel"),
        name="moe_combine",
    )(x2, mod_l, top_w, final_g.reshape(1, d), *([y4] * TOP_K))


def _moe_ffn(x2, norm_g, mod_l, seq, router_w, router_b, w1_all, b1, w2_all, b2, layer, final_g, final_norm):
    n, d = x2.shape
    n_exp = router_w.shape[1]
    tile = EXPERT_TILE
    hp, idx_f, top_w, rank_f, counts = _router(x2, norm_g, mod_l, 3, seq, router_w, router_b)
    counts = counts[0, :n_exp].astype(jnp.int32)
    padded = (counts + tile - 1) // tile * tile
    ends = jnp.cumsum(padded)
    starts = ends - padded
    idx = idx_f[:, :TOP_K].astype(jnp.int32)
    expert_ids = jnp.arange(n_exp, dtype=jnp.int32)
    start_of = jnp.sum(jnp.where(idx[..., None] == expert_ids, starts, 0), axis=-1)
    pos_flat = (start_of + rank_f[:, :TOP_K].astype(jnp.int32)).reshape(-1)
    n_rows = n * TOP_K + n_exp * tile
    n_tiles = n_rows // tile
    tile_start = jnp.arange(n_tiles, dtype=jnp.int32) * tile
    tile_expert = jnp.sum((ends[None, :] <= tile_start[:, None]).astype(jnp.int32), axis=1)
    tile_expert = jnp.minimum(tile_expert, n_exp - 1)
    n_used = (ends[-1:] // tile).astype(jnp.int32)
    nonempty = counts > 0
    ordinal = jnp.cumsum(nonempty.astype(jnp.int32)) - 1
    later = nonempty[None, :] & (expert_ids[None, :] > expert_ids[:, None])
    nxt_e = jnp.min(jnp.where(later, expert_ids[None, :], n_exp), axis=1)
    nxt_e = jnp.where(nxt_e == n_exp, -1, nxt_e)
    slot = (ordinal[tile_expert] % 2).astype(jnp.int32)
    nxt = nxt_e[tile_expert].astype(jnp.int32)
    pos_slot_major = pos_flat.reshape(n, TOP_K).T
    xs = _sc_scatter_rows(hp, pos_slot_major, n_rows)
    ys = _experts(xs, tile_expert, n_used, slot, nxt, w1_all, b1, w2_all, b2, layer)
    y4 = _sc_gather_rows(ys, pos_slot_major.reshape(-1))
    return _combine_dense(x2, mod_l, 5, seq, top_w, y4, final_g, final_norm)


def kernel(x, c, ada_w, ada_b, norm_mix_g, norm_ffn_g, router_w, router_b, moe_w1, moe_b1, moe_w2, moe_b2, ab_w_in, ab_w_out, hgrn_lb, hgrn_norm_g, rwkv_mu, rwkv_w0, rwkv_w2, rwkv_a0, rwkv_a2, rwkv_g2, rwkv_k_k, rwkv_k_a, rwkv_r_k, rwkv_ln_g, rwkv_ln_b, attn_w_in, attn_w_out, attn_lambda, attn_subln_g, rel_bias_table, final_norm_g):
    batch, seq, d = x.shape
    n = batch * seq
    depth = ada_w.shape[0]
    x2 = x.reshape(n, d)
    mod = _adaln(c, ada_w, ada_b)
    lb_all = jnp.cumsum(jax.nn.softmax(hgrn_lb.astype(F32), axis=1), axis=1)
    for layer in range(depth):
        mod_l = mod[layer].reshape(batch, 1, 6 * d)
        j = layer // 2
        if layer % 2 == 0:
            a_cols = 5 * (d // 2)
            w_in = ab_w_in[j].astype(BF16)
            p_a, p_b = _normmod_proj(x2, norm_mix_g[layer], mod_l, 0, seq,
                                     [w_in[:, :a_cols], w_in[:, a_cols:]], [F32, F32])
            y_a = _hgrn2(p_a, lb_all[:, j], hgrn_norm_g[j], batch, seq)
            y_b = _rwkv7(p_b, rwkv_mu[j], rwkv_w0[j], rwkv_w2[j], rwkv_a0[j], rwkv_a2[j], rwkv_g2[j],
                         rwkv_k_k[j], rwkv_k_a[j], rwkv_r_k[j], rwkv_ln_g[j], rwkv_ln_b[j], batch, seq)
            w_out = ab_w_out[j].astype(BF16)
            x2 = _outproj_residual(x2, mod_l, 2, seq, [y_a, y_b], [w_out[:d // 2], w_out[d // 2:]])
        else:
            (qkv,) = _normmod_proj(x2, norm_mix_g[layer], mod_l, 0, seq, [attn_w_in[j].astype(BF16)], [BF16])
            lam = attn_lambda[j].astype(F32)
            lam_init = 0.8 - 0.6 * math.exp(-0.3 * layer)
            lam_full = jnp.exp(jnp.sum(lam[0] * lam[1])) - jnp.exp(jnp.sum(lam[2] * lam[3])) + lam_init
            o = _diff_attention(qkv, rel_bias_table, lam_full, attn_subln_g[j], lam_init, batch, seq)
            x2 = _outproj_residual(x2, mod_l, 2, seq, [o], [attn_w_out[j].astype(BF16)])
        x2 = _moe_ffn(x2, norm_ffn_g[layer], mod_l, seq, router_w[layer], router_b[layer], moe_w1,
                      moe_b1[layer], moe_w2, moe_b2[layer], layer, final_norm_g, layer == depth - 1)
    return x2.reshape(batch, seq, d)
```

```python
import functools
import math

import jax
import jax.numpy as jnp
import numpy as np
from jax import lax
from jax.experimental import pallas as pl
from jax.experimental.pallas import tpu as pltpu
from jax.experimental.pallas import tpu_sc as plsc

F32 = jnp.float32
BF16 = jnp.bfloat16
HIGHEST = lax.Precision.HIGHEST

NORM_EPS = 1e-6

A_HEAD_DIM = 128
A_CHUNK = 32
A_GROUP = 512
A_HEADS_PER_STEP = 2
B_HEAD_DIM = 64
B_CHUNK = 64
B_STEPS = 2
B_LN_EPS = 1e-5 * B_HEAD_DIM
C_HEAD_DIM = 64
REL_BUCKETS = 32
REL_MAX_DISTANCE = 128
TOP_K = 4
SWIGLU_LIMIT = 7.0
SWIGLU_ALPHA = 1.702

V7X_VMEM_BYTES = 64 * 1024 * 1024
VMEM_LIMIT = V7X_VMEM_BYTES - 8 * 1024 * 1024
LANES = 128


def _params(*sem):
    return pltpu.CompilerParams(dimension_semantics=sem, vmem_limit_bytes=VMEM_LIMIT)


def _sigmoid(x):
    return 1.0 / (1.0 + jnp.exp(-x))


def _silu(x):
    return x * _sigmoid(x)


def _dot(a, b):
    return jnp.dot(a, b, preferred_element_type=F32)


def _dot_nt(a, b):
    return lax.dot_general(a, b, (((1,), (1,)), ((), ())), preferred_element_type=F32)


def _dot_tn(a, b):
    return lax.dot_general(a, b, (((0,), (0,)), ((), ())), preferred_element_type=F32)


def _adaln_kernel(c_ref, w_ref, b_ref, o_ref):
    cond = _silu(c_ref[...])
    o_ref[0] = jnp.dot(cond, w_ref[0], precision=HIGHEST, preferred_element_type=F32) + b_ref[0]


def _adaln(c, ada_w, ada_b):
    n_layers, d, n_out = ada_w.shape
    batch = c.shape[0]
    tn = 1536
    return pl.pallas_call(
        _adaln_kernel,
        grid=(n_layers, n_out // tn),
        in_specs=[
            pl.BlockSpec((batch, d), lambda l, j: (0, 0)),
            pl.BlockSpec((1, d, tn), lambda l, j: (l, 0, j)),
            pl.BlockSpec((1, 1, tn), lambda l, j: (l, 0, j)),
        ],
        out_specs=pl.BlockSpec((1, batch, tn), lambda l, j: (l, 0, j)),
        out_shape=jax.ShapeDtypeStruct((n_layers, batch, n_out), F32),
        compiler_params=_params("parallel", "parallel"),
        name="adaln",
    )(c, ada_w, ada_b.reshape(n_layers, 1, n_out))


def _modulated_norm(x, g, shift, scale):
    y = x * lax.rsqrt(jnp.mean(x * x, axis=-1, keepdims=True) + NORM_EPS)
    return (y * g) * (1.0 + scale) + shift


def _normmod_proj_kernel(x_ref, g_ref, sh_ref, sc_ref, *rest, n_w):
    w_refs, o_refs = rest[:n_w], rest[n_w:]
    h = _modulated_norm(x_ref[...], g_ref[...], sh_ref[0], sc_ref[0]).astype(BF16)
    for w_ref, o_ref in zip(w_refs, o_refs):
        o_ref[...] = _dot(h, w_ref[...]).astype(o_ref.dtype)


def _normmod_proj(x2, g, mod_l, shift_idx, seq, weights, out_dtypes, tm=512):
    n, d = x2.shape
    per_batch = seq // tm
    in_specs = [
        pl.BlockSpec((tm, d), lambda i: (i, 0)),
        pl.BlockSpec((1, d), lambda i: (0, 0)),
        pl.BlockSpec((1, 1, d), lambda i: (i // per_batch, 0, shift_idx)),
        pl.BlockSpec((1, 1, d), lambda i: (i // per_batch, 0, shift_idx + 1)),
    ]
    in_specs += [pl.BlockSpec(w.shape, lambda i: (0, 0)) for w in weights]
    out_specs = [pl.BlockSpec((tm, w.shape[1]), lambda i: (i, 0)) for w in weights]
    out_shape = [jax.ShapeDtypeStruct((n, w.shape[1]), dt) for w, dt in zip(weights, out_dtypes)]
    return pl.pallas_call(
        functools.partial(_normmod_proj_kernel, n_w=len(weights)),
        grid=(n // tm,),
        in_specs=in_specs,
        out_specs=out_specs,
        out_shape=out_shape,
        compiler_params=_params("parallel"),
        name="normmod_proj",
    )(x2, g.reshape(1, d), mod_l, mod_l, *weights)


def _outproj_kernel(x_ref, gate_ref, *rest, n_y):
    y_refs, w_refs, o_ref = rest[:n_y], rest[n_y:2 * n_y], rest[2 * n_y]
    acc = _dot(y_refs[0][...], w_refs[0][...])
    for y_ref, w_ref in zip(y_refs[1:], w_refs[1:]):
        acc += _dot(y_ref[...], w_ref[...])
    o_ref[...] = x_ref[...] + gate_ref[0] * acc


def _outproj_residual(x2, mod_l, gate_idx, seq, ys, ws, tm=512):
    n, d = x2.shape
    per_batch = seq // tm
    in_specs = [
        pl.BlockSpec((tm, d), lambda i: (i, 0)),
        pl.BlockSpec((1, 1, d), lambda i: (i // per_batch, 0, gate_idx)),
    ]
    in_specs += [pl.BlockSpec((tm, y.shape[1]), lambda i: (i, 0)) for y in ys]
    in_specs += [pl.BlockSpec(w.shape, lambda i: (0, 0)) for w in ws]
    return pl.pallas_call(
        functools.partial(_outproj_kernel, n_y=len(ys)),
        grid=(n // tm,),
        in_specs=in_specs,
        out_specs=pl.BlockSpec((tm, d), lambda i: (i, 0)),
        out_shape=jax.ShapeDtypeStruct((n, d), F32),
        compiler_params=_params("parallel"),
        name="outproj_residual",
    )(x2, mod_l, *ys, *ws)


def _chunk_cumsum(x, chunk, reverse):
    rows = x.shape[0]
    pos = lax.broadcasted_iota(jnp.int32, x.shape, 0) % chunk
    s = 1
    while s < chunk:
        if reverse:
            x = x + jnp.where(pos < chunk - s, pltpu.roll(x, rows - s, axis=0), 0.0)
        else:
            x = x + jnp.where(pos >= s, pltpu.roll(x, s, axis=0), 0.0)
        s *= 2
    return x


def _hgrn_groups(slabs):
    g_rows, dk = slabs[0][0].shape
    n_chunks = g_rows // A_CHUNK
    ti = lax.broadcasted_iota(jnp.int32, (A_CHUNK, A_CHUNK), 0)
    si = lax.broadcasted_iota(jnp.int32, (A_CHUNK, A_CHUNK), 1)
    chunk_rows = [slice(c * A_CHUNK, (c + 1) * A_CHUNK) for c in range(n_chunks)]
    prep = []
    for q, f, v, lb, st, reverse in slabs:
        fg = lb + (1.0 - lb) * _sigmoid(f)
        k = 1.0 - fg
        b = _chunk_cumsum(jnp.log(fg), A_CHUNK, reverse)
        b3 = b.reshape(n_chunks, A_CHUNK, dk)
        edge = b3[:, 0:1, :] if reverse else b3[:, A_CHUNK - 1:A_CHUNK, :]
        prep.append(dict(
            q_in=(q * jnp.exp(b)).astype(BF16), k_in=(k * jnp.exp(-b)).astype(BF16),
            k_st=(k.reshape(n_chunks, A_CHUNK, dk) * jnp.exp(edge - b3)).astype(BF16),
            decay=jnp.exp(edge), vb=v.astype(BF16), st=st, reverse=reverse,
            mask=(si >= ti) if reverse else (si <= ti)))
    for p in prep:
        p["scores"] = [_dot_nt(p["q_in"][r], p["k_in"][r]) for r in chunk_rows]
        p["dstate"] = [_dot_tn(p["vb"][r], p["k_st"][c]) for c, r in enumerate(chunk_rows)]
    for p in prep:
        p["intra"] = [_dot(jnp.where(p["mask"], s, 0.0).astype(BF16), p["vb"][r])
                      for s, r in zip(p["scores"], chunk_rows)]
    results = []
    for p in prep:
        st = p["st"]
        outs = [None] * n_chunks
        for c in (range(n_chunks - 1, -1, -1) if p["reverse"] else range(n_chunks)):
            outs[c] = p["intra"][c] + _dot_nt(p["q_in"][chunk_rows[c]], st.astype(BF16))
            st = st * p["decay"][c] + p["dstate"][c]
        results.append((jnp.concatenate(outs, axis=0), st))
    return results


def _hgrn_kernel(q_ref, ff_ref, fb_ref, i_ref, g_ref, lb_ref, ng_ref, o_ref, of_ref, ob_ref):
    seq = q_ref.shape[0]
    dk = A_HEAD_DIM
    n_heads = q_ref.shape[1] // dk
    n_groups = seq // A_GROUP
    head_cols = [slice(h * dk, (h + 1) * dk) for h in range(n_heads)]

    def body(j, states):
        rf = pl.ds(pl.multiple_of(j * A_GROUP, A_GROUP), A_GROUP)
        rb = pl.ds(pl.multiple_of((n_groups - 1 - j) * A_GROUP, A_GROUP), A_GROUP)
        slabs = []
        for h, cols in enumerate(head_cols):
            slabs.append((_silu(q_ref[rf, cols]), ff_ref[rf, cols], i_ref[rf, cols], lb_ref[0:1, cols],
                          states[2 * h], False))
            slabs.append((_silu(q_ref[rb, cols]), fb_ref[rb, cols], i_ref[rb, cols], lb_ref[1:2, cols],
                          states[2 * h + 1], True))
        results = _hgrn_groups(slabs)
        for h, cols in enumerate(head_cols):
            of_ref[rf, cols] = results[2 * h][0]
            ob_ref[rb, cols] = results[2 * h + 1][0]
        return tuple(r[1] for r in results)

    zero = jnp.zeros((dk, dk), F32)
    lax.fori_loop(0, n_groups, body, (zero,) * (2 * n_heads))

    def finish(j, carry):
        r = pl.ds(pl.multiple_of(j * A_GROUP, A_GROUP), A_GROUP)
        for cols in head_cols:
            o = of_ref[r, cols] + ob_ref[r, cols]
            o = o * lax.rsqrt(jnp.mean(o * o, axis=-1, keepdims=True) + NORM_EPS) * ng_ref[...]
            o_ref[r, cols] = (o * _silu(g_ref[r, cols])).astype(o_ref.dtype)
        return carry

    lax.fori_loop(0, n_groups, finish, 0)


def _hgrn2(p_a, lb, norm_g, batch, seq):
    n = p_a.shape[0]
    width = p_a.shape[1] // 5
    cols = A_HEADS_PER_STEP * A_HEAD_DIM
    steps = width // cols
    sect = lambda s: pl.BlockSpec((seq, cols), lambda b, h, s=s: (b, s * steps + h))
    return pl.pallas_call(
        _hgrn_kernel,
        grid=(batch, steps),
        in_specs=[sect(0), sect(1), sect(2), sect(3), sect(4),
                  pl.BlockSpec((2, cols), lambda b, h: (0, h)),
                  pl.BlockSpec((1, A_HEAD_DIM), lambda b, h: (0, 0))],
        out_specs=pl.BlockSpec((seq, cols), lambda b, h: (b, h)),
        out_shape=jax.ShapeDtypeStruct((n, width), BF16),
        scratch_shapes=[pltpu.VMEM((seq, cols), F32), pltpu.VMEM((seq, cols), F32)],
        compiler_params=_params("parallel", "parallel"),
        name="hgrn2",
    )(p_a, p_a, p_a, p_a, p_a, lb, norm_g.reshape(1, A_HEAD_DIM))


def _softplus(z):
    return jnp.maximum(z, 0.0) + jnp.log(1.0 + jnp.exp(-jnp.abs(z)))


def _rwkv_prep_kernel(p_ref, prev_ref, next_ref, mu_ref, w0_ref, w2_ref, a0_ref, a2_ref, g2_ref,
                      kk_ref, ka_ref, rk_ref, ones_ref, tri_ref,
                      r_out, k_out, v_out, kkn_out, b_out, cumf_out, cumb_out, g_out, bonus_out):
    i = pl.program_id(1)
    last = pl.num_programs(1) - 1
    ts = p_ref.shape[0]
    width = a0_ref.shape[1]
    heads = width // B_HEAD_DIM
    p = p_ref[...]
    prev_row = jnp.where(i == 0, 0.0, prev_ref[7:8, :])
    next_row = jnp.where(i == last, 0.0, next_ref[0:1, :])
    row = lax.broadcasted_iota(jnp.int32, (ts, 1), 0)
    up = jnp.where(row == 0, prev_row, pltpu.roll(p, 1, axis=0))
    dn = jnp.where(row == ts - 1, next_row, pltpu.roll(p, ts - 1, axis=0))
    p = p + mu_ref[...] * (0.5 * (up + dn) - p)

    r = p[:, 0:width]
    k = p[:, width:2 * width]
    v = p[:, 2 * width:3 * width]
    o = 3 * width
    wlo_f = p[:, o:o + 64]
    wlo_b = p[:, o + 64:o + 128]
    alo = p[:, o + 128:o + 192]
    glo = p[:, o + 192:o + 320]

    def log_decay(wlo, d):
        z = w0_ref[d:d + 1, :] + _dot(jnp.tanh(wlo).astype(BF16), w2_ref[d])
        w = -_softplus(-z) - 0.5
        return -jnp.exp(w)

    def split_dot(sel, t, right):
        hi = t.astype(BF16)
        lo = (t - hi.astype(F32)).astype(BF16)
        return (_dot(hi, sel) + _dot(lo, sel)) if right else (_dot(sel, hi) + _dot(sel, lo))

    cum_f = split_dot(tri_ref[0], log_decay(wlo_f, 0), False)
    cum_b = split_dot(tri_ref[1], log_decay(wlo_b, 1), False)
    a = _sigmoid(a0_ref[...] + _dot(alo.astype(BF16), a2_ref[...]))
    g = _dot(_sigmoid(glo).astype(BF16), g2_ref[...])
    kk = k * kk_ref[...]
    head_sum = lambda t: split_dot(ones_ref[...], t, True)

    kk_n = kk / jnp.maximum(jnp.sqrt(head_sum(kk * kk)), 1e-12)
    k_mod = k * (1.0 + (a - 1.0) * ka_ref[...])
    bonus = head_sum(r * k_mod * rk_ref[...]) * v
    g_out[...] = g
    bonus_out[...] = bonus
    for h in range(heads):
        sl = slice(h * B_HEAD_DIM, (h + 1) * B_HEAD_DIM)
        r_out[0, h] = r[:, sl]
        k_out[0, h] = k_mod[:, sl]
        v_out[0, h] = v[:, sl]
        kkn_out[0, h] = kk_n[:, sl]
        b_out[0, h] = (kk_n * a)[:, sl]
        cumf_out[0, h] = cum_f[:, sl]
        cumb_out[0, h] = cum_b[:, sl]


def _rwkv_prep(p_b, mu, w0, w2, a0, a2, g2, k_k, k_a, r_k, batch, seq, ts=256):
    n, cols = p_b.shape
    width = a0.shape[0]
    heads = width // B_HEAD_DIM
    nblk = seq // ts
    rows8 = ts // 8
    head_id = np.arange(width) // B_HEAD_DIM
    ones = jnp.asarray(head_id[:, None] == head_id[None, :], BF16)
    full = lambda a: pl.BlockSpec(a.shape, lambda b, i: (0,) * a.ndim)
    vec = lambda a: a.reshape(1, -1)
    row = np.arange(ts)
    same_chunk = (row[:, None] // B_CHUNK) == (row[None, :] // B_CHUNK)
    tri = jnp.asarray(np.stack([same_chunk & (row[None, :] <= row[:, None]),
                                same_chunk & (row[None, :] >= row[:, None])]), BF16)
    args = [vec(mu), w0, w2.astype(BF16), vec(a0), a2.astype(BF16), g2.astype(BF16),
            vec(k_k), vec(k_a), vec(r_k), ones, tri]
    hm = pl.BlockSpec((1, heads, ts, B_HEAD_DIM), lambda b, i: (b, 0, i, 0))
    tokm = pl.BlockSpec((ts, width), lambda b, i: (b * nblk + i, 0))
    hm_shape = jax.ShapeDtypeStruct((batch, heads, seq, B_HEAD_DIM), F32)
    tok_shape = jax.ShapeDtypeStruct((n, width), F32)
    return pl.pallas_call(
        _rwkv_prep_kernel,
        grid=(batch, nblk),
        in_specs=[
            pl.BlockSpec((ts, cols), lambda b, i: (b * nblk + i, 0)),
            pl.BlockSpec((8, cols), lambda b, i: (jnp.maximum((b * nblk + i) * rows8 - 1, 0), 0)),
            pl.BlockSpec((8, cols), lambda b, i: (jnp.minimum((b * nblk + i + 1) * rows8, n // 8 - 1), 0)),
        ] + [full(a) for a in args],
        out_specs=[hm] * 7 + [tokm] * 2,
        out_shape=[hm_shape] * 7 + [tok_shape] * 2,
        compiler_params=_params("parallel", "parallel"),
        name="rwkv_prep",
    )(p_b, p_b, p_b, *args)


def _rwkv_chunks(chains):
    c, d = chains[0][0].shape
    ti = lax.broadcasted_iota(jnp.int32, (c, c), 0)
    si = lax.broadcasted_iota(jnp.int32, (c, c), 1)
    eye = jnp.where(ti == si, 1.0, 0.0)
    masks = {rev: (((si > ti), (si >= ti)) if rev else ((si < ti), (si <= ti))) for rev in (False, True)}
    t2 = lax.broadcasted_iota(jnp.int32, (c, 2 * c), 0)
    s2 = lax.broadcasted_iota(jnp.int32, (c, 2 * c), 1)
    s2 = jnp.where(s2 >= c, s2 - c, s2)
    incl2 = {False: s2 <= t2, True: s2 >= t2}

    row = lax.broadcasted_iota(jnp.int32, (c, d), 0)
    st = []
    for r, k, v, kk, b, cum, rev in chains:
        before = (jnp.where(row == c - 1, 0.0, pltpu.roll(cum, c - 1, axis=0)) if rev
                  else jnp.where(row == 0, 0.0, pltpu.roll(cum, 1, axis=0)))
        grow = jnp.exp(-cum)
        kt = (kk * jnp.exp(before)).astype(BF16)
        rt = r * jnp.exp(cum)
        b_ = (b * grow).astype(BF16)
        kb = jnp.concatenate([(k * grow).astype(BF16), b_], axis=0)
        st.append(dict(vb=v.astype(BF16), kt=kt, rt=rt, b_=b_, kb=kb,
                       decay=jnp.exp(cum[0:1, :] if rev else cum[c - 1:c, :]), rev=rev,
                       lhs=jnp.concatenate([kt, rt.astype(BF16)], axis=0)))
    for s in st:
        s["big"] = _dot_nt(s["lhs"], s["kb"])
    for s in st:
        strict, incl = masks[s["rev"]]
        big = s["big"]
        a_ab = jnp.where(strict, big[:c, c:], 0.0)
        s["a_ak"] = jnp.where(strict, big[:c, :c], 0.0).astype(BF16)
        s["p"] = jnp.where(incl2[s["rev"]], big[c:], 0.0).astype(BF16)
        s["m"] = eye - a_ab
        s["ab"] = a_ab.astype(BF16)
    for s in st:
        s["aj"] = _dot(s["ab"], s["ab"])
        s["av"] = _dot(s["a_ak"], s["vb"])
    span = 2
    while span < c:
        span *= 2
        for s in st:
            ajb = s["aj"].astype(BF16)
            if span < c:
                both = _dot(jnp.concatenate([ajb, s["m"].astype(BF16)], axis=0), ajb)
                s["aj"], s["m"] = both[:c], s["m"] + both[c:]
            else:
                s["m"] = s["m"] + _dot(s["m"].astype(BF16), ajb)
    for s in st:
        wu = _dot(s["m"].astype(BF16), jnp.concatenate([s["kt"], s["av"].astype(BF16)], axis=1))
        s["nwu"] = -wu.astype(BF16)
    for s in st:
        nw, nu0, vb = s["nwu"][:, :d], s["nwu"][:, d:], s["vb"]
        rhs = jnp.concatenate([jnp.concatenate([vb, jnp.zeros_like(vb)], axis=1),
                               jnp.concatenate([nu0, nw], axis=1)], axis=0)
        s["yq"] = _dot(s["p"], rhs)
        s["h_t"] = _dot_tn(jnp.concatenate([vb, nu0], axis=0), s["kb"])
        s["nbw"] = _dot_tn(s["b_"], nw)
    return [dict(q=(s["rt"] + s["yq"][:, d:]).astype(BF16), y1=s["yq"][:, :d], nbw=s["nbw"].astype(BF16),
                 h_t=s["h_t"], decay=s["decay"]) for s in st]


def _rwkv_advance(states, steps):
    ttbs = [tt.astype(BF16) for tt in states]
    ys = [_dot_nt(s["q"], ttb) + s["y1"] for s, ttb in zip(steps, ttbs)]
    new = [(tt + _dot_nt(ttb, s["nbw"]) + s["h_t"]) * s["decay"] for tt, ttb, s in zip(states, ttbs, steps)]
    return ys, new


def _rwkv_scan_kernel(rf, kf, vf, kkf, bf, lwf, rb, kb, vb, kkb, bb, lwb, yf_ref, yb_ref, tf_ref, tb_ref):
    heads, ts = rf.shape[1], rf.shape[2]
    n_chunks = ts // B_CHUNK

    @pl.when(pl.program_id(1) == 0)
    def _():
        tf_ref[...] = jnp.zeros_like(tf_ref)
        tb_ref[...] = jnp.zeros_like(tb_ref)

    def body(j, carry):
        rows_f = [pl.ds(pl.multiple_of((j * B_STEPS + s) * B_CHUNK, B_CHUNK), B_CHUNK) for s in range(B_STEPS)]
        rows_b = [pl.ds(pl.multiple_of((n_chunks - 1 - j * B_STEPS - s) * B_CHUNK, B_CHUNK), B_CHUNK)
                  for s in range(B_STEPS)]
        chains = []
        for s in range(B_STEPS):
            for h in range(heads):
                chains.append([ref[0, h, rows_f[s], :] for ref in (rf, kf, vf, kkf, bf, lwf)] + [False])
                chains.append([ref[0, h, rows_b[s], :] for ref in (rb, kb, vb, kkb, bb, lwb)] + [True])
        solved = _rwkv_chunks(chains)
        states = []
        for h in range(heads):
            states += [tf_ref[h], tb_ref[h]]
        for s in range(B_STEPS):
            ys, states = _rwkv_advance(states, solved[s * 2 * heads:(s + 1) * 2 * heads])
            for h in range(heads):
                yf_ref[0, h, rows_f[s], :] = ys[2 * h]
                yb_ref[0, h, rows_b[s], :] = ys[2 * h + 1]
        for h in range(heads):
            tf_ref[h], tb_ref[h] = states[2 * h], states[2 * h + 1]
        return carry

    lax.fori_loop(0, n_chunks // B_STEPS, body, 0)


def _rwkv_scan(r, k, v, kkn, b, lw_f, lw_b, ts=256):
    batch, heads, seq, d = r.shape
    nblk = seq // ts
    fwd = pl.BlockSpec((1, heads, ts, d), lambda bi, i: (bi, 0, i, 0))
    bwd = pl.BlockSpec((1, heads, ts, d), lambda bi, i: (bi, 0, nblk - 1 - i, 0))
    shape = jax.ShapeDtypeStruct(r.shape, F32)
    return pl.pallas_call(
        _rwkv_scan_kernel,
        grid=(batch, nblk),
        in_specs=[fwd] * 6 + [bwd] * 6,
        out_specs=[fwd, bwd],
        out_shape=[shape, shape],
        scratch_shapes=[pltpu.VMEM((heads, d, d), F32), pltpu.VMEM((heads, d, d), F32)],
        compiler_params=_params("parallel", "arbitrary"),
        name="rwkv_scan",
    )(r, k, v, kkn, b, lw_f, r, k, v, kkn, b, lw_b)


def _rwkv_post_kernel(yf_ref, yb_ref, g_ref, bonus_ref, lng_ref, lnb_ref, o_ref):
    heads = yf_ref.shape[1]
    outs = []
    for h in range(heads):
        y = yf_ref[0, h] + yb_ref[0, h]
        mean = jnp.mean(y, axis=-1, keepdims=True)
        var = jnp.mean(jnp.square(y - mean), axis=-1, keepdims=True)
        outs.append((y - mean) * lax.rsqrt(var + B_LN_EPS))
    yn = jnp.concatenate(outs, axis=1) * lng_ref[...] + lnb_ref[...]
    o_ref[...] = ((yn + bonus_ref[...]) * g_ref[...]).astype(o_ref.dtype)


def _rwkv_post(y_f, y_b, g, bonus, ln_g, ln_b, ts=256):
    batch, heads, seq, d = y_f.shape
    n, width = g.shape
    nblk = seq // ts
    hm = pl.BlockSpec((1, heads, ts, d), lambda b, i: (b, 0, i, 0))
    tokm = pl.BlockSpec((ts, width), lambda b, i: (b * nblk + i, 0))
    vec = pl.BlockSpec((1, width), lambda b, i: (0, 0))
    return pl.pallas_call(
        _rwkv_post_kernel,
        grid=(batch, nblk),
        in_specs=[hm, hm, tokm, tokm, vec, vec],
        out_specs=tokm,
        out_shape=jax.ShapeDtypeStruct((n, width), BF16),
        compiler_params=_params("parallel", "parallel"),
        name="rwkv_post",
    )(y_f, y_b, g, bonus, ln_g.reshape(1, width), ln_b.reshape(1, width))


def _rwkv7(p_b, mu, w0, w2, a0, a2, g2, k_k, k_a, r_k, ln_g, ln_b, batch, seq):
    r, k, v, kkn, b, lw_f, lw_b, g, bonus = _rwkv_prep(p_b, mu, w0, w2, a0, a2, g2, k_k, k_a, r_k, batch, seq)
    y_f, y_b = _rwkv_scan(r, k, v, kkn, b, lw_f, lw_b)
    return _rwkv_post(y_f, y_b, g, bonus, ln_g, ln_b)


_BUCKET_EDGES = (0, 1, 2, 3, 4, 5, 6, 7, 8, 12, 16, 23, 32, 46, 64, 91)
ATTN_TK = 128
C_HEADS_PER_STEP = 4


def _bias_band_kernel(table_ref, o_ref):
    h = pl.program_id(0)
    n_d, tq, tk = o_ref.shape[1:]
    half = REL_BUCKETS // 2
    r = lax.broadcasted_iota(jnp.int32, (tq, tk), 0)
    c = lax.broadcasted_iota(jnp.int32, (tq, tk), 1)
    for d in range(n_d):
        rel = tk * (d - 1) + c - r
        n = jnp.abs(rel)
        vals = []
        for sign in range(2):
            val = jnp.full((tq, tk), table_ref[sign * half + half - 1, h], F32)
            for bkt in range(half - 2, -1, -1):
                val = jnp.where(n < _BUCKET_EDGES[bkt + 1], table_ref[sign * half + bkt, h], val)
            vals.append(val)
        o_ref[0, d] = jnp.where(rel > 0, vals[1], vals[0])


def _bias_band(rel_table, tq):
    heads = rel_table.shape[1]
    n_d = tq // ATTN_TK + 2
    return pl.pallas_call(
        _bias_band_kernel,
        grid=(heads,),
        in_specs=[pl.BlockSpec(memory_space=pltpu.SMEM)],
        out_specs=pl.BlockSpec((1, n_d, tq, ATTN_TK), lambda h: (h, 0, 0, 0)),
        out_shape=jax.ShapeDtypeStruct((heads, n_d, tq, ATTN_TK), F32),
        compiler_params=_params("parallel"),
        name="bias_band",
    )(rel_table)


def _diff_attn_kernel(table_ref, lam_ref, q_ref, k_ref, v_ref, band_ref, g_ref, o_ref, lg_ref, vext_ref, kt_ref,
                      *, lam_init):
    i = pl.program_id(2)
    tq = q_ref.shape[0]
    dh2 = 2 * C_HEAD_DIM
    n_heads = q_ref.shape[1] // dh2
    seq = k_ref.shape[0]
    n_d = band_ref.shape[1]
    n_kt = seq // ATTN_TK
    ratio = tq // ATTN_TK
    half = REL_BUCKETS // 2
    head_cols = [slice(hh * dh2, (hh + 1) * dh2) for hh in range(n_heads)]

    @pl.when(i == 0)
    def _():
        for hh, cols in enumerate(head_cols):
            kt_ref[hh] = k_ref[:, cols].astype(F32).T.astype(BF16)
            v = v_ref[:, cols]
            vext_ref[hh] = jnp.concatenate([v, jnp.ones_like(v)], axis=1)

    lane = lax.broadcasted_iota(jnp.int32, (1, dh2), 1)
    first_band = ratio * i - 1
    col = lax.broadcasted_iota(jnp.int32, (1, seq), 1)
    for hh, cols in enumerate(head_cols):
        h = pl.program_id(1) * n_heads + hh
        q = q_ref[:, cols] * jnp.asarray(C_HEAD_DIM ** -0.5, q_ref.dtype)
        zero = jnp.zeros_like(q)
        far_row = jnp.where(col < first_band * ATTN_TK, table_ref[half - 1, h],
                            jnp.where(col >= (first_band + n_d) * ATTN_TK, table_ref[REL_BUCKETS - 1, h],
                                      -jnp.inf))
        row_max = []
        for m in range(2):
            qm = jnp.where((lane >= m * C_HEAD_DIM) & (lane < (m + 1) * C_HEAD_DIM), q, zero)
            far = _dot(qm, kt_ref[hh]) + far_row
            mx = jnp.max(far, axis=-1, keepdims=True)
            lg_ref[hh, m] = far
            for d in (0, n_d - 1) + tuple(range(1, n_d - 1)):
                kt = first_band + d
                valid = (kt >= 0) & (kt < n_kt)
                c0 = pl.multiple_of(jnp.clip(kt, 0, n_kt - 1) * ATTN_TK, ATTN_TK)
                near = _dot(qm, kt_ref[hh, :, pl.ds(c0, ATTN_TK)]) + band_ref[hh, d]
                mx = jnp.maximum(mx, jnp.where(valid, jnp.max(near, axis=-1, keepdims=True), -jnp.inf))
                lg_ref[hh, m, :, pl.ds(c0, ATTN_TK)] = near
            row_max.append(mx)
        parts = []
        for m in range(2):
            e = jnp.exp(lg_ref[hh, m] - row_max[m]).astype(BF16)
            pv = _dot(e, vext_ref[hh])
            parts.append(pv[:, :dh2] * (1.0 / pv[:, dh2:dh2 + 1]))
        o = parts[0] - lam_ref[0] * parts[1]
        o = o * lax.rsqrt(jnp.mean(o * o, axis=-1, keepdims=True) + NORM_EPS) * g_ref[...]
        o_ref[:, cols] = (o * (1.0 - lam_init)).astype(o_ref.dtype)


def _diff_attention(qkv, rel_table, lam_full, subln_g, lam_init, batch, seq, tq=256):
    n = qkv.shape[0]
    dh2 = 2 * C_HEAD_DIM
    heads = qkv.shape[1] // (3 * dh2)
    nq = seq // tq
    per = C_HEADS_PER_STEP
    groups = heads // per
    cols = per * dh2
    band = _bias_band(rel_table, tq)
    return pl.pallas_call(
        functools.partial(_diff_attn_kernel, lam_init=lam_init),
        grid=(batch, groups, nq),
        in_specs=[
            pl.BlockSpec(memory_space=pltpu.SMEM),
            pl.BlockSpec(memory_space=pltpu.SMEM),
            pl.BlockSpec((tq, cols), lambda b, h, i: (b * nq + i, h)),
            pl.BlockSpec((seq, cols), lambda b, h, i: (b, groups + h)),
            pl.BlockSpec((seq, cols), lambda b, h, i: (b, 2 * groups + h)),
            pl.BlockSpec((per,) + band.shape[1:], lambda b, h, i: (h, 0, 0, 0)),
            pl.BlockSpec((1, dh2), lambda b, h, i: (0, 0)),
        ],
        out_specs=pl.BlockSpec((tq, cols), lambda b, h, i: (b * nq + i, h)),
        out_shape=jax.ShapeDtypeStruct((n, heads * dh2), BF16),
        scratch_shapes=[pltpu.VMEM((per, 2, tq, seq), F32), pltpu.VMEM((per, seq, 2 * dh2), BF16),
                        pltpu.VMEM((per, dh2, seq), BF16)],
        compiler_params=_params("parallel", "parallel", "arbitrary"),
        name="diff_attention",
    )(rel_table, lam_full.reshape(1), qkv, qkv, qkv, band, subln_g.reshape(1, dh2))


MOE_TILE = 256
EXPERT_TILE = 512
EXPERT_CHUNK = 512
_HI_MASK = np.uint32(0xFFFF0000)


def _pack_bf16_pair(lo, hi):
    as_bits = lambda t: lax.bitcast_convert_type(t.astype(BF16).astype(F32), jnp.uint32)
    return (as_bits(hi) & _HI_MASK) | (as_bits(lo) >> 16)


def _unpack_bf16_pair(word):
    lo = lax.bitcast_convert_type(word << 16, F32).astype(BF16)
    hi = lax.bitcast_convert_type(word & _HI_MASK, F32).astype(BF16)
    return lo, hi


def _router_kernel(x_ref, g_ref, sh_ref, sc_ref, rw_ref, rb_ref, hp_ref, idx_ref, w_ref, rank_ref, cnt_ref,
                   run_ref, *, n_experts):
    @pl.when(pl.program_id(0) == 0)
    def _():
        run_ref[...] = jnp.zeros_like(run_ref)

    tm, d = x_ref.shape
    h = _modulated_norm(x_ref[...], g_ref[...], sh_ref[0], sc_ref[0])
    hp_ref[...] = _pack_bf16_pair(h[:, :d // 2], h[:, d // 2:])
    logits = _dot(h.astype(BF16), rw_ref[...]) + rb_ref[...]
    lane = lax.broadcasted_iota(jnp.int32, logits.shape, 1).astype(F32)
    neg = jnp.float32(-jnp.inf)
    logits = jnp.where(lane < n_experts, logits, neg)
    picks, vals = [], []
    for _ in range(TOP_K):
        m = jnp.max(logits, axis=-1, keepdims=True)
        first = jnp.min(jnp.where(logits == m, lane, float(LANES)), axis=-1, keepdims=True)
        hit = lane == first
        picks.append((first, hit))
        vals.append(m)
        logits = jnp.where(hit, neg, logits)
    es = [jnp.exp(v - vals[0]) for v in vals]
    inv = 1.0 / (es[0] + es[1] + es[2] + es[3])
    assigned = picks[0][1] | picks[1][1] | picks[2][1] | picks[3][1]
    onehot = jnp.where(assigned, 1.0, 0.0)
    ri = lax.broadcasted_iota(jnp.int32, (tm, tm), 0)
    ci = lax.broadcasted_iota(jnp.int32, (tm, tm), 1)
    earlier = jnp.where(ci < ri, 1.0, 0.0).astype(BF16)
    before = _dot(earlier, onehot.astype(BF16)) + run_ref[...]
    idx_o = jnp.zeros(logits.shape, F32)
    w_o = jnp.zeros(logits.shape, F32)
    rank_o = jnp.zeros(logits.shape, F32)
    for j, (first, hit) in enumerate(picks):
        rank_j = jnp.sum(jnp.where(hit, before, 0.0), axis=-1, keepdims=True)
        idx_o = jnp.where(lane == j, first, idx_o)
        w_o = jnp.where(lane == j, es[j] * inv, w_o)
        rank_o = jnp.where(lane == j, rank_j, rank_o)
    idx_ref[...] = idx_o
    w_ref[...] = w_o
    rank_ref[...] = rank_o
    run_ref[...] += jnp.sum(onehot, axis=0, keepdims=True)
    cnt_ref[...] = run_ref[...]


def _router(x2, g, mod_l, shift_idx, seq, router_w, router_b):
    n, d = x2.shape
    tm = MOE_TILE
    n_experts = router_w.shape[1]
    per_batch = seq // tm
    rw = jnp.zeros((d, LANES), BF16).at[:, :n_experts].set(router_w.astype(BF16))
    rb = jnp.zeros((1, LANES), F32).at[0, :n_experts].set(router_b)
    lane_out = pl.BlockSpec((tm, LANES), lambda i: (i, 0))
    lane_shape = jax.ShapeDtypeStruct((n, LANES), F32)
    return pl.pallas_call(
        functools.partial(_router_kernel, n_experts=n_experts),
        grid=(n // tm,),
        in_specs=[
            pl.BlockSpec((tm, d), lambda i: (i, 0)),
            pl.BlockSpec((1, d), lambda i: (0, 0)),
            pl.BlockSpec((1, 1, d), lambda i: (i // per_batch, 0, shift_idx)),
            pl.BlockSpec((1, 1, d), lambda i: (i // per_batch, 0, shift_idx + 1)),
            pl.BlockSpec((d, LANES), lambda i: (0, 0)),
            pl.BlockSpec((1, LANES), lambda i: (0, 0)),
        ],
        out_specs=[pl.BlockSpec((tm, d // 2), lambda i: (i, 0)), lane_out, lane_out, lane_out,
                   pl.BlockSpec((1, LANES), lambda i: (0, 0))],
        out_shape=[jax.ShapeDtypeStruct((n, d // 2), jnp.uint32), lane_shape, lane_shape, lane_shape,
                   jax.ShapeDtypeStruct((1, LANES), F32)],
        scratch_shapes=[pltpu.VMEM((1, LANES), F32)],
        compiler_params=_params("arbitrary"),
        name="moe_router",
    )(x2, g.reshape(1, d), mod_l, mod_l, rw, rb)


GLU_TILE = 2 * LANES


def _experts_kernel(te_ref, nu_ref, slot_ref, nxt_ref, xs_ref, b1_ref, b2_ref, perm_ref, w1_hbm, w2_hbm,
                    ys_ref, w1buf, w2buf, w1p_ref, w2b_ref, sem, *, layer):
    t = pl.program_id(0)
    used = t < nu_ref[0]
    new_expert = (t == 0) | (te_ref[t] != te_ref[jnp.maximum(t - 1, 0)])
    n_col_tiles = w1p_ref.shape[1] // GLU_TILE

    def weight_copies(e, slot):
        return (pltpu.make_async_copy(w1_hbm.at[layer, e], w1buf.at[slot], sem.at[0, slot]),
                pltpu.make_async_copy(w2_hbm.at[layer, e], w2buf.at[slot], sem.at[1, slot]))

    @pl.when(used & new_expert)
    def _():
        slot = slot_ref[t]

        @pl.when(t == 0)
        def _():
            for cp in weight_copies(te_ref[t], slot):
                cp.start()

        for cp in weight_copies(te_ref[t], slot):
            cp.wait()

        @pl.when(nxt_ref[t] >= 0)
        def _():
            for cp in weight_copies(nxt_ref[t], 1 - slot):
                cp.start()

        for c in range(n_col_tiles):
            cols = slice(c * GLU_TILE, (c + 1) * GLU_TILE)
            w1p_ref[:, cols] = _dot(w1buf[slot, :, cols].astype(BF16), perm_ref[...]).astype(BF16)
        w2b_ref[...] = w2buf[slot].astype(BF16)

    @pl.when(used)
    def _():
        lo, hi = _unpack_bf16_pair(xs_ref[...])
        x = jnp.concatenate([lo, hi], axis=1)
        out = b2_ref[0]
        for j in range(n_col_tiles * LANES // EXPERT_CHUNK):
            hcols = slice(2 * j * EXPERT_CHUNK, 2 * (j + 1) * EXPERT_CHUNK)
            hh = _dot(x, w1p_ref[:, hcols]) + b1_ref[0, :, hcols]
            acts = []
            for c in range(2 * EXPERT_CHUNK // GLU_TILE):
                x_glu = jnp.minimum(hh[:, c * GLU_TILE:c * GLU_TILE + LANES], SWIGLU_LIMIT)
                x_lin = jnp.clip(hh[:, c * GLU_TILE + LANES:(c + 1) * GLU_TILE], -SWIGLU_LIMIT, SWIGLU_LIMIT)
                acts.append((x_glu * _sigmoid(SWIGLU_ALPHA * x_glu) * (x_lin + 1.0)).astype(BF16))
            out = out + _dot(jnp.concatenate(acts, axis=1), w2b_ref[j * EXPERT_CHUNK:(j + 1) * EXPERT_CHUNK, :])
        half = out.shape[1] // 2
        ys_ref[...] = _pack_bf16_pair(out[:, :half], out[:, half:])

    @pl.when(t >= nu_ref[0])
    def _():
        ys_ref[...] = jnp.zeros_like(ys_ref)


def _experts(xs, tile_expert, n_used, slot, nxt, w1_all, b1, w2_all, b2, layer):
    n_rows, dw = xs.shape
    tm = EXPERT_TILE
    _, n_exp, d, de2 = w1_all.shape
    src = np.arange(GLU_TILE)
    dst = np.where(src % 2 == 0, src // 2, LANES + src // 2)
    perm = jnp.asarray(dst[:, None] == np.arange(GLU_TILE)[None, :], BF16)
    b1p = b1.reshape(n_exp, de2 // GLU_TILE, LANES, 2).transpose(0, 1, 3, 2).reshape(n_exp, 1, de2)
    row = lambda t, te, nu, sl, nx: (jnp.maximum(jnp.minimum(t, nu[0] - 1), 0), 0)
    exp3 = lambda t, te, nu, sl, nx: (te[t], 0, 0)
    grid_spec = pltpu.PrefetchScalarGridSpec(
        num_scalar_prefetch=4,
        grid=(n_rows // tm,),
        in_specs=[
            pl.BlockSpec((tm, dw), row),
            pl.BlockSpec((1, 1, de2), exp3),
            pl.BlockSpec((1, 1, d), exp3),
            pl.BlockSpec((GLU_TILE, GLU_TILE), lambda t, te, nu, sl, nx: (0, 0)),
            pl.BlockSpec(memory_space=pl.ANY),
            pl.BlockSpec(memory_space=pl.ANY),
        ],
        out_specs=pl.BlockSpec((tm, d // 2), lambda t, te, nu, sl, nx: (t, 0)),
        scratch_shapes=[pltpu.VMEM((2, d, de2), F32), pltpu.VMEM((2, de2 // 2, d), F32),
                        pltpu.VMEM((d, de2), BF16), pltpu.VMEM((de2 // 2, d), BF16),
                        pltpu.SemaphoreType.DMA((2, 2))],
    )
    return pl.pallas_call(
        functools.partial(_experts_kernel, layer=layer),
        grid_spec=grid_spec,
        out_shape=jax.ShapeDtypeStruct((n_rows, d // 2), jnp.uint32),
        compiler_params=_params("arbitrary"),
        name="moe_experts",
    )(tile_expert, n_used, slot, nxt, xs, b1p, b2.reshape(n_exp, 1, d), perm, w1_all, w2_all)


SC_WINDOW = 128


def _sc_worker_rows(n_rows):
    sc = plsc.get_sparse_core_info()
    per_worker = n_rows // (sc.num_cores * sc.num_subcores)
    worker = lax.axis_index("subcore") * sc.num_cores + lax.axis_index("core")
    return worker * per_worker, per_worker


def _sc_mesh():
    return plsc.VectorSubcoreMesh(core_axis_name="core", subcore_axis_name="subcore")


def _sc_gather_rows(table, idx):
    n_idx, width = idx.shape[0], table.shape[1]

    @functools.partial(
        pl.kernel, out_type=jax.ShapeDtypeStruct((n_idx, width), table.dtype), mesh=_sc_mesh(),
        scratch_types=[pltpu.VMEM((SC_WINDOW,), jnp.int32), pltpu.VMEM((SC_WINDOW, width), table.dtype)],
        name="sc_gather_rows")
    def gather(table_hbm, idx_hbm, out_hbm, idx_v, rows_v):
        first, count = _sc_worker_rows(n_idx)

        @pl.loop(0, count // SC_WINDOW)
        def _(c):
            rows = pl.ds(first + c * SC_WINDOW, SC_WINDOW)
            pltpu.sync_copy(idx_hbm.at[rows], idx_v)
            pltpu.sync_copy(table_hbm.at[idx_v], rows_v)
            pltpu.sync_copy(rows_v, out_hbm.at[rows])

    return gather(table, idx)


def _sc_scatter_rows(rows, pos, n_out):
    n, width = rows.shape
    k = pos.shape[0]

    @functools.partial(
        pl.kernel, out_type=jax.ShapeDtypeStruct((n_out, width), rows.dtype), mesh=_sc_mesh(),
        scratch_types=[pltpu.VMEM((k, SC_WINDOW), jnp.int32), pltpu.VMEM((SC_WINDOW, width), rows.dtype)],
        name="sc_scatter_rows")
    def scatter(rows_hbm, pos_hbm, out_hbm, pos_v, rows_v):
        first, count = _sc_worker_rows(n)

        @pl.loop(0, count // SC_WINDOW)
        def _(c):
            src = pl.ds(first + c * SC_WINDOW, SC_WINDOW)
            pltpu.sync_copy(pos_hbm.at[:, src], pos_v)
            pltpu.sync_copy(rows_hbm.at[src], rows_v)
            for j in range(k):
                pltpu.sync_copy(rows_v, out_hbm.at[pos_v.at[j]])

    return scatter(rows, pos)


def _combine_dense_kernel(x_ref, gate_ref, w_ref, fg_ref, *rest, final_norm):
    y_refs, o_ref = rest[:TOP_K], rest[TOP_K]
    w = w_ref[...]
    y_lo = y_hi = None
    for j in range(TOP_K):
        lo, hi = _unpack_bf16_pair(y_refs[j][...])
        wj = w[:, j:j + 1]
        y_lo = wj * lo if j == 0 else y_lo + wj * lo
        y_hi = wj * hi if j == 0 else y_hi + wj * hi
    out = x_ref[...] + gate_ref[0] * jnp.concatenate([y_lo, y_hi], axis=1)
    if final_norm:
        out = out * lax.rsqrt(jnp.mean(out * out, axis=-1, keepdims=True) + NORM_EPS) * fg_ref[...]
    o_ref[...] = out


def _combine_dense(x2, mod_l, gate_idx, seq, top_w, y4, final_g, final_norm, tm=512):
    n, d = x2.shape
    per_batch = seq // tm
    blocks = n // tm
    slot_spec = lambda j: pl.BlockSpec((tm, d // 2), lambda i: (j * blocks + i, 0))
    return pl.pallas_call(
        functools.partial(_combine_dense_kernel, final_norm=final_norm),
        grid=(blocks,),
        in_specs=[
            pl.BlockSpec((tm, d), lambda i: (i, 0)),
            pl.BlockSpec((1, 1, d), lambda i: (i // per_batch, 0, gate_idx)),
            pl.BlockSpec((tm, LANES), lambda i: (i, 0)),
            pl.BlockSpec((1, d), lambda i: (0, 0)),
        ] + [slot_spec(j) for j in range(TOP_K)],
        out_specs=pl.BlockSpec((tm, d), lambda i: (i, 0)),
        out_shape=jax.ShapeDtypeStruct((n, d), F32),
        compiler_params=_params("parallel"),
        name="moe_combine",
    )(x2, mod_l, top_w, final_g.reshape(1, d), *([y4] * TOP_K))


def _moe_ffn(x2, norm_g, mod_l, seq, router_w, router_b, w1_all, b1, w2_all, b2, layer, final_g, final_norm):
    n, d = x2.shape
    n_exp = router_w.shape[1]
    tile = EXPERT_TILE
    hp, idx_f, top_w, rank_f, counts = _router(x2, norm_g, mod_l, 3, seq, router_w, router_b)
    counts = counts[0, :n_exp].astype(jnp.int32)
    padded = (counts + tile - 1) // tile * tile
    ends = jnp.cumsum(padded)
    starts = ends - padded
    idx = idx_f[:, :TOP_K].astype(jnp.int32)
    expert_ids = jnp.arange(n_exp, dtype=jnp.int32)
    start_of = jnp.sum(jnp.where(idx[..., None] == expert_ids, starts, 0), axis=-1)
    pos_flat = (start_of + rank_f[:, :TOP_K].astype(jnp.int32)).reshape(-1)
    n_rows = n * TOP_K + n_exp * tile
    n_tiles = n_rows // tile
    tile_start = jnp.arange(n_tiles, dtype=jnp.int32) * tile
    tile_expert = jnp.sum((ends[None, :] <= tile_start[:, None]).astype(jnp.int32), axis=1)
    tile_expert = jnp.minimum(tile_expert, n_exp - 1)
    n_used = (ends[-1:] // tile).astype(jnp.int32)
    nonempty = counts > 0
    ordinal = jnp.cumsum(nonempty.astype(jnp.int32)) - 1
    later = nonempty[None, :] & (expert_ids[None, :] > expert_ids[:, None])
    nxt_e = jnp.min(jnp.where(later, expert_ids[None, :], n_exp), axis=1)
    nxt_e = jnp.where(nxt_e == n_exp, -1, nxt_e)
    slot = (ordinal[tile_expert] % 2).astype(jnp.int32)
    nxt = nxt_e[tile_expert].astype(jnp.int32)
    pos_slot_major = pos_flat.reshape(n, TOP_K).T
    xs = _sc_scatter_rows(hp, pos_slot_major, n_rows)
    ys = _experts(xs, tile_expert, n_used, slot, nxt, w1_all, b1, w2_all, b2, layer)
    y4 = _sc_gather_rows(ys, pos_slot_major.reshape(-1))
    return _combine_dense(x2, mod_l, 5, seq, top_w, y4, final_g, final_norm)


def kernel(x, c, ada_w, ada_b, norm_mix_g, norm_ffn_g, router_w, router_b, moe_w1, moe_b1, moe_w2, moe_b2, ab_w_in, ab_w_out, hgrn_lb, hgrn_norm_g, rwkv_mu, rwkv_w0, rwkv_w2, rwkv_a0, rwkv_a2, rwkv_g2, rwkv_k_k, rwkv_k_a, rwkv_r_k, rwkv_ln_g, rwkv_ln_b, attn_w_in, attn_w_out, attn_lambda, attn_subln_g, rel_bias_table, final_norm_g):
    batch, seq, d = x.shape
    n = batch * seq
    depth = ada_w.shape[0]
    x2 = x.reshape(n, d)
    mod = _adaln(c, ada_w, ada_b)
    lb_all = jnp.cumsum(jax.nn.softmax(hgrn_lb.astype(F32), axis=1), axis=1)
    for layer in range(depth):
        mod_l = mod[layer].reshape(batch, 1, 6 * d)
        j = layer // 2
        if layer % 2 == 0:
            a_cols = 5 * (d // 2)
            w_in = ab_w_in[j].astype(BF16)
            p_a, p_b = _normmod_proj(x2, norm_mix_g[layer], mod_l, 0, seq,
                                     [w_in[:, :a_cols], w_in[:, a_cols:]], [F32, F32])
            y_a = _hgrn2(p_a, lb_all[:, j], hgrn_norm_g[j], batch, seq)
            y_b = _rwkv7(p_b, rwkv_mu[j], rwkv_w0[j], rwkv_w2[j], rwkv_a0[j], rwkv_a2[j], rwkv_g2[j],
                         rwkv_k_k[j], rwkv_k_a[j], rwkv_r_k[j], rwkv_ln_g[j], rwkv_ln_b[j], batch, seq)
            w_out = ab_w_out[j].astype(BF16)
            x2 = _outproj_residual(x2, mod_l, 2, seq, [y_a, y_b], [w_out[:d // 2], w_out[d // 2:]])
        else:
            (qkv,) = _normmod_proj(x2, norm_mix_g[layer], mod_l, 0, seq, [attn_w_in[j].astype(BF16)], [BF16])
            lam = attn_lambda[j].astype(F32)
            lam_init = 0.8 - 0.6 * math.exp(-0.3 * layer)
            lam_full = jnp.exp(jnp.sum(lam[0] * lam[1])) - jnp.exp(jnp.sum(lam[2] * lam[3])) + lam_init
            o = _diff_attention(qkv, rel_bias_table, lam_full, attn_subln_g[j], lam_init, batch, seq)
            x2 = _outproj_residual(x2, mod_l, 2, seq, [o], [attn_w_out[j].astype(BF16)])
        x2 = _moe_ffn(x2, norm_ffn_g[layer], mod_l, seq, router_w[layer], router_b[layer], moe_w1,
                      moe_b1[layer], moe_w2, moe_b2[layer], layer, final_norm_g, layer == depth - 1)
    return x2.reshape(batch, seq, d)
```

```python
import functools
import math

import jax
import jax.numpy as jnp
import numpy as np
from jax import lax
from jax.experimental import pallas as pl
from jax.experimental.pallas import tpu as pltpu
from jax.experimental.pallas import tpu_sc as plsc

F32 = jnp.float32
BF16 = jnp.bfloat16
HIGHEST = lax.Precision.HIGHEST

NORM_EPS = 1e-6

A_HEAD_DIM = 128
A_CHUNK = 32
A_GROUP = 512
A_HEADS_PER_STEP = 2
B_HEAD_DIM = 64
B_CHUNK = 64
B_STEPS = 2
B_LN_EPS = 1e-5 * B_HEAD_DIM
C_HEAD_DIM = 64
REL_BUCKETS = 32
REL_MAX_DISTANCE = 128
TOP_K = 4
SWIGLU_LIMIT = 7.0
SWIGLU_ALPHA = 1.702

V7X_VMEM_BYTES = 64 * 1024 * 1024
VMEM_LIMIT = V7X_VMEM_BYTES - 8 * 1024 * 1024
LANES = 128


def _params(*sem):
    return pltpu.CompilerParams(dimension_semantics=sem, vmem_limit_bytes=VMEM_LIMIT)


def _sigmoid(x):
    return 1.0 / (1.0 + jnp.exp(-x))


def _silu(x):
    return x * _sigmoid(x)


def _dot(a, b):
    return jnp.dot(a, b, preferred_element_type=F32)


def _dot_nt(a, b):
    return lax.dot_general(a, b, (((1,), (1,)), ((), ())), preferred_element_type=F32)


def _dot_tn(a, b):
    return lax.dot_general(a, b, (((0,), (0,)), ((), ())), preferred_element_type=F32)


def _adaln_kernel(c_ref, w_ref, b_ref, o_ref):
    cond = _silu(c_ref[...])
    o_ref[0] = jnp.dot(cond, w_ref[0], precision=HIGHEST, preferred_element_type=F32) + b_ref[0]


def _adaln(c, ada_w, ada_b):
    n_layers, d, n_out = ada_w.shape
    batch = c.shape[0]
    tn = 1536
    return pl.pallas_call(
        _adaln_kernel,
        grid=(n_layers, n_out // tn),
        in_specs=[
            pl.BlockSpec((batch, d), lambda l, j: (0, 0)),
            pl.BlockSpec((1, d, tn), lambda l, j: (l, 0, j)),
            pl.BlockSpec((1, 1, tn), lambda l, j: (l, 0, j)),
        ],
        out_specs=pl.BlockSpec((1, batch, tn), lambda l, j: (l, 0, j)),
        out_shape=jax.ShapeDtypeStruct((n_layers, batch, n_out), F32),
        compiler_params=_params("parallel", "parallel"),
        name="adaln",
    )(c, ada_w, ada_b.reshape(n_layers, 1, n_out))


def _modulated_norm(x, g, shift, scale):
    y = x * lax.rsqrt(jnp.mean(x * x, axis=-1, keepdims=True) + NORM_EPS)
    return (y * g) * (1.0 + scale) + shift


def _normmod_proj_kernel(x_ref, g_ref, sh_ref, sc_ref, *rest, n_w):
    w_refs, o_refs = rest[:n_w], rest[n_w:]
    h = _modulated_norm(x_ref[...], g_ref[...], sh_ref[0], sc_ref[0]).astype(BF16)
    for w_ref, o_ref in zip(w_refs, o_refs):
        o_ref[...] = _dot(h, w_ref[...]).astype(o_ref.dtype)


def _normmod_proj(x2, g, mod_l, shift_idx, seq, weights, out_dtypes, tm=512):
    n, d = x2.shape
    per_batch = seq // tm
    in_specs = [
        pl.BlockSpec((tm, d), lambda i: (i, 0)),
        pl.BlockSpec((1, d), lambda i: (0, 0)),
        pl.BlockSpec((1, 1, d), lambda i: (i // per_batch, 0, shift_idx)),
        pl.BlockSpec((1, 1, d), lambda i: (i // per_batch, 0, shift_idx + 1)),
    ]
    in_specs += [pl.BlockSpec(w.shape, lambda i: (0, 0)) for w in weights]
    out_specs = [pl.BlockSpec((tm, w.shape[1]), lambda i: (i, 0)) for w in weights]
    out_shape = [jax.ShapeDtypeStruct((n, w.shape[1]), dt) for w, dt in zip(weights, out_dtypes)]
    return pl.pallas_call(
        functools.partial(_normmod_proj_kernel, n_w=len(weights)),
        grid=(n // tm,),
        in_specs=in_specs,
        out_specs=out_specs,
        out_shape=out_shape,
        compiler_params=_params("parallel"),
        name="normmod_proj",
    )(x2, g.reshape(1, d), mod_l, mod_l, *weights)


def _outproj_kernel(x_ref, gate_ref, *rest, n_y):
    y_refs, w_refs, o_ref = rest[:n_y], rest[n_y:2 * n_y], rest[2 * n_y]
    acc = _dot(y_refs[0][...], w_refs[0][...])
    for y_ref, w_ref in zip(y_refs[1:], w_refs[1:]):
        acc += _dot(y_ref[...], w_ref[...])
    o_ref[...] = x_ref[...] + gate_ref[0] * acc


def _outproj_residual(x2, mod_l, gate_idx, seq, ys, ws, tm=512):
    n, d = x2.shape
    per_batch = seq // tm
    in_specs = [
        pl.BlockSpec((tm, d), lambda i: (i, 0)),
        pl.BlockSpec((1, 1, d), lambda i: (i // per_batch, 0, gate_idx)),
    ]
    in_specs += [pl.BlockSpec((tm, y.shape[1]), lambda i: (i, 0)) for y in ys]
    in_specs += [pl.BlockSpec(w.shape, lambda i: (0, 0)) for w in ws]
    return pl.pallas_call(
        functools.partial(_outproj_kernel, n_y=len(ys)),
        grid=(n // tm,),
        in_specs=in_specs,
        out_specs=pl.BlockSpec((tm, d), lambda i: (i, 0)),
        out_shape=jax.ShapeDtypeStruct((n, d), F32),
        compiler_params=_params("parallel"),
        name="outproj_residual",
    )(x2, mod_l, *ys, *ws)


def _chunk_cumsum(x, chunk, reverse):
    rows = x.shape[0]
    pos = lax.broadcasted_iota(jnp.int32, x.shape, 0) % chunk
    s = 1
    while s < chunk:
        if reverse:
            x = x + jnp.where(pos < chunk - s, pltpu.roll(x, rows - s, axis=0), 0.0)
        else:
            x = x + jnp.where(pos >= s, pltpu.roll(x, s, axis=0), 0.0)
        s *= 2
    return x


def _hgrn_groups(slabs):
    g_rows, dk = slabs[0][0].shape
    n_chunks = g_rows // A_CHUNK
    ti = lax.broadcasted_iota(jnp.int32, (A_CHUNK, A_CHUNK), 0)
    si = lax.broadcasted_iota(jnp.int32, (A_CHUNK, A_CHUNK), 1)
    chunk_rows = [slice(c * A_CHUNK, (c + 1) * A_CHUNK) for c in range(n_chunks)]
    prep = []
    for q, f, v, lb, st, reverse in slabs:
        fg = lb + (1.0 - lb) * _sigmoid(f)
        k = 1.0 - fg
        b = _chunk_cumsum(jnp.log(fg), A_CHUNK, reverse)
        b3 = b.reshape(n_chunks, A_CHUNK, dk)
        edge = b3[:, 0:1, :] if reverse else b3[:, A_CHUNK - 1:A_CHUNK, :]
        prep.append(dict(
            q_in=(q * jnp.exp(b)).astype(BF16), k_in=(k * jnp.exp(-b)).astype(BF16),
            k_st=(k.reshape(n_chunks, A_CHUNK, dk) * jnp.exp(edge - b3)).astype(BF16),
            decay=jnp.exp(edge), vb=v.astype(BF16), st=st, reverse=reverse,
            mask=(si >= ti) if reverse else (si <= ti)))
    for p in prep:
        p["scores"] = [_dot_nt(p["q_in"][r], p["k_in"][r]) for r in chunk_rows]
        p["dstate"] = [_dot_tn(p["vb"][r], p["k_st"][c]) for c, r in enumerate(chunk_rows)]
    for p in prep:
        p["intra"] = [_dot(jnp.where(p["mask"], s, 0.0).astype(BF16), p["vb"][r])
                      for s, r in zip(p["scores"], chunk_rows)]
    results = []
    for p in prep:
        st = p["st"]
        outs = [None] * n_chunks
        for c in (range(n_chunks - 1, -1, -1) if p["reverse"] else range(n_chunks)):
            outs[c] = p["intra"][c] + _dot_nt(p["q_in"][chunk_rows[c]], st.astype(BF16))
            st = st * p["decay"][c] + p["dstate"][c]
        results.append((jnp.concatenate(outs, axis=0), st))
    return results


def _hgrn_kernel(q_ref, ff_ref, fb_ref, i_ref, g_ref, lb_ref, ng_ref, o_ref, of_ref, ob_ref):
    seq = q_ref.shape[0]
    dk = A_HEAD_DIM
    n_heads = q_ref.shape[1] // dk
    n_groups = seq // A_GROUP
    head_cols = [slice(h * dk, (h + 1) * dk) for h in range(n_heads)]

    def body(j, states):
        rf = pl.ds(pl.multiple_of(j * A_GROUP, A_GROUP), A_GROUP)
        rb = pl.ds(pl.multiple_of((n_groups - 1 - j) * A_GROUP, A_GROUP), A_GROUP)
        slabs = []
        for h, cols in enumerate(head_cols):
            slabs.append((_silu(q_ref[rf, cols]), ff_ref[rf, cols], i_ref[rf, cols], lb_ref[0:1, cols],
                          states[2 * h], False))
            slabs.append((_silu(q_ref[rb, cols]), fb_ref[rb, cols], i_ref[rb, cols], lb_ref[1:2, cols],
                          states[2 * h + 1], True))
        results = _hgrn_groups(slabs)
        for h, cols in enumerate(head_cols):
            of_ref[rf, cols] = results[2 * h][0]
            ob_ref[rb, cols] = results[2 * h + 1][0]
        return tuple(r[1] for r in results)

    zero = jnp.zeros((dk, dk), F32)
    lax.fori_loop(0, n_groups, body, (zero,) * (2 * n_heads))

    def finish(j, carry):
        r = pl.ds(pl.multiple_of(j * A_GROUP, A_GROUP), A_GROUP)
        for cols in head_cols:
            o = of_ref[r, cols] + ob_ref[r, cols]
            o = o * lax.rsqrt(jnp.mean(o * o, axis=-1, keepdims=True) + NORM_EPS) * ng_ref[...]
            o_ref[r, cols] = (o * _silu(g_ref[r, cols])).astype(o_ref.dtype)
        return carry

    lax.fori_loop(0, n_groups, finish, 0)


def _hgrn2(p_a, lb, norm_g, batch, seq):
    n = p_a.shape[0]
    width = p_a.shape[1] // 5
    cols = A_HEADS_PER_STEP * A_HEAD_DIM
    steps = width // cols
    sect = lambda s: pl.BlockSpec((seq, cols), lambda b, h, s=s: (b, s * steps + h))
    return pl.pallas_call(
        _hgrn_kernel,
        grid=(batch, steps),
        in_specs=[sect(0), sect(1), sect(2), sect(3), sect(4),
                  pl.BlockSpec((2, cols), lambda b, h: (0, h)),
                  pl.BlockSpec((1, A_HEAD_DIM), lambda b, h: (0, 0))],
        out_specs=pl.BlockSpec((seq, cols), lambda b, h: (b, h)),
        out_shape=jax.ShapeDtypeStruct((n, width), BF16),
        scratch_shapes=[pltpu.VMEM((seq, cols), F32), pltpu.VMEM((seq, cols), F32)],
        compiler_params=_params("parallel", "parallel"),
        name="hgrn2",
    )(p_a, p_a, p_a, p_a, p_a, lb, norm_g.reshape(1, A_HEAD_DIM))


def _softplus(z):
    return jnp.maximum(z, 0.0) + jnp.log(1.0 + jnp.exp(-jnp.abs(z)))


def _rwkv_prep_kernel(p_ref, prev_ref, next_ref, mu_ref, w0_ref, w2_ref, a0_ref, a2_ref, g2_ref,
                      kk_ref, ka_ref, rk_ref, ones_ref, tri_ref,
                      r_out, k_out, v_out, kkn_out, b_out, cumf_out, cumb_out, g_out, bonus_out):
    i = pl.program_id(1)
    last = pl.num_programs(1) - 1
    ts = p_ref.shape[0]
    width = a0_ref.shape[1]
    heads = width // B_HEAD_DIM
    p = p_ref[...]
    prev_row = jnp.where(i == 0, 0.0, prev_ref[7:8, :])
    next_row = jnp.where(i == last, 0.0, next_ref[0:1, :])
    row = lax.broadcasted_iota(jnp.int32, (ts, 1), 0)
    up = jnp.where(row == 0, prev_row, pltpu.roll(p, 1, axis=0))
    dn = jnp.where(row == ts - 1, next_row, pltpu.roll(p, ts - 1, axis=0))
    p = p + mu_ref[...] * (0.5 * (up + dn) - p)

    r = p[:, 0:width]
    k = p[:, width:2 * width]
    v = p[:, 2 * width:3 * width]
    o = 3 * width
    wlo_f = p[:, o:o + 64]
    wlo_b = p[:, o + 64:o + 128]
    alo = p[:, o + 128:o + 192]
    glo = p[:, o + 192:o + 320]

    def log_decay(wlo, d):
        z = w0_ref[d:d + 1, :] + _dot(jnp.tanh(wlo).astype(BF16), w2_ref[d])
        w = -_softplus(-z) - 0.5
        return -jnp.exp(w)

    def split_dot(sel, t, right):
        hi = t.astype(BF16)
        lo = (t - hi.astype(F32)).astype(BF16)
        return (_dot(hi, sel) + _dot(lo, sel)) if right else (_dot(sel, hi) + _dot(sel, lo))

    cum_f = split_dot(tri_ref[0], log_decay(wlo_f, 0), False)
    cum_b = split_dot(tri_ref[1], log_decay(wlo_b, 1), False)
    a = _sigmoid(a0_ref[...] + _dot(alo.astype(BF16), a2_ref[...]))
    g = _dot(_sigmoid(glo).astype(BF16), g2_ref[...])
    kk = k * kk_ref[...]
    head_sum = lambda t: split_dot(ones_ref[...], t, True)

    kk_n = kk / jnp.maximum(jnp.sqrt(head_sum(kk * kk)), 1e-12)
    k_mod = k * (1.0 + (a - 1.0) * ka_ref[...])
    bonus = head_sum(r * k_mod * rk_ref[...]) * v
    g_out[...] = g
    bonus_out[...] = bonus
    for h in range(heads):
        sl = slice(h * B_HEAD_DIM, (h + 1) * B_HEAD_DIM)
        r_out[0, h] = r[:, sl]
        k_out[0, h] = k_mod[:, sl]
        v_out[0, h] = v[:, sl]
        kkn_out[0, h] = kk_n[:, sl]
        b_out[0, h] = (kk_n * a)[:, sl]
        cumf_out[0, h] = cum_f[:, sl]
        cumb_out[0, h] = cum_b[:, sl]


def _rwkv_prep(p_b, mu, w0, w2, a0, a2, g2, k_k, k_a, r_k, batch, seq, ts=256):
    n, cols = p_b.shape
    width = a0.shape[0]
    heads = width // B_HEAD_DIM
    nblk = seq // ts
    rows8 = ts // 8
    head_id = np.arange(width) // B_HEAD_DIM
    ones = jnp.asarray(head_id[:, None] == head_id[None, :], BF16)
    full = lambda a: pl.BlockSpec(a.shape, lambda b, i: (0,) * a.ndim)
    vec = lambda a: a.reshape(1, -1)
    row = np.arange(ts)
    same_chunk = (row[:, None] // B_CHUNK) == (row[None, :] // B_CHUNK)
    tri = jnp.asarray(np.stack([same_chunk & (row[None, :] <= row[:, None]),
                                same_chunk & (row[None, :] >= row[:, None])]), BF16)
    args = [vec(mu), w0, w2.astype(BF16), vec(a0), a2.astype(BF16), g2.astype(BF16),
            vec(k_k), vec(k_a), vec(r_k), ones, tri]
    hm = pl.BlockSpec((1, heads, ts, B_HEAD_DIM), lambda b, i: (b, 0, i, 0))
    tokm = pl.BlockSpec((ts, width), lambda b, i: (b * nblk + i, 0))
    hm_shape = jax.ShapeDtypeStruct((batch, heads, seq, B_HEAD_DIM), F32)
    tok_shape = jax.ShapeDtypeStruct((n, width), F32)
    return pl.pallas_call(
        _rwkv_prep_kernel,
        grid=(batch, nblk),
        in_specs=[
            pl.BlockSpec((ts, cols), lambda b, i: (b * nblk + i, 0)),
            pl.BlockSpec((8, cols), lambda b, i: (jnp.maximum((b * nblk + i) * rows8 - 1, 0), 0)),
            pl.BlockSpec((8, cols), lambda b, i: (jnp.minimum((b * nblk + i + 1) * rows8, n // 8 - 1), 0)),
        ] + [full(a) for a in args],
        out_specs=[hm] * 7 + [tokm] * 2,
        out_shape=[hm_shape] * 7 + [tok_shape] * 2,
        compiler_params=_params("parallel", "parallel"),
        name="rwkv_prep",
    )(p_b, p_b, p_b, *args)


def _rwkv_chunks(chains):
    c, d = chains[0][0].shape
    ti = lax.broadcasted_iota(jnp.int32, (c, c), 0)
    si = lax.broadcasted_iota(jnp.int32, (c, c), 1)
    eye = jnp.where(ti == si, 1.0, 0.0)
    masks = {rev: (((si > ti), (si >= ti)) if rev else ((si < ti), (si <= ti))) for rev in (False, True)}
    t2 = lax.broadcasted_iota(jnp.int32, (c, 2 * c), 0)
    s2 = lax.broadcasted_iota(jnp.int32, (c, 2 * c), 1)
    s2 = jnp.where(s2 >= c, s2 - c, s2)
    incl2 = {False: s2 <= t2, True: s2 >= t2}

    row = lax.broadcasted_iota(jnp.int32, (c, d), 0)
    st = []
    for r, k, v, kk, b, cum, rev in chains:
        before = (jnp.where(row == c - 1, 0.0, pltpu.roll(cum, c - 1, axis=0)) if rev
                  else jnp.where(row == 0, 0.0, pltpu.roll(cum, 1, axis=0)))
        grow = jnp.exp(-cum)
        kt = (kk * jnp.exp(before)).astype(BF16)
        rt = r * jnp.exp(cum)
        b_ = (b * grow).astype(BF16)
        kb = jnp.concatenate([(k * grow).astype(BF16), b_], axis=0)
        st.append(dict(vb=v.astype(BF16), kt=kt, rt=rt, b_=b_, kb=kb,
                       decay=jnp.exp(cum[0:1, :] if rev else cum[c - 1:c, :]), rev=rev,
                       lhs=jnp.concatenate([kt, rt.astype(BF16)], axis=0)))
    for s in st:
        s["big"] = _dot_nt(s["lhs"], s["kb"])
    for s in st:
        strict, incl = masks[s["rev"]]
        big = s["big"]
        a_ab = jnp.where(strict, big[:c, c:], 0.0)
        s["a_ak"] = jnp.where(strict, big[:c, :c], 0.0).astype(BF16)
        s["p"] = jnp.where(incl2[s["rev"]], big[c:], 0.0).astype(BF16)
        s["m"] = eye - a_ab
        s["ab"] = a_ab.astype(BF16)
    for s in st:
        s["aj"] = _dot(s["ab"], s["ab"])
        s["av"] = _dot(s["a_ak"], s["vb"])
    span = 2
    while span < c:
        span *= 2
        for s in st:
            ajb = s["aj"].astype(BF16)
            if span < c:
                both = _dot(jnp.concatenate([ajb, s["m"].astype(BF16)], axis=0), ajb)
                s["aj"], s["m"] = both[:c], s["m"] + both[c:]
            else:
                s["m"] = s["m"] + _dot(s["m"].astype(BF16), ajb)
    for s in st:
        wu = _dot(s["m"].astype(BF16), jnp.concatenate([s["kt"], s["av"].astype(BF16)], axis=1))
        s["nwu"] = -wu.astype(BF16)
    for s in st:
        nw, nu0, vb = s["nwu"][:, :d], s["nwu"][:, d:], s["vb"]
        rhs = jnp.concatenate([jnp.concatenate([vb, jnp.zeros_like(vb)], axis=1),
                               jnp.concatenate([nu0, nw], axis=1)], axis=0)
        s["yq"] = _dot(s["p"], rhs)
        s["h_t"] = _dot_tn(jnp.concatenate([vb, nu0], axis=0), s["kb"])
        s["nbw"] = _dot_tn(s["b_"], nw)
    return [dict(q=(s["rt"] + s["yq"][:, d:]).astype(BF16), y1=s["yq"][:, :d], nbw=s["nbw"].astype(BF16),
                 h_t=s["h_t"], decay=s["decay"]) for s in st]


def _rwkv_advance(states, steps):
    ttbs = [tt.astype(BF16) for tt in states]
    ys = [_dot_nt(s["q"], ttb) + s["y1"] for s, ttb in zip(steps, ttbs)]
    new = [(tt + _dot_nt(ttb, s["nbw"]) + s["h_t"]) * s["decay"] for tt, ttb, s in zip(states, ttbs, steps)]
    return ys, new


def _rwkv_scan_kernel(rf, kf, vf, kkf, bf, lwf, rb, kb, vb, kkb, bb, lwb, yf_ref, yb_ref, tf_ref, tb_ref):
    heads, ts = rf.shape[1], rf.shape[2]
    n_chunks = ts // B_CHUNK

    @pl.when(pl.program_id(1) == 0)
    def _():
        tf_ref[...] = jnp.zeros_like(tf_ref)
        tb_ref[...] = jnp.zeros_like(tb_ref)

    def body(j, carry):
        rows_f = [pl.ds(pl.multiple_of((j * B_STEPS + s) * B_CHUNK, B_CHUNK), B_CHUNK) for s in range(B_STEPS)]
        rows_b = [pl.ds(pl.multiple_of((n_chunks - 1 - j * B_STEPS - s) * B_CHUNK, B_CHUNK), B_CHUNK)
                  for s in range(B_STEPS)]
        chains = []
        for s in range(B_STEPS):
            for h in range(heads):
                chains.append([ref[0, h, rows_f[s], :] for ref in (rf, kf, vf, kkf, bf, lwf)] + [False])
                chains.append([ref[0, h, rows_b[s], :] for ref in (rb, kb, vb, kkb, bb, lwb)] + [True])
        solved = _rwkv_chunks(chains)
        states = []
        for h in range(heads):
            states += [tf_ref[h], tb_ref[h]]
        for s in range(B_STEPS):
            ys, states = _rwkv_advance(states, solved[s * 2 * heads:(s + 1) * 2 * heads])
            for h in range(heads):
                yf_ref[0, h, rows_f[s], :] = ys[2 * h]
                yb_ref[0, h, rows_b[s], :] = ys[2 * h + 1]
        for h in range(heads):
            tf_ref[h], tb_ref[h] = states[2 * h], states[2 * h + 1]
        return carry

    lax.fori_loop(0, n_chunks // B_STEPS, body, 0)


def _rwkv_scan(r, k, v, kkn, b, lw_f, lw_b, ts=256):
    batch, heads, seq, d = r.shape
    nblk = seq // ts
    fwd = pl.BlockSpec((1, heads, ts, d), lambda bi, i: (bi, 0, i, 0))
    bwd = pl.BlockSpec((1, heads, ts, d), lambda bi, i: (bi, 0, nblk - 1 - i, 0))
    shape = jax.ShapeDtypeStruct(r.shape, F32)
    return pl.pallas_call(
        _rwkv_scan_kernel,
        grid=(batch, nblk),
        in_specs=[fwd] * 6 + [bwd] * 6,
        out_specs=[fwd, bwd],
        out_shape=[shape, shape],
        scratch_shapes=[pltpu.VMEM((heads, d, d), F32), pltpu.VMEM((heads, d, d), F32)],
        compiler_params=_params("parallel", "arbitrary"),
        name="rwkv_scan",
    )(r, k, v, kkn, b, lw_f, r, k, v, kkn, b, lw_b)


def _rwkv_post_kernel(yf_ref, yb_ref, g_ref, bonus_ref, lng_ref, lnb_ref, o_ref):
    heads = yf_ref.shape[1]
    outs = []
    for h in range(heads):
        y = yf_ref[0, h] + yb_ref[0, h]
        mean = jnp.mean(y, axis=-1, keepdims=True)
        var = jnp.mean(jnp.square(y - mean), axis=-1, keepdims=True)
        outs.append((y - mean) * lax.rsqrt(var + B_LN_EPS))
    yn = jnp.concatenate(outs, axis=1) * lng_ref[...] + lnb_ref[...]
    o_ref[...] = ((yn + bonus_ref[...]) * g_ref[...]).astype(o_ref.dtype)


def _rwkv_post(y_f, y_b, g, bonus, ln_g, ln_b, ts=512):
    batch, heads, seq, d = y_f.shape
    n, width = g.shape
    nblk = seq // ts
    hm = pl.BlockSpec((1, heads, ts, d), lambda b, i: (b, 0, i, 0))
    tokm = pl.BlockSpec((ts, width), lambda b, i: (b * nblk + i, 0))
    vec = pl.BlockSpec((1, width), lambda b, i: (0, 0))
    return pl.pallas_call(
        _rwkv_post_kernel,
        grid=(batch, nblk),
        in_specs=[hm, hm, tokm, tokm, vec, vec],
        out_specs=tokm,
        out_shape=jax.ShapeDtypeStruct((n, width), BF16),
        compiler_params=_params("parallel", "parallel"),
        name="rwkv_post",
    )(y_f, y_b, g, bonus, ln_g.reshape(1, width), ln_b.reshape(1, width))


def _rwkv7(p_b, mu, w0, w2, a0, a2, g2, k_k, k_a, r_k, ln_g, ln_b, batch, seq):
    r, k, v, kkn, b, lw_f, lw_b, g, bonus = _rwkv_prep(p_b, mu, w0, w2, a0, a2, g2, k_k, k_a, r_k, batch, seq)
    y_f, y_b = _rwkv_scan(r, k, v, kkn, b, lw_f, lw_b)
    return _rwkv_post(y_f, y_b, g, bonus, ln_g, ln_b)


_BUCKET_EDGES = (0, 1, 2, 3, 4, 5, 6, 7, 8, 12, 16, 23, 32, 46, 64, 91)
ATTN_TK = 128
C_HEADS_PER_STEP = 4


def _bias_band_kernel(table_ref, o_ref):
    h = pl.program_id(0)
    n_d, tq, tk = o_ref.shape[1:]
    half = REL_BUCKETS // 2
    r = lax.broadcasted_iota(jnp.int32, (tq, tk), 0)
    c = lax.broadcasted_iota(jnp.int32, (tq, tk), 1)
    for d in range(n_d):
        rel = tk * (d - 1) + c - r
        n = jnp.abs(rel)
        vals = []
        for sign in range(2):
            val = jnp.full((tq, tk), table_ref[sign * half + half - 1, h], F32)
            for bkt in range(half - 2, -1, -1):
                val = jnp.where(n < _BUCKET_EDGES[bkt + 1], table_ref[sign * half + bkt, h], val)
            vals.append(val)
        o_ref[0, d] = jnp.where(rel > 0, vals[1], vals[0])


def _bias_band(rel_table, tq):
    heads = rel_table.shape[1]
    n_d = tq // ATTN_TK + 2
    return pl.pallas_call(
        _bias_band_kernel,
        grid=(heads,),
        in_specs=[pl.BlockSpec(memory_space=pltpu.SMEM)],
        out_specs=pl.BlockSpec((1, n_d, tq, ATTN_TK), lambda h: (h, 0, 0, 0)),
        out_shape=jax.ShapeDtypeStruct((heads, n_d, tq, ATTN_TK), F32),
        compiler_params=_params("parallel"),
        name="bias_band",
    )(rel_table)


def _diff_attn_kernel(table_ref, lam_ref, q_ref, k_ref, v_ref, band_ref, g_ref, o_ref, lg_ref, vext_ref, kt_ref,
                      *, lam_init):
    i = pl.program_id(2)
    tq = q_ref.shape[0]
    dh2 = 2 * C_HEAD_DIM
    n_heads = q_ref.shape[1] // dh2
    seq = k_ref.shape[0]
    n_d = band_ref.shape[1]
    n_kt = seq // ATTN_TK
    ratio = tq // ATTN_TK
    half = REL_BUCKETS // 2
    head_cols = [slice(hh * dh2, (hh + 1) * dh2) for hh in range(n_heads)]

    @pl.when(i == 0)
    def _():
        for hh, cols in enumerate(head_cols):
            kt_ref[hh] = k_ref[:, cols].astype(F32).T.astype(BF16)
            v = v_ref[:, cols]
            vext_ref[hh] = jnp.concatenate([v, jnp.ones_like(v)], axis=1)

    lane = lax.broadcasted_iota(jnp.int32, (1, dh2), 1)
    first_band = ratio * i - 1
    col = lax.broadcasted_iota(jnp.int32, (1, seq), 1)
    for hh, cols in enumerate(head_cols):
        h = pl.program_id(1) * n_heads + hh
        q = q_ref[:, cols] * jnp.asarray(C_HEAD_DIM ** -0.5, q_ref.dtype)
        zero = jnp.zeros_like(q)
        far_row = jnp.where(col < first_band * ATTN_TK, table_ref[half - 1, h],
                            jnp.where(col >= (first_band + n_d) * ATTN_TK, table_ref[REL_BUCKETS - 1, h],
                                      -jnp.inf))
        row_max = []
        for m in range(2):
            qm = jnp.where((lane >= m * C_HEAD_DIM) & (lane < (m + 1) * C_HEAD_DIM), q, zero)
            far = _dot(qm, kt_ref[hh]) + far_row
            mx = jnp.max(far, axis=-1, keepdims=True)
            lg_ref[hh, m] = far
            for d in (0, n_d - 1) + tuple(range(1, n_d - 1)):
                kt = first_band + d
                valid = (kt >= 0) & (kt < n_kt)
                c0 = pl.multiple_of(jnp.clip(kt, 0, n_kt - 1) * ATTN_TK, ATTN_TK)
                near = _dot(qm, kt_ref[hh, :, pl.ds(c0, ATTN_TK)]) + band_ref[hh, d]
                mx = jnp.maximum(mx, jnp.where(valid, jnp.max(near, axis=-1, keepdims=True), -jnp.inf))
                lg_ref[hh, m, :, pl.ds(c0, ATTN_TK)] = near
            row_max.append(mx)
        parts = []
        for m in range(2):
            e = jnp.exp(lg_ref[hh, m] - row_max[m]).astype(BF16)
            pv = _dot(e, vext_ref[hh])
            parts.append(pv[:, :dh2] * (1.0 / pv[:, dh2:dh2 + 1]))
        o = parts[0] - lam_ref[0] * parts[1]
        o = o * lax.rsqrt(jnp.mean(o * o, axis=-1, keepdims=True) + NORM_EPS) * g_ref[...]
        o_ref[:, cols] = (o * (1.0 - lam_init)).astype(o_ref.dtype)


def _diff_attention(qkv, rel_table, lam_full, subln_g, lam_init, batch, seq, tq=256):
    n = qkv.shape[0]
    dh2 = 2 * C_HEAD_DIM
    heads = qkv.shape[1] // (3 * dh2)
    nq = seq // tq
    per = C_HEADS_PER_STEP
    groups = heads // per
    cols = per * dh2
    band = _bias_band(rel_table, tq)
    return pl.pallas_call(
        functools.partial(_diff_attn_kernel, lam_init=lam_init),
        grid=(batch, groups, nq),
        in_specs=[
            pl.BlockSpec(memory_space=pltpu.SMEM),
            pl.BlockSpec(memory_space=pltpu.SMEM),
            pl.BlockSpec((tq, cols), lambda b, h, i: (b * nq + i, h)),
            pl.BlockSpec((seq, cols), lambda b, h, i: (b, groups + h)),
            pl.BlockSpec((seq, cols), lambda b, h, i: (b, 2 * groups + h)),
            pl.BlockSpec((per,) + band.shape[1:], lambda b, h, i: (h, 0, 0, 0)),
            pl.BlockSpec((1, dh2), lambda b, h, i: (0, 0)),
        ],
        out_specs=pl.BlockSpec((tq, cols), lambda b, h, i: (b * nq + i, h)),
        out_shape=jax.ShapeDtypeStruct((n, heads * dh2), BF16),
        scratch_shapes=[pltpu.VMEM((per, 2, tq, seq), F32), pltpu.VMEM((per, seq, 2 * dh2), BF16),
                        pltpu.VMEM((per, dh2, seq), BF16)],
        compiler_params=_params("parallel", "parallel", "arbitrary"),
        name="diff_attention",
    )(rel_table, lam_full.reshape(1), qkv, qkv, qkv, band, subln_g.reshape(1, dh2))


MOE_TILE = 256
EXPERT_TILE = 512
EXPERT_CHUNK = 512
_HI_MASK = np.uint32(0xFFFF0000)


def _pack_bf16_pair(lo, hi):
    as_bits = lambda t: lax.bitcast_convert_type(t.astype(BF16).astype(F32), jnp.uint32)
    return (as_bits(hi) & _HI_MASK) | (as_bits(lo) >> 16)


def _unpack_bf16_pair(word):
    lo = lax.bitcast_convert_type(word << 16, F32).astype(BF16)
    hi = lax.bitcast_convert_type(word & _HI_MASK, F32).astype(BF16)
    return lo, hi


def _router_kernel(x_ref, g_ref, sh_ref, sc_ref, rw_ref, rb_ref, hp_ref, idx_ref, w_ref, rank_ref, cnt_ref,
                   run_ref, *, n_experts):
    @pl.when(pl.program_id(0) == 0)
    def _():
        run_ref[...] = jnp.zeros_like(run_ref)

    tm, d = x_ref.shape
    h = _modulated_norm(x_ref[...], g_ref[...], sh_ref[0], sc_ref[0])
    hp_ref[...] = _pack_bf16_pair(h[:, :d // 2], h[:, d // 2:])
    logits = _dot(h.astype(BF16), rw_ref[...]) + rb_ref[...]
    lane = lax.broadcasted_iota(jnp.int32, logits.shape, 1).astype(F32)
    neg = jnp.float32(-jnp.inf)
    logits = jnp.where(lane < n_experts, logits, neg)
    picks, vals = [], []
    for _ in range(TOP_K):
        m = jnp.max(logits, axis=-1, keepdims=True)
        first = jnp.min(jnp.where(logits == m, lane, float(LANES)), axis=-1, keepdims=True)
        hit = lane == first
        picks.append((first, hit))
        vals.append(m)
        logits = jnp.where(hit, neg, logits)
    es = [jnp.exp(v - vals[0]) for v in vals]
    inv = 1.0 / (es[0] + es[1] + es[2] + es[3])
    assigned = picks[0][1] | picks[1][1] | picks[2][1] | picks[3][1]
    onehot = jnp.where(assigned, 1.0, 0.0)
    ri = lax.broadcasted_iota(jnp.int32, (tm, tm), 0)
    ci = lax.broadcasted_iota(jnp.int32, (tm, tm), 1)
    earlier = jnp.where(ci < ri, 1.0, 0.0).astype(BF16)
    before = _dot(earlier, onehot.astype(BF16)) + run_ref[...]
    idx_o = jnp.zeros(logits.shape, F32)
    w_o = jnp.zeros(logits.shape, F32)
    rank_o = jnp.zeros(logits.shape, F32)
    for j, (first, hit) in enumerate(picks):
        rank_j = jnp.sum(jnp.where(hit, before, 0.0), axis=-1, keepdims=True)
        idx_o = jnp.where(lane == j, first, idx_o)
        w_o = jnp.where(lane == j, es[j] * inv, w_o)
        rank_o = jnp.where(lane == j, rank_j, rank_o)
    idx_ref[...] = idx_o
    w_ref[...] = w_o
    rank_ref[...] = rank_o
    run_ref[...] += jnp.sum(onehot, axis=0, keepdims=True)
    cnt_ref[...] = run_ref[...]


def _router(x2, g, mod_l, shift_idx, seq, router_w, router_b):
    n, d = x2.shape
    tm = MOE_TILE
    n_experts = router_w.shape[1]
    per_batch = seq // tm
    rw = jnp.zeros((d, LANES), BF16).at[:, :n_experts].set(router_w.astype(BF16))
    rb = jnp.zeros((1, LANES), F32).at[0, :n_experts].set(router_b)
    lane_out = pl.BlockSpec((tm, LANES), lambda i: (i, 0))
    lane_shape = jax.ShapeDtypeStruct((n, LANES), F32)
    return pl.pallas_call(
        functools.partial(_router_kernel, n_experts=n_experts),
        grid=(n // tm,),
        in_specs=[
            pl.BlockSpec((tm, d), lambda i: (i, 0)),
            pl.BlockSpec((1, d), lambda i: (0, 0)),
            pl.BlockSpec((1, 1, d), lambda i: (i // per_batch, 0, shift_idx)),
            pl.BlockSpec((1, 1, d), lambda i: (i // per_batch, 0, shift_idx + 1)),
            pl.BlockSpec((d, LANES), lambda i: (0, 0)),
            pl.BlockSpec((1, LANES), lambda i: (0, 0)),
        ],
        out_specs=[pl.BlockSpec((tm, d // 2), lambda i: (i, 0)), lane_out, lane_out, lane_out,
                   pl.BlockSpec((1, LANES), lambda i: (0, 0))],
        out_shape=[jax.ShapeDtypeStruct((n, d // 2), jnp.uint32), lane_shape, lane_shape, lane_shape,
                   jax.ShapeDtypeStruct((1, LANES), F32)],
        scratch_shapes=[pltpu.VMEM((1, LANES), F32)],
        compiler_params=_params("arbitrary"),
        name="moe_router",
    )(x2, g.reshape(1, d), mod_l, mod_l, rw, rb)


GLU_TILE = 2 * LANES


def _experts_kernel(te_ref, nu_ref, slot_ref, nxt_ref, rows_ref, xs_ref, b1_ref, b2_ref, perm_ref, w1_hbm, w2_hbm,
                    ys_ref, w1buf, w2buf, w1p_ref, w2b_ref, sem, *, layer):
    t = pl.program_id(0)
    used = t < nu_ref[0]
    new_expert = (t == 0) | (te_ref[t] != te_ref[jnp.maximum(t - 1, 0)])
    n_col_tiles = w1p_ref.shape[1] // GLU_TILE

    def weight_copies(e, slot):
        return (pltpu.make_async_copy(w1_hbm.at[layer, e], w1buf.at[slot], sem.at[0, slot]),
                pltpu.make_async_copy(w2_hbm.at[layer, e], w2buf.at[slot], sem.at[1, slot]))

    @pl.when(used & new_expert)
    def _():
        slot = slot_ref[t]

        @pl.when(t == 0)
        def _():
            for cp in weight_copies(te_ref[t], slot):
                cp.start()

        for cp in weight_copies(te_ref[t], slot):
            cp.wait()

        @pl.when(nxt_ref[t] >= 0)
        def _():
            for cp in weight_copies(nxt_ref[t], 1 - slot):
                cp.start()

        for c in range(n_col_tiles):
            cols = slice(c * GLU_TILE, (c + 1) * GLU_TILE)
            w1p_ref[:, cols] = _dot(w1buf[slot, :, cols].astype(BF16), perm_ref[...]).astype(BF16)
        w2b_ref[...] = w2buf[slot].astype(BF16)

    def swiglu_rows(n_rows):
        lo, hi = _unpack_bf16_pair(xs_ref[0:n_rows, :])
        x = jnp.concatenate([lo, hi], axis=1)
        out = b2_ref[0]
        for j in range(n_col_tiles * LANES // EXPERT_CHUNK):
            hcols = slice(2 * j * EXPERT_CHUNK, 2 * (j + 1) * EXPERT_CHUNK)
            hh = _dot(x, w1p_ref[:, hcols]) + b1_ref[0, :, hcols]
            acts = []
            for c in range(2 * EXPERT_CHUNK // GLU_TILE):
                x_glu = jnp.minimum(hh[:, c * GLU_TILE:c * GLU_TILE + LANES], SWIGLU_LIMIT)
                x_lin = jnp.clip(hh[:, c * GLU_TILE + LANES:(c + 1) * GLU_TILE], -SWIGLU_LIMIT, SWIGLU_LIMIT)
                acts.append((x_glu * _sigmoid(SWIGLU_ALPHA * x_glu) * (x_lin + 1.0)).astype(BF16))
            out = out + _dot(jnp.concatenate(acts, axis=1), w2b_ref[j * EXPERT_CHUNK:(j + 1) * EXPERT_CHUNK, :])
        half = out.shape[1] // 2
        ys_ref[0:n_rows, :] = _pack_bf16_pair(out[:, :half], out[:, half:])
        if n_rows < ys_ref.shape[0]:
            ys_ref[n_rows:, :] = jnp.zeros((ys_ref.shape[0] - n_rows, ys_ref.shape[1]), ys_ref.dtype)

    tile_rows = xs_ref.shape[0]
    pl.when(used & (rows_ref[t] > tile_rows // 2))(lambda: swiglu_rows(tile_rows))
    pl.when(used & (rows_ref[t] <= tile_rows // 2))(lambda: swiglu_rows(tile_rows // 2))

    @pl.when(t >= nu_ref[0])
    def _():
        ys_ref[...] = jnp.zeros_like(ys_ref)


def _experts(xs, tile_expert, n_used, slot, nxt, tile_rows, w1_all, b1, w2_all, b2, layer):
    n_rows, dw = xs.shape
    tm = EXPERT_TILE
    _, n_exp, d, de2 = w1_all.shape
    src = np.arange(GLU_TILE)
    dst = np.where(src % 2 == 0, src // 2, LANES + src // 2)
    perm = jnp.asarray(dst[:, None] == np.arange(GLU_TILE)[None, :], BF16)
    b1p = b1.reshape(n_exp, de2 // GLU_TILE, LANES, 2).transpose(0, 1, 3, 2).reshape(n_exp, 1, de2)
    row = lambda t, te, nu, sl, nx, nr: (jnp.maximum(jnp.minimum(t, nu[0] - 1), 0), 0)
    exp3 = lambda t, te, nu, sl, nx, nr: (te[t], 0, 0)
    grid_spec = pltpu.PrefetchScalarGridSpec(
        num_scalar_prefetch=5,
        grid=(n_rows // tm,),
        in_specs=[
            pl.BlockSpec((tm, dw), row),
            pl.BlockSpec((1, 1, de2), exp3),
            pl.BlockSpec((1, 1, d), exp3),
            pl.BlockSpec((GLU_TILE, GLU_TILE), lambda t, te, nu, sl, nx, nr: (0, 0)),
            pl.BlockSpec(memory_space=pl.ANY),
            pl.BlockSpec(memory_space=pl.ANY),
        ],
        out_specs=pl.BlockSpec((tm, d // 2), lambda t, te, nu, sl, nx, nr: (t, 0)),
        scratch_shapes=[pltpu.VMEM((2, d, de2), F32), pltpu.VMEM((2, de2 // 2, d), F32),
                        pltpu.VMEM((d, de2), BF16), pltpu.VMEM((de2 // 2, d), BF16),
                        pltpu.SemaphoreType.DMA((2, 2))],
    )
    return pl.pallas_call(
        functools.partial(_experts_kernel, layer=layer),
        grid_spec=grid_spec,
        out_shape=jax.ShapeDtypeStruct((n_rows, d // 2), jnp.uint32),
        compiler_params=_params("arbitrary"),
        name="moe_experts",
    )(tile_expert, n_used, slot, nxt, tile_rows, xs, b1p, b2.reshape(n_exp, 1, d), perm, w1_all, w2_all)


SC_WINDOW = 128


def _sc_worker_rows(n_rows):
    sc = plsc.get_sparse_core_info()
    per_worker = n_rows // (sc.num_cores * sc.num_subcores)
    worker = lax.axis_index("subcore") * sc.num_cores + lax.axis_index("core")
    return worker * per_worker, per_worker


def _sc_mesh():
    return plsc.VectorSubcoreMesh(core_axis_name="core", subcore_axis_name="subcore")


def _sc_gather_rows(table, idx):
    n_idx, width = idx.shape[0], table.shape[1]

    @functools.partial(
        pl.kernel, out_type=jax.ShapeDtypeStruct((n_idx, width), table.dtype), mesh=_sc_mesh(),
        scratch_types=[pltpu.VMEM((SC_WINDOW,), jnp.int32), pltpu.VMEM((SC_WINDOW, width), table.dtype)],
        name="sc_gather_rows")
    def gather(table_hbm, idx_hbm, out_hbm, idx_v, rows_v):
        first, count = _sc_worker_rows(n_idx)

        @pl.loop(0, count // SC_WINDOW)
        def _(c):
            rows = pl.ds(first + c * SC_WINDOW, SC_WINDOW)
            pltpu.sync_copy(idx_hbm.at[rows], idx_v)
            pltpu.sync_copy(table_hbm.at[idx_v], rows_v)
            pltpu.sync_copy(rows_v, out_hbm.at[rows])

    return gather(table, idx)


def _sc_scatter_rows(rows, pos, n_out):
    n, width = rows.shape
    k = pos.shape[0]

    @functools.partial(
        pl.kernel, out_type=jax.ShapeDtypeStruct((n_out, width), rows.dtype), mesh=_sc_mesh(),
        scratch_types=[pltpu.VMEM((k, SC_WINDOW), jnp.int32), pltpu.VMEM((SC_WINDOW, width), rows.dtype)],
        name="sc_scatter_rows")
    def scatter(rows_hbm, pos_hbm, out_hbm, pos_v, rows_v):
        first, count = _sc_worker_rows(n)

        @pl.loop(0, count // SC_WINDOW)
        def _(c):
            src = pl.ds(first + c * SC_WINDOW, SC_WINDOW)
            pltpu.sync_copy(pos_hbm.at[:, src], pos_v)
            pltpu.sync_copy(rows_hbm.at[src], rows_v)
            for j in range(k):
                pltpu.sync_copy(rows_v, out_hbm.at[pos_v.at[j]])

    return scatter(rows, pos)


def _combine_dense_kernel(x_ref, gate_ref, w_ref, fg_ref, *rest, final_norm):
    y_refs, o_ref = rest[:TOP_K], rest[TOP_K]
    w = w_ref[...]
    y_lo = y_hi = None
    for j in range(TOP_K):
        lo, hi = _unpack_bf16_pair(y_refs[j][...])
        wj = w[:, j:j + 1]
        y_lo = wj * lo if j == 0 else y_lo + wj * lo
        y_hi = wj * hi if j == 0 else y_hi + wj * hi
    out = x_ref[...] + gate_ref[0] * jnp.concatenate([y_lo, y_hi], axis=1)
    if final_norm:
        out = out * lax.rsqrt(jnp.mean(out * out, axis=-1, keepdims=True) + NORM_EPS) * fg_ref[...]
    o_ref[...] = out


def _combine_dense(x2, mod_l, gate_idx, seq, top_w, y4, final_g, final_norm, tm=512):
    n, d = x2.shape
    per_batch = seq // tm
    blocks = n // tm
    slot_spec = lambda j: pl.BlockSpec((tm, d // 2), lambda i: (j * blocks + i, 0))
    return pl.pallas_call(
        functools.partial(_combine_dense_kernel, final_norm=final_norm),
        grid=(blocks,),
        in_specs=[
            pl.BlockSpec((tm, d), lambda i: (i, 0)),
            pl.BlockSpec((1, 1, d), lambda i: (i // per_batch, 0, gate_idx)),
            pl.BlockSpec((tm, LANES), lambda i: (i, 0)),
            pl.BlockSpec((1, d), lambda i: (0, 0)),
        ] + [slot_spec(j) for j in range(TOP_K)],
        out_specs=pl.BlockSpec((tm, d), lambda i: (i, 0)),
        out_shape=jax.ShapeDtypeStruct((n, d), F32),
        compiler_params=_params("parallel"),
        name="moe_combine",
    )(x2, mod_l, top_w, final_g.reshape(1, d), *([y4] * TOP_K))


def _moe_ffn(x2, norm_g, mod_l, seq, router_w, router_b, w1_all, b1, w2_all, b2, layer, final_g, final_norm):
    n, d = x2.shape
    n_exp = router_w.shape[1]
    tile = EXPERT_TILE
    hp, idx_f, top_w, rank_f, counts = _router(x2, norm_g, mod_l, 3, seq, router_w, router_b)
    counts = counts[0, :n_exp].astype(jnp.int32)
    padded = (counts + tile - 1) // tile * tile
    ends = jnp.cumsum(padded)
    starts = ends - padded
    idx = idx_f[:, :TOP_K].astype(jnp.int32)
    expert_ids = jnp.arange(n_exp, dtype=jnp.int32)
    start_of = jnp.sum(jnp.where(idx[..., None] == expert_ids, starts, 0), axis=-1)
    pos_flat = (start_of + rank_f[:, :TOP_K].astype(jnp.int32)).reshape(-1)
    n_rows = n * TOP_K + n_exp * tile
    n_tiles = n_rows // tile
    tile_start = jnp.arange(n_tiles, dtype=jnp.int32) * tile
    tile_expert = jnp.sum((ends[None, :] <= tile_start[:, None]).astype(jnp.int32), axis=1)
    tile_expert = jnp.minimum(tile_expert, n_exp - 1)
    n_used = (ends[-1:] // tile).astype(jnp.int32)
    nonempty = counts > 0
    ordinal = jnp.cumsum(nonempty.astype(jnp.int32)) - 1
    later = nonempty[None, :] & (expert_ids[None, :] > expert_ids[:, None])
    nxt_e = jnp.min(jnp.where(later, expert_ids[None, :], n_exp), axis=1)
    nxt_e = jnp.where(nxt_e == n_exp, -1, nxt_e)
    slot = (ordinal[tile_expert] % 2).astype(jnp.int32)
    nxt = nxt_e[tile_expert].astype(jnp.int32)
    pos_slot_major = pos_flat.reshape(n, TOP_K).T
    xs = _sc_scatter_rows(hp, pos_slot_major, n_rows)
    filled = jnp.clip((starts + counts)[tile_expert] - tile_start, 0, tile).astype(jnp.int32)
    ys = _experts(xs, tile_expert, n_used, slot, nxt, filled, w1_all, b1, w2_all, b2, layer)
    y4 = _sc_gather_rows(ys, pos_slot_major.reshape(-1))
    return _combine_dense(x2, mod_l, 5, seq, top_w, y4, final_g, final_norm)


def kernel(x, c, ada_w, ada_b, norm_mix_g, norm_ffn_g, router_w, router_b, moe_w1, moe_b1, moe_w2, moe_b2, ab_w_in, ab_w_out, hgrn_lb, hgrn_norm_g, rwkv_mu, rwkv_w0, rwkv_w2, rwkv_a0, rwkv_a2, rwkv_g2, rwkv_k_k, rwkv_k_a, rwkv_r_k, rwkv_ln_g, rwkv_ln_b, attn_w_in, attn_w_out, attn_lambda, attn_subln_g, rel_bias_table, final_norm_g):
    batch, seq, d = x.shape
    n = batch * seq
    depth = ada_w.shape[0]
    x2 = x.reshape(n, d)
    mod = _adaln(c, ada_w, ada_b)
    lb_all = jnp.cumsum(jax.nn.softmax(hgrn_lb.astype(F32), axis=1), axis=1)
    for layer in range(depth):
        mod_l = mod[layer].reshape(batch, 1, 6 * d)
        j = layer // 2
        if layer % 2 == 0:
            a_cols = 5 * (d // 2)
            w_in = ab_w_in[j].astype(BF16)
            p_a, p_b = _normmod_proj(x2, norm_mix_g[layer], mod_l, 0, seq,
                                     [w_in[:, :a_cols], w_in[:, a_cols:]], [F32, F32])
            y_a = _hgrn2(p_a, lb_all[:, j], hgrn_norm_g[j], batch, seq)
            y_b = _rwkv7(p_b, rwkv_mu[j], rwkv_w0[j], rwkv_w2[j], rwkv_a0[j], rwkv_a2[j], rwkv_g2[j],
                         rwkv_k_k[j], rwkv_k_a[j], rwkv_r_k[j], rwkv_ln_g[j], rwkv_ln_b[j], batch, seq)
            w_out = ab_w_out[j].astype(BF16)
            x2 = _outproj_residual(x2, mod_l, 2, seq, [y_a, y_b], [w_out[:d // 2], w_out[d // 2:]])
        else:
            (qkv,) = _normmod_proj(x2, norm_mix_g[layer], mod_l, 0, seq, [attn_w_in[j].astype(BF16)], [BF16])
            lam = attn_lambda[j].astype(F32)
            lam_init = 0.8 - 0.6 * math.exp(-0.3 * layer)
            lam_full = jnp.exp(jnp.sum(lam[0] * lam[1])) - jnp.exp(jnp.sum(lam[2] * lam[3])) + lam_init
            o = _diff_attention(qkv, rel_bias_table, lam_full, attn_subln_g[j], lam_init, batch, seq)
            x2 = _outproj_residual(x2, mod_l, 2, seq, [o], [attn_w_out[j].astype(BF16)])
        x2 = _moe_ffn(x2, norm_ffn_g[layer], mod_l, seq, router_w[layer], router_b[layer], moe_w1,
                      moe_b1[layer], moe_w2, moe_b2[layer], layer, final_norm_g, layer == depth - 1)
    return x2.reshape(batch, seq, d)
```

```python
import functools
import math

import jax
import jax.numpy as jnp
import numpy as np
from jax import lax
from jax.experimental import pallas as pl
from jax.experimental.pallas import tpu as pltpu
from jax.experimental.pallas import tpu_sc as plsc

F32 = jnp.float32
BF16 = jnp.bfloat16
HIGHEST = lax.Precision.HIGHEST

NORM_EPS = 1e-6

A_HEAD_DIM = 128
A_CHUNK = 32
A_GROUP = 512
A_HEADS_PER_STEP = 2
B_HEAD_DIM = 64
B_CHUNK = 64
B_STEPS = 2
B_LN_EPS = 1e-5 * B_HEAD_DIM
C_HEAD_DIM = 64
REL_BUCKETS = 32
REL_MAX_DISTANCE = 128
TOP_K = 4
SWIGLU_LIMIT = 7.0
SWIGLU_ALPHA = 1.702

V7X_VMEM_BYTES = 64 * 1024 * 1024
VMEM_LIMIT = V7X_VMEM_BYTES - 8 * 1024 * 1024
LANES = 128


def _params(*sem):
    return pltpu.CompilerParams(dimension_semantics=sem, vmem_limit_bytes=VMEM_LIMIT)


def _sigmoid(x):
    return 1.0 / (1.0 + jnp.exp(-x))


def _silu(x):
    return x * _sigmoid(x)


def _dot(a, b):
    return jnp.dot(a, b, preferred_element_type=F32)


def _dot_nt(a, b):
    return lax.dot_general(a, b, (((1,), (1,)), ((), ())), preferred_element_type=F32)


def _dot_tn(a, b):
    return lax.dot_general(a, b, (((0,), (0,)), ((), ())), preferred_element_type=F32)


def _adaln_kernel(c_ref, w_ref, b_ref, o_ref):
    cond = _silu(c_ref[...])
    o_ref[0] = jnp.dot(cond, w_ref[0], precision=HIGHEST, preferred_element_type=F32) + b_ref[0]


def _adaln(c, ada_w, ada_b):
    n_layers, d, n_out = ada_w.shape
    batch = c.shape[0]
    tn = 1536
    return pl.pallas_call(
        _adaln_kernel,
        grid=(n_layers, n_out // tn),
        in_specs=[
            pl.BlockSpec((batch, d), lambda l, j: (0, 0)),
            pl.BlockSpec((1, d, tn), lambda l, j: (l, 0, j)),
            pl.BlockSpec((1, 1, tn), lambda l, j: (l, 0, j)),
        ],
        out_specs=pl.BlockSpec((1, batch, tn), lambda l, j: (l, 0, j)),
        out_shape=jax.ShapeDtypeStruct((n_layers, batch, n_out), F32),
        compiler_params=_params("parallel", "parallel"),
        name="adaln",
    )(c, ada_w, ada_b.reshape(n_layers, 1, n_out))


def _modulated_norm(x, g, shift, scale):
    y = x * lax.rsqrt(jnp.mean(x * x, axis=-1, keepdims=True) + NORM_EPS)
    return (y * g) * (1.0 + scale) + shift


def _normmod_proj_kernel(x_ref, g_ref, sh_ref, sc_ref, *rest, n_w):
    w_refs, o_refs = rest[:n_w], rest[n_w:]
    h = _modulated_norm(x_ref[...], g_ref[...], sh_ref[0], sc_ref[0]).astype(BF16)
    for w_ref, o_ref in zip(w_refs, o_refs):
        o_ref[...] = _dot(h, w_ref[...]).astype(o_ref.dtype)


def _normmod_proj(x2, g, mod_l, shift_idx, seq, weights, out_dtypes, tm=512):
    n, d = x2.shape
    per_batch = seq // tm
    in_specs = [
        pl.BlockSpec((tm, d), lambda i: (i, 0)),
        pl.BlockSpec((1, d), lambda i: (0, 0)),
        pl.BlockSpec((1, 1, d), lambda i: (i // per_batch, 0, shift_idx)),
        pl.BlockSpec((1, 1, d), lambda i: (i // per_batch, 0, shift_idx + 1)),
    ]
    in_specs += [pl.BlockSpec(w.shape, lambda i: (0, 0)) for w in weights]
    out_specs = [pl.BlockSpec((tm, w.shape[1]), lambda i: (i, 0)) for w in weights]
    out_shape = [jax.ShapeDtypeStruct((n, w.shape[1]), dt) for w, dt in zip(weights, out_dtypes)]
    return pl.pallas_call(
        functools.partial(_normmod_proj_kernel, n_w=len(weights)),
        grid=(n // tm,),
        in_specs=in_specs,
        out_specs=out_specs,
        out_shape=out_shape,
        compiler_params=_params("parallel"),
        name="normmod_proj",
    )(x2, g.reshape(1, d), mod_l, mod_l, *weights)


def _outproj_kernel(x_ref, gate_ref, *rest, n_y):
    y_refs, w_refs, o_ref = rest[:n_y], rest[n_y:2 * n_y], rest[2 * n_y]
    acc = _dot(y_refs[0][...], w_refs[0][...])
    for y_ref, w_ref in zip(y_refs[1:], w_refs[1:]):
        acc += _dot(y_ref[...], w_ref[...])
    o_ref[...] = x_ref[...] + gate_ref[0] * acc


def _outproj_residual(x2, mod_l, gate_idx, seq, ys, ws, tm=512):
    n, d = x2.shape
    per_batch = seq // tm
    in_specs = [
        pl.BlockSpec((tm, d), lambda i: (i, 0)),
        pl.BlockSpec((1, 1, d), lambda i: (i // per_batch, 0, gate_idx)),
    ]
    in_specs += [pl.BlockSpec((tm, y.shape[1]), lambda i: (i, 0)) for y in ys]
    in_specs += [pl.BlockSpec(w.shape, lambda i: (0, 0)) for w in ws]
    return pl.pallas_call(
        functools.partial(_outproj_kernel, n_y=len(ys)),
        grid=(n // tm,),
        in_specs=in_specs,
        out_specs=pl.BlockSpec((tm, d), lambda i: (i, 0)),
        out_shape=jax.ShapeDtypeStruct((n, d), F32),
        compiler_params=_params("parallel"),
        name="outproj_residual",
    )(x2, mod_l, *ys, *ws)


def _chunk_cumsum(x, chunk, reverse):
    rows = x.shape[0]
    pos = lax.broadcasted_iota(jnp.int32, x.shape, 0) % chunk
    s = 1
    while s < chunk:
        if reverse:
            x = x + jnp.where(pos < chunk - s, pltpu.roll(x, rows - s, axis=0), 0.0)
        else:
            x = x + jnp.where(pos >= s, pltpu.roll(x, s, axis=0), 0.0)
        s *= 2
    return x


def _hgrn_groups(slabs):
    g_rows, dk = slabs[0][0].shape
    n_chunks = g_rows // A_CHUNK
    ti = lax.broadcasted_iota(jnp.int32, (A_CHUNK, A_CHUNK), 0)
    si = lax.broadcasted_iota(jnp.int32, (A_CHUNK, A_CHUNK), 1)
    chunk_rows = [slice(c * A_CHUNK, (c + 1) * A_CHUNK) for c in range(n_chunks)]
    prep = []
    for q, f, v, lb, st, reverse in slabs:
        fg = lb + (1.0 - lb) * _sigmoid(f)
        k = 1.0 - fg
        b = _chunk_cumsum(jnp.log(fg), A_CHUNK, reverse)
        b3 = b.reshape(n_chunks, A_CHUNK, dk)
        edge = b3[:, 0:1, :] if reverse else b3[:, A_CHUNK - 1:A_CHUNK, :]
        prep.append(dict(
            q_in=(q * jnp.exp(b)).astype(BF16), k_in=(k * jnp.exp(-b)).astype(BF16),
            k_st=(k.reshape(n_chunks, A_CHUNK, dk) * jnp.exp(edge - b3)).astype(BF16),
            decay=jnp.exp(edge), vb=v.astype(BF16), st=st, reverse=reverse,
            mask=(si >= ti) if reverse else (si <= ti)))
    for p in prep:
        p["scores"] = [_dot_nt(p["q_in"][r], p["k_in"][r]) for r in chunk_rows]
        p["dstate"] = [_dot_tn(p["vb"][r], p["k_st"][c]) for c, r in enumerate(chunk_rows)]
    for p in prep:
        p["intra"] = [_dot(jnp.where(p["mask"], s, 0.0).astype(BF16), p["vb"][r])
                      for s, r in zip(p["scores"], chunk_rows)]
    results = []
    for p in prep:
        st = p["st"]
        outs = [None] * n_chunks
        for c in (range(n_chunks - 1, -1, -1) if p["reverse"] else range(n_chunks)):
            outs[c] = p["intra"][c] + _dot_nt(p["q_in"][chunk_rows[c]], st.astype(BF16))
            st = st * p["decay"][c] + p["dstate"][c]
        results.append((jnp.concatenate(outs, axis=0), st))
    return results


def _hgrn_kernel(q_ref, ff_ref, fb_ref, i_ref, g_ref, lb_ref, ng_ref, o_ref, of_ref, ob_ref):
    seq = q_ref.shape[0]
    dk = A_HEAD_DIM
    n_heads = q_ref.shape[1] // dk
    n_groups = seq // A_GROUP
    head_cols = [slice(h * dk, (h + 1) * dk) for h in range(n_heads)]

    def body(j, states):
        rf = pl.ds(pl.multiple_of(j * A_GROUP, A_GROUP), A_GROUP)
        rb = pl.ds(pl.multiple_of((n_groups - 1 - j) * A_GROUP, A_GROUP), A_GROUP)
        slabs = []
        for h, cols in enumerate(head_cols):
            slabs.append((_silu(q_ref[rf, cols]), ff_ref[rf, cols], i_ref[rf, cols], lb_ref[0:1, cols],
                          states[2 * h], False))
            slabs.append((_silu(q_ref[rb, cols]), fb_ref[rb, cols], i_ref[rb, cols], lb_ref[1:2, cols],
                          states[2 * h + 1], True))
        results = _hgrn_groups(slabs)
        for h, cols in enumerate(head_cols):
            of_ref[rf, cols] = results[2 * h][0]
            ob_ref[rb, cols] = results[2 * h + 1][0]
        return tuple(r[1] for r in results)

    zero = jnp.zeros((dk, dk), F32)
    lax.fori_loop(0, n_groups, body, (zero,) * (2 * n_heads))

    def finish(j, carry):
        r = pl.ds(pl.multiple_of(j * A_GROUP, A_GROUP), A_GROUP)
        for cols in head_cols:
            o = of_ref[r, cols] + ob_ref[r, cols]
            o = o * lax.rsqrt(jnp.mean(o * o, axis=-1, keepdims=True) + NORM_EPS) * ng_ref[...]
            o_ref[r, cols] = (o * _silu(g_ref[r, cols])).astype(o_ref.dtype)
        return carry

    lax.fori_loop(0, n_groups, finish, 0)


def _hgrn2(p_a, lb, norm_g, batch, seq):
    n = p_a.shape[0]
    width = p_a.shape[1] // 5
    cols = A_HEADS_PER_STEP * A_HEAD_DIM
    steps = width // cols
    sect = lambda s: pl.BlockSpec((seq, cols), lambda b, h, s=s: (b, s * steps + h))
    return pl.pallas_call(
        _hgrn_kernel,
        grid=(batch, steps),
        in_specs=[sect(0), sect(1), sect(2), sect(3), sect(4),
                  pl.BlockSpec((2, cols), lambda b, h: (0, h)),
                  pl.BlockSpec((1, A_HEAD_DIM), lambda b, h: (0, 0))],
        out_specs=pl.BlockSpec((seq, cols), lambda b, h: (b, h)),
        out_shape=jax.ShapeDtypeStruct((n, width), BF16),
        scratch_shapes=[pltpu.VMEM((seq, cols), F32), pltpu.VMEM((seq, cols), F32)],
        compiler_params=_params("parallel", "parallel"),
        name="hgrn2",
    )(p_a, p_a, p_a, p_a, p_a, lb, norm_g.reshape(1, A_HEAD_DIM))


def _softplus(z):
    return jnp.maximum(z, 0.0) + jnp.log(1.0 + jnp.exp(-jnp.abs(z)))


def _rwkv_prep_kernel(p_ref, prev_ref, next_ref, mu_ref, w0_ref, w2_ref, a0_ref, a2_ref, g2_ref,
                      kk_ref, ka_ref, rk_ref, ones_ref, tri_ref,
                      r_out, k_out, v_out, kkn_out, b_out, cumf_out, cumb_out, g_out, bonus_out):
    i = pl.program_id(1)
    last = pl.num_programs(1) - 1
    ts = p_ref.shape[0]
    width = a0_ref.shape[1]
    heads = width // B_HEAD_DIM
    p = p_ref[...]
    prev_row = jnp.where(i == 0, 0.0, prev_ref[7:8, :])
    next_row = jnp.where(i == last, 0.0, next_ref[0:1, :])
    row = lax.broadcasted_iota(jnp.int32, (ts, 1), 0)
    up = jnp.where(row == 0, prev_row, pltpu.roll(p, 1, axis=0))
    dn = jnp.where(row == ts - 1, next_row, pltpu.roll(p, ts - 1, axis=0))
    p = p + mu_ref[...] * (0.5 * (up + dn) - p)

    r = p[:, 0:width]
    k = p[:, width:2 * width]
    v = p[:, 2 * width:3 * width]
    o = 3 * width
    wlo_f = p[:, o:o + 64]
    wlo_b = p[:, o + 64:o + 128]
    alo = p[:, o + 128:o + 192]
    glo = p[:, o + 192:o + 320]

    def log_decay(wlo, d):
        z = w0_ref[d:d + 1, :] + _dot(jnp.tanh(wlo).astype(BF16), w2_ref[d])
        w = -_softplus(-z) - 0.5
        return -jnp.exp(w)

    def split_dot(sel, t, right):
        hi = t.astype(BF16)
        lo = (t - hi.astype(F32)).astype(BF16)
        return (_dot(hi, sel) + _dot(lo, sel)) if right else (_dot(sel, hi) + _dot(sel, lo))

    cum_f = split_dot(tri_ref[0], log_decay(wlo_f, 0), False)
    cum_b = split_dot(tri_ref[1], log_decay(wlo_b, 1), False)
    a = _sigmoid(a0_ref[...] + _dot(alo.astype(BF16), a2_ref[...]))
    g = _dot(_sigmoid(glo).astype(BF16), g2_ref[...])
    kk = k * kk_ref[...]
    head_sum = lambda t: split_dot(ones_ref[...], t, True)

    kk_n = kk / jnp.maximum(jnp.sqrt(head_sum(kk * kk)), 1e-12)
    k_mod = k * (1.0 + (a - 1.0) * ka_ref[...])
    bonus = head_sum(r * k_mod * rk_ref[...]) * v
    g_out[...] = g
    bonus_out[...] = bonus
    for h in range(heads):
        sl = slice(h * B_HEAD_DIM, (h + 1) * B_HEAD_DIM)
        r_out[0, h] = r[:, sl]
        k_out[0, h] = k_mod[:, sl]
        v_out[0, h] = v[:, sl]
        kkn_out[0, h] = kk_n[:, sl]
        b_out[0, h] = (kk_n * a)[:, sl]
        cumf_out[0, h] = cum_f[:, sl]
        cumb_out[0, h] = cum_b[:, sl]


def _rwkv_prep(p_b, mu, w0, w2, a0, a2, g2, k_k, k_a, r_k, batch, seq, ts=256):
    n, cols = p_b.shape
    width = a0.shape[0]
    heads = width // B_HEAD_DIM
    nblk = seq // ts
    rows8 = ts // 8
    head_id = np.arange(width) // B_HEAD_DIM
    ones = jnp.asarray(head_id[:, None] == head_id[None, :], BF16)
    full = lambda a: pl.BlockSpec(a.shape, lambda b, i: (0,) * a.ndim)
    vec = lambda a: a.reshape(1, -1)
    row = np.arange(ts)
    same_chunk = (row[:, None] // B_CHUNK) == (row[None, :] // B_CHUNK)
    tri = jnp.asarray(np.stack([same_chunk & (row[None, :] <= row[:, None]),
                                same_chunk & (row[None, :] >= row[:, None])]), BF16)
    args = [vec(mu), w0, w2.astype(BF16), vec(a0), a2.astype(BF16), g2.astype(BF16),
            vec(k_k), vec(k_a), vec(r_k), ones, tri]
    hm = pl.BlockSpec((1, heads, ts, B_HEAD_DIM), lambda b, i: (b, 0, i, 0))
    tokm = pl.BlockSpec((ts, width), lambda b, i: (b * nblk + i, 0))
    hm_shape = jax.ShapeDtypeStruct((batch, heads, seq, B_HEAD_DIM), F32)
    tok_shape = jax.ShapeDtypeStruct((n, width), F32)
    return pl.pallas_call(
        _rwkv_prep_kernel,
        grid=(batch, nblk),
        in_specs=[
            pl.BlockSpec((ts, cols), lambda b, i: (b * nblk + i, 0)),
            pl.BlockSpec((8, cols), lambda b, i: (jnp.maximum((b * nblk + i) * rows8 - 1, 0), 0)),
            pl.BlockSpec((8, cols), lambda b, i: (jnp.minimum((b * nblk + i + 1) * rows8, n // 8 - 1), 0)),
        ] + [full(a) for a in args],
        out_specs=[hm] * 7 + [tokm] * 2,
        out_shape=[hm_shape] * 7 + [tok_shape] * 2,
        compiler_params=_params("parallel", "parallel"),
        name="rwkv_prep",
    )(p_b, p_b, p_b, *args)


def _rwkv_chunks(chains):
    c, d = chains[0][0].shape
    ti = lax.broadcasted_iota(jnp.int32, (c, c), 0)
    si = lax.broadcasted_iota(jnp.int32, (c, c), 1)
    eye = jnp.where(ti == si, 1.0, 0.0)
    masks = {rev: (((si > ti), (si >= ti)) if rev else ((si < ti), (si <= ti))) for rev in (False, True)}
    t2 = lax.broadcasted_iota(jnp.int32, (c, 2 * c), 0)
    s2 = lax.broadcasted_iota(jnp.int32, (c, 2 * c), 1)
    s2 = jnp.where(s2 >= c, s2 - c, s2)
    incl2 = {False: s2 <= t2, True: s2 >= t2}

    row = lax.broadcasted_iota(jnp.int32, (c, d), 0)
    st = []
    for r, k, v, kk, b, cum, rev in chains:
        before = (jnp.where(row == c - 1, 0.0, pltpu.roll(cum, c - 1, axis=0)) if rev
                  else jnp.where(row == 0, 0.0, pltpu.roll(cum, 1, axis=0)))
        grow = jnp.exp(-cum)
        kt = (kk * jnp.exp(before)).astype(BF16)
        rt = r * jnp.exp(cum)
        b_ = (b * grow).astype(BF16)
        kb = jnp.concatenate([(k * grow).astype(BF16), b_], axis=0)
        st.append(dict(vb=v.astype(BF16), kt=kt, rt=rt, b_=b_, kb=kb,
                       decay=jnp.exp(cum[0:1, :] if rev else cum[c - 1:c, :]), rev=rev,
                       lhs=jnp.concatenate([kt, rt.astype(BF16)], axis=0)))
    for s in st:
        s["big"] = _dot_nt(s["lhs"], s["kb"])
    for s in st:
        strict, incl = masks[s["rev"]]
        big = s["big"]
        a_ab = jnp.where(strict, big[:c, c:], 0.0)
        s["a_ak"] = jnp.where(strict, big[:c, :c], 0.0).astype(BF16)
        s["p"] = jnp.where(incl2[s["rev"]], big[c:], 0.0).astype(BF16)
        s["m"] = eye - a_ab
        s["ab"] = a_ab.astype(BF16)
    for s in st:
        s["aj"] = _dot(s["ab"], s["ab"])
        s["av"] = _dot(s["a_ak"], s["vb"])
    span = 2
    while span < c:
        span *= 2
        for s in st:
            ajb = s["aj"].astype(BF16)
            if span < c:
                both = _dot(jnp.concatenate([ajb, s["m"].astype(BF16)], axis=0), ajb)
                s["aj"], s["m"] = both[:c], s["m"] + both[c:]
            else:
                s["m"] = s["m"] + _dot(s["m"].astype(BF16), ajb)
    for s in st:
        wu = _dot(s["m"].astype(BF16), jnp.concatenate([s["kt"], s["av"].astype(BF16)], axis=1))
        s["nwu"] = -wu.astype(BF16)
    for s in st:
        nw, nu0, vb = s["nwu"][:, :d], s["nwu"][:, d:], s["vb"]
        rhs = jnp.concatenate([jnp.concatenate([vb, jnp.zeros_like(vb)], axis=1),
                               jnp.concatenate([nu0, nw], axis=1)], axis=0)
        s["yq"] = _dot(s["p"], rhs)
        s["h_t"] = _dot_tn(jnp.concatenate([vb, nu0], axis=0), s["kb"])
        s["nbw"] = _dot_tn(s["b_"], nw)
    return [dict(q=(s["rt"] + s["yq"][:, d:]).astype(BF16), y1=s["yq"][:, :d], nbw=s["nbw"].astype(BF16),
                 h_t=s["h_t"], decay=s["decay"]) for s in st]


def _rwkv_advance(states, steps):
    ttbs = [tt.astype(BF16) for tt in states]
    ys = [_dot_nt(s["q"], ttb) + s["y1"] for s, ttb in zip(steps, ttbs)]
    new = [(tt + _dot_nt(ttb, s["nbw"]) + s["h_t"]) * s["decay"] for tt, ttb, s in zip(states, ttbs, steps)]
    return ys, new


def _rwkv_scan_kernel(rf, kf, vf, kkf, bf, lwf, rb, kb, vb, kkb, bb, lwb, yf_ref, yb_ref, tf_ref, tb_ref):
    heads, ts = rf.shape[1], rf.shape[2]
    n_chunks = ts // B_CHUNK

    @pl.when(pl.program_id(1) == 0)
    def _():
        tf_ref[...] = jnp.zeros_like(tf_ref)
        tb_ref[...] = jnp.zeros_like(tb_ref)

    def body(j, carry):
        rows_f = [pl.ds(pl.multiple_of((j * B_STEPS + s) * B_CHUNK, B_CHUNK), B_CHUNK) for s in range(B_STEPS)]
        rows_b = [pl.ds(pl.multiple_of((n_chunks - 1 - j * B_STEPS - s) * B_CHUNK, B_CHUNK), B_CHUNK)
                  for s in range(B_STEPS)]
        chains = []
        for s in range(B_STEPS):
            for h in range(heads):
                chains.append([ref[0, h, rows_f[s], :] for ref in (rf, kf, vf, kkf, bf, lwf)] + [False])
                chains.append([ref[0, h, rows_b[s], :] for ref in (rb, kb, vb, kkb, bb, lwb)] + [True])
        solved = _rwkv_chunks(chains)
        states = []
        for h in range(heads):
            states += [tf_ref[h], tb_ref[h]]
        for s in range(B_STEPS):
            ys, states = _rwkv_advance(states, solved[s * 2 * heads:(s + 1) * 2 * heads])
            for h in range(heads):
                yf_ref[0, h, rows_f[s], :] = ys[2 * h]
                yb_ref[0, h, rows_b[s], :] = ys[2 * h + 1]
        for h in range(heads):
            tf_ref[h], tb_ref[h] = states[2 * h], states[2 * h + 1]
        return carry

    lax.fori_loop(0, n_chunks // B_STEPS, body, 0)


def _rwkv_scan(r, k, v, kkn, b, lw_f, lw_b, ts=256):
    batch, heads, seq, d = r.shape
    nblk = seq // ts
    fwd = pl.BlockSpec((1, heads, ts, d), lambda bi, i: (bi, 0, i, 0))
    bwd = pl.BlockSpec((1, heads, ts, d), lambda bi, i: (bi, 0, nblk - 1 - i, 0))
    shape = jax.ShapeDtypeStruct(r.shape, F32)
    return pl.pallas_call(
        _rwkv_scan_kernel,
        grid=(batch, nblk),
        in_specs=[fwd] * 6 + [bwd] * 6,
        out_specs=[fwd, bwd],
        out_shape=[shape, shape],
        scratch_shapes=[pltpu.VMEM((heads, d, d), F32), pltpu.VMEM((heads, d, d), F32)],
        compiler_params=_params("parallel", "arbitrary"),
        name="rwkv_scan",
    )(r, k, v, kkn, b, lw_f, r, k, v, kkn, b, lw_b)


def _rwkv_post_kernel(yf_ref, yb_ref, g_ref, bonus_ref, lng_ref, lnb_ref, o_ref):
    heads = yf_ref.shape[1]
    outs = []
    for h in range(heads):
        y = yf_ref[0, h] + yb_ref[0, h]
        mean = jnp.mean(y, axis=-1, keepdims=True)
        var = jnp.mean(jnp.square(y - mean), axis=-1, keepdims=True)
        outs.append((y - mean) * lax.rsqrt(var + B_LN_EPS))
    yn = jnp.concatenate(outs, axis=1) * lng_ref[...] + lnb_ref[...]
    o_ref[...] = ((yn + bonus_ref[...]) * g_ref[...]).astype(o_ref.dtype)


def _rwkv_post(y_f, y_b, g, bonus, ln_g, ln_b, ts=512):
    batch, heads, seq, d = y_f.shape
    n, width = g.shape
    nblk = seq // ts
    hm = pl.BlockSpec((1, heads, ts, d), lambda b, i: (b, 0, i, 0))
    tokm = pl.BlockSpec((ts, width), lambda b, i: (b * nblk + i, 0))
    vec = pl.BlockSpec((1, width), lambda b, i: (0, 0))
    return pl.pallas_call(
        _rwkv_post_kernel,
        grid=(batch, nblk),
        in_specs=[hm, hm, tokm, tokm, vec, vec],
        out_specs=tokm,
        out_shape=jax.ShapeDtypeStruct((n, width), BF16),
        compiler_params=_params("parallel", "parallel"),
        name="rwkv_post",
    )(y_f, y_b, g, bonus, ln_g.reshape(1, width), ln_b.reshape(1, width))


def _rwkv7(p_b, mu, w0, w2, a0, a2, g2, k_k, k_a, r_k, ln_g, ln_b, batch, seq):
    r, k, v, kkn, b, lw_f, lw_b, g, bonus = _rwkv_prep(p_b, mu, w0, w2, a0, a2, g2, k_k, k_a, r_k, batch, seq)
    y_f, y_b = _rwkv_scan(r, k, v, kkn, b, lw_f, lw_b)
    return _rwkv_post(y_f, y_b, g, bonus, ln_g, ln_b)


_BUCKET_EDGES = (0, 1, 2, 3, 4, 5, 6, 7, 8, 12, 16, 23, 32, 46, 64, 91)
ATTN_TK = 128
C_HEADS_PER_STEP = 4


def _bias_band_kernel(table_ref, o_ref):
    h = pl.program_id(0)
    n_d, tq, tk = o_ref.shape[1:]
    half = REL_BUCKETS // 2
    r = lax.broadcasted_iota(jnp.int32, (tq, tk), 0)
    c = lax.broadcasted_iota(jnp.int32, (tq, tk), 1)
    for d in range(n_d):
        rel = tk * (d - 1) + c - r
        n = jnp.abs(rel)
        vals = []
        for sign in range(2):
            val = jnp.full((tq, tk), table_ref[sign * half + half - 1, h], F32)
            for bkt in range(half - 2, -1, -1):
                val = jnp.where(n < _BUCKET_EDGES[bkt + 1], table_ref[sign * half + bkt, h], val)
            vals.append(val)
        o_ref[0, d] = jnp.where(rel > 0, vals[1], vals[0])


def _bias_band(rel_table, tq):
    heads = rel_table.shape[1]
    n_d = tq // ATTN_TK + 2
    return pl.pallas_call(
        _bias_band_kernel,
        grid=(heads,),
        in_specs=[pl.BlockSpec(memory_space=pltpu.SMEM)],
        out_specs=pl.BlockSpec((1, n_d, tq, ATTN_TK), lambda h: (h, 0, 0, 0)),
        out_shape=jax.ShapeDtypeStruct((heads, n_d, tq, ATTN_TK), F32),
        compiler_params=_params("parallel"),
        name="bias_band",
    )(rel_table)


def _diff_attn_kernel(table_ref, lam_ref, q_ref, k_ref, v_ref, band_ref, g_ref, o_ref, lg_ref, vext_ref, kt_ref,
                      *, lam_init):
    i = pl.program_id(2)
    tq = q_ref.shape[0]
    dh2 = 2 * C_HEAD_DIM
    n_heads = q_ref.shape[1] // dh2
    seq = k_ref.shape[0]
    n_d = band_ref.shape[1]
    n_kt = seq // ATTN_TK
    ratio = tq // ATTN_TK
    half = REL_BUCKETS // 2
    head_cols = [slice(hh * dh2, (hh + 1) * dh2) for hh in range(n_heads)]

    @pl.when(i == 0)
    def _():
        for hh, cols in enumerate(head_cols):
            kt_ref[hh] = k_ref[:, cols].astype(F32).T.astype(BF16)
            v = v_ref[:, cols]
            vext_ref[hh] = jnp.concatenate([v, jnp.ones_like(v)], axis=1)

    lane = lax.broadcasted_iota(jnp.int32, (1, dh2), 1)
    first_band = ratio * i - 1
    col = lax.broadcasted_iota(jnp.int32, (1, seq), 1)
    for hh, cols in enumerate(head_cols):
        h = pl.program_id(1) * n_heads + hh
        q = q_ref[:, cols] * jnp.asarray(C_HEAD_DIM ** -0.5, q_ref.dtype)
        zero = jnp.zeros_like(q)
        far_row = jnp.where(col < first_band * ATTN_TK, table_ref[half - 1, h],
                            jnp.where(col >= (first_band + n_d) * ATTN_TK, table_ref[REL_BUCKETS - 1, h],
                                      -jnp.inf))
        row_max = []
        for m in range(2):
            qm = jnp.where((lane >= m * C_HEAD_DIM) & (lane < (m + 1) * C_HEAD_DIM), q, zero)
            far = _dot(qm, kt_ref[hh]) + far_row
            mx = jnp.max(far, axis=-1, keepdims=True)
            lg_ref[hh, m] = far
            for d in (0, n_d - 1) + tuple(range(1, n_d - 1)):
                kt = first_band + d
                valid = (kt >= 0) & (kt < n_kt)
                c0 = pl.multiple_of(jnp.clip(kt, 0, n_kt - 1) * ATTN_TK, ATTN_TK)
                near = _dot(qm, kt_ref[hh, :, pl.ds(c0, ATTN_TK)]) + band_ref[hh, d]
                mx = jnp.maximum(mx, jnp.where(valid, jnp.max(near, axis=-1, keepdims=True), -jnp.inf))
                lg_ref[hh, m, :, pl.ds(c0, ATTN_TK)] = near
            row_max.append(mx)
        parts = []
        for m in range(2):
            e = jnp.exp(lg_ref[hh, m] - row_max[m]).astype(BF16)
            pv = _dot(e, vext_ref[hh])
            parts.append(pv[:, :dh2] * (1.0 / pv[:, dh2:dh2 + 1]))
        o = parts[0] - lam_ref[0] * parts[1]
        o = o * lax.rsqrt(jnp.mean(o * o, axis=-1, keepdims=True) + NORM_EPS) * g_ref[...]
        o_ref[:, cols] = (o * (1.0 - lam_init)).astype(o_ref.dtype)


def _diff_attention(qkv, rel_table, lam_full, subln_g, lam_init, batch, seq, tq=256):
    n = qkv.shape[0]
    dh2 = 2 * C_HEAD_DIM
    heads = qkv.shape[1] // (3 * dh2)
    nq = seq // tq
    per = C_HEADS_PER_STEP
    groups = heads // per
    cols = per * dh2
    band = _bias_band(rel_table, tq)
    return pl.pallas_call(
        functools.partial(_diff_attn_kernel, lam_init=lam_init),
        grid=(batch, groups, nq),
        in_specs=[
            pl.BlockSpec(memory_space=pltpu.SMEM),
            pl.BlockSpec(memory_space=pltpu.SMEM),
            pl.BlockSpec((tq, cols), lambda b, h, i: (b * nq + i, h)),
            pl.BlockSpec((seq, cols), lambda b, h, i: (b, groups + h)),
            pl.BlockSpec((seq, cols), lambda b, h, i: (b, 2 * groups + h)),
            pl.BlockSpec((per,) + band.shape[1:], lambda b, h, i: (h, 0, 0, 0)),
            pl.BlockSpec((1, dh2), lambda b, h, i: (0, 0)),
        ],
        out_specs=pl.BlockSpec((tq, cols), lambda b, h, i: (b * nq + i, h)),
        out_shape=jax.ShapeDtypeStruct((n, heads * dh2), BF16),
        scratch_shapes=[pltpu.VMEM((per, 2, tq, seq), F32), pltpu.VMEM((per, seq, 2 * dh2), BF16),
                        pltpu.VMEM((per, dh2, seq), BF16)],
        compiler_params=_params("parallel", "parallel", "arbitrary"),
        name="diff_attention",
    )(rel_table, lam_full.reshape(1), qkv, qkv, qkv, band, subln_g.reshape(1, dh2))


MOE_TILE = 256
EXPERT_TILE = 512
EXPERT_CHUNK = 256
_HI_MASK = np.uint32(0xFFFF0000)


def _pack_bf16_pair(lo, hi):
    as_bits = lambda t: lax.bitcast_convert_type(t.astype(BF16).astype(F32), jnp.uint32)
    return (as_bits(hi) & _HI_MASK) | (as_bits(lo) >> 16)


def _unpack_bf16_pair(word):
    lo = lax.bitcast_convert_type(word << 16, F32).astype(BF16)
    hi = lax.bitcast_convert_type(word & _HI_MASK, F32).astype(BF16)
    return lo, hi


def _router_kernel(x_ref, g_ref, sh_ref, sc_ref, rw_ref, rb_ref, hp_ref, idx_ref, w_ref, rank_ref, cnt_ref,
                   run_ref, *, n_experts):
    @pl.when(pl.program_id(0) == 0)
    def _():
        run_ref[...] = jnp.zeros_like(run_ref)

    tm, d = x_ref.shape
    h = _modulated_norm(x_ref[...], g_ref[...], sh_ref[0], sc_ref[0])
    hp_ref[...] = _pack_bf16_pair(h[:, :d // 2], h[:, d // 2:])
    logits = _dot(h.astype(BF16), rw_ref[...]) + rb_ref[...]
    lane = lax.broadcasted_iota(jnp.int32, logits.shape, 1).astype(F32)
    neg = jnp.float32(-jnp.inf)
    logits = jnp.where(lane < n_experts, logits, neg)
    picks, vals = [], []
    for _ in range(TOP_K):
        m = jnp.max(logits, axis=-1, keepdims=True)
        first = jnp.min(jnp.where(logits == m, lane, float(LANES)), axis=-1, keepdims=True)
        hit = lane == first
        picks.append((first, hit))
        vals.append(m)
        logits = jnp.where(hit, neg, logits)
    es = [jnp.exp(v - vals[0]) for v in vals]
    inv = 1.0 / (es[0] + es[1] + es[2] + es[3])
    assigned = picks[0][1] | picks[1][1] | picks[2][1] | picks[3][1]
    onehot = jnp.where(assigned, 1.0, 0.0)
    ri = lax.broadcasted_iota(jnp.int32, (tm, tm), 0)
    ci = lax.broadcasted_iota(jnp.int32, (tm, tm), 1)
    earlier = jnp.where(ci < ri, 1.0, 0.0).astype(BF16)
    before = _dot(earlier, onehot.astype(BF16)) + run_ref[...]
    idx_o = jnp.zeros(logits.shape, F32)
    w_o = jnp.zeros(logits.shape, F32)
    rank_o = jnp.zeros(logits.shape, F32)
    for j, (first, hit) in enumerate(picks):
        rank_j = jnp.sum(jnp.where(hit, before, 0.0), axis=-1, keepdims=True)
        idx_o = jnp.where(lane == j, first, idx_o)
        w_o = jnp.where(lane == j, es[j] * inv, w_o)
        rank_o = jnp.where(lane == j, rank_j, rank_o)
    idx_ref[...] = idx_o
    w_ref[...] = w_o
    rank_ref[...] = rank_o
    run_ref[...] += jnp.sum(onehot, axis=0, keepdims=True)
    cnt_ref[...] = run_ref[...]


def _router(x2, g, mod_l, shift_idx, seq, router_w, router_b):
    n, d = x2.shape
    tm = MOE_TILE
    n_experts = router_w.shape[1]
    per_batch = seq // tm
    rw = jnp.zeros((d, LANES), BF16).at[:, :n_experts].set(router_w.astype(BF16))
    rb = jnp.zeros((1, LANES), F32).at[0, :n_experts].set(router_b)
    lane_out = pl.BlockSpec((tm, LANES), lambda i: (i, 0))
    lane_shape = jax.ShapeDtypeStruct((n, LANES), F32)
    return pl.pallas_call(
        functools.partial(_router_kernel, n_experts=n_experts),
        grid=(n // tm,),
        in_specs=[
            pl.BlockSpec((tm, d), lambda i: (i, 0)),
            pl.BlockSpec((1, d), lambda i: (0, 0)),
            pl.BlockSpec((1, 1, d), lambda i: (i // per_batch, 0, shift_idx)),
            pl.BlockSpec((1, 1, d), lambda i: (i // per_batch, 0, shift_idx + 1)),
            pl.BlockSpec((d, LANES), lambda i: (0, 0)),
            pl.BlockSpec((1, LANES), lambda i: (0, 0)),
        ],
        out_specs=[pl.BlockSpec((tm, d // 2), lambda i: (i, 0)), lane_out, lane_out, lane_out,
                   pl.BlockSpec((1, LANES), lambda i: (0, 0))],
        out_shape=[jax.ShapeDtypeStruct((n, d // 2), jnp.uint32), lane_shape, lane_shape, lane_shape,
                   jax.ShapeDtypeStruct((1, LANES), F32)],
        scratch_shapes=[pltpu.VMEM((1, LANES), F32)],
        compiler_params=_params("arbitrary"),
        name="moe_router",
    )(x2, g.reshape(1, d), mod_l, mod_l, rw, rb)


GLU_TILE = 2 * LANES


def _experts_kernel(te_ref, nu_ref, slot_ref, nxt_ref, rows_ref, xs_ref, b1_ref, b2_ref, perm_ref, w1_hbm, w2_hbm,
                    ys_ref, w1buf, w2buf, w1p_ref, w2b_ref, sem, *, layer):
    t = pl.program_id(0)
    used = t < nu_ref[0]
    new_expert = (t == 0) | (te_ref[t] != te_ref[jnp.maximum(t - 1, 0)])
    n_col_tiles = w1p_ref.shape[1] // GLU_TILE

    def weight_copies(e, slot):
        return (pltpu.make_async_copy(w1_hbm.at[layer, e], w1buf.at[slot], sem.at[0, slot]),
                pltpu.make_async_copy(w2_hbm.at[layer, e], w2buf.at[slot], sem.at[1, slot]))

    @pl.when(used & new_expert)
    def _():
        slot = slot_ref[t]

        @pl.when(t == 0)
        def _():
            for cp in weight_copies(te_ref[t], slot):
                cp.start()

        for cp in weight_copies(te_ref[t], slot):
            cp.wait()

        @pl.when(nxt_ref[t] >= 0)
        def _():
            for cp in weight_copies(nxt_ref[t], 1 - slot):
                cp.start()

        for c in range(n_col_tiles):
            cols = slice(c * GLU_TILE, (c + 1) * GLU_TILE)
            w1p_ref[:, cols] = _dot(w1buf[slot, :, cols].astype(BF16), perm_ref[...]).astype(BF16)
        w2b_ref[...] = w2buf[slot].astype(BF16)

    def swiglu_rows(n_rows):
        lo, hi = _unpack_bf16_pair(xs_ref[0:n_rows, :])
        x = jnp.concatenate([lo, hi], axis=1)
        out = b2_ref[0]
        for j in range(n_col_tiles * LANES // EXPERT_CHUNK):
            hcols = slice(2 * j * EXPERT_CHUNK, 2 * (j + 1) * EXPERT_CHUNK)
            hh = _dot(x, w1p_ref[:, hcols]) + b1_ref[0, :, hcols]
            acts = []
            for c in range(2 * EXPERT_CHUNK // GLU_TILE):
                x_glu = jnp.minimum(hh[:, c * GLU_TILE:c * GLU_TILE + LANES], SWIGLU_LIMIT)
                x_lin = jnp.clip(hh[:, c * GLU_TILE + LANES:(c + 1) * GLU_TILE], -SWIGLU_LIMIT, SWIGLU_LIMIT)
                acts.append((x_glu * _sigmoid(SWIGLU_ALPHA * x_glu) * (x_lin + 1.0)).astype(BF16))
            out = out + _dot(jnp.concatenate(acts, axis=1), w2b_ref[j * EXPERT_CHUNK:(j + 1) * EXPERT_CHUNK, :])
        half = out.shape[1] // 2
        ys_ref[0:n_rows, :] = _pack_bf16_pair(out[:, :half], out[:, half:])
        if n_rows < ys_ref.shape[0]:
            ys_ref[n_rows:, :] = jnp.zeros((ys_ref.shape[0] - n_rows, ys_ref.shape[1]), ys_ref.dtype)

    tile_rows = xs_ref.shape[0]
    pl.when(used & (rows_ref[t] > tile_rows // 2))(lambda: swiglu_rows(tile_rows))
    pl.when(used & (rows_ref[t] <= tile_rows // 2))(lambda: swiglu_rows(tile_rows // 2))

    @pl.when(t >= nu_ref[0])
    def _():
        ys_ref[...] = jnp.zeros_like(ys_ref)


def _experts(xs, tile_expert, n_used, slot, nxt, tile_rows, w1_all, b1, w2_all, b2, layer):
    n_rows, dw = xs.shape
    tm = EXPERT_TILE
    _, n_exp, d, de2 = w1_all.shape
    src = np.arange(GLU_TILE)
    dst = np.where(src % 2 == 0, src // 2, LANES + src // 2)
    perm = jnp.asarray(dst[:, None] == np.arange(GLU_TILE)[None, :], BF16)
    b1p = b1.reshape(n_exp, de2 // GLU_TILE, LANES, 2).transpose(0, 1, 3, 2).reshape(n_exp, 1, de2)
    row = lambda t, te, nu, sl, nx, nr: (jnp.maximum(jnp.minimum(t, nu[0] - 1), 0), 0)
    exp3 = lambda t, te, nu, sl, nx, nr: (te[t], 0, 0)
    grid_spec = pltpu.PrefetchScalarGridSpec(
        num_scalar_prefetch=5,
        grid=(n_rows // tm,),
        in_specs=[
            pl.BlockSpec((tm, dw), row),
            pl.BlockSpec((1, 1, de2), exp3),
            pl.BlockSpec((1, 1, d), exp3),
            pl.BlockSpec((GLU_TILE, GLU_TILE), lambda t, te, nu, sl, nx, nr: (0, 0)),
            pl.BlockSpec(memory_space=pl.ANY),
            pl.BlockSpec(memory_space=pl.ANY),
        ],
        out_specs=pl.BlockSpec((tm, d // 2), lambda t, te, nu, sl, nx, nr: (t, 0)),
        scratch_shapes=[pltpu.VMEM((2, d, de2), F32), pltpu.VMEM((2, de2 // 2, d), F32),
                        pltpu.VMEM((d, de2), BF16), pltpu.VMEM((de2 // 2, d), BF16),
                        pltpu.SemaphoreType.DMA((2, 2))],
    )
    return pl.pallas_call(
        functools.partial(_experts_kernel, layer=layer),
        grid_spec=grid_spec,
        out_shape=jax.ShapeDtypeStruct((n_rows, d // 2), jnp.uint32),
        compiler_params=_params("arbitrary"),
        name="moe_experts",
    )(tile_expert, n_used, slot, nxt, tile_rows, xs, b1p, b2.reshape(n_exp, 1, d), perm, w1_all, w2_all)


SC_WINDOW = 128


def _sc_worker_rows(n_rows):
    sc = plsc.get_sparse_core_info()
    per_worker = n_rows // (sc.num_cores * sc.num_subcores)
    worker = lax.axis_index("subcore") * sc.num_cores + lax.axis_index("core")
    return worker * per_worker, per_worker


def _sc_mesh():
    return plsc.VectorSubcoreMesh(core_axis_name="core", subcore_axis_name="subcore")


def _sc_gather_rows(table, idx):
    n_idx, width = idx.shape[0], table.shape[1]

    @functools.partial(
        pl.kernel, out_type=jax.ShapeDtypeStruct((n_idx, width), table.dtype), mesh=_sc_mesh(),
        scratch_types=[pltpu.VMEM((SC_WINDOW,), jnp.int32), pltpu.VMEM((SC_WINDOW, width), table.dtype)],
        name="sc_gather_rows")
    def gather(table_hbm, idx_hbm, out_hbm, idx_v, rows_v):
        first, count = _sc_worker_rows(n_idx)

        @pl.loop(0, count // SC_WINDOW)
        def _(c):
            rows = pl.ds(first + c * SC_WINDOW, SC_WINDOW)
            pltpu.sync_copy(idx_hbm.at[rows], idx_v)
            pltpu.sync_copy(table_hbm.at[idx_v], rows_v)
            pltpu.sync_copy(rows_v, out_hbm.at[rows])

    return gather(table, idx)


def _sc_scatter_rows(rows, pos, n_out):
    n, width = rows.shape
    k = pos.shape[0]

    @functools.partial(
        pl.kernel, out_type=jax.ShapeDtypeStruct((n_out, width), rows.dtype), mesh=_sc_mesh(),
        scratch_types=[pltpu.VMEM((k, SC_WINDOW), jnp.int32), pltpu.VMEM((SC_WINDOW, width), rows.dtype)],
        name="sc_scatter_rows")
    def scatter(rows_hbm, pos_hbm, out_hbm, pos_v, rows_v):
        first, count = _sc_worker_rows(n)

        @pl.loop(0, count // SC_WINDOW)
        def _(c):
            src = pl.ds(first + c * SC_WINDOW, SC_WINDOW)
            pltpu.sync_copy(pos_hbm.at[:, src], pos_v)
            pltpu.sync_copy(rows_hbm.at[src], rows_v)
            for j in range(k):
                pltpu.sync_copy(rows_v, out_hbm.at[pos_v.at[j]])

    return scatter(rows, pos)


def _combine_dense_kernel(x_ref, gate_ref, w_ref, fg_ref, *rest, final_norm):
    y_refs, o_ref = rest[:TOP_K], rest[TOP_K]
    w = w_ref[...]
    y_lo = y_hi = None
    for j in range(TOP_K):
        lo, hi = _unpack_bf16_pair(y_refs[j][...])
        wj = w[:, j:j + 1]
        y_lo = wj * lo if j == 0 else y_lo + wj * lo
        y_hi = wj * hi if j == 0 else y_hi + wj * hi
    out = x_ref[...] + gate_ref[0] * jnp.concatenate([y_lo, y_hi], axis=1)
    if final_norm:
        out = out * lax.rsqrt(jnp.mean(out * out, axis=-1, keepdims=True) + NORM_EPS) * fg_ref[...]
    o_ref[...] = out


def _combine_dense(x2, mod_l, gate_idx, seq, top_w, y4, final_g, final_norm, tm=512):
    n, d = x2.shape
    per_batch = seq // tm
    blocks = n // tm
    slot_spec = lambda j: pl.BlockSpec((tm, d // 2), lambda i: (j * blocks + i, 0))
    return pl.pallas_call(
        functools.partial(_combine_dense_kernel, final_norm=final_norm),
        grid=(blocks,),
        in_specs=[
            pl.BlockSpec((tm, d), lambda i: (i, 0)),
            pl.BlockSpec((1, 1, d), lambda i: (i // per_batch, 0, gate_idx)),
            pl.BlockSpec((tm, LANES), lambda i: (i, 0)),
            pl.BlockSpec((1, d), lambda i: (0, 0)),
        ] + [slot_spec(j) for j in range(TOP_K)],
        out_specs=pl.BlockSpec((tm, d), lambda i: (i, 0)),
        out_shape=jax.ShapeDtypeStruct((n, d), F32),
        compiler_params=_params("parallel"),
        name="moe_combine",
    )(x2, mod_l, top_w, final_g.reshape(1, d), *([y4] * TOP_K))


def _moe_ffn(x2, norm_g, mod_l, seq, router_w, router_b, w1_all, b1, w2_all, b2, layer, final_g, final_norm):
    n, d = x2.shape
    n_exp = router_w.shape[1]
    tile = EXPERT_TILE
    hp, idx_f, top_w, rank_f, counts = _router(x2, norm_g, mod_l, 3, seq, router_w, router_b)
    counts = counts[0, :n_exp].astype(jnp.int32)
    padded = (counts + tile - 1) // tile * tile
    ends = jnp.cumsum(padded)
    starts = ends - padded
    idx = idx_f[:, :TOP_K].astype(jnp.int32)
    expert_ids = jnp.arange(n_exp, dtype=jnp.int32)
    start_of = jnp.sum(jnp.where(idx[..., None] == expert_ids, starts, 0), axis=-1)
    pos_flat = (start_of + rank_f[:, :TOP_K].astype(jnp.int32)).reshape(-1)
    n_rows = n * TOP_K + n_exp * tile
    n_tiles = n_rows // tile
    tile_start = jnp.arange(n_tiles, dtype=jnp.int32) * tile
    tile_expert = jnp.sum((ends[None, :] <= tile_start[:, None]).astype(jnp.int32), axis=1)
    tile_expert = jnp.minimum(tile_expert, n_exp - 1)
    n_used = (ends[-1:] // tile).astype(jnp.int32)
    nonempty = counts > 0
    ordinal = jnp.cumsum(nonempty.astype(jnp.int32)) - 1
    later = nonempty[None, :] & (expert_ids[None, :] > expert_ids[:, None])
    nxt_e = jnp.min(jnp.where(later, expert_ids[None, :], n_exp), axis=1)
    nxt_e = jnp.where(nxt_e == n_exp, -1, nxt_e)
    slot = (ordinal[tile_expert] % 2).astype(jnp.int32)
    nxt = nxt_e[tile_expert].astype(jnp.int32)
    pos_slot_major = pos_flat.reshape(n, TOP_K).T
    xs = _sc_scatter_rows(hp, pos_slot_major, n_rows)
    filled = jnp.clip((starts + counts)[tile_expert] - tile_start, 0, tile).astype(jnp.int32)
    ys = _experts(xs, tile_expert, n_used, slot, nxt, filled, w1_all, b1, w2_all, b2, layer)
    y4 = _sc_gather_rows(ys, pos_slot_major.reshape(-1))
    return _combine_dense(x2, mod_l, 5, seq, top_w, y4, final_g, final_norm)


def kernel(x, c, ada_w, ada_b, norm_mix_g, norm_ffn_g, router_w, router_b, moe_w1, moe_b1, moe_w2, moe_b2, ab_w_in, ab_w_out, hgrn_lb, hgrn_norm_g, rwkv_mu, rwkv_w0, rwkv_w2, rwkv_a0, rwkv_a2, rwkv_g2, rwkv_k_k, rwkv_k_a, rwkv_r_k, rwkv_ln_g, rwkv_ln_b, attn_w_in, attn_w_out, attn_lambda, attn_subln_g, rel_bias_table, final_norm_g):
    batch, seq, d = x.shape
    n = batch * seq
    depth = ada_w.shape[0]
    x2 = x.reshape(n, d)
    mod = _adaln(c, ada_w, ada_b)
    lb_all = jnp.cumsum(jax.nn.softmax(hgrn_lb.astype(F32), axis=1), axis=1)
    for layer in range(depth):
        mod_l = mod[layer].reshape(batch, 1, 6 * d)
        j = layer // 2
        if layer % 2 == 0:
            a_cols = 5 * (d // 2)
            w_in = ab_w_in[j].astype(BF16)
            p_a, p_b = _normmod_proj(x2, norm_mix_g[layer], mod_l, 0, seq,
                                     [w_in[:, :a_cols], w_in[:, a_cols:]], [F32, F32])
            y_a = _hgrn2(p_a, lb_all[:, j], hgrn_norm_g[j], batch, seq)
            y_b = _rwkv7(p_b, rwkv_mu[j], rwkv_w0[j], rwkv_w2[j], rwkv_a0[j], rwkv_a2[j], rwkv_g2[j],
                         rwkv_k_k[j], rwkv_k_a[j], rwkv_r_k[j], rwkv_ln_g[j], rwkv_ln_b[j], batch, seq)
            w_out = ab_w_out[j].astype(BF16)
            x2 = _outproj_residual(x2, mod_l, 2, seq, [y_a, y_b], [w_out[:d // 2], w_out[d // 2:]])
        else:
            (qkv,) = _normmod_proj(x2, norm_mix_g[layer], mod_l, 0, seq, [attn_w_in[j].astype(BF16)], [BF16])
            lam = attn_lambda[j].astype(F32)
            lam_init = 0.8 - 0.6 * math.exp(-0.3 * layer)
            lam_full = jnp.exp(jnp.sum(lam[0] * lam[1])) - jnp.exp(jnp.sum(lam[2] * lam[3])) + lam_init
            o = _diff_attention(qkv, rel_bias_table, lam_full, attn_subln_g[j], lam_init, batch, seq)
            x2 = _outproj_residual(x2, mod_l, 2, seq, [o], [attn_w_out[j].astype(BF16)])
        x2 = _moe_ffn(x2, norm_ffn_g[layer], mod_l, seq, router_w[layer], router_b[layer], moe_w1,
                      moe_b1[layer], moe_w2, moe_b2[layer], layer, final_norm_g, layer == depth - 1)
    return x2.reshape(batch, seq, d)
```

```python
import functools
import math

import jax
import jax.numpy as jnp
import numpy as np
from jax import lax
from jax.experimental import pallas as pl
from jax.experimental.pallas import tpu as pltpu
from jax.experimental.pallas import tpu_sc as plsc

F32 = jnp.float32
BF16 = jnp.bfloat16
HIGHEST = lax.Precision.HIGHEST

NORM_EPS = 1e-6

A_HEAD_DIM = 128
A_CHUNK = 32
A_GROUP = 512
A_HEADS_PER_STEP = 2
B_HEAD_DIM = 64
B_CHUNK = 64
B_STEPS = 2
B_LN_EPS = 1e-5 * B_HEAD_DIM
C_HEAD_DIM = 64
REL_BUCKETS = 32
REL_MAX_DISTANCE = 128
TOP_K = 4
SWIGLU_LIMIT = 7.0
SWIGLU_ALPHA = 1.702

V7X_VMEM_BYTES = 64 * 1024 * 1024
VMEM_LIMIT = V7X_VMEM_BYTES - 8 * 1024 * 1024
LANES = 128


def _params(*sem):
    return pltpu.CompilerParams(dimension_semantics=sem, vmem_limit_bytes=VMEM_LIMIT)


def _sigmoid(x):
    return 1.0 / (1.0 + jnp.exp(-x))


def _silu(x):
    return x * _sigmoid(x)


def _dot(a, b):
    return jnp.dot(a, b, preferred_element_type=F32)


def _dot_nt(a, b):
    return lax.dot_general(a, b, (((1,), (1,)), ((), ())), preferred_element_type=F32)


def _dot_tn(a, b):
    return lax.dot_general(a, b, (((0,), (0,)), ((), ())), preferred_element_type=F32)


def _adaln_kernel(c_ref, w_ref, b_ref, o_ref):
    cond = _silu(c_ref[...])
    o_ref[0] = jnp.dot(cond, w_ref[0], precision=HIGHEST, preferred_element_type=F32) + b_ref[0]


def _adaln(c, ada_w, ada_b):
    n_layers, d, n_out = ada_w.shape
    batch = c.shape[0]
    tn = 1536
    return pl.pallas_call(
        _adaln_kernel,
        grid=(n_layers, n_out // tn),
        in_specs=[
            pl.BlockSpec((batch, d), lambda l, j: (0, 0)),
            pl.BlockSpec((1, d, tn), lambda l, j: (l, 0, j)),
            pl.BlockSpec((1, 1, tn), lambda l, j: (l, 0, j)),
        ],
        out_specs=pl.BlockSpec((1, batch, tn), lambda l, j: (l, 0, j)),
        out_shape=jax.ShapeDtypeStruct((n_layers, batch, n_out), F32),
        compiler_params=_params("parallel", "parallel"),
        name="adaln",
    )(c, ada_w, ada_b.reshape(n_layers, 1, n_out))


def _modulated_norm(x, g, shift, scale):
    y = x * lax.rsqrt(jnp.mean(x * x, axis=-1, keepdims=True) + NORM_EPS)
    return (y * g) * (1.0 + scale) + shift


def _normmod_proj_kernel(x_ref, g_ref, sh_ref, sc_ref, *rest, n_w):
    w_refs, o_refs = rest[:n_w], rest[n_w:]
    h = _modulated_norm(x_ref[...], g_ref[...], sh_ref[0], sc_ref[0]).astype(BF16)
    for w_ref, o_ref in zip(w_refs, o_refs):
        o_ref[...] = _dot(h, w_ref[...]).astype(o_ref.dtype)


def _normmod_proj(x2, g, mod_l, shift_idx, seq, weights, out_dtypes, tm=512):
    n, d = x2.shape
    per_batch = seq // tm
    in_specs = [
        pl.BlockSpec((tm, d), lambda i: (i, 0)),
        pl.BlockSpec((1, d), lambda i: (0, 0)),
        pl.BlockSpec((1, 1, d), lambda i: (i // per_batch, 0, shift_idx)),
        pl.BlockSpec((1, 1, d), lambda i: (i // per_batch, 0, shift_idx + 1)),
    ]
    in_specs += [pl.BlockSpec(w.shape, lambda i: (0, 0)) for w in weights]
    out_specs = [pl.BlockSpec((tm, w.shape[1]), lambda i: (i, 0)) for w in weights]
    out_shape = [jax.ShapeDtypeStruct((n, w.shape[1]), dt) for w, dt in zip(weights, out_dtypes)]
    return pl.pallas_call(
        functools.partial(_normmod_proj_kernel, n_w=len(weights)),
        grid=(n // tm,),
        in_specs=in_specs,
        out_specs=out_specs,
        out_shape=out_shape,
        compiler_params=_params("parallel"),
        name="normmod_proj",
    )(x2, g.reshape(1, d), mod_l, mod_l, *weights)


def _outproj_kernel(x_ref, gate_ref, *rest, n_y):
    y_refs, w_refs, o_ref = rest[:n_y], rest[n_y:2 * n_y], rest[2 * n_y]
    acc = _dot(y_refs[0][...], w_refs[0][...])
    for y_ref, w_ref in zip(y_refs[1:], w_refs[1:]):
        acc += _dot(y_ref[...], w_ref[...])
    o_ref[...] = x_ref[...] + gate_ref[0] * acc


def _outproj_residual(x2, mod_l, gate_idx, seq, ys, ws, tm=512):
    n, d = x2.shape
    per_batch = seq // tm
    in_specs = [
        pl.BlockSpec((tm, d), lambda i: (i, 0)),
        pl.BlockSpec((1, 1, d), lambda i: (i // per_batch, 0, gate_idx)),
    ]
    in_specs += [pl.BlockSpec((tm, y.shape[1]), lambda i: (i, 0)) for y in ys]
    in_specs += [pl.BlockSpec(w.shape, lambda i: (0, 0)) for w in ws]
    return pl.pallas_call(
        functools.partial(_outproj_kernel, n_y=len(ys)),
        grid=(n // tm,),
        in_specs=in_specs,
        out_specs=pl.BlockSpec((tm, d), lambda i: (i, 0)),
        out_shape=jax.ShapeDtypeStruct((n, d), F32),
        compiler_params=_params("parallel"),
        name="outproj_residual",
    )(x2, mod_l, *ys, *ws)


def _chunk_cumsum(x, chunk, reverse):
    rows = x.shape[0]
    pos = lax.broadcasted_iota(jnp.int32, x.shape, 0) % chunk
    s = 1
    while s < chunk:
        if reverse:
            x = x + jnp.where(pos < chunk - s, pltpu.roll(x, rows - s, axis=0), 0.0)
        else:
            x = x + jnp.where(pos >= s, pltpu.roll(x, s, axis=0), 0.0)
        s *= 2
    return x


def _hgrn_groups(slabs):
    g_rows, dk = slabs[0][0].shape
    n_chunks = g_rows // A_CHUNK
    ti = lax.broadcasted_iota(jnp.int32, (A_CHUNK, A_CHUNK), 0)
    si = lax.broadcasted_iota(jnp.int32, (A_CHUNK, A_CHUNK), 1)
    chunk_rows = [slice(c * A_CHUNK, (c + 1) * A_CHUNK) for c in range(n_chunks)]
    prep = []
    for q, f, v, lb, st, reverse in slabs:
        fg = lb + (1.0 - lb) * _sigmoid(f)
        k = 1.0 - fg
        b = _chunk_cumsum(jnp.log(fg), A_CHUNK, reverse)
        b3 = b.reshape(n_chunks, A_CHUNK, dk)
        edge = b3[:, 0:1, :] if reverse else b3[:, A_CHUNK - 1:A_CHUNK, :]
        prep.append(dict(
            q_in=(q * jnp.exp(b)).astype(BF16), k_in=(k * jnp.exp(-b)).astype(BF16),
            k_st=(k.reshape(n_chunks, A_CHUNK, dk) * jnp.exp(edge - b3)).astype(BF16),
            decay=jnp.exp(edge), vb=v.astype(BF16), st=st, reverse=reverse,
            mask=(si >= ti) if reverse else (si <= ti)))
    for p in prep:
        p["scores"] = [_dot_nt(p["q_in"][r], p["k_in"][r]) for r in chunk_rows]
        p["dstate"] = [_dot_tn(p["vb"][r], p["k_st"][c]) for c, r in enumerate(chunk_rows)]
    for p in prep:
        p["intra"] = [_dot(jnp.where(p["mask"], s, 0.0).astype(BF16), p["vb"][r])
                      for s, r in zip(p["scores"], chunk_rows)]
    results = []
    for p in prep:
        st = p["st"]
        outs = [None] * n_chunks
        for c in (range(n_chunks - 1, -1, -1) if p["reverse"] else range(n_chunks)):
            outs[c] = p["intra"][c] + _dot_nt(p["q_in"][chunk_rows[c]], st.astype(BF16))
            st = st * p["decay"][c] + p["dstate"][c]
        results.append((jnp.concatenate(outs, axis=0), st))
    return results


def _hgrn_kernel(q_ref, ff_ref, fb_ref, i_ref, g_ref, lb_ref, ng_ref, o_ref, of_ref, ob_ref):
    seq = q_ref.shape[0]
    dk = A_HEAD_DIM
    n_heads = q_ref.shape[1] // dk
    n_groups = seq // A_GROUP
    head_cols = [slice(h * dk, (h + 1) * dk) for h in range(n_heads)]

    def body(j, states):
        rf = pl.ds(pl.multiple_of(j * A_GROUP, A_GROUP), A_GROUP)
        rb = pl.ds(pl.multiple_of((n_groups - 1 - j) * A_GROUP, A_GROUP), A_GROUP)
        slabs = []
        for h, cols in enumerate(head_cols):
            slabs.append((_silu(q_ref[rf, cols]), ff_ref[rf, cols], i_ref[rf, cols], lb_ref[0:1, cols],
                          states[2 * h], False))
            slabs.append((_silu(q_ref[rb, cols]), fb_ref[rb, cols], i_ref[rb, cols], lb_ref[1:2, cols],
                          states[2 * h + 1], True))
        results = _hgrn_groups(slabs)
        for h, cols in enumerate(head_cols):
            of_ref[rf, cols] = results[2 * h][0]
            ob_ref[rb, cols] = results[2 * h + 1][0]
        return tuple(r[1] for r in results)

    zero = jnp.zeros((dk, dk), F32)
    lax.fori_loop(0, n_groups, body, (zero,) * (2 * n_heads))

    def finish(j, carry):
        r = pl.ds(pl.multiple_of(j * A_GROUP, A_GROUP), A_GROUP)
        for cols in head_cols:
            o = of_ref[r, cols] + ob_ref[r, cols]
            o = o * lax.rsqrt(jnp.mean(o * o, axis=-1, keepdims=True) + NORM_EPS) * ng_ref[...]
            o_ref[r, cols] = (o * _silu(g_ref[r, cols])).astype(o_ref.dtype)
        return carry

    lax.fori_loop(0, n_groups, finish, 0)


def _hgrn2(p_a, lb, norm_g, batch, seq):
    n = p_a.shape[0]
    width = p_a.shape[1] // 5
    cols = A_HEADS_PER_STEP * A_HEAD_DIM
    steps = width // cols
    sect = lambda s: pl.BlockSpec((seq, cols), lambda b, h, s=s: (b, s * steps + h))
    return pl.pallas_call(
        _hgrn_kernel,
        grid=(batch, steps),
        in_specs=[sect(0), sect(1), sect(2), sect(3), sect(4),
                  pl.BlockSpec((2, cols), lambda b, h: (0, h)),
                  pl.BlockSpec((1, A_HEAD_DIM), lambda b, h: (0, 0))],
        out_specs=pl.BlockSpec((seq, cols), lambda b, h: (b, h)),
        out_shape=jax.ShapeDtypeStruct((n, width), BF16),
        scratch_shapes=[pltpu.VMEM((seq, cols), F32), pltpu.VMEM((seq, cols), F32)],
        compiler_params=_params("parallel", "parallel"),
        name="hgrn2",
    )(p_a, p_a, p_a, p_a, p_a, lb, norm_g.reshape(1, A_HEAD_DIM))


def _softplus(z):
    return jnp.maximum(z, 0.0) + jnp.log(1.0 + jnp.exp(-jnp.abs(z)))


def _rwkv_prep_kernel(p_ref, prev_ref, next_ref, mu_ref, w0_ref, w2_ref, a0_ref, a2_ref, g2_ref,
                      kk_ref, ka_ref, rk_ref, ones_ref, tri_ref,
                      r_out, k_out, v_out, kkn_out, b_out, cumf_out, cumb_out, g_out, bonus_out):
    i = pl.program_id(1)
    last = pl.num_programs(1) - 1
    ts = p_ref.shape[0]
    width = a0_ref.shape[1]
    heads = width // B_HEAD_DIM
    p = p_ref[...]
    prev_row = jnp.where(i == 0, 0.0, prev_ref[7:8, :])
    next_row = jnp.where(i == last, 0.0, next_ref[0:1, :])
    row = lax.broadcasted_iota(jnp.int32, (ts, 1), 0)
    up = jnp.where(row == 0, prev_row, pltpu.roll(p, 1, axis=0))
    dn = jnp.where(row == ts - 1, next_row, pltpu.roll(p, ts - 1, axis=0))
    p = p + mu_ref[...] * (0.5 * (up + dn) - p)

    r = p[:, 0:width]
    k = p[:, width:2 * width]
    v = p[:, 2 * width:3 * width]
    o = 3 * width
    wlo_f = p[:, o:o + 64]
    wlo_b = p[:, o + 64:o + 128]
    alo = p[:, o + 128:o + 192]
    glo = p[:, o + 192:o + 320]

    def log_decay(wlo, d):
        z = w0_ref[d:d + 1, :] + _dot(jnp.tanh(wlo).astype(BF16), w2_ref[d])
        w = -_softplus(-z) - 0.5
        return -jnp.exp(w)

    def split_dot(sel, t, right):
        hi = t.astype(BF16)
        lo = (t - hi.astype(F32)).astype(BF16)
        return (_dot(hi, sel) + _dot(lo, sel)) if right else (_dot(sel, hi) + _dot(sel, lo))

    cum_f = split_dot(tri_ref[0], log_decay(wlo_f, 0), False)
    cum_b = split_dot(tri_ref[1], log_decay(wlo_b, 1), False)
    a = _sigmoid(a0_ref[...] + _dot(alo.astype(BF16), a2_ref[...]))
    g = _dot(_sigmoid(glo).astype(BF16), g2_ref[...])
    kk = k * kk_ref[...]
    head_sum = lambda t: split_dot(ones_ref[...], t, True)

    kk_n = kk / jnp.maximum(jnp.sqrt(head_sum(kk * kk)), 1e-12)
    k_mod = k * (1.0 + (a - 1.0) * ka_ref[...])
    bonus = head_sum(r * k_mod * rk_ref[...]) * v
    g_out[...] = g
    bonus_out[...] = bonus
    for h in range(heads):
        sl = slice(h * B_HEAD_DIM, (h + 1) * B_HEAD_DIM)
        r_out[0, h] = r[:, sl]
        k_out[0, h] = k_mod[:, sl]
        v_out[0, h] = v[:, sl]
        kkn_out[0, h] = kk_n[:, sl]
        b_out[0, h] = (kk_n * a)[:, sl]
        cumf_out[0, h] = cum_f[:, sl]
        cumb_out[0, h] = cum_b[:, sl]


def _rwkv_prep(p_b, mu, w0, w2, a0, a2, g2, k_k, k_a, r_k, batch, seq, ts=256):
    n, cols = p_b.shape
    width = a0.shape[0]
    heads = width // B_HEAD_DIM
    nblk = seq // ts
    rows8 = ts // 8
    head_id = np.arange(width) // B_HEAD_DIM
    ones = jnp.asarray(head_id[:, None] == head_id[None, :], BF16)
    full = lambda a: pl.BlockSpec(a.shape, lambda b, i: (0,) * a.ndim)
    vec = lambda a: a.reshape(1, -1)
    row = np.arange(ts)
    same_chunk = (row[:, None] // B_CHUNK) == (row[None, :] // B_CHUNK)
    tri = jnp.asarray(np.stack([same_chunk & (row[None, :] <= row[:, None]),
                                same_chunk & (row[None, :] >= row[:, None])]), BF16)
    args = [vec(mu), w0, w2.astype(BF16), vec(a0), a2.astype(BF16), g2.astype(BF16),
            vec(k_k), vec(k_a), vec(r_k), ones, tri]
    hm = pl.BlockSpec((1, heads, ts, B_HEAD_DIM), lambda b, i: (b, 0, i, 0))
    tokm = pl.BlockSpec((ts, width), lambda b, i: (b * nblk + i, 0))
    hm_shape = jax.ShapeDtypeStruct((batch, heads, seq, B_HEAD_DIM), F32)
    tok_shape = jax.ShapeDtypeStruct((n, width), F32)
    return pl.pallas_call(
        _rwkv_prep_kernel,
        grid=(batch, nblk),
        in_specs=[
            pl.BlockSpec((ts, cols), lambda b, i: (b * nblk + i, 0)),
            pl.BlockSpec((8, cols), lambda b, i: (jnp.maximum((b * nblk + i) * rows8 - 1, 0), 0)),
            pl.BlockSpec((8, cols), lambda b, i: (jnp.minimum((b * nblk + i + 1) * rows8, n // 8 - 1), 0)),
        ] + [full(a) for a in args],
        out_specs=[hm] * 7 + [tokm] * 2,
        out_shape=[hm_shape] * 7 + [tok_shape] * 2,
        compiler_params=_params("parallel", "parallel"),
        name="rwkv_prep",
    )(p_b, p_b, p_b, *args)


def _rwkv_chunks(chains):
    c, d = chains[0][0].shape
    ti = lax.broadcasted_iota(jnp.int32, (c, c), 0)
    si = lax.broadcasted_iota(jnp.int32, (c, c), 1)
    eye = jnp.where(ti == si, 1.0, 0.0)
    masks = {rev: (((si > ti), (si >= ti)) if rev else ((si < ti), (si <= ti))) for rev in (False, True)}
    t2 = lax.broadcasted_iota(jnp.int32, (c, 2 * c), 0)
    s2 = lax.broadcasted_iota(jnp.int32, (c, 2 * c), 1)
    s2 = jnp.where(s2 >= c, s2 - c, s2)
    incl2 = {False: s2 <= t2, True: s2 >= t2}

    row = lax.broadcasted_iota(jnp.int32, (c, d), 0)
    st = []
    for r, k, v, kk, b, cum, rev in chains:
        before = (jnp.where(row == c - 1, 0.0, pltpu.roll(cum, c - 1, axis=0)) if rev
                  else jnp.where(row == 0, 0.0, pltpu.roll(cum, 1, axis=0)))
        grow = jnp.exp(-cum)
        kt = (kk * jnp.exp(before)).astype(BF16)
        rt = r * jnp.exp(cum)
        b_ = (b * grow).astype(BF16)
        kb = jnp.concatenate([(k * grow).astype(BF16), b_], axis=0)
        st.append(dict(vb=v.astype(BF16), kt=kt, rt=rt, b_=b_, kb=kb,
                       decay=jnp.exp(cum[0:1, :] if rev else cum[c - 1:c, :]), rev=rev,
                       lhs=jnp.concatenate([kt, rt.astype(BF16)], axis=0)))
    for s in st:
        s["big"] = _dot_nt(s["lhs"], s["kb"])
    for s in st:
        strict, incl = masks[s["rev"]]
        big = s["big"]
        a_ab = jnp.where(strict, big[:c, c:], 0.0)
        s["a_ak"] = jnp.where(strict, big[:c, :c], 0.0).astype(BF16)
        s["p"] = jnp.where(incl2[s["rev"]], big[c:], 0.0).astype(BF16)
        s["m"] = eye - a_ab
        s["ab"] = a_ab.astype(BF16)
    for s in st:
        s["aj"] = _dot(s["ab"], s["ab"])
        s["av"] = _dot(s["a_ak"], s["vb"])
    span = 2
    while span < c:
        span *= 2
        for s in st:
            ajb = s["aj"].astype(BF16)
            if span < c:
                both = _dot(jnp.concatenate([ajb, s["m"].astype(BF16)], axis=0), ajb)
                s["aj"], s["m"] = both[:c], s["m"] + both[c:]
            else:
                s["m"] = s["m"] + _dot(s["m"].astype(BF16), ajb)
    for s in st:
        wu = _dot(s["m"].astype(BF16), jnp.concatenate([s["kt"], s["av"].astype(BF16)], axis=1))
        s["nwu"] = -wu.astype(BF16)
    for s in st:
        nw, nu0, vb = s["nwu"][:, :d], s["nwu"][:, d:], s["vb"]
        rhs = jnp.concatenate([jnp.concatenate([vb, jnp.zeros_like(vb)], axis=1),
                               jnp.concatenate([nu0, nw], axis=1)], axis=0)
        s["yq"] = _dot(s["p"], rhs)
        s["h_t"] = _dot_tn(jnp.concatenate([vb, nu0], axis=0), s["kb"])
        s["nbw"] = _dot_tn(s["b_"], nw)
    return [dict(q=(s["rt"] + s["yq"][:, d:]).astype(BF16), y1=s["yq"][:, :d], nbw=s["nbw"].astype(BF16),
                 h_t=s["h_t"], decay=s["decay"]) for s in st]


def _rwkv_advance(states, steps):
    ttbs = [tt.astype(BF16) for tt in states]
    ys = [_dot_nt(s["q"], ttb) + s["y1"] for s, ttb in zip(steps, ttbs)]
    new = [(tt + _dot_nt(ttb, s["nbw"]) + s["h_t"]) * s["decay"] for tt, ttb, s in zip(states, ttbs, steps)]
    return ys, new


def _rwkv_scan_kernel(rf, kf, vf, kkf, bf, lwf, rb, kb, vb, kkb, bb, lwb, yf_ref, yb_ref, tf_ref, tb_ref):
    heads, ts = rf.shape[1], rf.shape[2]
    n_chunks = ts // B_CHUNK

    @pl.when(pl.program_id(1) == 0)
    def _():
        tf_ref[...] = jnp.zeros_like(tf_ref)
        tb_ref[...] = jnp.zeros_like(tb_ref)

    def body(j, carry):
        rows_f = [pl.ds(pl.multiple_of((j * B_STEPS + s) * B_CHUNK, B_CHUNK), B_CHUNK) for s in range(B_STEPS)]
        rows_b = [pl.ds(pl.multiple_of((n_chunks - 1 - j * B_STEPS - s) * B_CHUNK, B_CHUNK), B_CHUNK)
                  for s in range(B_STEPS)]
        chains = []
        for s in range(B_STEPS):
            for h in range(heads):
                chains.append([ref[0, h, rows_f[s], :] for ref in (rf, kf, vf, kkf, bf, lwf)] + [False])
                chains.append([ref[0, h, rows_b[s], :] for ref in (rb, kb, vb, kkb, bb, lwb)] + [True])
        solved = _rwkv_chunks(chains)
        states = []
        for h in range(heads):
            states += [tf_ref[h], tb_ref[h]]
        for s in range(B_STEPS):
            ys, states = _rwkv_advance(states, solved[s * 2 * heads:(s + 1) * 2 * heads])
            for h in range(heads):
                yf_ref[0, h, rows_f[s], :] = ys[2 * h]
                yb_ref[0, h, rows_b[s], :] = ys[2 * h + 1]
        for h in range(heads):
            tf_ref[h], tb_ref[h] = states[2 * h], states[2 * h + 1]
        return carry

    lax.fori_loop(0, n_chunks // B_STEPS, body, 0)


def _rwkv_scan(r, k, v, kkn, b, lw_f, lw_b, ts=256):
    batch, heads, seq, d = r.shape
    nblk = seq // ts
    fwd = pl.BlockSpec((1, heads, ts, d), lambda bi, i: (bi, 0, i, 0))
    bwd = pl.BlockSpec((1, heads, ts, d), lambda bi, i: (bi, 0, nblk - 1 - i, 0))
    shape = jax.ShapeDtypeStruct(r.shape, F32)
    return pl.pallas_call(
        _rwkv_scan_kernel,
        grid=(batch, nblk),
        in_specs=[fwd] * 6 + [bwd] * 6,
        out_specs=[fwd, bwd],
        out_shape=[shape, shape],
        scratch_shapes=[pltpu.VMEM((heads, d, d), F32), pltpu.VMEM((heads, d, d), F32)],
        compiler_params=_params("parallel", "arbitrary"),
        name="rwkv_scan",
    )(r, k, v, kkn, b, lw_f, r, k, v, kkn, b, lw_b)


def _rwkv_post_kernel(yf_ref, yb_ref, g_ref, bonus_ref, lng_ref, lnb_ref, o_ref):
    heads = yf_ref.shape[1]
    outs = []
    for h in range(heads):
        y = yf_ref[0, h] + yb_ref[0, h]
        mean = jnp.mean(y, axis=-1, keepdims=True)
        var = jnp.mean(jnp.square(y - mean), axis=-1, keepdims=True)
        outs.append((y - mean) * lax.rsqrt(var + B_LN_EPS))
    yn = jnp.concatenate(outs, axis=1) * lng_ref[...] + lnb_ref[...]
    o_ref[...] = ((yn + bonus_ref[...]) * g_ref[...]).astype(o_ref.dtype)


def _rwkv_post(y_f, y_b, g, bonus, ln_g, ln_b, ts=512):
    batch, heads, seq, d = y_f.shape
    n, width = g.shape
    nblk = seq // ts
    hm = pl.BlockSpec((1, heads, ts, d), lambda b, i: (b, 0, i, 0))
    tokm = pl.BlockSpec((ts, width), lambda b, i: (b * nblk + i, 0))
    vec = pl.BlockSpec((1, width), lambda b, i: (0, 0))
    return pl.pallas_call(
        _rwkv_post_kernel,
        grid=(batch, nblk),
        in_specs=[hm, hm, tokm, tokm, vec, vec],
        out_specs=tokm,
        out_shape=jax.ShapeDtypeStruct((n, width), BF16),
        compiler_params=_params("parallel", "parallel"),
        name="rwkv_post",
    )(y_f, y_b, g, bonus, ln_g.reshape(1, width), ln_b.reshape(1, width))


def _rwkv7(p_b, mu, w0, w2, a0, a2, g2, k_k, k_a, r_k, ln_g, ln_b, batch, seq):
    r, k, v, kkn, b, lw_f, lw_b, g, bonus = _rwkv_prep(p_b, mu, w0, w2, a0, a2, g2, k_k, k_a, r_k, batch, seq)
    y_f, y_b = _rwkv_scan(r, k, v, kkn, b, lw_f, lw_b)
    return _rwkv_post(y_f, y_b, g, bonus, ln_g, ln_b)


_BUCKET_EDGES = (0, 1, 2, 3, 4, 5, 6, 7, 8, 12, 16, 23, 32, 46, 64, 91)
ATTN_TK = 128
C_HEADS_PER_STEP = 4


def _bias_band_kernel(table_ref, o_ref):
    h = pl.program_id(0)
    n_d, tq, tk = o_ref.shape[1:]
    half = REL_BUCKETS // 2
    r = lax.broadcasted_iota(jnp.int32, (tq, tk), 0)
    c = lax.broadcasted_iota(jnp.int32, (tq, tk), 1)
    for d in range(n_d):
        rel = tk * (d - 1) + c - r
        n = jnp.abs(rel)
        vals = []
        for sign in range(2):
            val = jnp.full((tq, tk), table_ref[sign * half + half - 1, h], F32)
            for bkt in range(half - 2, -1, -1):
                val = jnp.where(n < _BUCKET_EDGES[bkt + 1], table_ref[sign * half + bkt, h], val)
            vals.append(val)
        o_ref[0, d] = jnp.where(rel > 0, vals[1], vals[0])


def _bias_band(rel_table, tq):
    heads = rel_table.shape[1]
    n_d = tq // ATTN_TK + 2
    return pl.pallas_call(
        _bias_band_kernel,
        grid=(heads,),
        in_specs=[pl.BlockSpec(memory_space=pltpu.SMEM)],
        out_specs=pl.BlockSpec((1, n_d, tq, ATTN_TK), lambda h: (h, 0, 0, 0)),
        out_shape=jax.ShapeDtypeStruct((heads, n_d, tq, ATTN_TK), F32),
        compiler_params=_params("parallel"),
        name="bias_band",
    )(rel_table)


def _diff_attn_kernel(table_ref, lam_ref, q_ref, k_ref, v_ref, band_ref, g_ref, o_ref, lg_ref, vext_ref, kt_ref,
                      *, lam_init):
    i = pl.program_id(2)
    tq = q_ref.shape[0]
    dh2 = 2 * C_HEAD_DIM
    n_heads = q_ref.shape[1] // dh2
    seq = k_ref.shape[0]
    n_d = band_ref.shape[1]
    n_kt = seq // ATTN_TK
    ratio = tq // ATTN_TK
    half = REL_BUCKETS // 2
    head_cols = [slice(hh * dh2, (hh + 1) * dh2) for hh in range(n_heads)]

    @pl.when(i == 0)
    def _():
        for hh, cols in enumerate(head_cols):
            kt_ref[hh] = k_ref[:, cols].astype(F32).T.astype(BF16)
            v = v_ref[:, cols]
            vext_ref[hh] = jnp.concatenate([v, jnp.ones_like(v)], axis=1)

    lane = lax.broadcasted_iota(jnp.int32, (1, dh2), 1)
    first_band = ratio * i - 1
    col = lax.broadcasted_iota(jnp.int32, (1, seq), 1)
    for hh, cols in enumerate(head_cols):
        h = pl.program_id(1) * n_heads + hh
        q = q_ref[:, cols] * jnp.asarray(C_HEAD_DIM ** -0.5, q_ref.dtype)
        zero = jnp.zeros_like(q)
        far_row = jnp.where(col < first_band * ATTN_TK, table_ref[half - 1, h],
                            jnp.where(col >= (first_band + n_d) * ATTN_TK, table_ref[REL_BUCKETS - 1, h],
                                      -jnp.inf))
        row_max = []
        for m in range(2):
            qm = jnp.where((lane >= m * C_HEAD_DIM) & (lane < (m + 1) * C_HEAD_DIM), q, zero)
            far = _dot(qm, kt_ref[hh]) + far_row
            mx = jnp.max(far, axis=-1, keepdims=True)
            lg_ref[hh, m] = far
            for d in (0, n_d - 1) + tuple(range(1, n_d - 1)):
                kt = first_band + d
                valid = (kt >= 0) & (kt < n_kt)
                c0 = pl.multiple_of(jnp.clip(kt, 0, n_kt - 1) * ATTN_TK, ATTN_TK)
                near = _dot(qm, kt_ref[hh, :, pl.ds(c0, ATTN_TK)]) + band_ref[hh, d]
                mx = jnp.maximum(mx, jnp.where(valid, jnp.max(near, axis=-1, keepdims=True), -jnp.inf))
                lg_ref[hh, m, :, pl.ds(c0, ATTN_TK)] = near
            row_max.append(mx)
        parts = []
        for m in range(2):
            e = jnp.exp(lg_ref[hh, m] - row_max[m]).astype(BF16)
            pv = _dot(e, vext_ref[hh])
            parts.append(pv[:, :dh2] * (1.0 / pv[:, dh2:dh2 + 1]))
        o = parts[0] - lam_ref[0] * parts[1]
        o = o * lax.rsqrt(jnp.mean(o * o, axis=-1, keepdims=True) + NORM_EPS) * g_ref[...]
        o_ref[:, cols] = (o * (1.0 - lam_init)).astype(o_ref.dtype)


def _diff_attention(qkv, rel_table, lam_full, subln_g, lam_init, batch, seq, tq=256):
    n = qkv.shape[0]
    dh2 = 2 * C_HEAD_DIM
    heads = qkv.shape[1] // (3 * dh2)
    nq = seq // tq
    per = C_HEADS_PER_STEP
    groups = heads // per
    cols = per * dh2
    band = _bias_band(rel_table, tq)
    return pl.pallas_call(
        functools.partial(_diff_attn_kernel, lam_init=lam_init),
        grid=(batch, groups, nq),
        in_specs=[
            pl.BlockSpec(memory_space=pltpu.SMEM),
            pl.BlockSpec(memory_space=pltpu.SMEM),
            pl.BlockSpec((tq, cols), lambda b, h, i: (b * nq + i, h)),
            pl.BlockSpec((seq, cols), lambda b, h, i: (b, groups + h)),
            pl.BlockSpec((seq, cols), lambda b, h, i: (b, 2 * groups + h)),
            pl.BlockSpec((per,) + band.shape[1:], lambda b, h, i: (h, 0, 0, 0)),
            pl.BlockSpec((1, dh2), lambda b, h, i: (0, 0)),
        ],
        out_specs=pl.BlockSpec((tq, cols), lambda b, h, i: (b * nq + i, h)),
        out_shape=jax.ShapeDtypeStruct((n, heads * dh2), BF16),
        scratch_shapes=[pltpu.VMEM((per, 2, tq, seq), F32), pltpu.VMEM((per, seq, 2 * dh2), BF16),
                        pltpu.VMEM((per, dh2, seq), BF16)],
        compiler_params=_params("parallel", "parallel", "arbitrary"),
        name="diff_attention",
    )(rel_table, lam_full.reshape(1), qkv, qkv, qkv, band, subln_g.reshape(1, dh2))


MOE_TILE = 256
EXPERT_TILE = 512
EXPERT_CHUNK = 512
_HI_MASK = np.uint32(0xFFFF0000)


def _pack_bf16_pair(lo, hi):
    as_bits = lambda t: lax.bitcast_convert_type(t.astype(BF16).astype(F32), jnp.uint32)
    return (as_bits(hi) & _HI_MASK) | (as_bits(lo) >> 16)


def _unpack_bf16_pair(word):
    lo = lax.bitcast_convert_type(word << 16, F32).astype(BF16)
    hi = lax.bitcast_convert_type(word & _HI_MASK, F32).astype(BF16)
    return lo, hi


def _router_kernel(x_ref, g_ref, sh_ref, sc_ref, rw_ref, rb_ref, hp_ref, idx_ref, w_ref, rank_ref, cnt_ref,
                   run_ref, *, n_experts):
    @pl.when(pl.program_id(0) == 0)
    def _():
        run_ref[...] = jnp.zeros_like(run_ref)

    tm, d = x_ref.shape
    h = _modulated_norm(x_ref[...], g_ref[...], sh_ref[0], sc_ref[0])
    hp_ref[...] = _pack_bf16_pair(h[:, :d // 2], h[:, d // 2:])
    logits = _dot_nt(rw_ref[...], h.astype(BF16)) + rb_ref[...]
    expert = lax.broadcasted_iota(jnp.int32, logits.shape, 0).astype(F32)
    neg = jnp.float32(-jnp.inf)
    picks, vals = [], []
    for _ in range(TOP_K):
        m = jnp.max(logits, axis=0, keepdims=True)
        first = jnp.min(jnp.where(logits == m, expert, float(n_experts)), axis=0, keepdims=True)
        hit = expert == first
        picks.append((first, hit))
        vals.append(m)
        logits = jnp.where(hit, neg, logits)
    es = [jnp.exp(v - vals[0]) for v in vals]
    inv = 1.0 / (es[0] + es[1] + es[2] + es[3])
    onehot = sum(jnp.where(hit, 1.0, 0.0) for _, hit in picks)
    ri = lax.broadcasted_iota(jnp.int32, (tm, tm), 0)
    ci = lax.broadcasted_iota(jnp.int32, (tm, tm), 1)
    earlier = jnp.where(ri < ci, 1.0, 0.0).astype(BF16)
    before = _dot(onehot.astype(BF16), earlier) + run_ref[...]
    row = lax.broadcasted_iota(jnp.int32, idx_ref.shape, 0)
    idx_o = jnp.zeros(idx_ref.shape, F32)
    w_o = jnp.zeros(idx_ref.shape, F32)
    rank_o = jnp.zeros(idx_ref.shape, F32)
    for j, (first, hit) in enumerate(picks):
        rank_j = jnp.sum(jnp.where(hit, before, 0.0), axis=0, keepdims=True)
        idx_o = jnp.where(row == j, first, idx_o)
        w_o = jnp.where(row == j, es[j] * inv, w_o)
        rank_o = jnp.where(row == j, rank_j, rank_o)
    idx_ref[...] = idx_o
    w_ref[...] = w_o
    rank_ref[...] = rank_o
    run_ref[...] += jnp.sum(onehot, axis=1, keepdims=True)
    cnt_ref[...] = jnp.broadcast_to(run_ref[...], cnt_ref.shape)


def _router(x2, g, mod_l, shift_idx, seq, router_w, router_b):
    n, d = x2.shape
    tm = MOE_TILE
    n_experts = router_w.shape[1]
    per_batch = seq // tm
    slots = 8
    slot_out = pl.BlockSpec((slots, tm), lambda i: (0, i))
    slot_shape = jax.ShapeDtypeStruct((slots, n), F32)
    return pl.pallas_call(
        functools.partial(_router_kernel, n_experts=n_experts),
        grid=(n // tm,),
        in_specs=[
            pl.BlockSpec((tm, d), lambda i: (i, 0)),
            pl.BlockSpec((1, d), lambda i: (0, 0)),
            pl.BlockSpec((1, 1, d), lambda i: (i // per_batch, 0, shift_idx)),
            pl.BlockSpec((1, 1, d), lambda i: (i // per_batch, 0, shift_idx + 1)),
            pl.BlockSpec((n_experts, d), lambda i: (0, 0)),
            pl.BlockSpec((n_experts, 1), lambda i: (0, 0)),
        ],
        out_specs=[pl.BlockSpec((tm, d // 2), lambda i: (i, 0)), slot_out, slot_out, slot_out,
                   pl.BlockSpec((n_experts, LANES), lambda i: (0, 0))],
        out_shape=[jax.ShapeDtypeStruct((n, d // 2), jnp.uint32), slot_shape, slot_shape, slot_shape,
                   jax.ShapeDtypeStruct((n_experts, LANES), F32)],
        scratch_shapes=[pltpu.VMEM((n_experts, 1), F32)],
        compiler_params=_params("arbitrary"),
        name="moe_router",
    )(x2, g.reshape(1, d), mod_l, mod_l, router_w.T.astype(BF16), router_b.reshape(n_experts, 1))


GLU_TILE = 2 * LANES


def _experts_kernel(te_ref, nu_ref, slot_ref, nxt_ref, rows_ref, xs_ref, b1_ref, b2_ref, perm_ref, w1_hbm, w2_hbm,
                    ys_ref, w1buf, w2buf, w1p_ref, w2b_ref, sem, *, layer):
    t = pl.program_id(0)
    used = t < nu_ref[0]
    new_expert = (t == 0) | (te_ref[t] != te_ref[jnp.maximum(t - 1, 0)])
    n_col_tiles = w1p_ref.shape[1] // GLU_TILE

    def weight_copies(e, slot):
        return (pltpu.make_async_copy(w1_hbm.at[layer, e], w1buf.at[slot], sem.at[0, slot]),
                pltpu.make_async_copy(w2_hbm.at[layer, e], w2buf.at[slot], sem.at[1, slot]))

    @pl.when(used & new_expert)
    def _():
        slot = slot_ref[t]

        @pl.when(t == 0)
        def _():
            for cp in weight_copies(te_ref[t], slot):
                cp.start()

        for cp in weight_copies(te_ref[t], slot):
            cp.wait()

        @pl.when(nxt_ref[t] >= 0)
        def _():
            for cp in weight_copies(nxt_ref[t], 1 - slot):
                cp.start()

        for c in range(n_col_tiles):
            cols = slice(c * GLU_TILE, (c + 1) * GLU_TILE)
            w1p_ref[:, cols] = _dot(w1buf[slot, :, cols].astype(BF16), perm_ref[...]).astype(BF16)
        w2b_ref[...] = w2buf[slot].astype(BF16)

    def swiglu_rows(n_rows):
        lo, hi = _unpack_bf16_pair(xs_ref[0:n_rows, :])
        x = jnp.concatenate([lo, hi], axis=1)
        out = b2_ref[0]
        for j in range(n_col_tiles * LANES // EXPERT_CHUNK):
            hcols = slice(2 * j * EXPERT_CHUNK, 2 * (j + 1) * EXPERT_CHUNK)
            hh = _dot(x, w1p_ref[:, hcols]) + b1_ref[0, :, hcols]
            acts = []
            for c in range(2 * EXPERT_CHUNK // GLU_TILE):
                x_glu = jnp.minimum(hh[:, c * GLU_TILE:c * GLU_TILE + LANES], SWIGLU_LIMIT)
                x_lin = jnp.clip(hh[:, c * GLU_TILE + LANES:(c + 1) * GLU_TILE], -SWIGLU_LIMIT, SWIGLU_LIMIT)
                acts.append((x_glu * _sigmoid(SWIGLU_ALPHA * x_glu) * (x_lin + 1.0)).astype(BF16))
            out = out + _dot(jnp.concatenate(acts, axis=1), w2b_ref[j * EXPERT_CHUNK:(j + 1) * EXPERT_CHUNK, :])
        half = out.shape[1] // 2
        ys_ref[0:n_rows, :] = _pack_bf16_pair(out[:, :half], out[:, half:])
        if n_rows < ys_ref.shape[0]:
            ys_ref[n_rows:, :] = jnp.zeros((ys_ref.shape[0] - n_rows, ys_ref.shape[1]), ys_ref.dtype)

    tile_rows = xs_ref.shape[0]
    pl.when(used & (rows_ref[t] > tile_rows // 2))(lambda: swiglu_rows(tile_rows))
    pl.when(used & (rows_ref[t] <= tile_rows // 2))(lambda: swiglu_rows(tile_rows // 2))

    @pl.when(t >= nu_ref[0])
    def _():
        ys_ref[...] = jnp.zeros_like(ys_ref)


def _experts(xs, tile_expert, n_used, slot, nxt, tile_rows, w1_all, b1, w2_all, b2, layer):
    n_rows, dw = xs.shape
    tm = EXPERT_TILE
    _, n_exp, d, de2 = w1_all.shape
    src = np.arange(GLU_TILE)
    dst = np.where(src % 2 == 0, src // 2, LANES + src // 2)
    perm = jnp.asarray(dst[:, None] == np.arange(GLU_TILE)[None, :], BF16)
    b1p = b1.reshape(n_exp, de2 // GLU_TILE, LANES, 2).transpose(0, 1, 3, 2).reshape(n_exp, 1, de2)
    row = lambda t, te, nu, sl, nx, nr: (jnp.maximum(jnp.minimum(t, nu[0] - 1), 0), 0)
    exp3 = lambda t, te, nu, sl, nx, nr: (te[t], 0, 0)
    grid_spec = pltpu.PrefetchScalarGridSpec(
        num_scalar_prefetch=5,
        grid=(n_rows // tm,),
        in_specs=[
            pl.BlockSpec((tm, dw), row),
            pl.BlockSpec((1, 1, de2), exp3),
            pl.BlockSpec((1, 1, d), exp3),
            pl.BlockSpec((GLU_TILE, GLU_TILE), lambda t, te, nu, sl, nx, nr: (0, 0)),
            pl.BlockSpec(memory_space=pl.ANY),
            pl.BlockSpec(memory_space=pl.ANY),
        ],
        out_specs=pl.BlockSpec((tm, d // 2), lambda t, te, nu, sl, nx, nr: (t, 0)),
        scratch_shapes=[pltpu.VMEM((2, d, de2), F32), pltpu.VMEM((2, de2 // 2, d), F32),
                        pltpu.VMEM((d, de2), BF16), pltpu.VMEM((de2 // 2, d), BF16),
                        pltpu.SemaphoreType.DMA((2, 2))],
    )
    return pl.pallas_call(
        functools.partial(_experts_kernel, layer=layer),
        grid_spec=grid_spec,
        out_shape=jax.ShapeDtypeStruct((n_rows, d // 2), jnp.uint32),
        compiler_params=_params("arbitrary"),
        name="moe_experts",
    )(tile_expert, n_used, slot, nxt, tile_rows, xs, b1p, b2.reshape(n_exp, 1, d), perm, w1_all, w2_all)


SC_WINDOW = 128


def _sc_worker_rows(n_rows):
    sc = plsc.get_sparse_core_info()
    per_worker = n_rows // (sc.num_cores * sc.num_subcores)
    worker = lax.axis_index("subcore") * sc.num_cores + lax.axis_index("core")
    return worker * per_worker, per_worker


def _sc_mesh():
    return plsc.VectorSubcoreMesh(core_axis_name="core", subcore_axis_name="subcore")


def _sc_gather_rows(table, idx):
    n_idx, width = idx.shape[0], table.shape[1]

    @functools.partial(
        pl.kernel, out_type=jax.ShapeDtypeStruct((n_idx, width), table.dtype), mesh=_sc_mesh(),
        scratch_types=[pltpu.VMEM((SC_WINDOW,), jnp.int32), pltpu.VMEM((SC_WINDOW, width), table.dtype)],
        name="sc_gather_rows")
    def gather(table_hbm, idx_hbm, out_hbm, idx_v, rows_v):
        first, count = _sc_worker_rows(n_idx)

        @pl.loop(0, count // SC_WINDOW)
        def _(c):
            rows = pl.ds(first + c * SC_WINDOW, SC_WINDOW)
            pltpu.sync_copy(idx_hbm.at[rows], idx_v)
            pltpu.sync_copy(table_hbm.at[idx_v], rows_v)
            pltpu.sync_copy(rows_v, out_hbm.at[rows])

    return gather(table, idx)


def _sc_scatter_rows(rows, pos, n_out):
    n, width = rows.shape
    k = pos.shape[0]

    @functools.partial(
        pl.kernel, out_type=jax.ShapeDtypeStruct((n_out, width), rows.dtype), mesh=_sc_mesh(),
        scratch_types=[pltpu.VMEM((k, SC_WINDOW), jnp.int32), pltpu.VMEM((SC_WINDOW, width), rows.dtype)],
        name="sc_scatter_rows")
    def scatter(rows_hbm, pos_hbm, out_hbm, pos_v, rows_v):
        first, count = _sc_worker_rows(n)

        @pl.loop(0, count // SC_WINDOW)
        def _(c):
            src = pl.ds(first + c * SC_WINDOW, SC_WINDOW)
            pltpu.sync_copy(pos_hbm.at[:, src], pos_v)
            pltpu.sync_copy(rows_hbm.at[src], rows_v)
            for j in range(k):
                pltpu.sync_copy(rows_v, out_hbm.at[pos_v.at[j]])

    return scatter(rows, pos)


def _combine_dense_kernel(x_ref, gate_ref, w_ref, fg_ref, *rest, final_norm):
    y_refs, o_ref = rest[:TOP_K], rest[TOP_K]
    w = w_ref[...].T
    y_lo = y_hi = None
    for j in range(TOP_K):
        lo, hi = _unpack_bf16_pair(y_refs[j][...])
        wj = w[:, j:j + 1]
        y_lo = wj * lo if j == 0 else y_lo + wj * lo
        y_hi = wj * hi if j == 0 else y_hi + wj * hi
    out = x_ref[...] + gate_ref[0] * jnp.concatenate([y_lo, y_hi], axis=1)
    if final_norm:
        out = out * lax.rsqrt(jnp.mean(out * out, axis=-1, keepdims=True) + NORM_EPS) * fg_ref[...]
    o_ref[...] = out


def _combine_dense(x2, mod_l, gate_idx, seq, top_w, y4, final_g, final_norm, tm=512):
    n, d = x2.shape
    per_batch = seq // tm
    blocks = n // tm
    slot_spec = lambda j: pl.BlockSpec((tm, d // 2), lambda i: (j * blocks + i, 0))
    return pl.pallas_call(
        functools.partial(_combine_dense_kernel, final_norm=final_norm),
        grid=(blocks,),
        in_specs=[
            pl.BlockSpec((tm, d), lambda i: (i, 0)),
            pl.BlockSpec((1, 1, d), lambda i: (i // per_batch, 0, gate_idx)),
            pl.BlockSpec((top_w.shape[0], tm), lambda i: (0, i)),
            pl.BlockSpec((1, d), lambda i: (0, 0)),
        ] + [slot_spec(j) for j in range(TOP_K)],
        out_specs=pl.BlockSpec((tm, d), lambda i: (i, 0)),
        out_shape=jax.ShapeDtypeStruct((n, d), F32),
        compiler_params=_params("parallel"),
        name="moe_combine",
    )(x2, mod_l, top_w, final_g.reshape(1, d), *([y4] * TOP_K))


def _moe_ffn(x2, norm_g, mod_l, seq, router_w, router_b, w1_all, b1, w2_all, b2, layer, final_g, final_norm):
    n, d = x2.shape
    n_exp = router_w.shape[1]
    tile = EXPERT_TILE
    hp, idx_f, top_w, rank_f, counts = _router(x2, norm_g, mod_l, 3, seq, router_w, router_b)
    counts = counts[:, 0].astype(jnp.int32)
    padded = (counts + tile - 1) // tile * tile
    ends = jnp.cumsum(padded)
    starts = ends - padded
    idx = idx_f[:TOP_K].astype(jnp.int32)
    expert_ids = jnp.arange(n_exp, dtype=jnp.int32)
    start_of = jnp.sum(jnp.where(idx[..., None] == expert_ids, starts, 0), axis=-1)
    pos_slot_major = start_of + rank_f[:TOP_K].astype(jnp.int32)
    n_rows = n * TOP_K + n_exp * tile
    n_tiles = n_rows // tile
    tile_start = jnp.arange(n_tiles, dtype=jnp.int32) * tile
    tile_expert = jnp.sum((ends[None, :] <= tile_start[:, None]).astype(jnp.int32), axis=1)
    tile_expert = jnp.minimum(tile_expert, n_exp - 1)
    n_used = (ends[-1:] // tile).astype(jnp.int32)
    nonempty = counts > 0
    ordinal = jnp.cumsum(nonempty.astype(jnp.int32)) - 1
    later = nonempty[None, :] & (expert_ids[None, :] > expert_ids[:, None])
    nxt_e = jnp.min(jnp.where(later, expert_ids[None, :], n_exp), axis=1)
    nxt_e = jnp.where(nxt_e == n_exp, -1, nxt_e)
    slot = (ordinal[tile_expert] % 2).astype(jnp.int32)
    nxt = nxt_e[tile_expert].astype(jnp.int32)
    xs = _sc_scatter_rows(hp, pos_slot_major, n_rows)
    filled = jnp.clip((starts + counts)[tile_expert] - tile_start, 0, tile).astype(jnp.int32)
    ys = _experts(xs, tile_expert, n_used, slot, nxt, filled, w1_all, b1, w2_all, b2, layer)
    y4 = _sc_gather_rows(ys, pos_slot_major.reshape(-1))
    return _combine_dense(x2, mod_l, 5, seq, top_w, y4, final_g, final_norm)


def kernel(x, c, ada_w, ada_b, norm_mix_g, norm_ffn_g, router_w, router_b, moe_w1, moe_b1, moe_w2, moe_b2, ab_w_in, ab_w_out, hgrn_lb, hgrn_norm_g, rwkv_mu, rwkv_w0, rwkv_w2, rwkv_a0, rwkv_a2, rwkv_g2, rwkv_k_k, rwkv_k_a, rwkv_r_k, rwkv_ln_g, rwkv_ln_b, attn_w_in, attn_w_out, attn_lambda, attn_subln_g, rel_bias_table, final_norm_g):
    batch, seq, d = x.shape
    n = batch * seq
    depth = ada_w.shape[0]
    x2 = x.reshape(n, d)
    mod = _adaln(c, ada_w, ada_b)
    lb_all = jnp.cumsum(jax.nn.softmax(hgrn_lb.astype(F32), axis=1), axis=1)
    for layer in range(depth):
        mod_l = mod[layer].reshape(batch, 1, 6 * d)
        j = layer // 2
        if layer % 2 == 0:
            a_cols = 5 * (d // 2)
            w_in = ab_w_in[j].astype(BF16)
            p_a, p_b = _normmod_proj(x2, norm_mix_g[layer], mod_l, 0, seq,
                                     [w_in[:, :a_cols], w_in[:, a_cols:]], [F32, F32])
            y_a = _hgrn2(p_a, lb_all[:, j], hgrn_norm_g[j], batch, seq)
            y_b = _rwkv7(p_b, rwkv_mu[j], rwkv_w0[j], rwkv_w2[j], rwkv_a0[j], rwkv_a2[j], rwkv_g2[j],
                         rwkv_k_k[j], rwkv_k_a[j], rwkv_r_k[j], rwkv_ln_g[j], rwkv_ln_b[j], batch, seq)
            w_out = ab_w_out[j].astype(BF16)
            x2 = _outproj_residual(x2, mod_l, 2, seq, [y_a, y_b], [w_out[:d // 2], w_out[d // 2:]])
        else:
            (qkv,) = _normmod_proj(x2, norm_mix_g[layer], mod_l, 0, seq, [attn_w_in[j].astype(BF16)], [BF16])
            lam = attn_lambda[j].astype(F32)
            lam_init = 0.8 - 0.6 * math.exp(-0.3 * layer)
            lam_full = jnp.exp(jnp.sum(lam[0] * lam[1])) - jnp.exp(jnp.sum(lam[2] * lam[3])) + lam_init
            o = _diff_attention(qkv, rel_bias_table, lam_full, attn_subln_g[j], lam_init, batch, seq)
            x2 = _outproj_residual(x2, mod_l, 2, seq, [o], [attn_w_out[j].astype(BF16)])
        x2 = _moe_ffn(x2, norm_ffn_g[layer], mod_l, seq, router_w[layer], router_b[layer], moe_w1,
                      moe_b1[layer], moe_w2, moe_b2[layer], layer, final_norm_g, layer == depth - 1)
    return x2.reshape(batch, seq, d)
```

```python
import functools
import math

import jax
import jax.numpy as jnp
import numpy as np
from jax import lax
from jax.experimental import pallas as pl
from jax.experimental.pallas import tpu as pltpu
from jax.experimental.pallas import tpu_sc as plsc

F32 = jnp.float32
BF16 = jnp.bfloat16
HIGHEST = lax.Precision.HIGHEST

NORM_EPS = 1e-6

A_HEAD_DIM = 128
A_CHUNK = 32
A_GROUP = 512
A_HEADS_PER_STEP = 2
B_HEAD_DIM = 64
B_CHUNK = 64
B_STEPS = 2
B_LN_EPS = 1e-5 * B_HEAD_DIM
C_HEAD_DIM = 64
REL_BUCKETS = 32
REL_MAX_DISTANCE = 128
TOP_K = 4
SWIGLU_LIMIT = 7.0
SWIGLU_ALPHA = 1.702

V7X_VMEM_BYTES = 64 * 1024 * 1024
VMEM_LIMIT = V7X_VMEM_BYTES - 8 * 1024 * 1024
LANES = 128


def _params(*sem):
    return pltpu.CompilerParams(dimension_semantics=sem, vmem_limit_bytes=VMEM_LIMIT)


def _sigmoid(x):
    return 1.0 / (1.0 + jnp.exp(-x))


def _silu(x):
    return x * _sigmoid(x)


def _dot(a, b):
    return jnp.dot(a, b, preferred_element_type=F32)


def _dot_nt(a, b):
    return lax.dot_general(a, b, (((1,), (1,)), ((), ())), preferred_element_type=F32)


def _dot_tn(a, b):
    return lax.dot_general(a, b, (((0,), (0,)), ((), ())), preferred_element_type=F32)


def _adaln_kernel(c_ref, w_ref, b_ref, o_ref):
    cond = _silu(c_ref[...])
    o_ref[0] = jnp.dot(cond, w_ref[0], precision=HIGHEST, preferred_element_type=F32) + b_ref[0]


def _adaln(c, ada_w, ada_b):
    n_layers, d, n_out = ada_w.shape
    batch = c.shape[0]
    tn = 1536
    return pl.pallas_call(
        _adaln_kernel,
        grid=(n_layers, n_out // tn),
        in_specs=[
            pl.BlockSpec((batch, d), lambda l, j: (0, 0)),
            pl.BlockSpec((1, d, tn), lambda l, j: (l, 0, j)),
            pl.BlockSpec((1, 1, tn), lambda l, j: (l, 0, j)),
        ],
        out_specs=pl.BlockSpec((1, batch, tn), lambda l, j: (l, 0, j)),
        out_shape=jax.ShapeDtypeStruct((n_layers, batch, n_out), F32),
        compiler_params=_params("parallel", "parallel"),
        name="adaln",
    )(c, ada_w, ada_b.reshape(n_layers, 1, n_out))


def _modulated_norm(x, g, shift, scale):
    y = x * lax.rsqrt(jnp.mean(x * x, axis=-1, keepdims=True) + NORM_EPS)
    return (y * g) * (1.0 + scale) + shift


def _normmod_proj_kernel(x_ref, g_ref, sh_ref, sc_ref, *rest, n_w):
    w_refs, o_refs = rest[:n_w], rest[n_w:]
    h = _modulated_norm(x_ref[...], g_ref[...], sh_ref[0], sc_ref[0]).astype(BF16)
    for w_ref, o_ref in zip(w_refs, o_refs):
        o_ref[...] = _dot(h, w_ref[...]).astype(o_ref.dtype)


def _normmod_proj(x2, g, mod_l, shift_idx, seq, weights, out_dtypes, tm=512):
    n, d = x2.shape
    per_batch = seq // tm
    in_specs = [
        pl.BlockSpec((tm, d), lambda i: (i, 0)),
        pl.BlockSpec((1, d), lambda i: (0, 0)),
        pl.BlockSpec((1, 1, d), lambda i: (i // per_batch, 0, shift_idx)),
        pl.BlockSpec((1, 1, d), lambda i: (i // per_batch, 0, shift_idx + 1)),
    ]
    in_specs += [pl.BlockSpec(w.shape, lambda i: (0, 0)) for w in weights]
    out_specs = [pl.BlockSpec((tm, w.shape[1]), lambda i: (i, 0)) for w in weights]
    out_shape = [jax.ShapeDtypeStruct((n, w.shape[1]), dt) for w, dt in zip(weights, out_dtypes)]
    return pl.pallas_call(
        functools.partial(_normmod_proj_kernel, n_w=len(weights)),
        grid=(n // tm,),
        in_specs=in_specs,
        out_specs=out_specs,
        out_shape=out_shape,
        compiler_params=_params("parallel"),
        name="normmod_proj",
    )(x2, g.reshape(1, d), mod_l, mod_l, *weights)


def _outproj_kernel(x_ref, gate_ref, *rest, n_y):
    y_refs, w_refs, o_ref = rest[:n_y], rest[n_y:2 * n_y], rest[2 * n_y]
    acc = _dot(y_refs[0][...], w_refs[0][...])
    for y_ref, w_ref in zip(y_refs[1:], w_refs[1:]):
        acc += _dot(y_ref[...], w_ref[...])
    o_ref[...] = x_ref[...] + gate_ref[0] * acc


def _outproj_residual(x2, mod_l, gate_idx, seq, ys, ws, tm=512):
    n, d = x2.shape
    per_batch = seq // tm
    in_specs = [
        pl.BlockSpec((tm, d), lambda i: (i, 0)),
        pl.BlockSpec((1, 1, d), lambda i: (i // per_batch, 0, gate_idx)),
    ]
    in_specs += [pl.BlockSpec((tm, y.shape[1]), lambda i: (i, 0)) for y in ys]
    in_specs += [pl.BlockSpec(w.shape, lambda i: (0, 0)) for w in ws]
    return pl.pallas_call(
        functools.partial(_outproj_kernel, n_y=len(ys)),
        grid=(n // tm,),
        in_specs=in_specs,
        out_specs=pl.BlockSpec((tm, d), lambda i: (i, 0)),
        out_shape=jax.ShapeDtypeStruct((n, d), F32),
        compiler_params=_params("parallel"),
        name="outproj_residual",
    )(x2, mod_l, *ys, *ws)


def _chunk_cumsum(x, chunk, reverse):
    rows = x.shape[0]
    pos = lax.broadcasted_iota(jnp.int32, x.shape, 0) % chunk
    s = 1
    while s < chunk:
        if reverse:
            x = x + jnp.where(pos < chunk - s, pltpu.roll(x, rows - s, axis=0), 0.0)
        else:
            x = x + jnp.where(pos >= s, pltpu.roll(x, s, axis=0), 0.0)
        s *= 2
    return x


def _hgrn_groups(slabs):
    g_rows, dk = slabs[0][0].shape
    n_chunks = g_rows // A_CHUNK
    ti = lax.broadcasted_iota(jnp.int32, (A_CHUNK, A_CHUNK), 0)
    si = lax.broadcasted_iota(jnp.int32, (A_CHUNK, A_CHUNK), 1)
    chunk_rows = [slice(c * A_CHUNK, (c + 1) * A_CHUNK) for c in range(n_chunks)]
    prep = []
    for q, f, v, lb, st, reverse in slabs:
        fg = lb + (1.0 - lb) * _sigmoid(f)
        k = 1.0 - fg
        b = _chunk_cumsum(jnp.log(fg), A_CHUNK, reverse)
        b3 = b.reshape(n_chunks, A_CHUNK, dk)
        edge = b3[:, 0:1, :] if reverse else b3[:, A_CHUNK - 1:A_CHUNK, :]
        prep.append(dict(
            q_in=(q * jnp.exp(b)).astype(BF16), k_in=(k * jnp.exp(-b)).astype(BF16),
            k_st=(k.reshape(n_chunks, A_CHUNK, dk) * jnp.exp(edge - b3)).astype(BF16),
            decay=jnp.exp(edge), vb=v.astype(BF16), st=st, reverse=reverse,
            mask=(si >= ti) if reverse else (si <= ti)))
    for p in prep:
        p["scores"] = [_dot_nt(p["q_in"][r], p["k_in"][r]) for r in chunk_rows]
        p["dstate"] = [_dot_tn(p["vb"][r], p["k_st"][c]) for c, r in enumerate(chunk_rows)]
    for p in prep:
        p["intra"] = [_dot(jnp.where(p["mask"], s, 0.0).astype(BF16), p["vb"][r])
                      for s, r in zip(p["scores"], chunk_rows)]
    results = []
    for p in prep:
        st = p["st"]
        outs = [None] * n_chunks
        for c in (range(n_chunks - 1, -1, -1) if p["reverse"] else range(n_chunks)):
            outs[c] = p["intra"][c] + _dot_nt(p["q_in"][chunk_rows[c]], st.astype(BF16))
            st = st * p["decay"][c] + p["dstate"][c]
        results.append((jnp.concatenate(outs, axis=0), st))
    return results


def _hgrn_kernel(q_ref, ff_ref, fb_ref, i_ref, g_ref, lb_ref, ng_ref, o_ref, of_ref, ob_ref):
    seq = q_ref.shape[0]
    dk = A_HEAD_DIM
    n_heads = q_ref.shape[1] // dk
    n_groups = seq // A_GROUP
    head_cols = [slice(h * dk, (h + 1) * dk) for h in range(n_heads)]

    def body(j, states):
        rf = pl.ds(pl.multiple_of(j * A_GROUP, A_GROUP), A_GROUP)
        rb = pl.ds(pl.multiple_of((n_groups - 1 - j) * A_GROUP, A_GROUP), A_GROUP)
        slabs = []
        for h, cols in enumerate(head_cols):
            slabs.append((_silu(q_ref[rf, cols]), ff_ref[rf, cols], i_ref[rf, cols], lb_ref[0:1, cols],
                          states[2 * h], False))
            slabs.append((_silu(q_ref[rb, cols]), fb_ref[rb, cols], i_ref[rb, cols], lb_ref[1:2, cols],
                          states[2 * h + 1], True))
        results = _hgrn_groups(slabs)
        for h, cols in enumerate(head_cols):
            of_ref[rf, cols] = results[2 * h][0]
            ob_ref[rb, cols] = results[2 * h + 1][0]
        return tuple(r[1] for r in results)

    zero = jnp.zeros((dk, dk), F32)
    lax.fori_loop(0, n_groups, body, (zero,) * (2 * n_heads))

    def finish(j, carry):
        r = pl.ds(pl.multiple_of(j * A_GROUP, A_GROUP), A_GROUP)
        for cols in head_cols:
            o = of_ref[r, cols] + ob_ref[r, cols]
            o = o * lax.rsqrt(jnp.mean(o * o, axis=-1, keepdims=True) + NORM_EPS) * ng_ref[...]
            o_ref[r, cols] = (o * _silu(g_ref[r, cols])).astype(o_ref.dtype)
        return carry

    lax.fori_loop(0, n_groups, finish, 0)


def _hgrn2(p_a, lb, norm_g, batch, seq):
    n = p_a.shape[0]
    width = p_a.shape[1] // 5
    cols = A_HEADS_PER_STEP * A_HEAD_DIM
    steps = width // cols
    sect = lambda s: pl.BlockSpec((seq, cols), lambda b, h, s=s: (b, s * steps + h))
    return pl.pallas_call(
        _hgrn_kernel,
        grid=(batch, steps),
        in_specs=[sect(0), sect(1), sect(2), sect(3), sect(4),
                  pl.BlockSpec((2, cols), lambda b, h: (0, h)),
                  pl.BlockSpec((1, A_HEAD_DIM), lambda b, h: (0, 0))],
        out_specs=pl.BlockSpec((seq, cols), lambda b, h: (b, h)),
        out_shape=jax.ShapeDtypeStruct((n, width), BF16),
        scratch_shapes=[pltpu.VMEM((seq, cols), F32), pltpu.VMEM((seq, cols), F32)],
        compiler_params=_params("parallel", "parallel"),
        name="hgrn2",
    )(p_a, p_a, p_a, p_a, p_a, lb, norm_g.reshape(1, A_HEAD_DIM))


def _softplus(z):
    return jnp.maximum(z, 0.0) + jnp.log(1.0 + jnp.exp(-jnp.abs(z)))


def _rwkv_prep_kernel(p_ref, prev_ref, next_ref, mu_ref, w0_ref, w2_ref, a0_ref, a2_ref, g2_ref,
                      kk_ref, ka_ref, rk_ref, ones_ref, tri_ref,
                      r_out, k_out, v_out, kkn_out, b_out, cumf_out, cumb_out, g_out, bonus_out):
    i = pl.program_id(1)
    last = pl.num_programs(1) - 1
    ts = p_ref.shape[0]
    width = a0_ref.shape[1]
    heads = width // B_HEAD_DIM
    p = p_ref[...]
    prev_row = jnp.where(i == 0, 0.0, prev_ref[7:8, :])
    next_row = jnp.where(i == last, 0.0, next_ref[0:1, :])
    row = lax.broadcasted_iota(jnp.int32, (ts, 1), 0)
    up = jnp.where(row == 0, prev_row, pltpu.roll(p, 1, axis=0))
    dn = jnp.where(row == ts - 1, next_row, pltpu.roll(p, ts - 1, axis=0))
    p = p + mu_ref[...] * (0.5 * (up + dn) - p)

    r = p[:, 0:width]
    k = p[:, width:2 * width]
    v = p[:, 2 * width:3 * width]
    o = 3 * width
    wlo_f = p[:, o:o + 64]
    wlo_b = p[:, o + 64:o + 128]
    alo = p[:, o + 128:o + 192]
    glo = p[:, o + 192:o + 320]

    def log_decay(wlo, d):
        z = w0_ref[d:d + 1, :] + _dot(jnp.tanh(wlo).astype(BF16), w2_ref[d])
        w = -_softplus(-z) - 0.5
        return -jnp.exp(w)

    def split_dot(sel, t, right):
        hi = t.astype(BF16)
        lo = (t - hi.astype(F32)).astype(BF16)
        return (_dot(hi, sel) + _dot(lo, sel)) if right else (_dot(sel, hi) + _dot(sel, lo))

    cum_f = split_dot(tri_ref[0], log_decay(wlo_f, 0), False)
    cum_b = split_dot(tri_ref[1], log_decay(wlo_b, 1), False)
    a = _sigmoid(a0_ref[...] + _dot(alo.astype(BF16), a2_ref[...]))
    g = _dot(_sigmoid(glo).astype(BF16), g2_ref[...])
    kk = k * kk_ref[...]
    head_sum = lambda t: split_dot(ones_ref[...], t, True)

    kk_n = kk / jnp.maximum(jnp.sqrt(head_sum(kk * kk)), 1e-12)
    k_mod = k * (1.0 + (a - 1.0) * ka_ref[...])
    bonus = head_sum(r * k_mod * rk_ref[...]) * v
    g_out[...] = g
    bonus_out[...] = bonus
    for h in range(heads):
        sl = slice(h * B_HEAD_DIM, (h + 1) * B_HEAD_DIM)
        r_out[0, h] = r[:, sl]
        k_out[0, h] = k_mod[:, sl]
        v_out[0, h] = v[:, sl]
        kkn_out[0, h] = kk_n[:, sl]
        b_out[0, h] = (kk_n * a)[:, sl]
        cumf_out[0, h] = cum_f[:, sl]
        cumb_out[0, h] = cum_b[:, sl]


def _rwkv_prep(p_b, mu, w0, w2, a0, a2, g2, k_k, k_a, r_k, batch, seq, ts=256):
    n, cols = p_b.shape
    width = a0.shape[0]
    heads = width // B_HEAD_DIM
    nblk = seq // ts
    rows8 = ts // 8
    head_id = np.arange(width) // B_HEAD_DIM
    ones = jnp.asarray(head_id[:, None] == head_id[None, :], BF16)
    full = lambda a: pl.BlockSpec(a.shape, lambda b, i: (0,) * a.ndim)
    vec = lambda a: a.reshape(1, -1)
    row = np.arange(ts)
    same_chunk = (row[:, None] // B_CHUNK) == (row[None, :] // B_CHUNK)
    tri = jnp.asarray(np.stack([same_chunk & (row[None, :] <= row[:, None]),
                                same_chunk & (row[None, :] >= row[:, None])]), BF16)
    args = [vec(mu), w0, w2.astype(BF16), vec(a0), a2.astype(BF16), g2.astype(BF16),
            vec(k_k), vec(k_a), vec(r_k), ones, tri]
    hm = pl.BlockSpec((1, heads, ts, B_HEAD_DIM), lambda b, i: (b, 0, i, 0))
    tokm = pl.BlockSpec((ts, width), lambda b, i: (b * nblk + i, 0))
    hm_shape = jax.ShapeDtypeStruct((batch, heads, seq, B_HEAD_DIM), F32)
    tok_shape = jax.ShapeDtypeStruct((n, width), F32)
    return pl.pallas_call(
        _rwkv_prep_kernel,
        grid=(batch, nblk),
        in_specs=[
            pl.BlockSpec((ts, cols), lambda b, i: (b * nblk + i, 0)),
            pl.BlockSpec((8, cols), lambda b, i: (jnp.maximum((b * nblk + i) * rows8 - 1, 0), 0)),
            pl.BlockSpec((8, cols), lambda b, i: (jnp.minimum((b * nblk + i + 1) * rows8, n // 8 - 1), 0)),
        ] + [full(a) for a in args],
        out_specs=[hm] * 7 + [tokm] * 2,
        out_shape=[hm_shape] * 7 + [tok_shape] * 2,
        compiler_params=_params("parallel", "parallel"),
        name="rwkv_prep",
    )(p_b, p_b, p_b, *args)


def _rwkv_chunks(chains):
    c, d = chains[0][0].shape
    ti = lax.broadcasted_iota(jnp.int32, (c, c), 0)
    si = lax.broadcasted_iota(jnp.int32, (c, c), 1)
    eye = jnp.where(ti == si, 1.0, 0.0)
    masks = {rev: (((si > ti), (si >= ti)) if rev else ((si < ti), (si <= ti))) for rev in (False, True)}
    t2 = lax.broadcasted_iota(jnp.int32, (c, 2 * c), 0)
    s2 = lax.broadcasted_iota(jnp.int32, (c, 2 * c), 1)
    s2 = jnp.where(s2 >= c, s2 - c, s2)
    incl2 = {False: s2 <= t2, True: s2 >= t2}

    row = lax.broadcasted_iota(jnp.int32, (c, d), 0)
    st = []
    for r, k, v, kk, b, cum, rev in chains:
        before = (jnp.where(row == c - 1, 0.0, pltpu.roll(cum, c - 1, axis=0)) if rev
                  else jnp.where(row == 0, 0.0, pltpu.roll(cum, 1, axis=0)))
        grow = jnp.exp(-cum)
        kt = (kk * jnp.exp(before)).astype(BF16)
        rt = r * jnp.exp(cum)
        b_ = (b * grow).astype(BF16)
        kb = jnp.concatenate([(k * grow).astype(BF16), b_], axis=0)
        st.append(dict(vb=v.astype(BF16), kt=kt, rt=rt, b_=b_, kb=kb,
                       decay=jnp.exp(cum[0:1, :] if rev else cum[c - 1:c, :]), rev=rev,
                       lhs=jnp.concatenate([kt, rt.astype(BF16)], axis=0)))
    for s in st:
        s["big"] = _dot_nt(s["lhs"], s["kb"])
    for s in st:
        strict, incl = masks[s["rev"]]
        big = s["big"]
        a_ab = jnp.where(strict, big[:c, c:], 0.0)
        s["a_ak"] = jnp.where(strict, big[:c, :c], 0.0).astype(BF16)
        s["p"] = jnp.where(incl2[s["rev"]], big[c:], 0.0).astype(BF16)
        s["m"] = eye - a_ab
        s["ab"] = a_ab.astype(BF16)
    for s in st:
        s["aj"] = _dot(s["ab"], s["ab"])
        s["av"] = _dot(s["a_ak"], s["vb"])
    span = 2
    while span < c:
        span *= 2
        for s in st:
            ajb = s["aj"].astype(BF16)
            if span < c:
                both = _dot(jnp.concatenate([ajb, s["m"].astype(BF16)], axis=0), ajb)
                s["aj"], s["m"] = both[:c], s["m"] + both[c:]
            else:
                s["m"] = s["m"] + _dot(s["m"].astype(BF16), ajb)
    for s in st:
        wu = _dot(s["m"].astype(BF16), jnp.concatenate([s["kt"], s["av"].astype(BF16)], axis=1))
        s["nwu"] = -wu.astype(BF16)
    for s in st:
        nw, nu0, vb = s["nwu"][:, :d], s["nwu"][:, d:], s["vb"]
        rhs = jnp.concatenate([jnp.concatenate([vb, jnp.zeros_like(vb)], axis=1),
                               jnp.concatenate([nu0, nw], axis=1)], axis=0)
        s["yq"] = _dot(s["p"], rhs)
        s["h_t"] = _dot_tn(jnp.concatenate([vb, nu0], axis=0), s["kb"])
        s["nbw"] = _dot_tn(s["b_"], nw)
    return [dict(q=(s["rt"] + s["yq"][:, d:]).astype(BF16), y1=s["yq"][:, :d], nbw=s["nbw"].astype(BF16),
                 h_t=s["h_t"], decay=s["decay"]) for s in st]


def _rwkv_advance(states, steps):
    ttbs = [tt.astype(BF16) for tt in states]
    ys = [_dot_nt(s["q"], ttb) + s["y1"] for s, ttb in zip(steps, ttbs)]
    new = [(tt + _dot_nt(ttb, s["nbw"]) + s["h_t"]) * s["decay"] for tt, ttb, s in zip(states, ttbs, steps)]
    return ys, new


def _rwkv_scan_kernel(rf, kf, vf, kkf, bf, lwf, rb, kb, vb, kkb, bb, lwb, yf_ref, yb_ref, tf_ref, tb_ref):
    heads, ts = rf.shape[1], rf.shape[2]
    n_chunks = ts // B_CHUNK

    @pl.when(pl.program_id(1) == 0)
    def _():
        tf_ref[...] = jnp.zeros_like(tf_ref)
        tb_ref[...] = jnp.zeros_like(tb_ref)

    def body(j, carry):
        rows_f = [pl.ds(pl.multiple_of((j * B_STEPS + s) * B_CHUNK, B_CHUNK), B_CHUNK) for s in range(B_STEPS)]
        rows_b = [pl.ds(pl.multiple_of((n_chunks - 1 - j * B_STEPS - s) * B_CHUNK, B_CHUNK), B_CHUNK)
                  for s in range(B_STEPS)]
        chains = []
        for s in range(B_STEPS):
            for h in range(heads):
                chains.append([ref[0, h, rows_f[s], :] for ref in (rf, kf, vf, kkf, bf, lwf)] + [False])
                chains.append([ref[0, h, rows_b[s], :] for ref in (rb, kb, vb, kkb, bb, lwb)] + [True])
        solved = _rwkv_chunks(chains)
        states = []
        for h in range(heads):
            states += [tf_ref[h], tb_ref[h]]
        for s in range(B_STEPS):
            ys, states = _rwkv_advance(states, solved[s * 2 * heads:(s + 1) * 2 * heads])
            for h in range(heads):
                yf_ref[0, h, rows_f[s], :] = ys[2 * h]
                yb_ref[0, h, rows_b[s], :] = ys[2 * h + 1]
        for h in range(heads):
            tf_ref[h], tb_ref[h] = states[2 * h], states[2 * h + 1]
        return carry

    lax.fori_loop(0, n_chunks // B_STEPS, body, 0)


def _rwkv_scan(r, k, v, kkn, b, lw_f, lw_b, ts=256):
    batch, heads, seq, d = r.shape
    nblk = seq // ts
    fwd = pl.BlockSpec((1, heads, ts, d), lambda bi, i: (bi, 0, i, 0))
    bwd = pl.BlockSpec((1, heads, ts, d), lambda bi, i: (bi, 0, nblk - 1 - i, 0))
    shape = jax.ShapeDtypeStruct(r.shape, F32)
    return pl.pallas_call(
        _rwkv_scan_kernel,
        grid=(batch, nblk),
        in_specs=[fwd] * 6 + [bwd] * 6,
        out_specs=[fwd, bwd],
        out_shape=[shape, shape],
        scratch_shapes=[pltpu.VMEM((heads, d, d), F32), pltpu.VMEM((heads, d, d), F32)],
        compiler_params=_params("parallel", "arbitrary"),
        name="rwkv_scan",
    )(r, k, v, kkn, b, lw_f, r, k, v, kkn, b, lw_b)


def _rwkv_post_kernel(yf_ref, yb_ref, g_ref, bonus_ref, lng_ref, lnb_ref, o_ref):
    heads = yf_ref.shape[1]
    outs = []
    for h in range(heads):
        y = yf_ref[0, h] + yb_ref[0, h]
        mean = jnp.mean(y, axis=-1, keepdims=True)
        var = jnp.mean(jnp.square(y - mean), axis=-1, keepdims=True)
        outs.append((y - mean) * lax.rsqrt(var + B_LN_EPS))
    yn = jnp.concatenate(outs, axis=1) * lng_ref[...] + lnb_ref[...]
    o_ref[...] = ((yn + bonus_ref[...]) * g_ref[...]).astype(o_ref.dtype)


def _rwkv_post(y_f, y_b, g, bonus, ln_g, ln_b, ts=512):
    batch, heads, seq, d = y_f.shape
    n, width = g.shape
    nblk = seq // ts
    hm = pl.BlockSpec((1, heads, ts, d), lambda b, i: (b, 0, i, 0))
    tokm = pl.BlockSpec((ts, width), lambda b, i: (b * nblk + i, 0))
    vec = pl.BlockSpec((1, width), lambda b, i: (0, 0))
    return pl.pallas_call(
        _rwkv_post_kernel,
        grid=(batch, nblk),
        in_specs=[hm, hm, tokm, tokm, vec, vec],
        out_specs=tokm,
        out_shape=jax.ShapeDtypeStruct((n, width), BF16),
        compiler_params=_params("parallel", "parallel"),
        name="rwkv_post",
    )(y_f, y_b, g, bonus, ln_g.reshape(1, width), ln_b.reshape(1, width))


def _rwkv7(p_b, mu, w0, w2, a0, a2, g2, k_k, k_a, r_k, ln_g, ln_b, batch, seq):
    r, k, v, kkn, b, lw_f, lw_b, g, bonus = _rwkv_prep(p_b, mu, w0, w2, a0, a2, g2, k_k, k_a, r_k, batch, seq)
    y_f, y_b = _rwkv_scan(r, k, v, kkn, b, lw_f, lw_b)
    return _rwkv_post(y_f, y_b, g, bonus, ln_g, ln_b)


_BUCKET_EDGES = (0, 1, 2, 3, 4, 5, 6, 7, 8, 12, 16, 23, 32, 46, 64, 91)
ATTN_TK = 128
C_HEADS_PER_STEP = 4


def _bias_band_kernel(table_ref, o_ref):
    h = pl.program_id(0)
    n_d, tq, tk = o_ref.shape[1:]
    half = REL_BUCKETS // 2
    r = lax.broadcasted_iota(jnp.int32, (tq, tk), 0)
    c = lax.broadcasted_iota(jnp.int32, (tq, tk), 1)
    for d in range(n_d):
        rel = tk * (d - 1) + c - r
        n = jnp.abs(rel)
        vals = []
        for sign in range(2):
            val = jnp.full((tq, tk), table_ref[sign * half + half - 1, h], F32)
            for bkt in range(half - 2, -1, -1):
                val = jnp.where(n < _BUCKET_EDGES[bkt + 1], table_ref[sign * half + bkt, h], val)
            vals.append(val)
        o_ref[0, d] = jnp.where(rel > 0, vals[1], vals[0])


def _bias_band(rel_table, tq):
    heads = rel_table.shape[1]
    n_d = tq // ATTN_TK + 2
    return pl.pallas_call(
        _bias_band_kernel,
        grid=(heads,),
        in_specs=[pl.BlockSpec(memory_space=pltpu.SMEM)],
        out_specs=pl.BlockSpec((1, n_d, tq, ATTN_TK), lambda h: (h, 0, 0, 0)),
        out_shape=jax.ShapeDtypeStruct((heads, n_d, tq, ATTN_TK), F32),
        compiler_params=_params("parallel"),
        name="bias_band",
    )(rel_table)


def _diff_attn_kernel(table_ref, lam_ref, q_ref, k_ref, v_ref, band_ref, g_ref, o_ref, lg_ref, vext_ref, kt_ref,
                      *, lam_init):
    i = pl.program_id(2)
    tq = q_ref.shape[0]
    dh2 = 2 * C_HEAD_DIM
    n_heads = q_ref.shape[1] // dh2
    seq = k_ref.shape[0]
    n_d = band_ref.shape[1]
    n_kt = seq // ATTN_TK
    ratio = tq // ATTN_TK
    half = REL_BUCKETS // 2
    head_cols = [slice(hh * dh2, (hh + 1) * dh2) for hh in range(n_heads)]

    @pl.when(i == 0)
    def _():
        for hh, cols in enumerate(head_cols):
            kt_ref[hh] = k_ref[:, cols].astype(F32).T.astype(BF16)
            v = v_ref[:, cols]
            vext_ref[hh] = jnp.concatenate([v, jnp.ones_like(v)], axis=1)

    lane = lax.broadcasted_iota(jnp.int32, (1, dh2), 1)
    first_band = ratio * i - 1
    col = lax.broadcasted_iota(jnp.int32, (1, seq), 1)
    for hh, cols in enumerate(head_cols):
        h = pl.program_id(1) * n_heads + hh
        q = q_ref[:, cols] * jnp.asarray(C_HEAD_DIM ** -0.5, q_ref.dtype)
        zero = jnp.zeros_like(q)
        far_row = jnp.where(col < first_band * ATTN_TK, table_ref[half - 1, h],
                            jnp.where(col >= (first_band + n_d) * ATTN_TK, table_ref[REL_BUCKETS - 1, h],
                                      -jnp.inf))
        row_max = []
        for m in range(2):
            qm = jnp.where((lane >= m * C_HEAD_DIM) & (lane < (m + 1) * C_HEAD_DIM), q, zero)
            far = _dot(qm, kt_ref[hh]) + far_row
            mx = jnp.max(far, axis=-1, keepdims=True)
            lg_ref[hh, m] = far
            for d in (0, n_d - 1) + tuple(range(1, n_d - 1)):
                kt = first_band + d
                valid = (kt >= 0) & (kt < n_kt)
                c0 = pl.multiple_of(jnp.clip(kt, 0, n_kt - 1) * ATTN_TK, ATTN_TK)
                near = _dot(qm, kt_ref[hh, :, pl.ds(c0, ATTN_TK)]) + band_ref[hh, d]
                mx = jnp.maximum(mx, jnp.where(valid, jnp.max(near, axis=-1, keepdims=True), -jnp.inf))
                lg_ref[hh, m, :, pl.ds(c0, ATTN_TK)] = near
            row_max.append(mx)
        parts = []
        for m in range(2):
            e = jnp.exp(lg_ref[hh, m] - row_max[m]).astype(BF16)
            pv = _dot(e, vext_ref[hh])
            parts.append(pv[:, :dh2] * (1.0 / pv[:, dh2:dh2 + 1]))
        o = parts[0] - lam_ref[0] * parts[1]
        o = o * lax.rsqrt(jnp.mean(o * o, axis=-1, keepdims=True) + NORM_EPS) * g_ref[...]
        o_ref[:, cols] = (o * (1.0 - lam_init)).astype(o_ref.dtype)


def _diff_attention(qkv, rel_table, lam_full, subln_g, lam_init, batch, seq, tq=256):
    n = qkv.shape[0]
    dh2 = 2 * C_HEAD_DIM
    heads = qkv.shape[1] // (3 * dh2)
    nq = seq // tq
    per = C_HEADS_PER_STEP
    groups = heads // per
    cols = per * dh2
    band = _bias_band(rel_table, tq)
    return pl.pallas_call(
        functools.partial(_diff_attn_kernel, lam_init=lam_init),
        grid=(batch, groups, nq),
        in_specs=[
            pl.BlockSpec(memory_space=pltpu.SMEM),
            pl.BlockSpec(memory_space=pltpu.SMEM),
            pl.BlockSpec((tq, cols), lambda b, h, i: (b * nq + i, h)),
            pl.BlockSpec((seq, cols), lambda b, h, i: (b, groups + h)),
            pl.BlockSpec((seq, cols), lambda b, h, i: (b, 2 * groups + h)),
            pl.BlockSpec((per,) + band.shape[1:], lambda b, h, i: (h, 0, 0, 0)),
            pl.BlockSpec((1, dh2), lambda b, h, i: (0, 0)),
        ],
        out_specs=pl.BlockSpec((tq, cols), lambda b, h, i: (b * nq + i, h)),
        out_shape=jax.ShapeDtypeStruct((n, heads * dh2), BF16),
        scratch_shapes=[pltpu.VMEM((per, 2, tq, seq), F32), pltpu.VMEM((per, seq, 2 * dh2), BF16),
                        pltpu.VMEM((per, dh2, seq), BF16)],
        compiler_params=_params("parallel", "parallel", "arbitrary"),
        name="diff_attention",
    )(rel_table, lam_full.reshape(1), qkv, qkv, qkv, band, subln_g.reshape(1, dh2))


MOE_TILE = 1024
EXPERT_TILE = 512
EXPERT_CHUNK = 512
_HI_MASK = np.uint32(0xFFFF0000)


def _pack_bf16_pair(lo, hi):
    as_bits = lambda t: lax.bitcast_convert_type(t.astype(BF16).astype(F32), jnp.uint32)
    return (as_bits(hi) & _HI_MASK) | (as_bits(lo) >> 16)


def _unpack_bf16_pair(word):
    lo = lax.bitcast_convert_type(word << 16, F32).astype(BF16)
    hi = lax.bitcast_convert_type(word & _HI_MASK, F32).astype(BF16)
    return lo, hi


def _router_kernel(x_ref, g_ref, sh_ref, sc_ref, rw_ref, rb_ref, hp_ref, idx_ref, w_ref, rank_ref, cnt_ref,
                   run_ref, *, n_experts):
    @pl.when(pl.program_id(0) == 0)
    def _():
        run_ref[...] = jnp.zeros_like(run_ref)

    tm, d = x_ref.shape
    h = _modulated_norm(x_ref[...], g_ref[...], sh_ref[0], sc_ref[0])
    hp_ref[...] = _pack_bf16_pair(h[:, :d // 2], h[:, d // 2:])
    logits = _dot_nt(rw_ref[...], h.astype(BF16)) + rb_ref[...]
    expert = lax.broadcasted_iota(jnp.int32, logits.shape, 0).astype(F32)
    neg = jnp.float32(-jnp.inf)
    picks, vals = [], []
    for _ in range(TOP_K):
        m = jnp.max(logits, axis=0, keepdims=True)
        first = jnp.min(jnp.where(logits == m, expert, float(n_experts)), axis=0, keepdims=True)
        hit = expert == first
        picks.append((first, hit))
        vals.append(m)
        logits = jnp.where(hit, neg, logits)
    es = [jnp.exp(v - vals[0]) for v in vals]
    inv = 1.0 / (es[0] + es[1] + es[2] + es[3])
    onehot = sum(jnp.where(hit, 1.0, 0.0) for _, hit in picks)
    ri = lax.broadcasted_iota(jnp.int32, (tm, tm), 0)
    ci = lax.broadcasted_iota(jnp.int32, (tm, tm), 1)
    earlier = jnp.where(ri < ci, 1.0, 0.0).astype(BF16)
    before = _dot(onehot.astype(BF16), earlier) + run_ref[...]
    row = lax.broadcasted_iota(jnp.int32, idx_ref.shape, 0)
    idx_o = jnp.zeros(idx_ref.shape, F32)
    w_o = jnp.zeros(idx_ref.shape, F32)
    rank_o = jnp.zeros(idx_ref.shape, F32)
    for j, (first, hit) in enumerate(picks):
        rank_j = jnp.sum(jnp.where(hit, before, 0.0), axis=0, keepdims=True)
        idx_o = jnp.where(row == j, first, idx_o)
        w_o = jnp.where(row == j, es[j] * inv, w_o)
        rank_o = jnp.where(row == j, rank_j, rank_o)
    idx_ref[...] = idx_o
    w_ref[...] = w_o
    rank_ref[...] = rank_o
    run_ref[...] += jnp.sum(onehot, axis=1, keepdims=True)
    cnt_ref[...] = jnp.broadcast_to(run_ref[...], cnt_ref.shape)


def _router(x2, g, mod_l, shift_idx, seq, router_w, router_b):
    n, d = x2.shape
    tm = MOE_TILE
    n_experts = router_w.shape[1]
    per_batch = seq // tm
    slots = 8
    slot_out = pl.BlockSpec((slots, tm), lambda i: (0, i))
    slot_shape = jax.ShapeDtypeStruct((slots, n), F32)
    return pl.pallas_call(
        functools.partial(_router_kernel, n_experts=n_experts),
        grid=(n // tm,),
        in_specs=[
            pl.BlockSpec((tm, d), lambda i: (i, 0)),
            pl.BlockSpec((1, d), lambda i: (0, 0)),
            pl.BlockSpec((1, 1, d), lambda i: (i // per_batch, 0, shift_idx)),
            pl.BlockSpec((1, 1, d), lambda i: (i // per_batch, 0, shift_idx + 1)),
            pl.BlockSpec((n_experts, d), lambda i: (0, 0)),
            pl.BlockSpec((n_experts, 1), lambda i: (0, 0)),
        ],
        out_specs=[pl.BlockSpec((tm, d // 2), lambda i: (i, 0)), slot_out, slot_out, slot_out,
                   pl.BlockSpec((n_experts, LANES), lambda i: (0, 0))],
        out_shape=[jax.ShapeDtypeStruct((n, d // 2), jnp.uint32), slot_shape, slot_shape, slot_shape,
                   jax.ShapeDtypeStruct((n_experts, LANES), F32)],
        scratch_shapes=[pltpu.VMEM((n_experts, 1), F32)],
        compiler_params=_params("arbitrary"),
        name="moe_router",
    )(x2, g.reshape(1, d), mod_l, mod_l, router_w.T.astype(BF16), router_b.reshape(n_experts, 1))


GLU_TILE = 2 * LANES


def _experts_kernel(te_ref, nu_ref, slot_ref, nxt_ref, rows_ref, xs_ref, b1_ref, b2_ref, perm_ref, w1_hbm, w2_hbm,
                    ys_ref, w1buf, w2buf, w1p_ref, w2b_ref, sem, *, layer):
    t = pl.program_id(0)
    used = t < nu_ref[0]
    new_expert = (t == 0) | (te_ref[t] != te_ref[jnp.maximum(t - 1, 0)])
    n_col_tiles = w1p_ref.shape[1] // GLU_TILE

    def weight_copies(e, slot):
        return (pltpu.make_async_copy(w1_hbm.at[layer, e], w1buf.at[slot], sem.at[0, slot]),
                pltpu.make_async_copy(w2_hbm.at[layer, e], w2buf.at[slot], sem.at[1, slot]))

    @pl.when(used & new_expert)
    def _():
        slot = slot_ref[t]

        @pl.when(t == 0)
        def _():
            for cp in weight_copies(te_ref[t], slot):
                cp.start()

        for cp in weight_copies(te_ref[t], slot):
            cp.wait()

        @pl.when(nxt_ref[t] >= 0)
        def _():
            for cp in weight_copies(nxt_ref[t], 1 - slot):
                cp.start()

        for c in range(n_col_tiles):
            cols = slice(c * GLU_TILE, (c + 1) * GLU_TILE)
            w1p_ref[:, cols] = _dot(w1buf[slot, :, cols].astype(BF16), perm_ref[...]).astype(BF16)
        w2b_ref[...] = w2buf[slot].astype(BF16)

    def swiglu_rows(n_rows):
        lo, hi = _unpack_bf16_pair(xs_ref[0:n_rows, :])
        x = jnp.concatenate([lo, hi], axis=1)
        out = b2_ref[0]
        for j in range(n_col_tiles * LANES // EXPERT_CHUNK):
            hcols = slice(2 * j * EXPERT_CHUNK, 2 * (j + 1) * EXPERT_CHUNK)
            hh = _dot(x, w1p_ref[:, hcols]) + b1_ref[0, :, hcols]
            acts = []
            for c in range(2 * EXPERT_CHUNK // GLU_TILE):
                x_glu = jnp.minimum(hh[:, c * GLU_TILE:c * GLU_TILE + LANES], SWIGLU_LIMIT)
                x_lin = jnp.clip(hh[:, c * GLU_TILE + LANES:(c + 1) * GLU_TILE], -SWIGLU_LIMIT, SWIGLU_LIMIT)
                acts.append((x_glu * _sigmoid(SWIGLU_ALPHA * x_glu) * (x_lin + 1.0)).astype(BF16))
            out = out + _dot(jnp.concatenate(acts, axis=1), w2b_ref[j * EXPERT_CHUNK:(j + 1) * EXPERT_CHUNK, :])
        half = out.shape[1] // 2
        ys_ref[0:n_rows, :] = _pack_bf16_pair(out[:, :half], out[:, half:])
        if n_rows < ys_ref.shape[0]:
            ys_ref[n_rows:, :] = jnp.zeros((ys_ref.shape[0] - n_rows, ys_ref.shape[1]), ys_ref.dtype)

    tile_rows = xs_ref.shape[0]
    pl.when(used & (rows_ref[t] > tile_rows // 2))(lambda: swiglu_rows(tile_rows))
    pl.when(used & (rows_ref[t] <= tile_rows // 2))(lambda: swiglu_rows(tile_rows // 2))

    @pl.when(t >= nu_ref[0])
    def _():
        ys_ref[...] = jnp.zeros_like(ys_ref)


def _experts(xs, tile_expert, n_used, slot, nxt, tile_rows, w1_all, b1, w2_all, b2, layer):
    n_rows, dw = xs.shape
    tm = EXPERT_TILE
    _, n_exp, d, de2 = w1_all.shape
    src = np.arange(GLU_TILE)
    dst = np.where(src % 2 == 0, src // 2, LANES + src // 2)
    perm = jnp.asarray(dst[:, None] == np.arange(GLU_TILE)[None, :], BF16)
    b1p = b1.reshape(n_exp, de2 // GLU_TILE, LANES, 2).transpose(0, 1, 3, 2).reshape(n_exp, 1, de2)
    row = lambda t, te, nu, sl, nx, nr: (jnp.maximum(jnp.minimum(t, nu[0] - 1), 0), 0)
    exp3 = lambda t, te, nu, sl, nx, nr: (te[t], 0, 0)
    grid_spec = pltpu.PrefetchScalarGridSpec(
        num_scalar_prefetch=5,
        grid=(n_rows // tm,),
        in_specs=[
            pl.BlockSpec((tm, dw), row),
            pl.BlockSpec((1, 1, de2), exp3),
            pl.BlockSpec((1, 1, d), exp3),
            pl.BlockSpec((GLU_TILE, GLU_TILE), lambda t, te, nu, sl, nx, nr: (0, 0)),
            pl.BlockSpec(memory_space=pl.ANY),
            pl.BlockSpec(memory_space=pl.ANY),
        ],
        out_specs=pl.BlockSpec((tm, d // 2), lambda t, te, nu, sl, nx, nr: (t, 0)),
        scratch_shapes=[pltpu.VMEM((2, d, de2), F32), pltpu.VMEM((2, de2 // 2, d), F32),
                        pltpu.VMEM((d, de2), BF16), pltpu.VMEM((de2 // 2, d), BF16),
                        pltpu.SemaphoreType.DMA((2, 2))],
    )
    return pl.pallas_call(
        functools.partial(_experts_kernel, layer=layer),
        grid_spec=grid_spec,
        out_shape=jax.ShapeDtypeStruct((n_rows, d // 2), jnp.uint32),
        compiler_params=_params("arbitrary"),
        name="moe_experts",
    )(tile_expert, n_used, slot, nxt, tile_rows, xs, b1p, b2.reshape(n_exp, 1, d), perm, w1_all, w2_all)


SC_WINDOW = 128


def _sc_worker_rows(n_rows):
    sc = plsc.get_sparse_core_info()
    per_worker = n_rows // (sc.num_cores * sc.num_subcores)
    worker = lax.axis_index("subcore") * sc.num_cores + lax.axis_index("core")
    return worker * per_worker, per_worker


def _sc_mesh():
    return plsc.VectorSubcoreMesh(core_axis_name="core", subcore_axis_name="subcore")


def _sc_gather_rows(table, idx):
    n_idx, width = idx.shape[0], table.shape[1]

    @functools.partial(
        pl.kernel, out_type=jax.ShapeDtypeStruct((n_idx, width), table.dtype), mesh=_sc_mesh(),
        scratch_types=[pltpu.VMEM((SC_WINDOW,), jnp.int32), pltpu.VMEM((SC_WINDOW, width), table.dtype)],
        name="sc_gather_rows")
    def gather(table_hbm, idx_hbm, out_hbm, idx_v, rows_v):
        first, count = _sc_worker_rows(n_idx)

        @pl.loop(0, count // SC_WINDOW)
        def _(c):
            rows = pl.ds(first + c * SC_WINDOW, SC_WINDOW)
            pltpu.sync_copy(idx_hbm.at[rows], idx_v)
            pltpu.sync_copy(table_hbm.at[idx_v], rows_v)
            pltpu.sync_copy(rows_v, out_hbm.at[rows])

    return gather(table, idx)


def _sc_scatter_rows(rows, pos, n_out):
    n, width = rows.shape
    k = pos.shape[0]

    @functools.partial(
        pl.kernel, out_type=jax.ShapeDtypeStruct((n_out, width), rows.dtype), mesh=_sc_mesh(),
        scratch_types=[pltpu.VMEM((k, SC_WINDOW), jnp.int32), pltpu.VMEM((SC_WINDOW, width), rows.dtype)],
        name="sc_scatter_rows")
    def scatter(rows_hbm, pos_hbm, out_hbm, pos_v, rows_v):
        first, count = _sc_worker_rows(n)

        @pl.loop(0, count // SC_WINDOW)
        def _(c):
            src = pl.ds(first + c * SC_WINDOW, SC_WINDOW)
            pltpu.sync_copy(pos_hbm.at[:, src], pos_v)
            pltpu.sync_copy(rows_hbm.at[src], rows_v)
            for j in range(k):
                pltpu.sync_copy(rows_v, out_hbm.at[pos_v.at[j]])

    return scatter(rows, pos)


def _combine_dense_kernel(x_ref, gate_ref, w_ref, fg_ref, *rest, final_norm):
    y_refs, o_ref = rest[:TOP_K], rest[TOP_K]
    w = w_ref[...].T
    y_lo = y_hi = None
    for j in range(TOP_K):
        lo, hi = _unpack_bf16_pair(y_refs[j][...])
        wj = w[:, j:j + 1]
        y_lo = wj * lo if j == 0 else y_lo + wj * lo
        y_hi = wj * hi if j == 0 else y_hi + wj * hi
    out = x_ref[...] + gate_ref[0] * jnp.concatenate([y_lo, y_hi], axis=1)
    if final_norm:
        out = out * lax.rsqrt(jnp.mean(out * out, axis=-1, keepdims=True) + NORM_EPS) * fg_ref[...]
    o_ref[...] = out


def _combine_dense(x2, mod_l, gate_idx, seq, top_w, y4, final_g, final_norm, tm=512):
    n, d = x2.shape
    per_batch = seq // tm
    blocks = n // tm
    slot_spec = lambda j: pl.BlockSpec((tm, d // 2), lambda i: (j * blocks + i, 0))
    return pl.pallas_call(
        functools.partial(_combine_dense_kernel, final_norm=final_norm),
        grid=(blocks,),
        in_specs=[
            pl.BlockSpec((tm, d), lambda i: (i, 0)),
            pl.BlockSpec((1, 1, d), lambda i: (i // per_batch, 0, gate_idx)),
            pl.BlockSpec((top_w.shape[0], tm), lambda i: (0, i)),
            pl.BlockSpec((1, d), lambda i: (0, 0)),
        ] + [slot_spec(j) for j in range(TOP_K)],
        out_specs=pl.BlockSpec((tm, d), lambda i: (i, 0)),
        out_shape=jax.ShapeDtypeStruct((n, d), F32),
        compiler_params=_params("parallel"),
        name="moe_combine",
    )(x2, mod_l, top_w, final_g.reshape(1, d), *([y4] * TOP_K))


def _moe_ffn(x2, norm_g, mod_l, seq, router_w, router_b, w1_all, b1, w2_all, b2, layer, final_g, final_norm):
    n, d = x2.shape
    n_exp = router_w.shape[1]
    tile = EXPERT_TILE
    hp, idx_f, top_w, rank_f, counts = _router(x2, norm_g, mod_l, 3, seq, router_w, router_b)
    counts = counts[:, 0].astype(jnp.int32)
    padded = (counts + tile - 1) // tile * tile
    ends = jnp.cumsum(padded)
    starts = ends - padded
    idx = idx_f[:TOP_K].astype(jnp.int32)
    expert_ids = jnp.arange(n_exp, dtype=jnp.int32)
    start_of = sum(jnp.where(idx == e, starts[e], 0) for e in range(n_exp))
    pos_slot_major = start_of + rank_f[:TOP_K].astype(jnp.int32)
    n_rows = n * TOP_K + n_exp * tile
    n_tiles = n_rows // tile
    tile_start = jnp.arange(n_tiles, dtype=jnp.int32) * tile
    tile_expert = jnp.sum((ends[None, :] <= tile_start[:, None]).astype(jnp.int32), axis=1)
    tile_expert = jnp.minimum(tile_expert, n_exp - 1)
    n_used = (ends[-1:] // tile).astype(jnp.int32)
    nonempty = counts > 0
    ordinal = jnp.cumsum(nonempty.astype(jnp.int32)) - 1
    later = nonempty[None, :] & (expert_ids[None, :] > expert_ids[:, None])
    nxt_e = jnp.min(jnp.where(later, expert_ids[None, :], n_exp), axis=1)
    nxt_e = jnp.where(nxt_e == n_exp, -1, nxt_e)
    slot = (ordinal[tile_expert] % 2).astype(jnp.int32)
    nxt = nxt_e[tile_expert].astype(jnp.int32)
    xs = _sc_scatter_rows(hp, pos_slot_major, n_rows)
    filled = jnp.clip((starts + counts)[tile_expert] - tile_start, 0, tile).astype(jnp.int32)
    ys = _experts(xs, tile_expert, n_used, slot, nxt, filled, w1_all, b1, w2_all, b2, layer)
    y4 = _sc_gather_rows(ys, pos_slot_major.reshape(-1))
    return _combine_dense(x2, mod_l, 5, seq, top_w, y4, final_g, final_norm)


def kernel(x, c, ada_w, ada_b, norm_mix_g, norm_ffn_g, router_w, router_b, moe_w1, moe_b1, moe_w2, moe_b2, ab_w_in, ab_w_out, hgrn_lb, hgrn_norm_g, rwkv_mu, rwkv_w0, rwkv_w2, rwkv_a0, rwkv_a2, rwkv_g2, rwkv_k_k, rwkv_k_a, rwkv_r_k, rwkv_ln_g, rwkv_ln_b, attn_w_in, attn_w_out, attn_lambda, attn_subln_g, rel_bias_table, final_norm_g):
    batch, seq, d = x.shape
    n = batch * seq
    depth = ada_w.shape[0]
    x2 = x.reshape(n, d)
    mod = _adaln(c, ada_w, ada_b)
    lb_all = jnp.cumsum(jax.nn.softmax(hgrn_lb.astype(F32), axis=1), axis=1)
    for layer in range(depth):
        mod_l = mod[layer].reshape(batch, 1, 6 * d)
        j = layer // 2
        if layer % 2 == 0:
            a_cols = 5 * (d // 2)
            w_in = ab_w_in[j].astype(BF16)
            p_a, p_b = _normmod_proj(x2, norm_mix_g[layer], mod_l, 0, seq,
                                     [w_in[:, :a_cols], w_in[:, a_cols:]], [F32, F32])
            y_a = _hgrn2(p_a, lb_all[:, j], hgrn_norm_g[j], batch, seq)
            y_b = _rwkv7(p_b, rwkv_mu[j], rwkv_w0[j], rwkv_w2[j], rwkv_a0[j], rwkv_a2[j], rwkv_g2[j],
                         rwkv_k_k[j], rwkv_k_a[j], rwkv_r_k[j], rwkv_ln_g[j], rwkv_ln_b[j], batch, seq)
            w_out = ab_w_out[j].astype(BF16)
            x2 = _outproj_residual(x2, mod_l, 2, seq, [y_a, y_b], [w_out[:d // 2], w_out[d // 2:]])
        else:
            (qkv,) = _normmod_proj(x2, norm_mix_g[layer], mod_l, 0, seq, [attn_w_in[j].astype(BF16)], [BF16])
            lam = attn_lambda[j].astype(F32)
            lam_init = 0.8 - 0.6 * math.exp(-0.3 * layer)
            lam_full = jnp.exp(jnp.sum(lam[0] * lam[1])) - jnp.exp(jnp.sum(lam[2] * lam[3])) + lam_init
            o = _diff_attention(qkv, rel_bias_table, lam_full, attn_subln_g[j], lam_init, batch, seq)
            x2 = _outproj_residual(x2, mod_l, 2, seq, [o], [attn_w_out[j].astype(BF16)])
        x2 = _moe_ffn(x2, norm_ffn_g[layer], mod_l, seq, router_w[layer], router_b[layer], moe_w1,
                      moe_b1[layer], moe_w2, moe_b2[layer], layer, final_norm_g, layer == depth - 1)
    return x2.reshape(batch, seq, d)
```

```python
import functools
import math

import jax
import jax.numpy as jnp
import numpy as np
from jax import lax
from jax.experimental import pallas as pl
from jax.experimental.pallas import tpu as pltpu
from jax.experimental.pallas import tpu_sc as plsc

F32 = jnp.float32
BF16 = jnp.bfloat16
HIGHEST = lax.Precision.HIGHEST

NORM_EPS = 1e-6

A_HEAD_DIM = 128
A_CHUNK = 32
A_GROUP = 512
A_HEADS_PER_STEP = 2
B_HEAD_DIM = 64
B_CHUNK = 64
B_STEPS = 2
B_LN_EPS = 1e-5 * B_HEAD_DIM
C_HEAD_DIM = 64
REL_BUCKETS = 32
REL_MAX_DISTANCE = 128
TOP_K = 4
SWIGLU_LIMIT = 7.0
SWIGLU_ALPHA = 1.702

V7X_VMEM_BYTES = 64 * 1024 * 1024
VMEM_LIMIT = V7X_VMEM_BYTES - 8 * 1024 * 1024
LANES = 128


def _params(*sem):
    return pltpu.CompilerParams(dimension_semantics=sem, vmem_limit_bytes=VMEM_LIMIT)


def _sigmoid(x):
    return 1.0 / (1.0 + jnp.exp(-x))


def _silu(x):
    return x * _sigmoid(x)


def _dot(a, b):
    return jnp.dot(a, b, preferred_element_type=F32)


def _dot_nt(a, b):
    return lax.dot_general(a, b, (((1,), (1,)), ((), ())), preferred_element_type=F32)


def _dot_tn(a, b):
    return lax.dot_general(a, b, (((0,), (0,)), ((), ())), preferred_element_type=F32)


def _adaln_kernel(c_ref, w_ref, b_ref, o_ref):
    cond = _silu(c_ref[...])
    o_ref[0] = jnp.dot(cond, w_ref[0], precision=HIGHEST, preferred_element_type=F32) + b_ref[0]


def _adaln(c, ada_w, ada_b):
    n_layers, d, n_out = ada_w.shape
    batch = c.shape[0]
    tn = 1536
    return pl.pallas_call(
        _adaln_kernel,
        grid=(n_layers, n_out // tn),
        in_specs=[
            pl.BlockSpec((batch, d), lambda l, j: (0, 0)),
            pl.BlockSpec((1, d, tn), lambda l, j: (l, 0, j)),
            pl.BlockSpec((1, 1, tn), lambda l, j: (l, 0, j)),
        ],
        out_specs=pl.BlockSpec((1, batch, tn), lambda l, j: (l, 0, j)),
        out_shape=jax.ShapeDtypeStruct((n_layers, batch, n_out), F32),
        compiler_params=_params("parallel", "parallel"),
        name="adaln",
    )(c, ada_w, ada_b.reshape(n_layers, 1, n_out))


def _modulated_norm(x, g, shift, scale):
    y = x * lax.rsqrt(jnp.mean(x * x, axis=-1, keepdims=True) + NORM_EPS)
    return (y * g) * (1.0 + scale) + shift


def _normmod_proj_kernel(x_ref, g_ref, sh_ref, sc_ref, *rest, n_w):
    w_refs, o_refs = rest[:n_w], rest[n_w:]
    h = _modulated_norm(x_ref[...], g_ref[...], sh_ref[0], sc_ref[0]).astype(BF16)
    for w_ref, o_ref in zip(w_refs, o_refs):
        o_ref[...] = _dot(h, w_ref[...]).astype(o_ref.dtype)


def _normmod_proj(x2, g, mod_l, shift_idx, seq, weights, out_dtypes, tm=512):
    n, d = x2.shape
    per_batch = seq // tm
    in_specs = [
        pl.BlockSpec((tm, d), lambda i: (i, 0)),
        pl.BlockSpec((1, d), lambda i: (0, 0)),
        pl.BlockSpec((1, 1, d), lambda i: (i // per_batch, 0, shift_idx)),
        pl.BlockSpec((1, 1, d), lambda i: (i // per_batch, 0, shift_idx + 1)),
    ]
    in_specs += [pl.BlockSpec(w.shape, lambda i: (0, 0)) for w in weights]
    out_specs = [pl.BlockSpec((tm, w.shape[1]), lambda i: (i, 0)) for w in weights]
    out_shape = [jax.ShapeDtypeStruct((n, w.shape[1]), dt) for w, dt in zip(weights, out_dtypes)]
    return pl.pallas_call(
        functools.partial(_normmod_proj_kernel, n_w=len(weights)),
        grid=(n // tm,),
        in_specs=in_specs,
        out_specs=out_specs,
        out_shape=out_shape,
        compiler_params=_params("parallel"),
        name="normmod_proj",
    )(x2, g.reshape(1, d), mod_l, mod_l, *weights)


def _outproj_kernel(x_ref, gate_ref, *rest, n_y):
    y_refs, w_refs, o_ref = rest[:n_y], rest[n_y:2 * n_y], rest[2 * n_y]
    acc = _dot(y_refs[0][...], w_refs[0][...])
    for y_ref, w_ref in zip(y_refs[1:], w_refs[1:]):
        acc += _dot(y_ref[...], w_ref[...])
    o_ref[...] = x_ref[...] + gate_ref[0] * acc


def _outproj_residual(x2, mod_l, gate_idx, seq, ys, ws, tm=1024):
    n, d = x2.shape
    per_batch = seq // tm
    in_specs = [
        pl.BlockSpec((tm, d), lambda i: (i, 0)),
        pl.BlockSpec((1, 1, d), lambda i: (i // per_batch, 0, gate_idx)),
    ]
    in_specs += [pl.BlockSpec((tm, y.shape[1]), lambda i: (i, 0)) for y in ys]
    in_specs += [pl.BlockSpec(w.shape, lambda i: (0, 0)) for w in ws]
    return pl.pallas_call(
        functools.partial(_outproj_kernel, n_y=len(ys)),
        grid=(n // tm,),
        in_specs=in_specs,
        out_specs=pl.BlockSpec((tm, d), lambda i: (i, 0)),
        out_shape=jax.ShapeDtypeStruct((n, d), F32),
        compiler_params=_params("parallel"),
        name="outproj_residual",
    )(x2, mod_l, *ys, *ws)


def _chunk_cumsum(x, chunk, reverse):
    rows = x.shape[0]
    pos = lax.broadcasted_iota(jnp.int32, x.shape, 0) % chunk
    s = 1
    while s < chunk:
        if reverse:
            x = x + jnp.where(pos < chunk - s, pltpu.roll(x, rows - s, axis=0), 0.0)
        else:
            x = x + jnp.where(pos >= s, pltpu.roll(x, s, axis=0), 0.0)
        s *= 2
    return x


def _hgrn_groups(slabs):
    g_rows, dk = slabs[0][0].shape
    n_chunks = g_rows // A_CHUNK
    ti = lax.broadcasted_iota(jnp.int32, (A_CHUNK, A_CHUNK), 0)
    si = lax.broadcasted_iota(jnp.int32, (A_CHUNK, A_CHUNK), 1)
    chunk_rows = [slice(c * A_CHUNK, (c + 1) * A_CHUNK) for c in range(n_chunks)]
    prep = []
    for q, f, v, lb, st, reverse in slabs:
        fg = lb + (1.0 - lb) * _sigmoid(f)
        k = 1.0 - fg
        b = _chunk_cumsum(jnp.log(fg), A_CHUNK, reverse)
        b3 = b.reshape(n_chunks, A_CHUNK, dk)
        edge = b3[:, 0:1, :] if reverse else b3[:, A_CHUNK - 1:A_CHUNK, :]
        prep.append(dict(
            q_in=(q * jnp.exp(b)).astype(BF16), k_in=(k * jnp.exp(-b)).astype(BF16),
            k_st=(k.reshape(n_chunks, A_CHUNK, dk) * jnp.exp(edge - b3)).astype(BF16),
            decay=jnp.exp(edge), vb=v.astype(BF16), st=st, reverse=reverse,
            mask=(si >= ti) if reverse else (si <= ti)))
    for p in prep:
        p["scores"] = [_dot_nt(p["q_in"][r], p["k_in"][r]) for r in chunk_rows]
        p["dstate"] = [_dot_tn(p["vb"][r], p["k_st"][c]) for c, r in enumerate(chunk_rows)]
    for p in prep:
        p["intra"] = [_dot(jnp.where(p["mask"], s, 0.0).astype(BF16), p["vb"][r])
                      for s, r in zip(p["scores"], chunk_rows)]
    results = []
    for p in prep:
        st = p["st"]
        outs = [None] * n_chunks
        for c in (range(n_chunks - 1, -1, -1) if p["reverse"] else range(n_chunks)):
            outs[c] = p["intra"][c] + _dot_nt(p["q_in"][chunk_rows[c]], st.astype(BF16))
            st = st * p["decay"][c] + p["dstate"][c]
        results.append((jnp.concatenate(outs, axis=0), st))
    return results


def _hgrn_kernel(q_ref, ff_ref, fb_ref, i_ref, g_ref, lb_ref, ng_ref, o_ref, of_ref, ob_ref):
    seq = q_ref.shape[0]
    dk = A_HEAD_DIM
    n_heads = q_ref.shape[1] // dk
    n_groups = seq // A_GROUP
    head_cols = [slice(h * dk, (h + 1) * dk) for h in range(n_heads)]

    def body(j, states):
        rf = pl.ds(pl.multiple_of(j * A_GROUP, A_GROUP), A_GROUP)
        rb = pl.ds(pl.multiple_of((n_groups - 1 - j) * A_GROUP, A_GROUP), A_GROUP)
        slabs = []
        for h, cols in enumerate(head_cols):
            slabs.append((_silu(q_ref[rf, cols]), ff_ref[rf, cols], i_ref[rf, cols], lb_ref[0:1, cols],
                          states[2 * h], False))
            slabs.append((_silu(q_ref[rb, cols]), fb_ref[rb, cols], i_ref[rb, cols], lb_ref[1:2, cols],
                          states[2 * h + 1], True))
        results = _hgrn_groups(slabs)
        for h, cols in enumerate(head_cols):
            of_ref[rf, cols] = results[2 * h][0]
            ob_ref[rb, cols] = results[2 * h + 1][0]
        return tuple(r[1] for r in results)

    zero = jnp.zeros((dk, dk), F32)
    lax.fori_loop(0, n_groups, body, (zero,) * (2 * n_heads))

    def finish(j, carry):
        r = pl.ds(pl.multiple_of(j * A_GROUP, A_GROUP), A_GROUP)
        for cols in head_cols:
            o = of_ref[r, cols] + ob_ref[r, cols]
            o = o * lax.rsqrt(jnp.mean(o * o, axis=-1, keepdims=True) + NORM_EPS) * ng_ref[...]
            o_ref[r, cols] = (o * _silu(g_ref[r, cols])).astype(o_ref.dtype)
        return carry

    lax.fori_loop(0, n_groups, finish, 0)


def _hgrn2(p_a, lb, norm_g, batch, seq):
    n = p_a.shape[0]
    width = p_a.shape[1] // 5
    cols = A_HEADS_PER_STEP * A_HEAD_DIM
    steps = width // cols
    sect = lambda s: pl.BlockSpec((seq, cols), lambda b, h, s=s: (b, s * steps + h))
    return pl.pallas_call(
        _hgrn_kernel,
        grid=(batch, steps),
        in_specs=[sect(0), sect(1), sect(2), sect(3), sect(4),
                  pl.BlockSpec((2, cols), lambda b, h: (0, h)),
                  pl.BlockSpec((1, A_HEAD_DIM), lambda b, h: (0, 0))],
        out_specs=pl.BlockSpec((seq, cols), lambda b, h: (b, h)),
        out_shape=jax.ShapeDtypeStruct((n, width), BF16),
        scratch_shapes=[pltpu.VMEM((seq, cols), F32), pltpu.VMEM((seq, cols), F32)],
        compiler_params=_params("parallel", "parallel"),
        name="hgrn2",
    )(p_a, p_a, p_a, p_a, p_a, lb, norm_g.reshape(1, A_HEAD_DIM))


def _softplus(z):
    return jnp.maximum(z, 0.0) + jnp.log(1.0 + jnp.exp(-jnp.abs(z)))


def _rwkv_prep_kernel(p_ref, prev_ref, next_ref, mu_ref, w0_ref, w2_ref, a0_ref, a2_ref, g2_ref,
                      kk_ref, ka_ref, rk_ref, ones_ref, tri_ref,
                      r_out, k_out, v_out, kkn_out, b_out, cumf_out, cumb_out, g_out, bonus_out):
    i = pl.program_id(1)
    last = pl.num_programs(1) - 1
    ts = p_ref.shape[0]
    width = a0_ref.shape[1]
    heads = width // B_HEAD_DIM
    p = p_ref[...]
    prev_row = jnp.where(i == 0, 0.0, prev_ref[7:8, :])
    next_row = jnp.where(i == last, 0.0, next_ref[0:1, :])
    row = lax.broadcasted_iota(jnp.int32, (ts, 1), 0)
    up = jnp.where(row == 0, prev_row, pltpu.roll(p, 1, axis=0))
    dn = jnp.where(row == ts - 1, next_row, pltpu.roll(p, ts - 1, axis=0))
    p = p + mu_ref[...] * (0.5 * (up + dn) - p)

    r = p[:, 0:width]
    k = p[:, width:2 * width]
    v = p[:, 2 * width:3 * width]
    o = 3 * width
    wlo_f = p[:, o:o + 64]
    wlo_b = p[:, o + 64:o + 128]
    alo = p[:, o + 128:o + 192]
    glo = p[:, o + 192:o + 320]

    def log_decay(wlo, d):
        z = w0_ref[d:d + 1, :] + _dot(jnp.tanh(wlo).astype(BF16), w2_ref[d])
        w = -_softplus(-z) - 0.5
        return -jnp.exp(w)

    def split_dot(sel, t, right):
        hi = t.astype(BF16)
        lo = (t - hi.astype(F32)).astype(BF16)
        return (_dot(hi, sel) + _dot(lo, sel)) if right else (_dot(sel, hi) + _dot(sel, lo))

    cum_f = split_dot(tri_ref[0], log_decay(wlo_f, 0), False)
    cum_b = split_dot(tri_ref[1], log_decay(wlo_b, 1), False)
    a = _sigmoid(a0_ref[...] + _dot(alo.astype(BF16), a2_ref[...]))
    g = _dot(_sigmoid(glo).astype(BF16), g2_ref[...])
    kk = k * kk_ref[...]
    head_sum = lambda t: split_dot(ones_ref[...], t, True)

    kk_n = kk / jnp.maximum(jnp.sqrt(head_sum(kk * kk)), 1e-12)
    k_mod = k * (1.0 + (a - 1.0) * ka_ref[...])
    bonus = head_sum(r * k_mod * rk_ref[...]) * v
    g_out[...] = g
    bonus_out[...] = bonus
    for h in range(heads):
        sl = slice(h * B_HEAD_DIM, (h + 1) * B_HEAD_DIM)
        r_out[0, h] = r[:, sl]
        k_out[0, h] = k_mod[:, sl]
        v_out[0, h] = v[:, sl]
        kkn_out[0, h] = kk_n[:, sl]
        b_out[0, h] = (kk_n * a)[:, sl]
        cumf_out[0, h] = cum_f[:, sl]
        cumb_out[0, h] = cum_b[:, sl]


def _rwkv_prep(p_b, mu, w0, w2, a0, a2, g2, k_k, k_a, r_k, batch, seq, ts=512):
    n, cols = p_b.shape
    width = a0.shape[0]
    heads = width // B_HEAD_DIM
    nblk = seq // ts
    rows8 = ts // 8
    head_id = np.arange(width) // B_HEAD_DIM
    ones = jnp.asarray(head_id[:, None] == head_id[None, :], BF16)
    full = lambda a: pl.BlockSpec(a.shape, lambda b, i: (0,) * a.ndim)
    vec = lambda a: a.reshape(1, -1)
    row = np.arange(ts)
    same_chunk = (row[:, None] // B_CHUNK) == (row[None, :] // B_CHUNK)
    tri = jnp.asarray(np.stack([same_chunk & (row[None, :] <= row[:, None]),
                                same_chunk & (row[None, :] >= row[:, None])]), BF16)
    args = [vec(mu), w0, w2.astype(BF16), vec(a0), a2.astype(BF16), g2.astype(BF16),
            vec(k_k), vec(k_a), vec(r_k), ones, tri]
    hm = pl.BlockSpec((1, heads, ts, B_HEAD_DIM), lambda b, i: (b, 0, i, 0))
    tokm = pl.BlockSpec((ts, width), lambda b, i: (b * nblk + i, 0))
    hm_shape = jax.ShapeDtypeStruct((batch, heads, seq, B_HEAD_DIM), F32)
    tok_shape = jax.ShapeDtypeStruct((n, width), F32)
    return pl.pallas_call(
        _rwkv_prep_kernel,
        grid=(batch, nblk),
        in_specs=[
            pl.BlockSpec((ts, cols), lambda b, i: (b * nblk + i, 0)),
            pl.BlockSpec((8, cols), lambda b, i: (jnp.maximum((b * nblk + i) * rows8 - 1, 0), 0)),
            pl.BlockSpec((8, cols), lambda b, i: (jnp.minimum((b * nblk + i + 1) * rows8, n // 8 - 1), 0)),
        ] + [full(a) for a in args],
        out_specs=[hm] * 7 + [tokm] * 2,
        out_shape=[hm_shape] * 7 + [tok_shape] * 2,
        compiler_params=_params("parallel", "parallel"),
        name="rwkv_prep",
    )(p_b, p_b, p_b, *args)


def _rwkv_chunks(chains):
    c, d = chains[0][0].shape
    ti = lax.broadcasted_iota(jnp.int32, (c, c), 0)
    si = lax.broadcasted_iota(jnp.int32, (c, c), 1)
    eye = jnp.where(ti == si, 1.0, 0.0)
    masks = {rev: (((si > ti), (si >= ti)) if rev else ((si < ti), (si <= ti))) for rev in (False, True)}
    t2 = lax.broadcasted_iota(jnp.int32, (c, 2 * c), 0)
    s2 = lax.broadcasted_iota(jnp.int32, (c, 2 * c), 1)
    s2 = jnp.where(s2 >= c, s2 - c, s2)
    incl2 = {False: s2 <= t2, True: s2 >= t2}

    row = lax.broadcasted_iota(jnp.int32, (c, d), 0)
    st = []
    for r, k, v, kk, b, cum, rev in chains:
        before = (jnp.where(row == c - 1, 0.0, pltpu.roll(cum, c - 1, axis=0)) if rev
                  else jnp.where(row == 0, 0.0, pltpu.roll(cum, 1, axis=0)))
        grow = jnp.exp(-cum)
        kt = (kk * jnp.exp(before)).astype(BF16)
        rt = r * jnp.exp(cum)
        b_ = (b * grow).astype(BF16)
        kb = jnp.concatenate([(k * grow).astype(BF16), b_], axis=0)
        st.append(dict(vb=v.astype(BF16), kt=kt, rt=rt, b_=b_, kb=kb,
                       decay=jnp.exp(cum[0:1, :] if rev else cum[c - 1:c, :]), rev=rev,
                       lhs=jnp.concatenate([kt, rt.astype(BF16)], axis=0)))
    for s in st:
        s["big"] = _dot_nt(s["lhs"], s["kb"])
    for s in st:
        strict, incl = masks[s["rev"]]
        big = s["big"]
        a_ab = jnp.where(strict, big[:c, c:], 0.0)
        s["a_ak"] = jnp.where(strict, big[:c, :c], 0.0).astype(BF16)
        s["p"] = jnp.where(incl2[s["rev"]], big[c:], 0.0).astype(BF16)
        s["m"] = eye - a_ab
        s["ab"] = a_ab.astype(BF16)
    for s in st:
        s["aj"] = _dot(s["ab"], s["ab"])
        s["av"] = _dot(s["a_ak"], s["vb"])
    span = 2
    while span < c:
        span *= 2
        for s in st:
            ajb = s["aj"].astype(BF16)
            if span < c:
                both = _dot(jnp.concatenate([ajb, s["m"].astype(BF16)], axis=0), ajb)
                s["aj"], s["m"] = both[:c], s["m"] + both[c:]
            else:
                s["m"] = s["m"] + _dot(s["m"].astype(BF16), ajb)
    for s in st:
        wu = _dot(s["m"].astype(BF16), jnp.concatenate([s["kt"], s["av"].astype(BF16)], axis=1))
        s["nwu"] = -wu.astype(BF16)
    for s in st:
        nw, nu0, vb = s["nwu"][:, :d], s["nwu"][:, d:], s["vb"]
        rhs = jnp.concatenate([jnp.concatenate([vb, jnp.zeros_like(vb)], axis=1),
                               jnp.concatenate([nu0, nw], axis=1)], axis=0)
        s["yq"] = _dot(s["p"], rhs)
        s["h_t"] = _dot_tn(jnp.concatenate([vb, nu0], axis=0), s["kb"])
        s["nbw"] = _dot_tn(s["b_"], nw)
    return [dict(q=(s["rt"] + s["yq"][:, d:]).astype(BF16), y1=s["yq"][:, :d], nbw=s["nbw"].astype(BF16),
                 h_t=s["h_t"], decay=s["decay"]) for s in st]


def _rwkv_advance(states, steps):
    ttbs = [tt.astype(BF16) for tt in states]
    ys = [_dot_nt(s["q"], ttb) + s["y1"] for s, ttb in zip(steps, ttbs)]
    new = [(tt + _dot_nt(ttb, s["nbw"]) + s["h_t"]) * s["decay"] for tt, ttb, s in zip(states, ttbs, steps)]
    return ys, new


def _rwkv_scan_kernel(rf, kf, vf, kkf, bf, lwf, rb, kb, vb, kkb, bb, lwb, yf_ref, yb_ref, tf_ref, tb_ref):
    heads, ts = rf.shape[1], rf.shape[2]
    n_chunks = ts // B_CHUNK

    @pl.when(pl.program_id(1) == 0)
    def _():
        tf_ref[...] = jnp.zeros_like(tf_ref)
        tb_ref[...] = jnp.zeros_like(tb_ref)

    def body(j, carry):
        rows_f = [pl.ds(pl.multiple_of((j * B_STEPS + s) * B_CHUNK, B_CHUNK), B_CHUNK) for s in range(B_STEPS)]
        rows_b = [pl.ds(pl.multiple_of((n_chunks - 1 - j * B_STEPS - s) * B_CHUNK, B_CHUNK), B_CHUNK)
                  for s in range(B_STEPS)]
        chains = []
        for s in range(B_STEPS):
            for h in range(heads):
                chains.append([ref[0, h, rows_f[s], :] for ref in (rf, kf, vf, kkf, bf, lwf)] + [False])
                chains.append([ref[0, h, rows_b[s], :] for ref in (rb, kb, vb, kkb, bb, lwb)] + [True])
        solved = _rwkv_chunks(chains)
        states = []
        for h in range(heads):
            states += [tf_ref[h], tb_ref[h]]
        for s in range(B_STEPS):
            ys, states = _rwkv_advance(states, solved[s * 2 * heads:(s + 1) * 2 * heads])
            for h in range(heads):
                yf_ref[0, h, rows_f[s], :] = ys[2 * h]
                yb_ref[0, h, rows_b[s], :] = ys[2 * h + 1]
        for h in range(heads):
            tf_ref[h], tb_ref[h] = states[2 * h], states[2 * h + 1]
        return carry

    lax.fori_loop(0, n_chunks // B_STEPS, body, 0)


def _rwkv_scan(r, k, v, kkn, b, lw_f, lw_b, ts=256):
    batch, heads, seq, d = r.shape
    nblk = seq // ts
    fwd = pl.BlockSpec((1, heads, ts, d), lambda bi, i: (bi, 0, i, 0))
    bwd = pl.BlockSpec((1, heads, ts, d), lambda bi, i: (bi, 0, nblk - 1 - i, 0))
    shape = jax.ShapeDtypeStruct(r.shape, F32)
    return pl.pallas_call(
        _rwkv_scan_kernel,
        grid=(batch, nblk),
        in_specs=[fwd] * 6 + [bwd] * 6,
        out_specs=[fwd, bwd],
        out_shape=[shape, shape],
        scratch_shapes=[pltpu.VMEM((heads, d, d), F32), pltpu.VMEM((heads, d, d), F32)],
        compiler_params=_params("parallel", "arbitrary"),
        name="rwkv_scan",
    )(r, k, v, kkn, b, lw_f, r, k, v, kkn, b, lw_b)


def _rwkv_post_kernel(yf_ref, yb_ref, g_ref, bonus_ref, lng_ref, lnb_ref, o_ref):
    heads = yf_ref.shape[1]
    outs = []
    for h in range(heads):
        y = yf_ref[0, h] + yb_ref[0, h]
        mean = jnp.mean(y, axis=-1, keepdims=True)
        var = jnp.mean(jnp.square(y - mean), axis=-1, keepdims=True)
        outs.append((y - mean) * lax.rsqrt(var + B_LN_EPS))
    yn = jnp.concatenate(outs, axis=1) * lng_ref[...] + lnb_ref[...]
    o_ref[...] = ((yn + bonus_ref[...]) * g_ref[...]).astype(o_ref.dtype)


def _rwkv_post(y_f, y_b, g, bonus, ln_g, ln_b, ts=512):
    batch, heads, seq, d = y_f.shape
    n, width = g.shape
    nblk = seq // ts
    hm = pl.BlockSpec((1, heads, ts, d), lambda b, i: (b, 0, i, 0))
    tokm = pl.BlockSpec((ts, width), lambda b, i: (b * nblk + i, 0))
    vec = pl.BlockSpec((1, width), lambda b, i: (0, 0))
    return pl.pallas_call(
        _rwkv_post_kernel,
        grid=(batch, nblk),
        in_specs=[hm, hm, tokm, tokm, vec, vec],
        out_specs=tokm,
        out_shape=jax.ShapeDtypeStruct((n, width), BF16),
        compiler_params=_params("parallel", "parallel"),
        name="rwkv_post",
    )(y_f, y_b, g, bonus, ln_g.reshape(1, width), ln_b.reshape(1, width))


def _rwkv7(p_b, mu, w0, w2, a0, a2, g2, k_k, k_a, r_k, ln_g, ln_b, batch, seq):
    r, k, v, kkn, b, lw_f, lw_b, g, bonus = _rwkv_prep(p_b, mu, w0, w2, a0, a2, g2, k_k, k_a, r_k, batch, seq)
    y_f, y_b = _rwkv_scan(r, k, v, kkn, b, lw_f, lw_b)
    return _rwkv_post(y_f, y_b, g, bonus, ln_g, ln_b)


_BUCKET_EDGES = (0, 1, 2, 3, 4, 5, 6, 7, 8, 12, 16, 23, 32, 46, 64, 91)
ATTN_TK = 128
C_HEADS_PER_STEP = 4


def _bias_band_kernel(table_ref, o_ref):
    h = pl.program_id(0)
    n_d, tq, tk = o_ref.shape[1:]
    half = REL_BUCKETS // 2
    r = lax.broadcasted_iota(jnp.int32, (tq, tk), 0)
    c = lax.broadcasted_iota(jnp.int32, (tq, tk), 1)
    for d in range(n_d):
        rel = tk * (d - 1) + c - r
        n = jnp.abs(rel)
        vals = []
        for sign in range(2):
            val = jnp.full((tq, tk), table_ref[sign * half + half - 1, h], F32)
            for bkt in range(half - 2, -1, -1):
                val = jnp.where(n < _BUCKET_EDGES[bkt + 1], table_ref[sign * half + bkt, h], val)
            vals.append(val)
        o_ref[0, d] = jnp.where(rel > 0, vals[1], vals[0])


def _bias_band(rel_table, tq):
    heads = rel_table.shape[1]
    n_d = tq // ATTN_TK + 2
    return pl.pallas_call(
        _bias_band_kernel,
        grid=(heads,),
        in_specs=[pl.BlockSpec(memory_space=pltpu.SMEM)],
        out_specs=pl.BlockSpec((1, n_d, tq, ATTN_TK), lambda h: (h, 0, 0, 0)),
        out_shape=jax.ShapeDtypeStruct((heads, n_d, tq, ATTN_TK), F32),
        compiler_params=_params("parallel"),
        name="bias_band",
    )(rel_table)


def _diff_attn_kernel(table_ref, lam_ref, q_ref, k_ref, v_ref, band_ref, g_ref, o_ref, lg_ref, vext_ref, kt_ref,
                      *, lam_init):
    i = pl.program_id(2)
    tq = q_ref.shape[0]
    dh2 = 2 * C_HEAD_DIM
    n_heads = q_ref.shape[1] // dh2
    seq = k_ref.shape[0]
    n_d = band_ref.shape[1]
    n_kt = seq // ATTN_TK
    ratio = tq // ATTN_TK
    half = REL_BUCKETS // 2
    head_cols = [slice(hh * dh2, (hh + 1) * dh2) for hh in range(n_heads)]

    @pl.when(i == 0)
    def _():
        for hh, cols in enumerate(head_cols):
            kt_ref[hh] = k_ref[:, cols].astype(F32).T.astype(BF16)
            v = v_ref[:, cols]
            vext_ref[hh] = jnp.concatenate([v, jnp.ones_like(v)], axis=1)

    lane = lax.broadcasted_iota(jnp.int32, (1, dh2), 1)
    first_band = ratio * i - 1
    col = lax.broadcasted_iota(jnp.int32, (1, seq), 1)
    for hh, cols in enumerate(head_cols):
        h = pl.program_id(1) * n_heads + hh
        q = q_ref[:, cols] * jnp.asarray(C_HEAD_DIM ** -0.5, q_ref.dtype)
        zero = jnp.zeros_like(q)
        far_row = jnp.where(col < first_band * ATTN_TK, table_ref[half - 1, h],
                            jnp.where(col >= (first_band + n_d) * ATTN_TK, table_ref[REL_BUCKETS - 1, h],
                                      -jnp.inf))
        row_max = []
        for m in range(2):
            qm = jnp.where((lane >= m * C_HEAD_DIM) & (lane < (m + 1) * C_HEAD_DIM), q, zero)
            far = _dot(qm, kt_ref[hh]) + far_row
            mx = jnp.max(far, axis=-1, keepdims=True)
            lg_ref[hh, m] = far
            for d in (0, n_d - 1) + tuple(range(1, n_d - 1)):
                kt = first_band + d
                valid = (kt >= 0) & (kt < n_kt)
                c0 = pl.multiple_of(jnp.clip(kt, 0, n_kt - 1) * ATTN_TK, ATTN_TK)
                near = _dot(qm, kt_ref[hh, :, pl.ds(c0, ATTN_TK)]) + band_ref[hh, d]
                mx = jnp.maximum(mx, jnp.where(valid, jnp.max(near, axis=-1, keepdims=True), -jnp.inf))
                lg_ref[hh, m, :, pl.ds(c0, ATTN_TK)] = near
            row_max.append(mx)
        parts = []
        for m in range(2):
            e = jnp.exp(lg_ref[hh, m] - row_max[m]).astype(BF16)
            pv = _dot(e, vext_ref[hh])
            parts.append(pv[:, :dh2] * (1.0 / pv[:, dh2:dh2 + 1]))
        o = parts[0] - lam_ref[0] * parts[1]
        o = o * lax.rsqrt(jnp.mean(o * o, axis=-1, keepdims=True) + NORM_EPS) * g_ref[...]
        o_ref[:, cols] = (o * (1.0 - lam_init)).astype(o_ref.dtype)


def _diff_attention(qkv, rel_table, lam_full, subln_g, lam_init, batch, seq, tq=256):
    n = qkv.shape[0]
    dh2 = 2 * C_HEAD_DIM
    heads = qkv.shape[1] // (3 * dh2)
    nq = seq // tq
    per = C_HEADS_PER_STEP
    groups = heads // per
    cols = per * dh2
    band = _bias_band(rel_table, tq)
    return pl.pallas_call(
        functools.partial(_diff_attn_kernel, lam_init=lam_init),
        grid=(batch, groups, nq),
        in_specs=[
            pl.BlockSpec(memory_space=pltpu.SMEM),
            pl.BlockSpec(memory_space=pltpu.SMEM),
            pl.BlockSpec((tq, cols), lambda b, h, i: (b * nq + i, h)),
            pl.BlockSpec((seq, cols), lambda b, h, i: (b, groups + h)),
            pl.BlockSpec((seq, cols), lambda b, h, i: (b, 2 * groups + h)),
            pl.BlockSpec((per,) + band.shape[1:], lambda b, h, i: (h, 0, 0, 0)),
            pl.BlockSpec((1, dh2), lambda b, h, i: (0, 0)),
        ],
        out_specs=pl.BlockSpec((tq, cols), lambda b, h, i: (b * nq + i, h)),
        out_shape=jax.ShapeDtypeStruct((n, heads * dh2), BF16),
        scratch_shapes=[pltpu.VMEM((per, 2, tq, seq), F32), pltpu.VMEM((per, seq, 2 * dh2), BF16),
                        pltpu.VMEM((per, dh2, seq), BF16)],
        compiler_params=_params("parallel", "parallel", "arbitrary"),
        name="diff_attention",
    )(rel_table, lam_full.reshape(1), qkv, qkv, qkv, band, subln_g.reshape(1, dh2))


MOE_TILE = 1024
EXPERT_TILE = 512
EXPERT_CHUNK = 512
_HI_MASK = np.uint32(0xFFFF0000)


def _pack_bf16_pair(lo, hi):
    as_bits = lambda t: lax.bitcast_convert_type(t.astype(BF16).astype(F32), jnp.uint32)
    return (as_bits(hi) & _HI_MASK) | (as_bits(lo) >> 16)


def _unpack_bf16_pair(word):
    lo = lax.bitcast_convert_type(word << 16, F32).astype(BF16)
    hi = lax.bitcast_convert_type(word & _HI_MASK, F32).astype(BF16)
    return lo, hi


def _router_kernel(x_ref, g_ref, sh_ref, sc_ref, rw_ref, rb_ref, hp_ref, idx_ref, w_ref, rank_ref, cnt_ref,
                   run_ref, *, n_experts):
    @pl.when(pl.program_id(0) == 0)
    def _():
        run_ref[...] = jnp.zeros_like(run_ref)

    tm, d = x_ref.shape
    h = _modulated_norm(x_ref[...], g_ref[...], sh_ref[0], sc_ref[0])
    hp_ref[...] = _pack_bf16_pair(h[:, :d // 2], h[:, d // 2:])
    logits = _dot_nt(rw_ref[...], h.astype(BF16)) + rb_ref[...]
    expert = lax.broadcasted_iota(jnp.int32, logits.shape, 0).astype(F32)
    neg = jnp.float32(-jnp.inf)
    picks, vals = [], []
    for _ in range(TOP_K):
        m = jnp.max(logits, axis=0, keepdims=True)
        first = jnp.min(jnp.where(logits == m, expert, float(n_experts)), axis=0, keepdims=True)
        hit = expert == first
        picks.append((first, hit))
        vals.append(m)
        logits = jnp.where(hit, neg, logits)
    es = [jnp.exp(v - vals[0]) for v in vals]
    inv = 1.0 / (es[0] + es[1] + es[2] + es[3])
    onehot = sum(jnp.where(hit, 1.0, 0.0) for _, hit in picks)
    ri = lax.broadcasted_iota(jnp.int32, (tm, tm), 0)
    ci = lax.broadcasted_iota(jnp.int32, (tm, tm), 1)
    earlier = jnp.where(ri < ci, 1.0, 0.0).astype(BF16)
    before = _dot(onehot.astype(BF16), earlier) + run_ref[...]
    row = lax.broadcasted_iota(jnp.int32, idx_ref.shape, 0)
    idx_o = jnp.zeros(idx_ref.shape, F32)
    w_o = jnp.zeros(idx_ref.shape, F32)
    rank_o = jnp.zeros(idx_ref.shape, F32)
    for j, (first, hit) in enumerate(picks):
        rank_j = jnp.sum(jnp.where(hit, before, 0.0), axis=0, keepdims=True)
        idx_o = jnp.where(row == j, first, idx_o)
        w_o = jnp.where(row == j, es[j] * inv, w_o)
        rank_o = jnp.where(row == j, rank_j, rank_o)
    idx_ref[...] = idx_o
    w_ref[...] = w_o
    rank_ref[...] = rank_o
    run_ref[...] += jnp.sum(onehot, axis=1, keepdims=True)
    cnt_ref[...] = jnp.broadcast_to(run_ref[...], cnt_ref.shape)


def _router(x2, g, mod_l, shift_idx, seq, router_w, router_b):
    n, d = x2.shape
    tm = MOE_TILE
    n_experts = router_w.shape[1]
    per_batch = seq // tm
    slots = 8
    slot_out = pl.BlockSpec((slots, tm), lambda i: (0, i))
    slot_shape = jax.ShapeDtypeStruct((slots, n), F32)
    return pl.pallas_call(
        functools.partial(_router_kernel, n_experts=n_experts),
        grid=(n // tm,),
        in_specs=[
            pl.BlockSpec((tm, d), lambda i: (i, 0)),
            pl.BlockSpec((1, d), lambda i: (0, 0)),
            pl.BlockSpec((1, 1, d), lambda i: (i // per_batch, 0, shift_idx)),
            pl.BlockSpec((1, 1, d), lambda i: (i // per_batch, 0, shift_idx + 1)),
            pl.BlockSpec((n_experts, d), lambda i: (0, 0)),
            pl.BlockSpec((n_experts, 1), lambda i: (0, 0)),
        ],
        out_specs=[pl.BlockSpec((tm, d // 2), lambda i: (i, 0)), slot_out, slot_out, slot_out,
                   pl.BlockSpec((n_experts, LANES), lambda i: (0, 0))],
        out_shape=[jax.ShapeDtypeStruct((n, d // 2), jnp.uint32), slot_shape, slot_shape, slot_shape,
                   jax.ShapeDtypeStruct((n_experts, LANES), F32)],
        scratch_shapes=[pltpu.VMEM((n_experts, 1), F32)],
        compiler_params=_params("arbitrary"),
        name="moe_router",
    )(x2, g.reshape(1, d), mod_l, mod_l, router_w.T.astype(BF16), router_b.reshape(n_experts, 1))


GLU_TILE = 2 * LANES


def _experts_kernel(te_ref, nu_ref, slot_ref, nxt_ref, rows_ref, xs_ref, b1_ref, b2_ref, perm_ref, w1_hbm, w2_hbm,
                    ys_ref, w1buf, w2buf, w1p_ref, w2b_ref, sem, *, layer):
    t = pl.program_id(0)
    used = t < nu_ref[0]
    new_expert = (t == 0) | (te_ref[t] != te_ref[jnp.maximum(t - 1, 0)])
    n_col_tiles = w1p_ref.shape[1] // GLU_TILE

    def weight_copies(e, slot):
        return (pltpu.make_async_copy(w1_hbm.at[layer, e], w1buf.at[slot], sem.at[0, slot]),
                pltpu.make_async_copy(w2_hbm.at[layer, e], w2buf.at[slot], sem.at[1, slot]))

    @pl.when(used & new_expert)
    def _():
        slot = slot_ref[t]

        @pl.when(t == 0)
        def _():
            for cp in weight_copies(te_ref[t], slot):
                cp.start()

        for cp in weight_copies(te_ref[t], slot):
            cp.wait()

        @pl.when(nxt_ref[t] >= 0)
        def _():
            for cp in weight_copies(nxt_ref[t], 1 - slot):
                cp.start()

        for c in range(n_col_tiles):
            cols = slice(c * GLU_TILE, (c + 1) * GLU_TILE)
            w1p_ref[:, cols] = _dot(w1buf[slot, :, cols].astype(BF16), perm_ref[...]).astype(BF16)
        w2b_ref[...] = w2buf[slot].astype(BF16)

    def swiglu_rows(n_rows):
        lo, hi = _unpack_bf16_pair(xs_ref[0:n_rows, :])
        x = jnp.concatenate([lo, hi], axis=1)
        out = b2_ref[0]
        for j in range(n_col_tiles * LANES // EXPERT_CHUNK):
            hcols = slice(2 * j * EXPERT_CHUNK, 2 * (j + 1) * EXPERT_CHUNK)
            hh = _dot(x, w1p_ref[:, hcols]) + b1_ref[0, :, hcols]
            acts = []
            for c in range(2 * EXPERT_CHUNK // GLU_TILE):
                x_glu = jnp.minimum(hh[:, c * GLU_TILE:c * GLU_TILE + LANES], SWIGLU_LIMIT)
                x_lin = jnp.clip(hh[:, c * GLU_TILE + LANES:(c + 1) * GLU_TILE], -SWIGLU_LIMIT, SWIGLU_LIMIT)
                acts.append((x_glu * _sigmoid(SWIGLU_ALPHA * x_glu) * (x_lin + 1.0)).astype(BF16))
            out = out + _dot(jnp.concatenate(acts, axis=1), w2b_ref[j * EXPERT_CHUNK:(j + 1) * EXPERT_CHUNK, :])
        half = out.shape[1] // 2
        ys_ref[0:n_rows, :] = _pack_bf16_pair(out[:, :half], out[:, half:])
        if n_rows < ys_ref.shape[0]:
            ys_ref[n_rows:, :] = jnp.zeros((ys_ref.shape[0] - n_rows, ys_ref.shape[1]), ys_ref.dtype)

    tile_rows = xs_ref.shape[0]
    pl.when(used & (rows_ref[t] > tile_rows // 2))(lambda: swiglu_rows(tile_rows))
    pl.when(used & (rows_ref[t] <= tile_rows // 2))(lambda: swiglu_rows(tile_rows // 2))

    @pl.when(t >= nu_ref[0])
    def _():
        ys_ref[...] = jnp.zeros_like(ys_ref)


def _experts(xs, tile_expert, n_used, slot, nxt, tile_rows, w1_all, b1, w2_all, b2, layer):
    n_rows, dw = xs.shape
    tm = EXPERT_TILE
    _, n_exp, d, de2 = w1_all.shape
    src = np.arange(GLU_TILE)
    dst = np.where(src % 2 == 0, src // 2, LANES + src // 2)
    perm = jnp.asarray(dst[:, None] == np.arange(GLU_TILE)[None, :], BF16)
    b1p = b1.reshape(n_exp, de2 // GLU_TILE, LANES, 2).transpose(0, 1, 3, 2).reshape(n_exp, 1, de2)
    row = lambda t, te, nu, sl, nx, nr: (jnp.maximum(jnp.minimum(t, nu[0] - 1), 0), 0)
    exp3 = lambda t, te, nu, sl, nx, nr: (te[t], 0, 0)
    grid_spec = pltpu.PrefetchScalarGridSpec(
        num_scalar_prefetch=5,
        grid=(n_rows // tm,),
        in_specs=[
            pl.BlockSpec((tm, dw), row),
            pl.BlockSpec((1, 1, de2), exp3),
            pl.BlockSpec((1, 1, d), exp3),
            pl.BlockSpec((GLU_TILE, GLU_TILE), lambda t, te, nu, sl, nx, nr: (0, 0)),
            pl.BlockSpec(memory_space=pl.ANY),
            pl.BlockSpec(memory_space=pl.ANY),
        ],
        out_specs=pl.BlockSpec((tm, d // 2), lambda t, te, nu, sl, nx, nr: (t, 0)),
        scratch_shapes=[pltpu.VMEM((2, d, de2), F32), pltpu.VMEM((2, de2 // 2, d), F32),
                        pltpu.VMEM((d, de2), BF16), pltpu.VMEM((de2 // 2, d), BF16),
                        pltpu.SemaphoreType.DMA((2, 2))],
    )
    return pl.pallas_call(
        functools.partial(_experts_kernel, layer=layer),
        grid_spec=grid_spec,
        out_shape=jax.ShapeDtypeStruct((n_rows, d // 2), jnp.uint32),
        compiler_params=_params("arbitrary"),
        name="moe_experts",
    )(tile_expert, n_used, slot, nxt, tile_rows, xs, b1p, b2.reshape(n_exp, 1, d), perm, w1_all, w2_all)


SC_WINDOW = 128


def _sc_worker_rows(n_rows):
    sc = plsc.get_sparse_core_info()
    per_worker = n_rows // (sc.num_cores * sc.num_subcores)
    worker = lax.axis_index("subcore") * sc.num_cores + lax.axis_index("core")
    return worker * per_worker, per_worker


def _sc_mesh():
    return plsc.VectorSubcoreMesh(core_axis_name="core", subcore_axis_name="subcore")


def _sc_gather_rows(table, idx):
    n_idx, width = idx.shape[0], table.shape[1]

    @functools.partial(
        pl.kernel, out_type=jax.ShapeDtypeStruct((n_idx, width), table.dtype), mesh=_sc_mesh(),
        scratch_types=[pltpu.VMEM((SC_WINDOW,), jnp.int32), pltpu.VMEM((SC_WINDOW, width), table.dtype)],
        name="sc_gather_rows")
    def gather(table_hbm, idx_hbm, out_hbm, idx_v, rows_v):
        first, count = _sc_worker_rows(n_idx)

        @pl.loop(0, count // SC_WINDOW)
        def _(c):
            rows = pl.ds(first + c * SC_WINDOW, SC_WINDOW)
            pltpu.sync_copy(idx_hbm.at[rows], idx_v)
            pltpu.sync_copy(table_hbm.at[idx_v], rows_v)
            pltpu.sync_copy(rows_v, out_hbm.at[rows])

    return gather(table, idx)


def _sc_scatter_rows(rows, pos, n_out):
    n, width = rows.shape
    k = pos.shape[0]

    @functools.partial(
        pl.kernel, out_type=jax.ShapeDtypeStruct((n_out, width), rows.dtype), mesh=_sc_mesh(),
        scratch_types=[pltpu.VMEM((k, SC_WINDOW), jnp.int32), pltpu.VMEM((SC_WINDOW, width), rows.dtype)],
        name="sc_scatter_rows")
    def scatter(rows_hbm, pos_hbm, out_hbm, pos_v, rows_v):
        first, count = _sc_worker_rows(n)

        @pl.loop(0, count // SC_WINDOW)
        def _(c):
            src = pl.ds(first + c * SC_WINDOW, SC_WINDOW)
            pltpu.sync_copy(pos_hbm.at[:, src], pos_v)
            pltpu.sync_copy(rows_hbm.at[src], rows_v)
            for j in range(k):
                pltpu.sync_copy(rows_v, out_hbm.at[pos_v.at[j]])

    return scatter(rows, pos)


def _combine_dense_kernel(x_ref, gate_ref, w_ref, fg_ref, *rest, final_norm):
    y_refs, o_ref = rest[:TOP_K], rest[TOP_K]
    w = w_ref[...].T
    y_lo = y_hi = None
    for j in range(TOP_K):
        lo, hi = _unpack_bf16_pair(y_refs[j][...])
        wj = w[:, j:j + 1]
        y_lo = wj * lo if j == 0 else y_lo + wj * lo
        y_hi = wj * hi if j == 0 else y_hi + wj * hi
    out = x_ref[...] + gate_ref[0] * jnp.concatenate([y_lo, y_hi], axis=1)
    if final_norm:
        out = out * lax.rsqrt(jnp.mean(out * out, axis=-1, keepdims=True) + NORM_EPS) * fg_ref[...]
    o_ref[...] = out


def _combine_dense(x2, mod_l, gate_idx, seq, top_w, y4, final_g, final_norm, tm=1024):
    n, d = x2.shape
    per_batch = seq // tm
    blocks = n // tm
    slot_spec = lambda j: pl.BlockSpec((tm, d // 2), lambda i: (j * blocks + i, 0))
    return pl.pallas_call(
        functools.partial(_combine_dense_kernel, final_norm=final_norm),
        grid=(blocks,),
        in_specs=[
            pl.BlockSpec((tm, d), lambda i: (i, 0)),
            pl.BlockSpec((1, 1, d), lambda i: (i // per_batch, 0, gate_idx)),
            pl.BlockSpec((top_w.shape[0], tm), lambda i: (0, i)),
            pl.BlockSpec((1, d), lambda i: (0, 0)),
        ] + [slot_spec(j) for j in range(TOP_K)],
        out_specs=pl.BlockSpec((tm, d), lambda i: (i, 0)),
        out_shape=jax.ShapeDtypeStruct((n, d), F32),
        compiler_params=_params("parallel"),
        name="moe_combine",
    )(x2, mod_l, top_w, final_g.reshape(1, d), *([y4] * TOP_K))


def _moe_ffn(x2, norm_g, mod_l, seq, router_w, router_b, w1_all, b1, w2_all, b2, layer, final_g, final_norm):
    n, d = x2.shape
    n_exp = router_w.shape[1]
    tile = EXPERT_TILE
    hp, idx_f, top_w, rank_f, counts = _router(x2, norm_g, mod_l, 3, seq, router_w, router_b)
    counts = counts[:, 0].astype(jnp.int32)
    padded = (counts + tile - 1) // tile * tile
    ends = jnp.cumsum(padded)
    starts = ends - padded
    idx = idx_f[:TOP_K].astype(jnp.int32)
    expert_ids = jnp.arange(n_exp, dtype=jnp.int32)
    start_of = jnp.sum(jnp.where(idx[..., None] == expert_ids, starts, 0), axis=-1)
    pos_slot_major = start_of + rank_f[:TOP_K].astype(jnp.int32)
    n_rows = n * TOP_K + n_exp * tile
    n_tiles = n_rows // tile
    tile_start = jnp.arange(n_tiles, dtype=jnp.int32) * tile
    tile_expert = jnp.sum((ends[None, :] <= tile_start[:, None]).astype(jnp.int32), axis=1)
    tile_expert = jnp.minimum(tile_expert, n_exp - 1)
    n_used = (ends[-1:] // tile).astype(jnp.int32)
    nonempty = counts > 0
    ordinal = jnp.cumsum(nonempty.astype(jnp.int32)) - 1
    later = nonempty[None, :] & (expert_ids[None, :] > expert_ids[:, None])
    nxt_e = jnp.min(jnp.where(later, expert_ids[None, :], n_exp), axis=1)
    nxt_e = jnp.where(nxt_e == n_exp, -1, nxt_e)
    slot = (ordinal[tile_expert] % 2).astype(jnp.int32)
    nxt = nxt_e[tile_expert].astype(jnp.int32)
    xs = _sc_scatter_rows(hp, pos_slot_major, n_rows)
    filled = jnp.clip((starts + counts)[tile_expert] - tile_start, 0, tile).astype(jnp.int32)
    ys = _experts(xs, tile_expert, n_used, slot, nxt, filled, w1_all, b1, w2_all, b2, layer)
    y4 = _sc_gather_rows(ys, pos_slot_major.reshape(-1))
    return _combine_dense(x2, mod_l, 5, seq, top_w, y4, final_g, final_norm)


def kernel(x, c, ada_w, ada_b, norm_mix_g, norm_ffn_g, router_w, router_b, moe_w1, moe_b1, moe_w2, moe_b2, ab_w_in, ab_w_out, hgrn_lb, hgrn_norm_g, rwkv_mu, rwkv_w0, rwkv_w2, rwkv_a0, rwkv_a2, rwkv_g2, rwkv_k_k, rwkv_k_a, rwkv_r_k, rwkv_ln_g, rwkv_ln_b, attn_w_in, attn_w_out, attn_lambda, attn_subln_g, rel_bias_table, final_norm_g):
    batch, seq, d = x.shape
    n = batch * seq
    depth = ada_w.shape[0]
    x2 = x.reshape(n, d)
    mod = _adaln(c, ada_w, ada_b)
    lb_all = jnp.cumsum(jax.nn.softmax(hgrn_lb.astype(F32), axis=1), axis=1)
    for layer in range(depth):
        mod_l = mod[layer].reshape(batch, 1, 6 * d)
        j = layer // 2
        if layer % 2 == 0:
            a_cols = 5 * (d // 2)
            w_in = ab_w_in[j].astype(BF16)
            p_a, p_b = _normmod_proj(x2, norm_mix_g[layer], mod_l, 0, seq,
                                     [w_in[:, :a_cols], w_in[:, a_cols:]], [F32, F32])
            y_a = _hgrn2(p_a, lb_all[:, j], hgrn_norm_g[j], batch, seq)
            y_b = _rwkv7(p_b, rwkv_mu[j], rwkv_w0[j], rwkv_w2[j], rwkv_a0[j], rwkv_a2[j], rwkv_g2[j],
                         rwkv_k_k[j], rwkv_k_a[j], rwkv_r_k[j], rwkv_ln_g[j], rwkv_ln_b[j], batch, seq)
            w_out = ab_w_out[j].astype(BF16)
            x2 = _outproj_residual(x2, mod_l, 2, seq, [y_a, y_b], [w_out[:d // 2], w_out[d // 2:]])
        else:
            (qkv,) = _normmod_proj(x2, norm_mix_g[layer], mod_l, 0, seq, [attn_w_in[j].astype(BF16)], [BF16])
            lam = attn_lambda[j].astype(F32)
            lam_init = 0.8 - 0.6 * math.exp(-0.3 * layer)
            lam_full = jnp.exp(jnp.sum(lam[0] * lam[1])) - jnp.exp(jnp.sum(lam[2] * lam[3])) + lam_init
            o = _diff_attention(qkv, rel_bias_table, lam_full, attn_subln_g[j], lam_init, batch, seq)
            x2 = _outproj_residual(x2, mod_l, 2, seq, [o], [attn_w_out[j].astype(BF16)])
        x2 = _moe_ffn(x2, norm_ffn_g[layer], mod_l, seq, router_w[layer], router_b[layer], moe_w1,
                      moe_b1[layer], moe_w2, moe_b2[layer], layer, final_norm_g, layer == depth - 1)
    return x2.reshape(batch, seq, d)
```

```python
import functools
import math

import jax
import jax.numpy as jnp
import numpy as np
from jax import lax
from jax.experimental import pallas as pl
from jax.experimental.pallas import tpu as pltpu
from jax.experimental.pallas import tpu_sc as plsc

F32 = jnp.float32
BF16 = jnp.bfloat16
HIGHEST = lax.Precision.HIGHEST

NORM_EPS = 1e-6

A_HEAD_DIM = 128
A_CHUNK = 32
A_GROUP = 512
A_HEADS_PER_STEP = 2
B_HEAD_DIM = 64
B_CHUNK = 64
B_STEPS = 2
B_LN_EPS = 1e-5 * B_HEAD_DIM
C_HEAD_DIM = 64
REL_BUCKETS = 32
REL_MAX_DISTANCE = 128
TOP_K = 4
SWIGLU_LIMIT = 7.0
SWIGLU_ALPHA = 1.702

V7X_VMEM_BYTES = 64 * 1024 * 1024
VMEM_LIMIT = V7X_VMEM_BYTES - 8 * 1024 * 1024
LANES = 128


def _params(*sem):
    return pltpu.CompilerParams(dimension_semantics=sem, vmem_limit_bytes=VMEM_LIMIT)


def _sigmoid(x):
    return 1.0 / (1.0 + jnp.exp(-x))


def _silu(x):
    return x * _sigmoid(x)


def _dot(a, b):
    return jnp.dot(a, b, preferred_element_type=F32)


def _dot_nt(a, b):
    return lax.dot_general(a, b, (((1,), (1,)), ((), ())), preferred_element_type=F32)


def _dot_tn(a, b):
    return lax.dot_general(a, b, (((0,), (0,)), ((), ())), preferred_element_type=F32)


def _adaln_kernel(c_ref, w_ref, b_ref, o_ref):
    cond = _silu(c_ref[...])
    o_ref[0] = jnp.dot(cond, w_ref[0], precision=HIGHEST, preferred_element_type=F32) + b_ref[0]


def _adaln(c, ada_w, ada_b):
    n_layers, d, n_out = ada_w.shape
    batch = c.shape[0]
    tn = 1536
    return pl.pallas_call(
        _adaln_kernel,
        grid=(n_layers, n_out // tn),
        in_specs=[
            pl.BlockSpec((batch, d), lambda l, j: (0, 0)),
            pl.BlockSpec((1, d, tn), lambda l, j: (l, 0, j)),
            pl.BlockSpec((1, 1, tn), lambda l, j: (l, 0, j)),
        ],
        out_specs=pl.BlockSpec((1, batch, tn), lambda l, j: (l, 0, j)),
        out_shape=jax.ShapeDtypeStruct((n_layers, batch, n_out), F32),
        compiler_params=_params("parallel", "parallel"),
        name="adaln",
    )(c, ada_w, ada_b.reshape(n_layers, 1, n_out))


def _modulated_norm(x, g, shift, scale):
    y = x * lax.rsqrt(jnp.mean(x * x, axis=-1, keepdims=True) + NORM_EPS)
    return (y * g) * (1.0 + scale) + shift


def _normmod_proj_kernel(x_ref, g_ref, sh_ref, sc_ref, *rest, n_w):
    w_refs, o_refs = rest[:n_w], rest[n_w:]
    h = _modulated_norm(x_ref[...], g_ref[...], sh_ref[0], sc_ref[0]).astype(BF16)
    for w_ref, o_ref in zip(w_refs, o_refs):
        o_ref[...] = _dot(h, w_ref[...]).astype(o_ref.dtype)


def _normmod_proj(x2, g, mod_l, shift_idx, seq, weights, out_dtypes, tm=512):
    n, d = x2.shape
    per_batch = seq // tm
    in_specs = [
        pl.BlockSpec((tm, d), lambda i: (i, 0)),
        pl.BlockSpec((1, d), lambda i: (0, 0)),
        pl.BlockSpec((1, 1, d), lambda i: (i // per_batch, 0, shift_idx)),
        pl.BlockSpec((1, 1, d), lambda i: (i // per_batch, 0, shift_idx + 1)),
    ]
    in_specs += [pl.BlockSpec(w.shape, lambda i: (0, 0)) for w in weights]
    out_specs = [pl.BlockSpec((tm, w.shape[1]), lambda i: (i, 0)) for w in weights]
    out_shape = [jax.ShapeDtypeStruct((n, w.shape[1]), dt) for w, dt in zip(weights, out_dtypes)]
    return pl.pallas_call(
        functools.partial(_normmod_proj_kernel, n_w=len(weights)),
        grid=(n // tm,),
        in_specs=in_specs,
        out_specs=out_specs,
        out_shape=out_shape,
        compiler_params=_params("parallel"),
        name="normmod_proj",
    )(x2, g.reshape(1, d), mod_l, mod_l, *weights)


def _outproj_kernel(x_ref, gate_ref, *rest, n_y):
    y_refs, w_refs, o_ref = rest[:n_y], rest[n_y:2 * n_y], rest[2 * n_y]
    acc = _dot(y_refs[0][...], w_refs[0][...])
    for y_ref, w_ref in zip(y_refs[1:], w_refs[1:]):
        acc += _dot(y_ref[...], w_ref[...])
    o_ref[...] = x_ref[...] + gate_ref[0] * acc


def _outproj_residual(x2, mod_l, gate_idx, seq, ys, ws, tm=1024):
    n, d = x2.shape
    per_batch = seq // tm
    in_specs = [
        pl.BlockSpec((tm, d), lambda i: (i, 0)),
        pl.BlockSpec((1, 1, d), lambda i: (i // per_batch, 0, gate_idx)),
    ]
    in_specs += [pl.BlockSpec((tm, y.shape[1]), lambda i: (i, 0)) for y in ys]
    in_specs += [pl.BlockSpec(w.shape, lambda i: (0, 0)) for w in ws]
    return pl.pallas_call(
        functools.partial(_outproj_kernel, n_y=len(ys)),
        grid=(n // tm,),
        in_specs=in_specs,
        out_specs=pl.BlockSpec((tm, d), lambda i: (i, 0)),
        out_shape=jax.ShapeDtypeStruct((n, d), F32),
        compiler_params=_params("parallel"),
        name="outproj_residual",
    )(x2, mod_l, *ys, *ws)


def _chunk_cumsum(x, chunk, reverse):
    rows = x.shape[0]
    pos = lax.broadcasted_iota(jnp.int32, x.shape, 0) % chunk
    s = 1
    while s < chunk:
        if reverse:
            x = x + jnp.where(pos < chunk - s, pltpu.roll(x, rows - s, axis=0), 0.0)
        else:
            x = x + jnp.where(pos >= s, pltpu.roll(x, s, axis=0), 0.0)
        s *= 2
    return x


def _hgrn_groups(slabs):
    g_rows, dk = slabs[0][0].shape
    n_chunks = g_rows // A_CHUNK
    ti = lax.broadcasted_iota(jnp.int32, (A_CHUNK, A_CHUNK), 0)
    si = lax.broadcasted_iota(jnp.int32, (A_CHUNK, A_CHUNK), 1)
    chunk_rows = [slice(c * A_CHUNK, (c + 1) * A_CHUNK) for c in range(n_chunks)]
    prep = []
    for q, f, v, lb, st, reverse in slabs:
        fg = lb + (1.0 - lb) * _sigmoid(f)
        k = 1.0 - fg
        b = _chunk_cumsum(jnp.log(fg), A_CHUNK, reverse)
        b3 = b.reshape(n_chunks, A_CHUNK, dk)
        edge = b3[:, 0:1, :] if reverse else b3[:, A_CHUNK - 1:A_CHUNK, :]
        prep.append(dict(
            q_in=(q * jnp.exp(b)).astype(BF16), k_in=(k * jnp.exp(-b)).astype(BF16),
            k_st=(k.reshape(n_chunks, A_CHUNK, dk) * jnp.exp(edge - b3)).astype(BF16),
            decay=jnp.exp(edge), vb=v.astype(BF16), st=st, reverse=reverse,
            mask=(si >= ti) if reverse else (si <= ti)))
    for p in prep:
        p["scores"] = [_dot_nt(p["q_in"][r], p["k_in"][r]) for r in chunk_rows]
        p["dstate"] = [_dot_tn(p["vb"][r], p["k_st"][c]) for c, r in enumerate(chunk_rows)]
    for p in prep:
        p["intra"] = [_dot(jnp.where(p["mask"], s, 0.0).astype(BF16), p["vb"][r])
                      for s, r in zip(p["scores"], chunk_rows)]
    results = []
    for p in prep:
        st = p["st"]
        outs = [None] * n_chunks
        for c in (range(n_chunks - 1, -1, -1) if p["reverse"] else range(n_chunks)):
            outs[c] = p["intra"][c] + _dot_nt(p["q_in"][chunk_rows[c]], st.astype(BF16))
            st = st * p["decay"][c] + p["dstate"][c]
        results.append((jnp.concatenate(outs, axis=0), st))
    return results


def _hgrn_kernel(q_ref, ff_ref, fb_ref, i_ref, g_ref, lb_ref, ng_ref, o_ref, of_ref, ob_ref):
    seq = q_ref.shape[0]
    dk = A_HEAD_DIM
    n_heads = q_ref.shape[1] // dk
    n_groups = seq // A_GROUP
    head_cols = [slice(h * dk, (h + 1) * dk) for h in range(n_heads)]

    def body(j, states):
        rf = pl.ds(pl.multiple_of(j * A_GROUP, A_GROUP), A_GROUP)
        rb = pl.ds(pl.multiple_of((n_groups - 1 - j) * A_GROUP, A_GROUP), A_GROUP)
        slabs = []
        for h, cols in enumerate(head_cols):
            slabs.append((_silu(q_ref[rf, cols]), ff_ref[rf, cols], i_ref[rf, cols], lb_ref[0:1, cols],
                          states[2 * h], False))
            slabs.append((_silu(q_ref[rb, cols]), fb_ref[rb, cols], i_ref[rb, cols], lb_ref[1:2, cols],
                          states[2 * h + 1], True))
        results = _hgrn_groups(slabs)
        for h, cols in enumerate(head_cols):
            of_ref[rf, cols] = results[2 * h][0]
            ob_ref[rb, cols] = results[2 * h + 1][0]
        return tuple(r[1] for r in results)

    zero = jnp.zeros((dk, dk), F32)
    lax.fori_loop(0, n_groups, body, (zero,) * (2 * n_heads))

    def finish(j, carry):
        r = pl.ds(pl.multiple_of(j * A_GROUP, A_GROUP), A_GROUP)
        for cols in head_cols:
            o = of_ref[r, cols] + ob_ref[r, cols]
            o = o * lax.rsqrt(jnp.mean(o * o, axis=-1, keepdims=True) + NORM_EPS) * ng_ref[...]
            o_ref[r, cols] = (o * _silu(g_ref[r, cols])).astype(o_ref.dtype)
        return carry

    lax.fori_loop(0, n_groups, finish, 0)


def _hgrn2(p_a, lb, norm_g, batch, seq):
    n = p_a.shape[0]
    width = p_a.shape[1] // 5
    cols = A_HEADS_PER_STEP * A_HEAD_DIM
    steps = width // cols
    sect = lambda s: pl.BlockSpec((seq, cols), lambda b, h, s=s: (b, s * steps + h))
    return pl.pallas_call(
        _hgrn_kernel,
        grid=(batch, steps),
        in_specs=[sect(0), sect(1), sect(2), sect(3), sect(4),
                  pl.BlockSpec((2, cols), lambda b, h: (0, h)),
                  pl.BlockSpec((1, A_HEAD_DIM), lambda b, h: (0, 0))],
        out_specs=pl.BlockSpec((seq, cols), lambda b, h: (b, h)),
        out_shape=jax.ShapeDtypeStruct((n, width), BF16),
        scratch_shapes=[pltpu.VMEM((seq, cols), F32), pltpu.VMEM((seq, cols), F32)],
        compiler_params=_params("parallel", "parallel"),
        name="hgrn2",
    )(p_a, p_a, p_a, p_a, p_a, lb, norm_g.reshape(1, A_HEAD_DIM))


def _softplus(z):
    return jnp.maximum(z, 0.0) + jnp.log(1.0 + jnp.exp(-jnp.abs(z)))


def _rwkv_prep_kernel(p_ref, prev_ref, next_ref, mu_ref, w0_ref, w2_ref, a0_ref, a2_ref, g2_ref,
                      kk_ref, ka_ref, rk_ref, ones_ref, tri_ref,
                      r_out, k_out, v_out, kkn_out, b_out, cumf_out, cumb_out, g_out, bonus_out):
    i = pl.program_id(1)
    last = pl.num_programs(1) - 1
    ts = p_ref.shape[0]
    width = a0_ref.shape[1]
    heads = width // B_HEAD_DIM
    p = p_ref[...]
    prev_row = jnp.where(i == 0, 0.0, prev_ref[7:8, :])
    next_row = jnp.where(i == last, 0.0, next_ref[0:1, :])
    row = lax.broadcasted_iota(jnp.int32, (ts, 1), 0)
    up = jnp.where(row == 0, prev_row, pltpu.roll(p, 1, axis=0))
    dn = jnp.where(row == ts - 1, next_row, pltpu.roll(p, ts - 1, axis=0))
    p = p + mu_ref[...] * (0.5 * (up + dn) - p)

    r = p[:, 0:width]
    k = p[:, width:2 * width]
    v = p[:, 2 * width:3 * width]
    o = 3 * width
    wlo_f = p[:, o:o + 64]
    wlo_b = p[:, o + 64:o + 128]
    alo = p[:, o + 128:o + 192]
    glo = p[:, o + 192:o + 320]

    def log_decay(wlo, d):
        z = w0_ref[d:d + 1, :] + _dot(jnp.tanh(wlo).astype(BF16), w2_ref[d])
        w = -_softplus(-z) - 0.5
        return -jnp.exp(w)

    def split_dot(sel, t, right):
        hi = t.astype(BF16)
        lo = (t - hi.astype(F32)).astype(BF16)
        return (_dot(hi, sel) + _dot(lo, sel)) if right else (_dot(sel, hi) + _dot(sel, lo))

    cum_f = split_dot(tri_ref[0], log_decay(wlo_f, 0), False)
    cum_b = split_dot(tri_ref[1], log_decay(wlo_b, 1), False)
    a = _sigmoid(a0_ref[...] + _dot(alo.astype(BF16), a2_ref[...]))
    g = _dot(_sigmoid(glo).astype(BF16), g2_ref[...])
    kk = k * kk_ref[...]
    head_sum = lambda t: split_dot(ones_ref[...], t, True)

    kk_n = kk / jnp.maximum(jnp.sqrt(head_sum(kk * kk)), 1e-12)
    k_mod = k * (1.0 + (a - 1.0) * ka_ref[...])
    bonus = head_sum(r * k_mod * rk_ref[...]) * v
    g_out[...] = g
    bonus_out[...] = bonus
    for h in range(heads):
        sl = slice(h * B_HEAD_DIM, (h + 1) * B_HEAD_DIM)
        r_out[0, h] = r[:, sl]
        k_out[0, h] = k_mod[:, sl]
        v_out[0, h] = v[:, sl]
        kkn_out[0, h] = kk_n[:, sl]
        b_out[0, h] = (kk_n * a)[:, sl]
        cumf_out[0, h] = cum_f[:, sl]
        cumb_out[0, h] = cum_b[:, sl]


def _rwkv_prep(p_b, mu, w0, w2, a0, a2, g2, k_k, k_a, r_k, batch, seq, ts=512):
    n, cols = p_b.shape
    width = a0.shape[0]
    heads = width // B_HEAD_DIM
    nblk = seq // ts
    rows8 = ts // 8
    head_id = np.arange(width) // B_HEAD_DIM
    ones = jnp.asarray(head_id[:, None] == head_id[None, :], BF16)
    full = lambda a: pl.BlockSpec(a.shape, lambda b, i: (0,) * a.ndim)
    vec = lambda a: a.reshape(1, -1)
    row = np.arange(ts)
    same_chunk = (row[:, None] // B_CHUNK) == (row[None, :] // B_CHUNK)
    tri = jnp.asarray(np.stack([same_chunk & (row[None, :] <= row[:, None]),
                                same_chunk & (row[None, :] >= row[:, None])]), BF16)
    args = [vec(mu), w0, w2.astype(BF16), vec(a0), a2.astype(BF16), g2.astype(BF16),
            vec(k_k), vec(k_a), vec(r_k), ones, tri]
    hm = pl.BlockSpec((1, heads, ts, B_HEAD_DIM), lambda b, i: (b, 0, i, 0))
    tokm = pl.BlockSpec((ts, width), lambda b, i: (b * nblk + i, 0))
    hm_shape = jax.ShapeDtypeStruct((batch, heads, seq, B_HEAD_DIM), F32)
    tok_shape = jax.ShapeDtypeStruct((n, width), F32)
    return pl.pallas_call(
        _rwkv_prep_kernel,
        grid=(batch, nblk),
        in_specs=[
            pl.BlockSpec((ts, cols), lambda b, i: (b * nblk + i, 0)),
            pl.BlockSpec((8, cols), lambda b, i: (jnp.maximum((b * nblk + i) * rows8 - 1, 0), 0)),
            pl.BlockSpec((8, cols), lambda b, i: (jnp.minimum((b * nblk + i + 1) * rows8, n // 8 - 1), 0)),
        ] + [full(a) for a in args],
        out_specs=[hm] * 7 + [tokm] * 2,
        out_shape=[hm_shape] * 7 + [tok_shape] * 2,
        compiler_params=_params("parallel", "parallel"),
        name="rwkv_prep",
    )(p_b, p_b, p_b, *args)


def _rwkv_chunks(chains):
    c, d = chains[0][0].shape
    ti = lax.broadcasted_iota(jnp.int32, (c, c), 0)
    si = lax.broadcasted_iota(jnp.int32, (c, c), 1)
    eye = jnp.where(ti == si, 1.0, 0.0)
    masks = {rev: (((si > ti), (si >= ti)) if rev else ((si < ti), (si <= ti))) for rev in (False, True)}
    t2 = lax.broadcasted_iota(jnp.int32, (c, 2 * c), 0)
    s2 = lax.broadcasted_iota(jnp.int32, (c, 2 * c), 1)
    s2 = jnp.where(s2 >= c, s2 - c, s2)
    incl2 = {False: s2 <= t2, True: s2 >= t2}

    row = lax.broadcasted_iota(jnp.int32, (c, d), 0)
    st = []
    for r, k, v, kk, b, cum, rev in chains:
        before = (jnp.where(row == c - 1, 0.0, pltpu.roll(cum, c - 1, axis=0)) if rev
                  else jnp.where(row == 0, 0.0, pltpu.roll(cum, 1, axis=0)))
        grow = jnp.exp(-cum)
        kt = (kk * jnp.exp(before)).astype(BF16)
        rt = r * jnp.exp(cum)
        b_ = (b * grow).astype(BF16)
        kb = jnp.concatenate([(k * grow).astype(BF16), b_], axis=0)
        st.append(dict(vb=v.astype(BF16), kt=kt, rt=rt, b_=b_, kb=kb,
                       decay=jnp.exp(cum[0:1, :] if rev else cum[c - 1:c, :]), rev=rev,
                       lhs=jnp.concatenate([kt, rt.astype(BF16)], axis=0)))
    for s in st:
        s["big"] = _dot_nt(s["lhs"], s["kb"])
    for s in st:
        strict, incl = masks[s["rev"]]
        big = s["big"]
        a_ab = jnp.where(strict, big[:c, c:], 0.0)
        s["a_ak"] = jnp.where(strict, big[:c, :c], 0.0).astype(BF16)
        s["p"] = jnp.where(incl2[s["rev"]], big[c:], 0.0).astype(BF16)
        s["m"] = eye - a_ab
        s["ab"] = a_ab.astype(BF16)
    for s in st:
        s["aj"] = _dot(s["ab"], s["ab"])
        s["av"] = _dot(s["a_ak"], s["vb"])
    span = 2
    while span < c:
        span *= 2
        for s in st:
            ajb = s["aj"].astype(BF16)
            if span < c:
                both = _dot(jnp.concatenate([ajb, s["m"].astype(BF16)], axis=0), ajb)
                s["aj"], s["m"] = both[:c], s["m"] + both[c:]
            else:
                s["m"] = s["m"] + _dot(s["m"].astype(BF16), ajb)
    for s in st:
        wu = _dot(s["m"].astype(BF16), jnp.concatenate([s["kt"], s["av"].astype(BF16)], axis=1))
        s["nwu"] = -wu.astype(BF16)
    for s in st:
        nw, nu0, vb = s["nwu"][:, :d], s["nwu"][:, d:], s["vb"]
        rhs = jnp.concatenate([jnp.concatenate([vb, jnp.zeros_like(vb)], axis=1),
                               jnp.concatenate([nu0, nw], axis=1)], axis=0)
        s["yq"] = _dot(s["p"], rhs)
        s["h_t"] = _dot_tn(jnp.concatenate([vb, nu0], axis=0), s["kb"])
        s["nbw"] = _dot_tn(s["b_"], nw)
    return [dict(q=(s["rt"] + s["yq"][:, d:]).astype(BF16), y1=s["yq"][:, :d], nbw=s["nbw"].astype(BF16),
                 h_t=s["h_t"], decay=s["decay"]) for s in st]


def _rwkv_advance(states, steps):
    ttbs = [tt.astype(BF16) for tt in states]
    ys = [_dot_nt(s["q"], ttb) + s["y1"] for s, ttb in zip(steps, ttbs)]
    new = [(tt + _dot_nt(ttb, s["nbw"]) + s["h_t"]) * s["decay"] for tt, ttb, s in zip(states, ttbs, steps)]
    return ys, new


def _rwkv_scan_kernel(rf, kf, vf, kkf, bf, lwf, rb, kb, vb, kkb, bb, lwb, yf_ref, yb_ref, tf_ref, tb_ref):
    heads, ts = rf.shape[1], rf.shape[2]
    n_chunks = ts // B_CHUNK

    @pl.when(pl.program_id(1) == 0)
    def _():
        tf_ref[...] = jnp.zeros_like(tf_ref)
        tb_ref[...] = jnp.zeros_like(tb_ref)

    def body(j, carry):
        rows_f = [pl.ds(pl.multiple_of((j * B_STEPS + s) * B_CHUNK, B_CHUNK), B_CHUNK) for s in range(B_STEPS)]
        rows_b = [pl.ds(pl.multiple_of((n_chunks - 1 - j * B_STEPS - s) * B_CHUNK, B_CHUNK), B_CHUNK)
                  for s in range(B_STEPS)]
        chains = []
        for s in range(B_STEPS):
            for h in range(heads):
                chains.append([ref[0, h, rows_f[s], :] for ref in (rf, kf, vf, kkf, bf, lwf)] + [False])
                chains.append([ref[0, h, rows_b[s], :] for ref in (rb, kb, vb, kkb, bb, lwb)] + [True])
        solved = _rwkv_chunks(chains)
        states = []
        for h in range(heads):
            states += [tf_ref[h], tb_ref[h]]
        for s in range(B_STEPS):
            ys, states = _rwkv_advance(states, solved[s * 2 * heads:(s + 1) * 2 * heads])
            for h in range(heads):
                yf_ref[0, h, rows_f[s], :] = ys[2 * h]
                yb_ref[0, h, rows_b[s], :] = ys[2 * h + 1]
        for h in range(heads):
            tf_ref[h], tb_ref[h] = states[2 * h], states[2 * h + 1]
        return carry

    lax.fori_loop(0, n_chunks // B_STEPS, body, 0)


def _rwkv_scan(r, k, v, kkn, b, lw_f, lw_b, ts=256):
    batch, heads, seq, d = r.shape
    nblk = seq // ts
    fwd = pl.BlockSpec((1, heads, ts, d), lambda bi, i: (bi, 0, i, 0))
    bwd = pl.BlockSpec((1, heads, ts, d), lambda bi, i: (bi, 0, nblk - 1 - i, 0))
    shape = jax.ShapeDtypeStruct(r.shape, F32)
    return pl.pallas_call(
        _rwkv_scan_kernel,
        grid=(batch, nblk),
        in_specs=[fwd] * 6 + [bwd] * 6,
        out_specs=[fwd, bwd],
        out_shape=[shape, shape],
        scratch_shapes=[pltpu.VMEM((heads, d, d), F32), pltpu.VMEM((heads, d, d), F32)],
        compiler_params=_params("parallel", "arbitrary"),
        name="rwkv_scan",
    )(r, k, v, kkn, b, lw_f, r, k, v, kkn, b, lw_b)


def _rwkv_post_kernel(yf_ref, yb_ref, g_ref, bonus_ref, lng_ref, lnb_ref, o_ref):
    heads = yf_ref.shape[1]
    outs = []
    for h in range(heads):
        y = yf_ref[0, h] + yb_ref[0, h]
        mean = jnp.mean(y, axis=-1, keepdims=True)
        var = jnp.mean(jnp.square(y - mean), axis=-1, keepdims=True)
        outs.append((y - mean) * lax.rsqrt(var + B_LN_EPS))
    yn = jnp.concatenate(outs, axis=1) * lng_ref[...] + lnb_ref[...]
    o_ref[...] = ((yn + bonus_ref[...]) * g_ref[...]).astype(o_ref.dtype)


def _rwkv_post(y_f, y_b, g, bonus, ln_g, ln_b, ts=1024):
    batch, heads, seq, d = y_f.shape
    n, width = g.shape
    nblk = seq // ts
    hm = pl.BlockSpec((1, heads, ts, d), lambda b, i: (b, 0, i, 0))
    tokm = pl.BlockSpec((ts, width), lambda b, i: (b * nblk + i, 0))
    vec = pl.BlockSpec((1, width), lambda b, i: (0, 0))
    return pl.pallas_call(
        _rwkv_post_kernel,
        grid=(batch, nblk),
        in_specs=[hm, hm, tokm, tokm, vec, vec],
        out_specs=tokm,
        out_shape=jax.ShapeDtypeStruct((n, width), BF16),
        compiler_params=_params("parallel", "parallel"),
        name="rwkv_post",
    )(y_f, y_b, g, bonus, ln_g.reshape(1, width), ln_b.reshape(1, width))


def _rwkv7(p_b, mu, w0, w2, a0, a2, g2, k_k, k_a, r_k, ln_g, ln_b, batch, seq):
    r, k, v, kkn, b, lw_f, lw_b, g, bonus = _rwkv_prep(p_b, mu, w0, w2, a0, a2, g2, k_k, k_a, r_k, batch, seq)
    y_f, y_b = _rwkv_scan(r, k, v, kkn, b, lw_f, lw_b)
    return _rwkv_post(y_f, y_b, g, bonus, ln_g, ln_b)


_BUCKET_EDGES = (0, 1, 2, 3, 4, 5, 6, 7, 8, 12, 16, 23, 32, 46, 64, 91)
ATTN_TK = 128
C_HEADS_PER_STEP = 4


def _bias_band_kernel(table_ref, o_ref):
    h = pl.program_id(0)
    n_d, tq, tk = o_ref.shape[1:]
    half = REL_BUCKETS // 2
    r = lax.broadcasted_iota(jnp.int32, (tq, tk), 0)
    c = lax.broadcasted_iota(jnp.int32, (tq, tk), 1)
    for d in range(n_d):
        rel = tk * (d - 1) + c - r
        n = jnp.abs(rel)
        vals = []
        for sign in range(2):
            val = jnp.full((tq, tk), table_ref[sign * half + half - 1, h], F32)
            for bkt in range(half - 2, -1, -1):
                val = jnp.where(n < _BUCKET_EDGES[bkt + 1], table_ref[sign * half + bkt, h], val)
            vals.append(val)
        o_ref[0, d] = jnp.where(rel > 0, vals[1], vals[0])


def _bias_band(rel_table, tq):
    heads = rel_table.shape[1]
    n_d = tq // ATTN_TK + 2
    return pl.pallas_call(
        _bias_band_kernel,
        grid=(heads,),
        in_specs=[pl.BlockSpec(memory_space=pltpu.SMEM)],
        out_specs=pl.BlockSpec((1, n_d, tq, ATTN_TK), lambda h: (h, 0, 0, 0)),
        out_shape=jax.ShapeDtypeStruct((heads, n_d, tq, ATTN_TK), F32),
        compiler_params=_params("parallel"),
        name="bias_band",
    )(rel_table)


def _diff_attn_kernel(table_ref, lam_ref, q_ref, k_ref, v_ref, band_ref, g_ref, o_ref, lg_ref, vext_ref, kt_ref,
                      *, lam_init):
    i = pl.program_id(2)
    tq = q_ref.shape[0]
    dh2 = 2 * C_HEAD_DIM
    n_heads = q_ref.shape[1] // dh2
    seq = k_ref.shape[0]
    n_d = band_ref.shape[1]
    n_kt = seq // ATTN_TK
    ratio = tq // ATTN_TK
    half = REL_BUCKETS // 2
    head_cols = [slice(hh * dh2, (hh + 1) * dh2) for hh in range(n_heads)]

    @pl.when(i == 0)
    def _():
        for hh, cols in enumerate(head_cols):
            kt_ref[hh] = k_ref[:, cols].astype(F32).T.astype(BF16)
            v = v_ref[:, cols]
            vext_ref[hh] = jnp.concatenate([v, jnp.ones_like(v)], axis=1)

    lane = lax.broadcasted_iota(jnp.int32, (1, dh2), 1)
    first_band = ratio * i - 1
    col = lax.broadcasted_iota(jnp.int32, (1, seq), 1)
    for hh, cols in enumerate(head_cols):
        h = pl.program_id(1) * n_heads + hh
        q = q_ref[:, cols] * jnp.asarray(C_HEAD_DIM ** -0.5, q_ref.dtype)
        zero = jnp.zeros_like(q)
        far_row = jnp.where(col < first_band * ATTN_TK, table_ref[half - 1, h],
                            jnp.where(col >= (first_band + n_d) * ATTN_TK, table_ref[REL_BUCKETS - 1, h],
                                      -jnp.inf))
        row_max = []
        for m in range(2):
            qm = jnp.where((lane >= m * C_HEAD_DIM) & (lane < (m + 1) * C_HEAD_DIM), q, zero)
            far = _dot(qm, kt_ref[hh]) + far_row
            mx = jnp.max(far, axis=-1, keepdims=True)
            lg_ref[hh, m] = far
            for d in (0, n_d - 1) + tuple(range(1, n_d - 1)):
                kt = first_band + d
                valid = (kt >= 0) & (kt < n_kt)
                c0 = pl.multiple_of(jnp.clip(kt, 0, n_kt - 1) * ATTN_TK, ATTN_TK)
                near = _dot(qm, kt_ref[hh, :, pl.ds(c0, ATTN_TK)]) + band_ref[hh, d]
                mx = jnp.maximum(mx, jnp.where(valid, jnp.max(near, axis=-1, keepdims=True), -jnp.inf))
                lg_ref[hh, m, :, pl.ds(c0, ATTN_TK)] = near
            row_max.append(mx)
        parts = []
        for m in range(2):
            e = jnp.exp(lg_ref[hh, m] - row_max[m]).astype(BF16)
            pv = _dot(e, vext_ref[hh])
            parts.append(pv[:, :dh2] * (1.0 / pv[:, dh2:dh2 + 1]))
        o = parts[0] - lam_ref[0] * parts[1]
        o = o * lax.rsqrt(jnp.mean(o * o, axis=-1, keepdims=True) + NORM_EPS) * g_ref[...]
        o_ref[:, cols] = (o * (1.0 - lam_init)).astype(o_ref.dtype)


def _diff_attention(qkv, rel_table, lam_full, subln_g, lam_init, batch, seq, tq=256):
    n = qkv.shape[0]
    dh2 = 2 * C_HEAD_DIM
    heads = qkv.shape[1] // (3 * dh2)
    nq = seq // tq
    per = C_HEADS_PER_STEP
    groups = heads // per
    cols = per * dh2
    band = _bias_band(rel_table, tq)
    return pl.pallas_call(
        functools.partial(_diff_attn_kernel, lam_init=lam_init),
        grid=(batch, groups, nq),
        in_specs=[
            pl.BlockSpec(memory_space=pltpu.SMEM),
            pl.BlockSpec(memory_space=pltpu.SMEM),
            pl.BlockSpec((tq, cols), lambda b, h, i: (b * nq + i, h)),
            pl.BlockSpec((seq, cols), lambda b, h, i: (b, groups + h)),
            pl.BlockSpec((seq, cols), lambda b, h, i: (b, 2 * groups + h)),
            pl.BlockSpec((per,) + band.shape[1:], lambda b, h, i: (h, 0, 0, 0)),
            pl.BlockSpec((1, dh2), lambda b, h, i: (0, 0)),
        ],
        out_specs=pl.BlockSpec((tq, cols), lambda b, h, i: (b * nq + i, h)),
        out_shape=jax.ShapeDtypeStruct((n, heads * dh2), BF16),
        scratch_shapes=[pltpu.VMEM((per, 2, tq, seq), F32), pltpu.VMEM((per, seq, 2 * dh2), BF16),
                        pltpu.VMEM((per, dh2, seq), BF16)],
        compiler_params=_params("parallel", "parallel", "arbitrary"),
        name="diff_attention",
    )(rel_table, lam_full.reshape(1), qkv, qkv, qkv, band, subln_g.reshape(1, dh2))


MOE_TILE = 1024
EXPERT_TILE = 512
EXPERT_CHUNK = 512
_HI_MASK = np.uint32(0xFFFF0000)


def _pack_bf16_pair(lo, hi):
    as_bits = lambda t: lax.bitcast_convert_type(t.astype(BF16).astype(F32), jnp.uint32)
    return (as_bits(hi) & _HI_MASK) | (as_bits(lo) >> 16)


def _unpack_bf16_pair(word):
    lo = lax.bitcast_convert_type(word << 16, F32).astype(BF16)
    hi = lax.bitcast_convert_type(word & _HI_MASK, F32).astype(BF16)
    return lo, hi


def _router_kernel(x_ref, g_ref, sh_ref, sc_ref, rw_ref, rb_ref, hp_ref, idx_ref, w_ref, rank_ref, cnt_ref,
                   run_ref, *, n_experts):
    @pl.when(pl.program_id(0) == 0)
    def _():
        run_ref[...] = jnp.zeros_like(run_ref)

    tm, d = x_ref.shape
    h = _modulated_norm(x_ref[...], g_ref[...], sh_ref[0], sc_ref[0])
    hp_ref[...] = _pack_bf16_pair(h[:, :d // 2], h[:, d // 2:])
    logits = _dot_nt(rw_ref[...], h.astype(BF16)) + rb_ref[...]
    expert = lax.broadcasted_iota(jnp.int32, logits.shape, 0).astype(F32)
    neg = jnp.float32(-jnp.inf)
    picks, vals = [], []
    for _ in range(TOP_K):
        m = jnp.max(logits, axis=0, keepdims=True)
        first = jnp.min(jnp.where(logits == m, expert, float(n_experts)), axis=0, keepdims=True)
        hit = expert == first
        picks.append((first, hit))
        vals.append(m)
        logits = jnp.where(hit, neg, logits)
    es = [jnp.exp(v - vals[0]) for v in vals]
    inv = 1.0 / (es[0] + es[1] + es[2] + es[3])
    onehot = sum(jnp.where(hit, 1.0, 0.0) for _, hit in picks)
    ri = lax.broadcasted_iota(jnp.int32, (tm, tm), 0)
    ci = lax.broadcasted_iota(jnp.int32, (tm, tm), 1)
    earlier = jnp.where(ri < ci, 1.0, 0.0).astype(BF16)
    before = _dot(onehot.astype(BF16), earlier) + run_ref[...]
    row = lax.broadcasted_iota(jnp.int32, idx_ref.shape, 0)
    idx_o = jnp.zeros(idx_ref.shape, F32)
    w_o = jnp.zeros(idx_ref.shape, F32)
    rank_o = jnp.zeros(idx_ref.shape, F32)
    for j, (first, hit) in enumerate(picks):
        rank_j = jnp.sum(jnp.where(hit, before, 0.0), axis=0, keepdims=True)
        idx_o = jnp.where(row == j, first, idx_o)
        w_o = jnp.where(row == j, es[j] * inv, w_o)
        rank_o = jnp.where(row == j, rank_j, rank_o)
    idx_ref[...] = idx_o
    w_ref[...] = w_o
    rank_ref[...] = rank_o
    run_ref[...] += jnp.sum(onehot, axis=1, keepdims=True)
    cnt_ref[...] = jnp.broadcast_to(run_ref[...], cnt_ref.shape)


def _router(x2, g, mod_l, shift_idx, seq, router_w, router_b):
    n, d = x2.shape
    tm = MOE_TILE
    n_experts = router_w.shape[1]
    per_batch = seq // tm
    slots = 8
    slot_out = pl.BlockSpec((slots, tm), lambda i: (0, i))
    slot_shape = jax.ShapeDtypeStruct((slots, n), F32)
    return pl.pallas_call(
        functools.partial(_router_kernel, n_experts=n_experts),
        grid=(n // tm,),
        in_specs=[
            pl.BlockSpec((tm, d), lambda i: (i, 0)),
            pl.BlockSpec((1, d), lambda i: (0, 0)),
            pl.BlockSpec((1, 1, d), lambda i: (i // per_batch, 0, shift_idx)),
            pl.BlockSpec((1, 1, d), lambda i: (i // per_batch, 0, shift_idx + 1)),
            pl.BlockSpec((n_experts, d), lambda i: (0, 0)),
            pl.BlockSpec((n_experts, 1), lambda i: (0, 0)),
        ],
        out_specs=[pl.BlockSpec((tm, d // 2), lambda i: (i, 0)), slot_out, slot_out, slot_out,
                   pl.BlockSpec((n_experts, LANES), lambda i: (0, 0))],
        out_shape=[jax.ShapeDtypeStruct((n, d // 2), jnp.uint32), slot_shape, slot_shape, slot_shape,
                   jax.ShapeDtypeStruct((n_experts, LANES), F32)],
        scratch_shapes=[pltpu.VMEM((n_experts, 1), F32)],
        compiler_params=_params("arbitrary"),
        name="moe_router",
    )(x2, g.reshape(1, d), mod_l, mod_l, router_w.T.astype(BF16), router_b.reshape(n_experts, 1))


GLU_TILE = 2 * LANES


def _experts_kernel(te_ref, nu_ref, slot_ref, nxt_ref, rows_ref, xs_ref, b1_ref, b2_ref, perm_ref, w1_hbm, w2_hbm,
                    ys_ref, w1buf, w2buf, w1p_ref, w2b_ref, sem, *, layer):
    t = pl.program_id(0)
    used = t < nu_ref[0]
    new_expert = (t == 0) | (te_ref[t] != te_ref[jnp.maximum(t - 1, 0)])
    n_col_tiles = w1p_ref.shape[1] // GLU_TILE

    def weight_copies(e, slot):
        return (pltpu.make_async_copy(w1_hbm.at[layer, e], w1buf.at[slot], sem.at[0, slot]),
                pltpu.make_async_copy(w2_hbm.at[layer, e], w2buf.at[slot], sem.at[1, slot]))

    @pl.when(used & new_expert)
    def _():
        slot = slot_ref[t]

        @pl.when(t == 0)
        def _():
            for cp in weight_copies(te_ref[t], slot):
                cp.start()

        for cp in weight_copies(te_ref[t], slot):
            cp.wait()

        @pl.when(nxt_ref[t] >= 0)
        def _():
            for cp in weight_copies(nxt_ref[t], 1 - slot):
                cp.start()

        for c in range(n_col_tiles):
            cols = slice(c * GLU_TILE, (c + 1) * GLU_TILE)
            w1p_ref[:, cols] = _dot(w1buf[slot, :, cols].astype(BF16), perm_ref[...]).astype(BF16)
        w2b_ref[...] = w2buf[slot].astype(BF16)

    def swiglu_rows(n_rows):
        lo, hi = _unpack_bf16_pair(xs_ref[0:n_rows, :])
        x = jnp.concatenate([lo, hi], axis=1)
        out = b2_ref[0]
        for j in range(n_col_tiles * LANES // EXPERT_CHUNK):
            hcols = slice(2 * j * EXPERT_CHUNK, 2 * (j + 1) * EXPERT_CHUNK)
            hh = _dot(x, w1p_ref[:, hcols]) + b1_ref[0, :, hcols]
            acts = []
            for c in range(2 * EXPERT_CHUNK // GLU_TILE):
                x_glu = jnp.minimum(hh[:, c * GLU_TILE:c * GLU_TILE + LANES], SWIGLU_LIMIT)
                x_lin = jnp.clip(hh[:, c * GLU_TILE + LANES:(c + 1) * GLU_TILE], -SWIGLU_LIMIT, SWIGLU_LIMIT)
                acts.append((x_glu * _sigmoid(SWIGLU_ALPHA * x_glu) * (x_lin + 1.0)).astype(BF16))
            out = out + _dot(jnp.concatenate(acts, axis=1), w2b_ref[j * EXPERT_CHUNK:(j + 1) * EXPERT_CHUNK, :])
        half = out.shape[1] // 2
        ys_ref[0:n_rows, :] = _pack_bf16_pair(out[:, :half], out[:, half:])
        if n_rows < ys_ref.shape[0]:
            ys_ref[n_rows:, :] = jnp.zeros((ys_ref.shape[0] - n_rows, ys_ref.shape[1]), ys_ref.dtype)

    tile_rows = xs_ref.shape[0]
    quarter = tile_rows // 4
    for n_q in range(1, 5):
        lo_rows, hi_rows = (n_q - 1) * quarter, n_q * quarter
        fits = (rows_ref[t] > lo_rows) & (rows_ref[t] <= hi_rows) if n_q > 1 else rows_ref[t] <= hi_rows
        pl.when(used & fits)(functools.partial(swiglu_rows, hi_rows))

    @pl.when(t >= nu_ref[0])
    def _():
        ys_ref[...] = jnp.zeros_like(ys_ref)


def _experts(xs, tile_expert, n_used, slot, nxt, tile_rows, w1_all, b1, w2_all, b2, layer):
    n_rows, dw = xs.shape
    tm = EXPERT_TILE
    _, n_exp, d, de2 = w1_all.shape
    src = np.arange(GLU_TILE)
    dst = np.where(src % 2 == 0, src // 2, LANES + src // 2)
    perm = jnp.asarray(dst[:, None] == np.arange(GLU_TILE)[None, :], BF16)
    b1p = b1.reshape(n_exp, de2 // GLU_TILE, LANES, 2).transpose(0, 1, 3, 2).reshape(n_exp, 1, de2)
    row = lambda t, te, nu, sl, nx, nr: (jnp.maximum(jnp.minimum(t, nu[0] - 1), 0), 0)
    exp3 = lambda t, te, nu, sl, nx, nr: (te[t], 0, 0)
    grid_spec = pltpu.PrefetchScalarGridSpec(
        num_scalar_prefetch=5,
        grid=(n_rows // tm,),
        in_specs=[
            pl.BlockSpec((tm, dw), row),
            pl.BlockSpec((1, 1, de2), exp3),
            pl.BlockSpec((1, 1, d), exp3),
            pl.BlockSpec((GLU_TILE, GLU_TILE), lambda t, te, nu, sl, nx, nr: (0, 0)),
            pl.BlockSpec(memory_space=pl.ANY),
            pl.BlockSpec(memory_space=pl.ANY),
        ],
        out_specs=pl.BlockSpec((tm, d // 2), lambda t, te, nu, sl, nx, nr: (t, 0)),
        scratch_shapes=[pltpu.VMEM((2, d, de2), F32), pltpu.VMEM((2, de2 // 2, d), F32),
                        pltpu.VMEM((d, de2), BF16), pltpu.VMEM((de2 // 2, d), BF16),
                        pltpu.SemaphoreType.DMA((2, 2))],
    )
    return pl.pallas_call(
        functools.partial(_experts_kernel, layer=layer),
        grid_spec=grid_spec,
        out_shape=jax.ShapeDtypeStruct((n_rows, d // 2), jnp.uint32),
        compiler_params=_params("arbitrary"),
        name="moe_experts",
    )(tile_expert, n_used, slot, nxt, tile_rows, xs, b1p, b2.reshape(n_exp, 1, d), perm, w1_all, w2_all)


SC_WINDOW = 128


def _sc_worker_rows(n_rows):
    sc = plsc.get_sparse_core_info()
    per_worker = n_rows // (sc.num_cores * sc.num_subcores)
    worker = lax.axis_index("subcore") * sc.num_cores + lax.axis_index("core")
    return worker * per_worker, per_worker


def _sc_mesh():
    return plsc.VectorSubcoreMesh(core_axis_name="core", subcore_axis_name="subcore")


def _sc_gather_rows(table, idx):
    n_idx, width = idx.shape[0], table.shape[1]

    @functools.partial(
        pl.kernel, out_type=jax.ShapeDtypeStruct((n_idx, width), table.dtype), mesh=_sc_mesh(),
        scratch_types=[pltpu.VMEM((SC_WINDOW,), jnp.int32), pltpu.VMEM((SC_WINDOW, width), table.dtype)],
        name="sc_gather_rows")
    def gather(table_hbm, idx_hbm, out_hbm, idx_v, rows_v):
        first, count = _sc_worker_rows(n_idx)

        @pl.loop(0, count // SC_WINDOW)
        def _(c):
            rows = pl.ds(first + c * SC_WINDOW, SC_WINDOW)
            pltpu.sync_copy(idx_hbm.at[rows], idx_v)
            pltpu.sync_copy(table_hbm.at[idx_v], rows_v)
            pltpu.sync_copy(rows_v, out_hbm.at[rows])

    return gather(table, idx)


def _sc_scatter_rows(rows, pos, n_out):
    n, width = rows.shape
    k = pos.shape[0]

    @functools.partial(
        pl.kernel, out_type=jax.ShapeDtypeStruct((n_out, width), rows.dtype), mesh=_sc_mesh(),
        scratch_types=[pltpu.VMEM((k, SC_WINDOW), jnp.int32), pltpu.VMEM((SC_WINDOW, width), rows.dtype)],
        name="sc_scatter_rows")
    def scatter(rows_hbm, pos_hbm, out_hbm, pos_v, rows_v):
        first, count = _sc_worker_rows(n)

        @pl.loop(0, count // SC_WINDOW)
        def _(c):
            src = pl.ds(first + c * SC_WINDOW, SC_WINDOW)
            pltpu.sync_copy(pos_hbm.at[:, src], pos_v)
            pltpu.sync_copy(rows_hbm.at[src], rows_v)
            for j in range(k):
                pltpu.sync_copy(rows_v, out_hbm.at[pos_v.at[j]])

    return scatter(rows, pos)


def _combine_dense_kernel(x_ref, gate_ref, w_ref, fg_ref, *rest, final_norm):
    y_refs, o_ref = rest[:TOP_K], rest[TOP_K]
    w = w_ref[...].T
    y_lo = y_hi = None
    for j in range(TOP_K):
        lo, hi = _unpack_bf16_pair(y_refs[j][...])
        wj = w[:, j:j + 1]
        y_lo = wj * lo if j == 0 else y_lo + wj * lo
        y_hi = wj * hi if j == 0 else y_hi + wj * hi
    out = x_ref[...] + gate_ref[0] * jnp.concatenate([y_lo, y_hi], axis=1)
    if final_norm:
        out = out * lax.rsqrt(jnp.mean(out * out, axis=-1, keepdims=True) + NORM_EPS) * fg_ref[...]
    o_ref[...] = out


def _combine_dense(x2, mod_l, gate_idx, seq, top_w, y4, final_g, final_norm, tm=1024):
    n, d = x2.shape
    per_batch = seq // tm
    blocks = n // tm
    slot_spec = lambda j: pl.BlockSpec((tm, d // 2), lambda i: (j * blocks + i, 0))
    return pl.pallas_call(
        functools.partial(_combine_dense_kernel, final_norm=final_norm),
        grid=(blocks,),
        in_specs=[
            pl.BlockSpec((tm, d), lambda i: (i, 0)),
            pl.BlockSpec((1, 1, d), lambda i: (i // per_batch, 0, gate_idx)),
            pl.BlockSpec((top_w.shape[0], tm), lambda i: (0, i)),
            pl.BlockSpec((1, d), lambda i: (0, 0)),
        ] + [slot_spec(j) for j in range(TOP_K)],
        out_specs=pl.BlockSpec((tm, d), lambda i: (i, 0)),
        out_shape=jax.ShapeDtypeStruct((n, d), F32),
        compiler_params=_params("parallel"),
        name="moe_combine",
    )(x2, mod_l, top_w, final_g.reshape(1, d), *([y4] * TOP_K))


def _moe_ffn(x2, norm_g, mod_l, seq, router_w, router_b, w1_all, b1, w2_all, b2, layer, final_g, final_norm):
    n, d = x2.shape
    n_exp = router_w.shape[1]
    tile = EXPERT_TILE
    hp, idx_f, top_w, rank_f, counts = _router(x2, norm_g, mod_l, 3, seq, router_w, router_b)
    counts = counts[:, 0].astype(jnp.int32)
    padded = (counts + tile - 1) // tile * tile
    ends = jnp.cumsum(padded)
    starts = ends - padded
    idx = idx_f[:TOP_K].astype(jnp.int32)
    expert_ids = jnp.arange(n_exp, dtype=jnp.int32)
    start_of = jnp.sum(jnp.where(idx[..., None] == expert_ids, starts, 0), axis=-1)
    pos_slot_major = start_of + rank_f[:TOP_K].astype(jnp.int32)
    n_rows = n * TOP_K + n_exp * tile
    n_tiles = n_rows // tile
    tile_start = jnp.arange(n_tiles, dtype=jnp.int32) * tile
    tile_expert = jnp.sum((ends[None, :] <= tile_start[:, None]).astype(jnp.int32), axis=1)
    tile_expert = jnp.minimum(tile_expert, n_exp - 1)
    n_used = (ends[-1:] // tile).astype(jnp.int32)
    nonempty = counts > 0
    ordinal = jnp.cumsum(nonempty.astype(jnp.int32)) - 1
    later = nonempty[None, :] & (expert_ids[None, :] > expert_ids[:, None])
    nxt_e = jnp.min(jnp.where(later, expert_ids[None, :], n_exp), axis=1)
    nxt_e = jnp.where(nxt_e == n_exp, -1, nxt_e)
    slot = (ordinal[tile_expert] % 2).astype(jnp.int32)
    nxt = nxt_e[tile_expert].astype(jnp.int32)
    xs = _sc_scatter_rows(hp, pos_slot_major, n_rows)
    filled = jnp.clip((starts + counts)[tile_expert] - tile_start, 0, tile).astype(jnp.int32)
    ys = _experts(xs, tile_expert, n_used, slot, nxt, filled, w1_all, b1, w2_all, b2, layer)
    y4 = _sc_gather_rows(ys, pos_slot_major.reshape(-1))
    return _combine_dense(x2, mod_l, 5, seq, top_w, y4, final_g, final_norm)


def kernel(x, c, ada_w, ada_b, norm_mix_g, norm_ffn_g, router_w, router_b, moe_w1, moe_b1, moe_w2, moe_b2, ab_w_in, ab_w_out, hgrn_lb, hgrn_norm_g, rwkv_mu, rwkv_w0, rwkv_w2, rwkv_a0, rwkv_a2, rwkv_g2, rwkv_k_k, rwkv_k_a, rwkv_r_k, rwkv_ln_g, rwkv_ln_b, attn_w_in, attn_w_out, attn_lambda, attn_subln_g, rel_bias_table, final_norm_g):
    batch, seq, d = x.shape
    n = batch * seq
    depth = ada_w.shape[0]
    x2 = x.reshape(n, d)
    mod = _adaln(c, ada_w, ada_b)
    lb_all = jnp.cumsum(jax.nn.softmax(hgrn_lb.astype(F32), axis=1), axis=1)
    for layer in range(depth):
        mod_l = mod[layer].reshape(batch, 1, 6 * d)
        j = layer // 2
        if layer % 2 == 0:
            a_cols = 5 * (d // 2)
            w_in = ab_w_in[j].astype(BF16)
            p_a, p_b = _normmod_proj(x2, norm_mix_g[layer], mod_l, 0, seq,
                                     [w_in[:, :a_cols], w_in[:, a_cols:]], [F32, F32])
            y_a = _hgrn2(p_a, lb_all[:, j], hgrn_norm_g[j], batch, seq)
            y_b = _rwkv7(p_b, rwkv_mu[j], rwkv_w0[j], rwkv_w2[j], rwkv_a0[j], rwkv_a2[j], rwkv_g2[j],
                         rwkv_k_k[j], rwkv_k_a[j], rwkv_r_k[j], rwkv_ln_g[j], rwkv_ln_b[j], batch, seq)
            w_out = ab_w_out[j].astype(BF16)
            x2 = _outproj_residual(x2, mod_l, 2, seq, [y_a, y_b], [w_out[:d // 2], w_out[d // 2:]])
        else:
            (qkv,) = _normmod_proj(x2, norm_mix_g[layer], mod_l, 0, seq, [attn_w_in[j].astype(BF16)], [BF16])
            lam = attn_lambda[j].astype(F32)
            lam_init = 0.8 - 0.6 * math.exp(-0.3 * layer)
            lam_full = jnp.exp(jnp.sum(lam[0] * lam[1])) - jnp.exp(jnp.sum(lam[2] * lam[3])) + lam_init
            o = _diff_attention(qkv, rel_bias_table, lam_full, attn_subln_g[j], lam_init, batch, seq)
            x2 = _outproj_residual(x2, mod_l, 2, seq, [o], [attn_w_out[j].astype(BF16)])
        x2 = _moe_ffn(x2, norm_ffn_g[layer], mod_l, seq, router_w[layer], router_b[layer], moe_w1,
                      moe_b1[layer], moe_w2, moe_b2[layer], layer, final_norm_g, layer == depth - 1)
    return x2.reshape(batch, seq, d)
```

```python
import functools
import math

import jax
import jax.numpy as jnp
import numpy as np
from jax import lax
from jax.experimental import pallas as pl
from jax.experimental.pallas import tpu as pltpu
from jax.experimental.pallas import tpu_sc as plsc

F32 = jnp.float32
BF16 = jnp.bfloat16
HIGHEST = lax.Precision.HIGHEST

NORM_EPS = 1e-6

A_HEAD_DIM = 128
A_CHUNK = 32
A_GROUP = 512
A_HEADS_PER_STEP = 2
B_HEAD_DIM = 64
B_CHUNK = 64
B_STEPS = 2
B_LN_EPS = 1e-5 * B_HEAD_DIM
C_HEAD_DIM = 64
REL_BUCKETS = 32
REL_MAX_DISTANCE = 128
TOP_K = 4
SWIGLU_LIMIT = 7.0
SWIGLU_ALPHA = 1.702

V7X_VMEM_BYTES = 64 * 1024 * 1024
VMEM_LIMIT = V7X_VMEM_BYTES - 8 * 1024 * 1024
LANES = 128


def _params(*sem):
    return pltpu.CompilerParams(dimension_semantics=sem, vmem_limit_bytes=VMEM_LIMIT)


def _sigmoid(x):
    return 1.0 / (1.0 + jnp.exp(-x))


def _silu(x):
    return x * _sigmoid(x)


def _dot(a, b):
    return jnp.dot(a, b, preferred_element_type=F32)


def _dot_nt(a, b):
    return lax.dot_general(a, b, (((1,), (1,)), ((), ())), preferred_element_type=F32)


def _dot_tn(a, b):
    return lax.dot_general(a, b, (((0,), (0,)), ((), ())), preferred_element_type=F32)


def _adaln_kernel(c_ref, w_ref, b_ref, o_ref):
    cond = _silu(c_ref[...])
    o_ref[0] = jnp.dot(cond, w_ref[0], precision=HIGHEST, preferred_element_type=F32) + b_ref[0]


def _adaln(c, ada_w, ada_b):
    n_layers, d, n_out = ada_w.shape
    batch = c.shape[0]
    tn = 1536
    return pl.pallas_call(
        _adaln_kernel,
        grid=(n_layers, n_out // tn),
        in_specs=[
            pl.BlockSpec((batch, d), lambda l, j: (0, 0)),
            pl.BlockSpec((1, d, tn), lambda l, j: (l, 0, j)),
            pl.BlockSpec((1, 1, tn), lambda l, j: (l, 0, j)),
        ],
        out_specs=pl.BlockSpec((1, batch, tn), lambda l, j: (l, 0, j)),
        out_shape=jax.ShapeDtypeStruct((n_layers, batch, n_out), F32),
        compiler_params=_params("parallel", "parallel"),
        name="adaln",
    )(c, ada_w, ada_b.reshape(n_layers, 1, n_out))


def _modulated_norm(x, g, shift, scale):
    y = x * lax.rsqrt(jnp.mean(x * x, axis=-1, keepdims=True) + NORM_EPS)
    return (y * g) * (1.0 + scale) + shift


def _normmod_proj_kernel(x_ref, g_ref, sh_ref, sc_ref, *rest, n_w):
    w_refs, o_refs = rest[:n_w], rest[n_w:]
    h = _modulated_norm(x_ref[...], g_ref[...], sh_ref[0], sc_ref[0]).astype(BF16)
    for w_ref, o_ref in zip(w_refs, o_refs):
        o_ref[...] = _dot(h, w_ref[...]).astype(o_ref.dtype)


def _normmod_proj(x2, g, mod_l, shift_idx, seq, weights, out_dtypes, tm=512):
    n, d = x2.shape
    per_batch = seq // tm
    in_specs = [
        pl.BlockSpec((tm, d), lambda i: (i, 0)),
        pl.BlockSpec((1, d), lambda i: (0, 0)),
        pl.BlockSpec((1, 1, d), lambda i: (i // per_batch, 0, shift_idx)),
        pl.BlockSpec((1, 1, d), lambda i: (i // per_batch, 0, shift_idx + 1)),
    ]
    in_specs += [pl.BlockSpec(w.shape, lambda i: (0, 0)) for w in weights]
    out_specs = [pl.BlockSpec((tm, w.shape[1]), lambda i: (i, 0)) for w in weights]
    out_shape = [jax.ShapeDtypeStruct((n, w.shape[1]), dt) for w, dt in zip(weights, out_dtypes)]
    return pl.pallas_call(
        functools.partial(_normmod_proj_kernel, n_w=len(weights)),
        grid=(n // tm,),
        in_specs=in_specs,
        out_specs=out_specs,
        out_shape=out_shape,
        compiler_params=_params("parallel"),
        name="normmod_proj",
    )(x2, g.reshape(1, d), mod_l, mod_l, *weights)


def _outproj_kernel(x_ref, gate_ref, *rest, n_y):
    y_refs, w_refs, o_ref = rest[:n_y], rest[n_y:2 * n_y], rest[2 * n_y]
    acc = _dot(y_refs[0][...], w_refs[0][...])
    for y_ref, w_ref in zip(y_refs[1:], w_refs[1:]):
        acc += _dot(y_ref[...], w_ref[...])
    o_ref[...] = x_ref[...] + gate_ref[0] * acc


def _outproj_residual(x2, mod_l, gate_idx, seq, ys, ws, tm=1024):
    n, d = x2.shape
    per_batch = seq // tm
    in_specs = [
        pl.BlockSpec((tm, d), lambda i: (i, 0)),
        pl.BlockSpec((1, 1, d), lambda i: (i // per_batch, 0, gate_idx)),
    ]
    in_specs += [pl.BlockSpec((tm, y.shape[1]), lambda i: (i, 0)) for y in ys]
    in_specs += [pl.BlockSpec(w.shape, lambda i: (0, 0)) for w in ws]
    return pl.pallas_call(
        functools.partial(_outproj_kernel, n_y=len(ys)),
        grid=(n // tm,),
        in_specs=in_specs,
        out_specs=pl.BlockSpec((tm, d), lambda i: (i, 0)),
        out_shape=jax.ShapeDtypeStruct((n, d), F32),
        compiler_params=_params("parallel"),
        name="outproj_residual",
    )(x2, mod_l, *ys, *ws)


def _chunk_cumsum(x, chunk, reverse):
    rows = x.shape[0]
    pos = lax.broadcasted_iota(jnp.int32, x.shape, 0) % chunk
    s = 1
    while s < chunk:
        if reverse:
            x = x + jnp.where(pos < chunk - s, pltpu.roll(x, rows - s, axis=0), 0.0)
        else:
            x = x + jnp.where(pos >= s, pltpu.roll(x, s, axis=0), 0.0)
        s *= 2
    return x


def _hgrn_groups(slabs):
    g_rows, dk = slabs[0][0].shape
    n_chunks = g_rows // A_CHUNK
    ti = lax.broadcasted_iota(jnp.int32, (A_CHUNK, A_CHUNK), 0)
    si = lax.broadcasted_iota(jnp.int32, (A_CHUNK, A_CHUNK), 1)
    chunk_rows = [slice(c * A_CHUNK, (c + 1) * A_CHUNK) for c in range(n_chunks)]
    prep = []
    for q, f, v, lb, st, reverse in slabs:
        fg = lb + (1.0 - lb) * _sigmoid(f)
        k = 1.0 - fg
        b = _chunk_cumsum(jnp.log(fg), A_CHUNK, reverse)
        b3 = b.reshape(n_chunks, A_CHUNK, dk)
        edge = b3[:, 0:1, :] if reverse else b3[:, A_CHUNK - 1:A_CHUNK, :]
        prep.append(dict(
            q_in=(q * jnp.exp(b)).astype(BF16), k_in=(k * jnp.exp(-b)).astype(BF16),
            k_st=(k.reshape(n_chunks, A_CHUNK, dk) * jnp.exp(edge - b3)).astype(BF16),
            decay=jnp.exp(edge), vb=v.astype(BF16), st=st, reverse=reverse,
            mask=(si >= ti) if reverse else (si <= ti)))
    for p in prep:
        p["scores"] = [_dot_nt(p["q_in"][r], p["k_in"][r]) for r in chunk_rows]
        p["dstate"] = [_dot_tn(p["vb"][r], p["k_st"][c]) for c, r in enumerate(chunk_rows)]
    for p in prep:
        p["intra"] = [_dot(jnp.where(p["mask"], s, 0.0).astype(BF16), p["vb"][r])
                      for s, r in zip(p["scores"], chunk_rows)]
    results = []
    for p in prep:
        st = p["st"]
        outs = [None] * n_chunks
        for c in (range(n_chunks - 1, -1, -1) if p["reverse"] else range(n_chunks)):
            outs[c] = p["intra"][c] + _dot_nt(p["q_in"][chunk_rows[c]], st.astype(BF16))
            st = st * p["decay"][c] + p["dstate"][c]
        results.append((jnp.concatenate(outs, axis=0), st))
    return results


def _hgrn_kernel(q_ref, ff_ref, fb_ref, i_ref, g_ref, lb_ref, ng_ref, o_ref, of_ref, ob_ref):
    seq = q_ref.shape[0]
    dk = A_HEAD_DIM
    n_heads = q_ref.shape[1] // dk
    n_groups = seq // A_GROUP
    head_cols = [slice(h * dk, (h + 1) * dk) for h in range(n_heads)]

    def body(j, states):
        rf = pl.ds(pl.multiple_of(j * A_GROUP, A_GROUP), A_GROUP)
        rb = pl.ds(pl.multiple_of((n_groups - 1 - j) * A_GROUP, A_GROUP), A_GROUP)
        slabs = []
        for h, cols in enumerate(head_cols):
            slabs.append((_silu(q_ref[rf, cols]), ff_ref[rf, cols], i_ref[rf, cols], lb_ref[0:1, cols],
                          states[2 * h], False))
            slabs.append((_silu(q_ref[rb, cols]), fb_ref[rb, cols], i_ref[rb, cols], lb_ref[1:2, cols],
                          states[2 * h + 1], True))
        results = _hgrn_groups(slabs)
        for h, cols in enumerate(head_cols):
            of_ref[rf, cols] = results[2 * h][0]
            ob_ref[rb, cols] = results[2 * h + 1][0]
        return tuple(r[1] for r in results)

    zero = jnp.zeros((dk, dk), F32)
    lax.fori_loop(0, n_groups, body, (zero,) * (2 * n_heads))

    def finish(j, carry):
        r = pl.ds(pl.multiple_of(j * A_GROUP, A_GROUP), A_GROUP)
        for cols in head_cols:
            o = of_ref[r, cols] + ob_ref[r, cols]
            o = o * lax.rsqrt(jnp.mean(o * o, axis=-1, keepdims=True) + NORM_EPS) * ng_ref[...]
            o_ref[r, cols] = (o * _silu(g_ref[r, cols])).astype(o_ref.dtype)
        return carry

    lax.fori_loop(0, n_groups, finish, 0)


def _hgrn2(p_a, lb, norm_g, batch, seq):
    n = p_a.shape[0]
    width = p_a.shape[1] // 5
    cols = A_HEADS_PER_STEP * A_HEAD_DIM
    steps = width // cols
    sect = lambda s: pl.BlockSpec((seq, cols), lambda b, h, s=s: (b, s * steps + h))
    return pl.pallas_call(
        _hgrn_kernel,
        grid=(batch, steps),
        in_specs=[sect(0), sect(1), sect(2), sect(3), sect(4),
                  pl.BlockSpec((2, cols), lambda b, h: (0, h)),
                  pl.BlockSpec((1, A_HEAD_DIM), lambda b, h: (0, 0))],
        out_specs=pl.BlockSpec((seq, cols), lambda b, h: (b, h)),
        out_shape=jax.ShapeDtypeStruct((n, width), BF16),
        scratch_shapes=[pltpu.VMEM((seq, cols), F32), pltpu.VMEM((seq, cols), F32)],
        compiler_params=_params("parallel", "parallel"),
        name="hgrn2",
    )(p_a, p_a, p_a, p_a, p_a, lb, norm_g.reshape(1, A_HEAD_DIM))


def _softplus(z):
    return jnp.maximum(z, 0.0) + jnp.log(1.0 + jnp.exp(-jnp.abs(z)))


def _rwkv_prep_kernel(p_ref, prev_ref, next_ref, mu_ref, w0_ref, w2_ref, a0_ref, a2_ref, g2_ref,
                      kk_ref, ka_ref, rk_ref, ones_ref, tri_ref,
                      r_out, k_out, v_out, kkn_out, b_out, cumf_out, cumb_out, g_out, bonus_out):
    i = pl.program_id(1)
    last = pl.num_programs(1) - 1
    ts = p_ref.shape[0]
    width = a0_ref.shape[1]
    heads = width // B_HEAD_DIM
    p = p_ref[...]
    prev_row = jnp.where(i == 0, 0.0, prev_ref[7:8, :])
    next_row = jnp.where(i == last, 0.0, next_ref[0:1, :])
    row = lax.broadcasted_iota(jnp.int32, (ts, 1), 0)
    up = jnp.where(row == 0, prev_row, pltpu.roll(p, 1, axis=0))
    dn = jnp.where(row == ts - 1, next_row, pltpu.roll(p, ts - 1, axis=0))
    p = p + mu_ref[...] * (0.5 * (up + dn) - p)

    r = p[:, 0:width]
    k = p[:, width:2 * width]
    v = p[:, 2 * width:3 * width]
    o = 3 * width
    wlo_f = p[:, o:o + 64]
    wlo_b = p[:, o + 64:o + 128]
    alo = p[:, o + 128:o + 192]
    glo = p[:, o + 192:o + 320]

    def log_decay(wlo, d):
        z = w0_ref[d:d + 1, :] + _dot(jnp.tanh(wlo).astype(BF16), w2_ref[d])
        w = -_softplus(-z) - 0.5
        return -jnp.exp(w)

    def split_dot(sel, t, right):
        hi = t.astype(BF16)
        lo = (t - hi.astype(F32)).astype(BF16)
        return (_dot(hi, sel) + _dot(lo, sel)) if right else (_dot(sel, hi) + _dot(sel, lo))

    cum_f = split_dot(tri_ref[0], log_decay(wlo_f, 0), False)
    cum_b = split_dot(tri_ref[1], log_decay(wlo_b, 1), False)
    a = _sigmoid(a0_ref[...] + _dot(alo.astype(BF16), a2_ref[...]))
    g = _dot(_sigmoid(glo).astype(BF16), g2_ref[...])
    kk = k * kk_ref[...]
    head_sum = lambda t: split_dot(ones_ref[...], t, True)

    kk_n = kk / jnp.maximum(jnp.sqrt(head_sum(kk * kk)), 1e-12)
    k_mod = k * (1.0 + (a - 1.0) * ka_ref[...])
    bonus = head_sum(r * k_mod * rk_ref[...]) * v
    g_out[...] = g
    bonus_out[...] = bonus
    for h in range(heads):
        sl = slice(h * B_HEAD_DIM, (h + 1) * B_HEAD_DIM)
        r_out[0, h] = r[:, sl]
        k_out[0, h] = k_mod[:, sl]
        v_out[0, h] = v[:, sl]
        kkn_out[0, h] = kk_n[:, sl]
        b_out[0, h] = (kk_n * a)[:, sl]
        cumf_out[0, h] = cum_f[:, sl]
        cumb_out[0, h] = cum_b[:, sl]


def _rwkv_prep(p_b, mu, w0, w2, a0, a2, g2, k_k, k_a, r_k, batch, seq, ts=512):
    n, cols = p_b.shape
    width = a0.shape[0]
    heads = width // B_HEAD_DIM
    nblk = seq // ts
    rows8 = ts // 8
    head_id = np.arange(width) // B_HEAD_DIM
    ones = jnp.asarray(head_id[:, None] == head_id[None, :], BF16)
    full = lambda a: pl.BlockSpec(a.shape, lambda b, i: (0,) * a.ndim)
    vec = lambda a: a.reshape(1, -1)
    row = np.arange(ts)
    same_chunk = (row[:, None] // B_CHUNK) == (row[None, :] // B_CHUNK)
    tri = jnp.asarray(np.stack([same_chunk & (row[None, :] <= row[:, None]),
                                same_chunk & (row[None, :] >= row[:, None])]), BF16)
    args = [vec(mu), w0, w2.astype(BF16), vec(a0), a2.astype(BF16), g2.astype(BF16),
            vec(k_k), vec(k_a), vec(r_k), ones, tri]
    hm = pl.BlockSpec((1, heads, ts, B_HEAD_DIM), lambda b, i: (b, 0, i, 0))
    tokm = pl.BlockSpec((ts, width), lambda b, i: (b * nblk + i, 0))
    hm_shape = jax.ShapeDtypeStruct((batch, heads, seq, B_HEAD_DIM), F32)
    tok_shape = jax.ShapeDtypeStruct((n, width), F32)
    return pl.pallas_call(
        _rwkv_prep_kernel,
        grid=(batch, nblk),
        in_specs=[
            pl.BlockSpec((ts, cols), lambda b, i: (b * nblk + i, 0)),
            pl.BlockSpec((8, cols), lambda b, i: (jnp.maximum((b * nblk + i) * rows8 - 1, 0), 0)),
            pl.BlockSpec((8, cols), lambda b, i: (jnp.minimum((b * nblk + i + 1) * rows8, n // 8 - 1), 0)),
        ] + [full(a) for a in args],
        out_specs=[hm] * 7 + [tokm] * 2,
        out_shape=[hm_shape] * 7 + [tok_shape] * 2,
        compiler_params=_params("parallel", "parallel"),
        name="rwkv_prep",
    )(p_b, p_b, p_b, *args)


def _rwkv_chunks(chains):
    c, d = chains[0][0].shape
    ti = lax.broadcasted_iota(jnp.int32, (c, c), 0)
    si = lax.broadcasted_iota(jnp.int32, (c, c), 1)
    eye = jnp.where(ti == si, 1.0, 0.0)
    masks = {rev: (((si > ti), (si >= ti)) if rev else ((si < ti), (si <= ti))) for rev in (False, True)}
    t2 = lax.broadcasted_iota(jnp.int32, (c, 2 * c), 0)
    s2 = lax.broadcasted_iota(jnp.int32, (c, 2 * c), 1)
    s2 = jnp.where(s2 >= c, s2 - c, s2)
    incl2 = {False: s2 <= t2, True: s2 >= t2}

    row = lax.broadcasted_iota(jnp.int32, (c, d), 0)
    st = []
    for r, k, v, kk, b, cum, rev in chains:
        before = (jnp.where(row == c - 1, 0.0, pltpu.roll(cum, c - 1, axis=0)) if rev
                  else jnp.where(row == 0, 0.0, pltpu.roll(cum, 1, axis=0)))
        grow = jnp.exp(-cum)
        kt = (kk * jnp.exp(before)).astype(BF16)
        rt = r * jnp.exp(cum)
        b_ = (b * grow).astype(BF16)
        kb = jnp.concatenate([(k * grow).astype(BF16), b_], axis=0)
        st.append(dict(vb=v.astype(BF16), kt=kt, rt=rt, b_=b_, kb=kb,
                       decay=jnp.exp(cum[0:1, :] if rev else cum[c - 1:c, :]), rev=rev,
                       lhs=jnp.concatenate([kt, rt.astype(BF16)], axis=0)))
    for s in st:
        s["big"] = _dot_nt(s["lhs"], s["kb"])
    for s in st:
        strict, incl = masks[s["rev"]]
        big = s["big"]
        a_ab = jnp.where(strict, big[:c, c:], 0.0)
        s["a_ak"] = jnp.where(strict, big[:c, :c], 0.0).astype(BF16)
        s["p"] = jnp.where(incl2[s["rev"]], big[c:], 0.0).astype(BF16)
        s["m"] = eye - a_ab
        s["ab"] = a_ab.astype(BF16)
    for s in st:
        s["aj"] = _dot(s["ab"], s["ab"])
        s["av"] = _dot(s["a_ak"], s["vb"])
    span = 2
    while span < c:
        span *= 2
        for s in st:
            ajb = s["aj"].astype(BF16)
            if span < c:
                both = _dot(jnp.concatenate([ajb, s["m"].astype(BF16)], axis=0), ajb)
                s["aj"], s["m"] = both[:c], s["m"] + both[c:]
            else:
                s["m"] = s["m"] + _dot(s["m"].astype(BF16), ajb)
    for s in st:
        wu = _dot(s["m"].astype(BF16), jnp.concatenate([s["kt"], s["av"].astype(BF16)], axis=1))
        s["nwu"] = -wu.astype(BF16)
    for s in st:
        nw, nu0, vb = s["nwu"][:, :d], s["nwu"][:, d:], s["vb"]
        rhs = jnp.concatenate([jnp.concatenate([vb, jnp.zeros_like(vb)], axis=1),
                               jnp.concatenate([nu0, nw], axis=1)], axis=0)
        s["yq"] = _dot(s["p"], rhs)
        s["h_t"] = _dot_tn(jnp.concatenate([vb, nu0], axis=0), s["kb"])
        s["nbw"] = _dot_tn(s["b_"], nw)
    return [dict(q=(s["rt"] + s["yq"][:, d:]).astype(BF16), y1=s["yq"][:, :d], nbw=s["nbw"].astype(BF16),
                 h_t=s["h_t"], decay=s["decay"]) for s in st]


def _rwkv_advance(states, steps):
    ttbs = [tt.astype(BF16) for tt in states]
    ys = [_dot_nt(s["q"], ttb) + s["y1"] for s, ttb in zip(steps, ttbs)]
    new = [(tt + _dot_nt(ttb, s["nbw"]) + s["h_t"]) * s["decay"] for tt, ttb, s in zip(states, ttbs, steps)]
    return ys, new


def _rwkv_scan_kernel(rf, kf, vf, kkf, bf, lwf, rb, kb, vb, kkb, bb, lwb, yf_ref, yb_ref, tf_ref, tb_ref):
    heads, ts = rf.shape[1], rf.shape[2]
    n_chunks = ts // B_CHUNK

    @pl.when(pl.program_id(1) == 0)
    def _():
        tf_ref[...] = jnp.zeros_like(tf_ref)
        tb_ref[...] = jnp.zeros_like(tb_ref)

    def body(j, carry):
        rows_f = [pl.ds(pl.multiple_of((j * B_STEPS + s) * B_CHUNK, B_CHUNK), B_CHUNK) for s in range(B_STEPS)]
        rows_b = [pl.ds(pl.multiple_of((n_chunks - 1 - j * B_STEPS - s) * B_CHUNK, B_CHUNK), B_CHUNK)
                  for s in range(B_STEPS)]
        chains = []
        for s in range(B_STEPS):
            for h in range(heads):
                chains.append([ref[0, h, rows_f[s], :] for ref in (rf, kf, vf, kkf, bf, lwf)] + [False])
                chains.append([ref[0, h, rows_b[s], :] for ref in (rb, kb, vb, kkb, bb, lwb)] + [True])
        solved = _rwkv_chunks(chains)
        states = []
        for h in range(heads):
            states += [tf_ref[h], tb_ref[h]]
        for s in range(B_STEPS):
            ys, states = _rwkv_advance(states, solved[s * 2 * heads:(s + 1) * 2 * heads])
            for h in range(heads):
                yf_ref[0, h, rows_f[s], :] = ys[2 * h]
                yb_ref[0, h, rows_b[s], :] = ys[2 * h + 1]
        for h in range(heads):
            tf_ref[h], tb_ref[h] = states[2 * h], states[2 * h + 1]
        return carry

    lax.fori_loop(0, n_chunks // B_STEPS, body, 0)


def _rwkv_scan(r, k, v, kkn, b, lw_f, lw_b, ts=256):
    batch, heads, seq, d = r.shape
    nblk = seq // ts
    fwd = pl.BlockSpec((1, heads, ts, d), lambda bi, i: (bi, 0, i, 0))
    bwd = pl.BlockSpec((1, heads, ts, d), lambda bi, i: (bi, 0, nblk - 1 - i, 0))
    shape = jax.ShapeDtypeStruct(r.shape, F32)
    return pl.pallas_call(
        _rwkv_scan_kernel,
        grid=(batch, nblk),
        in_specs=[fwd] * 6 + [bwd] * 6,
        out_specs=[fwd, bwd],
        out_shape=[shape, shape],
        scratch_shapes=[pltpu.VMEM((heads, d, d), F32), pltpu.VMEM((heads, d, d), F32)],
        compiler_params=_params("parallel", "arbitrary"),
        name="rwkv_scan",
    )(r, k, v, kkn, b, lw_f, r, k, v, kkn, b, lw_b)


def _rwkv_post_kernel(yf_ref, yb_ref, g_ref, bonus_ref, lng_ref, lnb_ref, o_ref):
    heads = yf_ref.shape[1]
    outs = []
    for h in range(heads):
        y = yf_ref[0, h] + yb_ref[0, h]
        mean = jnp.mean(y, axis=-1, keepdims=True)
        var = jnp.mean(jnp.square(y - mean), axis=-1, keepdims=True)
        outs.append((y - mean) * lax.rsqrt(var + B_LN_EPS))
    yn = jnp.concatenate(outs, axis=1) * lng_ref[...] + lnb_ref[...]
    o_ref[...] = ((yn + bonus_ref[...]) * g_ref[...]).astype(o_ref.dtype)


def _rwkv_post(y_f, y_b, g, bonus, ln_g, ln_b, ts=1024):
    batch, heads, seq, d = y_f.shape
    n, width = g.shape
    nblk = seq // ts
    hm = pl.BlockSpec((1, heads, ts, d), lambda b, i: (b, 0, i, 0))
    tokm = pl.BlockSpec((ts, width), lambda b, i: (b * nblk + i, 0))
    vec = pl.BlockSpec((1, width), lambda b, i: (0, 0))
    return pl.pallas_call(
        _rwkv_post_kernel,
        grid=(batch, nblk),
        in_specs=[hm, hm, tokm, tokm, vec, vec],
        out_specs=tokm,
        out_shape=jax.ShapeDtypeStruct((n, width), BF16),
        compiler_params=_params("parallel", "parallel"),
        name="rwkv_post",
    )(y_f, y_b, g, bonus, ln_g.reshape(1, width), ln_b.reshape(1, width))


def _rwkv7(p_b, mu, w0, w2, a0, a2, g2, k_k, k_a, r_k, ln_g, ln_b, batch, seq):
    r, k, v, kkn, b, lw_f, lw_b, g, bonus = _rwkv_prep(p_b, mu, w0, w2, a0, a2, g2, k_k, k_a, r_k, batch, seq)
    y_f, y_b = _rwkv_scan(r, k, v, kkn, b, lw_f, lw_b)
    return _rwkv_post(y_f, y_b, g, bonus, ln_g, ln_b)


_BUCKET_EDGES = (0, 1, 2, 3, 4, 5, 6, 7, 8, 12, 16, 23, 32, 46, 64, 91)
ATTN_TK = 128
C_HEADS_PER_STEP = 4


def _bias_band_kernel(table_ref, o_ref):
    h = pl.program_id(0)
    n_d, tq, tk = o_ref.shape[1:]
    half = REL_BUCKETS // 2
    r = lax.broadcasted_iota(jnp.int32, (tq, tk), 0)
    c = lax.broadcasted_iota(jnp.int32, (tq, tk), 1)
    for d in range(n_d):
        rel = tk * (d - 1) + c - r
        n = jnp.abs(rel)
        vals = []
        for sign in range(2):
            val = jnp.full((tq, tk), table_ref[sign * half + half - 1, h], F32)
            for bkt in range(half - 2, -1, -1):
                val = jnp.where(n < _BUCKET_EDGES[bkt + 1], table_ref[sign * half + bkt, h], val)
            vals.append(val)
        o_ref[0, d] = jnp.where(rel > 0, vals[1], vals[0])


def _bias_band(rel_table, tq):
    heads = rel_table.shape[1]
    n_d = tq // ATTN_TK + 2
    return pl.pallas_call(
        _bias_band_kernel,
        grid=(heads,),
        in_specs=[pl.BlockSpec(memory_space=pltpu.SMEM)],
        out_specs=pl.BlockSpec((1, n_d, tq, ATTN_TK), lambda h: (h, 0, 0, 0)),
        out_shape=jax.ShapeDtypeStruct((heads, n_d, tq, ATTN_TK), F32),
        compiler_params=_params("parallel"),
        name="bias_band",
    )(rel_table)


def _diff_attn_kernel(table_ref, lam_ref, q_ref, k_ref, v_ref, band_ref, g_ref, o_ref, lg_ref, vext_ref, kt_ref,
                      *, lam_init):
    i = pl.program_id(2)
    tq = q_ref.shape[0]
    dh2 = 2 * C_HEAD_DIM
    n_heads = q_ref.shape[1] // dh2
    seq = k_ref.shape[0]
    n_d = band_ref.shape[1]
    n_kt = seq // ATTN_TK
    ratio = tq // ATTN_TK
    half = REL_BUCKETS // 2
    head_cols = [slice(hh * dh2, (hh + 1) * dh2) for hh in range(n_heads)]

    @pl.when(i == 0)
    def _():
        for hh, cols in enumerate(head_cols):
            kt_ref[hh] = k_ref[:, cols].astype(F32).T.astype(BF16)
            v = v_ref[:, cols]
            vext_ref[hh] = jnp.concatenate([v, jnp.ones_like(v)], axis=1)

    lane = lax.broadcasted_iota(jnp.int32, (1, dh2), 1)
    first_band = ratio * i - 1
    col = lax.broadcasted_iota(jnp.int32, (1, seq), 1)
    for hh, cols in enumerate(head_cols):
        h = pl.program_id(1) * n_heads + hh
        q = q_ref[:, cols] * jnp.asarray(C_HEAD_DIM ** -0.5, q_ref.dtype)
        zero = jnp.zeros_like(q)
        far_row = jnp.where(col < first_band * ATTN_TK, table_ref[half - 1, h],
                            jnp.where(col >= (first_band + n_d) * ATTN_TK, table_ref[REL_BUCKETS - 1, h],
                                      -jnp.inf))
        row_max = []
        for m in range(2):
            qm = jnp.where((lane >= m * C_HEAD_DIM) & (lane < (m + 1) * C_HEAD_DIM), q, zero)
            far = _dot(qm, kt_ref[hh]) + far_row
            mx = jnp.max(far, axis=-1, keepdims=True)
            lg_ref[hh, m] = far
            for d in (0, n_d - 1) + tuple(range(1, n_d - 1)):
                kt = first_band + d
                valid = (kt >= 0) & (kt < n_kt)
                c0 = pl.multiple_of(jnp.clip(kt, 0, n_kt - 1) * ATTN_TK, ATTN_TK)
                near = _dot(qm, kt_ref[hh, :, pl.ds(c0, ATTN_TK)]) + band_ref[hh, d]
                mx = jnp.maximum(mx, jnp.where(valid, jnp.max(near, axis=-1, keepdims=True), -jnp.inf))
                lg_ref[hh, m, :, pl.ds(c0, ATTN_TK)] = near
            row_max.append(mx)
        parts = []
        for m in range(2):
            e = jnp.exp(lg_ref[hh, m] - row_max[m]).astype(BF16)
            pv = _dot(e, vext_ref[hh])
            parts.append(pv[:, :dh2] * (1.0 / pv[:, dh2:dh2 + 1]))
        o = parts[0] - lam_ref[0] * parts[1]
        o = o * lax.rsqrt(jnp.mean(o * o, axis=-1, keepdims=True) + NORM_EPS) * g_ref[...]
        o_ref[:, cols] = (o * (1.0 - lam_init)).astype(o_ref.dtype)


def _diff_attention(qkv, rel_table, lam_full, subln_g, lam_init, batch, seq, tq=256):
    n = qkv.shape[0]
    dh2 = 2 * C_HEAD_DIM
    heads = qkv.shape[1] // (3 * dh2)
    nq = seq // tq
    per = C_HEADS_PER_STEP
    groups = heads // per
    cols = per * dh2
    band = _bias_band(rel_table, tq)
    return pl.pallas_call(
        functools.partial(_diff_attn_kernel, lam_init=lam_init),
        grid=(batch, groups, nq),
        in_specs=[
            pl.BlockSpec(memory_space=pltpu.SMEM),
            pl.BlockSpec(memory_space=pltpu.SMEM),
            pl.BlockSpec((tq, cols), lambda b, h, i: (b * nq + i, h)),
            pl.BlockSpec((seq, cols), lambda b, h, i: (b, groups + h)),
            pl.BlockSpec((seq, cols), lambda b, h, i: (b, 2 * groups + h)),
            pl.BlockSpec((per,) + band.shape[1:], lambda b, h, i: (h, 0, 0, 0)),
            pl.BlockSpec((1, dh2), lambda b, h, i: (0, 0)),
        ],
        out_specs=pl.BlockSpec((tq, cols), lambda b, h, i: (b * nq + i, h)),
        out_shape=jax.ShapeDtypeStruct((n, heads * dh2), BF16),
        scratch_shapes=[pltpu.VMEM((per, 2, tq, seq), F32), pltpu.VMEM((per, seq, 2 * dh2), BF16),
                        pltpu.VMEM((per, dh2, seq), BF16)],
        compiler_params=_params("parallel", "parallel", "arbitrary"),
        name="diff_attention",
    )(rel_table, lam_full.reshape(1), qkv, qkv, qkv, band, subln_g.reshape(1, dh2))


MOE_TILE = 1024
EXPERT_TILE = 1024
EXPERT_CHUNK = 512
_HI_MASK = np.uint32(0xFFFF0000)


def _pack_bf16_pair(lo, hi):
    as_bits = lambda t: lax.bitcast_convert_type(t.astype(BF16).astype(F32), jnp.uint32)
    return (as_bits(hi) & _HI_MASK) | (as_bits(lo) >> 16)


def _unpack_bf16_pair(word):
    lo = lax.bitcast_convert_type(word << 16, F32).astype(BF16)
    hi = lax.bitcast_convert_type(word & _HI_MASK, F32).astype(BF16)
    return lo, hi


def _router_kernel(x_ref, g_ref, sh_ref, sc_ref, rw_ref, rb_ref, hp_ref, idx_ref, w_ref, rank_ref, cnt_ref,
                   run_ref, *, n_experts):
    @pl.when(pl.program_id(0) == 0)
    def _():
        run_ref[...] = jnp.zeros_like(run_ref)

    tm, d = x_ref.shape
    h = _modulated_norm(x_ref[...], g_ref[...], sh_ref[0], sc_ref[0])
    hp_ref[...] = _pack_bf16_pair(h[:, :d // 2], h[:, d // 2:])
    logits = _dot_nt(rw_ref[...], h.astype(BF16)) + rb_ref[...]
    expert = lax.broadcasted_iota(jnp.int32, logits.shape, 0).astype(F32)
    neg = jnp.float32(-jnp.inf)
    picks, vals = [], []
    for _ in range(TOP_K):
        m = jnp.max(logits, axis=0, keepdims=True)
        first = jnp.min(jnp.where(logits == m, expert, float(n_experts)), axis=0, keepdims=True)
        hit = expert == first
        picks.append((first, hit))
        vals.append(m)
        logits = jnp.where(hit, neg, logits)
    es = [jnp.exp(v - vals[0]) for v in vals]
    inv = 1.0 / (es[0] + es[1] + es[2] + es[3])
    onehot = sum(jnp.where(hit, 1.0, 0.0) for _, hit in picks)
    ri = lax.broadcasted_iota(jnp.int32, (tm, tm), 0)
    ci = lax.broadcasted_iota(jnp.int32, (tm, tm), 1)
    earlier = jnp.where(ri < ci, 1.0, 0.0).astype(BF16)
    before = _dot(onehot.astype(BF16), earlier) + run_ref[...]
    row = lax.broadcasted_iota(jnp.int32, idx_ref.shape, 0)
    idx_o = jnp.zeros(idx_ref.shape, F32)
    w_o = jnp.zeros(idx_ref.shape, F32)
    rank_o = jnp.zeros(idx_ref.shape, F32)
    for j, (first, hit) in enumerate(picks):
        rank_j = jnp.sum(jnp.where(hit, before, 0.0), axis=0, keepdims=True)
        idx_o = jnp.where(row == j, first, idx_o)
        w_o = jnp.where(row == j, es[j] * inv, w_o)
        rank_o = jnp.where(row == j, rank_j, rank_o)
    idx_ref[...] = idx_o
    w_ref[...] = w_o
    rank_ref[...] = rank_o
    run_ref[...] += jnp.sum(onehot, axis=1, keepdims=True)
    cnt_ref[...] = jnp.broadcast_to(run_ref[...], cnt_ref.shape)


def _router(x2, g, mod_l, shift_idx, seq, router_w, router_b):
    n, d = x2.shape
    tm = MOE_TILE
    n_experts = router_w.shape[1]
    per_batch = seq // tm
    slots = 8
    slot_out = pl.BlockSpec((slots, tm), lambda i: (0, i))
    slot_shape = jax.ShapeDtypeStruct((slots, n), F32)
    return pl.pallas_call(
        functools.partial(_router_kernel, n_experts=n_experts),
        grid=(n // tm,),
        in_specs=[
            pl.BlockSpec((tm, d), lambda i: (i, 0)),
            pl.BlockSpec((1, d), lambda i: (0, 0)),
            pl.BlockSpec((1, 1, d), lambda i: (i // per_batch, 0, shift_idx)),
            pl.BlockSpec((1, 1, d), lambda i: (i // per_batch, 0, shift_idx + 1)),
            pl.BlockSpec((n_experts, d), lambda i: (0, 0)),
            pl.BlockSpec((n_experts, 1), lambda i: (0, 0)),
        ],
        out_specs=[pl.BlockSpec((tm, d // 2), lambda i: (i, 0)), slot_out, slot_out, slot_out,
                   pl.BlockSpec((n_experts, LANES), lambda i: (0, 0))],
        out_shape=[jax.ShapeDtypeStruct((n, d // 2), jnp.uint32), slot_shape, slot_shape, slot_shape,
                   jax.ShapeDtypeStruct((n_experts, LANES), F32)],
        scratch_shapes=[pltpu.VMEM((n_experts, 1), F32)],
        compiler_params=_params("arbitrary"),
        name="moe_router",
    )(x2, g.reshape(1, d), mod_l, mod_l, router_w.T.astype(BF16), router_b.reshape(n_experts, 1))


GLU_TILE = 2 * LANES


def _experts_kernel(te_ref, nu_ref, slot_ref, nxt_ref, rows_ref, xs_ref, b1_ref, b2_ref, perm_ref, w1_hbm, w2_hbm,
                    ys_ref, w1buf, w2buf, w1p_ref, w2b_ref, sem, *, layer):
    t = pl.program_id(0)
    used = t < nu_ref[0]
    new_expert = (t == 0) | (te_ref[t] != te_ref[jnp.maximum(t - 1, 0)])
    n_col_tiles = w1p_ref.shape[1] // GLU_TILE

    def weight_copies(e, slot):
        return (pltpu.make_async_copy(w1_hbm.at[layer, e], w1buf.at[slot], sem.at[0, slot]),
                pltpu.make_async_copy(w2_hbm.at[layer, e], w2buf.at[slot], sem.at[1, slot]))

    @pl.when(used & new_expert)
    def _():
        slot = slot_ref[t]

        @pl.when(t == 0)
        def _():
            for cp in weight_copies(te_ref[t], slot):
                cp.start()

        for cp in weight_copies(te_ref[t], slot):
            cp.wait()

        @pl.when(nxt_ref[t] >= 0)
        def _():
            for cp in weight_copies(nxt_ref[t], 1 - slot):
                cp.start()

        for c in range(n_col_tiles):
            cols = slice(c * GLU_TILE, (c + 1) * GLU_TILE)
            w1p_ref[:, cols] = _dot(w1buf[slot, :, cols].astype(BF16), perm_ref[...]).astype(BF16)
        w2b_ref[...] = w2buf[slot].astype(BF16)

    def swiglu_rows(n_rows):
        lo, hi = _unpack_bf16_pair(xs_ref[0:n_rows, :])
        x = jnp.concatenate([lo, hi], axis=1)
        out = b2_ref[0]
        for j in range(n_col_tiles * LANES // EXPERT_CHUNK):
            hcols = slice(2 * j * EXPERT_CHUNK, 2 * (j + 1) * EXPERT_CHUNK)
            hh = _dot(x, w1p_ref[:, hcols]) + b1_ref[0, :, hcols]
            acts = []
            for c in range(2 * EXPERT_CHUNK // GLU_TILE):
                x_glu = jnp.minimum(hh[:, c * GLU_TILE:c * GLU_TILE + LANES], SWIGLU_LIMIT)
                x_lin = jnp.clip(hh[:, c * GLU_TILE + LANES:(c + 1) * GLU_TILE], -SWIGLU_LIMIT, SWIGLU_LIMIT)
                acts.append((x_glu * _sigmoid(SWIGLU_ALPHA * x_glu) * (x_lin + 1.0)).astype(BF16))
            out = out + _dot(jnp.concatenate(acts, axis=1), w2b_ref[j * EXPERT_CHUNK:(j + 1) * EXPERT_CHUNK, :])
        half = out.shape[1] // 2
        ys_ref[0:n_rows, :] = _pack_bf16_pair(out[:, :half], out[:, half:])
        if n_rows < ys_ref.shape[0]:
            ys_ref[n_rows:, :] = jnp.zeros((ys_ref.shape[0] - n_rows, ys_ref.shape[1]), ys_ref.dtype)

    tile_rows = xs_ref.shape[0]
    quarter = tile_rows // 4
    for n_q in range(1, 5):
        lo_rows, hi_rows = (n_q - 1) * quarter, n_q * quarter
        fits = (rows_ref[t] > lo_rows) & (rows_ref[t] <= hi_rows) if n_q > 1 else rows_ref[t] <= hi_rows
        pl.when(used & fits)(functools.partial(swiglu_rows, hi_rows))

    @pl.when(t >= nu_ref[0])
    def _():
        ys_ref[...] = jnp.zeros_like(ys_ref)


def _experts(xs, tile_expert, n_used, slot, nxt, tile_rows, w1_all, b1, w2_all, b2, layer):
    n_rows, dw = xs.shape
    tm = EXPERT_TILE
    _, n_exp, d, de2 = w1_all.shape
    src = np.arange(GLU_TILE)
    dst = np.where(src % 2 == 0, src // 2, LANES + src // 2)
    perm = jnp.asarray(dst[:, None] == np.arange(GLU_TILE)[None, :], BF16)
    b1p = b1.reshape(n_exp, de2 // GLU_TILE, LANES, 2).transpose(0, 1, 3, 2).reshape(n_exp, 1, de2)
    row = lambda t, te, nu, sl, nx, nr: (jnp.maximum(jnp.minimum(t, nu[0] - 1), 0), 0)
    exp3 = lambda t, te, nu, sl, nx, nr: (te[t], 0, 0)
    grid_spec = pltpu.PrefetchScalarGridSpec(
        num_scalar_prefetch=5,
        grid=(n_rows // tm,),
        in_specs=[
            pl.BlockSpec((tm, dw), row),
            pl.BlockSpec((1, 1, de2), exp3),
            pl.BlockSpec((1, 1, d), exp3),
            pl.BlockSpec((GLU_TILE, GLU_TILE), lambda t, te, nu, sl, nx, nr: (0, 0)),
            pl.BlockSpec(memory_space=pl.ANY),
            pl.BlockSpec(memory_space=pl.ANY),
        ],
        out_specs=pl.BlockSpec((tm, d // 2), lambda t, te, nu, sl, nx, nr: (t, 0)),
        scratch_shapes=[pltpu.VMEM((2, d, de2), F32), pltpu.VMEM((2, de2 // 2, d), F32),
                        pltpu.VMEM((d, de2), BF16), pltpu.VMEM((de2 // 2, d), BF16),
                        pltpu.SemaphoreType.DMA((2, 2))],
    )
    return pl.pallas_call(
        functools.partial(_experts_kernel, layer=layer),
        grid_spec=grid_spec,
        out_shape=jax.ShapeDtypeStruct((n_rows, d // 2), jnp.uint32),
        compiler_params=_params("arbitrary"),
        name="moe_experts",
    )(tile_expert, n_used, slot, nxt, tile_rows, xs, b1p, b2.reshape(n_exp, 1, d), perm, w1_all, w2_all)


SC_WINDOW = 128


def _sc_worker_rows(n_rows):
    sc = plsc.get_sparse_core_info()
    per_worker = n_rows // (sc.num_cores * sc.num_subcores)
    worker = lax.axis_index("subcore") * sc.num_cores + lax.axis_index("core")
    return worker * per_worker, per_worker


def _sc_mesh():
    return plsc.VectorSubcoreMesh(core_axis_name="core", subcore_axis_name="subcore")


def _sc_gather_rows(table, idx):
    n_idx, width = idx.shape[0], table.shape[1]

    @functools.partial(
        pl.kernel, out_type=jax.ShapeDtypeStruct((n_idx, width), table.dtype), mesh=_sc_mesh(),
        scratch_types=[pltpu.VMEM((SC_WINDOW,), jnp.int32), pltpu.VMEM((SC_WINDOW, width), table.dtype)],
        name="sc_gather_rows")
    def gather(table_hbm, idx_hbm, out_hbm, idx_v, rows_v):
        first, count = _sc_worker_rows(n_idx)

        @pl.loop(0, count // SC_WINDOW)
        def _(c):
            rows = pl.ds(first + c * SC_WINDOW, SC_WINDOW)
            pltpu.sync_copy(idx_hbm.at[rows], idx_v)
            pltpu.sync_copy(table_hbm.at[idx_v], rows_v)
            pltpu.sync_copy(rows_v, out_hbm.at[rows])

    return gather(table, idx)


def _sc_scatter_rows(rows, pos, n_out):
    n, width = rows.shape
    k = pos.shape[0]

    @functools.partial(
        pl.kernel, out_type=jax.ShapeDtypeStruct((n_out, width), rows.dtype), mesh=_sc_mesh(),
        scratch_types=[pltpu.VMEM((k, SC_WINDOW), jnp.int32), pltpu.VMEM((SC_WINDOW, width), rows.dtype)],
        name="sc_scatter_rows")
    def scatter(rows_hbm, pos_hbm, out_hbm, pos_v, rows_v):
        first, count = _sc_worker_rows(n)

        @pl.loop(0, count // SC_WINDOW)
        def _(c):
            src = pl.ds(first + c * SC_WINDOW, SC_WINDOW)
            pltpu.sync_copy(pos_hbm.at[:, src], pos_v)
            pltpu.sync_copy(rows_hbm.at[src], rows_v)
            for j in range(k):
                pltpu.sync_copy(rows_v, out_hbm.at[pos_v.at[j]])

    return scatter(rows, pos)


def _combine_dense_kernel(x_ref, gate_ref, w_ref, fg_ref, *rest, final_norm):
    y_refs, o_ref = rest[:TOP_K], rest[TOP_K]
    w = w_ref[...].T
    y_lo = y_hi = None
    for j in range(TOP_K):
        lo, hi = _unpack_bf16_pair(y_refs[j][...])
        wj = w[:, j:j + 1]
        y_lo = wj * lo if j == 0 else y_lo + wj * lo
        y_hi = wj * hi if j == 0 else y_hi + wj * hi
    out = x_ref[...] + gate_ref[0] * jnp.concatenate([y_lo, y_hi], axis=1)
    if final_norm:
        out = out * lax.rsqrt(jnp.mean(out * out, axis=-1, keepdims=True) + NORM_EPS) * fg_ref[...]
    o_ref[...] = out


def _combine_dense(x2, mod_l, gate_idx, seq, top_w, y4, final_g, final_norm, tm=1024):
    n, d = x2.shape
    per_batch = seq // tm
    blocks = n // tm
    slot_spec = lambda j: pl.BlockSpec((tm, d // 2), lambda i: (j * blocks + i, 0))
    return pl.pallas_call(
        functools.partial(_combine_dense_kernel, final_norm=final_norm),
        grid=(blocks,),
        in_specs=[
            pl.BlockSpec((tm, d), lambda i: (i, 0)),
            pl.BlockSpec((1, 1, d), lambda i: (i // per_batch, 0, gate_idx)),
            pl.BlockSpec((top_w.shape[0], tm), lambda i: (0, i)),
            pl.BlockSpec((1, d), lambda i: (0, 0)),
        ] + [slot_spec(j) for j in range(TOP_K)],
        out_specs=pl.BlockSpec((tm, d), lambda i: (i, 0)),
        out_shape=jax.ShapeDtypeStruct((n, d), F32),
        compiler_params=_params("parallel"),
        name="moe_combine",
    )(x2, mod_l, top_w, final_g.reshape(1, d), *([y4] * TOP_K))


def _moe_ffn(x2, norm_g, mod_l, seq, router_w, router_b, w1_all, b1, w2_all, b2, layer, final_g, final_norm):
    n, d = x2.shape
    n_exp = router_w.shape[1]
    tile = EXPERT_TILE
    hp, idx_f, top_w, rank_f, counts = _router(x2, norm_g, mod_l, 3, seq, router_w, router_b)
    counts = counts[:, 0].astype(jnp.int32)
    padded = (counts + tile - 1) // tile * tile
    ends = jnp.cumsum(padded)
    starts = ends - padded
    idx = idx_f[:TOP_K].astype(jnp.int32)
    expert_ids = jnp.arange(n_exp, dtype=jnp.int32)
    start_of = jnp.sum(jnp.where(idx[..., None] == expert_ids, starts, 0), axis=-1)
    pos_slot_major = start_of + rank_f[:TOP_K].astype(jnp.int32)
    n_rows = n * TOP_K + n_exp * tile
    n_tiles = n_rows // tile
    tile_start = jnp.arange(n_tiles, dtype=jnp.int32) * tile
    tile_expert = jnp.sum((ends[None, :] <= tile_start[:, None]).astype(jnp.int32), axis=1)
    tile_expert = jnp.minimum(tile_expert, n_exp - 1)
    n_used = (ends[-1:] // tile).astype(jnp.int32)
    nonempty = counts > 0
    ordinal = jnp.cumsum(nonempty.astype(jnp.int32)) - 1
    later = nonempty[None, :] & (expert_ids[None, :] > expert_ids[:, None])
    nxt_e = jnp.min(jnp.where(later, expert_ids[None, :], n_exp), axis=1)
    nxt_e = jnp.where(nxt_e == n_exp, -1, nxt_e)
    slot = (ordinal[tile_expert] % 2).astype(jnp.int32)
    nxt = nxt_e[tile_expert].astype(jnp.int32)
    xs = _sc_scatter_rows(hp, pos_slot_major, n_rows)
    filled = jnp.clip((starts + counts)[tile_expert] - tile_start, 0, tile).astype(jnp.int32)
    ys = _experts(xs, tile_expert, n_used, slot, nxt, filled, w1_all, b1, w2_all, b2, layer)
    y4 = _sc_gather_rows(ys, pos_slot_major.reshape(-1))
    return _combine_dense(x2, mod_l, 5, seq, top_w, y4, final_g, final_norm)


def kernel(x, c, ada_w, ada_b, norm_mix_g, norm_ffn_g, router_w, router_b, moe_w1, moe_b1, moe_w2, moe_b2, ab_w_in, ab_w_out, hgrn_lb, hgrn_norm_g, rwkv_mu, rwkv_w0, rwkv_w2, rwkv_a0, rwkv_a2, rwkv_g2, rwkv_k_k, rwkv_k_a, rwkv_r_k, rwkv_ln_g, rwkv_ln_b, attn_w_in, attn_w_out, attn_lambda, attn_subln_g, rel_bias_table, final_norm_g):
    batch, seq, d = x.shape
    n = batch * seq
    depth = ada_w.shape[0]
    x2 = x.reshape(n, d)
    mod = _adaln(c, ada_w, ada_b)
    lb_all = jnp.cumsum(jax.nn.softmax(hgrn_lb.astype(F32), axis=1), axis=1)
    for layer in range(depth):
        mod_l = mod[layer].reshape(batch, 1, 6 * d)
        j = layer // 2
        if layer % 2 == 0:
            a_cols = 5 * (d // 2)
            w_in = ab_w_in[j].astype(BF16)
            p_a, p_b = _normmod_proj(x2, norm_mix_g[layer], mod_l, 0, seq,
                                     [w_in[:, :a_cols], w_in[:, a_cols:]], [F32, F32])
            y_a = _hgrn2(p_a, lb_all[:, j], hgrn_norm_g[j], batch, seq)
            y_b = _rwkv7(p_b, rwkv_mu[j], rwkv_w0[j], rwkv_w2[j], rwkv_a0[j], rwkv_a2[j], rwkv_g2[j],
                         rwkv_k_k[j], rwkv_k_a[j], rwkv_r_k[j], rwkv_ln_g[j], rwkv_ln_b[j], batch, seq)
            w_out = ab_w_out[j].astype(BF16)
            x2 = _outproj_residual(x2, mod_l, 2, seq, [y_a, y_b], [w_out[:d // 2], w_out[d // 2:]])
        else:
            (qkv,) = _normmod_proj(x2, norm_mix_g[layer], mod_l, 0, seq, [attn_w_in[j].astype(BF16)], [BF16])
            lam = attn_lambda[j].astype(F32)
            lam_init = 0.8 - 0.6 * math.exp(-0.3 * layer)
            lam_full = jnp.exp(jnp.sum(lam[0] * lam[1])) - jnp.exp(jnp.sum(lam[2] * lam[3])) + lam_init
            o = _diff_attention(qkv, rel_bias_table, lam_full, attn_subln_g[j], lam_init, batch, seq)
            x2 = _outproj_residual(x2, mod_l, 2, seq, [o], [attn_w_out[j].astype(BF16)])
        x2 = _moe_ffn(x2, norm_ffn_g[layer], mod_l, seq, router_w[layer], router_b[layer], moe_w1,
                      moe_b1[layer], moe_w2, moe_b2[layer], layer, final_norm_g, layer == depth - 1)
    return x2.reshape(batch, seq, d)
```

```python
import functools
import math

import jax
import jax.numpy as jnp
import numpy as np
from jax import lax
from jax.experimental import pallas as pl
from jax.experimental.pallas import tpu as pltpu
from jax.experimental.pallas import tpu_sc as plsc

F32 = jnp.float32
BF16 = jnp.bfloat16
HIGHEST = lax.Precision.HIGHEST

NORM_EPS = 1e-6

A_HEAD_DIM = 128
A_CHUNK = 32
A_GROUP = 512
A_HEADS_PER_STEP = 2
B_HEAD_DIM = 64
B_CHUNK = 64
B_STEPS = 2
B_LN_EPS = 1e-5 * B_HEAD_DIM
C_HEAD_DIM = 64
REL_BUCKETS = 32
REL_MAX_DISTANCE = 128
TOP_K = 4
SWIGLU_LIMIT = 7.0
SWIGLU_ALPHA = 1.702

V7X_VMEM_BYTES = 64 * 1024 * 1024
VMEM_LIMIT = V7X_VMEM_BYTES - 8 * 1024 * 1024
LANES = 128


def _params(*sem):
    return pltpu.CompilerParams(dimension_semantics=sem, vmem_limit_bytes=VMEM_LIMIT)


def _sigmoid(x):
    return 1.0 / (1.0 + jnp.exp(-x))


def _silu(x):
    return x * _sigmoid(x)


def _dot(a, b):
    return jnp.dot(a, b, preferred_element_type=F32)


def _dot_nt(a, b):
    return lax.dot_general(a, b, (((1,), (1,)), ((), ())), preferred_element_type=F32)


def _dot_tn(a, b):
    return lax.dot_general(a, b, (((0,), (0,)), ((), ())), preferred_element_type=F32)


def _adaln_kernel(c_ref, w_ref, b_ref, o_ref):
    cond = _silu(c_ref[...])
    o_ref[0] = jnp.dot(cond, w_ref[0], precision=HIGHEST, preferred_element_type=F32) + b_ref[0]


def _adaln(c, ada_w, ada_b):
    n_layers, d, n_out = ada_w.shape
    batch = c.shape[0]
    tn = 1536
    return pl.pallas_call(
        _adaln_kernel,
        grid=(n_layers, n_out // tn),
        in_specs=[
            pl.BlockSpec((batch, d), lambda l, j: (0, 0)),
            pl.BlockSpec((1, d, tn), lambda l, j: (l, 0, j)),
            pl.BlockSpec((1, 1, tn), lambda l, j: (l, 0, j)),
        ],
        out_specs=pl.BlockSpec((1, batch, tn), lambda l, j: (l, 0, j)),
        out_shape=jax.ShapeDtypeStruct((n_layers, batch, n_out), F32),
        compiler_params=_params("parallel", "parallel"),
        name="adaln",
    )(c, ada_w, ada_b.reshape(n_layers, 1, n_out))


def _modulated_norm(x, g, shift, scale):
    y = x * lax.rsqrt(jnp.mean(x * x, axis=-1, keepdims=True) + NORM_EPS)
    return (y * g) * (1.0 + scale) + shift


def _normmod_proj_kernel(x_ref, g_ref, sh_ref, sc_ref, *rest, n_w):
    w_refs, o_refs = rest[:n_w], rest[n_w:]
    h = _modulated_norm(x_ref[...], g_ref[...], sh_ref[0], sc_ref[0]).astype(BF16)
    for w_ref, o_ref in zip(w_refs, o_refs):
        o_ref[...] = _dot(h, w_ref[...]).astype(o_ref.dtype)


def _normmod_proj(x2, g, mod_l, shift_idx, seq, weights, out_dtypes, tm=512):
    n, d = x2.shape
    per_batch = seq // tm
    in_specs = [
        pl.BlockSpec((tm, d), lambda i: (i, 0)),
        pl.BlockSpec((1, d), lambda i: (0, 0)),
        pl.BlockSpec((1, 1, d), lambda i: (i // per_batch, 0, shift_idx)),
        pl.BlockSpec((1, 1, d), lambda i: (i // per_batch, 0, shift_idx + 1)),
    ]
    in_specs += [pl.BlockSpec(w.shape, lambda i: (0, 0)) for w in weights]
    out_specs = [pl.BlockSpec((tm, w.shape[1]), lambda i: (i, 0)) for w in weights]
    out_shape = [jax.ShapeDtypeStruct((n, w.shape[1]), dt) for w, dt in zip(weights, out_dtypes)]
    return pl.pallas_call(
        functools.partial(_normmod_proj_kernel, n_w=len(weights)),
        grid=(n // tm,),
        in_specs=in_specs,
        out_specs=out_specs,
        out_shape=out_shape,
        compiler_params=_params("parallel"),
        name="normmod_proj",
    )(x2, g.reshape(1, d), mod_l, mod_l, *weights)


def _outproj_kernel(x_ref, gate_ref, *rest, n_y):
    y_refs, w_refs, o_ref = rest[:n_y], rest[n_y:2 * n_y], rest[2 * n_y]
    acc = _dot(y_refs[0][...], w_refs[0][...])
    for y_ref, w_ref in zip(y_refs[1:], w_refs[1:]):
        acc += _dot(y_ref[...], w_ref[...])
    o_ref[...] = x_ref[...] + gate_ref[0] * acc


def _outproj_residual(x2, mod_l, gate_idx, seq, ys, ws, tm=1024):
    n, d = x2.shape
    per_batch = seq // tm
    in_specs = [
        pl.BlockSpec((tm, d), lambda i: (i, 0)),
        pl.BlockSpec((1, 1, d), lambda i: (i // per_batch, 0, gate_idx)),
    ]
    in_specs += [pl.BlockSpec((tm, y.shape[1]), lambda i: (i, 0)) for y in ys]
    in_specs += [pl.BlockSpec(w.shape, lambda i: (0, 0)) for w in ws]
    return pl.pallas_call(
        functools.partial(_outproj_kernel, n_y=len(ys)),
        grid=(n // tm,),
        in_specs=in_specs,
        out_specs=pl.BlockSpec((tm, d), lambda i: (i, 0)),
        out_shape=jax.ShapeDtypeStruct((n, d), F32),
        compiler_params=_params("parallel"),
        name="outproj_residual",
    )(x2, mod_l, *ys, *ws)


def _chunk_cumsum(x, chunk, reverse):
    rows = x.shape[0]
    pos = lax.broadcasted_iota(jnp.int32, x.shape, 0) % chunk
    s = 1
    while s < chunk:
        if reverse:
            x = x + jnp.where(pos < chunk - s, pltpu.roll(x, rows - s, axis=0), 0.0)
        else:
            x = x + jnp.where(pos >= s, pltpu.roll(x, s, axis=0), 0.0)
        s *= 2
    return x


def _hgrn_groups(slabs):
    g_rows, dk = slabs[0][0].shape
    n_chunks = g_rows // A_CHUNK
    ti = lax.broadcasted_iota(jnp.int32, (A_CHUNK, A_CHUNK), 0)
    si = lax.broadcasted_iota(jnp.int32, (A_CHUNK, A_CHUNK), 1)
    chunk_rows = [slice(c * A_CHUNK, (c + 1) * A_CHUNK) for c in range(n_chunks)]
    prep = []
    for q, f, v, lb, st, reverse in slabs:
        fg = lb + (1.0 - lb) * _sigmoid(f)
        k = 1.0 - fg
        b = _chunk_cumsum(jnp.log(fg), A_CHUNK, reverse)
        b3 = b.reshape(n_chunks, A_CHUNK, dk)
        edge = b3[:, 0:1, :] if reverse else b3[:, A_CHUNK - 1:A_CHUNK, :]
        prep.append(dict(
            q_in=(q * jnp.exp(b)).astype(BF16), k_in=(k * jnp.exp(-b)).astype(BF16),
            k_st=(k.reshape(n_chunks, A_CHUNK, dk) * jnp.exp(edge - b3)).astype(BF16),
            decay=jnp.exp(edge), vb=v.astype(BF16), st=st, reverse=reverse,
            mask=(si >= ti) if reverse else (si <= ti)))
    for p in prep:
        p["scores"] = [_dot_nt(p["q_in"][r], p["k_in"][r]) for r in chunk_rows]
        p["dstate"] = [_dot_tn(p["vb"][r], p["k_st"][c]) for c, r in enumerate(chunk_rows)]
    for p in prep:
        p["intra"] = [_dot(jnp.where(p["mask"], s, 0.0).astype(BF16), p["vb"][r])
                      for s, r in zip(p["scores"], chunk_rows)]
    results = []
    for p in prep:
        st = p["st"]
        outs = [None] * n_chunks
        for c in (range(n_chunks - 1, -1, -1) if p["reverse"] else range(n_chunks)):
            outs[c] = p["intra"][c] + _dot_nt(p["q_in"][chunk_rows[c]], st.astype(BF16))
            st = st * p["decay"][c] + p["dstate"][c]
        results.append((jnp.concatenate(outs, axis=0), st))
    return results


def _hgrn_kernel(q_ref, ff_ref, fb_ref, i_ref, g_ref, lb_ref, ng_ref, o_ref, of_ref, ob_ref):
    seq = q_ref.shape[0]
    dk = A_HEAD_DIM
    n_heads = q_ref.shape[1] // dk
    n_groups = seq // A_GROUP
    head_cols = [slice(h * dk, (h + 1) * dk) for h in range(n_heads)]

    def body(j, states):
        rf = pl.ds(pl.multiple_of(j * A_GROUP, A_GROUP), A_GROUP)
        rb = pl.ds(pl.multiple_of((n_groups - 1 - j) * A_GROUP, A_GROUP), A_GROUP)
        slabs = []
        for h, cols in enumerate(head_cols):
            slabs.append((_silu(q_ref[rf, cols]), ff_ref[rf, cols], i_ref[rf, cols], lb_ref[0:1, cols],
                          states[2 * h], False))
            slabs.append((_silu(q_ref[rb, cols]), fb_ref[rb, cols], i_ref[rb, cols], lb_ref[1:2, cols],
                          states[2 * h + 1], True))
        results = _hgrn_groups(slabs)
        for h, cols in enumerate(head_cols):
            of_ref[rf, cols] = results[2 * h][0]
            ob_ref[rb, cols] = results[2 * h + 1][0]
        return tuple(r[1] for r in results)

    zero = jnp.zeros((dk, dk), F32)
    lax.fori_loop(0, n_groups, body, (zero,) * (2 * n_heads))

    def finish(j, carry):
        r = pl.ds(pl.multiple_of(j * A_GROUP, A_GROUP), A_GROUP)
        for cols in head_cols:
            o = of_ref[r, cols] + ob_ref[r, cols]
            o = o * lax.rsqrt(jnp.mean(o * o, axis=-1, keepdims=True) + NORM_EPS) * ng_ref[...]
            o_ref[r, cols] = (o * _silu(g_ref[r, cols])).astype(o_ref.dtype)
        return carry

    lax.fori_loop(0, n_groups, finish, 0)


def _hgrn2(p_a, lb, norm_g, batch, seq):
    n = p_a.shape[0]
    width = p_a.shape[1] // 5
    cols = A_HEADS_PER_STEP * A_HEAD_DIM
    steps = width // cols
    sect = lambda s: pl.BlockSpec((seq, cols), lambda b, h, s=s: (b, s * steps + h))
    return pl.pallas_call(
        _hgrn_kernel,
        grid=(batch, steps),
        in_specs=[sect(0), sect(1), sect(2), sect(3), sect(4),
                  pl.BlockSpec((2, cols), lambda b, h: (0, h)),
                  pl.BlockSpec((1, A_HEAD_DIM), lambda b, h: (0, 0))],
        out_specs=pl.BlockSpec((seq, cols), lambda b, h: (b, h)),
        out_shape=jax.ShapeDtypeStruct((n, width), BF16),
        scratch_shapes=[pltpu.VMEM((seq, cols), F32), pltpu.VMEM((seq, cols), F32)],
        compiler_params=_params("parallel", "parallel"),
        name="hgrn2",
    )(p_a, p_a, p_a, p_a, p_a, lb, norm_g.reshape(1, A_HEAD_DIM))


def _softplus(z):
    return jnp.maximum(z, 0.0) + jnp.log(1.0 + jnp.exp(-jnp.abs(z)))


def _rwkv_prep_kernel(p_ref, prev_ref, next_ref, mu_ref, w0_ref, w2_ref, a0_ref, a2_ref, g2_ref,
                      kk_ref, ka_ref, rk_ref, ones_ref, tri_ref,
                      r_out, k_out, v_out, kkn_out, b_out, cumf_out, cumb_out, g_out, bonus_out):
    i = pl.program_id(1)
    last = pl.num_programs(1) - 1
    ts = p_ref.shape[0]
    width = a0_ref.shape[1]
    heads = width // B_HEAD_DIM
    p = p_ref[...]
    prev_row = jnp.where(i == 0, 0.0, prev_ref[7:8, :])
    next_row = jnp.where(i == last, 0.0, next_ref[0:1, :])
    row = lax.broadcasted_iota(jnp.int32, (ts, 1), 0)
    up = jnp.where(row == 0, prev_row, pltpu.roll(p, 1, axis=0))
    dn = jnp.where(row == ts - 1, next_row, pltpu.roll(p, ts - 1, axis=0))
    p = p + mu_ref[...] * (0.5 * (up + dn) - p)

    r = p[:, 0:width]
    k = p[:, width:2 * width]
    v = p[:, 2 * width:3 * width]
    o = 3 * width
    wlo_f = p[:, o:o + 64]
    wlo_b = p[:, o + 64:o + 128]
    alo = p[:, o + 128:o + 192]
    glo = p[:, o + 192:o + 320]

    def log_decay(wlo, d):
        z = w0_ref[d:d + 1, :] + _dot(jnp.tanh(wlo).astype(BF16), w2_ref[d])
        w = -_softplus(-z) - 0.5
        return -jnp.exp(w)

    def split_dot(sel, t, right):
        hi = t.astype(BF16)
        lo = (t - hi.astype(F32)).astype(BF16)
        return (_dot(hi, sel) + _dot(lo, sel)) if right else (_dot(sel, hi) + _dot(sel, lo))

    cum_f = split_dot(tri_ref[0], log_decay(wlo_f, 0), False)
    cum_b = split_dot(tri_ref[1], log_decay(wlo_b, 1), False)
    a = _sigmoid(a0_ref[...] + _dot(alo.astype(BF16), a2_ref[...]))
    g = _dot(_sigmoid(glo).astype(BF16), g2_ref[...])
    kk = k * kk_ref[...]
    head_sum = lambda t: split_dot(ones_ref[...], t, True)

    kk_n = kk / jnp.maximum(jnp.sqrt(head_sum(kk * kk)), 1e-12)
    k_mod = k * (1.0 + (a - 1.0) * ka_ref[...])
    bonus = head_sum(r * k_mod * rk_ref[...]) * v
    g_out[...] = g
    bonus_out[...] = bonus
    for h in range(heads):
        sl = slice(h * B_HEAD_DIM, (h + 1) * B_HEAD_DIM)
        r_out[0, h] = r[:, sl]
        k_out[0, h] = k_mod[:, sl]
        v_out[0, h] = v[:, sl]
        kkn_out[0, h] = kk_n[:, sl]
        b_out[0, h] = (kk_n * a)[:, sl]
        cumf_out[0, h] = cum_f[:, sl]
        cumb_out[0, h] = cum_b[:, sl]


def _rwkv_prep(p_b, mu, w0, w2, a0, a2, g2, k_k, k_a, r_k, batch, seq, ts=512):
    n, cols = p_b.shape
    width = a0.shape[0]
    heads = width // B_HEAD_DIM
    nblk = seq // ts
    rows8 = ts // 8
    head_id = np.arange(width) // B_HEAD_DIM
    ones = jnp.asarray(head_id[:, None] == head_id[None, :], BF16)
    full = lambda a: pl.BlockSpec(a.shape, lambda b, i: (0,) * a.ndim)
    vec = lambda a: a.reshape(1, -1)
    row = np.arange(ts)
    same_chunk = (row[:, None] // B_CHUNK) == (row[None, :] // B_CHUNK)
    tri = jnp.asarray(np.stack([same_chunk & (row[None, :] <= row[:, None]),
                                same_chunk & (row[None, :] >= row[:, None])]), BF16)
    args = [vec(mu), w0, w2.astype(BF16), vec(a0), a2.astype(BF16), g2.astype(BF16),
            vec(k_k), vec(k_a), vec(r_k), ones, tri]
    hm = pl.BlockSpec((1, heads, ts, B_HEAD_DIM), lambda b, i: (b, 0, i, 0))
    tokm = pl.BlockSpec((ts, width), lambda b, i: (b * nblk + i, 0))
    hm_shape = jax.ShapeDtypeStruct((batch, heads, seq, B_HEAD_DIM), F32)
    tok_shape = jax.ShapeDtypeStruct((n, width), F32)
    return pl.pallas_call(
        _rwkv_prep_kernel,
        grid=(batch, nblk),
        in_specs=[
            pl.BlockSpec((ts, cols), lambda b, i: (b * nblk + i, 0)),
            pl.BlockSpec((8, cols), lambda b, i: (jnp.maximum((b * nblk + i) * rows8 - 1, 0), 0)),
            pl.BlockSpec((8, cols), lambda b, i: (jnp.minimum((b * nblk + i + 1) * rows8, n // 8 - 1), 0)),
        ] + [full(a) for a in args],
        out_specs=[hm] * 7 + [tokm] * 2,
        out_shape=[hm_shape] * 7 + [tok_shape] * 2,
        compiler_params=_params("parallel", "parallel"),
        name="rwkv_prep",
    )(p_b, p_b, p_b, *args)


def _rwkv_chunks(chains):
    c, d = chains[0][0].shape
    ti = lax.broadcasted_iota(jnp.int32, (c, c), 0)
    si = lax.broadcasted_iota(jnp.int32, (c, c), 1)
    eye = jnp.where(ti == si, 1.0, 0.0)
    masks = {rev: (((si > ti), (si >= ti)) if rev else ((si < ti), (si <= ti))) for rev in (False, True)}
    t2 = lax.broadcasted_iota(jnp.int32, (c, 2 * c), 0)
    s2 = lax.broadcasted_iota(jnp.int32, (c, 2 * c), 1)
    s2 = jnp.where(s2 >= c, s2 - c, s2)
    incl2 = {False: s2 <= t2, True: s2 >= t2}

    row = lax.broadcasted_iota(jnp.int32, (c, d), 0)
    st = []
    for r, k, v, kk, b, cum, rev in chains:
        before = (jnp.where(row == c - 1, 0.0, pltpu.roll(cum, c - 1, axis=0)) if rev
                  else jnp.where(row == 0, 0.0, pltpu.roll(cum, 1, axis=0)))
        grow = jnp.exp(-cum)
        kt = (kk * jnp.exp(before)).astype(BF16)
        rt = r * jnp.exp(cum)
        b_ = (b * grow).astype(BF16)
        kb = jnp.concatenate([(k * grow).astype(BF16), b_], axis=0)
        st.append(dict(vb=v.astype(BF16), kt=kt, rt=rt, b_=b_, kb=kb,
                       decay=jnp.exp(cum[0:1, :] if rev else cum[c - 1:c, :]), rev=rev,
                       lhs=jnp.concatenate([kt, rt.astype(BF16)], axis=0)))
    for s in st:
        s["big"] = _dot_nt(s["lhs"], s["kb"])
    for s in st:
        strict, incl = masks[s["rev"]]
        big = s["big"]
        a_ab = jnp.where(strict, big[:c, c:], 0.0)
        s["a_ak"] = jnp.where(strict, big[:c, :c], 0.0).astype(BF16)
        s["p"] = jnp.where(incl2[s["rev"]], big[c:], 0.0).astype(BF16)
        s["m"] = eye - a_ab
        s["ab"] = a_ab.astype(BF16)
    for s in st:
        s["aj"] = _dot(s["ab"], s["ab"])
        s["av"] = _dot(s["a_ak"], s["vb"])
    span = 2
    while span < c:
        span *= 2
        for s in st:
            ajb = s["aj"].astype(BF16)
            if span < c:
                both = _dot(jnp.concatenate([ajb, s["m"].astype(BF16)], axis=0), ajb)
                s["aj"], s["m"] = both[:c], s["m"] + both[c:]
            else:
                s["m"] = s["m"] + _dot(s["m"].astype(BF16), ajb)
    for s in st:
        wu = _dot(s["m"].astype(BF16), jnp.concatenate([s["kt"], s["av"].astype(BF16)], axis=1))
        s["nwu"] = -wu.astype(BF16)
    for s in st:
        nw, nu0, vb = s["nwu"][:, :d], s["nwu"][:, d:], s["vb"]
        rhs = jnp.concatenate([jnp.concatenate([vb, jnp.zeros_like(vb)], axis=1),
                               jnp.concatenate([nu0, nw], axis=1)], axis=0)
        s["yq"] = _dot(s["p"], rhs)
        s["h_t"] = _dot_tn(jnp.concatenate([vb, nu0], axis=0), s["kb"])
        s["nbw"] = _dot_tn(s["b_"], nw)
    return [dict(q=(s["rt"] + s["yq"][:, d:]).astype(BF16), y1=s["yq"][:, :d], nbw=s["nbw"].astype(BF16),
                 h_t=s["h_t"], decay=s["decay"]) for s in st]


def _rwkv_advance(states, steps):
    ttbs = [tt.astype(BF16) for tt in states]
    ys = [_dot_nt(s["q"], ttb) + s["y1"] for s, ttb in zip(steps, ttbs)]
    new = [(tt + _dot_nt(ttb, s["nbw"]) + s["h_t"]) * s["decay"] for tt, ttb, s in zip(states, ttbs, steps)]
    return ys, new


def _rwkv_scan_kernel(rf, kf, vf, kkf, bf, lwf, rb, kb, vb, kkb, bb, lwb, yf_ref, yb_ref, tf_ref, tb_ref):
    heads, ts = rf.shape[1], rf.shape[2]
    n_chunks = ts // B_CHUNK

    @pl.when(pl.program_id(1) == 0)
    def _():
        tf_ref[...] = jnp.zeros_like(tf_ref)
        tb_ref[...] = jnp.zeros_like(tb_ref)

    def body(j, carry):
        rows_f = [pl.ds(pl.multiple_of((j * B_STEPS + s) * B_CHUNK, B_CHUNK), B_CHUNK) for s in range(B_STEPS)]
        rows_b = [pl.ds(pl.multiple_of((n_chunks - 1 - j * B_STEPS - s) * B_CHUNK, B_CHUNK), B_CHUNK)
                  for s in range(B_STEPS)]
        chains = []
        for s in range(B_STEPS):
            for h in range(heads):
                chains.append([ref[0, h, rows_f[s], :] for ref in (rf, kf, vf, kkf, bf, lwf)] + [False])
                chains.append([ref[0, h, rows_b[s], :] for ref in (rb, kb, vb, kkb, bb, lwb)] + [True])
        solved = _rwkv_chunks(chains)
        states = []
        for h in range(heads):
            states += [tf_ref[h], tb_ref[h]]
        for s in range(B_STEPS):
            ys, states = _rwkv_advance(states, solved[s * 2 * heads:(s + 1) * 2 * heads])
            for h in range(heads):
                yf_ref[0, h, rows_f[s], :] = ys[2 * h]
                yb_ref[0, h, rows_b[s], :] = ys[2 * h + 1]
        for h in range(heads):
            tf_ref[h], tb_ref[h] = states[2 * h], states[2 * h + 1]
        return carry

    lax.fori_loop(0, n_chunks // B_STEPS, body, 0)


def _rwkv_scan(r, k, v, kkn, b, lw_f, lw_b, ts=256):
    batch, heads, seq, d = r.shape
    nblk = seq // ts
    fwd = pl.BlockSpec((1, heads, ts, d), lambda bi, i: (bi, 0, i, 0))
    bwd = pl.BlockSpec((1, heads, ts, d), lambda bi, i: (bi, 0, nblk - 1 - i, 0))
    shape = jax.ShapeDtypeStruct(r.shape, F32)
    return pl.pallas_call(
        _rwkv_scan_kernel,
        grid=(batch, nblk),
        in_specs=[fwd] * 6 + [bwd] * 6,
        out_specs=[fwd, bwd],
        out_shape=[shape, shape],
        scratch_shapes=[pltpu.VMEM((heads, d, d), F32), pltpu.VMEM((heads, d, d), F32)],
        compiler_params=_params("parallel", "arbitrary"),
        name="rwkv_scan",
    )(r, k, v, kkn, b, lw_f, r, k, v, kkn, b, lw_b)


def _rwkv_post_kernel(yf_ref, yb_ref, g_ref, bonus_ref, lng_ref, lnb_ref, o_ref):
    heads = yf_ref.shape[1]
    outs = []
    for h in range(heads):
        y = yf_ref[0, h] + yb_ref[0, h]
        mean = jnp.mean(y, axis=-1, keepdims=True)
        var = jnp.mean(jnp.square(y - mean), axis=-1, keepdims=True)
        outs.append((y - mean) * lax.rsqrt(var + B_LN_EPS))
    yn = jnp.concatenate(outs, axis=1) * lng_ref[...] + lnb_ref[...]
    o_ref[...] = ((yn + bonus_ref[...]) * g_ref[...]).astype(o_ref.dtype)


def _rwkv_post(y_f, y_b, g, bonus, ln_g, ln_b, ts=1024):
    batch, heads, seq, d = y_f.shape
    n, width = g.shape
    nblk = seq // ts
    hm = pl.BlockSpec((1, heads, ts, d), lambda b, i: (b, 0, i, 0))
    tokm = pl.BlockSpec((ts, width), lambda b, i: (b * nblk + i, 0))
    vec = pl.BlockSpec((1, width), lambda b, i: (0, 0))
    return pl.pallas_call(
        _rwkv_post_kernel,
        grid=(batch, nblk),
        in_specs=[hm, hm, tokm, tokm, vec, vec],
        out_specs=tokm,
        out_shape=jax.ShapeDtypeStruct((n, width), BF16),
        compiler_params=_params("parallel", "parallel"),
        name="rwkv_post",
    )(y_f, y_b, g, bonus, ln_g.reshape(1, width), ln_b.reshape(1, width))


def _rwkv7(p_b, mu, w0, w2, a0, a2, g2, k_k, k_a, r_k, ln_g, ln_b, batch, seq):
    r, k, v, kkn, b, lw_f, lw_b, g, bonus = _rwkv_prep(p_b, mu, w0, w2, a0, a2, g2, k_k, k_a, r_k, batch, seq)
    y_f, y_b = _rwkv_scan(r, k, v, kkn, b, lw_f, lw_b)
    return _rwkv_post(y_f, y_b, g, bonus, ln_g, ln_b)


_BUCKET_EDGES = (0, 1, 2, 3, 4, 5, 6, 7, 8, 12, 16, 23, 32, 46, 64, 91)
ATTN_TK = 128
C_HEADS_PER_STEP = 4


def _bias_band_kernel(table_ref, o_ref):
    h = pl.program_id(0)
    n_d, tq, tk = o_ref.shape[1:]
    half = REL_BUCKETS // 2
    r = lax.broadcasted_iota(jnp.int32, (tq, tk), 0)
    c = lax.broadcasted_iota(jnp.int32, (tq, tk), 1)
    for d in range(n_d):
        rel = tk * (d - 1) + c - r
        n = jnp.abs(rel)
        vals = []
        for sign in range(2):
            val = jnp.full((tq, tk), table_ref[sign * half + half - 1, h], F32)
            for bkt in range(half - 2, -1, -1):
                val = jnp.where(n < _BUCKET_EDGES[bkt + 1], table_ref[sign * half + bkt, h], val)
            vals.append(val)
        o_ref[0, d] = jnp.where(rel > 0, vals[1], vals[0])


def _bias_band(rel_table, tq):
    heads = rel_table.shape[1]
    n_d = tq // ATTN_TK + 2
    return pl.pallas_call(
        _bias_band_kernel,
        grid=(heads,),
        in_specs=[pl.BlockSpec(memory_space=pltpu.SMEM)],
        out_specs=pl.BlockSpec((1, n_d, tq, ATTN_TK), lambda h: (h, 0, 0, 0)),
        out_shape=jax.ShapeDtypeStruct((heads, n_d, tq, ATTN_TK), F32),
        compiler_params=_params("parallel"),
        name="bias_band",
    )(rel_table)


def _diff_attn_kernel(table_ref, lam_ref, q_ref, k_ref, v_ref, band_ref, g_ref, o_ref, lg_ref, vext_ref, kt_ref,
                      *, lam_init):
    i = pl.program_id(2)
    tq = q_ref.shape[0]
    dh2 = 2 * C_HEAD_DIM
    n_heads = q_ref.shape[1] // dh2
    seq = k_ref.shape[0]
    n_d = band_ref.shape[1]
    n_kt = seq // ATTN_TK
    ratio = tq // ATTN_TK
    half = REL_BUCKETS // 2
    head_cols = [slice(hh * dh2, (hh + 1) * dh2) for hh in range(n_heads)]

    @pl.when(i == 0)
    def _():
        for hh, cols in enumerate(head_cols):
            kt_ref[hh] = k_ref[:, cols].astype(F32).T.astype(BF16)
            v = v_ref[:, cols]
            vext_ref[hh] = jnp.concatenate([v, jnp.ones_like(v)], axis=1)

    lane = lax.broadcasted_iota(jnp.int32, (1, dh2), 1)
    first_band = ratio * i - 1
    col = lax.broadcasted_iota(jnp.int32, (1, seq), 1)
    for hh, cols in enumerate(head_cols):
        h = pl.program_id(1) * n_heads + hh
        q = q_ref[:, cols] * jnp.asarray(C_HEAD_DIM ** -0.5, q_ref.dtype)
        zero = jnp.zeros_like(q)
        far_row = jnp.where(col < first_band * ATTN_TK, table_ref[half - 1, h],
                            jnp.where(col >= (first_band + n_d) * ATTN_TK, table_ref[REL_BUCKETS - 1, h],
                                      -jnp.inf))
        row_max = []
        for m in range(2):
            qm = jnp.where((lane >= m * C_HEAD_DIM) & (lane < (m + 1) * C_HEAD_DIM), q, zero)
            far = _dot(qm, kt_ref[hh]) + far_row
            mx = jnp.max(far, axis=-1, keepdims=True)
            lg_ref[hh, m] = far
            for d in (0, n_d - 1) + tuple(range(1, n_d - 1)):
                kt = first_band + d
                valid = (kt >= 0) & (kt < n_kt)
                c0 = pl.multiple_of(jnp.clip(kt, 0, n_kt - 1) * ATTN_TK, ATTN_TK)
                near = _dot(qm, kt_ref[hh, :, pl.ds(c0, ATTN_TK)]) + band_ref[hh, d]
                mx = jnp.maximum(mx, jnp.where(valid, jnp.max(near, axis=-1, keepdims=True), -jnp.inf))
                lg_ref[hh, m, :, pl.ds(c0, ATTN_TK)] = near
            row_max.append(mx)
        parts = []
        for m in range(2):
            e = jnp.exp(lg_ref[hh, m] - row_max[m]).astype(BF16)
            pv = _dot(e, vext_ref[hh])
            parts.append(pv[:, :dh2] * (1.0 / pv[:, dh2:dh2 + 1]))
        o = parts[0] - lam_ref[0] * parts[1]
        o = o * lax.rsqrt(jnp.mean(o * o, axis=-1, keepdims=True) + NORM_EPS) * g_ref[...]
        o_ref[:, cols] = (o * (1.0 - lam_init)).astype(o_ref.dtype)


def _diff_attention(qkv, rel_table, lam_full, subln_g, lam_init, batch, seq, tq=256):
    n = qkv.shape[0]
    dh2 = 2 * C_HEAD_DIM
    heads = qkv.shape[1] // (3 * dh2)
    nq = seq // tq
    per = C_HEADS_PER_STEP
    groups = heads // per
    cols = per * dh2
    band = _bias_band(rel_table, tq)
    return pl.pallas_call(
        functools.partial(_diff_attn_kernel, lam_init=lam_init),
        grid=(batch, groups, nq),
        in_specs=[
            pl.BlockSpec(memory_space=pltpu.SMEM),
            pl.BlockSpec(memory_space=pltpu.SMEM),
            pl.BlockSpec((tq, cols), lambda b, h, i: (b * nq + i, h)),
            pl.BlockSpec((seq, cols), lambda b, h, i: (b, groups + h)),
            pl.BlockSpec((seq, cols), lambda b, h, i: (b, 2 * groups + h)),
            pl.BlockSpec((per,) + band.shape[1:], lambda b, h, i: (h, 0, 0, 0)),
            pl.BlockSpec((1, dh2), lambda b, h, i: (0, 0)),
        ],
        out_specs=pl.BlockSpec((tq, cols), lambda b, h, i: (b * nq + i, h)),
        out_shape=jax.ShapeDtypeStruct((n, heads * dh2), BF16),
        scratch_shapes=[pltpu.VMEM((per, 2, tq, seq), F32), pltpu.VMEM((per, seq, 2 * dh2), BF16),
                        pltpu.VMEM((per, dh2, seq), BF16)],
        compiler_params=_params("parallel", "parallel", "arbitrary"),
        name="diff_attention",
    )(rel_table, lam_full.reshape(1), qkv, qkv, qkv, band, subln_g.reshape(1, dh2))


MOE_TILE = 1024
EXPERT_TILE = 1024
EXPERT_PART = 128
EXPERT_CHUNK = 512
_HI_MASK = np.uint32(0xFFFF0000)


def _pack_bf16_pair(lo, hi):
    as_bits = lambda t: lax.bitcast_convert_type(t.astype(BF16).astype(F32), jnp.uint32)
    return (as_bits(hi) & _HI_MASK) | (as_bits(lo) >> 16)


def _unpack_bf16_pair(word):
    lo = lax.bitcast_convert_type(word << 16, F32).astype(BF16)
    hi = lax.bitcast_convert_type(word & _HI_MASK, F32).astype(BF16)
    return lo, hi


def _router_kernel(x_ref, g_ref, sh_ref, sc_ref, rw_ref, rb_ref, hp_ref, idx_ref, w_ref, rank_ref, cnt_ref,
                   run_ref, *, n_experts):
    @pl.when(pl.program_id(0) == 0)
    def _():
        run_ref[...] = jnp.zeros_like(run_ref)

    tm, d = x_ref.shape
    h = _modulated_norm(x_ref[...], g_ref[...], sh_ref[0], sc_ref[0])
    hp_ref[...] = _pack_bf16_pair(h[:, :d // 2], h[:, d // 2:])
    logits = _dot_nt(rw_ref[...], h.astype(BF16)) + rb_ref[...]
    expert = lax.broadcasted_iota(jnp.int32, logits.shape, 0).astype(F32)
    neg = jnp.float32(-jnp.inf)
    picks, vals = [], []
    for _ in range(TOP_K):
        m = jnp.max(logits, axis=0, keepdims=True)
        first = jnp.min(jnp.where(logits == m, expert, float(n_experts)), axis=0, keepdims=True)
        hit = expert == first
        picks.append((first, hit))
        vals.append(m)
        logits = jnp.where(hit, neg, logits)
    es = [jnp.exp(v - vals[0]) for v in vals]
    inv = 1.0 / (es[0] + es[1] + es[2] + es[3])
    onehot = sum(jnp.where(hit, 1.0, 0.0) for _, hit in picks)
    ri = lax.broadcasted_iota(jnp.int32, (tm, tm), 0)
    ci = lax.broadcasted_iota(jnp.int32, (tm, tm), 1)
    earlier = jnp.where(ri < ci, 1.0, 0.0).astype(BF16)
    before = _dot(onehot.astype(BF16), earlier) + run_ref[...]
    row = lax.broadcasted_iota(jnp.int32, idx_ref.shape, 0)
    idx_o = jnp.zeros(idx_ref.shape, F32)
    w_o = jnp.zeros(idx_ref.shape, F32)
    rank_o = jnp.zeros(idx_ref.shape, F32)
    for j, (first, hit) in enumerate(picks):
        rank_j = jnp.sum(jnp.where(hit, before, 0.0), axis=0, keepdims=True)
        idx_o = jnp.where(row == j, first, idx_o)
        w_o = jnp.where(row == j, es[j] * inv, w_o)
        rank_o = jnp.where(row == j, rank_j, rank_o)
    idx_ref[...] = idx_o
    w_ref[...] = w_o
    rank_ref[...] = rank_o
    run_ref[...] += jnp.sum(onehot, axis=1, keepdims=True)
    cnt_ref[...] = jnp.broadcast_to(run_ref[...], cnt_ref.shape)


def _router(x2, g, mod_l, shift_idx, seq, router_w, router_b):
    n, d = x2.shape
    tm = MOE_TILE
    n_experts = router_w.shape[1]
    per_batch = seq // tm
    slots = 8
    slot_out = pl.BlockSpec((slots, tm), lambda i: (0, i))
    slot_shape = jax.ShapeDtypeStruct((slots, n), F32)
    return pl.pallas_call(
        functools.partial(_router_kernel, n_experts=n_experts),
        grid=(n // tm,),
        in_specs=[
            pl.BlockSpec((tm, d), lambda i: (i, 0)),
            pl.BlockSpec((1, d), lambda i: (0, 0)),
            pl.BlockSpec((1, 1, d), lambda i: (i // per_batch, 0, shift_idx)),
            pl.BlockSpec((1, 1, d), lambda i: (i // per_batch, 0, shift_idx + 1)),
            pl.BlockSpec((n_experts, d), lambda i: (0, 0)),
            pl.BlockSpec((n_experts, 1), lambda i: (0, 0)),
        ],
        out_specs=[pl.BlockSpec((tm, d // 2), lambda i: (i, 0)), slot_out, slot_out, slot_out,
                   pl.BlockSpec((n_experts, LANES), lambda i: (0, 0))],
        out_shape=[jax.ShapeDtypeStruct((n, d // 2), jnp.uint32), slot_shape, slot_shape, slot_shape,
                   jax.ShapeDtypeStruct((n_experts, LANES), F32)],
        scratch_shapes=[pltpu.VMEM((n_experts, 1), F32)],
        compiler_params=_params("arbitrary"),
        name="moe_router",
    )(x2, g.reshape(1, d), mod_l, mod_l, router_w.T.astype(BF16), router_b.reshape(n_experts, 1))


GLU_TILE = 2 * LANES


def _experts_kernel(te_ref, nu_ref, slot_ref, nxt_ref, rows_ref, xs_ref, b1_ref, b2_ref, perm_ref, w1_hbm, w2_hbm,
                    ys_ref, w1buf, w2buf, w1p_ref, w2b_ref, sem, *, layer):
    t = pl.program_id(0)
    used = t < nu_ref[0]
    new_expert = (t == 0) | (te_ref[t] != te_ref[jnp.maximum(t - 1, 0)])
    n_col_tiles = w1p_ref.shape[1] // GLU_TILE

    def weight_copies(e, slot):
        return (pltpu.make_async_copy(w1_hbm.at[layer, e], w1buf.at[slot], sem.at[0, slot]),
                pltpu.make_async_copy(w2_hbm.at[layer, e], w2buf.at[slot], sem.at[1, slot]))

    @pl.when(used & new_expert)
    def _():
        slot = slot_ref[t]

        @pl.when(t == 0)
        def _():
            for cp in weight_copies(te_ref[t], slot):
                cp.start()

        for cp in weight_copies(te_ref[t], slot):
            cp.wait()

        @pl.when(nxt_ref[t] >= 0)
        def _():
            for cp in weight_copies(nxt_ref[t], 1 - slot):
                cp.start()

        for c in range(n_col_tiles):
            cols = slice(c * GLU_TILE, (c + 1) * GLU_TILE)
            w1p_ref[:, cols] = _dot(w1buf[slot, :, cols].astype(BF16), perm_ref[...]).astype(BF16)
        w2b_ref[...] = w2buf[slot].astype(BF16)

    def swiglu_rows(n_rows):
        lo, hi = _unpack_bf16_pair(xs_ref[0:n_rows, :])
        x = jnp.concatenate([lo, hi], axis=1)
        out = b2_ref[0]
        for j in range(n_col_tiles * LANES // EXPERT_CHUNK):
            hcols = slice(2 * j * EXPERT_CHUNK, 2 * (j + 1) * EXPERT_CHUNK)
            hh = _dot(x, w1p_ref[:, hcols]) + b1_ref[0, :, hcols]
            acts = []
            for c in range(2 * EXPERT_CHUNK // GLU_TILE):
                x_glu = jnp.minimum(hh[:, c * GLU_TILE:c * GLU_TILE + LANES], SWIGLU_LIMIT)
                x_lin = jnp.clip(hh[:, c * GLU_TILE + LANES:(c + 1) * GLU_TILE], -SWIGLU_LIMIT, SWIGLU_LIMIT)
                acts.append((x_glu * _sigmoid(SWIGLU_ALPHA * x_glu) * (x_lin + 1.0)).astype(BF16))
            out = out + _dot(jnp.concatenate(acts, axis=1), w2b_ref[j * EXPERT_CHUNK:(j + 1) * EXPERT_CHUNK, :])
        half = out.shape[1] // 2
        ys_ref[0:n_rows, :] = _pack_bf16_pair(out[:, :half], out[:, half:])
        if n_rows < ys_ref.shape[0]:
            ys_ref[n_rows:, :] = jnp.zeros((ys_ref.shape[0] - n_rows, ys_ref.shape[1]), ys_ref.dtype)

    tile_rows = xs_ref.shape[0]
    for n_p in range(1, tile_rows // EXPERT_PART + 1):
        lo_rows, hi_rows = (n_p - 1) * EXPERT_PART, n_p * EXPERT_PART
        fits = (rows_ref[t] > lo_rows) & (rows_ref[t] <= hi_rows) if n_p > 1 else rows_ref[t] <= hi_rows
        pl.when(used & fits)(functools.partial(swiglu_rows, hi_rows))

    @pl.when(t >= nu_ref[0])
    def _():
        ys_ref[...] = jnp.zeros_like(ys_ref)


def _experts(xs, tile_expert, n_used, slot, nxt, tile_rows, w1_all, b1, w2_all, b2, layer):
    n_rows, dw = xs.shape
    tm = EXPERT_TILE
    _, n_exp, d, de2 = w1_all.shape
    src = np.arange(GLU_TILE)
    dst = np.where(src % 2 == 0, src // 2, LANES + src // 2)
    perm = jnp.asarray(dst[:, None] == np.arange(GLU_TILE)[None, :], BF16)
    b1p = b1.reshape(n_exp, de2 // GLU_TILE, LANES, 2).transpose(0, 1, 3, 2).reshape(n_exp, 1, de2)
    row = lambda t, te, nu, sl, nx, nr: (jnp.maximum(jnp.minimum(t, nu[0] - 1), 0), 0)
    exp3 = lambda t, te, nu, sl, nx, nr: (te[t], 0, 0)
    grid_spec = pltpu.PrefetchScalarGridSpec(
        num_scalar_prefetch=5,
        grid=(n_rows // tm,),
        in_specs=[
            pl.BlockSpec((tm, dw), row),
            pl.BlockSpec((1, 1, de2), exp3),
            pl.BlockSpec((1, 1, d), exp3),
            pl.BlockSpec((GLU_TILE, GLU_TILE), lambda t, te, nu, sl, nx, nr: (0, 0)),
            pl.BlockSpec(memory_space=pl.ANY),
            pl.BlockSpec(memory_space=pl.ANY),
        ],
        out_specs=pl.BlockSpec((tm, d // 2), lambda t, te, nu, sl, nx, nr: (t, 0)),
        scratch_shapes=[pltpu.VMEM((2, d, de2), F32), pltpu.VMEM((2, de2 // 2, d), F32),
                        pltpu.VMEM((d, de2), BF16), pltpu.VMEM((de2 // 2, d), BF16),
                        pltpu.SemaphoreType.DMA((2, 2))],
    )
    return pl.pallas_call(
        functools.partial(_experts_kernel, layer=layer),
        grid_spec=grid_spec,
        out_shape=jax.ShapeDtypeStruct((n_rows, d // 2), jnp.uint32),
        compiler_params=_params("arbitrary"),
        name="moe_experts",
    )(tile_expert, n_used, slot, nxt, tile_rows, xs, b1p, b2.reshape(n_exp, 1, d), perm, w1_all, w2_all)


SC_WINDOW = 128


def _sc_worker_rows(n_rows):
    sc = plsc.get_sparse_core_info()
    per_worker = n_rows // (sc.num_cores * sc.num_subcores)
    worker = lax.axis_index("subcore") * sc.num_cores + lax.axis_index("core")
    return worker * per_worker, per_worker


def _sc_mesh():
    return plsc.VectorSubcoreMesh(core_axis_name="core", subcore_axis_name="subcore")


def _sc_gather_rows(table, idx):
    n_idx, width = idx.shape[0], table.shape[1]

    @functools.partial(
        pl.kernel, out_type=jax.ShapeDtypeStruct((n_idx, width), table.dtype), mesh=_sc_mesh(),
        scratch_types=[pltpu.VMEM((SC_WINDOW,), jnp.int32), pltpu.VMEM((SC_WINDOW, width), table.dtype)],
        name="sc_gather_rows")
    def gather(table_hbm, idx_hbm, out_hbm, idx_v, rows_v):
        first, count = _sc_worker_rows(n_idx)

        @pl.loop(0, count // SC_WINDOW)
        def _(c):
            rows = pl.ds(first + c * SC_WINDOW, SC_WINDOW)
            pltpu.sync_copy(idx_hbm.at[rows], idx_v)
            pltpu.sync_copy(table_hbm.at[idx_v], rows_v)
            pltpu.sync_copy(rows_v, out_hbm.at[rows])

    return gather(table, idx)


def _sc_scatter_rows(rows, pos, n_out):
    n, width = rows.shape
    k = pos.shape[0]

    @functools.partial(
        pl.kernel, out_type=jax.ShapeDtypeStruct((n_out, width), rows.dtype), mesh=_sc_mesh(),
        scratch_types=[pltpu.VMEM((k, SC_WINDOW), jnp.int32), pltpu.VMEM((SC_WINDOW, width), rows.dtype)],
        name="sc_scatter_rows")
    def scatter(rows_hbm, pos_hbm, out_hbm, pos_v, rows_v):
        first, count = _sc_worker_rows(n)

        @pl.loop(0, count // SC_WINDOW)
        def _(c):
            src = pl.ds(first + c * SC_WINDOW, SC_WINDOW)
            pltpu.sync_copy(pos_hbm.at[:, src], pos_v)
            pltpu.sync_copy(rows_hbm.at[src], rows_v)
            for j in range(k):
                pltpu.sync_copy(rows_v, out_hbm.at[pos_v.at[j]])

    return scatter(rows, pos)


def _combine_dense_kernel(x_ref, gate_ref, w_ref, fg_ref, *rest, final_norm):
    y_refs, o_ref = rest[:TOP_K], rest[TOP_K]
    w = w_ref[...].T
    y_lo = y_hi = None
    for j in range(TOP_K):
        lo, hi = _unpack_bf16_pair(y_refs[j][...])
        wj = w[:, j:j + 1]
        y_lo = wj * lo if j == 0 else y_lo + wj * lo
        y_hi = wj * hi if j == 0 else y_hi + wj * hi
    out = x_ref[...] + gate_ref[0] * jnp.concatenate([y_lo, y_hi], axis=1)
    if final_norm:
        out = out * lax.rsqrt(jnp.mean(out * out, axis=-1, keepdims=True) + NORM_EPS) * fg_ref[...]
    o_ref[...] = out


def _combine_dense(x2, mod_l, gate_idx, seq, top_w, y4, final_g, final_norm, tm=1024):
    n, d = x2.shape
    per_batch = seq // tm
    blocks = n // tm
    slot_spec = lambda j: pl.BlockSpec((tm, d // 2), lambda i: (j * blocks + i, 0))
    return pl.pallas_call(
        functools.partial(_combine_dense_kernel, final_norm=final_norm),
        grid=(blocks,),
        in_specs=[
            pl.BlockSpec((tm, d), lambda i: (i, 0)),
            pl.BlockSpec((1, 1, d), lambda i: (i // per_batch, 0, gate_idx)),
            pl.BlockSpec((top_w.shape[0], tm), lambda i: (0, i)),
            pl.BlockSpec((1, d), lambda i: (0, 0)),
        ] + [slot_spec(j) for j in range(TOP_K)],
        out_specs=pl.BlockSpec((tm, d), lambda i: (i, 0)),
        out_shape=jax.ShapeDtypeStruct((n, d), F32),
        compiler_params=_params("parallel"),
        name="moe_combine",
    )(x2, mod_l, top_w, final_g.reshape(1, d), *([y4] * TOP_K))


def _moe_ffn(x2, norm_g, mod_l, seq, router_w, router_b, w1_all, b1, w2_all, b2, layer, final_g, final_norm):
    n, d = x2.shape
    n_exp = router_w.shape[1]
    tile = EXPERT_TILE
    hp, idx_f, top_w, rank_f, counts = _router(x2, norm_g, mod_l, 3, seq, router_w, router_b)
    counts = counts[:, 0].astype(jnp.int32)
    padded = (counts + tile - 1) // tile * tile
    ends = jnp.cumsum(padded)
    starts = ends - padded
    idx = idx_f[:TOP_K].astype(jnp.int32)
    expert_ids = jnp.arange(n_exp, dtype=jnp.int32)
    start_of = jnp.sum(jnp.where(idx[..., None] == expert_ids, starts, 0), axis=-1)
    pos_slot_major = start_of + rank_f[:TOP_K].astype(jnp.int32)
    n_rows = n * TOP_K + n_exp * tile
    n_tiles = n_rows // tile
    tile_start = jnp.arange(n_tiles, dtype=jnp.int32) * tile
    tile_expert = jnp.sum((ends[None, :] <= tile_start[:, None]).astype(jnp.int32), axis=1)
    tile_expert = jnp.minimum(tile_expert, n_exp - 1)
    n_used = (ends[-1:] // tile).astype(jnp.int32)
    nonempty = counts > 0
    ordinal = jnp.cumsum(nonempty.astype(jnp.int32)) - 1
    later = nonempty[None, :] & (expert_ids[None, :] > expert_ids[:, None])
    nxt_e = jnp.min(jnp.where(later, expert_ids[None, :], n_exp), axis=1)
    nxt_e = jnp.where(nxt_e == n_exp, -1, nxt_e)
    slot = (ordinal[tile_expert] % 2).astype(jnp.int32)
    nxt = nxt_e[tile_expert].astype(jnp.int32)
    xs = _sc_scatter_rows(hp, pos_slot_major, n_rows)
    filled = jnp.clip((starts + counts)[tile_expert] - tile_start, 0, tile).astype(jnp.int32)
    ys = _experts(xs, tile_expert, n_used, slot, nxt, filled, w1_all, b1, w2_all, b2, layer)
    y4 = _sc_gather_rows(ys, pos_slot_major.reshape(-1))
    return _combine_dense(x2, mod_l, 5, seq, top_w, y4, final_g, final_norm)


def kernel(x, c, ada_w, ada_b, norm_mix_g, norm_ffn_g, router_w, router_b, moe_w1, moe_b1, moe_w2, moe_b2, ab_w_in, ab_w_out, hgrn_lb, hgrn_norm_g, rwkv_mu, rwkv_w0, rwkv_w2, rwkv_a0, rwkv_a2, rwkv_g2, rwkv_k_k, rwkv_k_a, rwkv_r_k, rwkv_ln_g, rwkv_ln_b, attn_w_in, attn_w_out, attn_lambda, attn_subln_g, rel_bias_table, final_norm_g):
    batch, seq, d = x.shape
    n = batch * seq
    depth = ada_w.shape[0]
    x2 = x.reshape(n, d)
    mod = _adaln(c, ada_w, ada_b)
    lb_all = jnp.cumsum(jax.nn.softmax(hgrn_lb.astype(F32), axis=1), axis=1)
    for layer in range(depth):
        mod_l = mod[layer].reshape(batch, 1, 6 * d)
        j = layer // 2
        if layer % 2 == 0:
            a_cols = 5 * (d // 2)
            w_in = ab_w_in[j].astype(BF16)
            p_a, p_b = _normmod_proj(x2, norm_mix_g[layer], mod_l, 0, seq,
                                     [w_in[:, :a_cols], w_in[:, a_cols:]], [F32, F32])
            y_a = _hgrn2(p_a, lb_all[:, j], hgrn_norm_g[j], batch, seq)
            y_b = _rwkv7(p_b, rwkv_mu[j], rwkv_w0[j], rwkv_w2[j], rwkv_a0[j], rwkv_a2[j], rwkv_g2[j],
                         rwkv_k_k[j], rwkv_k_a[j], rwkv_r_k[j], rwkv_ln_g[j], rwkv_ln_b[j], batch, seq)
            w_out = ab_w_out[j].astype(BF16)
            x2 = _outproj_residual(x2, mod_l, 2, seq, [y_a, y_b], [w_out[:d // 2], w_out[d // 2:]])
        else:
            (qkv,) = _normmod_proj(x2, norm_mix_g[layer], mod_l, 0, seq, [attn_w_in[j].astype(BF16)], [BF16])
            lam = attn_lambda[j].astype(F32)
            lam_init = 0.8 - 0.6 * math.exp(-0.3 * layer)
            lam_full = jnp.exp(jnp.sum(lam[0] * lam[1])) - jnp.exp(jnp.sum(lam[2] * lam[3])) + lam_init
            o = _diff_attention(qkv, rel_bias_table, lam_full, attn_subln_g[j], lam_init, batch, seq)
            x2 = _outproj_residual(x2, mod_l, 2, seq, [o], [attn_w_out[j].astype(BF16)])
        x2 = _moe_ffn(x2, norm_ffn_g[layer], mod_l, seq, router_w[layer], router_b[layer], moe_w1,
                      moe_b1[layer], moe_w2, moe_b2[layer], layer, final_norm_g, layer == depth - 1)
    return x2.reshape(batch, seq, d)
```
